```python
import math
import jax
import jax.numpy as jnp
from jax import lax
import numpy as np

D_MODEL = 1024
BATCH = 8
SEQ = 2048
DEPTH = 1
DEC_BATCH = 128
DEC_SEQ = 1
PAST_LEN = 16384
PAGE_SIZE = 128

D_MIX = 2 * D_MODEL
D_SSD = D_MIX // 2
SSD_HEAD_DIM = 64
SSD_HEADS = D_SSD // SSD_HEAD_DIM
SSD_GROUPS = 2
SSD_STATE = 128
SSD_CHUNK = 128
CONV_W = 4
CONV_DIM = D_SSD + 2 * SSD_GROUPS * SSD_STATE
D_S5 = D_MIX - D_SSD
S5_CH = 16
S5_GROUPS = D_S5 // S5_CH
S5_STATE = 64
D_IN_PROJ = D_SSD + CONV_DIM + SSD_HEADS + D_S5
N_MEM = 256
X_HEADS = 4
X_HEAD_DIM = D_MODEL // X_HEADS
D_FF = -(-8 * D_MODEL // (3 * 256)) * 256
EPS = 1e-6

kernel_name = 'hymba_ssd_s5_xattn_step'


def _rmsnorm(x, g):
    xf = x.astype(jnp.float32)
    y = xf * lax.rsqrt(jnp.mean(xf * xf, axis=-1, keepdims=True) + EPS)
    return (y * g.astype(jnp.float32)).astype(x.dtype)


def _causal_dwconv(x, conv_state, w, b):
    xp = jnp.concatenate([conv_state.astype(x.dtype), x], axis=1)
    y = lax.conv_general_dilated(xp, w[:, None, :].astype(x.dtype), window_strides=(1,),
                                 padding='VALID', dimension_numbers=('NWC', 'WIO', 'NWC'),
                                 feature_group_count=x.shape[-1])
    return y + b.astype(x.dtype), xp[:, -(CONV_W - 1):]


def _ssd_scan(x, dt, a, bm, cm, s0):
    f32 = jnp.float32
    bsz, l = x.shape[0], x.shape[1]
    q = min(SSD_CHUNK, l)
    pad = (-l) % q
    nc = (l + pad) // q
    g_, e_ = SSD_GROUPS, SSD_HEADS // SSD_GROUPS

    def padt(t):
        return jnp.pad(t.astype(f32), [(0, 0), (0, pad)] + [(0, 0)] * (t.ndim - 2))

    xc = padt(x).reshape(bsz, nc, q, g_, e_, SSD_HEAD_DIM)
    dtc = padt(dt).reshape(bsz, nc, q, g_, e_)
    bc = padt(bm).reshape(bsz, nc, q, g_, SSD_STATE)
    cc = padt(cm).reshape(bsz, nc, q, g_, SSD_STATE)
    acum = jnp.cumsum(dtc * a.astype(f32).reshape(g_, e_), axis=2)
    dtx = dtc[..., None] * xc
    at = jnp.moveaxis(acum, 2, -1)
    seg = at[..., :, None] - at[..., None, :]
    causal = jnp.tril(jnp.ones((q, q), dtype=bool))
    lmat = jnp.exp(jnp.where(causal, seg, -jnp.inf))
    cb = jnp.einsum('bcign,bcjgn->bcgij', cc, bc)
    y_diag = jnp.einsum('bcgij,bcgeij,bcjgep->bcigep', cb, lmat, dtx)
    decay_end = jnp.exp(acum[:, :, -1:] - acum)
    chunk_states = jnp.einsum('bcjgn,bcjge,bcjgep->bcgepn', bc, decay_end, dtx)
    chunk_decay = jnp.exp(acum[:, :, -1])

    def step(s, inp):
        cd, cs = inp
        return cd[..., None, None] * s + cs, s

    s_init = s0.astype(f32).reshape(bsz, g_, e_, SSD_HEAD_DIM, SSD_STATE)
    s_fin, prev = lax.scan(step, s_init, (jnp.moveaxis(chunk_decay, 1, 0),
                                          jnp.moveaxis(chunk_states, 1, 0)))
    prev = jnp.moveaxis(prev, 0, 1)
    y_off = jnp.einsum('bcign,bcgepn,bcige->bcigep', cc, prev, jnp.exp(acum))
    y = (y_diag + y_off).reshape(bsz, nc * q, SSD_HEADS, SSD_HEAD_DIM)[:, :l]
    return y, s_fin.reshape(bsz, SSD_HEADS, SSD_HEAD_DIM, SSD_STATE)


def _s5_combine(e1, e2):
    a1r, a1i, b1r, b1i = e1
    a2r, a2i, b2r, b2i = e2
    return (a1r * a2r - a1i * a2i, a1r * a2i + a1i * a2r,
            a2r * b1r - a2i * b1i + b2r, a2r * b1i + a2i * b1r + b2i)


def _s5_scan(u, s0_re, s0_im, lam_re, lam_im, log_step, b_re, b_im, c_re, c_im, d_s5):
    f32 = jnp.float32
    bsz, l = u.shape[0], u.shape[1]
    uf = u.astype(f32)
    ug = uf.reshape(bsz, l, S5_GROUPS, S5_CH)
    lr = lam_re.astype(f32)
    li = lam_im.astype(f32)
    step = jnp.exp(log_step.astype(f32))[:, None]
    mag = jnp.exp(lr * step)
    ang = li * step
    lb_re = mag * jnp.cos(ang)
    lb_im = mag * jnp.sin(ang)
    den = lr * lr + li * li
    k_re = ((lb_re - 1.0) * lr + lb_im * li) / den
    k_im = (lb_im * lr - (lb_re - 1.0) * li) / den
    br = b_re.astype(f32)
    bi = b_im.astype(f32)
    bb_re = k_re[..., None] * br - k_im[..., None] * bi
    bb_im = k_re[..., None] * bi + k_im[..., None] * br
    bu_re = jnp.einsum('gpc,blgc->blgp', bb_re, ug)
    bu_im = jnp.einsum('gpc,blgc->blgp', bb_im, ug)
    s0r = s0_re.astype(f32)
    s0i = s0_im.astype(f32)
    bu_re = bu_re.at[:, 0].add(lb_re * s0r - lb_im * s0i)
    bu_im = bu_im.at[:, 0].add(lb_re * s0i + lb_im * s0r)
    ar = jnp.broadcast_to(lb_re, bu_re.shape)
    ai = jnp.broadcast_to(lb_im, bu_re.shape)
    _, _, s_re, s_im = lax.associative_scan(_s5_combine, (ar, ai, bu_re, bu_im), axis=1)
    y = (jnp.einsum('gcp,blgp->blgc', c_re.astype(f32), s_re)
         - jnp.einsum('gcp,blgp->blgc', c_im.astype(f32), s_im))
    y = y.reshape(bsz, l, D_S5) + d_s5.astype(f32) * uf
    return y.astype(u.dtype), s_re[:, -1], s_im[:, -1]


def _cross_attend(h, mem_k, mem_v, w_xq, w_xo):
    bsz, l = h.shape[0], h.shape[1]
    q = (h @ w_xq).reshape(bsz, l, X_HEADS, X_HEAD_DIM)
    s = jnp.einsum('blhd,bmhd->bhlm', q, mem_k.astype(h.dtype)).astype(jnp.float32)
    p = jax.nn.softmax(s * (X_HEAD_DIM ** -0.5), axis=-1).astype(h.dtype)
    o = jnp.einsum('bhlm,bmhd->blhd', p, mem_v.astype(h.dtype)).reshape(bsz, l, D_MODEL)
    return o @ w_xo


def _block(x, mem_k, mem_v, conv_state, ssm_state, s5_re, s5_im,
           ln1_g, w_in, conv_w, conv_b, dt_bias, a_log, d_ssd, gn_g,
           lam_re, lam_im, log_step, b_re, b_im, c_re, c_im, d_s5, w_glu, b_glu,
           w_out, ln2_g, w_xq, w_xo, ln3_g, w_gate, w_up, w_down):
    f32 = jnp.float32
    bsz, l = x.shape[0], x.shape[1]
    h = _rmsnorm(x, ln1_g)
    proj = h @ w_in
    o1 = D_SSD
    o2 = o1 + CONV_DIM
    o3 = o2 + SSD_HEADS
    z = proj[..., :o1]
    xbc = proj[..., o1:o2]
    dt_raw = proj[..., o2:o3]
    u = proj[..., o3:]
    xbc, new_conv = _causal_dwconv(xbc, conv_state, conv_w, conv_b)
    xbc = jax.nn.silu(xbc)
    nb = SSD_GROUPS * SSD_STATE
    xs = xbc[..., :D_SSD].reshape(bsz, l, SSD_HEADS, SSD_HEAD_DIM)
    bm = xbc[..., D_SSD:D_SSD + nb].reshape(bsz, l, SSD_GROUPS, SSD_STATE)
    cm = xbc[..., D_SSD + nb:].reshape(bsz, l, SSD_GROUPS, SSD_STATE)
    dt = jax.nn.softplus(dt_raw.astype(f32) + dt_bias.astype(f32))
    a = -jnp.exp(a_log.astype(f32))
    y, new_ssm = _ssd_scan(xs, dt, a, bm, cm, ssm_state)
    y = y + d_ssd.astype(f32)[:, None] * xs.astype(f32)
    y = y.reshape(bsz, l, D_SSD) * jax.nn.silu(z.astype(f32))
    y_ssd = _rmsnorm(y, gn_g).astype(x.dtype)
    ys5, new_re, new_im = _s5_scan(u, s5_re, s5_im, lam_re, lam_im, log_step,
                                   b_re, b_im, c_re, c_im, d_s5)
    g = jax.nn.gelu(ys5)
    y_s5 = g * jax.nn.sigmoid(g @ w_glu + b_glu)
    x = x + jnp.concatenate([y_ssd, y_s5], axis=-1) @ w_out
    x = x + _cross_attend(_rmsnorm(x, ln2_g), mem_k, mem_v, w_xq, w_xo)
    hf = _rmsnorm(x, ln3_g)
    x = x + (jax.nn.silu(hf @ w_gate) * (hf @ w_up)) @ w_down
    return (x, new_conv, new_ssm.astype(ssm_state.dtype),
            new_re.astype(s5_re.dtype), new_im.astype(s5_im.dtype))


def setup_inputs(seed: int = 0) -> dict:
    key = jax.random.key(seed)
    ks = iter(jax.random.split(key, 64))
    f32 = jnp.float32

    def nrm(shape, scale=1.0):
        return jax.random.normal(next(ks), shape, f32) * scale

    def gain(shape):
        return 1.0 + 0.01 * jax.random.normal(next(ks), shape, f32)

    def log_uniform(shape, lo, hi):
        return jax.random.uniform(next(ks), shape, f32, math.log(lo), math.log(hi))

    dt0 = jnp.exp(log_uniform((DEPTH, SSD_HEADS), 1e-3, 1e-1))
    lam_im0 = math.pi * jnp.arange(S5_STATE, dtype=f32)
    inp = {}
    inp['x_prompt'] = nrm((BATCH, SEQ, D_MODEL))
    inp['x_sample'] = nrm((DEC_BATCH, DEC_SEQ, D_MODEL))
    inp['mem_prompt'] = nrm((BATCH, N_MEM, D_MODEL))
    inp['state_conv'] = nrm((DEPTH, DEC_BATCH, CONV_W - 1, CONV_DIM), 0.5)
    inp['state_ssm'] = nrm((DEPTH, DEC_BATCH, SSD_HEADS, SSD_HEAD_DIM, SSD_STATE), 0.1)
    inp['state_s5_re'] = nrm((DEPTH, DEC_BATCH, S5_GROUPS, S5_STATE), 0.5)
    inp['state_s5_im'] = nrm((DEPTH, DEC_BATCH, S5_GROUPS, S5_STATE), 0.5)
    inp['cache_mem_k'] = nrm((DEPTH, DEC_BATCH, N_MEM, X_HEADS, X_HEAD_DIM))
    inp['cache_mem_v'] = nrm((DEPTH, DEC_BATCH, N_MEM, X_HEADS, X_HEAD_DIM))
    inp['ln1_g'] = gain((DEPTH, D_MODEL))
    inp['w_in'] = nrm((DEPTH, D_MODEL, D_IN_PROJ), D_MODEL ** -0.5)
    inp['conv_w'] = nrm((DEPTH, CONV_W, CONV_DIM), CONV_W ** -0.5)
    inp['conv_b'] = nrm((DEPTH, CONV_DIM), 0.01)
    inp['dt_bias'] = dt0 + jnp.log(-jnp.expm1(-dt0))
    inp['a_log'] = jnp.log(jax.random.uniform(next(ks), (DEPTH, SSD_HEADS), f32, 1.0, 16.0))
    inp['d_ssd'] = gain((DEPTH, SSD_HEADS))
    inp['gn_g'] = gain((DEPTH, D_SSD))
    inp['lam_re'] = -0.5 + nrm((DEPTH, S5_GROUPS, S5_STATE), 0.01)
    inp['lam_im'] = lam_im0 + nrm((DEPTH, S5_GROUPS, S5_STATE), 0.01)
    inp['log_step'] = log_uniform((DEPTH, S5_GROUPS), 1e-3, 1e-1)
    inp['b_re'] = nrm((DEPTH, S5_GROUPS, S5_STATE, S5_CH), (2 * S5_CH) ** -0.5)
    inp['b_im'] = nrm((DEPTH, S5_GROUPS, S5_STATE, S5_CH), (2 * S5_CH) ** -0.5)
    inp['c_re'] = nrm((DEPTH, S5_GROUPS, S5_CH, S5_STATE), (2 * S5_STATE) ** -0.5)
    inp['c_im'] = nrm((DEPTH, S5_GROUPS, S5_CH, S5_STATE), (2 * S5_STATE) ** -0.5)
    inp['d_s5'] = nrm((DEPTH, D_S5))
    inp['w_glu'] = nrm((DEPTH, D_S5, D_S5), D_S5 ** -0.5)
    inp['b_glu'] = nrm((DEPTH, D_S5), 0.01)
    inp['w_out'] = nrm((DEPTH, D_MIX, D_MODEL), D_MIX ** -0.5)
    inp['ln2_g'] = gain((DEPTH, D_MODEL))
    inp['w_xq'] = nrm((DEPTH, D_MODEL, D_MODEL), D_MODEL ** -0.5)
    inp['w_xk'] = nrm((DEPTH, D_MODEL, D_MODEL), D_MODEL ** -0.5)
    inp['w_xv'] = nrm((DEPTH, D_MODEL, D_MODEL), D_MODEL ** -0.5)
    inp['w_xo'] = nrm((DEPTH, D_MODEL, D_MODEL), D_MODEL ** -0.5)
    inp['ln3_g'] = gain((DEPTH, D_MODEL))
    inp['w_gate'] = nrm((DEPTH, D_MODEL, D_FF), D_MODEL ** -0.5)
    inp['w_up'] = nrm((DEPTH, D_MODEL, D_FF), D_MODEL ** -0.5)
    inp['w_down'] = nrm((DEPTH, D_FF, D_MODEL), D_FF ** -0.5)
    inp['final_g'] = gain((D_MODEL,))
    return inp


def reference(x_prompt, x_sample, mem_prompt, state_conv, state_ssm, state_s5_re, state_s5_im,
              cache_mem_k, cache_mem_v, ln1_g, w_in, conv_w, conv_b, dt_bias, a_log, d_ssd, gn_g,
              lam_re, lam_im, log_step, b_re, b_im, c_re, c_im, d_s5, w_glu, b_glu, w_out,
              ln2_g, w_xq, w_xk, w_xv, w_xo, ln3_g, w_gate, w_up, w_down, final_g):
    bp = x_prompt.shape[0]
    dt_ = x_prompt.dtype
    hp = x_prompt
    hs = x_sample
    conv_p, ssm_p, re_p, im_p, mk_p, mv_p = [], [], [], [], [], []
    conv_s, ssm_s, re_s, im_s = [], [], [], []
    for i in range(DEPTH):
        lw = (ln1_g[i], w_in[i], conv_w[i], conv_b[i], dt_bias[i], a_log[i], d_ssd[i], gn_g[i],
              lam_re[i], lam_im[i], log_step[i], b_re[i], b_im[i], c_re[i], c_im[i], d_s5[i],
              w_glu[i], b_glu[i], w_out[i], ln2_g[i], w_xq[i], w_xo[i], ln3_g[i],
              w_gate[i], w_up[i], w_down[i])
        mk = (mem_prompt @ w_xk[i]).reshape(bp, N_MEM, X_HEADS, X_HEAD_DIM)
        mv = (mem_prompt @ w_xv[i]).reshape(bp, N_MEM, X_HEADS, X_HEAD_DIM)
        z_conv = jnp.zeros((bp, CONV_W - 1, CONV_DIM), dt_)
        z_ssm = jnp.zeros((bp, SSD_HEADS, SSD_HEAD_DIM, SSD_STATE), dt_)
        z_s5 = jnp.zeros((bp, S5_GROUPS, S5_STATE), dt_)
        hp, c1, s1, r1, m1 = _block(hp, mk, mv, z_conv, z_ssm, z_s5, z_s5, *lw)
        conv_p.append(c1)
        ssm_p.append(s1)
        re_p.append(r1)
        im_p.append(m1)
        mk_p.append(mk)
        mv_p.append(mv)
        hs, c2, s2, r2, m2 = _block(hs, cache_mem_k[i], cache_mem_v[i], state_conv[i], state_ssm[i],
                                    state_s5_re[i], state_s5_im[i], *lw)
        conv_s.append(c2)
        ssm_s.append(s2)
        re_s.append(r2)
        im_s.append(m2)
    y_prompt = _rmsnorm(hp, final_g)
    y_sample = _rmsnorm(hs, final_g)
    return (y_prompt, y_sample,
            jnp.stack(conv_p), jnp.stack(ssm_p), jnp.stack(re_p), jnp.stack(im_p),
            jnp.stack(mk_p), jnp.stack(mv_p),
            jnp.stack(conv_s), jnp.stack(ssm_s), jnp.stack(re_s), jnp.stack(im_s))
```

```python
import functools
import math

import jax
import jax.numpy as jnp
from jax import lax
from jax.experimental import pallas as pl
from jax.experimental.pallas import tpu as pltpu

F32 = jnp.float32
BF16 = jnp.bfloat16
EPS = 1e-6

LANES = 128
VMEM_LIMIT = 56 * 1024 * 1024

SSD_HEAD_DIM = 64
SSD_STATE = 128
SSD_GROUPS = 2
SSD_CHUNK = 128
CONV_W = 4
S5_CH = 16
S5_STATE = 64
S5_Q = 16
X_HEADS = 4
TOKEN_TILE = 512


def _const_spec(shape):
    nd = len(shape)
    return pl.BlockSpec(shape, lambda *_: (0,) * nd, pipeline_mode=pl.Buffered(1))


def _params(*sem):
    return pltpu.CompilerParams(dimension_semantics=sem, vmem_limit_bytes=VMEM_LIMIT)


def _rms(x, g):
    return x * lax.rsqrt(jnp.mean(x * x, axis=-1, keepdims=True) + EPS) * g


def _sigmoid(x):
    return 1.0 / (1.0 + jnp.exp(-x))


def _silu(x):
    return x * _sigmoid(x)


def _gelu_tanh(x):
    return 0.5 * x * (1.0 + jnp.tanh(math.sqrt(2.0 / math.pi) * (x + 0.044715 * (x * x * x))))


def _softplus(x):
    return jnp.maximum(x, 0.0) + jnp.log1p(jnp.exp(-jnp.abs(x)))


def _dot(a, b):
    return jnp.dot(a, b, preferred_element_type=F32)


def _dot_nt(a, b):
    return lax.dot_general(a, b, (((1,), (1,)), ((), ())), preferred_element_type=F32)


def _dot_tn(a, b):
    return lax.dot_general(a, b, (((0,), (0,)), ((), ())), preferred_element_type=F32)


def _inproj_body(x_ref, g_ref, wz_ref, wx_ref, wd_ref, z_ref, xbc_ref, dt_ref):
    h = _rms(x_ref[...], g_ref[...]).astype(BF16)
    z_ref[...] = _dot(h, wz_ref[...])
    xbc_ref[...] = _dot(h, wx_ref[...])
    dt_ref[...] = _dot(h, wd_ref[...])


def _inproj(x2d, ln_g, wz, wx, wd):
    t, d = x2d.shape
    tm = min(TOKEN_TILE, t)
    nz, nx, nd = wz.shape[1], wx.shape[1], wd.shape[1]
    row = lambda n: pl.BlockSpec((tm, n), lambda i: (i, 0))
    return pl.pallas_call(
        _inproj_body,
        grid=(t // tm,),
        in_specs=[row(d), _const_spec((1, d)), _const_spec(wz.shape), _const_spec(wx.shape), _const_spec(wd.shape)],
        out_specs=[row(nz), row(nx), row(nd)],
        out_shape=[jax.ShapeDtypeStruct((t, nz), F32), jax.ShapeDtypeStruct((t, nx), F32),
                   jax.ShapeDtypeStruct((t, nd), F32)],
        compiler_params=_params("arbitrary"),
        name="inproj",
    )(x2d, ln_g, wz, wx, wd)


def _expand_heads(v, n_heads):
    rows = v.shape[0]
    lane = lax.broadcasted_iota(jnp.int32, (rows, LANES), 1)
    pieces = [jnp.where(lane < SSD_HEAD_DIM, v[:, 2 * j:2 * j + 1], v[:, 2 * j + 1:2 * j + 2])
              for j in range(n_heads // 2)]
    return jnp.concatenate(pieces, axis=1)


def _ssd_chunk_body(z_ref, xbc_ref, dt_ref, cw_ref, cb_ref, dtb_ref, alog_ref, dexp_ref, gn_ref,
                    y_ref, conv_ref, ssm_ref, xpad, state):
    q = SSD_CHUNK
    d_ssd = z_ref.shape[-1]
    n_heads = d_ssd // SSD_HEAD_DIM
    hpg = n_heads // SSD_GROUPS
    gw = hpg * SSD_HEAD_DIM
    c = pl.program_id(1)

    @pl.when(c == 0)
    def _():
        xpad[0:8, :] = jnp.zeros((8, xpad.shape[1]), F32)
        state[...] = jnp.zeros(state.shape, F32)

    x = xbc_ref[...]
    xpad[8:8 + q, :] = x
    cw = cw_ref[...]
    conv = (cw[3:4] * x + cw[2:3] * xpad[7:7 + q, :] + cw[1:2] * xpad[6:6 + q, :]
            + cw[0:1] * xpad[5:5 + q, :] + cb_ref[...])
    xpad[0:8, :] = xpad[q:q + 8, :]
    conv_ref[...] = x[q - (CONV_W - 1):q, :]
    xact = _silu(conv)
    xs = xact[:, :d_ssd]
    bm = xact[:, d_ssd:d_ssd + SSD_GROUPS * SSD_STATE].astype(BF16)
    cm = xact[:, d_ssd + SSD_GROUPS * SSD_STATE:].astype(BF16)

    dt = _softplus(dt_ref[...] + dtb_ref[...])
    a = -jnp.exp(alog_ref[...])
    row = lax.broadcasted_iota(jnp.int32, (q, q), 0)
    col = lax.broadcasted_iota(jnp.int32, (q, q), 1)
    causal = row >= col
    tri = jnp.where(causal, 1.0, 0.0).astype(F32)
    acum = jnp.dot(tri, dt * a, precision=lax.Precision.HIGHEST, preferred_element_type=F32)
    acum_t = acum.T
    last = acum[q - 1:q, :]
    dt_e = _expand_heads(dt, n_heads)
    ea_e = _expand_heads(jnp.exp(acum), n_heads)
    dend_e = _expand_heads(jnp.exp(last - acum), n_heads)
    cdec_t = jnp.exp(acum_t[:, q - 1:q])

    dtx = xs * dt_e
    dtx_b = dtx.astype(BF16)
    xdec_b = (dtx * dend_e).astype(BF16)

    y_parts = []
    for g in range(SSD_GROUPS):
        bg = bm[:, g * SSD_STATE:(g + 1) * SSD_STATE]
        cg = cm[:, g * SSD_STATE:(g + 1) * SSD_STATE]
        cb = _dot_nt(cg, bg)
        st_g = state[g * gw:(g + 1) * gw, :]
        y_off = _dot_nt(cg, st_g.astype(BF16)) * ea_e[:, g * gw:(g + 1) * gw]
        diag = []
        for e in range(hpg):
            h = g * hpg + e
            seg = acum[:, h:h + 1] - acum_t[h:h + 1, :]
            lmat = jnp.exp(jnp.where(causal, seg, -jnp.inf))
            m = (cb * lmat).astype(BF16)
            diag.append(_dot(m, dtx_b[:, h * SSD_HEAD_DIM:(h + 1) * SSD_HEAD_DIM]))
        y_parts.append(jnp.concatenate(diag, axis=1) + y_off)
        new = _dot_tn(xdec_b[:, g * gw:(g + 1) * gw], bg)
        dec = jnp.concatenate(
            [jnp.broadcast_to(cdec_t[g * hpg + e:g * hpg + e + 1, :], (SSD_HEAD_DIM, SSD_STATE)) for e in range(hpg)],
            axis=0)
        state[g * gw:(g + 1) * gw, :] = st_g * dec + new
    y = jnp.concatenate(y_parts, axis=1) + dexp_ref[...] * xs
    y = y * _silu(z_ref[...])
    y_ref[...] = _rms(y, gn_ref[...]).astype(y_ref.dtype)

    @pl.when(c == pl.num_programs(1) - 1)
    def _():
        ssm_ref[...] = state[...].reshape(ssm_ref.shape)


def _ssd_chunk(z, xbc, dt, cw, cb, dtb, alog, dexp, gn):
    b, l, d_ssd = z.shape
    conv_dim = xbc.shape[-1]
    n_heads = d_ssd // SSD_HEAD_DIM
    q = SSD_CHUNK
    blk = lambda n: pl.BlockSpec((None, q, n), lambda i, j: (i, j, 0))
    return pl.pallas_call(
        _ssd_chunk_body,
        grid=(b, l // q),
        in_specs=[blk(d_ssd), blk(conv_dim), blk(LANES), _const_spec(cw.shape), _const_spec(cb.shape),
                  _const_spec(dtb.shape), _const_spec(alog.shape), _const_spec(dexp.shape), _const_spec(gn.shape)],
        out_specs=[blk(d_ssd),
                   pl.BlockSpec((None, CONV_W - 1, conv_dim), lambda i, j: (i, 0, 0)),
                   pl.BlockSpec((None, n_heads, SSD_HEAD_DIM, SSD_STATE), lambda i, j: (i, 0, 0, 0))],
        out_shape=[jax.ShapeDtypeStruct((b, l, d_ssd), BF16),
                   jax.ShapeDtypeStruct((b, CONV_W - 1, conv_dim), F32),
                   jax.ShapeDtypeStruct((b, n_heads, SSD_HEAD_DIM, SSD_STATE), F32)],
        scratch_shapes=[pltpu.VMEM((q + 8, conv_dim), F32), pltpu.VMEM((d_ssd, SSD_STATE), F32)],
        compiler_params=_params("arbitrary", "arbitrary"),
        name="ssd_chunk",
    )(z, xbc, dt, cw, cb, dtb, alog, dexp, gn)


def _ssd_step_body(z_ref, xbc_ref, dt_ref, cs_ref, st_ref, cw_ref, cb_ref, dtb_ref, alog_ref, dexp_ref, gn_ref,
                   y_ref, conv_ref, ssm_ref):
    bt, d_ssd = z_ref.shape
    n_heads = d_ssd // SSD_HEAD_DIM
    hpg = n_heads // SSD_GROUPS
    x = xbc_ref[...]
    cw = cw_ref[...]
    conv = cw[0:1] * cs_ref[0] + cw[1:2] * cs_ref[1] + cw[2:3] * cs_ref[2] + cw[3:4] * x + cb_ref[...]
    conv_ref[0] = cs_ref[1]
    conv_ref[1] = cs_ref[2]
    conv_ref[2] = x
    xact = _silu(conv)
    xs = xact[:, :d_ssd]
    bm = xact[:, d_ssd:d_ssd + SSD_GROUPS * SSD_STATE]
    cm = xact[:, d_ssd + SSD_GROUPS * SSD_STATE:]
    dt = _softplus(dt_ref[...] + dtb_ref[...])
    da = jnp.exp(dt * (-jnp.exp(alog_ref[...])))
    dtx = xs * _expand_heads(dt, n_heads)
    pad = jnp.zeros((LANES - bt, d_ssd), F32)
    dtx_t = jnp.concatenate([dtx, pad], axis=0).T
    lane = lax.broadcasted_iota(jnp.int32, (d_ssd, LANES), 1)
    y_t = jnp.zeros((d_ssd, LANES), F32)
    for b in range(bt):
        cols = []
        for h in range(n_heads):
            g = h // hpg
            s = st_ref[b, h]
            xcol = dtx_t[h * SSD_HEAD_DIM:(h + 1) * SSD_HEAD_DIM, b:b + 1]
            brow = bm[b:b + 1, g * SSD_STATE:(g + 1) * SSD_STATE]
            crow = cm[b:b + 1, g * SSD_STATE:(g + 1) * SSD_STATE]
            s_new = s * da[b:b + 1, h:h + 1] + xcol * brow
            ssm_ref[b, h] = s_new
            cols.append(jnp.sum(s_new * crow, axis=1, keepdims=True))
        y_t = jnp.where(lane == b, jnp.concatenate(cols, axis=0), y_t)
    y = y_t.T[:bt, :] + dexp_ref[...] * xs
    y = y * _silu(z_ref[...])
    y_ref[...] = _rms(y, gn_ref[...]).astype(y_ref.dtype)


def _ssd_step(z, xbc, dt, conv_state_t, ssm_state, cw, cb, dtb, alog, dexp, gn):
    nb, d_ssd = z.shape
    conv_dim = xbc.shape[-1]
    n_heads = d_ssd // SSD_HEAD_DIM
    bt = 8
    row = lambda n: pl.BlockSpec((bt, n), lambda i: (i, 0))
    cs_spec = pl.BlockSpec((CONV_W - 1, bt, conv_dim), lambda i: (0, i, 0))
    st_spec = pl.BlockSpec((bt, n_heads, SSD_HEAD_DIM, SSD_STATE), lambda i: (i, 0, 0, 0))
    return pl.pallas_call(
        _ssd_step_body,
        grid=(nb // bt,),
        in_specs=[row(d_ssd), row(conv_dim), row(LANES), cs_spec, st_spec, _const_spec(cw.shape),
                  _const_spec(cb.shape), _const_spec(dtb.shape), _const_spec(alog.shape), _const_spec(dexp.shape),
                  _const_spec(gn.shape)],
        out_specs=[row(d_ssd), cs_spec, st_spec],
        out_shape=[jax.ShapeDtypeStruct((nb, d_ssd), BF16),
                   jax.ShapeDtypeStruct((CONV_W - 1, nb, conv_dim), F32),
                   jax.ShapeDtypeStruct(ssm_state.shape, F32)],
        compiler_params=_params("arbitrary"),
        name="ssd_step",
    )(z, xbc, dt, conv_state_t, ssm_state, cw, cb, dtb, alog, dexp, gn)


def _split_bf16(x):
    hi = x.astype(BF16)
    lo = (x - hi.astype(F32)).astype(BF16)
    return hi, lo


def _s5_prep_body(q, ls_ref, lrp_ref, lip_ref, lr2_ref, li2_ref, brt_ref, bit_ref, ccat_ref, ca_ref, cb_ref,
                  tt_ref, wt_ref, zt_ref, a1_ref, a2_ref):
    ng = ls_ref.shape[0]
    w = q * S5_CH
    step = jnp.exp(ls_ref[...])
    lr, li = lrp_ref[...], lip_ref[...]
    mag = jnp.exp(lr * step)
    ang = li * step
    lbr = mag * jnp.cos(ang)
    lbi = mag * jnp.sin(ang)
    den = lr * lr + li * li
    kr = ((lbr - 1.0) * lr + lbi * li) / den
    ki = (lbi * lr - (lbr - 1.0) * li) / den
    brt, bit = brt_ref[...], bit_ref[...]
    bbr = kr * brt - ki * bit
    bbi = kr * bit + ki * brt
    d = (lax.broadcasted_iota(jnp.int32, (1, 1, w), 2) // S5_CH).astype(F32)
    pm = jnp.exp(d * (lr * step))
    pa = d * ang
    pr = pm * jnp.cos(pa)
    pi = pm * jnp.sin(pa)
    ball = jnp.concatenate([pr * bbr - pi * bbi, pr * bbi + pi * bbr], axis=1)
    wt_ref[...] = ball.astype(BF16)

    lane3 = lax.broadcasted_iota(jnp.int32, (1, 1, 2 * S5_STATE), 2)
    first = lane3 < S5_STATE
    csign = jnp.where(first, ccat_ref[...], -ccat_ref[...])
    ch, cl = _split_bf16(csign)
    bh, bl = _split_bf16(ball)
    bdot = lambda x, y: lax.dot_general(x, y, (((2,), (1,)), ((0,), (0,))), preferred_element_type=F32)
    kall = bdot(ch, bh) + bdot(ch, bl) + bdot(cl, bh)
    k2 = kall.reshape(ng * S5_CH, w)
    lane2 = lax.broadcasted_iota(jnp.int32, (ng * S5_CH, w), 1)
    for t in range(q):
        sh = (q - 1 - t) * S5_CH
        r = pltpu.roll(k2, sh, 1) if sh else k2
        r = jnp.where(lane2 >= sh, r, 0.0)
        tt_ref[:, t * S5_CH:(t + 1) * S5_CH, :] = r.reshape(ng, S5_CH, w).astype(BF16)

    lr2, li2 = lr2_ref[...], li2_ref[...]
    t1 = (lax.broadcasted_iota(jnp.int32, (1, q, 1), 1) + 1).astype(F32)
    zm = jnp.exp(t1 * (lr2 * step))
    za = t1 * (li2 * step)
    zr = zm * jnp.cos(za)
    zi = zm * jnp.sin(za)
    ca, cb = ca_ref[...], cb_ref[...]
    for t in range(q):
        prt = zr[:, t:t + 1, :]
        pit = zi[:, t:t + 1, :]
        zt = jnp.where(first, ca * prt - cb * pit, -(ca * pit) - cb * prt)
        zt_ref[:, t * S5_CH:(t + 1) * S5_CH, :] = zt.astype(BF16)
    qf = float(q)
    mq = jnp.exp(qf * (lr2 * step))
    aq = qf * (li2 * step)
    ar = mq * jnp.cos(aq)
    ai = mq * jnp.sin(aq)
    a1_ref[...] = ar
    a2_ref[...] = jnp.where(first, -ai, ai)


def _s5_prep(q, lam_re, lam_im, log_step, b_re, b_im, c_re, c_im):
    ng, p = lam_re.shape
    w = q * S5_CH
    ls = log_step.reshape(ng, 1, 1)
    lrp = lam_re.reshape(ng, p, 1)
    lip = lam_im.reshape(ng, p, 1)
    lr2 = jnp.concatenate([lam_re, lam_re], axis=-1).reshape(ng, 1, 2 * p)
    li2 = jnp.concatenate([lam_im, lam_im], axis=-1).reshape(ng, 1, 2 * p)
    brt = jnp.tile(b_re, (1, 1, q))
    bit = jnp.tile(b_im, (1, 1, q))
    ccat = jnp.concatenate([c_re, c_im], axis=-1)
    ca = jnp.concatenate([c_re, c_re], axis=-1)
    cb = jnp.concatenate([c_im, c_im], axis=-1)
    gb = 8
    blk = lambda a, b: pl.BlockSpec((gb, a, b), lambda i: (i, 0, 0))
    return pl.pallas_call(
        functools.partial(_s5_prep_body, q),
        grid=(ng // gb,),
        in_specs=[blk(1, 1), blk(p, 1), blk(p, 1), blk(1, 2 * p), blk(1, 2 * p), blk(p, w), blk(p, w),
                  blk(S5_CH, 2 * p), blk(S5_CH, 2 * p), blk(S5_CH, 2 * p)],
        out_specs=[blk(w, w), blk(2 * p, w), blk(w, 2 * p), blk(1, 2 * p), blk(1, 2 * p)],
        out_shape=[jax.ShapeDtypeStruct((ng, w, w), BF16), jax.ShapeDtypeStruct((ng, 2 * p, w), BF16),
                   jax.ShapeDtypeStruct((ng, w, 2 * p), BF16), jax.ShapeDtypeStruct((ng, 1, 2 * p), F32),
                   jax.ShapeDtypeStruct((ng, 1, 2 * p), F32)],
        compiler_params=_params("arbitrary"),
        name=f"s5_prep_q{q}",
    )(ls, lrp, lip, lr2, li2, brt, bit, ccat, ca, cb)


def _s5_mix_body(q, nb, nph, x_ref, g_ref, wut_ref, tt_ref, wt_ref, zt_ref, a1_ref, a2_ref, s0_ref, dcol_ref,
                 wglut_ref, bglu_ref, wo_ref, p_ref, sfin_ref, ut, yt, carry, carry_sw):
    ng = tt_ref.shape[0]
    nkb = x_ref.shape[1]
    nk = nkb // nb
    d = x_ref.shape[2]
    ch = (q // nph) * nkb
    tile = pl.program_id(0)
    ph = pl.program_id(1)

    @pl.when(jnp.logical_and(tile == 0, ph == 0))
    def _():
        s0 = s0_ref[...]
        carry[...] = s0
        carry_sw[...] = pltpu.roll(s0.reshape(ng * nb, 2 * S5_STATE), S5_STATE, 1).reshape(s0.shape)

    @pl.when(ph < nph)
    def _():
        h = _rms(x_ref[...].reshape(ch, d), g_ref[...]).astype(BF16)
        u = _dot_nt(wut_ref[...], h)
        for j in range(nph):
            @pl.when(ph == j)
            def _():
                ut[:, j * ch:(j + 1) * ch] = u

    @pl.when(ph == nph - 1)
    def _():
        def group(g, _):
            r0 = pl.multiple_of(g * S5_CH, S5_CH)
            rows = ut[pl.ds(r0, S5_CH), :].astype(BF16)
            ugt = jnp.concatenate([rows[:, (q - 1 - j) * nkb:(q - j) * nkb] for j in range(q)], axis=0)
            y = _dot(tt_ref[g], ugt)
            vt = _dot(wt_ref[g], ugt)
            v = vt.T
            v_sw = jnp.concatenate([vt[S5_STATE:], vt[:S5_STATE]], axis=0).T
            a1 = a1_ref[g]
            a2 = a2_ref[g]
            s = carry[g]
            s_sw = carry_sw[g]
            prev = []
            for k in range(nk):
                prev.append(s)
                s, s_sw = (a1 * s + a2 * s_sw + v[k * nb:(k + 1) * nb, :],
                           a1 * s_sw - a2 * s + v_sw[k * nb:(k + 1) * nb, :])
            carry[g] = s
            carry_sw[g] = s_sw
            sprev = jnp.concatenate(prev, axis=0).astype(BF16)
            y = y + _dot_nt(zt_ref[g], sprev)
            for t in range(q):
                yt[pl.ds(r0, S5_CH), t * nkb:(t + 1) * nkb] = y[t * S5_CH:(t + 1) * S5_CH, :]
            return 0

        lax.fori_loop(0, ng, group, 0)
        sfin_ref[...] = carry[...]

    @pl.when(ph >= nph)
    def _():
        for j in range(nph):
            @pl.when(ph == nph + j)
            def _():
                sl = slice(j * ch, (j + 1) * ch)
                gt = _gelu_tanh(yt[:, sl] + dcol_ref[...] * ut[:, sl])
                gate = _dot(wglut_ref[...], gt.astype(BF16)) + bglu_ref[...]
                y5 = (gt * _sigmoid(gate)).astype(BF16)
                p_ref[...] = _dot_tn(y5, wo_ref[...]).reshape(p_ref.shape)


def _s5_mix(q, nb, xp, ln_g, wut, tt, wt, zt, a1, a2, s0, dcol, wglut, bglu, wo):
    _, nlt, d = xp.shape
    nkb = min(LANES, nlt)
    ntile = nlt // nkb
    tok = q * nkb
    nph = max(1, tok // TOKEN_TILE)
    qs = q // nph
    ng = tt.shape[0]
    dm = wo.shape[1]
    body = functools.partial(_s5_mix_body, q, nb, nph)
    x_spec = pl.BlockSpec((qs, nkb, d), lambda i, j: (jnp.minimum(j, nph - 1), i, 0))
    p_spec = pl.BlockSpec((qs, nkb, dm), lambda i, j: (jnp.maximum(j - nph, 0), i, 0))
    return pl.pallas_call(
        body,
        grid=(ntile, 2 * nph),
        in_specs=[x_spec, _const_spec(ln_g.shape), _const_spec(wut.shape), _const_spec(tt.shape),
                  _const_spec(wt.shape), _const_spec(zt.shape), _const_spec(a1.shape), _const_spec(a2.shape),
                  _const_spec(s0.shape), _const_spec(dcol.shape), _const_spec(wglut.shape),
                  _const_spec(bglu.shape), _const_spec(wo.shape)],
        out_specs=[p_spec, pl.BlockSpec(s0.shape, lambda i, j: (0, 0, 0))],
        out_shape=[jax.ShapeDtypeStruct((q, nlt, dm), F32), jax.ShapeDtypeStruct(s0.shape, F32)],
        scratch_shapes=[pltpu.VMEM((d, tok), F32), pltpu.VMEM((d, tok), F32),
                        pltpu.VMEM(s0.shape, F32), pltpu.VMEM(s0.shape, F32)],
        compiler_params=_params("arbitrary", "arbitrary"),
        name=f"s5_mix_q{q}",
    )(xp, ln_g, wut, tt, wt, zt, a1, a2, s0, dcol, wglut, bglu, wo)


def _kv_body(m_ref, wk_ref, wv_ref, k_ref, v_ref):
    m = m_ref[...].astype(BF16)
    k_ref[...] = _dot(m, wk_ref[...])
    v_ref[...] = _dot(m, wv_ref[...])


def _kv_proj(mem2d, wk, wv):
    t, d = mem2d.shape
    tm = min(TOKEN_TILE, t)
    row = pl.BlockSpec((tm, d), lambda i: (i, 0))
    return pl.pallas_call(
        _kv_body, grid=(t // tm,),
        in_specs=[row, _const_spec(wk.shape), _const_spec(wv.shape)],
        out_specs=[row, row],
        out_shape=[jax.ShapeDtypeStruct((t, d), F32)] * 2,
        compiler_params=_params("arbitrary"),
        name="kv_proj",
    )(mem2d, wk, wv)


def _attn_prompt_body(x_ref, ys_ref, p_ref, wo1_ref, g2_ref, wq_ref, k_ref, v_ref, wxo_ref, o_ref):
    d = x_ref.shape[-1]
    hd = d // X_HEADS
    x1 = x_ref[...] + _dot(ys_ref[...], wo1_ref[...]) + p_ref[...]
    hq = _rms(x1, g2_ref[...]).astype(BF16)
    qv = _dot(hq, wq_ref[...]).astype(BF16)
    kb = k_ref[...].astype(BF16)
    vb = v_ref[...].astype(BF16)
    outs = []
    for h in range(X_HEADS):
        sl = slice(h * hd, (h + 1) * hd)
        s = _dot_nt(qv[:, sl], kb[:, sl]) * (hd ** -0.5)
        e = jnp.exp(s - jnp.max(s, axis=-1, keepdims=True))
        p = (e / jnp.sum(e, axis=-1, keepdims=True)).astype(BF16)
        outs.append(_dot(p, vb[:, sl]))
    o = jnp.concatenate(outs, axis=1).astype(BF16)
    o_ref[...] = x1 + _dot(o, wxo_ref[...])


def _attn_prompt(x, ys, pm, wo1, g2, wq, mk, mv, wxo):
    b, l, d = x.shape
    nm = mk.shape[1]
    tm = min(TOKEN_TILE, l)
    row = pl.BlockSpec((None, tm, d), lambda i, j: (i, j, 0))
    kv = pl.BlockSpec((None, nm, d), lambda i, j: (i, 0, 0))
    return pl.pallas_call(
        _attn_prompt_body, grid=(b, l // tm),
        in_specs=[row, row, row, _const_spec(wo1.shape), _const_spec(g2.shape), _const_spec(wq.shape), kv, kv,
                  _const_spec(wxo.shape)],
        out_specs=row,
        out_shape=jax.ShapeDtypeStruct((b, l, d), F32),
        compiler_params=_params("arbitrary", "arbitrary"),
        name="attn_prompt",
    )(x, ys, pm, wo1, g2, wq, mk, mv, wxo)


def _merge_q_body(x_ref, ys_ref, p_ref, wo1_ref, g2_ref, wq_ref, x1_ref, q_ref):
    x1 = x_ref[...] + _dot(ys_ref[...], wo1_ref[...]) + p_ref[...]
    x1_ref[...] = x1
    q_ref[...] = _dot(_rms(x1, g2_ref[...]).astype(BF16), wq_ref[...])


def _merge_q(x2d, ys, pm, wo1, g2, wq):
    t, d = x2d.shape
    return pl.pallas_call(
        _merge_q_body,
        out_shape=[jax.ShapeDtypeStruct((t, d), F32)] * 2,
        compiler_params=pltpu.CompilerParams(vmem_limit_bytes=VMEM_LIMIT),
        name="merge_q",
    )(x2d, ys, pm, wo1, g2, wq)


def _attn_step_body(q_ref, k_ref, v_ref, o_ref):
    bt, nm, d = k_ref.shape
    hd = d // X_HEADS
    lane = lax.broadcasted_iota(jnp.int32, (1, d), 1)
    for b in range(bt):
        qrow = q_ref[b]
        prod = k_ref[b] * qrow
        pfull = jnp.zeros((nm, d), F32)
        for h in range(X_HEADS):
            s = jnp.sum(prod[:, h * hd:(h + 1) * hd], axis=1, keepdims=True) * (hd ** -0.5)
            e = jnp.exp(s - jnp.max(s, axis=0, keepdims=True))
            p = e / jnp.sum(e, axis=0, keepdims=True)
            pfull = jnp.where(jnp.logical_and(lane >= h * hd, lane < (h + 1) * hd), p, pfull)
        o_ref[b] = jnp.sum(pfull * v_ref[b], axis=0, keepdims=True)


def _attn_step(q3, mk, mv):
    nbt, _, d = q3.shape
    nm = mk.shape[1]
    bt = 4
    qs = pl.BlockSpec((bt, 1, d), lambda i: (i, 0, 0))
    kv = pl.BlockSpec((bt, nm, d), lambda i: (i, 0, 0))
    return pl.pallas_call(
        _attn_step_body, grid=(nbt // bt,),
        in_specs=[qs, kv, kv], out_specs=qs,
        out_shape=jax.ShapeDtypeStruct((nbt, 1, d), F32),
        compiler_params=_params("arbitrary"),
        name="attn_step",
    )(q3, mk, mv)


def _ffn_body(has_o, final, *refs):
    if has_o:
        x_ref, o_in_ref, wxo_ref, g3_ref, wg_ref, wu_ref, wd_ref, gf_ref, y_ref = refs
        x2 = x_ref[...] + _dot(o_in_ref[...].astype(BF16), wxo_ref[...])
    else:
        x_ref, g3_ref, wg_ref, wu_ref, wd_ref, gf_ref, y_ref = refs
        x2 = x_ref[...]
    hf = _rms(x2, g3_ref[...]).astype(BF16)
    act = (_silu(_dot(hf, wg_ref[...])) * _dot(hf, wu_ref[...])).astype(BF16)
    x3 = x2 + _dot(act, wd_ref[...])
    y_ref[...] = _rms(x3, gf_ref[...]) if final else x3


def _ffn(x2d, g3, wg, wu, wd, gf, final, o_in=None, wxo=None):
    t, d = x2d.shape
    tm = min(TOKEN_TILE, t)
    row = pl.BlockSpec((tm, d), lambda i: (i, 0))
    has_o = o_in is not None
    ins = [x2d] + ([o_in, wxo] if has_o else []) + [g3, wg, wu, wd, gf]
    specs = [row] + ([row, _const_spec(wxo.shape)] if has_o else []) + [
        _const_spec(g3.shape), _const_spec(wg.shape), _const_spec(wu.shape), _const_spec(wd.shape),
        _const_spec(gf.shape)]
    return pl.pallas_call(
        functools.partial(_ffn_body, has_o, final), grid=(t // tm,),
        in_specs=specs, out_specs=row,
        out_shape=jax.ShapeDtypeStruct((t, d), F32),
        compiler_params=_params("arbitrary"),
        name="ffn_o" if has_o else "ffn",
    )(*ins)


def _layer_weights(i, ln1_g, w_in, conv_w, conv_b, dt_bias, a_log, d_ssd, gn_g, lam_re, lam_im, log_step,
                   b_re, b_im, c_re, c_im, d_s5, w_glu, b_glu, w_out, ln2_g, w_xq, w_xk, w_xv, w_xo, ln3_g,
                   w_gate, w_up, w_down):
    d = w_in.shape[1]
    n_heads = dt_bias.shape[1]
    dssd = n_heads * SSD_HEAD_DIM
    conv_dim = conv_w.shape[2]
    o1, o2, o3 = dssd, dssd + conv_dim, dssd + conv_dim + n_heads
    win = w_in[i]
    w = {}
    w["ln1"] = ln1_g[i].reshape(1, d)
    w["wz"] = win[:, :o1].astype(BF16)
    w["wx"] = win[:, o1:o2].astype(BF16)
    w["wd"] = jnp.pad(win[:, o2:o3], ((0, 0), (0, LANES - n_heads))).astype(BF16)
    w["wut"] = win[:, o3:].T.astype(BF16)
    w["cw"] = conv_w[i]
    w["cb"] = conv_b[i].reshape(1, conv_dim)
    w["dtb"] = jnp.pad(dt_bias[i], (0, LANES - n_heads)).reshape(1, LANES)
    w["alog"] = jnp.pad(a_log[i], (0, LANES - n_heads)).reshape(1, LANES)
    w["dexp"] = jnp.repeat(d_ssd[i], SSD_HEAD_DIM).reshape(1, dssd)
    w["gn"] = gn_g[i].reshape(1, dssd)
    w["s5"] = (lam_re[i], lam_im[i], log_step[i], b_re[i], b_im[i], c_re[i], c_im[i])
    ds5 = d_s5.shape[1]
    w["dcol"] = d_s5[i].reshape(ds5, 1)
    w["wglut"] = w_glu[i].T.astype(BF16)
    w["bglu"] = b_glu[i].reshape(ds5, 1)
    w["wo1"] = w_out[i, :dssd].astype(BF16)
    w["wo2"] = w_out[i, dssd:].astype(BF16)
    w["ln2"] = ln2_g[i].reshape(1, d)
    w["wq"] = w_xq[i].astype(BF16)
    w["wk"] = w_xk[i].astype(BF16)
    w["wv"] = w_xv[i].astype(BF16)
    w["wxo"] = w_xo[i].astype(BF16)
    w["ln3"] = ln3_g[i].reshape(1, d)
    w["wg"] = w_gate[i].astype(BF16)
    w["wu"] = w_up[i].astype(BF16)
    w["wdn"] = w_down[i].astype(BF16)
    return w


def _state_to_rows(s_re, s_im):
    return jnp.concatenate([s_re, s_im], axis=-1).transpose(1, 0, 2)


def _rows_to_state(s):
    p = s.shape[-1] // 2
    st = s.transpose(1, 0, 2)
    return st[..., :p], st[..., p:]


def kernel(x_prompt, x_sample, mem_prompt, state_conv, state_ssm, state_s5_re, state_s5_im, cache_mem_k, cache_mem_v, ln1_g, w_in, conv_w, conv_b, dt_bias, a_log, d_ssd, gn_g, lam_re, lam_im, log_step, b_re, b_im, c_re, c_im, d_s5, w_glu, b_glu, w_out, ln2_g, w_xq, w_xk, w_xv, w_xo, ln3_g, w_gate, w_up, w_down, final_g):
    bp, seq, d = x_prompt.shape
    bs = x_sample.shape[0]
    depth = w_in.shape[0]
    nm = mem_prompt.shape[1]
    ng, ns = lam_re.shape[1], lam_re.shape[2]
    gf = final_g.reshape(1, d)
    hp = x_prompt
    hs = x_sample.reshape(bs, d)
    outs = {k: [] for k in ("conv_p", "ssm_p", "re_p", "im_p", "mk_p", "mv_p", "conv_s", "ssm_s", "re_s", "im_s")}
    yp = ys_out = None
    for i in range(depth):
        w = _layer_weights(i, ln1_g, w_in, conv_w, conv_b, dt_bias, a_log, d_ssd, gn_g, lam_re, lam_im, log_step,
                           b_re, b_im, c_re, c_im, d_s5, w_glu, b_glu, w_out, ln2_g, w_xq, w_xk, w_xv, w_xo, ln3_g,
                           w_gate, w_up, w_down)
        last = i == depth - 1
        mk, mv = _kv_proj(mem_prompt.reshape(bp * nm, d), w["wk"], w["wv"])
        mk = mk.reshape(bp, nm, d)
        mv = mv.reshape(bp, nm, d)
        z, xbc, dtr = _inproj(hp.reshape(bp * seq, d), w["ln1"], w["wz"], w["wx"], w["wd"])
        y_ssd, conv_p, ssm_p = _ssd_chunk(z.reshape(bp, seq, -1), xbc.reshape(bp, seq, -1), dtr.reshape(bp, seq, -1),
                                          w["cw"], w["cb"], w["dtb"], w["alog"], w["dexp"], w["gn"])
        q = S5_Q
        tt, wt, zt, a1, a2 = _s5_prep(q, *w["s5"])
        xperm = hp.reshape(bp, seq // q, q, d).transpose(2, 1, 0, 3).reshape(q, (seq // q) * bp, d)
        s0 = jnp.zeros((ng, bp, 2 * ns), F32)
        pperm, sfin = _s5_mix(q, bp, xperm, w["ln1"], w["wut"], tt, wt, zt, a1, a2, s0, w["dcol"], w["wglut"],
                              w["bglu"], w["wo2"])
        pm = pperm.reshape(q, seq // q, bp, d).transpose(2, 1, 0, 3).reshape(bp, seq, d)
        re_p, im_p = _rows_to_state(sfin)
        x2 = _attn_prompt(hp, y_ssd, pm, w["wo1"], w["ln2"], w["wq"], mk, mv, w["wxo"])
        hp = _ffn(x2.reshape(bp * seq, d), w["ln3"], w["wg"], w["wu"], w["wdn"], gf, last).reshape(bp, seq, d)
        zs, xbcs, dts = _inproj(hs, w["ln1"], w["wz"], w["wx"], w["wd"])
        ys_ssd, conv_s_t, ssm_s = _ssd_step(zs, xbcs, dts, state_conv[i].transpose(1, 0, 2), state_ssm[i],
                                            w["cw"], w["cb"], w["dtb"], w["alog"], w["dexp"], w["gn"])
        tt1, wt1, zt1, a11, a21 = _s5_prep(1, *w["s5"])
        s0s = _state_to_rows(state_s5_re[i], state_s5_im[i])
        ps, sfin_s = _s5_mix(1, bs, hs.reshape(1, bs, d), w["ln1"], w["wut"], tt1, wt1, zt1, a11, a21, s0s,
                             w["dcol"], w["wglut"], w["bglu"], w["wo2"])
        re_s, im_s = _rows_to_state(sfin_s)
        x1s, qs = _merge_q(hs, ys_ssd, ps.reshape(bs, d), w["wo1"], w["ln2"], w["wq"])
        o_s = _attn_step(qs.reshape(bs, 1, d), cache_mem_k[i].reshape(bs, nm, d), cache_mem_v[i].reshape(bs, nm, d))
        hs = _ffn(x1s, w["ln3"], w["wg"], w["wu"], w["wdn"], gf, last, o_in=o_s.reshape(bs, d), wxo=w["wxo"])
        for k, v in (("conv_p", conv_p), ("ssm_p", ssm_p), ("re_p", re_p), ("im_p", im_p),
                     ("mk_p", mk.reshape(bp, nm, X_HEADS, d // X_HEADS)),
                     ("mv_p", mv.reshape(bp, nm, X_HEADS, d // X_HEADS)),
                     ("conv_s", conv_s_t.transpose(1, 0, 2)), ("ssm_s", ssm_s), ("re_s", re_s), ("im_s", im_s)):
            outs[k].append(v)
    st = lambda k: jnp.stack(outs[k])
    return (hp, hs.reshape(bs, 1, d), st("conv_p"), st("ssm_p"), st("re_p"), st("im_p"), st("mk_p"), st("mv_p"),
            st("conv_s"), st("ssm_s"), st("re_s"), st("im_s"))
```

```python
import functools
import math

import jax
import jax.numpy as jnp
from jax import lax
from jax.experimental import pallas as pl
from jax.experimental.pallas import tpu as pltpu

F32 = jnp.float32
BF16 = jnp.bfloat16
EPS = 1e-6

LANES = 128
VMEM_LIMIT = 56 * 1024 * 1024

SSD_HEAD_DIM = 64
SSD_STATE = 128
SSD_GROUPS = 2
SSD_CHUNK = 128
CONV_W = 4
S5_CH = 16
S5_STATE = 64
S5_Q = 16
S5_GROUP_BATCH = 16
X_HEADS = 4
TOKEN_TILE = 512


def _const_spec(shape):
    nd = len(shape)
    return pl.BlockSpec(shape, lambda *_: (0,) * nd, pipeline_mode=pl.Buffered(1))


def _params(*sem):
    return pltpu.CompilerParams(dimension_semantics=sem, vmem_limit_bytes=VMEM_LIMIT)


def _rms(x, g):
    return x * lax.rsqrt(jnp.mean(x * x, axis=-1, keepdims=True) + EPS) * g


def _sigmoid(x):
    return 1.0 / (1.0 + jnp.exp(-x))


def _silu(x):
    return x * _sigmoid(x)


def _gelu_tanh(x):
    return 0.5 * x * (1.0 + jnp.tanh(math.sqrt(2.0 / math.pi) * (x + 0.044715 * (x * x * x))))


def _softplus(x):
    return jnp.maximum(x, 0.0) + jnp.log1p(jnp.exp(-jnp.abs(x)))


def _dot(a, b):
    return jnp.dot(a, b, preferred_element_type=F32)


def _dot_nt(a, b):
    return lax.dot_general(a, b, (((1,), (1,)), ((), ())), preferred_element_type=F32)


def _dot_tn(a, b):
    return lax.dot_general(a, b, (((0,), (0,)), ((), ())), preferred_element_type=F32)


def _inproj_body(x_ref, g_ref, wz_ref, wx_ref, wd_ref, z_ref, xbc_ref, dt_ref):
    h = _rms(x_ref[...], g_ref[...]).astype(BF16)
    z_ref[...] = _dot(h, wz_ref[...])
    xbc_ref[...] = _dot(h, wx_ref[...])
    dt_ref[...] = _dot(h, wd_ref[...])


def _inproj(x2d, ln_g, wz, wx, wd):
    t, d = x2d.shape
    tm = min(TOKEN_TILE, t)
    nz, nx, nd = wz.shape[1], wx.shape[1], wd.shape[1]
    row = lambda n: pl.BlockSpec((tm, n), lambda i: (i, 0))
    return pl.pallas_call(
        _inproj_body,
        grid=(t // tm,),
        in_specs=[row(d), _const_spec((1, d)), _const_spec(wz.shape), _const_spec(wx.shape), _const_spec(wd.shape)],
        out_specs=[row(nz), row(nx), row(nd)],
        out_shape=[jax.ShapeDtypeStruct((t, nz), F32), jax.ShapeDtypeStruct((t, nx), F32),
                   jax.ShapeDtypeStruct((t, nd), F32)],
        compiler_params=_params("arbitrary"),
        name="inproj",
    )(x2d, ln_g, wz, wx, wd)


def _expand_heads(v, n_heads):
    rows = v.shape[0]
    lane = lax.broadcasted_iota(jnp.int32, (rows, LANES), 1)
    pieces = [jnp.where(lane < SSD_HEAD_DIM, v[:, 2 * j:2 * j + 1], v[:, 2 * j + 1:2 * j + 2])
              for j in range(n_heads // 2)]
    return jnp.concatenate(pieces, axis=1)


def _ssd_chunk_body(z_ref, xbc_ref, dt_ref, cw_ref, cb_ref, dtb_ref, alog_ref, dexp_ref, gn_ref,
                    y_ref, conv_ref, ssm_ref, xpad, state):
    q = SSD_CHUNK
    d_ssd = z_ref.shape[-1]
    n_heads = d_ssd // SSD_HEAD_DIM
    hpg = n_heads // SSD_GROUPS
    gw = hpg * SSD_HEAD_DIM
    c = pl.program_id(1)

    @pl.when(c == 0)
    def _():
        xpad[0:8, :] = jnp.zeros((8, xpad.shape[1]), F32)
        state[...] = jnp.zeros(state.shape, F32)

    x = xbc_ref[...]
    xpad[8:8 + q, :] = x
    cw = cw_ref[...]
    conv = (cw[3:4] * x + cw[2:3] * xpad[7:7 + q, :] + cw[1:2] * xpad[6:6 + q, :]
            + cw[0:1] * xpad[5:5 + q, :] + cb_ref[...])
    xpad[0:8, :] = xpad[q:q + 8, :]
    conv_ref[...] = x[q - (CONV_W - 1):q, :]
    xact = _silu(conv)
    xs = xact[:, :d_ssd]
    bm = xact[:, d_ssd:d_ssd + SSD_GROUPS * SSD_STATE].astype(BF16)
    cm = xact[:, d_ssd + SSD_GROUPS * SSD_STATE:].astype(BF16)

    dt = _softplus(dt_ref[...] + dtb_ref[...])
    a = -jnp.exp(alog_ref[...])
    row = lax.broadcasted_iota(jnp.int32, (q, q), 0)
    col = lax.broadcasted_iota(jnp.int32, (q, q), 1)
    causal = row >= col
    tri = jnp.where(causal, 1.0, 0.0).astype(F32)
    acum = jnp.dot(tri, dt * a, precision=lax.Precision.HIGHEST, preferred_element_type=F32)
    acum_t = acum.T
    last = acum[q - 1:q, :]
    dt_e = _expand_heads(dt, n_heads)
    ea_e = _expand_heads(jnp.exp(acum), n_heads)
    dend_e = _expand_heads(jnp.exp(last - acum), n_heads)
    cdec_t = jnp.exp(acum_t[:, q - 1:q])

    dtx = xs * dt_e
    dtx_b = dtx.astype(BF16)
    xdec_b = (dtx * dend_e).astype(BF16)

    y_parts = []
    for g in range(SSD_GROUPS):
        bg = bm[:, g * SSD_STATE:(g + 1) * SSD_STATE]
        cg = cm[:, g * SSD_STATE:(g + 1) * SSD_STATE]
        cb = _dot_nt(cg, bg)
        st_g = state[g * gw:(g + 1) * gw, :]
        y_off = _dot_nt(cg, st_g.astype(BF16)) * ea_e[:, g * gw:(g + 1) * gw]
        diag = []
        for e in range(hpg):
            h = g * hpg + e
            seg = acum[:, h:h + 1] - acum_t[h:h + 1, :]
            lmat = jnp.exp(jnp.where(causal, seg, -jnp.inf))
            m = (cb * lmat).astype(BF16)
            diag.append(_dot(m, dtx_b[:, h * SSD_HEAD_DIM:(h + 1) * SSD_HEAD_DIM]))
        y_parts.append(jnp.concatenate(diag, axis=1) + y_off)
        new = _dot_tn(xdec_b[:, g * gw:(g + 1) * gw], bg)
        dec = jnp.concatenate(
            [jnp.broadcast_to(cdec_t[g * hpg + e:g * hpg + e + 1, :], (SSD_HEAD_DIM, SSD_STATE)) for e in range(hpg)],
            axis=0)
        state[g * gw:(g + 1) * gw, :] = st_g * dec + new
    y = jnp.concatenate(y_parts, axis=1) + dexp_ref[...] * xs
    y = y * _silu(z_ref[...])
    y_ref[...] = _rms(y, gn_ref[...]).astype(y_ref.dtype)

    @pl.when(c == pl.num_programs(1) - 1)
    def _():
        ssm_ref[...] = state[...].reshape(ssm_ref.shape)


def _ssd_chunk(z, xbc, dt, cw, cb, dtb, alog, dexp, gn):
    b, l, d_ssd = z.shape
    conv_dim = xbc.shape[-1]
    n_heads = d_ssd // SSD_HEAD_DIM
    q = SSD_CHUNK
    blk = lambda n: pl.BlockSpec((None, q, n), lambda i, j: (i, j, 0))
    return pl.pallas_call(
        _ssd_chunk_body,
        grid=(b, l // q),
        in_specs=[blk(d_ssd), blk(conv_dim), blk(LANES), _const_spec(cw.shape), _const_spec(cb.shape),
                  _const_spec(dtb.shape), _const_spec(alog.shape), _const_spec(dexp.shape), _const_spec(gn.shape)],
        out_specs=[blk(d_ssd),
                   pl.BlockSpec((None, CONV_W - 1, conv_dim), lambda i, j: (i, 0, 0)),
                   pl.BlockSpec((None, n_heads, SSD_HEAD_DIM, SSD_STATE), lambda i, j: (i, 0, 0, 0))],
        out_shape=[jax.ShapeDtypeStruct((b, l, d_ssd), BF16),
                   jax.ShapeDtypeStruct((b, CONV_W - 1, conv_dim), F32),
                   jax.ShapeDtypeStruct((b, n_heads, SSD_HEAD_DIM, SSD_STATE), F32)],
        scratch_shapes=[pltpu.VMEM((q + 8, conv_dim), F32), pltpu.VMEM((d_ssd, SSD_STATE), F32)],
        compiler_params=_params("arbitrary", "arbitrary"),
        name="ssd_chunk",
    )(z, xbc, dt, cw, cb, dtb, alog, dexp, gn)


def _ssd_step_body(z_ref, xbc_ref, dt_ref, cs_ref, st_ref, cw_ref, cb_ref, dtb_ref, alog_ref, dexp_ref, gn_ref,
                   y_ref, conv_ref, ssm_ref):
    bt, d_ssd = z_ref.shape
    n_heads = d_ssd // SSD_HEAD_DIM
    hpg = n_heads // SSD_GROUPS
    x = xbc_ref[...]
    cw = cw_ref[...]
    conv = cw[0:1] * cs_ref[0] + cw[1:2] * cs_ref[1] + cw[2:3] * cs_ref[2] + cw[3:4] * x + cb_ref[...]
    conv_ref[0] = cs_ref[1]
    conv_ref[1] = cs_ref[2]
    conv_ref[2] = x
    xact = _silu(conv)
    xs = xact[:, :d_ssd]
    bm = xact[:, d_ssd:d_ssd + SSD_GROUPS * SSD_STATE]
    cm = xact[:, d_ssd + SSD_GROUPS * SSD_STATE:]
    dt = _softplus(dt_ref[...] + dtb_ref[...])
    da = jnp.exp(dt * (-jnp.exp(alog_ref[...])))
    dtx = xs * _expand_heads(dt, n_heads)
    pad = jnp.zeros((LANES - bt, d_ssd), F32)
    to_cols = lambda v: jnp.concatenate([v, pad], axis=0).T
    dtx_t = to_cols(dtx)
    da_t = to_cols(_expand_heads(da, n_heads))
    lane = lax.broadcasted_iota(jnp.int32, (d_ssd, LANES), 1)
    gw = hpg * SSD_HEAD_DIM
    rows_of = lambda v, b: jnp.concatenate(
        [jnp.broadcast_to(v[b:b + 1, g * SSD_STATE:(g + 1) * SSD_STATE], (gw, SSD_STATE)) for g in range(SSD_GROUPS)],
        axis=0)
    y_t = jnp.zeros((d_ssd, LANES), F32)
    for b in range(bt):
        s0 = st_ref[b].reshape(d_ssd, SSD_STATE)
        s_new = s0 * da_t[:, b:b + 1] + dtx_t[:, b:b + 1] * rows_of(bm, b)
        ssm_ref[b] = s_new.reshape(n_heads, SSD_HEAD_DIM, SSD_STATE)
        ycol = jnp.sum(s_new * rows_of(cm, b), axis=1, keepdims=True)
        y_t = jnp.where(lane == b, ycol, y_t)
    y = y_t.T[:bt, :] + dexp_ref[...] * xs
    y = y * _silu(z_ref[...])
    y_ref[...] = _rms(y, gn_ref[...]).astype(y_ref.dtype)


def _ssd_step(z, xbc, dt, conv_state_t, ssm_state, cw, cb, dtb, alog, dexp, gn):
    nb, d_ssd = z.shape
    conv_dim = xbc.shape[-1]
    n_heads = d_ssd // SSD_HEAD_DIM
    bt = 8
    row = lambda n: pl.BlockSpec((bt, n), lambda i: (i, 0))
    cs_spec = pl.BlockSpec((CONV_W - 1, bt, conv_dim), lambda i: (0, i, 0))
    st_spec = pl.BlockSpec((bt, n_heads, SSD_HEAD_DIM, SSD_STATE), lambda i: (i, 0, 0, 0))
    return pl.pallas_call(
        _ssd_step_body,
        grid=(nb // bt,),
        in_specs=[row(d_ssd), row(conv_dim), row(LANES), cs_spec, st_spec, _const_spec(cw.shape),
                  _const_spec(cb.shape), _const_spec(dtb.shape), _const_spec(alog.shape), _const_spec(dexp.shape),
                  _const_spec(gn.shape)],
        out_specs=[row(d_ssd), cs_spec, st_spec],
        out_shape=[jax.ShapeDtypeStruct((nb, d_ssd), BF16),
                   jax.ShapeDtypeStruct((CONV_W - 1, nb, conv_dim), F32),
                   jax.ShapeDtypeStruct(ssm_state.shape, F32)],
        compiler_params=_params("arbitrary"),
        name="ssd_step",
    )(z, xbc, dt, conv_state_t, ssm_state, cw, cb, dtb, alog, dexp, gn)


def _split_bf16(x):
    hi = x.astype(BF16)
    lo = (x - hi.astype(F32)).astype(BF16)
    return hi, lo


def _s5_prep_body(q, ls_ref, lrp_ref, lip_ref, lr2_ref, li2_ref, brt_ref, bit_ref, ccat_ref, ca_ref, cb_ref,
                  tt_ref, wt_ref, zt_ref, a1_ref, a2_ref):
    ng = ls_ref.shape[0]
    w = q * S5_CH
    step = jnp.exp(ls_ref[...])
    lr, li = lrp_ref[...], lip_ref[...]
    mag = jnp.exp(lr * step)
    ang = li * step
    lbr = mag * jnp.cos(ang)
    lbi = mag * jnp.sin(ang)
    den = lr * lr + li * li
    kr = ((lbr - 1.0) * lr + lbi * li) / den
    ki = (lbi * lr - (lbr - 1.0) * li) / den
    brt, bit = brt_ref[...], bit_ref[...]
    bbr = kr * brt - ki * bit
    bbi = kr * bit + ki * brt
    d = (lax.broadcasted_iota(jnp.int32, (1, 1, w), 2) // S5_CH).astype(F32)
    pm = jnp.exp(d * (lr * step))
    pa = d * ang
    pr = pm * jnp.cos(pa)
    pi = pm * jnp.sin(pa)
    ball = jnp.concatenate([pr * bbr - pi * bbi, pr * bbi + pi * bbr], axis=1)
    wt_ref[...] = ball.astype(BF16)

    lane3 = lax.broadcasted_iota(jnp.int32, (1, 1, 2 * S5_STATE), 2)
    first = lane3 < S5_STATE
    csign = jnp.where(first, ccat_ref[...], -ccat_ref[...])
    ch, cl = _split_bf16(csign)
    bh, bl = _split_bf16(ball)
    bdot = lambda x, y: lax.dot_general(x, y, (((2,), (1,)), ((0,), (0,))), preferred_element_type=F32)
    kall = bdot(ch, bh) + bdot(ch, bl) + bdot(cl, bh)
    k2 = kall.reshape(ng * S5_CH, w)
    lane2 = lax.broadcasted_iota(jnp.int32, (ng * S5_CH, w), 1)
    for t in range(q):
        sh = (q - 1 - t) * S5_CH
        r = pltpu.roll(k2, sh, 1) if sh else k2
        r = jnp.where(lane2 >= sh, r, 0.0)
        tt_ref[:, t * S5_CH:(t + 1) * S5_CH, :] = r.reshape(ng, S5_CH, w).astype(BF16)

    lr2, li2 = lr2_ref[...], li2_ref[...]
    t1 = (lax.broadcasted_iota(jnp.int32, (1, q, 1), 1) + 1).astype(F32)
    zm = jnp.exp(t1 * (lr2 * step))
    za = t1 * (li2 * step)
    zr = zm * jnp.cos(za)
    zi = zm * jnp.sin(za)
    ca, cb = ca_ref[...], cb_ref[...]
    for t in range(q):
        prt = zr[:, t:t + 1, :]
        pit = zi[:, t:t + 1, :]
        zt = jnp.where(first, ca * prt - cb * pit, -(ca * pit) - cb * prt)
        zt_ref[:, t * S5_CH:(t + 1) * S5_CH, :] = zt.astype(BF16)
    qf = float(q)
    mq = jnp.exp(qf * (lr2 * step))
    aq = qf * (li2 * step)
    ar = mq * jnp.cos(aq)
    ai = mq * jnp.sin(aq)
    a1_ref[...] = ar
    a2_ref[...] = jnp.where(first, -ai, ai)


def _s5_prep(q, lam_re, lam_im, log_step, b_re, b_im, c_re, c_im):
    ng, p = lam_re.shape
    w = q * S5_CH
    ls = log_step.reshape(ng, 1, 1)
    lrp = lam_re.reshape(ng, p, 1)
    lip = lam_im.reshape(ng, p, 1)
    lr2 = jnp.concatenate([lam_re, lam_re], axis=-1).reshape(ng, 1, 2 * p)
    li2 = jnp.concatenate([lam_im, lam_im], axis=-1).reshape(ng, 1, 2 * p)
    brt = jnp.tile(b_re, (1, 1, q))
    bit = jnp.tile(b_im, (1, 1, q))
    ccat = jnp.concatenate([c_re, c_im], axis=-1)
    ca = jnp.concatenate([c_re, c_re], axis=-1)
    cb = jnp.concatenate([c_im, c_im], axis=-1)
    gb = 8
    blk = lambda a, b: pl.BlockSpec((gb, a, b), lambda i: (i, 0, 0))
    return pl.pallas_call(
        functools.partial(_s5_prep_body, q),
        grid=(ng // gb,),
        in_specs=[blk(1, 1), blk(p, 1), blk(p, 1), blk(1, 2 * p), blk(1, 2 * p), blk(p, w), blk(p, w),
                  blk(S5_CH, 2 * p), blk(S5_CH, 2 * p), blk(S5_CH, 2 * p)],
        out_specs=[blk(w, w), blk(2 * p, w), blk(w, 2 * p), blk(1, 2 * p), blk(1, 2 * p)],
        out_shape=[jax.ShapeDtypeStruct((ng, w, w), BF16), jax.ShapeDtypeStruct((ng, 2 * p, w), BF16),
                   jax.ShapeDtypeStruct((ng, w, 2 * p), BF16), jax.ShapeDtypeStruct((ng, 1, 2 * p), F32),
                   jax.ShapeDtypeStruct((ng, 1, 2 * p), F32)],
        compiler_params=_params("arbitrary"),
        name=f"s5_prep_q{q}",
    )(ls, lrp, lip, lr2, li2, brt, bit, ccat, ca, cb)


def _s5_mix_body(q, nb, nph, x_ref, g_ref, wut_ref, tt_ref, wt_ref, zt_ref, a1_ref, a2_ref, s0_ref, dcol_ref,
                 wglut_ref, bglu_ref, wo_ref, p_ref, sfin_ref, ut, yt, carry):
    ng = tt_ref.shape[0]
    nkb = x_ref.shape[1]
    nk = nkb // nb
    d = x_ref.shape[2]
    ch = (q // nph) * nkb
    tile = pl.program_id(0)
    ph = pl.program_id(1)

    @pl.when(jnp.logical_and(tile == 0, ph == 0))
    def _():
        carry[...] = s0_ref[...]

    @pl.when(ph < nph)
    def _():
        h = _rms(x_ref[...].reshape(ch, d), g_ref[...]).astype(BF16)
        u = _dot_nt(wut_ref[...], h)
        for j in range(nph):
            @pl.when(ph == j)
            def _():
                ut[:, j * ch:(j + 1) * ch] = u

    @pl.when(ph == nph - 1)
    def _():
        gu = S5_GROUP_BATCH
        bdot = lambda a, b: lax.dot_general(a, b, (((2,), (1,)), ((0,), (0,))), preferred_element_type=F32)
        bdot_nt = lambda a, b: lax.dot_general(a, b, (((2,), (2,)), ((0,), (0,))), preferred_element_type=F32)

        def groups(i, _):
            g0 = pl.multiple_of(i * gu, gu)
            r0 = pl.multiple_of(i * (gu * S5_CH), gu * S5_CH)
            gsl = pl.ds(g0, gu)
            rows = ut[pl.ds(r0, gu * S5_CH), :].astype(BF16).reshape(gu, S5_CH, q * nkb)
            ugt = jnp.concatenate([rows[:, :, (q - 1 - j) * nkb:(q - j) * nkb] for j in range(q)], axis=1)
            y = bdot(tt_ref[gsl], ugt)
            vt = bdot(wt_ref[gsl], ugt)
            v = jnp.swapaxes(vt, 1, 2)
            v_sw = jnp.swapaxes(jnp.concatenate([vt[:, S5_STATE:], vt[:, :S5_STATE]], axis=1), 1, 2)
            a1 = a1_ref[gsl]
            a2 = a2_ref[gsl]
            s = carry[gsl]
            s_sw = pltpu.roll(s.reshape(gu * nb, 2 * S5_STATE), S5_STATE, 1).reshape(s.shape)
            prev = []
            for k in range(nk):
                prev.append(s)
                s, s_sw = (a1 * s + a2 * s_sw + v[:, k * nb:(k + 1) * nb, :],
                           a1 * s_sw - a2 * s + v_sw[:, k * nb:(k + 1) * nb, :])
            carry[gsl] = s
            sprev = jnp.concatenate(prev, axis=1).astype(BF16)
            y = y + bdot_nt(zt_ref[gsl], sprev)
            for t in range(q):
                yt[pl.ds(r0, gu * S5_CH), t * nkb:(t + 1) * nkb] = (
                    y[:, t * S5_CH:(t + 1) * S5_CH, :].reshape(gu * S5_CH, nkb))
            return 0

        lax.fori_loop(0, ng // gu, groups, 0)
        sfin_ref[...] = carry[...]

    @pl.when(ph >= nph)
    def _():
        for j in range(nph):
            @pl.when(ph == nph + j)
            def _():
                sl = slice(j * ch, (j + 1) * ch)
                gt = _gelu_tanh(yt[:, sl] + dcol_ref[...] * ut[:, sl])
                gate = _dot(wglut_ref[...], gt.astype(BF16)) + bglu_ref[...]
                y5 = (gt * _sigmoid(gate)).astype(BF16)
                p_ref[...] = _dot_tn(y5, wo_ref[...]).reshape(p_ref.shape)


def _s5_mix(q, nb, xp, ln_g, wut, tt, wt, zt, a1, a2, s0, dcol, wglut, bglu, wo):
    _, nlt, d = xp.shape
    nkb = min(LANES, nlt)
    ntile = nlt // nkb
    tok = q * nkb
    nph = max(1, tok // TOKEN_TILE)
    qs = q // nph
    ng = tt.shape[0]
    dm = wo.shape[1]
    body = functools.partial(_s5_mix_body, q, nb, nph)
    x_spec = pl.BlockSpec((qs, nkb, d), lambda i, j: (jnp.minimum(j, nph - 1), i, 0))
    p_spec = pl.BlockSpec((qs, nkb, dm), lambda i, j: (jnp.maximum(j - nph, 0), i, 0))
    return pl.pallas_call(
        body,
        grid=(ntile, 2 * nph),
        in_specs=[x_spec, _const_spec(ln_g.shape), _const_spec(wut.shape), _const_spec(tt.shape),
                  _const_spec(wt.shape), _const_spec(zt.shape), _const_spec(a1.shape), _const_spec(a2.shape),
                  _const_spec(s0.shape), _const_spec(dcol.shape), _const_spec(wglut.shape),
                  _const_spec(bglu.shape), _const_spec(wo.shape)],
        out_specs=[p_spec, pl.BlockSpec(s0.shape, lambda i, j: (0, 0, 0))],
        out_shape=[jax.ShapeDtypeStruct((q, nlt, dm), F32), jax.ShapeDtypeStruct(s0.shape, F32)],
        scratch_shapes=[pltpu.VMEM((d, tok), F32), pltpu.VMEM((d, tok), F32), pltpu.VMEM(s0.shape, F32)],
        compiler_params=_params("arbitrary", "arbitrary"),
        name=f"s5_mix_q{q}",
    )(xp, ln_g, wut, tt, wt, zt, a1, a2, s0, dcol, wglut, bglu, wo)


def _kv_body(m_ref, wk_ref, wv_ref, k_ref, v_ref):
    m = m_ref[...].astype(BF16)
    k_ref[...] = _dot(m, wk_ref[...])
    v_ref[...] = _dot(m, wv_ref[...])


def _kv_proj(mem2d, wk, wv):
    t, d = mem2d.shape
    tm = min(TOKEN_TILE, t)
    row = pl.BlockSpec((tm, d), lambda i: (i, 0))
    return pl.pallas_call(
        _kv_body, grid=(t // tm,),
        in_specs=[row, _const_spec(wk.shape), _const_spec(wv.shape)],
        out_specs=[row, row],
        out_shape=[jax.ShapeDtypeStruct((t, d), F32)] * 2,
        compiler_params=_params("arbitrary"),
        name="kv_proj",
    )(mem2d, wk, wv)


def _attn_prompt_body(x_ref, ys_ref, p_ref, wo1_ref, g2_ref, wq_ref, k_ref, v_ref, wxo_ref, o_ref):
    d = x_ref.shape[-1]
    hd = d // X_HEADS
    x1 = x_ref[...] + _dot(ys_ref[...], wo1_ref[...]) + p_ref[...]
    hq = _rms(x1, g2_ref[...]).astype(BF16)
    qv = _dot(hq, wq_ref[...]).astype(BF16)
    kb = k_ref[...].astype(BF16)
    vb = v_ref[...].astype(BF16)
    outs = []
    for h in range(X_HEADS):
        sl = slice(h * hd, (h + 1) * hd)
        s = _dot_nt(qv[:, sl], kb[:, sl]) * (hd ** -0.5)
        e = jnp.exp(s - jnp.max(s, axis=-1, keepdims=True))
        p = (e / jnp.sum(e, axis=-1, keepdims=True)).astype(BF16)
        outs.append(_dot(p, vb[:, sl]))
    o = jnp.concatenate(outs, axis=1).astype(BF16)
    o_ref[...] = x1 + _dot(o, wxo_ref[...])


def _attn_prompt(x, ys, pm, wo1, g2, wq, mk, mv, wxo):
    b, l, d = x.shape
    nm = mk.shape[1]
    tm = min(TOKEN_TILE, l)
    row = pl.BlockSpec((None, tm, d), lambda i, j: (i, j, 0))
    kv = pl.BlockSpec((None, nm, d), lambda i, j: (i, 0, 0))
    return pl.pallas_call(
        _attn_prompt_body, grid=(b, l // tm),
        in_specs=[row, row, row, _const_spec(wo1.shape), _const_spec(g2.shape), _const_spec(wq.shape), kv, kv,
                  _const_spec(wxo.shape)],
        out_specs=row,
        out_shape=jax.ShapeDtypeStruct((b, l, d), F32),
        compiler_params=_params("arbitrary", "arbitrary"),
        name="attn_prompt",
    )(x, ys, pm, wo1, g2, wq, mk, mv, wxo)


def _merge_q_body(x_ref, ys_ref, p_ref, wo1_ref, g2_ref, wq_ref, x1_ref, q_ref):
    x1 = x_ref[...] + _dot(ys_ref[...], wo1_ref[...]) + p_ref[...]
    x1_ref[...] = x1
    q_ref[...] = _dot(_rms(x1, g2_ref[...]).astype(BF16), wq_ref[...])


def _merge_q(x2d, ys, pm, wo1, g2, wq):
    t, d = x2d.shape
    return pl.pallas_call(
        _merge_q_body,
        out_shape=[jax.ShapeDtypeStruct((t, d), F32)] * 2,
        compiler_params=pltpu.CompilerParams(vmem_limit_bytes=VMEM_LIMIT),
        name="merge_q",
    )(x2d, ys, pm, wo1, g2, wq)


def _tile_view(kv):
    b, m, nh, hd = kv.shape
    nt = hd // LANES
    return kv.reshape(b, m, nh, nt, LANES).transpose(0, 1, 3, 2, 4).reshape(b, m, nt * nh, LANES)


def _tile_perm_cols(w):
    d = w.shape[0]
    hd = w.shape[1] // X_HEADS
    return w.reshape(d, X_HEADS, hd // LANES, LANES).transpose(0, 2, 1, 3).reshape(d, X_HEADS * hd)


def _attn_step_body(q_ref, k_ref, v_ref, o_ref):
    bt, nm, rows, _ = k_ref.shape
    hd = rows * LANES // X_HEADS
    for b in range(bt):
        r = jnp.sum(k_ref[b] * q_ref[b], axis=-1, keepdims=True)
        s = (r + pltpu.roll(r, X_HEADS, 1)) * (hd ** -0.5)
        e = jnp.exp(s - jnp.max(s, axis=0, keepdims=True))
        p = e / jnp.sum(e, axis=0, keepdims=True)
        o_ref[b] = jnp.sum(p * v_ref[b], axis=0)


def _attn_step(q8, k8, v8):
    nbt, rows, _ = q8.shape
    nm = k8.shape[1]
    assert rows == 2 * X_HEADS, "score assembly assumes two 128-lane tiles per head"
    bt = 4
    qs = pl.BlockSpec((bt, rows, LANES), lambda i: (i, 0, 0))
    kv = pl.BlockSpec((bt, nm, rows, LANES), lambda i: (i, 0, 0, 0))
    return pl.pallas_call(
        _attn_step_body, grid=(nbt // bt,),
        in_specs=[qs, kv, kv], out_specs=qs,
        out_shape=jax.ShapeDtypeStruct((nbt, rows, LANES), F32),
        compiler_params=_params("arbitrary"),
        name="attn_step",
    )(q8, k8, v8)


def _ffn_body(has_o, final, *refs):
    if has_o:
        x_ref, o_in_ref, wxo_ref, g3_ref, wg_ref, wu_ref, wd_ref, gf_ref, y_ref = refs
        x2 = x_ref[...] + _dot(o_in_ref[...].astype(BF16), wxo_ref[...])
    else:
        x_ref, g3_ref, wg_ref, wu_ref, wd_ref, gf_ref, y_ref = refs
        x2 = x_ref[...]
    hf = _rms(x2, g3_ref[...]).astype(BF16)
    act = (_silu(_dot(hf, wg_ref[...])) * _dot(hf, wu_ref[...])).astype(BF16)
    x3 = x2 + _dot(act, wd_ref[...])
    y_ref[...] = _rms(x3, gf_ref[...]) if final else x3


def _ffn(x2d, g3, wg, wu, wd, gf, final, o_in=None, wxo=None):
    t, d = x2d.shape
    tm = min(TOKEN_TILE, t)
    row = pl.BlockSpec((tm, d), lambda i: (i, 0))
    has_o = o_in is not None
    ins = [x2d] + ([o_in, wxo] if has_o else []) + [g3, wg, wu, wd, gf]
    specs = [row] + ([row, _const_spec(wxo.shape)] if has_o else []) + [
        _const_spec(g3.shape), _const_spec(wg.shape), _const_spec(wu.shape), _const_spec(wd.shape),
        _const_spec(gf.shape)]
    return pl.pallas_call(
        functools.partial(_ffn_body, has_o, final), grid=(t // tm,),
        in_specs=specs, out_specs=row,
        out_shape=jax.ShapeDtypeStruct((t, d), F32),
        compiler_params=_params("arbitrary"),
        name="ffn_o" if has_o else "ffn",
    )(*ins)


def _layer_weights(i, ln1_g, w_in, conv_w, conv_b, dt_bias, a_log, d_ssd, gn_g, lam_re, lam_im, log_step,
                   b_re, b_im, c_re, c_im, d_s5, w_glu, b_glu, w_out, ln2_g, w_xq, w_xk, w_xv, w_xo, ln3_g,
                   w_gate, w_up, w_down):
    d = w_in.shape[1]
    n_heads = dt_bias.shape[1]
    dssd = n_heads * SSD_HEAD_DIM
    conv_dim = conv_w.shape[2]
    o1, o2, o3 = dssd, dssd + conv_dim, dssd + conv_dim + n_heads
    win = w_in[i]
    w = {}
    w["ln1"] = ln1_g[i].reshape(1, d)
    w["wz"] = win[:, :o1].astype(BF16)
    w["wx"] = win[:, o1:o2].astype(BF16)
    w["wd"] = jnp.pad(win[:, o2:o3], ((0, 0), (0, LANES - n_heads))).astype(BF16)
    w["wut"] = win[:, o3:].T.astype(BF16)
    w["cw"] = conv_w[i]
    w["cb"] = conv_b[i].reshape(1, conv_dim)
    w["dtb"] = jnp.pad(dt_bias[i], (0, LANES - n_heads)).reshape(1, LANES)
    w["alog"] = jnp.pad(a_log[i], (0, LANES - n_heads)).reshape(1, LANES)
    w["dexp"] = jnp.repeat(d_ssd[i], SSD_HEAD_DIM).reshape(1, dssd)
    w["gn"] = gn_g[i].reshape(1, dssd)
    w["s5"] = (lam_re[i], lam_im[i], log_step[i], b_re[i], b_im[i], c_re[i], c_im[i])
    ds5 = d_s5.shape[1]
    w["dcol"] = d_s5[i].reshape(ds5, 1)
    w["wglut"] = w_glu[i].T.astype(BF16)
    w["bglu"] = b_glu[i].reshape(ds5, 1)
    w["wo1"] = w_out[i, :dssd].astype(BF16)
    w["wo2"] = w_out[i, dssd:].astype(BF16)
    w["ln2"] = ln2_g[i].reshape(1, d)
    w["wq"] = w_xq[i].astype(BF16)
    w["wk"] = w_xk[i].astype(BF16)
    w["wv"] = w_xv[i].astype(BF16)
    w["wxo"] = w_xo[i].astype(BF16)
    w["wq_t"] = _tile_perm_cols(w_xq[i]).astype(BF16)
    w["wxo_t"] = _tile_perm_cols(w_xo[i].T).T.astype(BF16)
    w["ln3"] = ln3_g[i].reshape(1, d)
    w["wg"] = w_gate[i].astype(BF16)
    w["wu"] = w_up[i].astype(BF16)
    w["wdn"] = w_down[i].astype(BF16)
    return w


def _state_to_rows(s_re, s_im):
    return jnp.concatenate([s_re, s_im], axis=-1).transpose(1, 0, 2)


def _rows_to_state(s):
    p = s.shape[-1] // 2
    st = s.transpose(1, 0, 2)
    return st[..., :p], st[..., p:]


def kernel(x_prompt, x_sample, mem_prompt, state_conv, state_ssm, state_s5_re, state_s5_im, cache_mem_k, cache_mem_v, ln1_g, w_in, conv_w, conv_b, dt_bias, a_log, d_ssd, gn_g, lam_re, lam_im, log_step, b_re, b_im, c_re, c_im, d_s5, w_glu, b_glu, w_out, ln2_g, w_xq, w_xk, w_xv, w_xo, ln3_g, w_gate, w_up, w_down, final_g):
    bp, seq, d = x_prompt.shape
    bs = x_sample.shape[0]
    depth = w_in.shape[0]
    nm = mem_prompt.shape[1]
    ng, ns = lam_re.shape[1], lam_re.shape[2]
    gf = final_g.reshape(1, d)
    hp = x_prompt
    hs = x_sample.reshape(bs, d)
    outs = {k: [] for k in ("conv_p", "ssm_p", "re_p", "im_p", "mk_p", "mv_p", "conv_s", "ssm_s", "re_s", "im_s")}
    yp = ys_out = None
    for i in range(depth):
        w = _layer_weights(i, ln1_g, w_in, conv_w, conv_b, dt_bias, a_log, d_ssd, gn_g, lam_re, lam_im, log_step,
                           b_re, b_im, c_re, c_im, d_s5, w_glu, b_glu, w_out, ln2_g, w_xq, w_xk, w_xv, w_xo, ln3_g,
                           w_gate, w_up, w_down)
        last = i == depth - 1
        mk, mv = _kv_proj(mem_prompt.reshape(bp * nm, d), w["wk"], w["wv"])
        mk = mk.reshape(bp, nm, d)
        mv = mv.reshape(bp, nm, d)
        z, xbc, dtr = _inproj(hp.reshape(bp * seq, d), w["ln1"], w["wz"], w["wx"], w["wd"])
        y_ssd, conv_p, ssm_p = _ssd_chunk(z.reshape(bp, seq, -1), xbc.reshape(bp, seq, -1), dtr.reshape(bp, seq, -1),
                                          w["cw"], w["cb"], w["dtb"], w["alog"], w["dexp"], w["gn"])
        q = S5_Q
        tt, wt, zt, a1, a2 = _s5_prep(q, *w["s5"])
        xperm = hp.reshape(bp, seq // q, q, d).transpose(2, 1, 0, 3).reshape(q, (seq // q) * bp, d)
        s0 = jnp.zeros((ng, bp, 2 * ns), F32)
        pperm, sfin = _s5_mix(q, bp, xperm, w["ln1"], w["wut"], tt, wt, zt, a1, a2, s0, w["dcol"], w["wglut"],
                              w["bglu"], w["wo2"])
        pm = pperm.reshape(q, seq // q, bp, d).transpose(2, 1, 0, 3).reshape(bp, seq, d)
        re_p, im_p = _rows_to_state(sfin)
        x2 = _attn_prompt(hp, y_ssd, pm, w["wo1"], w["ln2"], w["wq"], mk, mv, w["wxo"])
        hp = _ffn(x2.reshape(bp * seq, d), w["ln3"], w["wg"], w["wu"], w["wdn"], gf, last).reshape(bp, seq, d)
        zs, xbcs, dts = _inproj(hs, w["ln1"], w["wz"], w["wx"], w["wd"])
        ys_ssd, conv_s_t, ssm_s = _ssd_step(zs, xbcs, dts, state_conv[i].transpose(1, 0, 2), state_ssm[i],
                                            w["cw"], w["cb"], w["dtb"], w["alog"], w["dexp"], w["gn"])
        tt1, wt1, zt1, a11, a21 = _s5_prep(1, *w["s5"])
        s0s = _state_to_rows(state_s5_re[i], state_s5_im[i])
        ps, sfin_s = _s5_mix(1, bs, hs.reshape(1, bs, d), w["ln1"], w["wut"], tt1, wt1, zt1, a11, a21, s0s,
                             w["dcol"], w["wglut"], w["bglu"], w["wo2"])
        re_s, im_s = _rows_to_state(sfin_s)
        x1s, qs = _merge_q(hs, ys_ssd, ps.reshape(bs, d), w["wo1"], w["ln2"], w["wq_t"])
        o_s = _attn_step(qs.reshape(bs, d // LANES, LANES), _tile_view(cache_mem_k[i]), _tile_view(cache_mem_v[i]))
        hs = _ffn(x1s, w["ln3"], w["wg"], w["wu"], w["wdn"], gf, last, o_in=o_s.reshape(bs, d), wxo=w["wxo_t"])
        for k, v in (("conv_p", conv_p), ("ssm_p", ssm_p), ("re_p", re_p), ("im_p", im_p),
                     ("mk_p", mk.reshape(bp, nm, X_HEADS, d // X_HEADS)),
                     ("mv_p", mv.reshape(bp, nm, X_HEADS, d // X_HEADS)),
                     ("conv_s", conv_s_t.transpose(1, 0, 2)), ("ssm_s", ssm_s), ("re_s", re_s), ("im_s", im_s)):
            outs[k].append(v)
    st = lambda k: jnp.stack(outs[k])
    return (hp, hs.reshape(bs, 1, d), st("conv_p"), st("ssm_p"), st("re_p"), st("im_p"), st("mk_p"), st("mv_p"),
            st("conv_s"), st("ssm_s"), st("re_s"), st("im_s"))
```

```python
import functools
import math

import jax
import jax.numpy as jnp
from jax import lax
from jax.experimental import pallas as pl
from jax.experimental.pallas import tpu as pltpu

F32 = jnp.float32
BF16 = jnp.bfloat16
EPS = 1e-6

LANES = 128
VMEM_LIMIT = 56 * 1024 * 1024

SSD_HEAD_DIM = 64
SSD_STATE = 128
SSD_GROUPS = 2
SSD_CHUNK = 128
CONV_W = 4
S5_CH = 16
S5_STATE = 64
S5_Q = 16
S5_GROUP_BATCH = 16
X_HEADS = 4
TOKEN_TILE = 512
PROJ_PIECE = 512


def _const_spec(shape):
    nd = len(shape)
    return pl.BlockSpec(shape, lambda *_: (0,) * nd, pipeline_mode=pl.Buffered(1))


def _params(*sem):
    return pltpu.CompilerParams(dimension_semantics=sem, vmem_limit_bytes=VMEM_LIMIT)


def _rms(x, g):
    return x * lax.rsqrt(jnp.mean(x * x, axis=-1, keepdims=True) + EPS) * g


def _sigmoid(x):
    return 1.0 / (1.0 + jnp.exp(-x))


def _silu(x):
    return x * _sigmoid(x)


def _gelu_tanh(x):
    return 0.5 * x * (1.0 + jnp.tanh(math.sqrt(2.0 / math.pi) * (x + 0.044715 * (x * x * x))))


def _softplus(x):
    return jnp.maximum(x, 0.0) + jnp.log1p(jnp.exp(-jnp.abs(x)))


def _dot(a, b):
    return jnp.dot(a, b, preferred_element_type=F32)


def _dot_nt(a, b):
    return lax.dot_general(a, b, (((1,), (1,)), ((), ())), preferred_element_type=F32)


def _dot_tn(a, b):
    return lax.dot_general(a, b, (((0,), (0,)), ((), ())), preferred_element_type=F32)


def _inproj_body(x_ref, g_ref, wz_ref, wx_ref, wd_ref, z_ref, xbc_ref, dt_ref):
    h = _rms(x_ref[...], g_ref[...]).astype(BF16)
    z_ref[...] = _dot(h, wz_ref[...])
    xbc_ref[...] = _dot(h, wx_ref[...])
    dt_ref[...] = _dot(h, wd_ref[...])


def _inproj(x2d, ln_g, wz, wx, wd):
    t, d = x2d.shape
    tm = min(TOKEN_TILE, t)
    nz, nx, nd = wz.shape[1], wx.shape[1], wd.shape[1]
    row = lambda n: pl.BlockSpec((tm, n), lambda i: (i, 0))
    return pl.pallas_call(
        _inproj_body,
        grid=(t // tm,),
        in_specs=[row(d), _const_spec((1, d)), _const_spec(wz.shape), _const_spec(wx.shape), _const_spec(wd.shape)],
        out_specs=[row(nz), row(nx), row(nd)],
        out_shape=[jax.ShapeDtypeStruct((t, nz), F32), jax.ShapeDtypeStruct((t, nx), F32),
                   jax.ShapeDtypeStruct((t, nd), F32)],
        compiler_params=_params("arbitrary"),
        name="inproj",
    )(x2d, ln_g, wz, wx, wd)


def _expand_heads(v, n_heads):
    rows = v.shape[0]
    lane = lax.broadcasted_iota(jnp.int32, (rows, LANES), 1)
    pieces = [jnp.where(lane < SSD_HEAD_DIM, v[:, 2 * j:2 * j + 1], v[:, 2 * j + 1:2 * j + 2])
              for j in range(n_heads // 2)]
    return jnp.concatenate(pieces, axis=1)


def _ssd_chunk_math(z, x, dt_raw, last_chunk, cw_ref, cb_ref, dtb_ref, alog_ref, dexp_ref, gn_ref, e2_ref,
                    y_ref, conv_ref, ssm_ref, xpad, state, side_jobs=()):
    jobs = list(side_jobs)

    def side(n=1):
        for _ in range(n):
            if jobs:
                jobs.pop(0)()

    q = SSD_CHUNK
    d_ssd = z.shape[-1]
    n_heads = d_ssd // SSD_HEAD_DIM
    hpg = n_heads // SSD_GROUPS
    gw = hpg * SSD_HEAD_DIM
    nct = xpad.shape[0]
    xt = d_ssd // LANES

    for t in range(nct):
        xpad[t, 8:8 + q, :] = x[:, t * LANES:(t + 1) * LANES]
    cw = cw_ref[...]
    conv = (cw[3] * xpad[:, 8:8 + q, :] + cw[2] * xpad[:, 7:7 + q, :] + cw[1] * xpad[:, 6:6 + q, :]
            + cw[0] * xpad[:, 5:5 + q, :] + cb_ref[...])
    xpad[:, 0:8, :] = xpad[:, q:q + 8, :]
    xact = _silu(conv)
    xs = jnp.concatenate([xact[t] for t in range(xt)], axis=1)
    bm = xact[xt:xt + SSD_GROUPS].astype(BF16)
    cm = xact[xt + SSD_GROUPS:].astype(BF16)
    side()

    dt = _softplus(dt_raw + dtb_ref[...])
    a = -jnp.exp(alog_ref[...])
    row = lax.broadcasted_iota(jnp.int32, (q, q), 0)
    col = lax.broadcasted_iota(jnp.int32, (q, q), 1)
    causal = row >= col
    tri = jnp.where(causal, 1.0, 0.0).astype(F32)
    acum = jnp.dot(tri, dt * a, precision=lax.Precision.HIGHEST, preferred_element_type=F32)
    acum_t = acum.T
    last = acum[q - 1:q, :]

    def expand(v):
        hi = v.astype(BF16)
        lo = (v - hi.astype(F32)).astype(BF16)
        return _dot(jnp.concatenate([hi, lo], axis=1), e2_ref[...])

    dt_e = expand(dt)
    ea_e = expand(jnp.exp(acum))
    dend_e = expand(jnp.exp(last - acum))

    dtx = xs * dt_e
    dtx_b = dtx.astype(BF16)
    xdec_b = (dtx * dend_e).astype(BF16)
    side()

    lane = lax.broadcasted_iota(jnp.int32, (q, LANES), 1)
    zero_b = jnp.zeros((q, LANES), BF16)
    y_tiles = []
    for g in range(SSD_GROUPS):
        bg = bm[g]
        cg = cm[g]
        cb = _dot_nt(cg, bg)
        gs = slice(g * gw, (g + 1) * gw)
        st_g = state[:, gs]
        y_off = _dot(cg, st_g.astype(BF16)) * ea_e[:, gs]
        for pr in range(hpg // 2):
            tile = g * (hpg // 2) + pr
            ms = []
            for h in (2 * tile, 2 * tile + 1):
                seg = acum[:, h:h + 1] - acum_t[h:h + 1, :]
                lmat = jnp.exp(jnp.where(causal, seg, -jnp.inf))
                ms.append((cb * lmat).astype(BF16))
            dtile = dtx_b[:, tile * LANES:(tile + 1) * LANES]
            rhs = jnp.concatenate([jnp.where(lane < SSD_HEAD_DIM, dtile, zero_b),
                                   jnp.where(lane >= SSD_HEAD_DIM, dtile, zero_b)], axis=0)
            y_tiles.append(_dot(jnp.concatenate(ms, axis=1), rhs)
                           + y_off[:, pr * LANES:(pr + 1) * LANES])
            if pr % 2 == 1:
                side()
        new = _dot_tn(bg, xdec_b[:, gs])
        state[:, gs] = st_g * ea_e[q - 1:q, gs] + new
    y = jnp.concatenate(y_tiles, axis=1) + dexp_ref[...] * xs
    y = y * _silu(z)
    ms = jnp.mean(y * y, axis=-1, keepdims=True)
    side(len(jobs))
    y_ref[...] = (y * lax.rsqrt(ms + EPS) * gn_ref[...]).astype(y_ref.dtype)

    @pl.when(last_chunk)
    def _():
        conv_ref[...] = x[q - (CONV_W - 1):q, :]
        ssm_ref[...] = state[...].T.reshape(ssm_ref.shape)


def _ssd_prompt_body(x_ref, g_ref, wz_ref, wx_ref, wd_ref, cw_ref, cb_ref, dtb_ref, alog_ref, dexp_ref, gn_ref,
                     e2_ref, y_ref, conv_ref, ssm_ref, za, xa, da, zb, xb, db, xpad, state):
    i = pl.program_id(0)
    j = pl.program_id(1)
    nc = pl.num_programs(1) - 1

    @pl.when(jnp.logical_and(i == 0, j == 0))
    def _():
        for r in (zb, xb, db):
            r[...] = jnp.zeros(r.shape, F32)

    @pl.when(j <= 1)
    def _():
        xpad[:, 0:8, :] = jnp.zeros((xpad.shape[0], 8, LANES), F32)
        state[...] = jnp.zeros(state.shape, F32)

    def step(wr, rd):
        xin = x_ref[...]
        msx = jnp.mean(xin * xin, axis=-1, keepdims=True)
        hbox = []

        def proj_piece(dst, w_ref, lo, hi):
            def run():
                if not hbox:
                    hbox.append((xin * lax.rsqrt(msx + EPS) * g_ref[...]).astype(BF16))
                dst[:, lo:hi] = _dot(hbox[0], w_ref[:, lo:hi])
            return run

        pieces = []
        for dst, w_ref in ((wr[0], wz_ref), (wr[2], wd_ref), (wr[1], wx_ref)):
            n = w_ref.shape[1]
            pieces += [proj_piece(dst, w_ref, lo, min(lo + PROJ_PIECE, n)) for lo in range(0, n, PROJ_PIECE)]
        _ssd_chunk_math(rd[0][...], rd[1][...], rd[2][...], j == nc, cw_ref, cb_ref, dtb_ref, alog_ref, dexp_ref,
                        gn_ref, e2_ref, y_ref, conv_ref, ssm_ref, xpad, state, pieces)

    @pl.when(j % 2 == 0)
    def _():
        step((za, xa, da), (zb, xb, db))

    @pl.when(j % 2 == 1)
    def _():
        step((zb, xb, db), (za, xa, da))


def _ssd_prompt(x, ln_g, wz, wx, wd, cw, cb, dtb, alog, dexp, gn):
    b, l, d = x.shape
    d_ssd, conv_dim = wz.shape[1], wx.shape[1]
    n_heads = d_ssd // SSD_HEAD_DIM
    q = SSD_CHUNK
    nc = l // q
    assert SSD_STATE == LANES and conv_dim == d_ssd + 2 * SSD_GROUPS * SSD_STATE
    nct = conv_dim // LANES
    cw4 = cw.reshape(CONV_W, nct, 1, LANES)
    cb3 = cb.reshape(nct, 1, LANES)
    e2 = (jnp.arange(2 * LANES)[:, None] % LANES == jnp.arange(d_ssd)[None, :] // SSD_HEAD_DIM).astype(BF16)
    slot = lambda n: pltpu.VMEM((q, n), F32)
    return pl.pallas_call(
        _ssd_prompt_body,
        grid=(b, nc + 1),
        in_specs=[pl.BlockSpec((None, q, d), lambda i, j: (i, jnp.minimum(j, nc - 1), 0)),
                  _const_spec(ln_g.shape), _const_spec(wz.shape), _const_spec(wx.shape), _const_spec(wd.shape),
                  _const_spec(cw4.shape), _const_spec(cb3.shape), _const_spec(dtb.shape), _const_spec(alog.shape),
                  _const_spec(dexp.shape), _const_spec(gn.shape), _const_spec(e2.shape)],
        out_specs=[pl.BlockSpec((None, q, d_ssd), lambda i, j: (i, jnp.maximum(j - 1, 0), 0)),
                   pl.BlockSpec((None, CONV_W - 1, conv_dim), lambda i, j: (i, 0, 0)),
                   pl.BlockSpec((None, n_heads, SSD_HEAD_DIM, SSD_STATE), lambda i, j: (i, 0, 0, 0))],
        out_shape=[jax.ShapeDtypeStruct((b, l, d_ssd), BF16),
                   jax.ShapeDtypeStruct((b, CONV_W - 1, conv_dim), F32),
                   jax.ShapeDtypeStruct((b, n_heads, SSD_HEAD_DIM, SSD_STATE), F32)],
        scratch_shapes=[slot(d_ssd), slot(conv_dim), slot(LANES), slot(d_ssd), slot(conv_dim), slot(LANES),
                        pltpu.VMEM((nct, q + 8, LANES), F32), pltpu.VMEM((SSD_STATE, d_ssd), F32)],
        compiler_params=_params("arbitrary", "arbitrary"),
        name="ssd_prompt",
    )(x, ln_g, wz, wx, wd, cw4, cb3, dtb, alog, dexp, gn, e2)


def _ssd_step_body(z_ref, xbc_ref, dt_ref, cs_ref, st_ref, cw_ref, cb_ref, dtb_ref, alog_ref, dexp_ref, gn_ref,
                   y_ref, conv_ref, ssm_ref):
    bt, d_ssd = z_ref.shape
    n_heads = d_ssd // SSD_HEAD_DIM
    hpg = n_heads // SSD_GROUPS
    x = xbc_ref[...]
    cw = cw_ref[...]
    conv = cw[0:1] * cs_ref[0] + cw[1:2] * cs_ref[1] + cw[2:3] * cs_ref[2] + cw[3:4] * x + cb_ref[...]
    conv_ref[0] = cs_ref[1]
    conv_ref[1] = cs_ref[2]
    conv_ref[2] = x
    xact = _silu(conv)
    xs = xact[:, :d_ssd]
    bm = xact[:, d_ssd:d_ssd + SSD_GROUPS * SSD_STATE]
    cm = xact[:, d_ssd + SSD_GROUPS * SSD_STATE:]
    dt = _softplus(dt_ref[...] + dtb_ref[...])
    da = jnp.exp(dt * (-jnp.exp(alog_ref[...])))
    dtx = xs * _expand_heads(dt, n_heads)
    pad = jnp.zeros((LANES - bt, d_ssd), F32)
    to_cols = lambda v: jnp.concatenate([v, pad], axis=0).T
    dtx_t = to_cols(dtx)
    da_t = to_cols(_expand_heads(da, n_heads))
    lane = lax.broadcasted_iota(jnp.int32, (d_ssd, LANES), 1)
    gw = hpg * SSD_HEAD_DIM
    rows_of = lambda v, b: jnp.concatenate(
        [jnp.broadcast_to(v[b:b + 1, g * SSD_STATE:(g + 1) * SSD_STATE], (gw, SSD_STATE)) for g in range(SSD_GROUPS)],
        axis=0)
    y_t = jnp.zeros((d_ssd, LANES), F32)
    for b in range(bt):
        s0 = st_ref[b].reshape(d_ssd, SSD_STATE)
        s_new = s0 * da_t[:, b:b + 1] + dtx_t[:, b:b + 1] * rows_of(bm, b)
        ssm_ref[b] = s_new.reshape(n_heads, SSD_HEAD_DIM, SSD_STATE)
        ycol = jnp.sum(s_new * rows_of(cm, b), axis=1, keepdims=True)
        y_t = jnp.where(lane == b, ycol, y_t)
    y = y_t.T[:bt, :] + dexp_ref[...] * xs
    y = y * _silu(z_ref[...])
    y_ref[...] = _rms(y, gn_ref[...]).astype(y_ref.dtype)


def _ssd_step(z, xbc, dt, conv_state_t, ssm_state, cw, cb, dtb, alog, dexp, gn):
    nb, d_ssd = z.shape
    conv_dim = xbc.shape[-1]
    n_heads = d_ssd // SSD_HEAD_DIM
    bt = 8
    row = lambda n: pl.BlockSpec((bt, n), lambda i: (i, 0))
    cs_spec = pl.BlockSpec((CONV_W - 1, bt, conv_dim), lambda i: (0, i, 0))
    st_spec = pl.BlockSpec((bt, n_heads, SSD_HEAD_DIM, SSD_STATE), lambda i: (i, 0, 0, 0))
    return pl.pallas_call(
        _ssd_step_body,
        grid=(nb // bt,),
        in_specs=[row(d_ssd), row(conv_dim), row(LANES), cs_spec, st_spec, _const_spec(cw.shape),
                  _const_spec(cb.shape), _const_spec(dtb.shape), _const_spec(alog.shape), _const_spec(dexp.shape),
                  _const_spec(gn.shape)],
        out_specs=[row(d_ssd), cs_spec, st_spec],
        out_shape=[jax.ShapeDtypeStruct((nb, d_ssd), BF16),
                   jax.ShapeDtypeStruct((CONV_W - 1, nb, conv_dim), F32),
                   jax.ShapeDtypeStruct(ssm_state.shape, F32)],
        compiler_params=_params("arbitrary"),
        name="ssd_step",
    )(z, xbc, dt, conv_state_t, ssm_state, cw, cb, dtb, alog, dexp, gn)


def _split_bf16(x):
    hi = x.astype(BF16)
    lo = (x - hi.astype(F32)).astype(BF16)
    return hi, lo


def _s5_prep_body(q, ls_ref, lrp_ref, lip_ref, lr2_ref, li2_ref, brt_ref, bit_ref, ccat_ref, ca_ref, cb_ref,
                  tt_ref, wt_ref, zt_ref, a1_ref, a2_ref):
    ng = ls_ref.shape[0]
    w = q * S5_CH
    step = jnp.exp(ls_ref[...])
    lr, li = lrp_ref[...], lip_ref[...]
    mag = jnp.exp(lr * step)
    ang = li * step
    lbr = mag * jnp.cos(ang)
    lbi = mag * jnp.sin(ang)
    den = lr * lr + li * li
    kr = ((lbr - 1.0) * lr + lbi * li) / den
    ki = (lbi * lr - (lbr - 1.0) * li) / den
    brt, bit = brt_ref[...], bit_ref[...]
    bbr = kr * brt - ki * bit
    bbi = kr * bit + ki * brt
    d = (lax.broadcasted_iota(jnp.int32, (1, 1, w), 2) // S5_CH).astype(F32)
    pm = jnp.exp(d * (lr * step))
    pa = d * ang
    pr = pm * jnp.cos(pa)
    pi = pm * jnp.sin(pa)
    ball = jnp.concatenate([pr * bbr - pi * bbi, pr * bbi + pi * bbr], axis=1)
    wt_ref[...] = ball.astype(BF16)

    lane3 = lax.broadcasted_iota(jnp.int32, (1, 1, 2 * S5_STATE), 2)
    first = lane3 < S5_STATE
    csign = jnp.where(first, ccat_ref[...], -ccat_ref[...])
    ch, cl = _split_bf16(csign)
    bh, bl = _split_bf16(ball)
    bdot = lambda x, y: lax.dot_general(x, y, (((2,), (1,)), ((0,), (0,))), preferred_element_type=F32)
    kall = bdot(ch, bh) + bdot(ch, bl) + bdot(cl, bh)
    k2 = kall.reshape(ng * S5_CH, w)
    lane2 = lax.broadcasted_iota(jnp.int32, (ng * S5_CH, w), 1)
    for t in range(q):
        sh = (q - 1 - t) * S5_CH
        r = pltpu.roll(k2, sh, 1) if sh else k2
        r = jnp.where(lane2 >= sh, r, 0.0)
        tt_ref[:, t * S5_CH:(t + 1) * S5_CH, :] = r.reshape(ng, S5_CH, w).astype(BF16)

    lr2, li2 = lr2_ref[...], li2_ref[...]
    t1 = (lax.broadcasted_iota(jnp.int32, (1, q, 1), 1) + 1).astype(F32)
    zm = jnp.exp(t1 * (lr2 * step))
    za = t1 * (li2 * step)
    zr = zm * jnp.cos(za)
    zi = zm * jnp.sin(za)
    ca, cb = ca_ref[...], cb_ref[...]
    for t in range(q):
        prt = zr[:, t:t + 1, :]
        pit = zi[:, t:t + 1, :]
        zt = jnp.where(first, ca * prt - cb * pit, -(ca * pit) - cb * prt)
        zt_ref[:, t * S5_CH:(t + 1) * S5_CH, :] = zt.astype(BF16)
    qf = float(q)
    mq = jnp.exp(qf * (lr2 * step))
    aq = qf * (li2 * step)
    ar = mq * jnp.cos(aq)
    ai = mq * jnp.sin(aq)
    a1_ref[...] = ar
    a2_ref[...] = jnp.where(first, -ai, ai)


def _s5_prep(q, lam_re, lam_im, log_step, b_re, b_im, c_re, c_im):
    ng, p = lam_re.shape
    w = q * S5_CH
    ls = log_step.reshape(ng, 1, 1)
    lrp = lam_re.reshape(ng, p, 1)
    lip = lam_im.reshape(ng, p, 1)
    lr2 = jnp.concatenate([lam_re, lam_re], axis=-1).reshape(ng, 1, 2 * p)
    li2 = jnp.concatenate([lam_im, lam_im], axis=-1).reshape(ng, 1, 2 * p)
    brt = jnp.tile(b_re, (1, 1, q))
    bit = jnp.tile(b_im, (1, 1, q))
    ccat = jnp.concatenate([c_re, c_im], axis=-1)
    ca = jnp.concatenate([c_re, c_re], axis=-1)
    cb = jnp.concatenate([c_im, c_im], axis=-1)
    gb = 8
    blk = lambda a, b: pl.BlockSpec((gb, a, b), lambda i: (i, 0, 0))
    return pl.pallas_call(
        functools.partial(_s5_prep_body, q),
        grid=(ng // gb,),
        in_specs=[blk(1, 1), blk(p, 1), blk(p, 1), blk(1, 2 * p), blk(1, 2 * p), blk(p, w), blk(p, w),
                  blk(S5_CH, 2 * p), blk(S5_CH, 2 * p), blk(S5_CH, 2 * p)],
        out_specs=[blk(w, w), blk(2 * p, w), blk(w, 2 * p), blk(1, 2 * p), blk(1, 2 * p)],
        out_shape=[jax.ShapeDtypeStruct((ng, w, w), BF16), jax.ShapeDtypeStruct((ng, 2 * p, w), BF16),
                   jax.ShapeDtypeStruct((ng, w, 2 * p), BF16), jax.ShapeDtypeStruct((ng, 1, 2 * p), F32),
                   jax.ShapeDtypeStruct((ng, 1, 2 * p), F32)],
        compiler_params=_params("arbitrary"),
        name=f"s5_prep_q{q}",
    )(ls, lrp, lip, lr2, li2, brt, bit, ccat, ca, cb)


def _s5_mix_body(q, nb, nph, x_ref, g_ref, wut_ref, tt_ref, wt_ref, zt_ref, a1_ref, a2_ref, s0_ref, dcol_ref,
                 wglut_ref, bglu_ref, wo_ref, p_ref, sfin_ref, ut, yt, carry):
    ng = tt_ref.shape[0]
    nkb = x_ref.shape[1]
    nk = nkb // nb
    d = x_ref.shape[2]
    ch = (q // nph) * nkb
    tile = pl.program_id(0)
    ph = pl.program_id(1)

    @pl.when(jnp.logical_and(tile == 0, ph == 0))
    def _():
        carry[...] = s0_ref[...]

    @pl.when(ph < nph)
    def _():
        h = _rms(x_ref[...].reshape(ch, d), g_ref[...]).astype(BF16)
        u = _dot_nt(wut_ref[...], h)
        for j in range(nph):
            @pl.when(ph == j)
            def _():
                ut[:, j * ch:(j + 1) * ch] = u

    @pl.when(ph == nph - 1)
    def _():
        gu = S5_GROUP_BATCH
        bdot = lambda a, b: lax.dot_general(a, b, (((2,), (1,)), ((0,), (0,))), preferred_element_type=F32)
        bdot_nt = lambda a, b: lax.dot_general(a, b, (((2,), (2,)), ((0,), (0,))), preferred_element_type=F32)

        def groups(i, _):
            g0 = pl.multiple_of(i * gu, gu)
            r0 = pl.multiple_of(i * (gu * S5_CH), gu * S5_CH)
            gsl = pl.ds(g0, gu)
            rows = ut[pl.ds(r0, gu * S5_CH), :].astype(BF16).reshape(gu, S5_CH, q * nkb)
            ugt = jnp.concatenate([rows[:, :, (q - 1 - j) * nkb:(q - j) * nkb] for j in range(q)], axis=1)
            y = bdot(tt_ref[gsl], ugt)
            vt = bdot(wt_ref[gsl], ugt)
            v = jnp.swapaxes(vt, 1, 2)
            v_sw = jnp.swapaxes(jnp.concatenate([vt[:, S5_STATE:], vt[:, :S5_STATE]], axis=1), 1, 2)
            a1 = a1_ref[gsl]
            a2 = a2_ref[gsl]
            s = carry[gsl]
            s_sw = pltpu.roll(s.reshape(gu * nb, 2 * S5_STATE), S5_STATE, 1).reshape(s.shape)
            prev = []
            for k in range(nk):
                prev.append(s)
                s, s_sw = (a1 * s + a2 * s_sw + v[:, k * nb:(k + 1) * nb, :],
                           a1 * s_sw - a2 * s + v_sw[:, k * nb:(k + 1) * nb, :])
            carry[gsl] = s
            sprev = jnp.concatenate(prev, axis=1).astype(BF16)
            y = y + bdot_nt(zt_ref[gsl], sprev)
            for t in range(q):
                yt[pl.ds(r0, gu * S5_CH), t * nkb:(t + 1) * nkb] = (
                    y[:, t * S5_CH:(t + 1) * S5_CH, :].reshape(gu * S5_CH, nkb))
            return 0

        lax.fori_loop(0, ng // gu, groups, 0)
        sfin_ref[...] = carry[...]

    @pl.when(ph >= nph)
    def _():
        for j in range(nph):
            @pl.when(ph == nph + j)
            def _():
                sl = slice(j * ch, (j + 1) * ch)
                gt = _gelu_tanh(yt[:, sl] + dcol_ref[...] * ut[:, sl])
                gate = _dot(wglut_ref[...], gt.astype(BF16)) + bglu_ref[...]
                y5 = (gt * _sigmoid(gate)).astype(BF16)
                p_ref[...] = _dot_tn(y5, wo_ref[...]).reshape(p_ref.shape)


def _s5_mix(q, nb, xp, ln_g, wut, tt, wt, zt, a1, a2, s0, dcol, wglut, bglu, wo):
    _, nlt, d = xp.shape
    nkb = min(LANES, nlt)
    ntile = nlt // nkb
    tok = q * nkb
    nph = max(1, tok // TOKEN_TILE)
    qs = q // nph
    ng = tt.shape[0]
    dm = wo.shape[1]
    body = functools.partial(_s5_mix_body, q, nb, nph)
    x_spec = pl.BlockSpec((qs, nkb, d), lambda i, j: (jnp.minimum(j, nph - 1), i, 0))
    p_spec = pl.BlockSpec((qs, nkb, dm), lambda i, j: (jnp.maximum(j - nph, 0), i, 0))
    return pl.pallas_call(
        body,
        grid=(ntile, 2 * nph),
        in_specs=[x_spec, _const_spec(ln_g.shape), _const_spec(wut.shape), _const_spec(tt.shape),
                  _const_spec(wt.shape), _const_spec(zt.shape), _const_spec(a1.shape), _const_spec(a2.shape),
                  _const_spec(s0.shape), _const_spec(dcol.shape), _const_spec(wglut.shape),
                  _const_spec(bglu.shape), _const_spec(wo.shape)],
        out_specs=[p_spec, pl.BlockSpec(s0.shape, lambda i, j: (0, 0, 0))],
        out_shape=[jax.ShapeDtypeStruct((q, nlt, dm), F32), jax.ShapeDtypeStruct(s0.shape, F32)],
        scratch_shapes=[pltpu.VMEM((d, tok), F32), pltpu.VMEM((d, tok), F32), pltpu.VMEM(s0.shape, F32)],
        compiler_params=_params("arbitrary", "arbitrary"),
        name=f"s5_mix_q{q}",
    )(xp, ln_g, wut, tt, wt, zt, a1, a2, s0, dcol, wglut, bglu, wo)


def _kv_body(m_ref, wk_ref, wv_ref, k_ref, v_ref):
    m = m_ref[...].astype(BF16)
    k_ref[...] = _dot(m, wk_ref[...])
    v_ref[...] = _dot(m, wv_ref[...])


def _kv_proj(mem2d, wk, wv):
    t, d = mem2d.shape
    tm = min(TOKEN_TILE, t)
    row = pl.BlockSpec((tm, d), lambda i: (i, 0))
    return pl.pallas_call(
        _kv_body, grid=(t // tm,),
        in_specs=[row, _const_spec(wk.shape), _const_spec(wv.shape)],
        out_specs=[row, row],
        out_shape=[jax.ShapeDtypeStruct((t, d), F32)] * 2,
        compiler_params=_params("arbitrary"),
        name="kv_proj",
    )(mem2d, wk, wv)


def _attn_prompt_body(x_ref, ys_ref, p_ref, wo1_ref, g2_ref, wq_ref, k_ref, v_ref, wxo_ref, o_ref):
    d = x_ref.shape[-1]
    hd = d // X_HEADS
    x1 = x_ref[...] + _dot(ys_ref[...], wo1_ref[...]) + p_ref[...]
    hq = _rms(x1, g2_ref[...]).astype(BF16)
    qv = _dot(hq, wq_ref[...]).astype(BF16)
    kb = k_ref[...].astype(BF16)
    vb = v_ref[...].astype(BF16)
    outs = []
    for h in range(X_HEADS):
        sl = slice(h * hd, (h + 1) * hd)
        s = _dot_nt(qv[:, sl], kb[:, sl]) * (hd ** -0.5)
        e = jnp.exp(s - jnp.max(s, axis=-1, keepdims=True))
        p = (e / jnp.sum(e, axis=-1, keepdims=True)).astype(BF16)
        outs.append(_dot(p, vb[:, sl]))
    o = jnp.concatenate(outs, axis=1).astype(BF16)
    o_ref[...] = x1 + _dot(o, wxo_ref[...])


def _attn_prompt(x, ys, pm, wo1, g2, wq, mk, mv, wxo):
    b, l, d = x.shape
    nm = mk.shape[1]
    tm = min(TOKEN_TILE, l)
    row = pl.BlockSpec((None, tm, d), lambda i, j: (i, j, 0))
    kv = pl.BlockSpec((None, nm, d), lambda i, j: (i, 0, 0))
    return pl.pallas_call(
        _attn_prompt_body, grid=(b, l // tm),
        in_specs=[row, row, row, _const_spec(wo1.shape), _const_spec(g2.shape), _const_spec(wq.shape), kv, kv,
                  _const_spec(wxo.shape)],
        out_specs=row,
        out_shape=jax.ShapeDtypeStruct((b, l, d), F32),
        compiler_params=_params("arbitrary", "arbitrary"),
        name="attn_prompt",
    )(x, ys, pm, wo1, g2, wq, mk, mv, wxo)


def _merge_q_body(x_ref, ys_ref, p_ref, wo1_ref, g2_ref, wq_ref, x1_ref, q_ref):
    x1 = x_ref[...] + _dot(ys_ref[...], wo1_ref[...]) + p_ref[...]
    x1_ref[...] = x1
    q_ref[...] = _dot(_rms(x1, g2_ref[...]).astype(BF16), wq_ref[...])


def _merge_q(x2d, ys, pm, wo1, g2, wq):
    t, d = x2d.shape
    return pl.pallas_call(
        _merge_q_body,
        out_shape=[jax.ShapeDtypeStruct((t, d), F32)] * 2,
        compiler_params=pltpu.CompilerParams(vmem_limit_bytes=VMEM_LIMIT),
        name="merge_q",
    )(x2d, ys, pm, wo1, g2, wq)


def _tile_view(kv):
    b, m, nh, hd = kv.shape
    nt = hd // LANES
    return kv.reshape(b, m, nh, nt, LANES).transpose(0, 1, 3, 2, 4).reshape(b, m, nt * nh, LANES)


def _tile_perm_cols(w):
    d = w.shape[0]
    hd = w.shape[1] // X_HEADS
    return w.reshape(d, X_HEADS, hd // LANES, LANES).transpose(0, 2, 1, 3).reshape(d, X_HEADS * hd)


def _attn_step_body(q_ref, k_ref, v_ref, o_ref):
    bt, nm, rows, _ = k_ref.shape
    hd = rows * LANES // X_HEADS
    for b in range(bt):
        r = jnp.sum(k_ref[b] * q_ref[b], axis=-1, keepdims=True)
        s = (r + pltpu.roll(r, X_HEADS, 1)) * (hd ** -0.5)
        e = jnp.exp(s - jnp.max(s, axis=0, keepdims=True))
        p = e / jnp.sum(e, axis=0, keepdims=True)
        o_ref[b] = jnp.sum(p * v_ref[b], axis=0)


def _attn_step(q8, k8, v8):
    nbt, rows, _ = q8.shape
    nm = k8.shape[1]
    assert rows == 2 * X_HEADS, "score assembly assumes two 128-lane tiles per head"
    bt = 4
    qs = pl.BlockSpec((bt, rows, LANES), lambda i: (i, 0, 0))
    kv = pl.BlockSpec((bt, nm, rows, LANES), lambda i: (i, 0, 0, 0))
    return pl.pallas_call(
        _attn_step_body, grid=(nbt // bt,),
        in_specs=[qs, kv, kv], out_specs=qs,
        out_shape=jax.ShapeDtypeStruct((nbt, rows, LANES), F32),
        compiler_params=_params("arbitrary"),
        name="attn_step",
    )(q8, k8, v8)


def _ffn_body(has_o, final, *refs):
    if has_o:
        x_ref, o_in_ref, wxo_ref, g3_ref, wg_ref, wu_ref, wd_ref, gf_ref, y_ref = refs
        x2 = x_ref[...] + _dot(o_in_ref[...].astype(BF16), wxo_ref[...])
    else:
        x_ref, g3_ref, wg_ref, wu_ref, wd_ref, gf_ref, y_ref = refs
        x2 = x_ref[...]
    hf = _rms(x2, g3_ref[...]).astype(BF16)
    act = (_silu(_dot(hf, wg_ref[...])) * _dot(hf, wu_ref[...])).astype(BF16)
    x3 = x2 + _dot(act, wd_ref[...])
    y_ref[...] = _rms(x3, gf_ref[...]) if final else x3


def _ffn(x2d, g3, wg, wu, wd, gf, final, o_in=None, wxo=None):
    t, d = x2d.shape
    tm = min(TOKEN_TILE, t)
    row = pl.BlockSpec((tm, d), lambda i: (i, 0))
    has_o = o_in is not None
    ins = [x2d] + ([o_in, wxo] if has_o else []) + [g3, wg, wu, wd, gf]
    specs = [row] + ([row, _const_spec(wxo.shape)] if has_o else []) + [
        _const_spec(g3.shape), _const_spec(wg.shape), _const_spec(wu.shape), _const_spec(wd.shape),
        _const_spec(gf.shape)]
    return pl.pallas_call(
        functools.partial(_ffn_body, has_o, final), grid=(t // tm,),
        in_specs=specs, out_specs=row,
        out_shape=jax.ShapeDtypeStruct((t, d), F32),
        compiler_params=_params("arbitrary"),
        name="ffn_o" if has_o else "ffn",
    )(*ins)


def _layer_weights(i, ln1_g, w_in, conv_w, conv_b, dt_bias, a_log, d_ssd, gn_g, lam_re, lam_im, log_step,
                   b_re, b_im, c_re, c_im, d_s5, w_glu, b_glu, w_out, ln2_g, w_xq, w_xk, w_xv, w_xo, ln3_g,
                   w_gate, w_up, w_down):
    d = w_in.shape[1]
    n_heads = dt_bias.shape[1]
    dssd = n_heads * SSD_HEAD_DIM
    conv_dim = conv_w.shape[2]
    o1, o2, o3 = dssd, dssd + conv_dim, dssd + conv_dim + n_heads
    win = w_in[i]
    w = {}
    w["ln1"] = ln1_g[i].reshape(1, d)
    w["wz"] = win[:, :o1].astype(BF16)
    w["wx"] = win[:, o1:o2].astype(BF16)
    w["wd"] = jnp.pad(win[:, o2:o3], ((0, 0), (0, LANES - n_heads))).astype(BF16)
    w["wut"] = win[:, o3:].T.astype(BF16)
    w["cw"] = conv_w[i]
    w["cb"] = conv_b[i].reshape(1, conv_dim)
    w["dtb"] = jnp.pad(dt_bias[i], (0, LANES - n_heads)).reshape(1, LANES)
    w["alog"] = jnp.pad(a_log[i], (0, LANES - n_heads)).reshape(1, LANES)
    w["dexp"] = jnp.repeat(d_ssd[i], SSD_HEAD_DIM).reshape(1, dssd)
    w["gn"] = gn_g[i].reshape(1, dssd)
    w["s5"] = (lam_re[i], lam_im[i], log_step[i], b_re[i], b_im[i], c_re[i], c_im[i])
    ds5 = d_s5.shape[1]
    w["dcol"] = d_s5[i].reshape(ds5, 1)
    w["wglut"] = w_glu[i].T.astype(BF16)
    w["bglu"] = b_glu[i].reshape(ds5, 1)
    w["wo1"] = w_out[i, :dssd].astype(BF16)
    w["wo2"] = w_out[i, dssd:].astype(BF16)
    w["ln2"] = ln2_g[i].reshape(1, d)
    w["wq"] = w_xq[i].astype(BF16)
    w["wk"] = w_xk[i].astype(BF16)
    w["wv"] = w_xv[i].astype(BF16)
    w["wxo"] = w_xo[i].astype(BF16)
    w["wq_t"] = _tile_perm_cols(w_xq[i]).astype(BF16)
    w["wxo_t"] = _tile_perm_cols(w_xo[i].T).T.astype(BF16)
    w["ln3"] = ln3_g[i].reshape(1, d)
    w["wg"] = w_gate[i].astype(BF16)
    w["wu"] = w_up[i].astype(BF16)
    w["wdn"] = w_down[i].astype(BF16)
    return w


def _state_to_rows(s_re, s_im):
    return jnp.concatenate([s_re, s_im], axis=-1).transpose(1, 0, 2)


def _rows_to_state(s):
    p = s.shape[-1] // 2
    st = s.transpose(1, 0, 2)
    return st[..., :p], st[..., p:]


def kernel(x_prompt, x_sample, mem_prompt, state_conv, state_ssm, state_s5_re, state_s5_im, cache_mem_k, cache_mem_v, ln1_g, w_in, conv_w, conv_b, dt_bias, a_log, d_ssd, gn_g, lam_re, lam_im, log_step, b_re, b_im, c_re, c_im, d_s5, w_glu, b_glu, w_out, ln2_g, w_xq, w_xk, w_xv, w_xo, ln3_g, w_gate, w_up, w_down, final_g):
    bp, seq, d = x_prompt.shape
    bs = x_sample.shape[0]
    depth = w_in.shape[0]
    nm = mem_prompt.shape[1]
    ng, ns = lam_re.shape[1], lam_re.shape[2]
    gf = final_g.reshape(1, d)
    hp = x_prompt
    hs = x_sample.reshape(bs, d)
    outs = {k: [] for k in ("conv_p", "ssm_p", "re_p", "im_p", "mk_p", "mv_p", "conv_s", "ssm_s", "re_s", "im_s")}
    yp = ys_out = None
    for i in range(depth):
        w = _layer_weights(i, ln1_g, w_in, conv_w, conv_b, dt_bias, a_log, d_ssd, gn_g, lam_re, lam_im, log_step,
                           b_re, b_im, c_re, c_im, d_s5, w_glu, b_glu, w_out, ln2_g, w_xq, w_xk, w_xv, w_xo, ln3_g,
                           w_gate, w_up, w_down)
        last = i == depth - 1
        mk, mv = _kv_proj(mem_prompt.reshape(bp * nm, d), w["wk"], w["wv"])
        mk = mk.reshape(bp, nm, d)
        mv = mv.reshape(bp, nm, d)
        y_ssd, conv_p, ssm_p = _ssd_prompt(hp, w["ln1"], w["wz"], w["wx"], w["wd"], w["cw"], w["cb"], w["dtb"],
                                           w["alog"], w["dexp"], w["gn"])
        q = S5_Q
        tt, wt, zt, a1, a2 = _s5_prep(q, *w["s5"])
        xperm = hp.reshape(bp, seq // q, q, d).transpose(2, 1, 0, 3).reshape(q, (seq // q) * bp, d)
        s0 = jnp.zeros((ng, bp, 2 * ns), F32)
        pperm, sfin = _s5_mix(q, bp, xperm, w["ln1"], w["wut"], tt, wt, zt, a1, a2, s0, w["dcol"], w["wglut"],
                              w["bglu"], w["wo2"])
        pm = pperm.reshape(q, seq // q, bp, d).transpose(2, 1, 0, 3).reshape(bp, seq, d)
        re_p, im_p = _rows_to_state(sfin)
        x2 = _attn_prompt(hp, y_ssd, pm, w["wo1"], w["ln2"], w["wq"], mk, mv, w["wxo"])
        hp = _ffn(x2.reshape(bp * seq, d), w["ln3"], w["wg"], w["wu"], w["wdn"], gf, last).reshape(bp, seq, d)
        zs, xbcs, dts = _inproj(hs, w["ln1"], w["wz"], w["wx"], w["wd"])
        ys_ssd, conv_s_t, ssm_s = _ssd_step(zs, xbcs, dts, state_conv[i].transpose(1, 0, 2), state_ssm[i],
                                            w["cw"], w["cb"], w["dtb"], w["alog"], w["dexp"], w["gn"])
        tt1, wt1, zt1, a11, a21 = _s5_prep(1, *w["s5"])
        s0s = _state_to_rows(state_s5_re[i], state_s5_im[i])
        ps, sfin_s = _s5_mix(1, bs, hs.reshape(1, bs, d), w["ln1"], w["wut"], tt1, wt1, zt1, a11, a21, s0s,
                             w["dcol"], w["wglut"], w["bglu"], w["wo2"])
        re_s, im_s = _rows_to_state(sfin_s)
        x1s, qs = _merge_q(hs, ys_ssd, ps.reshape(bs, d), w["wo1"], w["ln2"], w["wq_t"])
        o_s = _attn_step(qs.reshape(bs, d // LANES, LANES), _tile_view(cache_mem_k[i]), _tile_view(cache_mem_v[i]))
        hs = _ffn(x1s, w["ln3"], w["wg"], w["wu"], w["wdn"], gf, last, o_in=o_s.reshape(bs, d), wxo=w["wxo_t"])
        for k, v in (("conv_p", conv_p), ("ssm_p", ssm_p), ("re_p", re_p), ("im_p", im_p),
                     ("mk_p", mk.reshape(bp, nm, X_HEADS, d // X_HEADS)),
                     ("mv_p", mv.reshape(bp, nm, X_HEADS, d // X_HEADS)),
                     ("conv_s", conv_s_t.transpose(1, 0, 2)), ("ssm_s", ssm_s), ("re_s", re_s), ("im_s", im_s)):
            outs[k].append(v)
    st = lambda k: jnp.stack(outs[k])
    return (hp, hs.reshape(bs, 1, d), st("conv_p"), st("ssm_p"), st("re_p"), st("im_p"), st("mk_p"), st("mv_p"),
            st("conv_s"), st("ssm_s"), st("re_s"), st("im_s"))
```

```python
import functools
import math

import jax
import jax.numpy as jnp
from jax import lax
from jax.experimental import pallas as pl
from jax.experimental.pallas import tpu as pltpu

F32 = jnp.float32
BF16 = jnp.bfloat16
EPS = 1e-6

LANES = 128
VMEM_LIMIT = 56 * 1024 * 1024

SSD_HEAD_DIM = 64
SSD_STATE = 128
SSD_GROUPS = 2
SSD_CHUNK = 128
CONV_W = 4
S5_CH = 16
S5_STATE = 64
S5_Q = 16
S5_GROUP_BATCH = 16
X_HEADS = 4
TOKEN_TILE = 512
PROJ_PIECE = 512


def _const_spec(shape):
    nd = len(shape)
    return pl.BlockSpec(shape, lambda *_: (0,) * nd, pipeline_mode=pl.Buffered(1))


def _params(*sem):
    return pltpu.CompilerParams(dimension_semantics=sem, vmem_limit_bytes=VMEM_LIMIT)


def _rms(x, g):
    return x * lax.rsqrt(jnp.mean(x * x, axis=-1, keepdims=True) + EPS) * g


def _sigmoid(x):
    return 1.0 / (1.0 + jnp.exp(-x))


def _silu(x):
    return x * _sigmoid(x)


def _gelu_tanh(x):
    return 0.5 * x * (1.0 + jnp.tanh(math.sqrt(2.0 / math.pi) * (x + 0.044715 * (x * x * x))))


def _softplus(x):
    return jnp.maximum(x, 0.0) + jnp.log1p(jnp.exp(-jnp.abs(x)))


def _dot(a, b):
    return jnp.dot(a, b, preferred_element_type=F32)


def _dot_nt(a, b):
    return lax.dot_general(a, b, (((1,), (1,)), ((), ())), preferred_element_type=F32)


def _dot_tn(a, b):
    return lax.dot_general(a, b, (((0,), (0,)), ((), ())), preferred_element_type=F32)


def _inproj_body(x_ref, g_ref, wz_ref, wx_ref, wd_ref, z_ref, xbc_ref, dt_ref):
    h = _rms(x_ref[...], g_ref[...]).astype(BF16)
    z_ref[...] = _dot(h, wz_ref[...])
    xbc_ref[...] = _dot(h, wx_ref[...])
    dt_ref[...] = _dot(h, wd_ref[...])


def _inproj(x2d, ln_g, wz, wx, wd):
    t, d = x2d.shape
    tm = min(TOKEN_TILE, t)
    nz, nx, nd = wz.shape[1], wx.shape[1], wd.shape[1]
    row = lambda n: pl.BlockSpec((tm, n), lambda i: (i, 0))
    return pl.pallas_call(
        _inproj_body,
        grid=(t // tm,),
        in_specs=[row(d), _const_spec((1, d)), _const_spec(wz.shape), _const_spec(wx.shape), _const_spec(wd.shape)],
        out_specs=[row(nz), row(nx), row(nd)],
        out_shape=[jax.ShapeDtypeStruct((t, nz), F32), jax.ShapeDtypeStruct((t, nx), F32),
                   jax.ShapeDtypeStruct((t, nd), F32)],
        compiler_params=_params("arbitrary"),
        name="inproj",
    )(x2d, ln_g, wz, wx, wd)


def _expand_heads(v, n_heads):
    rows = v.shape[0]
    lane = lax.broadcasted_iota(jnp.int32, (rows, LANES), 1)
    pieces = [jnp.where(lane < SSD_HEAD_DIM, v[:, 2 * j:2 * j + 1], v[:, 2 * j + 1:2 * j + 2])
              for j in range(n_heads // 2)]
    return jnp.concatenate(pieces, axis=1)


def _ssd_chunk_math(z, x, dt_raw, last_chunk, cw_ref, cb_ref, dtb_ref, alog_ref, dexp_ref, gn_ref, e2_ref,
                    y_ref, conv_ref, ssm_ref, xpad, state, side_jobs=()):
    jobs = list(side_jobs)

    def side(n=1):
        for _ in range(n):
            if jobs:
                jobs.pop(0)()

    q = SSD_CHUNK
    d_ssd = z.shape[-1]
    n_heads = d_ssd // SSD_HEAD_DIM
    hpg = n_heads // SSD_GROUPS
    gw = hpg * SSD_HEAD_DIM
    nct = xpad.shape[0]
    xt = d_ssd // LANES

    for t in range(nct):
        xpad[t, 8:8 + q, :] = x[:, t * LANES:(t + 1) * LANES]
    cw = cw_ref[...]
    conv = (cw[3] * xpad[:, 8:8 + q, :] + cw[2] * xpad[:, 7:7 + q, :] + cw[1] * xpad[:, 6:6 + q, :]
            + cw[0] * xpad[:, 5:5 + q, :] + cb_ref[...])
    xpad[:, 0:8, :] = xpad[:, q:q + 8, :]
    xact = _silu(conv)
    xs = jnp.concatenate([xact[t] for t in range(xt)], axis=1)
    bm = xact[xt:xt + SSD_GROUPS].astype(BF16)
    cm = xact[xt + SSD_GROUPS:].astype(BF16)
    side()

    dt = _softplus(dt_raw + dtb_ref[...])
    a = -jnp.exp(alog_ref[...])
    row = lax.broadcasted_iota(jnp.int32, (q, q), 0)
    col = lax.broadcasted_iota(jnp.int32, (q, q), 1)
    causal = row >= col
    tri = jnp.where(causal, 1.0, 0.0).astype(F32)
    acum = jnp.dot(tri, dt * a, precision=lax.Precision.HIGHEST, preferred_element_type=F32)
    acum_t = acum.T
    last = acum[q - 1:q, :]

    def expand(v):
        hi = v.astype(BF16)
        lo = (v - hi.astype(F32)).astype(BF16)
        return _dot(jnp.concatenate([hi, lo], axis=1), e2_ref[...])

    dt_e = expand(dt)
    ea_e = expand(jnp.exp(acum))
    dend_e = expand(jnp.exp(last - acum))

    dtx = xs * dt_e
    dtx_b = dtx.astype(BF16)
    xdec_b = (dtx * dend_e).astype(BF16)
    side()

    lane = lax.broadcasted_iota(jnp.int32, (q, LANES), 1)
    zero_b = jnp.zeros((q, LANES), BF16)
    y_tiles = []
    for g in range(SSD_GROUPS):
        bg = bm[g]
        cg = cm[g]
        cb = _dot_nt(cg, bg)
        gs = slice(g * gw, (g + 1) * gw)
        st_g = state[:, gs]
        y_off = _dot(cg, st_g.astype(BF16)) * ea_e[:, gs]
        for pr in range(hpg // 2):
            tile = g * (hpg // 2) + pr
            ms = []
            for h in (2 * tile, 2 * tile + 1):
                seg = acum[:, h:h + 1] - acum_t[h:h + 1, :]
                lmat = jnp.exp(jnp.where(causal, seg, -jnp.inf))
                ms.append((cb * lmat).astype(BF16))
            dtile = dtx_b[:, tile * LANES:(tile + 1) * LANES]
            rhs = jnp.concatenate([jnp.where(lane < SSD_HEAD_DIM, dtile, zero_b),
                                   jnp.where(lane >= SSD_HEAD_DIM, dtile, zero_b)], axis=0)
            y_tiles.append(_dot(jnp.concatenate(ms, axis=1), rhs)
                           + y_off[:, pr * LANES:(pr + 1) * LANES])
            if pr % 2 == 1:
                side()
        new = _dot_tn(bg, xdec_b[:, gs])
        state[:, gs] = st_g * ea_e[q - 1:q, gs] + new
    y = jnp.concatenate(y_tiles, axis=1) + dexp_ref[...] * xs
    y = y * _silu(z)
    ms = jnp.mean(y * y, axis=-1, keepdims=True)
    side(len(jobs))
    y_ref[...] = (y * lax.rsqrt(ms + EPS) * gn_ref[...]).astype(y_ref.dtype)

    @pl.when(last_chunk)
    def _():
        conv_ref[...] = x[q - (CONV_W - 1):q, :]
        ssm_ref[...] = state[...].T.reshape(ssm_ref.shape)


def _ssd_prompt_body(x_ref, g_ref, wz_ref, wx_ref, wd_ref, cw_ref, cb_ref, dtb_ref, alog_ref, dexp_ref, gn_ref,
                     e2_ref, y_ref, conv_ref, ssm_ref, za, xa, da, zb, xb, db, xpad, state):
    i = pl.program_id(0)
    j = pl.program_id(1)
    nc = pl.num_programs(1) - 1

    @pl.when(jnp.logical_and(i == 0, j == 0))
    def _():
        for r in (zb, xb, db):
            r[...] = jnp.zeros(r.shape, F32)

    @pl.when(j <= 1)
    def _():
        xpad[:, 0:8, :] = jnp.zeros((xpad.shape[0], 8, LANES), F32)
        state[...] = jnp.zeros(state.shape, F32)

    def step(wr, rd):
        xin = x_ref[...]
        msx = jnp.mean(xin * xin, axis=-1, keepdims=True)
        hbox = []

        def proj_piece(dst, w_ref, lo, hi):
            def run():
                if not hbox:
                    hbox.append((xin * lax.rsqrt(msx + EPS) * g_ref[...]).astype(BF16))
                dst[:, lo:hi] = _dot(hbox[0], w_ref[:, lo:hi])
            return run

        pieces = []
        for dst, w_ref in ((wr[0], wz_ref), (wr[2], wd_ref), (wr[1], wx_ref)):
            n = w_ref.shape[1]
            pieces += [proj_piece(dst, w_ref, lo, min(lo + PROJ_PIECE, n)) for lo in range(0, n, PROJ_PIECE)]
        _ssd_chunk_math(rd[0][...], rd[1][...], rd[2][...], j == nc, cw_ref, cb_ref, dtb_ref, alog_ref, dexp_ref,
                        gn_ref, e2_ref, y_ref, conv_ref, ssm_ref, xpad, state, pieces)

    @pl.when(j % 2 == 0)
    def _():
        step((za, xa, da), (zb, xb, db))

    @pl.when(j % 2 == 1)
    def _():
        step((zb, xb, db), (za, xa, da))


def _ssd_prompt(x, ln_g, wz, wx, wd, cw, cb, dtb, alog, dexp, gn):
    b, l, d = x.shape
    d_ssd, conv_dim = wz.shape[1], wx.shape[1]
    n_heads = d_ssd // SSD_HEAD_DIM
    q = SSD_CHUNK
    nc = l // q
    assert SSD_STATE == LANES and conv_dim == d_ssd + 2 * SSD_GROUPS * SSD_STATE
    nct = conv_dim // LANES
    cw4 = cw.reshape(CONV_W, nct, 1, LANES)
    cb3 = cb.reshape(nct, 1, LANES)
    e2 = (jnp.arange(2 * LANES)[:, None] % LANES == jnp.arange(d_ssd)[None, :] // SSD_HEAD_DIM).astype(BF16)
    slot = lambda n: pltpu.VMEM((q, n), F32)
    return pl.pallas_call(
        _ssd_prompt_body,
        grid=(b, nc + 1),
        in_specs=[pl.BlockSpec((None, q, d), lambda i, j: (i, jnp.minimum(j, nc - 1), 0)),
                  _const_spec(ln_g.shape), _const_spec(wz.shape), _const_spec(wx.shape), _const_spec(wd.shape),
                  _const_spec(cw4.shape), _const_spec(cb3.shape), _const_spec(dtb.shape), _const_spec(alog.shape),
                  _const_spec(dexp.shape), _const_spec(gn.shape), _const_spec(e2.shape)],
        out_specs=[pl.BlockSpec((None, q, d_ssd), lambda i, j: (i, jnp.maximum(j - 1, 0), 0)),
                   pl.BlockSpec((None, CONV_W - 1, conv_dim), lambda i, j: (i, 0, 0)),
                   pl.BlockSpec((None, n_heads, SSD_HEAD_DIM, SSD_STATE), lambda i, j: (i, 0, 0, 0))],
        out_shape=[jax.ShapeDtypeStruct((b, l, d_ssd), BF16),
                   jax.ShapeDtypeStruct((b, CONV_W - 1, conv_dim), F32),
                   jax.ShapeDtypeStruct((b, n_heads, SSD_HEAD_DIM, SSD_STATE), F32)],
        scratch_shapes=[slot(d_ssd), slot(conv_dim), slot(LANES), slot(d_ssd), slot(conv_dim), slot(LANES),
                        pltpu.VMEM((nct, q + 8, LANES), F32), pltpu.VMEM((SSD_STATE, d_ssd), F32)],
        compiler_params=_params("arbitrary", "arbitrary"),
        name="ssd_prompt",
    )(x, ln_g, wz, wx, wd, cw4, cb3, dtb, alog, dexp, gn, e2)


def _ssd_step_body(z_ref, xbc_ref, dt_ref, cs_ref, st_ref, cw_ref, cb_ref, dtb_ref, alog_ref, dexp_ref, gn_ref,
                   y_ref, conv_ref, ssm_ref):
    bt, d_ssd = z_ref.shape
    n_heads = d_ssd // SSD_HEAD_DIM
    hpg = n_heads // SSD_GROUPS
    x = xbc_ref[...]
    cw = cw_ref[...]
    conv = cw[0:1] * cs_ref[0] + cw[1:2] * cs_ref[1] + cw[2:3] * cs_ref[2] + cw[3:4] * x + cb_ref[...]
    conv_ref[0] = cs_ref[1]
    conv_ref[1] = cs_ref[2]
    conv_ref[2] = x
    xact = _silu(conv)
    xs = xact[:, :d_ssd]
    bm = xact[:, d_ssd:d_ssd + SSD_GROUPS * SSD_STATE]
    cm = xact[:, d_ssd + SSD_GROUPS * SSD_STATE:]
    dt = _softplus(dt_ref[...] + dtb_ref[...])
    da = jnp.exp(dt * (-jnp.exp(alog_ref[...])))
    dtx = xs * _expand_heads(dt, n_heads)
    pad = jnp.zeros((LANES - bt, d_ssd), F32)
    to_cols = lambda v: jnp.concatenate([v, pad], axis=0).T
    dtx_t = to_cols(dtx)
    da_t = to_cols(_expand_heads(da, n_heads))
    lane = lax.broadcasted_iota(jnp.int32, (d_ssd, LANES), 1)
    gw = hpg * SSD_HEAD_DIM
    rows_of = lambda v, b: jnp.concatenate(
        [jnp.broadcast_to(v[b:b + 1, g * SSD_STATE:(g + 1) * SSD_STATE], (gw, SSD_STATE)) for g in range(SSD_GROUPS)],
        axis=0)
    y_t = jnp.zeros((d_ssd, LANES), F32)
    for b in range(bt):
        s0 = st_ref[b].reshape(d_ssd, SSD_STATE)
        s_new = s0 * da_t[:, b:b + 1] + dtx_t[:, b:b + 1] * rows_of(bm, b)
        ssm_ref[b] = s_new.reshape(n_heads, SSD_HEAD_DIM, SSD_STATE)
        ycol = jnp.sum(s_new * rows_of(cm, b), axis=1, keepdims=True)
        y_t = jnp.where(lane == b, ycol, y_t)
    y = y_t.T[:bt, :] + dexp_ref[...] * xs
    y = y * _silu(z_ref[...])
    y_ref[...] = _rms(y, gn_ref[...]).astype(y_ref.dtype)


def _ssd_step(z, xbc, dt, conv_state_t, ssm_state, cw, cb, dtb, alog, dexp, gn):
    nb, d_ssd = z.shape
    conv_dim = xbc.shape[-1]
    n_heads = d_ssd // SSD_HEAD_DIM
    bt = 8
    row = lambda n: pl.BlockSpec((bt, n), lambda i: (i, 0))
    cs_spec = pl.BlockSpec((CONV_W - 1, bt, conv_dim), lambda i: (0, i, 0))
    st_spec = pl.BlockSpec((bt, n_heads, SSD_HEAD_DIM, SSD_STATE), lambda i: (i, 0, 0, 0))
    return pl.pallas_call(
        _ssd_step_body,
        grid=(nb // bt,),
        in_specs=[row(d_ssd), row(conv_dim), row(LANES), cs_spec, st_spec, _const_spec(cw.shape),
                  _const_spec(cb.shape), _const_spec(dtb.shape), _const_spec(alog.shape), _const_spec(dexp.shape),
                  _const_spec(gn.shape)],
        out_specs=[row(d_ssd), cs_spec, st_spec],
        out_shape=[jax.ShapeDtypeStruct((nb, d_ssd), BF16),
                   jax.ShapeDtypeStruct((CONV_W - 1, nb, conv_dim), F32),
                   jax.ShapeDtypeStruct(ssm_state.shape, F32)],
        compiler_params=_params("arbitrary"),
        name="ssd_step",
    )(z, xbc, dt, conv_state_t, ssm_state, cw, cb, dtb, alog, dexp, gn)


def _split_bf16(x):
    hi = x.astype(BF16)
    lo = (x - hi.astype(F32)).astype(BF16)
    return hi, lo


def _s5_prep_body(q, ls_ref, lrp_ref, lip_ref, lr2_ref, li2_ref, brt_ref, bit_ref, ccat_ref, ca_ref, cb_ref,
                  tt_ref, wt_ref, zt_ref, a1_ref, a2_ref):
    ng = ls_ref.shape[0]
    w = q * S5_CH
    step = jnp.exp(ls_ref[...])
    lr, li = lrp_ref[...], lip_ref[...]
    mag = jnp.exp(lr * step)
    ang = li * step
    lbr = mag * jnp.cos(ang)
    lbi = mag * jnp.sin(ang)
    den = lr * lr + li * li
    kr = ((lbr - 1.0) * lr + lbi * li) / den
    ki = (lbi * lr - (lbr - 1.0) * li) / den
    brt, bit = brt_ref[...], bit_ref[...]
    bbr = kr * brt - ki * bit
    bbi = kr * bit + ki * brt
    if q == 1:
        ball = jnp.concatenate([bbr, bbi], axis=1)
    else:
        d = (lax.broadcasted_iota(jnp.int32, (1, 1, w), 2) // S5_CH).astype(F32)
        pm = jnp.exp(d * (lr * step))
        pa = d * ang
        pr = pm * jnp.cos(pa)
        pi = pm * jnp.sin(pa)
        ball = jnp.concatenate([pr * bbr - pi * bbi, pr * bbi + pi * bbr], axis=1)
    wt_ref[...] = ball.astype(BF16)

    lane3 = lax.broadcasted_iota(jnp.int32, (1, 1, 2 * S5_STATE), 2)
    first = lane3 < S5_STATE
    csign = jnp.where(first, ccat_ref[...], -ccat_ref[...])
    ch, cl = _split_bf16(csign)
    bh, bl = _split_bf16(ball)
    bdot = lambda x, y: lax.dot_general(x, y, (((2,), (1,)), ((0,), (0,))), preferred_element_type=F32)
    kall = bdot(ch, bh) + bdot(ch, bl) + bdot(cl, bh)
    k2 = kall.reshape(ng * S5_CH, w)
    lane2 = lax.broadcasted_iota(jnp.int32, (ng * S5_CH, w), 1)
    for t in range(q):
        sh = (q - 1 - t) * S5_CH
        r = pltpu.roll(k2, sh, 1) if sh else k2
        r = jnp.where(lane2 >= sh, r, 0.0)
        tt_ref[:, t * S5_CH:(t + 1) * S5_CH, :] = r.reshape(ng, S5_CH, w).astype(BF16)

    lr2, li2 = lr2_ref[...], li2_ref[...]
    t1 = (lax.broadcasted_iota(jnp.int32, (1, q, 1), 1) + 1).astype(F32)
    zm = jnp.exp(t1 * (lr2 * step))
    za = t1 * (li2 * step)
    zr = zm * jnp.cos(za)
    zi = zm * jnp.sin(za)
    ca, cb = ca_ref[...], cb_ref[...]
    for t in range(q):
        prt = zr[:, t:t + 1, :]
        pit = zi[:, t:t + 1, :]
        zt = jnp.where(first, ca * prt - cb * pit, -(ca * pit) - cb * prt)
        zt_ref[:, t * S5_CH:(t + 1) * S5_CH, :] = zt.astype(BF16)
    qf = float(q)
    mq = jnp.exp(qf * (lr2 * step))
    aq = qf * (li2 * step)
    ar = mq * jnp.cos(aq)
    ai = mq * jnp.sin(aq)
    a1_ref[...] = ar
    a2_ref[...] = jnp.where(first, -ai, ai)


def _s5_prep(q, lam_re, lam_im, log_step, b_re, b_im, c_re, c_im):
    ng, p = lam_re.shape
    w = q * S5_CH
    ls = log_step.reshape(ng, 1, 1)
    lrp = lam_re.reshape(ng, p, 1)
    lip = lam_im.reshape(ng, p, 1)
    lr2 = jnp.concatenate([lam_re, lam_re], axis=-1).reshape(ng, 1, 2 * p)
    li2 = jnp.concatenate([lam_im, lam_im], axis=-1).reshape(ng, 1, 2 * p)
    brt = jnp.tile(b_re, (1, 1, q))
    bit = jnp.tile(b_im, (1, 1, q))
    ccat = jnp.concatenate([c_re, c_im], axis=-1)
    ca = jnp.concatenate([c_re, c_re], axis=-1)
    cb = jnp.concatenate([c_im, c_im], axis=-1)
    gb = ng if q == 1 else 16
    blk = lambda a, b: pl.BlockSpec((gb, a, b), lambda i: (i, 0, 0))
    return pl.pallas_call(
        functools.partial(_s5_prep_body, q),
        grid=(ng // gb,),
        in_specs=[blk(1, 1), blk(p, 1), blk(p, 1), blk(1, 2 * p), blk(1, 2 * p), blk(p, w), blk(p, w),
                  blk(S5_CH, 2 * p), blk(S5_CH, 2 * p), blk(S5_CH, 2 * p)],
        out_specs=[blk(w, w), blk(2 * p, w), blk(w, 2 * p), blk(1, 2 * p), blk(1, 2 * p)],
        out_shape=[jax.ShapeDtypeStruct((ng, w, w), BF16), jax.ShapeDtypeStruct((ng, 2 * p, w), BF16),
                   jax.ShapeDtypeStruct((ng, w, 2 * p), BF16), jax.ShapeDtypeStruct((ng, 1, 2 * p), F32),
                   jax.ShapeDtypeStruct((ng, 1, 2 * p), F32)],
        compiler_params=_params("arbitrary"),
        name=f"s5_prep_q{q}",
    )(ls, lrp, lip, lr2, li2, brt, bit, ccat, ca, cb)


def _s5_mix_body(q, nb, nkb, npi, npt, perm, x_ref, g_ref, wut_ref, tt_ref, wt_ref, zt_ref, a1_ref, a2_ref, s0_ref,
                 dcol_ref, wglut_ref, bglu_ref, wo_ref, p_ref, sfin_ref, ut, yt, carry):
    ng = tt_ref.shape[0]
    nk = nkb // nb
    qi = q // npi
    tile = pl.program_id(0)
    ph = pl.program_id(1)

    @pl.when(jnp.logical_and(tile == 0, ph == 0))
    def _():
        carry[...] = s0_ref[...]

    def rows_of(r):
        if not perm:
            return x_ref[r]
        nseq, nblk, ndt, nr, _ = x_ref.shape
        flat = x_ref.reshape(nseq * nblk * ndt * nr, LANES)
        return jnp.concatenate(
            [jnp.concatenate([flat[pl.ds((k * ndt + dt) * nr + r, nseq, stride=nblk * ndt * nr), :]
                              for dt in range(ndt)], axis=1) for k in range(nblk)], axis=0)

    for hh in range(npi):
        @pl.when(ph == hh)
        def _():
            step = 2 if qi % 2 == 0 else 1
            for r in range(0, qi, step):
                xin = jnp.concatenate([rows_of(r + s) for s in range(step)], axis=0)
                u = _dot_nt(wut_ref[...], _rms(xin, g_ref[...]).astype(BF16))
                sl = slice((hh * qi + r) * nkb, (hh * qi + r + step) * nkb)
                ut[:, sl] = u.astype(BF16)
                yt[:, sl] = dcol_ref[...] * u

    @pl.when(ph == npi - 1)
    def _():
        gu = S5_GROUP_BATCH
        bdot = lambda a, b: lax.dot_general(a, b, (((2,), (1,)), ((0,), (0,))), preferred_element_type=F32)
        bdot_nt = lambda a, b: lax.dot_general(a, b, (((2,), (2,)), ((0,), (0,))), preferred_element_type=F32)

        def groups(i, _):
            g0 = pl.multiple_of(i * gu, gu)
            r0 = pl.multiple_of(i * (gu * S5_CH), gu * S5_CH)
            gsl = pl.ds(g0, gu)
            rows = ut[pl.ds(r0, gu * S5_CH), :].reshape(gu, S5_CH, q * nkb)
            ugt = jnp.concatenate([rows[:, :, (q - 1 - j) * nkb:(q - j) * nkb] for j in range(q)], axis=1)
            y = bdot(tt_ref[gsl], ugt)
            vt = bdot(wt_ref[gsl], ugt)
            v = jnp.swapaxes(vt, 1, 2)
            v_sw = jnp.swapaxes(jnp.concatenate([vt[:, S5_STATE:], vt[:, :S5_STATE]], axis=1), 1, 2)
            a1 = a1_ref[gsl]
            a2 = a2_ref[gsl]
            s = carry[gsl]
            s_sw = pltpu.roll(s.reshape(gu * nb, 2 * S5_STATE), S5_STATE, 1).reshape(s.shape)
            prev = []
            for k in range(nk):
                prev.append(s)
                s, s_sw = (a1 * s + a2 * s_sw + v[:, k * nb:(k + 1) * nb, :],
                           a1 * s_sw - a2 * s + v_sw[:, k * nb:(k + 1) * nb, :])
            carry[gsl] = s
            sprev = jnp.concatenate(prev, axis=1).astype(BF16)
            y = y + bdot_nt(zt_ref[gsl], sprev)
            for t in range(q):
                yt[pl.ds(r0, gu * S5_CH), t * nkb:(t + 1) * nkb] += (
                    y[:, t * S5_CH:(t + 1) * S5_CH, :].reshape(gu * S5_CH, nkb))
            return 0

        lax.fori_loop(0, ng // gu, groups, 0)
        sfin_ref[...] = carry[...]

    ch = (q // npt) * nkb
    for j in range(npt):
        @pl.when(ph == npi + j)
        def _():
            gt = _gelu_tanh(yt[:, j * ch:(j + 1) * ch])
            gate = _dot(wglut_ref[...], gt.astype(BF16)) + bglu_ref[...]
            y5 = (gt * _sigmoid(gate)).astype(BF16)
            p_ref[...] = _dot_tn(y5, wo_ref[...]).reshape(p_ref.shape)


def _s5_mix(q, nb, x, ln_g, wut, tt, wt, zt, a1, a2, s0, dcol, wglut, bglu, wo):
    nseq, l, d = x.shape
    nblk = l // q
    nlt = nblk * nseq
    nkb = min(LANES, nlt)
    ntile = nlt // nkb
    tok = q * nkb
    npt = max(1, tok // TOKEN_TILE)
    dm = wo.shape[1]
    perm = q > 1
    if perm:
        half = 8
        npi = q // half
        assert nb == nseq and d % LANES == 0 and q % half == 0
        xv = x.reshape(nseq, nblk, npi, half, d // LANES, LANES).transpose(0, 1, 2, 4, 3, 5)
        x_spec = pl.BlockSpec((nseq, nkb // nseq, None, d // LANES, half, LANES),
                              lambda i, j: (0, i, jnp.minimum(j, npi - 1), 0, 0, 0))
    else:
        npi = 1
        xv = x.reshape(1, nseq, d)
        x_spec = pl.BlockSpec((1, nkb, d), lambda i, j: (0, i, 0))
    body = functools.partial(_s5_mix_body, q, nb, nkb, npi, npt, perm)
    p_spec = pl.BlockSpec((q // npt, nkb, dm), lambda i, j: (jnp.maximum(j - npi, 0), i, 0))
    return pl.pallas_call(
        body,
        grid=(ntile, npi + npt),
        in_specs=[x_spec, _const_spec(ln_g.shape), _const_spec(wut.shape), _const_spec(tt.shape),
                  _const_spec(wt.shape), _const_spec(zt.shape), _const_spec(a1.shape), _const_spec(a2.shape),
                  _const_spec(s0.shape), _const_spec(dcol.shape), _const_spec(wglut.shape),
                  _const_spec(bglu.shape), _const_spec(wo.shape)],
        out_specs=[p_spec, pl.BlockSpec(s0.shape, lambda i, j: (0, 0, 0))],
        out_shape=[jax.ShapeDtypeStruct((q, nlt, dm), F32), jax.ShapeDtypeStruct(s0.shape, F32)],
        scratch_shapes=[pltpu.VMEM((d, tok), BF16), pltpu.VMEM((d, tok), F32), pltpu.VMEM(s0.shape, F32)],
        compiler_params=_params("arbitrary", "arbitrary"),
        name=f"s5_mix_q{q}",
    )(xv, ln_g, wut, tt, wt, zt, a1, a2, s0, dcol, wglut, bglu, wo)


def _kv_body(m_ref, wk_ref, wv_ref, k_ref, v_ref):
    m = m_ref[...].astype(BF16)
    k_ref[...] = _dot(m, wk_ref[...])
    v_ref[...] = _dot(m, wv_ref[...])


def _kv_proj(mem2d, wk, wv):
    t, d = mem2d.shape
    tm = min(TOKEN_TILE, t)
    row = pl.BlockSpec((tm, d), lambda i: (i, 0))
    return pl.pallas_call(
        _kv_body, grid=(t // tm,),
        in_specs=[row, _const_spec(wk.shape), _const_spec(wv.shape)],
        out_specs=[row, row],
        out_shape=[jax.ShapeDtypeStruct((t, d), F32)] * 2,
        compiler_params=_params("arbitrary"),
        name="kv_proj",
    )(mem2d, wk, wv)


def _attn_prompt_body(x_ref, ys_ref, p_ref, wo1_ref, g2_ref, wq_ref, k_ref, v_ref, wxo_ref, o_ref):
    d = x_ref.shape[-1]
    hd = d // X_HEADS
    x1 = x_ref[...] + _dot(ys_ref[...], wo1_ref[...]) + p_ref[...]
    hq = _rms(x1, g2_ref[...]).astype(BF16)
    qv = _dot(hq, wq_ref[...]).astype(BF16)
    kb = k_ref[...].astype(BF16)
    vb = v_ref[...].astype(BF16)
    outs = []
    for h in range(X_HEADS):
        sl = slice(h * hd, (h + 1) * hd)
        s = _dot_nt(qv[:, sl], kb[:, sl]) * (hd ** -0.5)
        e = jnp.exp(s - jnp.max(s, axis=-1, keepdims=True))
        p = (e / jnp.sum(e, axis=-1, keepdims=True)).astype(BF16)
        outs.append(_dot(p, vb[:, sl]))
    o = jnp.concatenate(outs, axis=1).astype(BF16)
    o_ref[...] = x1 + _dot(o, wxo_ref[...])


def _attn_prompt(x, ys, pm, wo1, g2, wq, mk, mv, wxo):
    b, l, d = x.shape
    nm = mk.shape[1]
    tm = min(TOKEN_TILE, l)
    row = pl.BlockSpec((None, tm, d), lambda i, j: (i, j, 0))
    kv = pl.BlockSpec((None, nm, d), lambda i, j: (i, 0, 0))
    return pl.pallas_call(
        _attn_prompt_body, grid=(b, l // tm),
        in_specs=[row, row, row, _const_spec(wo1.shape), _const_spec(g2.shape), _const_spec(wq.shape), kv, kv,
                  _const_spec(wxo.shape)],
        out_specs=row,
        out_shape=jax.ShapeDtypeStruct((b, l, d), F32),
        compiler_params=_params("arbitrary", "arbitrary"),
        name="attn_prompt",
    )(x, ys, pm, wo1, g2, wq, mk, mv, wxo)


def _merge_q_body(x_ref, ys_ref, p_ref, wo1_ref, g2_ref, wq_ref, x1_ref, q_ref):
    x1 = x_ref[...] + _dot(ys_ref[...], wo1_ref[...]) + p_ref[...]
    x1_ref[...] = x1
    q_ref[...] = _dot(_rms(x1, g2_ref[...]).astype(BF16), wq_ref[...])


def _merge_q(x2d, ys, pm, wo1, g2, wq):
    t, d = x2d.shape
    return pl.pallas_call(
        _merge_q_body,
        out_shape=[jax.ShapeDtypeStruct((t, d), F32)] * 2,
        compiler_params=pltpu.CompilerParams(vmem_limit_bytes=VMEM_LIMIT),
        name="merge_q",
    )(x2d, ys, pm, wo1, g2, wq)


def _tile_view(kv):
    b, m, nh, hd = kv.shape
    nt = hd // LANES
    return kv.reshape(b, m, nh, nt, LANES).transpose(0, 1, 3, 2, 4).reshape(b, m, nt * nh, LANES)


def _tile_perm_cols(w):
    d = w.shape[0]
    hd = w.shape[1] // X_HEADS
    return w.reshape(d, X_HEADS, hd // LANES, LANES).transpose(0, 2, 1, 3).reshape(d, X_HEADS * hd)


def _attn_step_body(q_ref, k_ref, v_ref, o_ref):
    bt, nm, rows, _ = k_ref.shape
    hd = rows * LANES // X_HEADS
    for b in range(bt):
        r = jnp.sum(k_ref[b] * q_ref[b], axis=-1, keepdims=True)
        s = (r + pltpu.roll(r, X_HEADS, 1)) * (hd ** -0.5)
        e = jnp.exp(s - jnp.max(s, axis=0, keepdims=True))
        p = e / jnp.sum(e, axis=0, keepdims=True)
        o_ref[b] = jnp.sum(p * v_ref[b], axis=0)


def _attn_step(q8, k8, v8):
    nbt, rows, _ = q8.shape
    nm = k8.shape[1]
    assert rows == 2 * X_HEADS, "score assembly assumes two 128-lane tiles per head"
    bt = 4
    qs = pl.BlockSpec((bt, rows, LANES), lambda i: (i, 0, 0))
    kv = pl.BlockSpec((bt, nm, rows, LANES), lambda i: (i, 0, 0, 0))
    return pl.pallas_call(
        _attn_step_body, grid=(nbt // bt,),
        in_specs=[qs, kv, kv], out_specs=qs,
        out_shape=jax.ShapeDtypeStruct((nbt, rows, LANES), F32),
        compiler_params=_params("arbitrary"),
        name="attn_step",
    )(q8, k8, v8)


def _ffn_body(has_o, final, *refs):
    if has_o:
        x_ref, o_in_ref, wxo_ref, g3_ref, wg_ref, wu_ref, wd_ref, gf_ref, y_ref = refs
        x2 = x_ref[...] + _dot(o_in_ref[...].astype(BF16), wxo_ref[...])
    else:
        x_ref, g3_ref, wg_ref, wu_ref, wd_ref, gf_ref, y_ref = refs
        x2 = x_ref[...]
    hf = _rms(x2, g3_ref[...]).astype(BF16)
    act = (_silu(_dot(hf, wg_ref[...])) * _dot(hf, wu_ref[...])).astype(BF16)
    x3 = x2 + _dot(act, wd_ref[...])
    y_ref[...] = _rms(x3, gf_ref[...]) if final else x3


def _ffn(x2d, g3, wg, wu, wd, gf, final, o_in=None, wxo=None):
    t, d = x2d.shape
    tm = min(TOKEN_TILE, t)
    row = pl.BlockSpec((tm, d), lambda i: (i, 0))
    has_o = o_in is not None
    ins = [x2d] + ([o_in, wxo] if has_o else []) + [g3, wg, wu, wd, gf]
    specs = [row] + ([row, _const_spec(wxo.shape)] if has_o else []) + [
        _const_spec(g3.shape), _const_spec(wg.shape), _const_spec(wu.shape), _const_spec(wd.shape),
        _const_spec(gf.shape)]
    return pl.pallas_call(
        functools.partial(_ffn_body, has_o, final), grid=(t // tm,),
        in_specs=specs, out_specs=row,
        out_shape=jax.ShapeDtypeStruct((t, d), F32),
        compiler_params=_params("arbitrary"),
        name="ffn_o" if has_o else "ffn",
    )(*ins)


def _layer_weights(i, ln1_g, w_in, conv_w, conv_b, dt_bias, a_log, d_ssd, gn_g, lam_re, lam_im, log_step,
                   b_re, b_im, c_re, c_im, d_s5, w_glu, b_glu, w_out, ln2_g, w_xq, w_xk, w_xv, w_xo, ln3_g,
                   w_gate, w_up, w_down):
    d = w_in.shape[1]
    n_heads = dt_bias.shape[1]
    dssd = n_heads * SSD_HEAD_DIM
    conv_dim = conv_w.shape[2]
    o1, o2, o3 = dssd, dssd + conv_dim, dssd + conv_dim + n_heads
    win = w_in[i]
    w = {}
    w["ln1"] = ln1_g[i].reshape(1, d)
    w["wz"] = win[:, :o1].astype(BF16)
    w["wx"] = win[:, o1:o2].astype(BF16)
    w["wd"] = jnp.pad(win[:, o2:o3], ((0, 0), (0, LANES - n_heads))).astype(BF16)
    w["wut"] = win[:, o3:].T.astype(BF16)
    w["cw"] = conv_w[i]
    w["cb"] = conv_b[i].reshape(1, conv_dim)
    w["dtb"] = jnp.pad(dt_bias[i], (0, LANES - n_heads)).reshape(1, LANES)
    w["alog"] = jnp.pad(a_log[i], (0, LANES - n_heads)).reshape(1, LANES)
    w["dexp"] = jnp.repeat(d_ssd[i], SSD_HEAD_DIM).reshape(1, dssd)
    w["gn"] = gn_g[i].reshape(1, dssd)
    w["s5"] = (lam_re[i], lam_im[i], log_step[i], b_re[i], b_im[i], c_re[i], c_im[i])
    ds5 = d_s5.shape[1]
    w["dcol"] = d_s5[i].reshape(ds5, 1)
    w["wglut"] = w_glu[i].T.astype(BF16)
    w["bglu"] = b_glu[i].reshape(ds5, 1)
    w["wo1"] = w_out[i, :dssd].astype(BF16)
    w["wo2"] = w_out[i, dssd:].astype(BF16)
    w["ln2"] = ln2_g[i].reshape(1, d)
    w["wq"] = w_xq[i].astype(BF16)
    w["wk"] = w_xk[i].astype(BF16)
    w["wv"] = w_xv[i].astype(BF16)
    w["wxo"] = w_xo[i].astype(BF16)
    w["wq_t"] = _tile_perm_cols(w_xq[i]).astype(BF16)
    w["wxo_t"] = _tile_perm_cols(w_xo[i].T).T.astype(BF16)
    w["ln3"] = ln3_g[i].reshape(1, d)
    w["wg"] = w_gate[i].astype(BF16)
    w["wu"] = w_up[i].astype(BF16)
    w["wdn"] = w_down[i].astype(BF16)
    return w


def _state_to_rows(s_re, s_im):
    return jnp.concatenate([s_re, s_im], axis=-1).transpose(1, 0, 2)


def _rows_to_state(s):
    p = s.shape[-1] // 2
    st = s.transpose(1, 0, 2)
    return st[..., :p], st[..., p:]


def kernel(x_prompt, x_sample, mem_prompt, state_conv, state_ssm, state_s5_re, state_s5_im, cache_mem_k, cache_mem_v, ln1_g, w_in, conv_w, conv_b, dt_bias, a_log, d_ssd, gn_g, lam_re, lam_im, log_step, b_re, b_im, c_re, c_im, d_s5, w_glu, b_glu, w_out, ln2_g, w_xq, w_xk, w_xv, w_xo, ln3_g, w_gate, w_up, w_down, final_g):
    bp, seq, d = x_prompt.shape
    bs = x_sample.shape[0]
    depth = w_in.shape[0]
    nm = mem_prompt.shape[1]
    ng, ns = lam_re.shape[1], lam_re.shape[2]
    gf = final_g.reshape(1, d)
    hp = x_prompt
    hs = x_sample.reshape(bs, d)
    outs = {k: [] for k in ("conv_p", "ssm_p", "re_p", "im_p", "mk_p", "mv_p", "conv_s", "ssm_s", "re_s", "im_s")}
    yp = ys_out = None
    for i in range(depth):
        w = _layer_weights(i, ln1_g, w_in, conv_w, conv_b, dt_bias, a_log, d_ssd, gn_g, lam_re, lam_im, log_step,
                           b_re, b_im, c_re, c_im, d_s5, w_glu, b_glu, w_out, ln2_g, w_xq, w_xk, w_xv, w_xo, ln3_g,
                           w_gate, w_up, w_down)
        last = i == depth - 1
        mk, mv = _kv_proj(mem_prompt.reshape(bp * nm, d), w["wk"], w["wv"])
        mk = mk.reshape(bp, nm, d)
        mv = mv.reshape(bp, nm, d)
        y_ssd, conv_p, ssm_p = _ssd_prompt(hp, w["ln1"], w["wz"], w["wx"], w["wd"], w["cw"], w["cb"], w["dtb"],
                                           w["alog"], w["dexp"], w["gn"])
        q = S5_Q
        tt, wt, zt, a1, a2 = _s5_prep(q, *w["s5"])
        s0 = jnp.zeros((ng, bp, 2 * ns), F32)
        pperm, sfin = _s5_mix(q, bp, hp, w["ln1"], w["wut"], tt, wt, zt, a1, a2, s0, w["dcol"], w["wglut"],
                              w["bglu"], w["wo2"])
        pm = pperm.reshape(q, seq // q, bp, d).transpose(2, 1, 0, 3).reshape(bp, seq, d)
        re_p, im_p = _rows_to_state(sfin)
        x2 = _attn_prompt(hp, y_ssd, pm, w["wo1"], w["ln2"], w["wq"], mk, mv, w["wxo"])
        hp = _ffn(x2.reshape(bp * seq, d), w["ln3"], w["wg"], w["wu"], w["wdn"], gf, last).reshape(bp, seq, d)
        zs, xbcs, dts = _inproj(hs, w["ln1"], w["wz"], w["wx"], w["wd"])
        ys_ssd, conv_s_t, ssm_s = _ssd_step(zs, xbcs, dts, state_conv[i].transpose(1, 0, 2), state_ssm[i],
                                            w["cw"], w["cb"], w["dtb"], w["alog"], w["dexp"], w["gn"])
        tt1, wt1, zt1, a11, a21 = _s5_prep(1, *w["s5"])
        s0s = _state_to_rows(state_s5_re[i], state_s5_im[i])
        ps, sfin_s = _s5_mix(1, bs, hs.reshape(bs, 1, d), w["ln1"], w["wut"], tt1, wt1, zt1, a11, a21, s0s,
                             w["dcol"], w["wglut"], w["bglu"], w["wo2"])
        re_s, im_s = _rows_to_state(sfin_s)
        x1s, qs = _merge_q(hs, ys_ssd, ps.reshape(bs, d), w["wo1"], w["ln2"], w["wq_t"])
        o_s = _attn_step(qs.reshape(bs, d // LANES, LANES), _tile_view(cache_mem_k[i]), _tile_view(cache_mem_v[i]))
        hs = _ffn(x1s, w["ln3"], w["wg"], w["wu"], w["wdn"], gf, last, o_in=o_s.reshape(bs, d), wxo=w["wxo_t"])
        for k, v in (("conv_p", conv_p), ("ssm_p", ssm_p), ("re_p", re_p), ("im_p", im_p),
                     ("mk_p", mk.reshape(bp, nm, X_HEADS, d // X_HEADS)),
                     ("mv_p", mv.reshape(bp, nm, X_HEADS, d // X_HEADS)),
                     ("conv_s", conv_s_t.transpose(1, 0, 2)), ("ssm_s", ssm_s), ("re_s", re_s), ("im_s", im_s)):
            outs[k].append(v)
    st = lambda k: jnp.stack(outs[k])
    return (hp, hs.reshape(bs, 1, d), st("conv_p"), st("ssm_p"), st("re_p"), st("im_p"), st("mk_p"), st("mv_p"),
            st("conv_s"), st("ssm_s"), st("re_s"), st("im_s"))
```

```python
import functools
import math

import jax
import jax.numpy as jnp
from jax import lax
from jax.experimental import pallas as pl
from jax.experimental.pallas import tpu as pltpu

F32 = jnp.float32
BF16 = jnp.bfloat16
EPS = 1e-6

LANES = 128
VMEM_LIMIT = 56 * 1024 * 1024

SSD_HEAD_DIM = 64
SSD_STATE = 128
SSD_GROUPS = 2
SSD_CHUNK = 128
CONV_W = 4
S5_CH = 16
S5_STATE = 64
S5_Q = 16
S5_GROUP_BATCH = 16
X_HEADS = 4
TOKEN_TILE = 512
PROJ_PIECE = 512


def _const_spec(shape):
    nd = len(shape)
    return pl.BlockSpec(shape, lambda *_: (0,) * nd, pipeline_mode=pl.Buffered(1))


def _params(*sem):
    return pltpu.CompilerParams(dimension_semantics=sem, vmem_limit_bytes=VMEM_LIMIT)


def _rms(x, g):
    return x * lax.rsqrt(jnp.mean(x * x, axis=-1, keepdims=True) + EPS) * g


def _sigmoid(x):
    return 1.0 / (1.0 + jnp.exp(-x))


def _silu(x):
    return x * _sigmoid(x)


def _gelu_tanh(x):
    return 0.5 * x * (1.0 + jnp.tanh(math.sqrt(2.0 / math.pi) * (x + 0.044715 * (x * x * x))))


def _softplus(x):
    return jnp.maximum(x, 0.0) + jnp.log1p(jnp.exp(-jnp.abs(x)))


def _dot(a, b):
    return jnp.dot(a, b, preferred_element_type=F32)


def _dot_nt(a, b):
    return lax.dot_general(a, b, (((1,), (1,)), ((), ())), preferred_element_type=F32)


def _dot_tn(a, b):
    return lax.dot_general(a, b, (((0,), (0,)), ((), ())), preferred_element_type=F32)


def _proj_windows(d_ssd, conv_dim):
    assert d_ssd % LANES == 0 and conv_dim % LANES == 0
    return (0, d_ssd), (d_ssd, d_ssd + conv_dim), (d_ssd + conv_dim, d_ssd + conv_dim + LANES)


def _inproj_body(wins, x_ref, g_ref, w_ref, z_ref, xbc_ref, dt_ref):
    h = _rms(x_ref[...], g_ref[...]).astype(BF16)
    for (lo, hi), o_ref in zip(wins, (z_ref, xbc_ref, dt_ref)):
        o_ref[...] = _dot(h, w_ref[:, lo:hi])


def _inproj(x2d, ln_g, w_in, d_ssd, conv_dim):
    t, d = x2d.shape
    tm = min(TOKEN_TILE, t)
    wins = _proj_windows(d_ssd, conv_dim)
    row = lambda n: pl.BlockSpec((tm, n), lambda i: (i, 0))
    return pl.pallas_call(
        functools.partial(_inproj_body, wins),
        grid=(t // tm,),
        in_specs=[row(d), _const_spec((1, d)), _const_spec(w_in.shape)],
        out_specs=[row(hi - lo) for lo, hi in wins],
        out_shape=[jax.ShapeDtypeStruct((t, hi - lo), F32) for lo, hi in wins],
        compiler_params=_params("arbitrary"),
        name="inproj",
    )(x2d, ln_g, w_in)


def _expand_heads(v, n_heads):
    rows = v.shape[0]
    lane = lax.broadcasted_iota(jnp.int32, (rows, LANES), 1)
    pieces = [jnp.where(lane < SSD_HEAD_DIM, v[:, 2 * j:2 * j + 1], v[:, 2 * j + 1:2 * j + 2])
              for j in range(n_heads // 2)]
    return jnp.concatenate(pieces, axis=1)


def _ssd_chunk_math(z, x, dt_raw, last_chunk, cw_ref, cb_ref, dtb_ref, alog_ref, dexp_ref, gn_ref, e2_ref,
                    y_ref, conv_ref, ssm_ref, xpad, state, side_jobs=()):
    jobs = list(side_jobs)

    def side(n=1):
        for _ in range(n):
            if jobs:
                jobs.pop(0)()

    q = SSD_CHUNK
    d_ssd = z.shape[-1]
    n_heads = d_ssd // SSD_HEAD_DIM
    hpg = n_heads // SSD_GROUPS
    gw = hpg * SSD_HEAD_DIM
    nct = xpad.shape[0]
    xt = d_ssd // LANES

    for t in range(nct):
        xpad[t, 8:8 + q, :] = x[:, t * LANES:(t + 1) * LANES]
    cw = cw_ref[...]
    conv = (cw[3] * xpad[:, 8:8 + q, :] + cw[2] * xpad[:, 7:7 + q, :] + cw[1] * xpad[:, 6:6 + q, :]
            + cw[0] * xpad[:, 5:5 + q, :] + cb_ref[...])
    xpad[:, 0:8, :] = xpad[:, q:q + 8, :]
    xact = _silu(conv)
    xs = jnp.concatenate([xact[t] for t in range(xt)], axis=1)
    bm = xact[xt:xt + SSD_GROUPS].astype(BF16)
    cm = xact[xt + SSD_GROUPS:].astype(BF16)
    side()

    dt = _softplus(dt_raw + dtb_ref[...])
    a = -jnp.exp(alog_ref[...])
    row = lax.broadcasted_iota(jnp.int32, (q, q), 0)
    col = lax.broadcasted_iota(jnp.int32, (q, q), 1)
    causal = row >= col
    tri = jnp.where(causal, 1.0, 0.0).astype(F32)
    acum = jnp.dot(tri, dt * a, precision=lax.Precision.HIGHEST, preferred_element_type=F32)
    acum_t = acum.T
    last = acum[q - 1:q, :]

    def expand(v):
        hi = v.astype(BF16)
        lo = (v - hi.astype(F32)).astype(BF16)
        return _dot(jnp.concatenate([hi, lo], axis=1), e2_ref[...])

    dt_e = expand(dt)
    ea_e = expand(jnp.exp(acum))
    dend_e = expand(jnp.exp(last - acum))

    dtx = xs * dt_e
    dtx_b = dtx.astype(BF16)
    xdec_b = (dtx * dend_e).astype(BF16)
    side()

    lane = lax.broadcasted_iota(jnp.int32, (q, LANES), 1)
    zero_b = jnp.zeros((q, LANES), BF16)
    y_tiles = []
    for g in range(SSD_GROUPS):
        bg = bm[g]
        cg = cm[g]
        cb = _dot_nt(cg, bg)
        gs = slice(g * gw, (g + 1) * gw)
        st_g = state[:, gs]
        y_off = _dot(cg, st_g.astype(BF16)) * ea_e[:, gs]
        for pr in range(hpg // 2):
            tile = g * (hpg // 2) + pr
            ms = []
            for h in (2 * tile, 2 * tile + 1):
                seg = acum[:, h:h + 1] - acum_t[h:h + 1, :]
                lmat = jnp.exp(jnp.where(causal, seg, -jnp.inf))
                ms.append((cb * lmat).astype(BF16))
            dtile = dtx_b[:, tile * LANES:(tile + 1) * LANES]
            rhs = jnp.concatenate([jnp.where(lane < SSD_HEAD_DIM, dtile, zero_b),
                                   jnp.where(lane >= SSD_HEAD_DIM, dtile, zero_b)], axis=0)
            y_tiles.append(_dot(jnp.concatenate(ms, axis=1), rhs)
                           + y_off[:, pr * LANES:(pr + 1) * LANES])
            if pr % 2 == 1:
                side()
        new = _dot_tn(bg, xdec_b[:, gs])
        state[:, gs] = st_g * ea_e[q - 1:q, gs] + new
    y = jnp.concatenate(y_tiles, axis=1) + dexp_ref[...] * xs
    y = y * _silu(z)
    ms = jnp.mean(y * y, axis=-1, keepdims=True)
    side(len(jobs))
    y_ref[...] = (y * lax.rsqrt(ms + EPS) * gn_ref[...]).astype(y_ref.dtype)

    @pl.when(last_chunk)
    def _():
        conv_ref[...] = x[q - (CONV_W - 1):q, :]
        ssm_ref[...] = state[...].T.reshape(ssm_ref.shape)


def _ssd_prompt_body(wins, x_ref, g_ref, w_ref, cw_ref, cb_ref, dtb_ref, alog_ref, dexp_ref, gn_ref,
                     e2_ref, y_ref, conv_ref, ssm_ref, za, xa, da, zb, xb, db, xpad, state):
    i = pl.program_id(0)
    j = pl.program_id(1)
    nc = pl.num_programs(1) - 1

    @pl.when(jnp.logical_and(i == 0, j == 0))
    def _():
        for r in (zb, xb, db):
            r[...] = jnp.zeros(r.shape, F32)

    @pl.when(j <= 1)
    def _():
        xpad[:, 0:8, :] = jnp.zeros((xpad.shape[0], 8, LANES), F32)
        state[...] = jnp.zeros(state.shape, F32)

    def step(wr, rd):
        xin = x_ref[...]
        msx = jnp.mean(xin * xin, axis=-1, keepdims=True)
        hbox = []

        def proj_piece(dst, base, lo, hi):
            def run():
                if not hbox:
                    hbox.append((xin * lax.rsqrt(msx + EPS) * g_ref[...]).astype(BF16))
                dst[:, lo:hi] = _dot(hbox[0], w_ref[:, base + lo:base + hi])
            return run

        pieces = []
        for dst, (w_lo, w_hi) in ((wr[0], wins[0]), (wr[2], wins[2]), (wr[1], wins[1])):
            n = w_hi - w_lo
            pieces += [proj_piece(dst, w_lo, lo, min(lo + PROJ_PIECE, n)) for lo in range(0, n, PROJ_PIECE)]
        _ssd_chunk_math(rd[0][...], rd[1][...], rd[2][...], j == nc, cw_ref, cb_ref, dtb_ref, alog_ref, dexp_ref,
                        gn_ref, e2_ref, y_ref, conv_ref, ssm_ref, xpad, state, pieces)

    @pl.when(j % 2 == 0)
    def _():
        step((za, xa, da), (zb, xb, db))

    @pl.when(j % 2 == 1)
    def _():
        step((zb, xb, db), (za, xa, da))


def _ssd_prompt(x, ln_g, w_in, cw, cb, dtb, alog, dexp, gn):
    b, l, d = x.shape
    conv_dim = cw.shape[1]
    d_ssd = conv_dim - 2 * SSD_GROUPS * SSD_STATE
    wins = _proj_windows(d_ssd, conv_dim)
    n_heads = d_ssd // SSD_HEAD_DIM
    q = SSD_CHUNK
    nc = l // q
    assert SSD_STATE == LANES and conv_dim == d_ssd + 2 * SSD_GROUPS * SSD_STATE
    nct = conv_dim // LANES
    cw4 = cw.reshape(CONV_W, nct, 1, LANES)
    cb3 = cb.reshape(nct, 1, LANES)
    e2 = (jnp.arange(2 * LANES)[:, None] % LANES == jnp.arange(d_ssd)[None, :] // SSD_HEAD_DIM).astype(BF16)
    slot = lambda n: pltpu.VMEM((q, n), F32)
    return pl.pallas_call(
        functools.partial(_ssd_prompt_body, wins),
        grid=(b, nc + 1),
        in_specs=[pl.BlockSpec((None, q, d), lambda i, j: (i, jnp.minimum(j, nc - 1), 0)),
                  _const_spec(ln_g.shape), _const_spec(w_in.shape),
                  _const_spec(cw4.shape), _const_spec(cb3.shape), _const_spec(dtb.shape), _const_spec(alog.shape),
                  _const_spec(dexp.shape), _const_spec(gn.shape), _const_spec(e2.shape)],
        out_specs=[pl.BlockSpec((None, q, d_ssd), lambda i, j: (i, jnp.maximum(j - 1, 0), 0)),
                   pl.BlockSpec((None, CONV_W - 1, conv_dim), lambda i, j: (i, 0, 0)),
                   pl.BlockSpec((None, n_heads, SSD_HEAD_DIM, SSD_STATE), lambda i, j: (i, 0, 0, 0))],
        out_shape=[jax.ShapeDtypeStruct((b, l, d_ssd), BF16),
                   jax.ShapeDtypeStruct((b, CONV_W - 1, conv_dim), F32),
                   jax.ShapeDtypeStruct((b, n_heads, SSD_HEAD_DIM, SSD_STATE), F32)],
        scratch_shapes=[slot(d_ssd), slot(conv_dim), slot(LANES), slot(d_ssd), slot(conv_dim), slot(LANES),
                        pltpu.VMEM((nct, q + 8, LANES), F32), pltpu.VMEM((SSD_STATE, d_ssd), F32)],
        compiler_params=_params("arbitrary", "arbitrary"),
        name="ssd_prompt",
    )(x, ln_g, w_in, cw4, cb3, dtb, alog, dexp, gn, e2)


def _ssd_step_body(z_ref, xbc_ref, dt_ref, cs_ref, st_ref, cw_ref, cb_ref, dtb_ref, alog_ref, dexp_ref, gn_ref,
                   y_ref, conv_ref, ssm_ref):
    bt, d_ssd = z_ref.shape
    n_heads = d_ssd // SSD_HEAD_DIM
    hpg = n_heads // SSD_GROUPS
    x = xbc_ref[...]
    cw = cw_ref[...]
    conv = cw[0:1] * cs_ref[0] + cw[1:2] * cs_ref[1] + cw[2:3] * cs_ref[2] + cw[3:4] * x + cb_ref[...]
    conv_ref[0] = cs_ref[1]
    conv_ref[1] = cs_ref[2]
    conv_ref[2] = x
    xact = _silu(conv)
    xs = xact[:, :d_ssd]
    bm = xact[:, d_ssd:d_ssd + SSD_GROUPS * SSD_STATE]
    cm = xact[:, d_ssd + SSD_GROUPS * SSD_STATE:]
    dt = _softplus(dt_ref[...] + dtb_ref[...])
    da = jnp.exp(dt * (-jnp.exp(alog_ref[...])))
    dtx = xs * _expand_heads(dt, n_heads)
    pad = jnp.zeros((LANES - bt, d_ssd), F32)
    to_cols = lambda v: jnp.concatenate([v, pad], axis=0).T
    dtx_t = to_cols(dtx)
    da_t = to_cols(_expand_heads(da, n_heads))
    lane = lax.broadcasted_iota(jnp.int32, (d_ssd, LANES), 1)
    gw = hpg * SSD_HEAD_DIM
    rows_of = lambda v, b: jnp.concatenate(
        [jnp.broadcast_to(v[b:b + 1, g * SSD_STATE:(g + 1) * SSD_STATE], (gw, SSD_STATE)) for g in range(SSD_GROUPS)],
        axis=0)
    y_t = jnp.zeros((d_ssd, LANES), F32)
    for b in range(bt):
        s0 = st_ref[b].reshape(d_ssd, SSD_STATE)
        s_new = s0 * da_t[:, b:b + 1] + dtx_t[:, b:b + 1] * rows_of(bm, b)
        ssm_ref[b] = s_new.reshape(n_heads, SSD_HEAD_DIM, SSD_STATE)
        ycol = jnp.sum(s_new * rows_of(cm, b), axis=1, keepdims=True)
        y_t = jnp.where(lane == b, ycol, y_t)
    y = y_t.T[:bt, :] + dexp_ref[...] * xs
    y = y * _silu(z_ref[...])
    y_ref[...] = _rms(y, gn_ref[...]).astype(y_ref.dtype)


def _ssd_step(z, xbc, dt, conv_state_t, ssm_state, cw, cb, dtb, alog, dexp, gn):
    nb, d_ssd = z.shape
    conv_dim = xbc.shape[-1]
    n_heads = d_ssd // SSD_HEAD_DIM
    bt = 8
    row = lambda n: pl.BlockSpec((bt, n), lambda i: (i, 0))
    cs_spec = pl.BlockSpec((CONV_W - 1, bt, conv_dim), lambda i: (0, i, 0))
    st_spec = pl.BlockSpec((bt, n_heads, SSD_HEAD_DIM, SSD_STATE), lambda i: (i, 0, 0, 0))
    return pl.pallas_call(
        _ssd_step_body,
        grid=(nb // bt,),
        in_specs=[row(d_ssd), row(conv_dim), row(LANES), cs_spec, st_spec, _const_spec(cw.shape),
                  _const_spec(cb.shape), _const_spec(dtb.shape), _const_spec(alog.shape), _const_spec(dexp.shape),
                  _const_spec(gn.shape)],
        out_specs=[row(d_ssd), cs_spec, st_spec],
        out_shape=[jax.ShapeDtypeStruct((nb, d_ssd), BF16),
                   jax.ShapeDtypeStruct((CONV_W - 1, nb, conv_dim), F32),
                   jax.ShapeDtypeStruct(ssm_state.shape, F32)],
        compiler_params=_params("arbitrary"),
        name="ssd_step",
    )(z, xbc, dt, conv_state_t, ssm_state, cw, cb, dtb, alog, dexp, gn)


def _split_bf16(x):
    hi = x.astype(BF16)
    lo = (x - hi.astype(F32)).astype(BF16)
    return hi, lo


def _s5_prep_body(q, ls_ref, lrp_ref, lip_ref, lr2_ref, li2_ref, br_ref, bi_ref, ccat_ref, ca_ref, cb_ref, rep_ref,
                  tt_ref, wt_ref, zt_ref, a1_ref, a2_ref):
    ng = ls_ref.shape[0]
    w = q * S5_CH
    step = jnp.exp(ls_ref[...])
    lr, li = lrp_ref[...], lip_ref[...]
    mag = jnp.exp(lr * step)
    ang = li * step
    lbr = mag * jnp.cos(ang)
    lbi = mag * jnp.sin(ang)
    den = lr * lr + li * li
    kr = ((lbr - 1.0) * lr + lbi * li) / den
    ki = (lbi * lr - (lbr - 1.0) * li) / den
    br, bi = br_ref[...], bi_ref[...]
    bbr = kr * br - ki * bi
    bbi = kr * bi + ki * br
    if q > 1:
        np_ = br.shape[1]
        rep = lambda v: jnp.dot(v.reshape(ng * np_, S5_CH), rep_ref[...], precision=lax.Precision.HIGHEST,
                                preferred_element_type=F32).reshape(ng, np_, w)
        bbr, bbi = rep(bbr), rep(bbi)
    if q == 1:
        ball = jnp.concatenate([bbr, bbi], axis=1)
    else:
        d = (lax.broadcasted_iota(jnp.int32, (1, 1, w), 2) // S5_CH).astype(F32)
        pm = jnp.exp(d * (lr * step))
        pa = d * ang
        pr = pm * jnp.cos(pa)
        pi = pm * jnp.sin(pa)
        ball = jnp.concatenate([pr * bbr - pi * bbi, pr * bbi + pi * bbr], axis=1)
    wt_ref[...] = ball.astype(BF16)

    lane3 = lax.broadcasted_iota(jnp.int32, (1, 1, 2 * S5_STATE), 2)
    first = lane3 < S5_STATE
    csign = jnp.where(first, ccat_ref[...], -ccat_ref[...])
    ch, cl = _split_bf16(csign)
    bh, bl = _split_bf16(ball)
    bdot = lambda x, y: lax.dot_general(x, y, (((2,), (1,)), ((0,), (0,))), preferred_element_type=F32)
    kall = bdot(ch, bh) + bdot(ch, bl) + bdot(cl, bh)
    k2 = kall.reshape(ng * S5_CH, w)
    lane2 = lax.broadcasted_iota(jnp.int32, (ng * S5_CH, w), 1)
    for t in range(q):
        sh = (q - 1 - t) * S5_CH
        r = pltpu.roll(k2, sh, 1) if sh else k2
        r = jnp.where(lane2 >= sh, r, 0.0)
        tt_ref[:, t * S5_CH:(t + 1) * S5_CH, :] = r.reshape(ng, S5_CH, w).astype(BF16)

    lr2, li2 = lr2_ref[...], li2_ref[...]
    t1 = (lax.broadcasted_iota(jnp.int32, (1, q, 1), 1) + 1).astype(F32)
    zm = jnp.exp(t1 * (lr2 * step))
    za = t1 * (li2 * step)
    zr = zm * jnp.cos(za)
    zi = zm * jnp.sin(za)
    ca, cb = ca_ref[...], cb_ref[...]
    for t in range(q):
        prt = zr[:, t:t + 1, :]
        pit = zi[:, t:t + 1, :]
        zt = jnp.where(first, ca * prt - cb * pit, -(ca * pit) - cb * prt)
        zt_ref[:, t * S5_CH:(t + 1) * S5_CH, :] = zt.astype(BF16)
    qf = float(q)
    mq = jnp.exp(qf * (lr2 * step))
    aq = qf * (li2 * step)
    ar = mq * jnp.cos(aq)
    ai = mq * jnp.sin(aq)
    a1_ref[...] = ar
    a2_ref[...] = jnp.where(first, -ai, ai)


def _s5_prep(q, lam_re, lam_im, log_step, b_re, b_im, c_re, c_im):
    ng, p = lam_re.shape
    w = q * S5_CH
    ls = log_step.reshape(ng, 1, 1)
    lrp = lam_re.reshape(ng, p, 1)
    lip = lam_im.reshape(ng, p, 1)
    lr2 = jnp.concatenate([lam_re, lam_re], axis=-1).reshape(ng, 1, 2 * p)
    li2 = jnp.concatenate([lam_im, lam_im], axis=-1).reshape(ng, 1, 2 * p)
    rep = (jnp.arange(S5_CH)[:, None] == jnp.arange(w)[None, :] % S5_CH).astype(F32)
    ccat = jnp.concatenate([c_re, c_im], axis=-1)
    ca = jnp.concatenate([c_re, c_re], axis=-1)
    cb = jnp.concatenate([c_im, c_im], axis=-1)
    gb = ng if q == 1 else 16
    blk = lambda a, b: pl.BlockSpec((gb, a, b), lambda i: (i, 0, 0))
    return pl.pallas_call(
        functools.partial(_s5_prep_body, q),
        grid=(ng // gb,),
        in_specs=[blk(1, 1), blk(p, 1), blk(p, 1), blk(1, 2 * p), blk(1, 2 * p), blk(p, S5_CH), blk(p, S5_CH),
                  blk(S5_CH, 2 * p), blk(S5_CH, 2 * p), blk(S5_CH, 2 * p), _const_spec(rep.shape)],
        out_specs=[blk(w, w), blk(2 * p, w), blk(w, 2 * p), blk(1, 2 * p), blk(1, 2 * p)],
        out_shape=[jax.ShapeDtypeStruct((ng, w, w), BF16), jax.ShapeDtypeStruct((ng, 2 * p, w), BF16),
                   jax.ShapeDtypeStruct((ng, w, 2 * p), BF16), jax.ShapeDtypeStruct((ng, 1, 2 * p), F32),
                   jax.ShapeDtypeStruct((ng, 1, 2 * p), F32)],
        compiler_params=_params("arbitrary"),
        name=f"s5_prep_q{q}",
    )(ls, lrp, lip, lr2, li2, b_re, b_im, ccat, ca, cb, rep)


def _s5_mix_body(q, nb, nkb, npi, npt, perm, x_ref, g_ref, wut_ref, tt_ref, wt_ref, zt_ref, a1_ref, a2_ref, s0_ref,
                 dcol_ref, wglut_ref, bglu_ref, wo_ref, p_ref, sfin_ref, ut, yt, carry):
    ng = tt_ref.shape[0]
    nk = nkb // nb
    qi = q // npi
    tile = pl.program_id(0)
    ph = pl.program_id(1)

    @pl.when(jnp.logical_and(tile == 0, ph == 0))
    def _():
        carry[...] = s0_ref[...]

    def rows_of(r):
        if not perm:
            return x_ref[r]
        nseq, nblk, ndt, nr, _ = x_ref.shape
        flat = x_ref.reshape(nseq * nblk * ndt * nr, LANES)
        return jnp.concatenate(
            [jnp.concatenate([flat[pl.ds((k * ndt + dt) * nr + r, nseq, stride=nblk * ndt * nr), :]
                              for dt in range(ndt)], axis=1) for k in range(nblk)], axis=0)

    for hh in range(npi):
        @pl.when(ph == hh)
        def _():
            step = 2 if qi % 2 == 0 else 1
            for r in range(0, qi, step):
                xin = jnp.concatenate([rows_of(r + s) for s in range(step)], axis=0)
                u = _dot_nt(wut_ref[...], _rms(xin, g_ref[...]).astype(BF16))
                sl = slice((hh * qi + r) * nkb, (hh * qi + r + step) * nkb)
                ut[:, sl] = u.astype(BF16)
                yt[:, sl] = dcol_ref[...] * u

    @pl.when(ph == npi - 1)
    def _():
        gu = S5_GROUP_BATCH
        bdot = lambda a, b: lax.dot_general(a, b, (((2,), (1,)), ((0,), (0,))), preferred_element_type=F32)
        bdot_nt = lambda a, b: lax.dot_general(a, b, (((2,), (2,)), ((0,), (0,))), preferred_element_type=F32)

        def groups(i, _):
            g0 = pl.multiple_of(i * gu, gu)
            r0 = pl.multiple_of(i * (gu * S5_CH), gu * S5_CH)
            gsl = pl.ds(g0, gu)
            rows = ut[pl.ds(r0, gu * S5_CH), :].reshape(gu, S5_CH, q * nkb)
            ugt = jnp.concatenate([rows[:, :, (q - 1 - j) * nkb:(q - j) * nkb] for j in range(q)], axis=1)
            y = bdot(tt_ref[gsl], ugt)
            vt = bdot(wt_ref[gsl], ugt)
            v = jnp.swapaxes(vt, 1, 2)
            v_sw = jnp.swapaxes(jnp.concatenate([vt[:, S5_STATE:], vt[:, :S5_STATE]], axis=1), 1, 2)
            a1 = a1_ref[gsl]
            a2 = a2_ref[gsl]
            s = carry[gsl]
            s_sw = pltpu.roll(s.reshape(gu * nb, 2 * S5_STATE), S5_STATE, 1).reshape(s.shape)
            prev = []
            for k in range(nk):
                prev.append(s)
                s, s_sw = (a1 * s + a2 * s_sw + v[:, k * nb:(k + 1) * nb, :],
                           a1 * s_sw - a2 * s + v_sw[:, k * nb:(k + 1) * nb, :])
            carry[gsl] = s
            sprev = jnp.concatenate(prev, axis=1).astype(BF16)
            y = y + bdot_nt(zt_ref[gsl], sprev)
            for t in range(q):
                yt[pl.ds(r0, gu * S5_CH), t * nkb:(t + 1) * nkb] += (
                    y[:, t * S5_CH:(t + 1) * S5_CH, :].reshape(gu * S5_CH, nkb))
            return 0

        lax.fori_loop(0, ng // gu, groups, 0)
        sfin_ref[...] = carry[...]

    ch = (q // npt) * nkb
    for j in range(npt):
        @pl.when(ph == npi + j)
        def _():
            gt = _gelu_tanh(yt[:, j * ch:(j + 1) * ch])
            gate = _dot(wglut_ref[...], gt.astype(BF16)) + bglu_ref[...]
            y5 = (gt * _sigmoid(gate)).astype(BF16)
            p_ref[...] = _dot_tn(y5, wo_ref[...]).reshape(p_ref.shape)


def _s5_mix(q, nb, x, ln_g, wut, tt, wt, zt, a1, a2, s0, dcol, wglut, bglu, wo):
    nseq, l, d = x.shape
    nblk = l // q
    nlt = nblk * nseq
    nkb = min(LANES, nlt)
    ntile = nlt // nkb
    tok = q * nkb
    npt = max(1, tok // TOKEN_TILE)
    dm = wo.shape[1]
    perm = q > 1
    if perm:
        half = 8
        npi = q // half
        assert nb == nseq and d % LANES == 0 and q % half == 0
        xv = x.reshape(nseq, nblk, npi, half, d // LANES, LANES).transpose(0, 1, 2, 4, 3, 5)
        x_spec = pl.BlockSpec((nseq, nkb // nseq, None, d // LANES, half, LANES),
                              lambda i, j: (0, i, jnp.minimum(j, npi - 1), 0, 0, 0))
    else:
        npi = 1
        xv = x.reshape(1, nseq, d)
        x_spec = pl.BlockSpec((1, nkb, d), lambda i, j: (0, i, 0))
    body = functools.partial(_s5_mix_body, q, nb, nkb, npi, npt, perm)
    p_spec = pl.BlockSpec((q // npt, nkb, dm), lambda i, j: (jnp.maximum(j - npi, 0), i, 0))
    return pl.pallas_call(
        body,
        grid=(ntile, npi + npt),
        in_specs=[x_spec, _const_spec(ln_g.shape), _const_spec(wut.shape), _const_spec(tt.shape),
                  _const_spec(wt.shape), _const_spec(zt.shape), _const_spec(a1.shape), _const_spec(a2.shape),
                  _const_spec(s0.shape), _const_spec(dcol.shape), _const_spec(wglut.shape),
                  _const_spec(bglu.shape), _const_spec(wo.shape)],
        out_specs=[p_spec, pl.BlockSpec(s0.shape, lambda i, j: (0, 0, 0))],
        out_shape=[jax.ShapeDtypeStruct((q, nlt, dm), F32), jax.ShapeDtypeStruct(s0.shape, F32)],
        scratch_shapes=[pltpu.VMEM((d, tok), BF16), pltpu.VMEM((d, tok), F32), pltpu.VMEM(s0.shape, F32)],
        compiler_params=_params("arbitrary", "arbitrary"),
        name=f"s5_mix_q{q}",
    )(xv, ln_g, wut, tt, wt, zt, a1, a2, s0, dcol, wglut, bglu, wo)


def _kv_body(m_ref, wk_ref, wv_ref, k_ref, v_ref, kb_ref, vb_ref):
    tm, rows, _ = k_ref.shape
    nt = rows // X_HEADS
    m = m_ref[...].astype(BF16)
    for w_ref, o_ref, ob_ref in ((wk_ref, k_ref, kb_ref), (wv_ref, v_ref, vb_ref)):
        r = _dot(m, w_ref[...])
        ob_ref[...] = r.astype(BF16)
        flat = o_ref.reshape(tm * rows, LANES)
        for h in range(X_HEADS):
            for dt in range(nt):
                c = (h * nt + dt) * LANES
                flat[pl.ds(dt * X_HEADS + h, tm, stride=rows), :] = r[:, c:c + LANES]


def _kv_proj(mem2d, wk, wv):
    t, d = mem2d.shape
    tm = min(TOKEN_TILE, t)
    rows = d // LANES
    row = pl.BlockSpec((tm, d), lambda i: (i, 0))
    tile = pl.BlockSpec((tm, rows, LANES), lambda i: (i, 0, 0))
    return pl.pallas_call(
        _kv_body, grid=(t // tm,),
        in_specs=[row, _const_spec(wk.shape), _const_spec(wv.shape)],
        out_specs=[tile, tile, row, row],
        out_shape=[jax.ShapeDtypeStruct((t, rows, LANES), F32)] * 2 + [jax.ShapeDtypeStruct((t, d), BF16)] * 2,
        compiler_params=_params("arbitrary"),
        name="kv_proj",
    )(mem2d, wk, wv)


def _tile_unview_kv(kv8, b, m):
    nt = kv8.shape[1] // X_HEADS
    return kv8.reshape(b, m, nt, X_HEADS, LANES).transpose(0, 1, 3, 2, 4).reshape(b, m, X_HEADS, nt * LANES)


def _attn_prompt_body(x_ref, ys_ref, p_ref, wo1_ref, g2_ref, wq_ref, k_ref, v_ref, wxo_ref, o_ref):
    d = x_ref.shape[-1]
    hd = d // X_HEADS
    x1 = x_ref[...] + _dot(ys_ref[...], wo1_ref[...]) + p_ref[...]
    hq = _rms(x1, g2_ref[...]).astype(BF16)
    qv = _dot(hq, wq_ref[...]).astype(BF16)
    kb = k_ref[...]
    vb = v_ref[...]
    outs = []
    for h in range(X_HEADS):
        sl = slice(h * hd, (h + 1) * hd)
        s = _dot_nt(qv[:, sl], kb[:, sl]) * (hd ** -0.5)
        e = jnp.exp(s - jnp.max(s, axis=-1, keepdims=True))
        p = (e / jnp.sum(e, axis=-1, keepdims=True)).astype(BF16)
        outs.append(_dot(p, vb[:, sl]))
    o = jnp.concatenate(outs, axis=1).astype(BF16)
    o_ref[...] = x1 + _dot(o, wxo_ref[...])


def _attn_prompt(x, ys, pm, wo1, g2, wq, mk, mv, wxo):
    b, l, d = x.shape
    nm = mk.shape[1]
    tm = min(TOKEN_TILE, l)
    row = pl.BlockSpec((None, tm, d), lambda i, j: (i, j, 0))
    kv = pl.BlockSpec((None, nm, d), lambda i, j: (i, 0, 0))
    return pl.pallas_call(
        _attn_prompt_body, grid=(b, l // tm),
        in_specs=[row, row, row, _const_spec(wo1.shape), _const_spec(g2.shape), _const_spec(wq.shape), kv, kv,
                  _const_spec(wxo.shape)],
        out_specs=row,
        out_shape=jax.ShapeDtypeStruct((b, l, d), F32),
        compiler_params=_params("arbitrary", "arbitrary"),
        name="attn_prompt",
    )(x, ys, pm, wo1, g2, wq, mk, mv, wxo)


def _merge_q_body(x_ref, ys_ref, p_ref, wo1_ref, g2_ref, wq_ref, x1_ref, q_ref):
    x1 = x_ref[...] + _dot(ys_ref[...], wo1_ref[...]) + p_ref[...]
    x1_ref[...] = x1
    q_ref[...] = _dot(_rms(x1, g2_ref[...]).astype(BF16), wq_ref[...])


def _merge_q(x2d, ys, pm, wo1, g2, wq):
    t, d = x2d.shape
    return pl.pallas_call(
        _merge_q_body,
        out_shape=[jax.ShapeDtypeStruct((t, d), F32)] * 2,
        compiler_params=pltpu.CompilerParams(vmem_limit_bytes=VMEM_LIMIT),
        name="merge_q",
    )(x2d, ys, pm, wo1, g2, wq)


def _tile_view(kv):
    b, m, nh, hd = kv.shape
    nt = hd // LANES
    return kv.reshape(b, m, nh, nt, LANES).transpose(0, 1, 3, 2, 4).reshape(b, m, nt * nh, LANES)


def _tile_unview(o):
    b, rows, _ = o.shape
    nt = rows // X_HEADS
    return o.reshape(b, nt, X_HEADS, LANES).transpose(0, 2, 1, 3).reshape(b, rows * LANES)


def _attn_step_body(q_ref, k_ref, v_ref, o_ref):
    bt, nm, rows, _ = k_ref.shape
    hd = rows * LANES // X_HEADS
    for b in range(bt):
        r = jnp.sum(k_ref[b] * q_ref[b], axis=-1, keepdims=True)
        s = (r + pltpu.roll(r, X_HEADS, 1)) * (hd ** -0.5)
        e = jnp.exp(s - jnp.max(s, axis=0, keepdims=True))
        p = e / jnp.sum(e, axis=0, keepdims=True)
        o_ref[b] = jnp.sum(p * v_ref[b], axis=0)


def _attn_step(q8, k8, v8):
    nbt, rows, _ = q8.shape
    nm = k8.shape[1]
    assert rows == 2 * X_HEADS, "score assembly assumes two 128-lane tiles per head"
    bt = 4
    qs = pl.BlockSpec((bt, rows, LANES), lambda i: (i, 0, 0))
    kv = pl.BlockSpec((bt, nm, rows, LANES), lambda i: (i, 0, 0, 0))
    return pl.pallas_call(
        _attn_step_body, grid=(nbt // bt,),
        in_specs=[qs, kv, kv], out_specs=qs,
        out_shape=jax.ShapeDtypeStruct((nbt, rows, LANES), F32),
        compiler_params=_params("arbitrary"),
        name="attn_step",
    )(q8, k8, v8)


def _ffn_body(has_o, final, *refs):
    if has_o:
        x_ref, o_in_ref, wxo_ref, g3_ref, wg_ref, wu_ref, wd_ref, gf_ref, y_ref = refs
        x2 = x_ref[...] + _dot(o_in_ref[...].astype(BF16), wxo_ref[...])
    else:
        x_ref, g3_ref, wg_ref, wu_ref, wd_ref, gf_ref, y_ref = refs
        x2 = x_ref[...]
    hf = _rms(x2, g3_ref[...]).astype(BF16)
    act = (_silu(_dot(hf, wg_ref[...])) * _dot(hf, wu_ref[...])).astype(BF16)
    x3 = x2 + _dot(act, wd_ref[...])
    y_ref[...] = _rms(x3, gf_ref[...]) if final else x3


def _ffn(x2d, g3, wg, wu, wd, gf, final, o_in=None, wxo=None):
    t, d = x2d.shape
    tm = min(TOKEN_TILE, t)
    row = pl.BlockSpec((tm, d), lambda i: (i, 0))
    has_o = o_in is not None
    ins = [x2d] + ([o_in, wxo] if has_o else []) + [g3, wg, wu, wd, gf]
    specs = [row] + ([row, _const_spec(wxo.shape)] if has_o else []) + [
        _const_spec(g3.shape), _const_spec(wg.shape), _const_spec(wu.shape), _const_spec(wd.shape),
        _const_spec(gf.shape)]
    return pl.pallas_call(
        functools.partial(_ffn_body, has_o, final), grid=(t // tm,),
        in_specs=specs, out_specs=row,
        out_shape=jax.ShapeDtypeStruct((t, d), F32),
        compiler_params=_params("arbitrary"),
        name="ffn_o" if has_o else "ffn",
    )(*ins)


def _layer_weights(i, ln1_g, w_in, conv_w, conv_b, dt_bias, a_log, d_ssd, gn_g, lam_re, lam_im, log_step,
                   b_re, b_im, c_re, c_im, d_s5, w_glu, b_glu, w_out, ln2_g, w_xq, w_xk, w_xv, w_xo, ln3_g,
                   w_gate, w_up, w_down):
    d = w_in.shape[1]
    n_heads = dt_bias.shape[1]
    dssd = n_heads * SSD_HEAD_DIM
    conv_dim = conv_w.shape[2]
    o1, o2, o3 = dssd, dssd + conv_dim, dssd + conv_dim + n_heads
    win = w_in[i]
    w = {}
    w["ln1"] = ln1_g[i].reshape(1, d)
    w["win"] = win.astype(BF16)
    w["dims"] = (dssd, conv_dim)
    w["wut"] = win[:, o3:].T.astype(BF16)
    w["cw"] = conv_w[i]
    w["cb"] = conv_b[i].reshape(1, conv_dim)
    w["dtb"] = jnp.pad(dt_bias[i], (0, LANES - n_heads)).reshape(1, LANES)
    w["alog"] = jnp.pad(a_log[i], (0, LANES - n_heads)).reshape(1, LANES)
    w["dexp"] = jnp.repeat(d_ssd[i], SSD_HEAD_DIM).reshape(1, dssd)
    w["gn"] = gn_g[i].reshape(1, dssd)
    w["s5"] = (lam_re[i], lam_im[i], log_step[i], b_re[i], b_im[i], c_re[i], c_im[i])
    ds5 = d_s5.shape[1]
    w["dcol"] = d_s5[i].reshape(ds5, 1)
    w["wglut"] = w_glu[i].T.astype(BF16)
    w["bglu"] = b_glu[i].reshape(ds5, 1)
    w["wo1"] = w_out[i, :dssd].astype(BF16)
    w["wo2"] = w_out[i, dssd:].astype(BF16)
    w["ln2"] = ln2_g[i].reshape(1, d)
    w["wq"] = w_xq[i].astype(BF16)
    w["wk"] = w_xk[i].astype(BF16)
    w["wv"] = w_xv[i].astype(BF16)
    w["wxo"] = w_xo[i].astype(BF16)
    w["ln3"] = ln3_g[i].reshape(1, d)
    w["wg"] = w_gate[i].astype(BF16)
    w["wu"] = w_up[i].astype(BF16)
    w["wdn"] = w_down[i].astype(BF16)
    return w


def _state_to_rows(s_re, s_im):
    return jnp.concatenate([s_re, s_im], axis=-1).transpose(1, 0, 2)


def _rows_to_state(s):
    p = s.shape[-1] // 2
    st = s.transpose(1, 0, 2)
    return st[..., :p], st[..., p:]


def kernel(x_prompt, x_sample, mem_prompt, state_conv, state_ssm, state_s5_re, state_s5_im, cache_mem_k, cache_mem_v, ln1_g, w_in, conv_w, conv_b, dt_bias, a_log, d_ssd, gn_g, lam_re, lam_im, log_step, b_re, b_im, c_re, c_im, d_s5, w_glu, b_glu, w_out, ln2_g, w_xq, w_xk, w_xv, w_xo, ln3_g, w_gate, w_up, w_down, final_g):
    bp, seq, d = x_prompt.shape
    bs = x_sample.shape[0]
    depth = w_in.shape[0]
    nm = mem_prompt.shape[1]
    ng, ns = lam_re.shape[1], lam_re.shape[2]
    gf = final_g.reshape(1, d)
    hp = x_prompt
    hs = x_sample.reshape(bs, d)
    outs = {k: [] for k in ("conv_p", "ssm_p", "re_p", "im_p", "mk_p", "mv_p", "conv_s", "ssm_s", "re_s", "im_s")}
    yp = ys_out = None
    for i in range(depth):
        w = _layer_weights(i, ln1_g, w_in, conv_w, conv_b, dt_bias, a_log, d_ssd, gn_g, lam_re, lam_im, log_step,
                           b_re, b_im, c_re, c_im, d_s5, w_glu, b_glu, w_out, ln2_g, w_xq, w_xk, w_xv, w_xo, ln3_g,
                           w_gate, w_up, w_down)
        last = i == depth - 1
        mk8, mv8, mk, mv = _kv_proj(mem_prompt.reshape(bp * nm, d), w["wk"], w["wv"])
        mk = mk.reshape(bp, nm, d)
        mv = mv.reshape(bp, nm, d)
        y_ssd, conv_p, ssm_p = _ssd_prompt(hp, w["ln1"], w["win"], w["cw"], w["cb"], w["dtb"], w["alog"], w["dexp"],
                                           w["gn"])
        q = S5_Q
        tt, wt, zt, a1, a2 = _s5_prep(q, *w["s5"])
        s0 = jnp.zeros((ng, bp, 2 * ns), F32)
        pperm, sfin = _s5_mix(q, bp, hp, w["ln1"], w["wut"], tt, wt, zt, a1, a2, s0, w["dcol"], w["wglut"],
                              w["bglu"], w["wo2"])
        pm = pperm.reshape(q, seq // q, bp, d).transpose(2, 1, 0, 3).reshape(bp, seq, d)
        re_p, im_p = _rows_to_state(sfin)
        x2 = _attn_prompt(hp, y_ssd, pm, w["wo1"], w["ln2"], w["wq"], mk, mv, w["wxo"])
        hp = _ffn(x2.reshape(bp * seq, d), w["ln3"], w["wg"], w["wu"], w["wdn"], gf, last).reshape(bp, seq, d)
        zs, xbcs, dts = _inproj(hs, w["ln1"], w["win"], *w["dims"])
        ys_ssd, conv_s_t, ssm_s = _ssd_step(zs, xbcs, dts, state_conv[i].transpose(1, 0, 2), state_ssm[i],
                                            w["cw"], w["cb"], w["dtb"], w["alog"], w["dexp"], w["gn"])
        tt1, wt1, zt1, a11, a21 = _s5_prep(1, *w["s5"])
        s0s = _state_to_rows(state_s5_re[i], state_s5_im[i])
        ps, sfin_s = _s5_mix(1, bs, hs.reshape(bs, 1, d), w["ln1"], w["wut"], tt1, wt1, zt1, a11, a21, s0s,
                             w["dcol"], w["wglut"], w["bglu"], w["wo2"])
        re_s, im_s = _rows_to_state(sfin_s)
        x1s, qs = _merge_q(hs, ys_ssd, ps.reshape(bs, d), w["wo1"], w["ln2"], w["wq"])
        o_s = _attn_step(_tile_view(qs.reshape(bs, 1, X_HEADS, d // X_HEADS))[:, 0], _tile_view(cache_mem_k[i]),
                         _tile_view(cache_mem_v[i]))
        hs = _ffn(x1s, w["ln3"], w["wg"], w["wu"], w["wdn"], gf, last, o_in=_tile_unview(o_s), wxo=w["wxo"])
        for k, v in (("conv_p", conv_p), ("ssm_p", ssm_p), ("re_p", re_p), ("im_p", im_p),
                     ("mk_p", _tile_unview_kv(mk8, bp, nm)), ("mv_p", _tile_unview_kv(mv8, bp, nm)),
                     ("conv_s", conv_s_t.transpose(1, 0, 2)), ("ssm_s", ssm_s), ("re_s", re_s), ("im_s", im_s)):
            outs[k].append(v)
    st = lambda k: jnp.stack(outs[k])
    return (hp, hs.reshape(bs, 1, d), st("conv_p"), st("ssm_p"), st("re_p"), st("im_p"), st("mk_p"), st("mv_p"),
            st("conv_s"), st("ssm_s"), st("re_s"), st("im_s"))
```

```python
import functools
import math

import jax
import jax.numpy as jnp
from jax import lax
from jax.experimental import pallas as pl
from jax.experimental.pallas import tpu as pltpu

F32 = jnp.float32
BF16 = jnp.bfloat16
EPS = 1e-6

LANES = 128
VMEM_LIMIT = 56 * 1024 * 1024

SSD_HEAD_DIM = 64
SSD_STATE = 128
SSD_GROUPS = 2
SSD_CHUNK = 128
CONV_W = 4
S5_CH = 16
S5_STATE = 64
S5_Q = 16
S5_GROUP_BATCH = 16
X_HEADS = 4
TOKEN_TILE = 512
PROJ_PIECE = 512
FF_CHUNK = 1536


def _const_spec(shape):
    nd = len(shape)
    return pl.BlockSpec(shape, lambda *_: (0,) * nd, pipeline_mode=pl.Buffered(1))


def _params(*sem):
    return pltpu.CompilerParams(dimension_semantics=sem, vmem_limit_bytes=VMEM_LIMIT)


def _rms(x, g):
    return x * lax.rsqrt(jnp.mean(x * x, axis=-1, keepdims=True) + EPS) * g


def _sigmoid(x):
    return 1.0 / (1.0 + jnp.exp(-x))


def _silu(x):
    return x * _sigmoid(x)


def _gelu_tanh(x):
    return 0.5 * x * (1.0 + jnp.tanh(math.sqrt(2.0 / math.pi) * (x + 0.044715 * (x * x * x))))


def _softplus(x):
    return jnp.maximum(x, 0.0) + jnp.log1p(jnp.exp(-jnp.abs(x)))


def _dot(a, b):
    return jnp.dot(a, b, preferred_element_type=F32)


def _dot_nt(a, b):
    return lax.dot_general(a, b, (((1,), (1,)), ((), ())), preferred_element_type=F32)


def _dot_tn(a, b):
    return lax.dot_general(a, b, (((0,), (0,)), ((), ())), preferred_element_type=F32)


def _proj_windows(d_ssd, conv_dim):
    assert d_ssd % LANES == 0 and conv_dim % LANES == 0
    return (0, d_ssd), (d_ssd, d_ssd + conv_dim), (d_ssd + conv_dim, d_ssd + conv_dim + LANES)


def _inproj_body(wins, x_ref, g_ref, w_ref, z_ref, xbc_ref, dt_ref):
    h = _rms(x_ref[...], g_ref[...]).astype(BF16)
    for (lo, hi), o_ref in zip(wins, (z_ref, xbc_ref, dt_ref)):
        o_ref[...] = _dot(h, w_ref[:, lo:hi])


def _inproj(x2d, ln_g, w_in, d_ssd, conv_dim):
    t, d = x2d.shape
    tm = min(TOKEN_TILE, t)
    wins = _proj_windows(d_ssd, conv_dim)
    row = lambda n: pl.BlockSpec((tm, n), lambda i: (i, 0))
    return pl.pallas_call(
        functools.partial(_inproj_body, wins),
        grid=(t // tm,),
        in_specs=[row(d), _const_spec((1, d)), _const_spec(w_in.shape)],
        out_specs=[row(hi - lo) for lo, hi in wins],
        out_shape=[jax.ShapeDtypeStruct((t, hi - lo), F32) for lo, hi in wins],
        compiler_params=_params("arbitrary"),
        name="inproj",
    )(x2d, ln_g, w_in)


def _expand_heads(v, n_heads):
    rows = v.shape[0]
    lane = lax.broadcasted_iota(jnp.int32, (rows, LANES), 1)
    pieces = [jnp.where(lane < SSD_HEAD_DIM, v[:, 2 * j:2 * j + 1], v[:, 2 * j + 1:2 * j + 2])
              for j in range(n_heads // 2)]
    return jnp.concatenate(pieces, axis=1)


def _ssd_chunk_math(z, x, dt_raw, last_chunk, cw_ref, cb_ref, dtb_ref, alog_ref, dexp_ref, gn_ref, e2_ref,
                    y_ref, conv_ref, ssm_ref, xpad, state, side_jobs=()):
    jobs = list(side_jobs)

    def side(n=1):
        for _ in range(n):
            if jobs:
                jobs.pop(0)()

    q = SSD_CHUNK
    d_ssd = z.shape[-1]
    n_heads = d_ssd // SSD_HEAD_DIM
    hpg = n_heads // SSD_GROUPS
    gw = hpg * SSD_HEAD_DIM
    nct = xpad.shape[0]
    xt = d_ssd // LANES

    for t in range(nct):
        xpad[t, 8:8 + q, :] = x[:, t * LANES:(t + 1) * LANES]
    cw = cw_ref[...]
    conv = (cw[3] * xpad[:, 8:8 + q, :] + cw[2] * xpad[:, 7:7 + q, :] + cw[1] * xpad[:, 6:6 + q, :]
            + cw[0] * xpad[:, 5:5 + q, :] + cb_ref[...])
    xpad[:, 0:8, :] = xpad[:, q:q + 8, :]
    xact = _silu(conv)
    xs = jnp.concatenate([xact[t] for t in range(xt)], axis=1)
    bm = xact[xt:xt + SSD_GROUPS].astype(BF16)
    cm = xact[xt + SSD_GROUPS:].astype(BF16)
    side()

    dt = _softplus(dt_raw + dtb_ref[...])
    a = -jnp.exp(alog_ref[...])
    row = lax.broadcasted_iota(jnp.int32, (q, q), 0)
    col = lax.broadcasted_iota(jnp.int32, (q, q), 1)
    causal = row >= col
    tri = jnp.where(causal, 1.0, 0.0).astype(F32)
    acum = jnp.dot(tri, dt * a, precision=lax.Precision.HIGHEST, preferred_element_type=F32)
    acum_t = acum.T
    last = acum[q - 1:q, :]

    def expand(v):
        hi = v.astype(BF16)
        lo = (v - hi.astype(F32)).astype(BF16)
        return _dot(jnp.concatenate([hi, lo], axis=1), e2_ref[...])

    dt_e = expand(dt)
    ea_e = expand(jnp.exp(acum))
    dend_e = expand(jnp.exp(last - acum))

    dtx = xs * dt_e
    dtx_b = dtx.astype(BF16)
    xdec_b = (dtx * dend_e).astype(BF16)
    side()

    lane = lax.broadcasted_iota(jnp.int32, (q, LANES), 1)
    zero_b = jnp.zeros((q, LANES), BF16)
    y_tiles = []
    for g in range(SSD_GROUPS):
        bg = bm[g]
        cg = cm[g]
        cb = _dot_nt(cg, bg)
        gs = slice(g * gw, (g + 1) * gw)
        st_g = state[:, gs]
        y_off = _dot(cg, st_g.astype(BF16)) * ea_e[:, gs]
        for pr in range(hpg // 2):
            tile = g * (hpg // 2) + pr
            ms = []
            for h in (2 * tile, 2 * tile + 1):
                seg = acum[:, h:h + 1] - acum_t[h:h + 1, :]
                lmat = jnp.exp(jnp.where(causal, seg, -jnp.inf))
                ms.append((cb * lmat).astype(BF16))
            dtile = dtx_b[:, tile * LANES:(tile + 1) * LANES]
            rhs = jnp.concatenate([jnp.where(lane < SSD_HEAD_DIM, dtile, zero_b),
                                   jnp.where(lane >= SSD_HEAD_DIM, dtile, zero_b)], axis=0)
            y_tiles.append(_dot(jnp.concatenate(ms, axis=1), rhs)
                           + y_off[:, pr * LANES:(pr + 1) * LANES])
            if pr % 2 == 1:
                side()
        new = _dot_tn(bg, xdec_b[:, gs])
        state[:, gs] = st_g * ea_e[q - 1:q, gs] + new
    y = jnp.concatenate(y_tiles, axis=1) + dexp_ref[...] * xs
    y = y * _silu(z)
    ms = jnp.mean(y * y, axis=-1, keepdims=True)
    side(len(jobs))
    y_ref[...] = (y * lax.rsqrt(ms + EPS) * gn_ref[...]).astype(y_ref.dtype)

    @pl.when(last_chunk)
    def _():
        conv_ref[...] = x[q - (CONV_W - 1):q, :]
        ssm_ref[...] = state[...].T.reshape(ssm_ref.shape)


def _ssd_prompt_body(wins, x_ref, g_ref, w_ref, cw_ref, cb_ref, dtb_ref, alog_ref, dexp_ref, gn_ref,
                     e2_ref, y_ref, conv_ref, ssm_ref, za, xa, da, zb, xb, db, xpad, state):
    i = pl.program_id(0)
    j = pl.program_id(1)
    nc = pl.num_programs(1) - 1

    @pl.when(jnp.logical_and(i == 0, j == 0))
    def _():
        for r in (zb, xb, db):
            r[...] = jnp.zeros(r.shape, F32)

    @pl.when(j <= 1)
    def _():
        xpad[:, 0:8, :] = jnp.zeros((xpad.shape[0], 8, LANES), F32)
        state[...] = jnp.zeros(state.shape, F32)

    def step(wr, rd):
        xin = x_ref[...]
        msx = jnp.mean(xin * xin, axis=-1, keepdims=True)
        hbox = []

        def proj_piece(dst, base, lo, hi):
            def run():
                if not hbox:
                    hbox.append((xin * lax.rsqrt(msx + EPS) * g_ref[...]).astype(BF16))
                dst[:, lo:hi] = _dot(hbox[0], w_ref[:, base + lo:base + hi])
            return run

        pieces = []
        for dst, (w_lo, w_hi) in ((wr[0], wins[0]), (wr[2], wins[2]), (wr[1], wins[1])):
            n = w_hi - w_lo
            pieces += [proj_piece(dst, w_lo, lo, min(lo + PROJ_PIECE, n)) for lo in range(0, n, PROJ_PIECE)]
        _ssd_chunk_math(rd[0][...], rd[1][...], rd[2][...], j == nc, cw_ref, cb_ref, dtb_ref, alog_ref, dexp_ref,
                        gn_ref, e2_ref, y_ref, conv_ref, ssm_ref, xpad, state, pieces)

    @pl.when(j % 2 == 0)
    def _():
        step((za, xa, da), (zb, xb, db))

    @pl.when(j % 2 == 1)
    def _():
        step((zb, xb, db), (za, xa, da))


def _ssd_prompt(x, ln_g, w_in, cw, cb, dtb, alog, dexp, gn):
    b, l, d = x.shape
    conv_dim = cw.shape[1]
    d_ssd = conv_dim - 2 * SSD_GROUPS * SSD_STATE
    wins = _proj_windows(d_ssd, conv_dim)
    n_heads = d_ssd // SSD_HEAD_DIM
    q = SSD_CHUNK
    nc = l // q
    assert SSD_STATE == LANES and conv_dim == d_ssd + 2 * SSD_GROUPS * SSD_STATE
    nct = conv_dim // LANES
    cw4 = cw.reshape(CONV_W, nct, 1, LANES)
    cb3 = cb.reshape(nct, 1, LANES)
    e2 = (jnp.arange(2 * LANES)[:, None] % LANES == jnp.arange(d_ssd)[None, :] // SSD_HEAD_DIM).astype(BF16)
    slot = lambda n: pltpu.VMEM((q, n), F32)
    return pl.pallas_call(
        functools.partial(_ssd_prompt_body, wins),
        grid=(b, nc + 1),
        in_specs=[pl.BlockSpec((None, q, d), lambda i, j: (i, jnp.minimum(j, nc - 1), 0)),
                  _const_spec(ln_g.shape), _const_spec(w_in.shape),
                  _const_spec(cw4.shape), _const_spec(cb3.shape), _const_spec(dtb.shape), _const_spec(alog.shape),
                  _const_spec(dexp.shape), _const_spec(gn.shape), _const_spec(e2.shape)],
        out_specs=[pl.BlockSpec((None, q, d_ssd), lambda i, j: (i, jnp.maximum(j - 1, 0), 0)),
                   pl.BlockSpec((None, CONV_W - 1, conv_dim), lambda i, j: (i, 0, 0)),
                   pl.BlockSpec((None, n_heads, SSD_HEAD_DIM, SSD_STATE), lambda i, j: (i, 0, 0, 0))],
        out_shape=[jax.ShapeDtypeStruct((b, l, d_ssd), BF16),
                   jax.ShapeDtypeStruct((b, CONV_W - 1, conv_dim), F32),
                   jax.ShapeDtypeStruct((b, n_heads, SSD_HEAD_DIM, SSD_STATE), F32)],
        scratch_shapes=[slot(d_ssd), slot(conv_dim), slot(LANES), slot(d_ssd), slot(conv_dim), slot(LANES),
                        pltpu.VMEM((nct, q + 8, LANES), F32), pltpu.VMEM((SSD_STATE, d_ssd), F32)],
        compiler_params=_params("arbitrary", "arbitrary"),
        name="ssd_prompt",
    )(x, ln_g, w_in, cw4, cb3, dtb, alog, dexp, gn, e2)


def _ssd_step_body(z_ref, xbc_ref, dt_ref, cs_ref, st_ref, cw_ref, cb_ref, dtb_ref, alog_ref, dexp_ref, gn_ref,
                   y_ref, conv_ref, ssm_ref):
    bt, d_ssd = z_ref.shape
    n_heads = d_ssd // SSD_HEAD_DIM
    hpg = n_heads // SSD_GROUPS
    x = xbc_ref[...]
    cw = cw_ref[...]
    conv = cw[0:1] * cs_ref[0] + cw[1:2] * cs_ref[1] + cw[2:3] * cs_ref[2] + cw[3:4] * x + cb_ref[...]
    conv_ref[0] = cs_ref[1]
    conv_ref[1] = cs_ref[2]
    conv_ref[2] = x
    xact = _silu(conv)
    xs = xact[:, :d_ssd]
    bm = xact[:, d_ssd:d_ssd + SSD_GROUPS * SSD_STATE]
    cm = xact[:, d_ssd + SSD_GROUPS * SSD_STATE:]
    dt = _softplus(dt_ref[...] + dtb_ref[...])
    da = jnp.exp(dt * (-jnp.exp(alog_ref[...])))
    dtx = xs * _expand_heads(dt, n_heads)
    pad = jnp.zeros((LANES - bt, d_ssd), F32)
    to_cols = lambda v: jnp.concatenate([v, pad], axis=0).T
    dtx_t = to_cols(dtx)
    da_t = to_cols(_expand_heads(da, n_heads))
    lane = lax.broadcasted_iota(jnp.int32, (d_ssd, LANES), 1)
    gw = hpg * SSD_HEAD_DIM
    rows_of = lambda v, b: jnp.concatenate(
        [jnp.broadcast_to(v[b:b + 1, g * SSD_STATE:(g + 1) * SSD_STATE], (gw, SSD_STATE)) for g in range(SSD_GROUPS)],
        axis=0)
    y_t = jnp.zeros((d_ssd, LANES), F32)
    for b in range(bt):
        s0 = st_ref[b].reshape(d_ssd, SSD_STATE)
        s_new = s0 * da_t[:, b:b + 1] + dtx_t[:, b:b + 1] * rows_of(bm, b)
        ssm_ref[b] = s_new.reshape(n_heads, SSD_HEAD_DIM, SSD_STATE)
        ycol = jnp.sum(s_new * rows_of(cm, b), axis=1, keepdims=True)
        y_t = jnp.where(lane == b, ycol, y_t)
    y = y_t.T[:bt, :] + dexp_ref[...] * xs
    y = y * _silu(z_ref[...])
    y_ref[...] = _rms(y, gn_ref[...]).astype(y_ref.dtype)


def _ssd_step_rider(z, xbc, dt, conv_state_t, ssm_state, cw, cb, dtb, alog, dexp, gn, step_of, nsteps):
    nb, d_ssd = z.shape
    conv_dim = xbc.shape[-1]
    n_heads = d_ssd // SSD_HEAD_DIM
    assert nb % nsteps == 0
    bt = nb // nsteps
    row = lambda n: pl.BlockSpec((None, bt, n), lambda *g: (step_of(*g), 0, 0))
    cs_spec = pl.BlockSpec((CONV_W - 1, None, bt, conv_dim), lambda *g: (0, step_of(*g), 0, 0))
    st_spec = pl.BlockSpec((bt, n_heads, SSD_HEAD_DIM, SSD_STATE), lambda *g: (step_of(*g), 0, 0, 0))
    cst = lambda a: pl.BlockSpec(a.shape, lambda *g: (0,) * a.ndim, pipeline_mode=pl.Buffered(1))
    ins = (z.reshape(nsteps, bt, d_ssd), xbc.reshape(nsteps, bt, conv_dim), dt.reshape(nsteps, bt, LANES),
           conv_state_t.reshape(CONV_W - 1, nsteps, bt, conv_dim), ssm_state, cw, cb, dtb, alog, dexp, gn)
    specs = (row(d_ssd), row(conv_dim), row(LANES), cs_spec, st_spec, cst(cw), cst(cb), cst(dtb), cst(alog),
             cst(dexp), cst(gn))
    out_shapes = (jax.ShapeDtypeStruct((nsteps, bt, d_ssd), BF16),
                  jax.ShapeDtypeStruct((CONV_W - 1, nsteps, bt, conv_dim), F32),
                  jax.ShapeDtypeStruct(ssm_state.shape, F32))
    return (_ssd_step_body, ins, specs, (row(d_ssd), cs_spec, st_spec), out_shapes)


def _split_bf16(x):
    hi = x.astype(BF16)
    lo = (x - hi.astype(F32)).astype(BF16)
    return hi, lo


def _s5_prep_body(q, ls_ref, lrp_ref, lip_ref, lr2_ref, li2_ref, br_ref, bi_ref, ccat_ref, ca_ref, cb_ref, rep_ref,
                  tt_ref, wt_ref, zt_ref, a1_ref, a2_ref):
    ng = ls_ref.shape[0]
    w = q * S5_CH
    step = jnp.exp(ls_ref[...])
    lr, li = lrp_ref[...], lip_ref[...]
    mag = jnp.exp(lr * step)
    ang = li * step
    lbr = mag * jnp.cos(ang)
    lbi = mag * jnp.sin(ang)
    den = lr * lr + li * li
    kr = ((lbr - 1.0) * lr + lbi * li) / den
    ki = (lbi * lr - (lbr - 1.0) * li) / den
    br, bi = br_ref[...], bi_ref[...]
    bbr = kr * br - ki * bi
    bbi = kr * bi + ki * br
    if q > 1:
        np_ = br.shape[1]
        rep = lambda v: jnp.dot(v.reshape(ng * np_, S5_CH), rep_ref[...], precision=lax.Precision.HIGHEST,
                                preferred_element_type=F32).reshape(ng, np_, w)
        bbr, bbi = rep(bbr), rep(bbi)
    if q == 1:
        ball = jnp.concatenate([bbr, bbi], axis=1)
    else:
        d = (lax.broadcasted_iota(jnp.int32, (1, 1, w), 2) // S5_CH).astype(F32)
        pm = jnp.exp(d * (lr * step))
        pa = d * ang
        pr = pm * jnp.cos(pa)
        pi = pm * jnp.sin(pa)
        ball = jnp.concatenate([pr * bbr - pi * bbi, pr * bbi + pi * bbr], axis=1)
    wt_ref[...] = ball.astype(BF16)

    lane3 = lax.broadcasted_iota(jnp.int32, (1, 1, 2 * S5_STATE), 2)
    first = lane3 < S5_STATE
    csign = jnp.where(first, ccat_ref[...], -ccat_ref[...])
    ch, cl = _split_bf16(csign)
    bh, bl = _split_bf16(ball)
    bdot = lambda x, y: lax.dot_general(x, y, (((2,), (1,)), ((0,), (0,))), preferred_element_type=F32)
    kall = bdot(ch, bh) + bdot(ch, bl) + bdot(cl, bh)
    k2 = kall.reshape(ng * S5_CH, w)
    lane2 = lax.broadcasted_iota(jnp.int32, (ng * S5_CH, w), 1)
    for t in range(q):
        sh = (q - 1 - t) * S5_CH
        r = pltpu.roll(k2, sh, 1) if sh else k2
        r = jnp.where(lane2 >= sh, r, 0.0)
        tt_ref[:, t * S5_CH:(t + 1) * S5_CH, :] = r.reshape(ng, S5_CH, w).astype(BF16)

    lr2, li2 = lr2_ref[...], li2_ref[...]
    t1 = (lax.broadcasted_iota(jnp.int32, (1, q, 1), 1) + 1).astype(F32)
    zm = jnp.exp(t1 * (lr2 * step))
    za = t1 * (li2 * step)
    zr = zm * jnp.cos(za)
    zi = zm * jnp.sin(za)
    ca, cb = ca_ref[...], cb_ref[...]
    for t in range(q):
        prt = zr[:, t:t + 1, :]
        pit = zi[:, t:t + 1, :]
        zt = jnp.where(first, ca * prt - cb * pit, -(ca * pit) - cb * prt)
        zt_ref[:, t * S5_CH:(t + 1) * S5_CH, :] = zt.astype(BF16)
    qf = float(q)
    mq = jnp.exp(qf * (lr2 * step))
    aq = qf * (li2 * step)
    ar = mq * jnp.cos(aq)
    ai = mq * jnp.sin(aq)
    a1_ref[...] = ar
    a2_ref[...] = jnp.where(first, -ai, ai)


def _s5_prep(q, lam_re, lam_im, log_step, b_re, b_im, c_re, c_im):
    ng, p = lam_re.shape
    w = q * S5_CH
    ls = log_step.reshape(ng, 1, 1)
    lrp = lam_re.reshape(ng, p, 1)
    lip = lam_im.reshape(ng, p, 1)
    lr2 = jnp.concatenate([lam_re, lam_re], axis=-1).reshape(ng, 1, 2 * p)
    li2 = jnp.concatenate([lam_im, lam_im], axis=-1).reshape(ng, 1, 2 * p)
    rep = (jnp.arange(S5_CH)[:, None] == jnp.arange(w)[None, :] % S5_CH).astype(F32)
    ccat = jnp.concatenate([c_re, c_im], axis=-1)
    ca = jnp.concatenate([c_re, c_re], axis=-1)
    cb = jnp.concatenate([c_im, c_im], axis=-1)
    gb = ng if q == 1 else 16
    blk = lambda a, b: pl.BlockSpec((gb, a, b), lambda i: (i, 0, 0))
    return pl.pallas_call(
        functools.partial(_s5_prep_body, q),
        grid=(ng // gb,),
        in_specs=[blk(1, 1), blk(p, 1), blk(p, 1), blk(1, 2 * p), blk(1, 2 * p), blk(p, S5_CH), blk(p, S5_CH),
                  blk(S5_CH, 2 * p), blk(S5_CH, 2 * p), blk(S5_CH, 2 * p), _const_spec(rep.shape)],
        out_specs=[blk(w, w), blk(2 * p, w), blk(w, 2 * p), blk(1, 2 * p), blk(1, 2 * p)],
        out_shape=[jax.ShapeDtypeStruct((ng, w, w), BF16), jax.ShapeDtypeStruct((ng, 2 * p, w), BF16),
                   jax.ShapeDtypeStruct((ng, w, 2 * p), BF16), jax.ShapeDtypeStruct((ng, 1, 2 * p), F32),
                   jax.ShapeDtypeStruct((ng, 1, 2 * p), F32)],
        compiler_params=_params("arbitrary"),
        name=f"s5_prep_q{q}",
    )(ls, lrp, lip, lr2, li2, b_re, b_im, ccat, ca, cb, rep)


def _s5_mix_body(q, nb, nkb, npi, npt, perm, x_ref, g_ref, wu_ref, tt_ref, wt_ref, zt_ref, a1_ref, a2_ref, s0_ref,
                 dcol_ref, wglut_ref, bglu_ref, wo_ref, p_ref, sfin_ref, ut, yt, carry, wut):
    ng = tt_ref.shape[0]
    nk = nkb // nb
    qi = q // npi
    tile = pl.program_id(0)
    ph = pl.program_id(1)

    @pl.when(jnp.logical_and(tile == 0, ph == 0))
    def _():
        carry[...] = s0_ref[...]
        cw_ = 256
        for c in range(0, wu_ref.shape[1], cw_):
            wut[c:c + cw_, :] = wu_ref[:, c:c + cw_].T

    def rows_of(r):
        if not perm:
            return x_ref[r]
        nseq, nblk, ndt, nr, _ = x_ref.shape
        flat = x_ref.reshape(nseq * nblk * ndt * nr, LANES)
        return jnp.concatenate(
            [jnp.concatenate([flat[pl.ds((k * ndt + dt) * nr + r, nseq, stride=nblk * ndt * nr), :]
                              for dt in range(ndt)], axis=1) for k in range(nblk)], axis=0)

    for hh in range(npi):
        @pl.when(ph == hh)
        def _():
            step = 2 if qi % 2 == 0 else 1

            def normed(r):
                xin = jnp.concatenate([rows_of(r + s) for s in range(step)], axis=0)
                return _rms(xin, g_ref[...]).astype(BF16)

            h = normed(0)
            for r in range(0, qi, step):
                h_next = normed(r + step) if r + step < qi else None
                u = _dot_nt(wut[...], h)
                sl = slice((hh * qi + r) * nkb, (hh * qi + r + step) * nkb)
                ut[:, sl] = u.astype(BF16)
                yt[:, sl] = dcol_ref[...] * u
                h = h_next

    @pl.when(ph == npi - 1)
    def _():
        gu = S5_GROUP_BATCH
        bdot = lambda a, b: lax.dot_general(a, b, (((2,), (1,)), ((0,), (0,))), preferred_element_type=F32)
        bdot_nt = lambda a, b: lax.dot_general(a, b, (((2,), (2,)), ((0,), (0,))), preferred_element_type=F32)

        def groups(i, _):
            g0 = pl.multiple_of(i * gu, gu)
            r0 = pl.multiple_of(i * (gu * S5_CH), gu * S5_CH)
            gsl = pl.ds(g0, gu)
            rows = ut[pl.ds(r0, gu * S5_CH), :].reshape(gu, S5_CH, q * nkb)
            ugt = jnp.concatenate([rows[:, :, (q - 1 - j) * nkb:(q - j) * nkb] for j in range(q)], axis=1)
            y = bdot(tt_ref[gsl], ugt)
            vt = bdot(wt_ref[gsl], ugt)
            v = jnp.swapaxes(vt, 1, 2)
            v_sw = jnp.swapaxes(jnp.concatenate([vt[:, S5_STATE:], vt[:, :S5_STATE]], axis=1), 1, 2)
            a1 = a1_ref[gsl]
            a2 = a2_ref[gsl]
            s = carry[gsl]
            s_sw = pltpu.roll(s.reshape(gu * nb, 2 * S5_STATE), S5_STATE, 1).reshape(s.shape)
            prev = []
            for k in range(nk):
                prev.append(s)
                s, s_sw = (a1 * s + a2 * s_sw + v[:, k * nb:(k + 1) * nb, :],
                           a1 * s_sw - a2 * s + v_sw[:, k * nb:(k + 1) * nb, :])
            carry[gsl] = s
            sprev = jnp.concatenate(prev, axis=1).astype(BF16)
            y = y + bdot_nt(zt_ref[gsl], sprev)
            for t in range(q):
                yt[pl.ds(r0, gu * S5_CH), t * nkb:(t + 1) * nkb] += (
                    y[:, t * S5_CH:(t + 1) * S5_CH, :].reshape(gu * S5_CH, nkb))
            return 0

        lax.fori_loop(0, ng // gu, groups, 0)
        sfin_ref[...] = carry[...]

    ch = (q // npt) * nkb
    for j in range(npt):
        @pl.when(ph == npi + j)
        def _():
            gt = _gelu_tanh(yt[:, j * ch:(j + 1) * ch])
            gate = _dot(wglut_ref[...], gt.astype(BF16)) + bglu_ref[...]
            y5 = (gt * _sigmoid(gate)).astype(BF16)
            p_ref[...] = _dot_tn(y5, wo_ref[...]).reshape(p_ref.shape)


def _s5_mix(q, nb, x, ln_g, wu, tt, wt, zt, a1, a2, s0, dcol, wglut, bglu, wo):
    nseq, l, d = x.shape
    nblk = l // q
    nlt = nblk * nseq
    nkb = min(LANES, nlt)
    ntile = nlt // nkb
    tok = q * nkb
    npt = max(1, tok // TOKEN_TILE)
    dm = wo.shape[1]
    perm = q > 1
    if perm:
        half = 8
        npi = q // half
        assert nb == nseq and d % LANES == 0 and q % half == 0
        xv = x.reshape(nseq, nblk, npi, half, d // LANES, LANES).transpose(0, 1, 2, 4, 3, 5)
        x_spec = pl.BlockSpec((nseq, nkb // nseq, None, d // LANES, half, LANES),
                              lambda i, j: (0, i, jnp.minimum(j, npi - 1), 0, 0, 0))
    else:
        npi = 1
        xv = x.reshape(1, nseq, d)
        x_spec = pl.BlockSpec((1, nkb, d), lambda i, j: (0, i, 0))
    body = functools.partial(_s5_mix_body, q, nb, nkb, npi, npt, perm)
    p_spec = pl.BlockSpec((q // npt, nkb, dm), lambda i, j: (jnp.maximum(j - npi, 0), i, 0))
    return pl.pallas_call(
        body,
        grid=(ntile, npi + npt),
        in_specs=[x_spec, _const_spec(ln_g.shape), _const_spec(wu.shape), _const_spec(tt.shape),
                  _const_spec(wt.shape), _const_spec(zt.shape), _const_spec(a1.shape), _const_spec(a2.shape),
                  _const_spec(s0.shape), _const_spec(dcol.shape), _const_spec(wglut.shape),
                  _const_spec(bglu.shape), _const_spec(wo.shape)],
        out_specs=[p_spec, pl.BlockSpec(s0.shape, lambda i, j: (0, 0, 0))],
        out_shape=[jax.ShapeDtypeStruct((q, nlt, dm), F32), jax.ShapeDtypeStruct(s0.shape, F32)],
        scratch_shapes=[pltpu.VMEM((d, tok), BF16), pltpu.VMEM((d, tok), F32), pltpu.VMEM(s0.shape, F32),
                        pltpu.VMEM((wu.shape[1], wu.shape[0]), BF16)],
        compiler_params=_params("arbitrary", "arbitrary"),
        name=f"s5_mix_q{q}",
    )(xv, ln_g, wu, tt, wt, zt, a1, a2, s0, dcol, wglut, bglu, wo)


def _kv_body(m_ref, wk_ref, wv_ref, k_ref, v_ref, kb_ref, vb_ref):
    tm, rows, _ = k_ref.shape
    nt = rows // X_HEADS
    m = m_ref[...].astype(BF16)
    for w_ref, o_ref, ob_ref in ((wk_ref, k_ref, kb_ref), (wv_ref, v_ref, vb_ref)):
        r = _dot(m, w_ref[...])
        ob_ref[...] = r.astype(BF16)
        flat = o_ref.reshape(tm * rows, LANES)
        for h in range(X_HEADS):
            for dt in range(nt):
                c = (h * nt + dt) * LANES
                flat[pl.ds(dt * X_HEADS + h, tm, stride=rows), :] = r[:, c:c + LANES]


def _kv_proj(mem2d, wk, wv):
    t, d = mem2d.shape
    tm = min(TOKEN_TILE, t)
    rows = d // LANES
    row = pl.BlockSpec((tm, d), lambda i: (i, 0))
    tile = pl.BlockSpec((tm, rows, LANES), lambda i: (i, 0, 0))
    return pl.pallas_call(
        _kv_body, grid=(t // tm,),
        in_specs=[row, _const_spec(wk.shape), _const_spec(wv.shape)],
        out_specs=[tile, tile, row, row],
        out_shape=[jax.ShapeDtypeStruct((t, rows, LANES), F32)] * 2 + [jax.ShapeDtypeStruct((t, d), BF16)] * 2,
        compiler_params=_params("arbitrary"),
        name="kv_proj",
    )(mem2d, wk, wv)


def _tile_unview_kv(kv8, b, m):
    nt = kv8.shape[1] // X_HEADS
    return kv8.reshape(b, m, nt, X_HEADS, LANES).transpose(0, 1, 3, 2, 4).reshape(b, m, X_HEADS, nt * LANES)


def _attn_prompt_body(x_ref, ys_ref, p_ref, wo1_ref, g2_ref, wq_ref, k_ref, v_ref, wxo_ref, o_ref):
    d = x_ref.shape[-1]
    hd = d // X_HEADS
    x1 = x_ref[...] + _dot(ys_ref[...], wo1_ref[...]) + p_ref[...]
    hq = _rms(x1, g2_ref[...]).astype(BF16)
    qv = _dot(hq, wq_ref[...]).astype(BF16)
    kb = k_ref[...]
    vb = v_ref[...]
    outs = []
    for h in range(X_HEADS):
        sl = slice(h * hd, (h + 1) * hd)
        s = _dot_nt(qv[:, sl], kb[:, sl]) * (hd ** -0.5)
        e = jnp.exp(s - jnp.max(s, axis=-1, keepdims=True))
        p = (e / jnp.sum(e, axis=-1, keepdims=True)).astype(BF16)
        outs.append(_dot(p, vb[:, sl]))
    o = jnp.concatenate(outs, axis=1).astype(BF16)
    o_ref[...] = x1 + _dot(o, wxo_ref[...])


def _attn_prompt(x, ys, pm, wo1, g2, wq, mk, mv, wxo, make_rider):
    b, l, d = x.shape
    nm = mk.shape[1]
    tm = min(TOKEN_TILE, l)
    nl = l // tm
    row = pl.BlockSpec((None, tm, d), lambda i, j: (i, j, 0))
    kv = pl.BlockSpec((None, nm, d), lambda i, j: (i, 0, 0))
    ins = [x, ys, pm, wo1, g2, wq, mk, mv, wxo]
    specs = [row, row, row, _const_spec(wo1.shape), _const_spec(g2.shape), _const_spec(wq.shape), kv, kv,
             _const_spec(wxo.shape)]
    rider = make_rider(lambda i, j: i * nl + j, b * nl)
    body = functools.partial(_two_bodies, _attn_prompt_body, len(ins), 1, rider[0], len(rider[1]))
    return pl.pallas_call(
        body, grid=(b, nl),
        in_specs=specs + list(rider[2]),
        out_specs=[row] + list(rider[3]),
        out_shape=[jax.ShapeDtypeStruct((b, l, d), F32)] + list(rider[4]),
        compiler_params=_params("arbitrary", "arbitrary"),
        name="attn_prompt",
    )(*ins, *rider[1])


def _merge_q_body(x_ref, ys_ref, p_ref, wo1_ref, g2_ref, wq_ref, x1_ref, q_ref):
    x1 = x_ref[...] + _dot(ys_ref[...], wo1_ref[...]) + p_ref[...]
    x1_ref[...] = x1
    q_ref[...] = _dot(_rms(x1, g2_ref[...]).astype(BF16), wq_ref[...])


def _merge_q(x2d, ys, pm, wo1, g2, wq):
    t, d = x2d.shape
    return pl.pallas_call(
        _merge_q_body,
        out_shape=[jax.ShapeDtypeStruct((t, d), F32)] * 2,
        compiler_params=pltpu.CompilerParams(vmem_limit_bytes=VMEM_LIMIT),
        name="merge_q",
    )(x2d, ys, pm, wo1, g2, wq)


def _tile_view(kv):
    b, m, nh, hd = kv.shape
    nt = hd // LANES
    return kv.reshape(b, m, nh, nt, LANES).transpose(0, 1, 3, 2, 4).reshape(b, m, nt * nh, LANES)


def _tile_unview(o):
    b, rows, _ = o.shape
    nt = rows // X_HEADS
    return o.reshape(b, nt, X_HEADS, LANES).transpose(0, 2, 1, 3).reshape(b, rows * LANES)


def _attn_step_body(q_ref, k_ref, v_ref, o_ref):
    bt, nm, rows, _ = k_ref.shape
    hd = rows * LANES // X_HEADS
    for b in range(bt):
        r = jnp.sum(k_ref[b] * q_ref[b], axis=-1, keepdims=True)
        s = (r + pltpu.roll(r, X_HEADS, 1)) * (hd ** -0.5)
        e = jnp.exp(s - jnp.max(s, axis=0, keepdims=True))
        p = e / jnp.sum(e, axis=0, keepdims=True)
        o_ref[b] = jnp.sum(p * v_ref[b], axis=0)


def _attn_step_rider(q8, k8, v8, nsteps):
    nbt, rows, _ = q8.shape
    nm = k8.shape[1]
    assert rows == 2 * X_HEADS, "score assembly assumes two 128-lane tiles per head"
    assert nbt % nsteps == 0
    bt = nbt // nsteps
    qs = pl.BlockSpec((bt, rows, LANES), lambda i: (i, 0, 0))
    kv = pl.BlockSpec((bt, nm, rows, LANES), lambda i: (i, 0, 0, 0))
    return (_attn_step_body, (q8, k8, v8), (qs, kv, kv), (qs,),
            (jax.ShapeDtypeStruct((nbt, rows, LANES), F32),))


def _ffn_body(has_o, final, *refs):
    if has_o:
        x_ref, o_in_ref, wxo_ref, g3_ref, wg_ref, wu_ref, wd_ref, gf_ref, y_ref = refs
        x2 = x_ref[...] + _dot(o_in_ref[...].astype(BF16), wxo_ref[...])
    else:
        x_ref, g3_ref, wg_ref, wu_ref, wd_ref, gf_ref, y_ref = refs
        x2 = x_ref[...]
    hf = _rms(x2, g3_ref[...]).astype(BF16)
    ff = wg_ref.shape[1]
    x3 = x2
    for lo in range(0, ff, FF_CHUNK):
        hi = min(lo + FF_CHUNK, ff)
        act = (_silu(_dot(hf, wg_ref[:, lo:hi])) * _dot(hf, wu_ref[:, lo:hi])).astype(BF16)
        x3 = x3 + _dot(act, wd_ref[lo:hi, :])
    y_ref[...] = _rms(x3, gf_ref[...]) if final else x3


def _ffn(x2d, g3, wg, wu, wd, gf, final, o_in=None, wxo=None, rider=None):
    t, d = x2d.shape
    tm = min(TOKEN_TILE, t)
    row = pl.BlockSpec((tm, d), lambda i: (i, 0))
    has_o = o_in is not None
    ins = [x2d] + ([o_in, wxo] if has_o else []) + [g3, wg, wu, wd, gf]
    specs = [row] + ([row, _const_spec(wxo.shape)] if has_o else []) + [
        _const_spec(g3.shape), _const_spec(wg.shape), _const_spec(wu.shape), _const_spec(wd.shape),
        _const_spec(gf.shape)]
    body = functools.partial(_ffn_body, has_o, final)
    out_specs, out_shape = [row], [jax.ShapeDtypeStruct((t, d), F32)]
    if rider is not None:
        body = functools.partial(_two_bodies, body, len(ins), 1, rider[0], len(rider[1]))
        ins, specs = ins + list(rider[1]), specs + list(rider[2])
        out_specs, out_shape = out_specs + list(rider[3]), out_shape + list(rider[4])
    res = pl.pallas_call(
        body, grid=(t // tm,),
        in_specs=specs, out_specs=out_specs, out_shape=out_shape,
        compiler_params=_params("arbitrary"),
        name="ffn_o" if has_o else "ffn",
    )(*ins)
    return res[0] if rider is None else res


def _two_bodies(body_a, n_in_a, n_out_a, body_b, n_in_b, *refs):
    outs = refs[n_in_a + n_in_b:]
    body_a(*refs[:n_in_a], *outs[:n_out_a])
    body_b(*refs[n_in_a:n_in_a + n_in_b], *outs[n_out_a:])


def _layer_weights(i, ln1_g, w_in, conv_w, conv_b, dt_bias, a_log, d_ssd, gn_g, lam_re, lam_im, log_step,
                   b_re, b_im, c_re, c_im, d_s5, w_glu, b_glu, w_out, ln2_g, w_xq, w_xk, w_xv, w_xo, ln3_g,
                   w_gate, w_up, w_down):
    d = w_in.shape[1]
    n_heads = dt_bias.shape[1]
    dssd = n_heads * SSD_HEAD_DIM
    conv_dim = conv_w.shape[2]
    o1, o2, o3 = dssd, dssd + conv_dim, dssd + conv_dim + n_heads
    win = w_in[i]
    w = {}
    w["ln1"] = ln1_g[i].reshape(1, d)
    w["win"] = win.astype(BF16)
    w["dims"] = (dssd, conv_dim)
    w["wu_s5"] = w["win"][:, o3:]
    w["cw"] = conv_w[i]
    w["cb"] = conv_b[i].reshape(1, conv_dim)
    w["dtb"] = jnp.pad(dt_bias[i], (0, LANES - n_heads)).reshape(1, LANES)
    w["alog"] = jnp.pad(a_log[i], (0, LANES - n_heads)).reshape(1, LANES)
    w["dexp"] = jnp.repeat(d_ssd[i], SSD_HEAD_DIM).reshape(1, dssd)
    w["gn"] = gn_g[i].reshape(1, dssd)
    w["s5"] = (lam_re[i], lam_im[i], log_step[i], b_re[i], b_im[i], c_re[i], c_im[i])
    ds5 = d_s5.shape[1]
    w["dcol"] = d_s5[i].reshape(ds5, 1)
    w["wglut"] = w_glu[i].T.astype(BF16)
    w["bglu"] = b_glu[i].reshape(ds5, 1)
    w["wo1"] = w_out[i, :dssd].astype(BF16)
    w["wo2"] = w_out[i, dssd:].astype(BF16)
    w["ln2"] = ln2_g[i].reshape(1, d)
    w["wq"] = w_xq[i].astype(BF16)
    w["wk"] = w_xk[i].astype(BF16)
    w["wv"] = w_xv[i].astype(BF16)
    w["wxo"] = w_xo[i].astype(BF16)
    w["ln3"] = ln3_g[i].reshape(1, d)
    w["wg"] = w_gate[i].astype(BF16)
    w["wu"] = w_up[i].astype(BF16)
    w["wdn"] = w_down[i].astype(BF16)
    return w


def _state_to_rows(s_re, s_im):
    return jnp.concatenate([s_re, s_im], axis=-1).transpose(1, 0, 2)


def _rows_to_state(s):
    p = s.shape[-1] // 2
    st = s.transpose(1, 0, 2)
    return st[..., :p], st[..., p:]


def kernel(x_prompt, x_sample, mem_prompt, state_conv, state_ssm, state_s5_re, state_s5_im, cache_mem_k, cache_mem_v, ln1_g, w_in, conv_w, conv_b, dt_bias, a_log, d_ssd, gn_g, lam_re, lam_im, log_step, b_re, b_im, c_re, c_im, d_s5, w_glu, b_glu, w_out, ln2_g, w_xq, w_xk, w_xv, w_xo, ln3_g, w_gate, w_up, w_down, final_g):
    bp, seq, d = x_prompt.shape
    bs = x_sample.shape[0]
    depth = w_in.shape[0]
    nm = mem_prompt.shape[1]
    ng, ns = lam_re.shape[1], lam_re.shape[2]
    gf = final_g.reshape(1, d)
    hp = x_prompt
    hs = x_sample.reshape(bs, d)
    outs = {k: [] for k in ("conv_p", "ssm_p", "re_p", "im_p", "mk_p", "mv_p", "conv_s", "ssm_s", "re_s", "im_s")}
    yp = ys_out = None
    for i in range(depth):
        w = _layer_weights(i, ln1_g, w_in, conv_w, conv_b, dt_bias, a_log, d_ssd, gn_g, lam_re, lam_im, log_step,
                           b_re, b_im, c_re, c_im, d_s5, w_glu, b_glu, w_out, ln2_g, w_xq, w_xk, w_xv, w_xo, ln3_g,
                           w_gate, w_up, w_down)
        last = i == depth - 1
        mk8, mv8, mk, mv = _kv_proj(mem_prompt.reshape(bp * nm, d), w["wk"], w["wv"])
        mk = mk.reshape(bp, nm, d)
        mv = mv.reshape(bp, nm, d)
        y_ssd, conv_p, ssm_p = _ssd_prompt(hp, w["ln1"], w["win"], w["cw"], w["cb"], w["dtb"], w["alog"], w["dexp"],
                                           w["gn"])
        q = S5_Q
        tt, wt, zt, a1, a2 = _s5_prep(q, *w["s5"])
        s0 = jnp.zeros((ng, bp, 2 * ns), F32)
        pperm, sfin = _s5_mix(q, bp, hp, w["ln1"], w["wu_s5"],tt, wt, zt, a1, a2, s0, w["dcol"], w["wglut"],
                              w["bglu"], w["wo2"])
        pm = pperm.reshape(q, seq // q, bp, d).transpose(2, 1, 0, 3).reshape(bp, seq, d)
        re_p, im_p = _rows_to_state(sfin)
        zs, xbcs, dts = _inproj(hs, w["ln1"], w["win"], *w["dims"])
        ssd_rider = functools.partial(_ssd_step_rider, zs, xbcs, dts, state_conv[i].transpose(1, 0, 2), state_ssm[i],
                                      w["cw"], w["cb"], w["dtb"], w["alog"], w["dexp"], w["gn"])
        x2, ys_ssd, conv_s_t, ssm_s = _attn_prompt(hp, y_ssd, pm, w["wo1"], w["ln2"], w["wq"], mk, mv, w["wxo"],
                                                   ssd_rider)
        ys_ssd = ys_ssd.reshape(bs, -1)
        conv_s_t = conv_s_t.reshape(CONV_W - 1, bs, -1)
        tt1, wt1, zt1, a11, a21 = _s5_prep(1, *w["s5"])
        s0s = _state_to_rows(state_s5_re[i], state_s5_im[i])
        ps, sfin_s = _s5_mix(1, bs, hs.reshape(bs, 1, d), w["ln1"], w["wu_s5"],tt1, wt1, zt1, a11, a21, s0s,
                             w["dcol"], w["wglut"], w["bglu"], w["wo2"])
        re_s, im_s = _rows_to_state(sfin_s)
        x1s, qs = _merge_q(hs, ys_ssd, ps.reshape(bs, d), w["wo1"], w["ln2"], w["wq"])
        attn_rider = _attn_step_rider(_tile_view(qs.reshape(bs, 1, X_HEADS, d // X_HEADS))[:, 0],
                                      _tile_view(cache_mem_k[i]), _tile_view(cache_mem_v[i]),
                                      (bp * seq) // min(TOKEN_TILE, bp * seq))
        hp, o_s = _ffn(x2.reshape(bp * seq, d), w["ln3"], w["wg"], w["wu"], w["wdn"], gf, last, rider=attn_rider)
        hp = hp.reshape(bp, seq, d)
        hs = _ffn(x1s, w["ln3"], w["wg"], w["wu"], w["wdn"], gf, last, o_in=_tile_unview(o_s), wxo=w["wxo"])
        for k, v in (("conv_p", conv_p), ("ssm_p", ssm_p), ("re_p", re_p), ("im_p", im_p),
                     ("mk_p", _tile_unview_kv(mk8, bp, nm)), ("mv_p", _tile_unview_kv(mv8, bp, nm)),
                     ("conv_s", conv_s_t.transpose(1, 0, 2)), ("ssm_s", ssm_s), ("re_s", re_s), ("im_s", im_s)):
            outs[k].append(v)
    st = lambda k: jnp.stack(outs[k])
    return (hp, hs.reshape(bs, 1, d), st("conv_p"), st("ssm_p"), st("re_p"), st("im_p"), st("mk_p"), st("mv_p"),
            st("conv_s"), st("ssm_s"), st("re_s"), st("im_s"))
```

```python
import functools
import math

import jax
import jax.numpy as jnp
from jax import lax
from jax.experimental import pallas as pl
from jax.experimental.pallas import tpu as pltpu

F32 = jnp.float32
BF16 = jnp.bfloat16
EPS = 1e-6

LANES = 128
VMEM_LIMIT = 56 * 1024 * 1024

SSD_HEAD_DIM = 64
SSD_STATE = 128
SSD_GROUPS = 2
SSD_CHUNK = 128
CONV_W = 4
S5_CH = 16
S5_STATE = 64
S5_Q = 16
S5_GROUP_BATCH = 16
X_HEADS = 4
TOKEN_TILE = 512
PROJ_PIECE = 512
FF_CHUNK = 1536


def _const_spec(shape):
    nd = len(shape)
    return pl.BlockSpec(shape, lambda *_: (0,) * nd, pipeline_mode=pl.Buffered(1))


def _params(*sem):
    return pltpu.CompilerParams(dimension_semantics=sem, vmem_limit_bytes=VMEM_LIMIT)


def _rms(x, g):
    return x * lax.rsqrt(jnp.mean(x * x, axis=-1, keepdims=True) + EPS) * g


def _sigmoid(x):
    return 1.0 / (1.0 + jnp.exp(-x))


def _silu(x):
    return x * _sigmoid(x)


def _gelu_tanh(x):
    return 0.5 * x * (1.0 + jnp.tanh(math.sqrt(2.0 / math.pi) * (x + 0.044715 * (x * x * x))))


def _softplus(x):
    return jnp.maximum(x, 0.0) + jnp.log1p(jnp.exp(-jnp.abs(x)))


def _dot(a, b):
    return jnp.dot(a, b, preferred_element_type=F32)


def _dot_nt(a, b):
    return lax.dot_general(a, b, (((1,), (1,)), ((), ())), preferred_element_type=F32)


def _dot_tn(a, b):
    return lax.dot_general(a, b, (((0,), (0,)), ((), ())), preferred_element_type=F32)


def _proj_windows(d_ssd, conv_dim):
    assert d_ssd % LANES == 0 and conv_dim % LANES == 0
    return (0, d_ssd), (d_ssd, d_ssd + conv_dim), (d_ssd + conv_dim, d_ssd + conv_dim + LANES)


def _inproj_body(wins, x_ref, g_ref, w_ref, z_ref, xbc_ref, dt_ref):
    h = _rms(x_ref[...], g_ref[...]).astype(BF16)
    for (lo, hi), o_ref in zip(wins, (z_ref, xbc_ref, dt_ref)):
        o_ref[...] = _dot(h, w_ref[:, lo:hi])


def _inproj(x2d, ln_g, w_in, d_ssd, conv_dim):
    t, d = x2d.shape
    tm = min(TOKEN_TILE, t)
    wins = _proj_windows(d_ssd, conv_dim)
    row = lambda n: pl.BlockSpec((tm, n), lambda i: (i, 0))
    return pl.pallas_call(
        functools.partial(_inproj_body, wins),
        grid=(t // tm,),
        in_specs=[row(d), _const_spec((1, d)), _const_spec(w_in.shape)],
        out_specs=[row(hi - lo) for lo, hi in wins],
        out_shape=[jax.ShapeDtypeStruct((t, hi - lo), F32) for lo, hi in wins],
        compiler_params=_params("arbitrary"),
        name="inproj",
    )(x2d, ln_g, w_in)


def _expand_heads(v, n_heads):
    rows = v.shape[0]
    lane = lax.broadcasted_iota(jnp.int32, (rows, LANES), 1)
    pieces = [jnp.where(lane < SSD_HEAD_DIM, v[:, 2 * j:2 * j + 1], v[:, 2 * j + 1:2 * j + 2])
              for j in range(n_heads // 2)]
    return jnp.concatenate(pieces, axis=1)


def _ssd_chunk_math(z, x, dt_raw, last_chunk, cw_ref, cb_ref, dtb_ref, alog_ref, dexp_ref, gn_ref, e2_ref,
                    y_ref, conv_ref, ssm_ref, xpad, state, side_jobs=()):
    jobs = list(side_jobs)

    def side(n=1):
        for _ in range(n):
            if jobs:
                jobs.pop(0)()

    q = SSD_CHUNK
    d_ssd = z.shape[-1]
    n_heads = d_ssd // SSD_HEAD_DIM
    hpg = n_heads // SSD_GROUPS
    gw = hpg * SSD_HEAD_DIM
    nct = xpad.shape[0]
    xt = d_ssd // LANES

    for t in range(nct):
        xpad[t, 8:8 + q, :] = x[:, t * LANES:(t + 1) * LANES]
    cw = cw_ref[...]
    conv = (cw[3] * xpad[:, 8:8 + q, :] + cw[2] * xpad[:, 7:7 + q, :] + cw[1] * xpad[:, 6:6 + q, :]
            + cw[0] * xpad[:, 5:5 + q, :] + cb_ref[...])
    xpad[:, 0:8, :] = xpad[:, q:q + 8, :]
    xact = _silu(conv)
    xs = jnp.concatenate([xact[t] for t in range(xt)], axis=1)
    bm = xact[xt:xt + SSD_GROUPS].astype(BF16)
    cm = xact[xt + SSD_GROUPS:].astype(BF16)
    side()

    dt = _softplus(dt_raw + dtb_ref[...])
    a = -jnp.exp(alog_ref[...])
    row = lax.broadcasted_iota(jnp.int32, (q, q), 0)
    col = lax.broadcasted_iota(jnp.int32, (q, q), 1)
    causal = row >= col
    tri = jnp.where(causal, 1.0, 0.0).astype(F32)
    acum = jnp.dot(tri, dt * a, precision=lax.Precision.HIGHEST, preferred_element_type=F32)
    acum_t = acum.T
    last = acum[q - 1:q, :]

    def expand(v):
        hi = v.astype(BF16)
        lo = (v - hi.astype(F32)).astype(BF16)
        return _dot(jnp.concatenate([hi, lo], axis=1), e2_ref[...])

    dt_e = expand(dt)
    ea_e = expand(jnp.exp(acum))
    dend_e = expand(jnp.exp(last - acum))

    dtx = xs * dt_e
    dtx_b = dtx.astype(BF16)
    xdec_b = (dtx * dend_e).astype(BF16)
    side()

    lane = lax.broadcasted_iota(jnp.int32, (q, LANES), 1)
    zero_b = jnp.zeros((q, LANES), BF16)
    y_tiles = []
    for g in range(SSD_GROUPS):
        bg = bm[g]
        cg = cm[g]
        cb = _dot_nt(cg, bg)
        gs = slice(g * gw, (g + 1) * gw)
        st_g = state[:, gs]
        y_off = _dot(cg, st_g.astype(BF16)) * ea_e[:, gs]
        for pr in range(hpg // 2):
            tile = g * (hpg // 2) + pr
            ms = []
            for h in (2 * tile, 2 * tile + 1):
                seg = acum[:, h:h + 1] - acum_t[h:h + 1, :]
                lmat = jnp.exp(jnp.where(causal, seg, -jnp.inf))
                ms.append((cb * lmat).astype(BF16))
            dtile = dtx_b[:, tile * LANES:(tile + 1) * LANES]
            rhs = jnp.concatenate([jnp.where(lane < SSD_HEAD_DIM, dtile, zero_b),
                                   jnp.where(lane >= SSD_HEAD_DIM, dtile, zero_b)], axis=0)
            y_tiles.append(_dot(jnp.concatenate(ms, axis=1), rhs)
                           + y_off[:, pr * LANES:(pr + 1) * LANES])
            if pr % 2 == 1:
                side()
        new = _dot_tn(bg, xdec_b[:, gs])
        state[:, gs] = st_g * ea_e[q - 1:q, gs] + new
    y = jnp.concatenate(y_tiles, axis=1) + dexp_ref[...] * xs
    y = y * _silu(z)
    ms = jnp.mean(y * y, axis=-1, keepdims=True)
    side(len(jobs))
    y_ref[...] = (y * lax.rsqrt(ms + EPS) * gn_ref[...]).astype(y_ref.dtype)

    @pl.when(last_chunk)
    def _():
        conv_ref[...] = x[q - (CONV_W - 1):q, :]
        ssm_ref[...] = state[...].T.reshape(ssm_ref.shape)


def _ssd_prompt_body(wins, x_ref, g_ref, w_ref, cw_ref, cb_ref, dtb_ref, alog_ref, dexp_ref, gn_ref,
                     e2_ref, y_ref, conv_ref, ssm_ref, za, xa, da, zb, xb, db, xpad, state):
    i = pl.program_id(0)
    j = pl.program_id(1)
    nc = pl.num_programs(1) - 1

    @pl.when(jnp.logical_and(i == 0, j == 0))
    def _():
        for r in (zb, xb, db):
            r[...] = jnp.zeros(r.shape, F32)

    @pl.when(j <= 1)
    def _():
        xpad[:, 0:8, :] = jnp.zeros((xpad.shape[0], 8, LANES), F32)
        state[...] = jnp.zeros(state.shape, F32)

    def step(wr, rd):
        xin = x_ref[...]
        msx = jnp.mean(xin * xin, axis=-1, keepdims=True)
        hbox = []

        def proj_piece(dst, base, lo, hi):
            def run():
                if not hbox:
                    hbox.append((xin * lax.rsqrt(msx + EPS) * g_ref[...]).astype(BF16))
                dst[:, lo:hi] = _dot(hbox[0], w_ref[:, base + lo:base + hi])
            return run

        pieces = []
        for dst, (w_lo, w_hi) in ((wr[0], wins[0]), (wr[2], wins[2]), (wr[1], wins[1])):
            n = w_hi - w_lo
            pieces += [proj_piece(dst, w_lo, lo, min(lo + PROJ_PIECE, n)) for lo in range(0, n, PROJ_PIECE)]
        _ssd_chunk_math(rd[0][...], rd[1][...], rd[2][...], j == nc, cw_ref, cb_ref, dtb_ref, alog_ref, dexp_ref,
                        gn_ref, e2_ref, y_ref, conv_ref, ssm_ref, xpad, state, pieces)

    @pl.when(j % 2 == 0)
    def _():
        step((za, xa, da), (zb, xb, db))

    @pl.when(j % 2 == 1)
    def _():
        step((zb, xb, db), (za, xa, da))


def _ssd_prompt(x, ln_g, w_in, cw, cb, dtb, alog, dexp, gn):
    b, l, d = x.shape
    conv_dim = cw.shape[1]
    d_ssd = conv_dim - 2 * SSD_GROUPS * SSD_STATE
    wins = _proj_windows(d_ssd, conv_dim)
    n_heads = d_ssd // SSD_HEAD_DIM
    q = SSD_CHUNK
    nc = l // q
    assert SSD_STATE == LANES and conv_dim == d_ssd + 2 * SSD_GROUPS * SSD_STATE
    nct = conv_dim // LANES
    cw4 = cw.reshape(CONV_W, nct, 1, LANES)
    cb3 = cb.reshape(nct, 1, LANES)
    e2 = (jnp.arange(2 * LANES)[:, None] % LANES == jnp.arange(d_ssd)[None, :] // SSD_HEAD_DIM).astype(BF16)
    slot = lambda n: pltpu.VMEM((q, n), F32)
    return pl.pallas_call(
        functools.partial(_ssd_prompt_body, wins),
        grid=(b, nc + 1),
        in_specs=[pl.BlockSpec((None, q, d), lambda i, j: (i, jnp.minimum(j, nc - 1), 0)),
                  _const_spec(ln_g.shape), _const_spec(w_in.shape),
                  _const_spec(cw4.shape), _const_spec(cb3.shape), _const_spec(dtb.shape), _const_spec(alog.shape),
                  _const_spec(dexp.shape), _const_spec(gn.shape), _const_spec(e2.shape)],
        out_specs=[pl.BlockSpec((None, q, d_ssd), lambda i, j: (i, jnp.maximum(j - 1, 0), 0)),
                   pl.BlockSpec((None, CONV_W - 1, conv_dim), lambda i, j: (i, 0, 0)),
                   pl.BlockSpec((None, n_heads, SSD_HEAD_DIM, SSD_STATE), lambda i, j: (i, 0, 0, 0))],
        out_shape=[jax.ShapeDtypeStruct((b, l, d_ssd), BF16),
                   jax.ShapeDtypeStruct((b, CONV_W - 1, conv_dim), F32),
                   jax.ShapeDtypeStruct((b, n_heads, SSD_HEAD_DIM, SSD_STATE), F32)],
        scratch_shapes=[slot(d_ssd), slot(conv_dim), slot(LANES), slot(d_ssd), slot(conv_dim), slot(LANES),
                        pltpu.VMEM((nct, q + 8, LANES), F32), pltpu.VMEM((SSD_STATE, d_ssd), F32)],
        compiler_params=_params("arbitrary", "arbitrary"),
        name="ssd_prompt",
    )(x, ln_g, w_in, cw4, cb3, dtb, alog, dexp, gn, e2)


def _ssd_step_body(z_ref, xbc_ref, dt_ref, cs_ref, st_ref, cw_ref, cb_ref, dtb_ref, alog_ref, dexp_ref, gn_ref,
                   y_ref, conv_ref, ssm_ref):
    bt, d_ssd = z_ref.shape
    n_heads = d_ssd // SSD_HEAD_DIM
    hpg = n_heads // SSD_GROUPS
    x = xbc_ref[...]
    cw = cw_ref[...]
    conv = cw[0:1] * cs_ref[0] + cw[1:2] * cs_ref[1] + cw[2:3] * cs_ref[2] + cw[3:4] * x + cb_ref[...]
    conv_ref[0] = cs_ref[1]
    conv_ref[1] = cs_ref[2]
    conv_ref[2] = x
    xact = _silu(conv)
    xs = xact[:, :d_ssd]
    bm = xact[:, d_ssd:d_ssd + SSD_GROUPS * SSD_STATE]
    cm = xact[:, d_ssd + SSD_GROUPS * SSD_STATE:]
    dt = _softplus(dt_ref[...] + dtb_ref[...])
    da = jnp.exp(dt * (-jnp.exp(alog_ref[...])))
    dtx = xs * _expand_heads(dt, n_heads)
    pad = jnp.zeros((LANES - bt, d_ssd), F32)
    to_cols = lambda v: jnp.concatenate([v, pad], axis=0).T
    dtx_t = to_cols(dtx)
    da_t = to_cols(_expand_heads(da, n_heads))
    lane = lax.broadcasted_iota(jnp.int32, (d_ssd, LANES), 1)
    gw = hpg * SSD_HEAD_DIM
    rows_of = lambda v, b: jnp.concatenate(
        [jnp.broadcast_to(v[b:b + 1, g * SSD_STATE:(g + 1) * SSD_STATE], (gw, SSD_STATE)) for g in range(SSD_GROUPS)],
        axis=0)
    y_t = jnp.zeros((d_ssd, LANES), F32)
    for b in range(bt):
        s0 = st_ref[b].reshape(d_ssd, SSD_STATE)
        s_new = s0 * da_t[:, b:b + 1] + dtx_t[:, b:b + 1] * rows_of(bm, b)
        ssm_ref[b] = s_new.reshape(n_heads, SSD_HEAD_DIM, SSD_STATE)
        ycol = jnp.sum(s_new * rows_of(cm, b), axis=1, keepdims=True)
        y_t = jnp.where(lane == b, ycol, y_t)
    y = y_t.T[:bt, :] + dexp_ref[...] * xs
    y = y * _silu(z_ref[...])
    y_ref[...] = _rms(y, gn_ref[...]).astype(y_ref.dtype)


def _ssd_step_rider(z, xbc, dt, conv_state_t, ssm_state, cw, cb, dtb, alog, dexp, gn, step_of, nsteps):
    nb, d_ssd = z.shape
    conv_dim = xbc.shape[-1]
    n_heads = d_ssd // SSD_HEAD_DIM
    assert nb % nsteps == 0
    bt = nb // nsteps
    row = lambda n: pl.BlockSpec((None, bt, n), lambda *g: (step_of(*g), 0, 0))
    cs_spec = pl.BlockSpec((CONV_W - 1, None, bt, conv_dim), lambda *g: (0, step_of(*g), 0, 0))
    st_spec = pl.BlockSpec((bt, n_heads, SSD_HEAD_DIM, SSD_STATE), lambda *g: (step_of(*g), 0, 0, 0))
    cst = lambda a: pl.BlockSpec(a.shape, lambda *g: (0,) * a.ndim, pipeline_mode=pl.Buffered(1))
    ins = (z.reshape(nsteps, bt, d_ssd), xbc.reshape(nsteps, bt, conv_dim), dt.reshape(nsteps, bt, LANES),
           conv_state_t.reshape(CONV_W - 1, nsteps, bt, conv_dim), ssm_state, cw, cb, dtb, alog, dexp, gn)
    specs = (row(d_ssd), row(conv_dim), row(LANES), cs_spec, st_spec, cst(cw), cst(cb), cst(dtb), cst(alog),
             cst(dexp), cst(gn))
    out_shapes = (jax.ShapeDtypeStruct((nsteps, bt, d_ssd), BF16),
                  jax.ShapeDtypeStruct((CONV_W - 1, nsteps, bt, conv_dim), F32),
                  jax.ShapeDtypeStruct(ssm_state.shape, F32))
    return (_ssd_step_body, ins, specs, (row(d_ssd), cs_spec, st_spec), out_shapes)


def _split_bf16(x):
    hi = x.astype(BF16)
    lo = (x - hi.astype(F32)).astype(BF16)
    return hi, lo


def _s5_prep_body(q, ls_ref, lrp_ref, lip_ref, lr2_ref, li2_ref, br_ref, bi_ref, ccat_ref, ca_ref, cb_ref, rep_ref,
                  tt_ref, wt_ref, zt_ref, a1_ref, a2_ref, a1s_ref, a2s_ref):
    ng = ls_ref.shape[0]
    w = q * S5_CH
    step = jnp.exp(ls_ref[...])
    lr, li = lrp_ref[...], lip_ref[...]
    mag = jnp.exp(lr * step)
    ang = li * step
    lbr = mag * jnp.cos(ang)
    lbi = mag * jnp.sin(ang)
    den = lr * lr + li * li
    kr = ((lbr - 1.0) * lr + lbi * li) / den
    ki = (lbi * lr - (lbr - 1.0) * li) / den
    br, bi = br_ref[...], bi_ref[...]
    bbr = kr * br - ki * bi
    bbi = kr * bi + ki * br
    if q > 1:
        np_ = br.shape[1]
        rep = lambda v: jnp.dot(v.reshape(ng * np_, S5_CH), rep_ref[...], precision=lax.Precision.HIGHEST,
                                preferred_element_type=F32).reshape(ng, np_, w)
        bbr, bbi = rep(bbr), rep(bbi)
    if q == 1:
        ball = jnp.concatenate([bbr, bbi], axis=1)
    else:
        d = lax.broadcasted_iota(jnp.int32, (1, 1, w), 2) // S5_CH
        fr, fi = lbr, lbi
        pr = pi = None
        for b in range((q - 1).bit_length()):
            bit = ((d >> b) & 1) == 1
            sr, si = jnp.where(bit, fr, 1.0), jnp.where(bit, fi, 0.0)
            pr, pi = (sr, si) if pr is None else (pr * sr - pi * si, pr * si + pi * sr)
            fr, fi = fr * fr - fi * fi, 2.0 * (fr * fi)
        ball = jnp.concatenate([pr * bbr - pi * bbi, pr * bbi + pi * bbr], axis=1)
    wt_ref[...] = ball.astype(BF16)

    lane3 = lax.broadcasted_iota(jnp.int32, (1, 1, 2 * S5_STATE), 2)
    first = lane3 < S5_STATE
    csign = jnp.where(first, ccat_ref[...], -ccat_ref[...])
    ch, cl = _split_bf16(csign)
    bh, bl = _split_bf16(ball)
    bdot = lambda x, y: lax.dot_general(x, y, (((2,), (1,)), ((0,), (0,))), preferred_element_type=F32)
    kall = bdot(ch, bh) + bdot(ch, bl) + bdot(cl, bh)
    k2 = kall.reshape(ng * S5_CH, w)
    lane2 = lax.broadcasted_iota(jnp.int32, (ng * S5_CH, w), 1)
    for t in range(q):
        sh = (q - 1 - t) * S5_CH
        r = pltpu.roll(k2, sh, 1) if sh else k2
        r = jnp.where(lane2 >= sh, r, 0.0)
        tt_ref[:, t * S5_CH:(t + 1) * S5_CH, :] = r.reshape(ng, S5_CH, w).astype(BF16)

    lr2, li2 = lr2_ref[...], li2_ref[...]
    t1 = (lax.broadcasted_iota(jnp.int32, (1, q, 1), 1) + 1).astype(F32)
    zm = jnp.exp(t1 * (lr2 * step))
    za = t1 * (li2 * step)
    zr = zm * jnp.cos(za)
    zi = zm * jnp.sin(za)
    ca, cb = ca_ref[...], cb_ref[...]
    for t in range(q):
        prt = zr[:, t:t + 1, :]
        pit = zi[:, t:t + 1, :]
        zt = jnp.where(first, ca * prt - cb * pit, -(ca * pit) - cb * prt)
        zt_ref[:, t * S5_CH:(t + 1) * S5_CH, :] = zt.astype(BF16)
    qf = float(q)
    mq = jnp.exp(qf * (lr2 * step))
    aq = qf * (li2 * step)
    ar = mq * jnp.cos(aq)
    ai = mq * jnp.sin(aq)
    a1_ref[...] = ar
    a2_ref[...] = jnp.where(first, -ai, ai)
    a1s_ref[...] = zr[:, 0:1, :]
    a2s_ref[...] = jnp.where(first, -zi[:, 0:1, :], zi[:, 0:1, :])


def _s5_prep(q, lam_re, lam_im, log_step, b_re, b_im, c_re, c_im):
    ng, p = lam_re.shape
    w = q * S5_CH
    ls = log_step.reshape(ng, 1, 1)
    lrp = lam_re.reshape(ng, p, 1)
    lip = lam_im.reshape(ng, p, 1)
    lr2 = jnp.concatenate([lam_re, lam_re], axis=-1).reshape(ng, 1, 2 * p)
    li2 = jnp.concatenate([lam_im, lam_im], axis=-1).reshape(ng, 1, 2 * p)
    rep = (jnp.arange(S5_CH)[:, None] == jnp.arange(w)[None, :] % S5_CH).astype(F32)
    ccat = jnp.concatenate([c_re, c_im], axis=-1)
    ca = jnp.concatenate([c_re, c_re], axis=-1)
    cb = jnp.concatenate([c_im, c_im], axis=-1)
    gb = ng if q == 1 else 16
    blk = lambda a, b: pl.BlockSpec((gb, a, b), lambda i: (i, 0, 0))
    return pl.pallas_call(
        functools.partial(_s5_prep_body, q),
        grid=(ng // gb,),
        in_specs=[blk(1, 1), blk(p, 1), blk(p, 1), blk(1, 2 * p), blk(1, 2 * p), blk(p, S5_CH), blk(p, S5_CH),
                  blk(S5_CH, 2 * p), blk(S5_CH, 2 * p), blk(S5_CH, 2 * p), _const_spec(rep.shape)],
        out_specs=[blk(w, w), blk(2 * p, w), blk(w, 2 * p)] + [blk(1, 2 * p)] * 4,
        out_shape=[jax.ShapeDtypeStruct((ng, w, w), BF16), jax.ShapeDtypeStruct((ng, 2 * p, w), BF16),
                   jax.ShapeDtypeStruct((ng, w, 2 * p), BF16)] + [jax.ShapeDtypeStruct((ng, 1, 2 * p), F32)] * 4,
        compiler_params=_params("arbitrary"),
        name=f"s5_prep_q{q}",
    )(ls, lrp, lip, lr2, li2, b_re, b_im, ccat, ca, cb, rep)


def _s5_single_token_mats(q, tt, wt, zt):
    c = S5_CH
    return tt[:, :c, (q - 1) * c:], wt[:, :, :c], zt[:, :c, :]


def _s5_mix_body(q, nb, nkb, npi, npt, perm, x_ref, g_ref, wu_ref, tt_ref, wt_ref, zt_ref, a1_ref, a2_ref, s0_ref,
                 dcol_ref, wglut_ref, bglu_ref, wo_ref, p_ref, sfin_ref, ut, yt, carry, wut):
    ng = tt_ref.shape[0]
    nk = nkb // nb
    qs = q // npt
    ch = qs * nkb
    tile = pl.program_id(0)
    ph = pl.program_id(1)

    @pl.when(jnp.logical_and(tile == 0, ph == 0))
    def _():
        carry[...] = s0_ref[...]
        cw_ = 256
        for c in range(0, wu_ref.shape[1], cw_):
            wut[c:c + cw_, :] = wu_ref[:, c:c + cw_].T

    def rows_of(r):
        if not perm:
            return x_ref[r]
        nseq, nblk, ndt, nr, _ = x_ref.shape
        flat = x_ref.reshape(nseq * nblk * ndt * nr, LANES)
        return jnp.concatenate(
            [jnp.concatenate([flat[pl.ds((k * ndt + dt) * nr + r, nseq, stride=nblk * ndt * nr), :]
                              for dt in range(ndt)], axis=1) for k in range(nblk)], axis=0)

    def project(j):
        step = 2 if qs % 2 == 0 else 1
        r0 = (j * qs) % (q // npi)
        for r in range(0, qs, step):
            xin = jnp.concatenate([rows_of(r0 + r + s) for s in range(step)], axis=0)
            u = _dot_nt(wut[...], _rms(xin, g_ref[...]).astype(BF16))
            sl = slice((j * qs + r) * nkb, (j * qs + r + step) * nkb)
            ut[:, sl] = u.astype(BF16)
            yt[:, sl] = dcol_ref[...] * u

    def finish(j):
        gt = _gelu_tanh(yt[:, j * ch:(j + 1) * ch])
        gate = _dot(wglut_ref[...], gt.astype(BF16)) + bglu_ref[...]
        y5 = (gt * _sigmoid(gate)).astype(BF16)
        p_ref[...] = _dot_tn(y5, wo_ref[...]).reshape(p_ref.shape)

    spp = npt // npi
    for hh in range(npi):
        @pl.when(ph == hh)
        def _():
            for j in range(hh * spp, (hh + 1) * spp):
                project(j)

    @pl.when(ph == npi - 1)
    def _():
        gu = S5_GROUP_BATCH
        bdot = lambda a, b: lax.dot_general(a, b, (((2,), (1,)), ((0,), (0,))), preferred_element_type=F32)
        bdot_nt = lambda a, b: lax.dot_general(a, b, (((2,), (2,)), ((0,), (0,))), preferred_element_type=F32)

        def groups(i, _):
            g0 = pl.multiple_of(i * gu, gu)
            r0 = pl.multiple_of(i * (gu * S5_CH), gu * S5_CH)
            gsl = pl.ds(g0, gu)
            rows = ut[pl.ds(r0, gu * S5_CH), :].reshape(gu, S5_CH, q * nkb)
            ugt = jnp.concatenate([rows[:, :, (q - 1 - j) * nkb:(q - j) * nkb] for j in range(q)], axis=1)
            y = bdot(tt_ref[gsl], ugt)
            vt = bdot(wt_ref[gsl], ugt)
            v = jnp.swapaxes(vt, 1, 2)
            v_sw = jnp.swapaxes(jnp.concatenate([vt[:, S5_STATE:], vt[:, :S5_STATE]], axis=1), 1, 2)
            a1 = a1_ref[gsl]
            a2 = a2_ref[gsl]
            s = carry[gsl]
            s_sw = pltpu.roll(s.reshape(gu * nb, 2 * S5_STATE), S5_STATE, 1).reshape(s.shape)
            prev = []
            for k in range(nk):
                prev.append(s)
                s, s_sw = (a1 * s + a2 * s_sw + v[:, k * nb:(k + 1) * nb, :],
                           a1 * s_sw - a2 * s + v_sw[:, k * nb:(k + 1) * nb, :])
            carry[gsl] = s
            sprev = jnp.concatenate(prev, axis=1).astype(BF16)
            y = y + bdot_nt(zt_ref[gsl], sprev)
            for t in range(q):
                yt[pl.ds(r0, gu * S5_CH), t * nkb:(t + 1) * nkb] += (
                    y[:, t * S5_CH:(t + 1) * S5_CH, :].reshape(gu * S5_CH, nkb))
            return 0

        lax.fori_loop(0, ng // gu, groups, 0)
        sfin_ref[...] = carry[...]

    for j in range(npt):
        @pl.when(ph == npi + j)
        def _():
            finish(j)


def _s5_mix(q, nb, x, ln_g, wu, tt, wt, zt, a1, a2, s0, dcol, wglut, bglu, wo):
    nseq, l, d = x.shape
    nblk = l // q
    nlt = nblk * nseq
    nkb = min(LANES, nlt)
    ntile = nlt // nkb
    tok = q * nkb
    npt = max(1, tok // TOKEN_TILE)
    dm = wo.shape[1]
    perm = q > 1
    if perm:
        half = 8
        npi = q // half
        assert nb == nseq and d % LANES == 0 and q % half == 0
        xv = x.reshape(nseq, nblk, npi, half, d // LANES, LANES).transpose(0, 1, 2, 4, 3, 5)
        assert npt % npi == 0
        x_spec = pl.BlockSpec((nseq, nkb // nseq, None, d // LANES, half, LANES),
                              lambda i, j: (0, i, jnp.minimum(j, npi - 1), 0, 0, 0))
    else:
        npi = 1
        xv = x.reshape(1, nseq, d)
        x_spec = pl.BlockSpec((1, nkb, d), lambda i, j: (0, i, 0))
    body = functools.partial(_s5_mix_body, q, nb, nkb, npi, npt, perm)
    p_spec = pl.BlockSpec((q // npt, nkb, dm), lambda i, j: (jnp.maximum(j - npi, 0), i, 0))
    return pl.pallas_call(
        body,
        grid=(ntile, npi + npt),
        in_specs=[x_spec, _const_spec(ln_g.shape), _const_spec(wu.shape), _const_spec(tt.shape),
                  _const_spec(wt.shape), _const_spec(zt.shape), _const_spec(a1.shape), _const_spec(a2.shape),
                  _const_spec(s0.shape), _const_spec(dcol.shape), _const_spec(wglut.shape),
                  _const_spec(bglu.shape), _const_spec(wo.shape)],
        out_specs=[p_spec, pl.BlockSpec(s0.shape, lambda i, j: (0, 0, 0))],
        out_shape=[jax.ShapeDtypeStruct((q, nlt, dm), F32), jax.ShapeDtypeStruct(s0.shape, F32)],
        scratch_shapes=[pltpu.VMEM((d, tok), BF16), pltpu.VMEM((d, tok), F32), pltpu.VMEM(s0.shape, F32),
                        pltpu.VMEM((wu.shape[1], wu.shape[0]), BF16)],
        compiler_params=_params("arbitrary", "arbitrary"),
        name=f"s5_mix_q{q}",
    )(xv, ln_g, wu, tt, wt, zt, a1, a2, s0, dcol, wglut, bglu, wo)


def _kv_body(m_ref, wk_ref, wv_ref, k_ref, v_ref, kb_ref, vb_ref):
    tm, rows, _ = k_ref.shape
    nt = rows // X_HEADS
    m = m_ref[...].astype(BF16)
    for w_ref, o_ref, ob_ref in ((wk_ref, k_ref, kb_ref), (wv_ref, v_ref, vb_ref)):
        r = _dot(m, w_ref[...])
        ob_ref[...] = r.astype(BF16)
        flat = o_ref.reshape(tm * rows, LANES)
        for h in range(X_HEADS):
            for dt in range(nt):
                c = (h * nt + dt) * LANES
                flat[pl.ds(dt * X_HEADS + h, tm, stride=rows), :] = r[:, c:c + LANES]


def _kv_proj(mem2d, wk, wv):
    t, d = mem2d.shape
    tm = min(TOKEN_TILE, t)
    rows = d // LANES
    row = pl.BlockSpec((tm, d), lambda i: (i, 0))
    tile = pl.BlockSpec((tm, rows, LANES), lambda i: (i, 0, 0))
    return pl.pallas_call(
        _kv_body, grid=(t // tm,),
        in_specs=[row, _const_spec(wk.shape), _const_spec(wv.shape)],
        out_specs=[tile, tile, row, row],
        out_shape=[jax.ShapeDtypeStruct((t, rows, LANES), F32)] * 2 + [jax.ShapeDtypeStruct((t, d), BF16)] * 2,
        compiler_params=_params("arbitrary"),
        name="kv_proj",
    )(mem2d, wk, wv)


def _tile_unview_kv(kv8, b, m):
    nt = kv8.shape[1] // X_HEADS
    return kv8.reshape(b, m, nt, X_HEADS, LANES).transpose(0, 1, 3, 2, 4).reshape(b, m, X_HEADS, nt * LANES)


def _attn_prompt_body(x_ref, ys_ref, p_ref, wo1_ref, g2_ref, wq_ref, k_ref, v_ref, wxo_ref, o_ref):
    d = x_ref.shape[-1]
    hd = d // X_HEADS
    x1 = x_ref[...] + _dot(ys_ref[...], wo1_ref[...]) + p_ref[...]
    hq = _rms(x1, g2_ref[...]).astype(BF16)
    qv = _dot(hq, wq_ref[...]).astype(BF16)
    kb = k_ref[...]
    vb = v_ref[...]
    outs = []
    for h in range(X_HEADS):
        sl = slice(h * hd, (h + 1) * hd)
        s = _dot_nt(qv[:, sl], kb[:, sl]) * (hd ** -0.5)
        e = jnp.exp(s - jnp.max(s, axis=-1, keepdims=True))
        p = (e / jnp.sum(e, axis=-1, keepdims=True)).astype(BF16)
        outs.append(_dot(p, vb[:, sl]))
    o = jnp.concatenate(outs, axis=1).astype(BF16)
    o_ref[...] = x1 + _dot(o, wxo_ref[...])


def _attn_prompt(x, ys, pm, wo1, g2, wq, mk, mv, wxo, make_rider):
    b, l, d = x.shape
    nm = mk.shape[1]
    tm = min(TOKEN_TILE, l)
    nl = l // tm
    row = pl.BlockSpec((None, tm, d), lambda i, j: (i, j, 0))
    kv = pl.BlockSpec((None, nm, d), lambda i, j: (i, 0, 0))
    ins = [x, ys, pm, wo1, g2, wq, mk, mv, wxo]
    specs = [row, row, row, _const_spec(wo1.shape), _const_spec(g2.shape), _const_spec(wq.shape), kv, kv,
             _const_spec(wxo.shape)]
    rider = make_rider(lambda i, j: i * nl + j, b * nl)
    body = functools.partial(_two_bodies, _attn_prompt_body, len(ins), 1, rider[0], len(rider[1]))
    return pl.pallas_call(
        body, grid=(b, nl),
        in_specs=specs + list(rider[2]),
        out_specs=[row] + list(rider[3]),
        out_shape=[jax.ShapeDtypeStruct((b, l, d), F32)] + list(rider[4]),
        compiler_params=_params("arbitrary", "arbitrary"),
        name="attn_prompt",
    )(*ins, *rider[1])


def _merge_q_body(x_ref, ys_ref, p_ref, wo1_ref, g2_ref, wq_ref, x1_ref, q_ref):
    x1 = x_ref[...] + _dot(ys_ref[...], wo1_ref[...]) + p_ref[...]
    x1_ref[...] = x1
    q_ref[...] = _dot(_rms(x1, g2_ref[...]).astype(BF16), wq_ref[...])


def _merge_q(x2d, ys, pm, wo1, g2, wq):
    t, d = x2d.shape
    return pl.pallas_call(
        _merge_q_body,
        out_shape=[jax.ShapeDtypeStruct((t, d), F32)] * 2,
        compiler_params=pltpu.CompilerParams(vmem_limit_bytes=VMEM_LIMIT),
        name="merge_q",
    )(x2d, ys, pm, wo1, g2, wq)


def _tile_view(kv):
    b, m, nh, hd = kv.shape
    nt = hd // LANES
    return kv.reshape(b, m, nh, nt, LANES).transpose(0, 1, 3, 2, 4).reshape(b, m, nt * nh, LANES)


def _tile_unview(o):
    b, rows, _ = o.shape
    nt = rows // X_HEADS
    return o.reshape(b, nt, X_HEADS, LANES).transpose(0, 2, 1, 3).reshape(b, rows * LANES)


def _attn_step_body(q_ref, k_ref, v_ref, o_ref):
    bt, nm, rows, _ = k_ref.shape
    hd = rows * LANES // X_HEADS
    for b in range(bt):
        r = jnp.sum(k_ref[b] * q_ref[b], axis=-1, keepdims=True)
        s = (r + pltpu.roll(r, X_HEADS, 1)) * (hd ** -0.5)
        e = jnp.exp(s - jnp.max(s, axis=0, keepdims=True))
        p = e / jnp.sum(e, axis=0, keepdims=True)
        o_ref[b] = jnp.sum(p * v_ref[b], axis=0)


def _attn_step_rider(q8, k8, v8, nsteps):
    nbt, rows, _ = q8.shape
    nm = k8.shape[1]
    assert rows == 2 * X_HEADS, "score assembly assumes two 128-lane tiles per head"
    assert nbt % nsteps == 0
    bt = nbt // nsteps
    qs = pl.BlockSpec((bt, rows, LANES), lambda i: (i, 0, 0))
    kv = pl.BlockSpec((bt, nm, rows, LANES), lambda i: (i, 0, 0, 0))
    return (_attn_step_body, (q8, k8, v8), (qs, kv, kv), (qs,),
            (jax.ShapeDtypeStruct((nbt, rows, LANES), F32),))


def _ffn_body(has_o, final, *refs):
    if has_o:
        x_ref, o_in_ref, wxo_ref, g3_ref, wg_ref, wu_ref, wd_ref, gf_ref, y_ref = refs
        x2 = x_ref[...] + _dot(o_in_ref[...].astype(BF16), wxo_ref[...])
    else:
        x_ref, g3_ref, wg_ref, wu_ref, wd_ref, gf_ref, y_ref = refs
        x2 = x_ref[...]
    hf = _rms(x2, g3_ref[...]).astype(BF16)
    ff = wg_ref.shape[1]
    x3 = x2
    for lo in range(0, ff, FF_CHUNK):
        hi = min(lo + FF_CHUNK, ff)
        act = (_silu(_dot(hf, wg_ref[:, lo:hi])) * _dot(hf, wu_ref[:, lo:hi])).astype(BF16)
        x3 = x3 + _dot(act, wd_ref[lo:hi, :])
    y_ref[...] = _rms(x3, gf_ref[...]) if final else x3


def _ffn(x2d, g3, wg, wu, wd, gf, final, o_in=None, wxo=None, rider=None):
    t, d = x2d.shape
    tm = min(TOKEN_TILE, t)
    row = pl.BlockSpec((tm, d), lambda i: (i, 0))
    has_o = o_in is not None
    ins = [x2d] + ([o_in, wxo] if has_o else []) + [g3, wg, wu, wd, gf]
    specs = [row] + ([row, _const_spec(wxo.shape)] if has_o else []) + [
        _const_spec(g3.shape), _const_spec(wg.shape), _const_spec(wu.shape), _const_spec(wd.shape),
        _const_spec(gf.shape)]
    body = functools.partial(_ffn_body, has_o, final)
    out_specs, out_shape = [row], [jax.ShapeDtypeStruct((t, d), F32)]
    if rider is not None:
        body = functools.partial(_two_bodies, body, len(ins), 1, rider[0], len(rider[1]))
        ins, specs = ins + list(rider[1]), specs + list(rider[2])
        out_specs, out_shape = out_specs + list(rider[3]), out_shape + list(rider[4])
    res = pl.pallas_call(
        body, grid=(t // tm,),
        in_specs=specs, out_specs=out_specs, out_shape=out_shape,
        compiler_params=_params("arbitrary"),
        name="ffn_o" if has_o else "ffn",
    )(*ins)
    return res[0] if rider is None else res


def _two_bodies(body_a, n_in_a, n_out_a, body_b, n_in_b, *refs):
    outs = refs[n_in_a + n_in_b:]
    body_a(*refs[:n_in_a], *outs[:n_out_a])
    body_b(*refs[n_in_a:n_in_a + n_in_b], *outs[n_out_a:])


def _layer_weights(i, ln1_g, w_in, conv_w, conv_b, dt_bias, a_log, d_ssd, gn_g, lam_re, lam_im, log_step,
                   b_re, b_im, c_re, c_im, d_s5, w_glu, b_glu, w_out, ln2_g, w_xq, w_xk, w_xv, w_xo, ln3_g,
                   w_gate, w_up, w_down):
    d = w_in.shape[1]
    n_heads = dt_bias.shape[1]
    dssd = n_heads * SSD_HEAD_DIM
    conv_dim = conv_w.shape[2]
    o1, o2, o3 = dssd, dssd + conv_dim, dssd + conv_dim + n_heads
    win = w_in[i]
    w = {}
    w["ln1"] = ln1_g[i].reshape(1, d)
    w["win"] = win.astype(BF16)
    w["dims"] = (dssd, conv_dim)
    w["wu_s5"] = w["win"][:, o3:]
    w["cw"] = conv_w[i]
    w["cb"] = conv_b[i].reshape(1, conv_dim)
    w["dtb"] = jnp.pad(dt_bias[i], (0, LANES - n_heads)).reshape(1, LANES)
    w["alog"] = jnp.pad(a_log[i], (0, LANES - n_heads)).reshape(1, LANES)
    w["dexp"] = jnp.repeat(d_ssd[i], SSD_HEAD_DIM).reshape(1, dssd)
    w["gn"] = gn_g[i].reshape(1, dssd)
    w["s5"] = (lam_re[i], lam_im[i], log_step[i], b_re[i], b_im[i], c_re[i], c_im[i])
    ds5 = d_s5.shape[1]
    w["dcol"] = d_s5[i].reshape(ds5, 1)
    w["wglut"] = w_glu[i].T.astype(BF16)
    w["bglu"] = b_glu[i].reshape(ds5, 1)
    w["wo1"] = w_out[i, :dssd].astype(BF16)
    w["wo2"] = w_out[i, dssd:].astype(BF16)
    w["ln2"] = ln2_g[i].reshape(1, d)
    w["wq"] = w_xq[i].astype(BF16)
    w["wk"] = w_xk[i].astype(BF16)
    w["wv"] = w_xv[i].astype(BF16)
    w["wxo"] = w_xo[i].astype(BF16)
    w["ln3"] = ln3_g[i].reshape(1, d)
    w["wg"] = w_gate[i].astype(BF16)
    w["wu"] = w_up[i].astype(BF16)
    w["wdn"] = w_down[i].astype(BF16)
    return w


def _state_to_rows(s_re, s_im):
    return jnp.concatenate([s_re, s_im], axis=-1).transpose(1, 0, 2)


def _rows_to_state(s):
    p = s.shape[-1] // 2
    st = s.transpose(1, 0, 2)
    return st[..., :p], st[..., p:]


def kernel(x_prompt, x_sample, mem_prompt, state_conv, state_ssm, state_s5_re, state_s5_im, cache_mem_k, cache_mem_v, ln1_g, w_in, conv_w, conv_b, dt_bias, a_log, d_ssd, gn_g, lam_re, lam_im, log_step, b_re, b_im, c_re, c_im, d_s5, w_glu, b_glu, w_out, ln2_g, w_xq, w_xk, w_xv, w_xo, ln3_g, w_gate, w_up, w_down, final_g):
    bp, seq, d = x_prompt.shape
    bs = x_sample.shape[0]
    depth = w_in.shape[0]
    nm = mem_prompt.shape[1]
    ng, ns = lam_re.shape[1], lam_re.shape[2]
    gf = final_g.reshape(1, d)
    hp = x_prompt
    hs = x_sample.reshape(bs, d)
    outs = {k: [] for k in ("conv_p", "ssm_p", "re_p", "im_p", "mk_p", "mv_p", "conv_s", "ssm_s", "re_s", "im_s")}
    yp = ys_out = None
    for i in range(depth):
        w = _layer_weights(i, ln1_g, w_in, conv_w, conv_b, dt_bias, a_log, d_ssd, gn_g, lam_re, lam_im, log_step,
                           b_re, b_im, c_re, c_im, d_s5, w_glu, b_glu, w_out, ln2_g, w_xq, w_xk, w_xv, w_xo, ln3_g,
                           w_gate, w_up, w_down)
        last = i == depth - 1
        mk8, mv8, mk, mv = _kv_proj(mem_prompt.reshape(bp * nm, d), w["wk"], w["wv"])
        mk = mk.reshape(bp, nm, d)
        mv = mv.reshape(bp, nm, d)
        y_ssd, conv_p, ssm_p = _ssd_prompt(hp, w["ln1"], w["win"], w["cw"], w["cb"], w["dtb"], w["alog"], w["dexp"],
                                           w["gn"])
        q = S5_Q
        tt, wt, zt, a1, a2, a11, a21 = _s5_prep(q, *w["s5"])
        s0 = jnp.zeros((ng, bp, 2 * ns), F32)
        pperm, sfin = _s5_mix(q, bp, hp, w["ln1"], w["wu_s5"],tt, wt, zt, a1, a2, s0, w["dcol"], w["wglut"],
                              w["bglu"], w["wo2"])
        pm = pperm.reshape(q, seq // q, bp, d).transpose(2, 1, 0, 3).reshape(bp, seq, d)
        re_p, im_p = _rows_to_state(sfin)
        zs, xbcs, dts = _inproj(hs, w["ln1"], w["win"], *w["dims"])
        ssd_rider = functools.partial(_ssd_step_rider, zs, xbcs, dts, state_conv[i].transpose(1, 0, 2), state_ssm[i],
                                      w["cw"], w["cb"], w["dtb"], w["alog"], w["dexp"], w["gn"])
        x2, ys_ssd, conv_s_t, ssm_s = _attn_prompt(hp, y_ssd, pm, w["wo1"], w["ln2"], w["wq"], mk, mv, w["wxo"],
                                                   ssd_rider)
        ys_ssd = ys_ssd.reshape(bs, -1)
        conv_s_t = conv_s_t.reshape(CONV_W - 1, bs, -1)
        tt1, wt1, zt1 = _s5_single_token_mats(q, tt, wt, zt)
        s0s = _state_to_rows(state_s5_re[i], state_s5_im[i])
        ps, sfin_s = _s5_mix(1, bs, hs.reshape(bs, 1, d), w["ln1"], w["wu_s5"],tt1, wt1, zt1, a11, a21, s0s,
                             w["dcol"], w["wglut"], w["bglu"], w["wo2"])
        re_s, im_s = _rows_to_state(sfin_s)
        x1s, qs = _merge_q(hs, ys_ssd, ps.reshape(bs, d), w["wo1"], w["ln2"], w["wq"])
        attn_rider = _attn_step_rider(_tile_view(qs.reshape(bs, 1, X_HEADS, d // X_HEADS))[:, 0],
                                      _tile_view(cache_mem_k[i]), _tile_view(cache_mem_v[i]),
                                      (bp * seq) // min(TOKEN_TILE, bp * seq))
        hp, o_s = _ffn(x2.reshape(bp * seq, d), w["ln3"], w["wg"], w["wu"], w["wdn"], gf, last, rider=attn_rider)
        hp = hp.reshape(bp, seq, d)
        hs = _ffn(x1s, w["ln3"], w["wg"], w["wu"], w["wdn"], gf, last, o_in=_tile_unview(o_s), wxo=w["wxo"])
        for k, v in (("conv_p", conv_p), ("ssm_p", ssm_p), ("re_p", re_p), ("im_p", im_p),
                     ("mk_p", _tile_unview_kv(mk8, bp, nm)), ("mv_p", _tile_unview_kv(mv8, bp, nm)),
                     ("conv_s", conv_s_t.transpose(1, 0, 2)), ("ssm_s", ssm_s), ("re_s", re_s), ("im_s", im_s)):
            outs[k].append(v)
    st = lambda k: jnp.stack(outs[k])
    return (hp, hs.reshape(bs, 1, d), st("conv_p"), st("ssm_p"), st("re_p"), st("im_p"), st("mk_p"), st("mv_p"),
            st("conv_s"), st("ssm_s"), st("re_s"), st("im_s"))
```

```python
import functools
import math

import jax
import jax.numpy as jnp
from jax import lax
from jax.experimental import pallas as pl
from jax.experimental.pallas import tpu as pltpu

F32 = jnp.float32
BF16 = jnp.bfloat16
EPS = 1e-6

LANES = 128
VMEM_LIMIT = 56 * 1024 * 1024

SSD_HEAD_DIM = 64
SSD_STATE = 128
SSD_GROUPS = 2
SSD_CHUNK = 128
CONV_W = 4
S5_CH = 16
S5_STATE = 64
S5_Q = 16
S5_GROUP_BATCH = 16
X_HEADS = 4
TOKEN_TILE = 512
PROJ_PIECE = 512
SSD_SEQS_PER_STEP = 2
FF_CHUNK = 1536


def _const_spec(shape):
    nd = len(shape)
    return pl.BlockSpec(shape, lambda *_: (0,) * nd, pipeline_mode=pl.Buffered(1))


def _params(*sem):
    return pltpu.CompilerParams(dimension_semantics=sem, vmem_limit_bytes=VMEM_LIMIT)


def _rms(x, g):
    return x * lax.rsqrt(jnp.mean(x * x, axis=-1, keepdims=True) + EPS) * g


def _sigmoid(x):
    return 1.0 / (1.0 + jnp.exp(-x))


def _silu(x):
    return x * _sigmoid(x)


def _gelu_tanh(x):
    return 0.5 * x * (1.0 + jnp.tanh(math.sqrt(2.0 / math.pi) * (x + 0.044715 * (x * x * x))))


def _softplus(x):
    return jnp.maximum(x, 0.0) + jnp.log1p(jnp.exp(-jnp.abs(x)))


def _dot(a, b):
    return jnp.dot(a, b, preferred_element_type=F32)


def _dot_nt(a, b):
    return lax.dot_general(a, b, (((1,), (1,)), ((), ())), preferred_element_type=F32)


def _dot_tn(a, b):
    return lax.dot_general(a, b, (((0,), (0,)), ((), ())), preferred_element_type=F32)


def _proj_windows(d_ssd, conv_dim):
    assert d_ssd % LANES == 0 and conv_dim % LANES == 0
    return (0, d_ssd), (d_ssd, d_ssd + conv_dim), (d_ssd + conv_dim, d_ssd + conv_dim + LANES)


def _inproj_body(wins, x_ref, g_ref, w_ref, z_ref, xbc_ref, dt_ref):
    h = _rms(x_ref[...], g_ref[...]).astype(BF16)
    for (lo, hi), o_ref in zip(wins, (z_ref, xbc_ref, dt_ref)):
        o_ref[...] = _dot(h, w_ref[:, lo:hi]).reshape(o_ref.shape)


def _inproj(x2d, ln_g, w_in, d_ssd, conv_dim, group):
    t, d = x2d.shape
    wins = _proj_windows(d_ssd, conv_dim)
    return pl.pallas_call(
        functools.partial(_inproj_body, wins),
        out_shape=[jax.ShapeDtypeStruct((t // group, group, hi - lo), F32) for lo, hi in wins],
        compiler_params=pltpu.CompilerParams(vmem_limit_bytes=VMEM_LIMIT),
        name="inproj",
    )(x2d, ln_g, w_in)


def _expand_heads(v, n_heads):
    rows = v.shape[0]
    lane = lax.broadcasted_iota(jnp.int32, (rows, LANES), 1)
    pieces = [jnp.where(lane < SSD_HEAD_DIM, v[:, 2 * j:2 * j + 1], v[:, 2 * j + 1:2 * j + 2])
              for j in range(n_heads // 2)]
    return jnp.concatenate(pieces, axis=1)


def _ssd_chunk_math(z, x, dt_raw, last_chunk, cw_ref, cb_ref, dtb_ref, alog_ref, dexp_ref, gn_ref, e2_ref,
                    y_ref, conv_ref, ssm_ref, xpad, state, side_jobs=()):
    jobs = list(side_jobs)

    def side(n=1):
        for _ in range(n):
            if jobs:
                jobs.pop(0)()

    q = SSD_CHUNK
    d_ssd = z.shape[-1]
    n_heads = d_ssd // SSD_HEAD_DIM
    hpg = n_heads // SSD_GROUPS
    gw = hpg * SSD_HEAD_DIM
    nct = xpad.shape[0]
    xt = d_ssd // LANES

    for t in range(nct):
        xpad[t, 8:8 + q, :] = x[:, t * LANES:(t + 1) * LANES]
    cw = cw_ref[...]
    conv = (cw[3] * xpad[:, 8:8 + q, :] + cw[2] * xpad[:, 7:7 + q, :] + cw[1] * xpad[:, 6:6 + q, :]
            + cw[0] * xpad[:, 5:5 + q, :] + cb_ref[...])
    xpad[:, 0:8, :] = xpad[:, q:q + 8, :]
    xact = _silu(conv)
    xs = jnp.concatenate([xact[t] for t in range(xt)], axis=1)
    bm = xact[xt:xt + SSD_GROUPS].astype(BF16)
    cm = xact[xt + SSD_GROUPS:].astype(BF16)
    side()

    dt = _softplus(dt_raw + dtb_ref[...])
    a = -jnp.exp(alog_ref[...])
    row = lax.broadcasted_iota(jnp.int32, (q, q), 0)
    col = lax.broadcasted_iota(jnp.int32, (q, q), 1)
    causal = row >= col
    tri = jnp.where(causal, 1.0, 0.0).astype(F32)
    acum = jnp.dot(tri, dt * a, precision=lax.Precision.HIGHEST, preferred_element_type=F32)
    acum_t = acum.T
    last = acum[q - 1:q, :]

    def expand(v):
        hi = v.astype(BF16)
        lo = (v - hi.astype(F32)).astype(BF16)
        return _dot(jnp.concatenate([hi, lo], axis=1), e2_ref[...])

    dt_e = expand(dt)
    ea_e = expand(jnp.exp(acum))
    dend_e = expand(jnp.exp(last - acum))

    dtx = xs * dt_e
    dtx_b = dtx.astype(BF16)
    xdec_b = (dtx * dend_e).astype(BF16)
    side()

    lane = lax.broadcasted_iota(jnp.int32, (q, LANES), 1)
    zero_b = jnp.zeros((q, LANES), BF16)
    y_tiles = []
    for g in range(SSD_GROUPS):
        bg = bm[g]
        cg = cm[g]
        cb = _dot_nt(cg, bg)
        gs = slice(g * gw, (g + 1) * gw)
        st_g = state[:, gs]
        y_off = _dot(cg, st_g.astype(BF16)) * ea_e[:, gs]
        for pr in range(hpg // 2):
            tile = g * (hpg // 2) + pr
            ms = []
            for h in (2 * tile, 2 * tile + 1):
                seg = acum[:, h:h + 1] - acum_t[h:h + 1, :]
                lmat = jnp.exp(jnp.where(causal, seg, -jnp.inf))
                ms.append((cb * lmat).astype(BF16))
            dtile = dtx_b[:, tile * LANES:(tile + 1) * LANES]
            rhs = jnp.concatenate([jnp.where(lane < SSD_HEAD_DIM, dtile, zero_b),
                                   jnp.where(lane >= SSD_HEAD_DIM, dtile, zero_b)], axis=0)
            y_tiles.append(_dot(jnp.concatenate(ms, axis=1), rhs)
                           + y_off[:, pr * LANES:(pr + 1) * LANES])
            if pr % 2 == 1:
                side()
        new = _dot_tn(bg, xdec_b[:, gs])
        state[:, gs] = st_g * ea_e[q - 1:q, gs] + new
    y = jnp.concatenate(y_tiles, axis=1) + dexp_ref[...] * xs
    y = y * _silu(z)
    ms = jnp.mean(y * y, axis=-1, keepdims=True)
    side(len(jobs))
    y_ref[...] = (y * lax.rsqrt(ms + EPS) * gn_ref[...]).astype(y_ref.dtype)

    @pl.when(last_chunk)
    def _():
        conv_ref[...] = x[q - (CONV_W - 1):q, :]
        ssm_ref[...] = state[...].T.reshape(ssm_ref.shape)


def _ssd_prompt_body(wins, x_ref, g_ref, w_ref, cw_ref, cb_ref, dtb_ref, alog_ref, dexp_ref, gn_ref,
                     e2_ref, y_ref, conv_ref, ssm_ref, za, xa, da, zb, xb, db, xpad, state):
    i = pl.program_id(0)
    j = pl.program_id(1)
    nc = pl.num_programs(1) - 1
    ns, q, d = x_ref.shape

    @pl.when(jnp.logical_and(i == 0, j == 0))
    def _():
        for r in (zb, xb, db):
            r[...] = jnp.zeros(r.shape, F32)

    @pl.when(j <= 1)
    def _():
        xpad[:, :, 0:8, :] = jnp.zeros(xpad.shape[:2] + (8, LANES), F32)
        state[...] = jnp.zeros(state.shape, F32)

    def step(wr, rd):
        xin = x_ref[...].reshape(ns * q, d)
        msx = jnp.mean(xin * xin, axis=-1, keepdims=True)
        hbox = []

        def proj_piece(dst, base, lo, hi):
            def run():
                if not hbox:
                    hbox.append((xin * lax.rsqrt(msx + EPS) * g_ref[...]).astype(BF16))
                dst[:, lo:hi] = _dot(hbox[0], w_ref[:, base + lo:base + hi])
            return run

        pieces = []
        for dst, (w_lo, w_hi) in ((wr[0], wins[0]), (wr[2], wins[2]), (wr[1], wins[1])):
            n = w_hi - w_lo
            pieces += [proj_piece(dst, w_lo, lo, min(lo + PROJ_PIECE, n)) for lo in range(0, n, PROJ_PIECE)]
        per = -(-len(pieces) // ns)
        for s in range(ns):
            rows = slice(s * q, (s + 1) * q)
            _ssd_chunk_math(rd[0][rows, :], rd[1][rows, :], rd[2][rows, :], j == nc, cw_ref, cb_ref, dtb_ref,
                            alog_ref, dexp_ref, gn_ref, e2_ref, y_ref.at[s], conv_ref.at[s], ssm_ref.at[s],
                            xpad.at[s], state.at[s], pieces[s * per:(s + 1) * per])

    @pl.when(j % 2 == 0)
    def _():
        step((za, xa, da), (zb, xb, db))

    @pl.when(j % 2 == 1)
    def _():
        step((zb, xb, db), (za, xa, da))


def _ssd_prompt(x, ln_g, w_in, cw, cb, dtb, alog, dexp, gn):
    b, l, d = x.shape
    conv_dim = cw.shape[1]
    d_ssd = conv_dim - 2 * SSD_GROUPS * SSD_STATE
    wins = _proj_windows(d_ssd, conv_dim)
    n_heads = d_ssd // SSD_HEAD_DIM
    q = SSD_CHUNK
    nc = l // q
    assert SSD_STATE == LANES and conv_dim == d_ssd + 2 * SSD_GROUPS * SSD_STATE
    nct = conv_dim // LANES
    cw4 = cw.reshape(CONV_W, nct, 1, LANES)
    cb3 = cb.reshape(nct, 1, LANES)
    e2 = (jnp.arange(2 * LANES)[:, None] % LANES == jnp.arange(d_ssd)[None, :] // SSD_HEAD_DIM).astype(BF16)
    ns = SSD_SEQS_PER_STEP if b % SSD_SEQS_PER_STEP == 0 else 1
    slot = lambda n: pltpu.VMEM((ns * q, n), F32)
    return pl.pallas_call(
        functools.partial(_ssd_prompt_body, wins),
        grid=(b // ns, nc + 1),
        in_specs=[pl.BlockSpec((ns, q, d), lambda i, j: (i, jnp.minimum(j, nc - 1), 0)),
                  _const_spec(ln_g.shape), _const_spec(w_in.shape),
                  _const_spec(cw4.shape), _const_spec(cb3.shape), _const_spec(dtb.shape), _const_spec(alog.shape),
                  _const_spec(dexp.shape), _const_spec(gn.shape), _const_spec(e2.shape)],
        out_specs=[pl.BlockSpec((ns, q, d_ssd), lambda i, j: (i, jnp.maximum(j - 1, 0), 0)),
                   pl.BlockSpec((ns, CONV_W - 1, conv_dim), lambda i, j: (i, 0, 0)),
                   pl.BlockSpec((ns, n_heads, SSD_HEAD_DIM, SSD_STATE), lambda i, j: (i, 0, 0, 0))],
        out_shape=[jax.ShapeDtypeStruct((b, l, d_ssd), BF16),
                   jax.ShapeDtypeStruct((b, CONV_W - 1, conv_dim), F32),
                   jax.ShapeDtypeStruct((b, n_heads, SSD_HEAD_DIM, SSD_STATE), F32)],
        scratch_shapes=[slot(d_ssd), slot(conv_dim), slot(LANES), slot(d_ssd), slot(conv_dim), slot(LANES),
                        pltpu.VMEM((ns, nct, q + 8, LANES), F32), pltpu.VMEM((ns, SSD_STATE, d_ssd), F32)],
        compiler_params=_params("arbitrary", "arbitrary"),
        name="ssd_prompt",
    )(x, ln_g, w_in, cw4, cb3, dtb, alog, dexp, gn, e2)


def _ssd_step_body(z_ref, xbc_ref, dt_ref, cs_ref, st_ref, cw_ref, cb_ref, dtb_ref, alog_ref, dexp_ref, gn_ref,
                   y_ref, conv_ref, ssm_ref):
    bt, d_ssd = z_ref.shape
    n_heads = d_ssd // SSD_HEAD_DIM
    hpg = n_heads // SSD_GROUPS
    x = xbc_ref[...]
    cw = cw_ref[...]
    cs = [cs_ref[:, k, :] for k in range(CONV_W - 1)]
    conv = cw[0:1] * cs[0] + cw[1:2] * cs[1] + cw[2:3] * cs[2] + cw[3:4] * x + cb_ref[...]
    conv_ref[:, 0, :] = cs[1]
    conv_ref[:, 1, :] = cs[2]
    conv_ref[:, 2, :] = x
    xact = _silu(conv)
    xs = xact[:, :d_ssd]
    bm = xact[:, d_ssd:d_ssd + SSD_GROUPS * SSD_STATE]
    cm = xact[:, d_ssd + SSD_GROUPS * SSD_STATE:]
    dt = _softplus(dt_ref[...] + dtb_ref[...])
    da = jnp.exp(dt * (-jnp.exp(alog_ref[...])))
    dtx = xs * _expand_heads(dt, n_heads)
    pad = jnp.zeros((LANES - bt, d_ssd), F32)
    to_cols = lambda v: jnp.concatenate([v, pad], axis=0).T
    dtx_t = to_cols(dtx)
    da_t = to_cols(_expand_heads(da, n_heads))
    lane = lax.broadcasted_iota(jnp.int32, (d_ssd, LANES), 1)
    gw = hpg * SSD_HEAD_DIM
    rows_of = lambda v, b: jnp.concatenate(
        [jnp.broadcast_to(v[b:b + 1, g * SSD_STATE:(g + 1) * SSD_STATE], (gw, SSD_STATE)) for g in range(SSD_GROUPS)],
        axis=0)
    y_t = jnp.zeros((d_ssd, LANES), F32)
    for b in range(bt):
        s0 = st_ref[b].reshape(d_ssd, SSD_STATE)
        s_new = s0 * da_t[:, b:b + 1] + dtx_t[:, b:b + 1] * rows_of(bm, b)
        ssm_ref[b] = s_new.reshape(n_heads, SSD_HEAD_DIM, SSD_STATE)
        ycol = jnp.sum(s_new * rows_of(cm, b), axis=1, keepdims=True)
        y_t = jnp.where(lane == b, ycol, y_t)
    y = y_t.T[:bt, :] + dexp_ref[...] * xs
    y = y * _silu(z_ref[...])
    y_ref[...] = _rms(y, gn_ref[...]).astype(y_ref.dtype)


def _ssd_step_rider(z, xbc, dt, conv_state, ssm_state, cw, cb, dtb, alog, dexp, gn, step_of, nsteps):
    _, bt, d_ssd = z.shape
    conv_dim = xbc.shape[-1]
    n_heads = d_ssd // SSD_HEAD_DIM
    assert z.shape[0] == nsteps and conv_state.shape[0] == nsteps * bt
    row = lambda n: pl.BlockSpec((None, bt, n), lambda *g: (step_of(*g), 0, 0))
    cs_spec = pl.BlockSpec((bt, CONV_W - 1, conv_dim), lambda *g: (step_of(*g), 0, 0))
    st_spec = pl.BlockSpec((bt, n_heads, SSD_HEAD_DIM, SSD_STATE), lambda *g: (step_of(*g), 0, 0, 0))
    cst = lambda a: pl.BlockSpec(a.shape, lambda *g: (0,) * a.ndim, pipeline_mode=pl.Buffered(1))
    ins = (z, xbc, dt, conv_state, ssm_state, cw, cb, dtb, alog, dexp, gn)
    specs = (row(d_ssd), row(conv_dim), row(LANES), cs_spec, st_spec, cst(cw), cst(cb), cst(dtb), cst(alog),
             cst(dexp), cst(gn))
    out_shapes = (jax.ShapeDtypeStruct((nsteps, bt, d_ssd), BF16),
                  jax.ShapeDtypeStruct(conv_state.shape, F32),
                  jax.ShapeDtypeStruct(ssm_state.shape, F32))
    return (_ssd_step_body, ins, specs, (row(d_ssd), cs_spec, st_spec), out_shapes)


def _split_bf16(x):
    hi = x.astype(BF16)
    lo = (x - hi.astype(F32)).astype(BF16)
    return hi, lo


def _s5_prep_body(q, ls_ref, lrp_ref, lip_ref, lr2_ref, li2_ref, br_ref, bi_ref, ccat_ref, ca_ref, cb_ref, rep_ref,
                  tt_ref, wt_ref, zt_ref, a1_ref, a2_ref, a1s_ref, a2s_ref):
    ng = ls_ref.shape[0]
    w = q * S5_CH
    step = jnp.exp(ls_ref[...])
    lr, li = lrp_ref[...], lip_ref[...]
    mag = jnp.exp(lr * step)
    ang = li * step
    lbr = mag * jnp.cos(ang)
    lbi = mag * jnp.sin(ang)
    den = lr * lr + li * li
    kr = ((lbr - 1.0) * lr + lbi * li) / den
    ki = (lbi * lr - (lbr - 1.0) * li) / den
    br, bi = br_ref[...], bi_ref[...]
    bbr = kr * br - ki * bi
    bbi = kr * bi + ki * br
    if q > 1:
        np_ = br.shape[1]
        rep = lambda v: jnp.dot(v.reshape(ng * np_, S5_CH), rep_ref[...], precision=lax.Precision.HIGHEST,
                                preferred_element_type=F32).reshape(ng, np_, w)
        bbr, bbi = rep(bbr), rep(bbi)
    if q == 1:
        ball = jnp.concatenate([bbr, bbi], axis=1)
    else:
        d = lax.broadcasted_iota(jnp.int32, (1, 1, w), 2) // S5_CH
        fr, fi = lbr, lbi
        pr = pi = None
        for b in range((q - 1).bit_length()):
            bit = ((d >> b) & 1) == 1
            sr, si = jnp.where(bit, fr, 1.0), jnp.where(bit, fi, 0.0)
            pr, pi = (sr, si) if pr is None else (pr * sr - pi * si, pr * si + pi * sr)
            fr, fi = fr * fr - fi * fi, 2.0 * (fr * fi)
        ball = jnp.concatenate([pr * bbr - pi * bbi, pr * bbi + pi * bbr], axis=1)
    wt_ref[...] = ball.astype(BF16)

    lane3 = lax.broadcasted_iota(jnp.int32, (1, 1, 2 * S5_STATE), 2)
    first = lane3 < S5_STATE
    csign = jnp.where(first, ccat_ref[...], -ccat_ref[...])
    ch, cl = _split_bf16(csign)
    bh, bl = _split_bf16(ball)
    bdot = lambda x, y: lax.dot_general(x, y, (((2,), (1,)), ((0,), (0,))), preferred_element_type=F32)
    kall = bdot(ch, bh) + bdot(ch, bl) + bdot(cl, bh)
    k2 = kall.reshape(ng * S5_CH, w)
    lane2 = lax.broadcasted_iota(jnp.int32, (ng * S5_CH, w), 1)
    for t in range(q):
        sh = (q - 1 - t) * S5_CH
        r = pltpu.roll(k2, sh, 1) if sh else k2
        r = jnp.where(lane2 >= sh, r, 0.0)
        tt_ref[:, t * S5_CH:(t + 1) * S5_CH, :] = r.reshape(ng, S5_CH, w).astype(BF16)

    lr2, li2 = lr2_ref[...], li2_ref[...]
    t1 = (lax.broadcasted_iota(jnp.int32, (1, q, 1), 1) + 1).astype(F32)
    zm = jnp.exp(t1 * (lr2 * step))
    za = t1 * (li2 * step)
    zr = zm * jnp.cos(za)
    zi = zm * jnp.sin(za)
    ca, cb = ca_ref[...], cb_ref[...]
    for t in range(q):
        prt = zr[:, t:t + 1, :]
        pit = zi[:, t:t + 1, :]
        zt = jnp.where(first, ca * prt - cb * pit, -(ca * pit) - cb * prt)
        zt_ref[:, t * S5_CH:(t + 1) * S5_CH, :] = zt.astype(BF16)
    qf = float(q)
    mq = jnp.exp(qf * (lr2 * step))
    aq = qf * (li2 * step)
    ar = mq * jnp.cos(aq)
    ai = mq * jnp.sin(aq)
    a1_ref[...] = ar
    a2_ref[...] = jnp.where(first, -ai, ai)
    a1s_ref[...] = zr[:, 0:1, :]
    a2s_ref[...] = jnp.where(first, -zi[:, 0:1, :], zi[:, 0:1, :])


def _s5_prep(q, lam_re, lam_im, log_step, b_re, b_im, c_re, c_im):
    ng, p = lam_re.shape
    w = q * S5_CH
    ls = log_step.reshape(ng, 1, 1)
    lrp = lam_re.reshape(ng, p, 1)
    lip = lam_im.reshape(ng, p, 1)
    lr2 = jnp.concatenate([lam_re, lam_re], axis=-1).reshape(ng, 1, 2 * p)
    li2 = jnp.concatenate([lam_im, lam_im], axis=-1).reshape(ng, 1, 2 * p)
    rep = (jnp.arange(S5_CH)[:, None] == jnp.arange(w)[None, :] % S5_CH).astype(F32)
    ccat = jnp.concatenate([c_re, c_im], axis=-1)
    ca = jnp.concatenate([c_re, c_re], axis=-1)
    cb = jnp.concatenate([c_im, c_im], axis=-1)
    gb = ng if q == 1 else 16
    blk = lambda a, b: pl.BlockSpec((gb, a, b), lambda i: (i, 0, 0))
    return pl.pallas_call(
        functools.partial(_s5_prep_body, q),
        grid=(ng // gb,),
        in_specs=[blk(1, 1), blk(p, 1), blk(p, 1), blk(1, 2 * p), blk(1, 2 * p), blk(p, S5_CH), blk(p, S5_CH),
                  blk(S5_CH, 2 * p), blk(S5_CH, 2 * p), blk(S5_CH, 2 * p), _const_spec(rep.shape)],
        out_specs=[blk(w, w), blk(2 * p, w), blk(w, 2 * p)] + [blk(1, 2 * p)] * 4,
        out_shape=[jax.ShapeDtypeStruct((ng, w, w), BF16), jax.ShapeDtypeStruct((ng, 2 * p, w), BF16),
                   jax.ShapeDtypeStruct((ng, w, 2 * p), BF16)] + [jax.ShapeDtypeStruct((ng, 1, 2 * p), F32)] * 4,
        compiler_params=_params("arbitrary"),
        name=f"s5_prep_q{q}",
    )(ls, lrp, lip, lr2, li2, b_re, b_im, ccat, ca, cb, rep)


def _s5_single_token_mats(q, tt, wt, zt):
    c = S5_CH
    return tt[:, :c, (q - 1) * c:], wt[:, :, :c], zt[:, :c, :]


def _s5_mix_body(q, nb, nkb, npi, npt, perm, x_ref, g_ref, wu_ref, tt_ref, wt_ref, zt_ref, a1_ref, a2_ref, s0_ref,
                 dcol_ref, wglut_ref, bglu_ref, wo_ref, p_ref, sfin_ref, ut, yt, carry, wut):
    ng = tt_ref.shape[0]
    nk = nkb // nb
    qs = q // npt
    ch = qs * nkb
    tile = pl.program_id(0)
    ph = pl.program_id(1)

    @pl.when(jnp.logical_and(tile == 0, ph == 0))
    def _():
        carry[...] = s0_ref[...]
        cw_ = 256
        for c in range(0, wu_ref.shape[1], cw_):
            wut[c:c + cw_, :] = wu_ref[:, c:c + cw_].T

    def rows_of(r):
        if not perm:
            return x_ref[r]
        nseq, nblk, ndt, nr, _ = x_ref.shape
        flat = x_ref.reshape(nseq * nblk * ndt * nr, LANES)
        return jnp.concatenate(
            [jnp.concatenate([flat[pl.ds((k * ndt + dt) * nr + r, nseq, stride=nblk * ndt * nr), :]
                              for dt in range(ndt)], axis=1) for k in range(nblk)], axis=0)

    def project(j):
        step = 2 if qs % 2 == 0 else 1
        r0 = (j * qs) % (q // npi)
        for r in range(0, qs, step):
            xin = jnp.concatenate([rows_of(r0 + r + s) for s in range(step)], axis=0)
            u = _dot_nt(wut[...], _rms(xin, g_ref[...]).astype(BF16))
            sl = slice((j * qs + r) * nkb, (j * qs + r + step) * nkb)
            ut[:, sl] = u.astype(BF16)
            yt[:, sl] = dcol_ref[...] * u

    def finish(j):
        gt = _gelu_tanh(yt[:, j * ch:(j + 1) * ch])
        gate = _dot(wglut_ref[...], gt.astype(BF16)) + bglu_ref[...]
        y5 = (gt * _sigmoid(gate)).astype(BF16)
        p_ref[...] = _dot_tn(y5, wo_ref[...]).reshape(p_ref.shape)

    spp = npt // npi
    for hh in range(npi):
        @pl.when(ph == hh)
        def _():
            for j in range(hh * spp, (hh + 1) * spp):
                project(j)

    @pl.when(ph == npi - 1)
    def _():
        gu = S5_GROUP_BATCH
        bdot = lambda a, b: lax.dot_general(a, b, (((2,), (1,)), ((0,), (0,))), preferred_element_type=F32)
        bdot_nt = lambda a, b: lax.dot_general(a, b, (((2,), (2,)), ((0,), (0,))), preferred_element_type=F32)

        def groups(i, _):
            g0 = pl.multiple_of(i * gu, gu)
            r0 = pl.multiple_of(i * (gu * S5_CH), gu * S5_CH)
            gsl = pl.ds(g0, gu)
            rows = ut[pl.ds(r0, gu * S5_CH), :].reshape(gu, S5_CH, q * nkb)
            ugt = jnp.concatenate([rows[:, :, (q - 1 - j) * nkb:(q - j) * nkb] for j in range(q)], axis=1)
            y = bdot(tt_ref[gsl], ugt)
            vt = bdot(wt_ref[gsl], ugt)
            v = jnp.swapaxes(vt, 1, 2)
            v_sw = jnp.swapaxes(jnp.concatenate([vt[:, S5_STATE:], vt[:, :S5_STATE]], axis=1), 1, 2)
            a1 = a1_ref[gsl]
            a2 = a2_ref[gsl]
            s = carry[gsl]
            s_sw = pltpu.roll(s.reshape(gu * nb, 2 * S5_STATE), S5_STATE, 1).reshape(s.shape)
            prev = []
            for k in range(nk):
                prev.append(s)
                s, s_sw = (a1 * s + a2 * s_sw + v[:, k * nb:(k + 1) * nb, :],
                           a1 * s_sw - a2 * s + v_sw[:, k * nb:(k + 1) * nb, :])
            carry[gsl] = s
            sprev = jnp.concatenate(prev, axis=1).astype(BF16)
            y = y + bdot_nt(zt_ref[gsl], sprev)
            for t in range(q):
                yt[pl.ds(r0, gu * S5_CH), t * nkb:(t + 1) * nkb] += (
                    y[:, t * S5_CH:(t + 1) * S5_CH, :].reshape(gu * S5_CH, nkb))
            return 0

        lax.fori_loop(0, ng // gu, groups, 0)
        sfin_ref[...] = carry[...]

    for j in range(npt):
        @pl.when(ph == npi + j)
        def _():
            finish(j)


def _s5_mix(q, nb, x, ln_g, wu, tt, wt, zt, a1, a2, s0, dcol, wglut, bglu, wo):
    nseq, l, d = x.shape
    nblk = l // q
    nlt = nblk * nseq
    nkb = min(LANES, nlt)
    ntile = nlt // nkb
    tok = q * nkb
    npt = max(1, tok // TOKEN_TILE)
    dm = wo.shape[1]
    perm = q > 1
    if perm:
        half = 8
        npi = q // half
        assert nb == nseq and d % LANES == 0 and q % half == 0
        xv = x.reshape(nseq, nblk, npi, half, d // LANES, LANES).transpose(0, 1, 2, 4, 3, 5)
        assert npt % npi == 0
        x_spec = pl.BlockSpec((nseq, nkb // nseq, None, d // LANES, half, LANES),
                              lambda i, j: (0, i, jnp.minimum(j, npi - 1), 0, 0, 0))
    else:
        npi = 1
        xv = x.reshape(1, nseq, d)
        x_spec = pl.BlockSpec((1, nkb, d), lambda i, j: (0, i, 0))
    body = functools.partial(_s5_mix_body, q, nb, nkb, npi, npt, perm)
    p_spec = pl.BlockSpec((q // npt, nkb, dm), lambda i, j: (jnp.maximum(j - npi, 0), i, 0))
    return pl.pallas_call(
        body,
        grid=(ntile, npi + npt),
        in_specs=[x_spec, _const_spec(ln_g.shape), _const_spec(wu.shape), _const_spec(tt.shape),
                  _const_spec(wt.shape), _const_spec(zt.shape), _const_spec(a1.shape), _const_spec(a2.shape),
                  _const_spec(s0.shape), _const_spec(dcol.shape), _const_spec(wglut.shape),
                  _const_spec(bglu.shape), _const_spec(wo.shape)],
        out_specs=[p_spec, pl.BlockSpec(s0.shape, lambda i, j: (0, 0, 0))],
        out_shape=[jax.ShapeDtypeStruct((q, nlt, dm), F32), jax.ShapeDtypeStruct(s0.shape, F32)],
        scratch_shapes=[pltpu.VMEM((d, tok), BF16), pltpu.VMEM((d, tok), F32), pltpu.VMEM(s0.shape, F32),
                        pltpu.VMEM((wu.shape[1], wu.shape[0]), BF16)],
        compiler_params=_params("arbitrary", "arbitrary"),
        name=f"s5_mix_q{q}",
    )(xv, ln_g, wu, tt, wt, zt, a1, a2, s0, dcol, wglut, bglu, wo)


def _kv_body(m_ref, wk_ref, wv_ref, k_ref, v_ref, kb_ref, vb_ref):
    tm, rows, _ = k_ref.shape
    nt = rows // X_HEADS
    m = m_ref[...].astype(BF16)
    for w_ref, o_ref, ob_ref in ((wk_ref, k_ref, kb_ref), (wv_ref, v_ref, vb_ref)):
        r = _dot(m, w_ref[...])
        ob_ref[...] = r.astype(BF16)
        flat = o_ref.reshape(tm * rows, LANES)
        for h in range(X_HEADS):
            for dt in range(nt):
                c = (h * nt + dt) * LANES
                flat[pl.ds(dt * X_HEADS + h, tm, stride=rows), :] = r[:, c:c + LANES]


def _kv_proj(mem2d, wk, wv):
    t, d = mem2d.shape
    tm = min(TOKEN_TILE, t)
    rows = d // LANES
    row = pl.BlockSpec((tm, d), lambda i: (i, 0))
    tile = pl.BlockSpec((tm, rows, LANES), lambda i: (i, 0, 0))
    return pl.pallas_call(
        _kv_body, grid=(t // tm,),
        in_specs=[row, _const_spec(wk.shape), _const_spec(wv.shape)],
        out_specs=[tile, tile, row, row],
        out_shape=[jax.ShapeDtypeStruct((t, rows, LANES), F32)] * 2 + [jax.ShapeDtypeStruct((t, d), BF16)] * 2,
        compiler_params=_params("arbitrary"),
        name="kv_proj",
    )(mem2d, wk, wv)


def _tile_unview_kv(kv8, b, m):
    nt = kv8.shape[1] // X_HEADS
    return kv8.reshape(b, m, nt, X_HEADS, LANES).transpose(0, 1, 3, 2, 4).reshape(b, m, X_HEADS, nt * LANES)


def _attn_prompt_body(x_ref, ys_ref, p_ref, wo1_ref, g2_ref, wq_ref, k_ref, v_ref, wxo_ref, o_ref):
    d = x_ref.shape[-1]
    hd = d // X_HEADS
    x1 = x_ref[...] + _dot(ys_ref[...], wo1_ref[...]) + p_ref[...]
    hq = _rms(x1, g2_ref[...]).astype(BF16)
    qv = _dot(hq, wq_ref[...]).astype(BF16)
    kb = k_ref[...]
    vb = v_ref[...]
    outs = []
    for h in range(X_HEADS):
        sl = slice(h * hd, (h + 1) * hd)
        s = _dot_nt(qv[:, sl], kb[:, sl]) * (hd ** -0.5)
        e = jnp.exp(s - jnp.max(s, axis=-1, keepdims=True))
        p = (e / jnp.sum(e, axis=-1, keepdims=True)).astype(BF16)
        outs.append(_dot(p, vb[:, sl]))
    o = jnp.concatenate(outs, axis=1).astype(BF16)
    o_ref[...] = x1 + _dot(o, wxo_ref[...])


def _attn_prompt(x, ys, pm, wo1, g2, wq, mk, mv, wxo, make_rider):
    b, l, d = x.shape
    nm = mk.shape[1]
    tm = min(TOKEN_TILE, l)
    nl = l // tm
    row = pl.BlockSpec((None, tm, d), lambda i, j: (i, j, 0))
    kv = pl.BlockSpec((None, nm, d), lambda i, j: (i, 0, 0))
    ins = [x, ys, pm, wo1, g2, wq, mk, mv, wxo]
    specs = [row, row, row, _const_spec(wo1.shape), _const_spec(g2.shape), _const_spec(wq.shape), kv, kv,
             _const_spec(wxo.shape)]
    rider = make_rider(lambda i, j: i * nl + j, b * nl)
    body = functools.partial(_two_bodies, _attn_prompt_body, len(ins), 1, rider[0], len(rider[1]))
    return pl.pallas_call(
        body, grid=(b, nl),
        in_specs=specs + list(rider[2]),
        out_specs=[row] + list(rider[3]),
        out_shape=[jax.ShapeDtypeStruct((b, l, d), F32)] + list(rider[4]),
        compiler_params=_params("arbitrary", "arbitrary"),
        name="attn_prompt",
    )(*ins, *rider[1])


def _merge_q_body(x_ref, ys_ref, p_ref, wo1_ref, g2_ref, wq_ref, x1_ref, q_ref):
    x1 = x_ref[...] + _dot(ys_ref[...], wo1_ref[...]) + p_ref[...]
    x1_ref[...] = x1
    q_ref[...] = _dot(_rms(x1, g2_ref[...]).astype(BF16), wq_ref[...])


def _merge_q(x2d, ys, pm, wo1, g2, wq):
    t, d = x2d.shape
    return pl.pallas_call(
        _merge_q_body,
        out_shape=[jax.ShapeDtypeStruct((t, d), F32)] * 2,
        compiler_params=pltpu.CompilerParams(vmem_limit_bytes=VMEM_LIMIT),
        name="merge_q",
    )(x2d, ys, pm, wo1, g2, wq)


def _tile_view(kv):
    b, m, nh, hd = kv.shape
    nt = hd // LANES
    return kv.reshape(b, m, nh, nt, LANES).transpose(0, 1, 3, 2, 4).reshape(b, m, nt * nh, LANES)


def _tile_unview(o):
    b, rows, _ = o.shape
    nt = rows // X_HEADS
    return o.reshape(b, nt, X_HEADS, LANES).transpose(0, 2, 1, 3).reshape(b, rows * LANES)


def _attn_step_body(q_ref, k_ref, v_ref, o_ref):
    bt, nm, rows, _ = k_ref.shape
    hd = rows * LANES // X_HEADS
    for b in range(bt):
        r = jnp.sum(k_ref[b] * q_ref[b], axis=-1, keepdims=True)
        s = (r + pltpu.roll(r, X_HEADS, 1)) * (hd ** -0.5)
        e = jnp.exp(s - jnp.max(s, axis=0, keepdims=True))
        p = e / jnp.sum(e, axis=0, keepdims=True)
        o_ref[b] = jnp.sum(p * v_ref[b], axis=0)


def _attn_step_rider(q8, k8, v8, nsteps):
    nbt, rows, _ = q8.shape
    nm = k8.shape[1]
    assert rows == 2 * X_HEADS, "score assembly assumes two 128-lane tiles per head"
    assert nbt % nsteps == 0
    bt = nbt // nsteps
    qs = pl.BlockSpec((bt, rows, LANES), lambda i: (i, 0, 0))
    kv = pl.BlockSpec((bt, nm, rows, LANES), lambda i: (i, 0, 0, 0))
    return (_attn_step_body, (q8, k8, v8), (qs, kv, kv), (qs,),
            (jax.ShapeDtypeStruct((nbt, rows, LANES), F32),))


def _ffn_body(has_o, final, *refs):
    if has_o:
        x_ref, o_in_ref, wxo_ref, g3_ref, wg_ref, wu_ref, wd_ref, gf_ref, y_ref = refs
        x2 = x_ref[...] + _dot(o_in_ref[...].astype(BF16), wxo_ref[...])
    else:
        x_ref, g3_ref, wg_ref, wu_ref, wd_ref, gf_ref, y_ref = refs
        x2 = x_ref[...]
    hf = _rms(x2, g3_ref[...]).astype(BF16)
    ff = wg_ref.shape[1]
    x3 = x2
    for lo in range(0, ff, FF_CHUNK):
        hi = min(lo + FF_CHUNK, ff)
        act = (_silu(_dot(hf, wg_ref[:, lo:hi])) * _dot(hf, wu_ref[:, lo:hi])).astype(BF16)
        x3 = x3 + _dot(act, wd_ref[lo:hi, :])
    y_ref[...] = _rms(x3, gf_ref[...]) if final else x3


def _ffn(x2d, g3, wg, wu, wd, gf, final, o_in=None, wxo=None, rider=None):
    t, d = x2d.shape
    tm = min(TOKEN_TILE, t)
    row = pl.BlockSpec((tm, d), lambda i: (i, 0))
    has_o = o_in is not None
    ins = [x2d] + ([o_in, wxo] if has_o else []) + [g3, wg, wu, wd, gf]
    specs = [row] + ([row, _const_spec(wxo.shape)] if has_o else []) + [
        _const_spec(g3.shape), _const_spec(wg.shape), _const_spec(wu.shape), _const_spec(wd.shape),
        _const_spec(gf.shape)]
    body = functools.partial(_ffn_body, has_o, final)
    out_specs, out_shape = [row], [jax.ShapeDtypeStruct((t, d), F32)]
    if rider is not None:
        body = functools.partial(_two_bodies, body, len(ins), 1, rider[0], len(rider[1]))
        ins, specs = ins + list(rider[1]), specs + list(rider[2])
        out_specs, out_shape = out_specs + list(rider[3]), out_shape + list(rider[4])
    res = pl.pallas_call(
        body, grid=(t // tm,),
        in_specs=specs, out_specs=out_specs, out_shape=out_shape,
        compiler_params=_params("arbitrary"),
        name="ffn_o" if has_o else "ffn",
    )(*ins)
    return res[0] if rider is None else res


def _two_bodies(body_a, n_in_a, n_out_a, body_b, n_in_b, *refs):
    outs = refs[n_in_a + n_in_b:]
    body_a(*refs[:n_in_a], *outs[:n_out_a])
    body_b(*refs[n_in_a:n_in_a + n_in_b], *outs[n_out_a:])


def _layer_weights(i, ln1_g, w_in, conv_w, conv_b, dt_bias, a_log, d_ssd, gn_g, lam_re, lam_im, log_step,
                   b_re, b_im, c_re, c_im, d_s5, w_glu, b_glu, w_out, ln2_g, w_xq, w_xk, w_xv, w_xo, ln3_g,
                   w_gate, w_up, w_down):
    d = w_in.shape[1]
    n_heads = dt_bias.shape[1]
    dssd = n_heads * SSD_HEAD_DIM
    conv_dim = conv_w.shape[2]
    o1, o2, o3 = dssd, dssd + conv_dim, dssd + conv_dim + n_heads
    win = w_in[i]
    w = {}
    w["ln1"] = ln1_g[i].reshape(1, d)
    w["win"] = win.astype(BF16)
    w["dims"] = (dssd, conv_dim)
    w["wu_s5"] = w["win"][:, o3:]
    w["cw"] = conv_w[i]
    w["cb"] = conv_b[i].reshape(1, conv_dim)
    w["dtb"] = jnp.pad(dt_bias[i], (0, LANES - n_heads)).reshape(1, LANES)
    w["alog"] = jnp.pad(a_log[i], (0, LANES - n_heads)).reshape(1, LANES)
    w["dexp"] = jnp.repeat(d_ssd[i], SSD_HEAD_DIM).reshape(1, dssd)
    w["gn"] = gn_g[i].reshape(1, dssd)
    w["s5"] = (lam_re[i], lam_im[i], log_step[i], b_re[i], b_im[i], c_re[i], c_im[i])
    ds5 = d_s5.shape[1]
    w["dcol"] = d_s5[i].reshape(ds5, 1)
    w["wglut"] = w_glu[i].T.astype(BF16)
    w["bglu"] = b_glu[i].reshape(ds5, 1)
    w["wo1"] = w_out[i, :dssd].astype(BF16)
    w["wo2"] = w_out[i, dssd:].astype(BF16)
    w["ln2"] = ln2_g[i].reshape(1, d)
    w["wq"] = w_xq[i].astype(BF16)
    w["wk"] = w_xk[i].astype(BF16)
    w["wv"] = w_xv[i].astype(BF16)
    w["wxo"] = w_xo[i].astype(BF16)
    w["ln3"] = ln3_g[i].reshape(1, d)
    w["wg"] = w_gate[i].astype(BF16)
    w["wu"] = w_up[i].astype(BF16)
    w["wdn"] = w_down[i].astype(BF16)
    return w


def _state_to_rows(s_re, s_im):
    return jnp.concatenate([s_re, s_im], axis=-1).transpose(1, 0, 2)


def _rows_to_state(s):
    p = s.shape[-1] // 2
    st = s.transpose(1, 0, 2)
    return st[..., :p], st[..., p:]


def kernel(x_prompt, x_sample, mem_prompt, state_conv, state_ssm, state_s5_re, state_s5_im, cache_mem_k, cache_mem_v, ln1_g, w_in, conv_w, conv_b, dt_bias, a_log, d_ssd, gn_g, lam_re, lam_im, log_step, b_re, b_im, c_re, c_im, d_s5, w_glu, b_glu, w_out, ln2_g, w_xq, w_xk, w_xv, w_xo, ln3_g, w_gate, w_up, w_down, final_g):
    bp, seq, d = x_prompt.shape
    bs = x_sample.shape[0]
    depth = w_in.shape[0]
    nm = mem_prompt.shape[1]
    ng, ns = lam_re.shape[1], lam_re.shape[2]
    gf = final_g.reshape(1, d)
    hp = x_prompt
    hs = x_sample.reshape(bs, d)
    outs = {k: [] for k in ("conv_p", "ssm_p", "re_p", "im_p", "mk_p", "mv_p", "conv_s", "ssm_s", "re_s", "im_s")}
    yp = ys_out = None
    for i in range(depth):
        w = _layer_weights(i, ln1_g, w_in, conv_w, conv_b, dt_bias, a_log, d_ssd, gn_g, lam_re, lam_im, log_step,
                           b_re, b_im, c_re, c_im, d_s5, w_glu, b_glu, w_out, ln2_g, w_xq, w_xk, w_xv, w_xo, ln3_g,
                           w_gate, w_up, w_down)
        last = i == depth - 1
        mk8, mv8, mk, mv = _kv_proj(mem_prompt.reshape(bp * nm, d), w["wk"], w["wv"])
        mk = mk.reshape(bp, nm, d)
        mv = mv.reshape(bp, nm, d)
        y_ssd, conv_p, ssm_p = _ssd_prompt(hp, w["ln1"], w["win"], w["cw"], w["cb"], w["dtb"], w["alog"], w["dexp"],
                                           w["gn"])
        q = S5_Q
        tt, wt, zt, a1, a2, a11, a21 = _s5_prep(q, *w["s5"])
        s0 = jnp.zeros((ng, bp, 2 * ns), F32)
        pperm, sfin = _s5_mix(q, bp, hp, w["ln1"], w["wu_s5"],tt, wt, zt, a1, a2, s0, w["dcol"], w["wglut"],
                              w["bglu"], w["wo2"])
        pm = pperm.reshape(q, seq // q, bp, d).transpose(2, 1, 0, 3).reshape(bp, seq, d)
        re_p, im_p = _rows_to_state(sfin)
        attn_steps = bp * (seq // min(TOKEN_TILE, seq))
        zs, xbcs, dts = _inproj(hs, w["ln1"], w["win"], *w["dims"], bs // attn_steps)
        ssd_rider = functools.partial(_ssd_step_rider, zs, xbcs, dts, state_conv[i], state_ssm[i],
                                      w["cw"], w["cb"], w["dtb"], w["alog"], w["dexp"], w["gn"])
        x2, ys_ssd, conv_s, ssm_s = _attn_prompt(hp, y_ssd, pm, w["wo1"], w["ln2"], w["wq"], mk, mv, w["wxo"],
                                                 ssd_rider)
        ys_ssd = ys_ssd.reshape(bs, -1)
        tt1, wt1, zt1 = _s5_single_token_mats(q, tt, wt, zt)
        s0s = _state_to_rows(state_s5_re[i], state_s5_im[i])
        ps, sfin_s = _s5_mix(1, bs, hs.reshape(bs, 1, d), w["ln1"], w["wu_s5"],tt1, wt1, zt1, a11, a21, s0s,
                             w["dcol"], w["wglut"], w["bglu"], w["wo2"])
        re_s, im_s = _rows_to_state(sfin_s)
        x1s, qs = _merge_q(hs, ys_ssd, ps.reshape(bs, d), w["wo1"], w["ln2"], w["wq"])
        attn_rider = _attn_step_rider(_tile_view(qs.reshape(bs, 1, X_HEADS, d // X_HEADS))[:, 0],
                                      _tile_view(cache_mem_k[i]), _tile_view(cache_mem_v[i]),
                                      (bp * seq) // min(TOKEN_TILE, bp * seq))
        hp, o_s = _ffn(x2.reshape(bp * seq, d), w["ln3"], w["wg"], w["wu"], w["wdn"], gf, last, rider=attn_rider)
        hp = hp.reshape(bp, seq, d)
        hs = _ffn(x1s, w["ln3"], w["wg"], w["wu"], w["wdn"], gf, last, o_in=_tile_unview(o_s), wxo=w["wxo"])
        for k, v in (("conv_p", conv_p), ("ssm_p", ssm_p), ("re_p", re_p), ("im_p", im_p),
                     ("mk_p", _tile_unview_kv(mk8, bp, nm)), ("mv_p", _tile_unview_kv(mv8, bp, nm)),
                     ("conv_s", conv_s), ("ssm_s", ssm_s), ("re_s", re_s), ("im_s", im_s)):
            outs[k].append(v)
    st = lambda k: jnp.stack(outs[k])
    return (hp, hs.reshape(bs, 1, d), st("conv_p"), st("ssm_p"), st("re_p"), st("im_p"), st("mk_p"), st("mv_p"),
            st("conv_s"), st("ssm_s"), st("re_s"), st("im_s"))
```

```python
import functools
import math

import jax
import jax.numpy as jnp
from jax import lax
from jax.experimental import pallas as pl
from jax.experimental.pallas import tpu as pltpu

F32 = jnp.float32
BF16 = jnp.bfloat16
EPS = 1e-6

LANES = 128
VMEM_LIMIT = 56 * 1024 * 1024

SSD_HEAD_DIM = 64
SSD_STATE = 128
SSD_GROUPS = 2
SSD_CHUNK = 128
CONV_W = 4
S5_CH = 16
S5_STATE = 64
S5_Q = 16
S5_GROUP_BATCH = 16
X_HEADS = 4
TOKEN_TILE = 512
PROJ_PIECE = 512
SSD_SEQS_PER_STEP = 2
FF_CHUNK = 1536


def _const_spec(shape):
    nd = len(shape)
    return pl.BlockSpec(shape, lambda *_: (0,) * nd, pipeline_mode=pl.Buffered(1))


def _params(*sem):
    return pltpu.CompilerParams(dimension_semantics=sem, vmem_limit_bytes=VMEM_LIMIT)


def _rms(x, g):
    return x * lax.rsqrt(jnp.mean(x * x, axis=-1, keepdims=True) + EPS) * g


def _sigmoid(x):
    return 1.0 / (1.0 + jnp.exp(-x))


def _silu(x):
    return x * _sigmoid(x)


def _gelu_tanh(x):
    return 0.5 * x * (1.0 + jnp.tanh(math.sqrt(2.0 / math.pi) * (x + 0.044715 * (x * x * x))))


def _softplus(x):
    return jnp.maximum(x, 0.0) + jnp.log1p(jnp.exp(-jnp.abs(x)))


def _dot(a, b):
    return jnp.dot(a, b, preferred_element_type=F32)


def _dot_nt(a, b):
    return lax.dot_general(a, b, (((1,), (1,)), ((), ())), preferred_element_type=F32)


def _dot_tn(a, b):
    return lax.dot_general(a, b, (((0,), (0,)), ((), ())), preferred_element_type=F32)


def _proj_windows(d_ssd, conv_dim):
    assert d_ssd % LANES == 0 and conv_dim % LANES == 0
    return (0, d_ssd), (d_ssd, d_ssd + conv_dim), (d_ssd + conv_dim, d_ssd + conv_dim + LANES)


def _inproj_body(wins, x_ref, g_ref, w_ref, z_ref, xbc_ref, dt_ref):
    h = _rms(x_ref[...], g_ref[...]).astype(BF16)
    for (lo, hi), o_ref in zip(wins, (z_ref, xbc_ref, dt_ref)):
        o_ref[...] = _dot(h, w_ref[:, lo:hi]).reshape(o_ref.shape)


def _inproj(x2d, ln_g, w_in, d_ssd, conv_dim, group):
    t, d = x2d.shape
    wins = _proj_windows(d_ssd, conv_dim)
    return pl.pallas_call(
        functools.partial(_inproj_body, wins),
        out_shape=[jax.ShapeDtypeStruct((t // group, group, hi - lo), F32) for lo, hi in wins],
        compiler_params=pltpu.CompilerParams(vmem_limit_bytes=VMEM_LIMIT),
        name="inproj",
    )(x2d, ln_g, w_in)


def _expand_heads(v, n_heads):
    rows = v.shape[0]
    lane = lax.broadcasted_iota(jnp.int32, (rows, LANES), 1)
    pieces = [jnp.where(lane < SSD_HEAD_DIM, v[:, 2 * j:2 * j + 1], v[:, 2 * j + 1:2 * j + 2])
              for j in range(n_heads // 2)]
    return jnp.concatenate(pieces, axis=1)


def _ssd_chunk_math(z, x, dt_raw, last_chunk, cw_ref, cb_ref, dtb_ref, alog_ref, dexp_ref, gn_ref, e2_ref,
                    y_ref, conv_ref, ssm_ref, xpad, state, side_jobs=()):
    jobs = list(side_jobs)

    def side(n=1):
        for _ in range(n):
            if jobs:
                jobs.pop(0)()

    q = SSD_CHUNK
    d_ssd = z.shape[-1]
    n_heads = d_ssd // SSD_HEAD_DIM
    hpg = n_heads // SSD_GROUPS
    gw = hpg * SSD_HEAD_DIM
    nct = xpad.shape[0]
    xt = d_ssd // LANES

    for t in range(nct):
        xpad[t, 8:8 + q, :] = x[:, t * LANES:(t + 1) * LANES]
    cw = cw_ref[...]
    conv = (cw[3] * xpad[:, 8:8 + q, :] + cw[2] * xpad[:, 7:7 + q, :] + cw[1] * xpad[:, 6:6 + q, :]
            + cw[0] * xpad[:, 5:5 + q, :] + cb_ref[...])
    xpad[:, 0:8, :] = xpad[:, q:q + 8, :]
    xact = _silu(conv)
    xs = jnp.concatenate([xact[t] for t in range(xt)], axis=1)
    bm = xact[xt:xt + SSD_GROUPS].astype(BF16)
    cm = xact[xt + SSD_GROUPS:].astype(BF16)
    side()

    dt = _softplus(dt_raw + dtb_ref[...])
    a = -jnp.exp(alog_ref[...])
    row = lax.broadcasted_iota(jnp.int32, (q, q), 0)
    col = lax.broadcasted_iota(jnp.int32, (q, q), 1)
    causal = row >= col
    tri = jnp.where(causal, 1.0, 0.0).astype(F32)
    acum = jnp.dot(tri, dt * a, precision=lax.Precision.HIGHEST, preferred_element_type=F32)
    acum_t = acum.T
    last = acum[q - 1:q, :]

    def expand(v):
        hi = v.astype(BF16)
        lo = (v - hi.astype(F32)).astype(BF16)
        return _dot(jnp.concatenate([hi, lo], axis=1), e2_ref[...])

    dt_e = expand(dt)
    ea_e = expand(jnp.exp(acum))
    dend_e = expand(jnp.exp(last - acum))

    dtx = xs * dt_e
    dtx_b = dtx.astype(BF16)
    xdec_b = (dtx * dend_e).astype(BF16)
    side()

    lane = lax.broadcasted_iota(jnp.int32, (q, LANES), 1)
    zero_b = jnp.zeros((q, LANES), BF16)
    y_tiles = []
    for g in range(SSD_GROUPS):
        bg = bm[g]
        cg = cm[g]
        cb = _dot_nt(cg, bg)
        gs = slice(g * gw, (g + 1) * gw)
        st_g = state[:, gs]
        y_off = _dot(cg, st_g.astype(BF16)) * ea_e[:, gs]
        for pr in range(hpg // 2):
            tile = g * (hpg // 2) + pr
            ms = []
            for h in (2 * tile, 2 * tile + 1):
                seg = acum[:, h:h + 1] - acum_t[h:h + 1, :]
                lmat = jnp.exp(jnp.where(causal, seg, -jnp.inf))
                ms.append((cb * lmat).astype(BF16))
            dtile = dtx_b[:, tile * LANES:(tile + 1) * LANES]
            rhs = jnp.concatenate([jnp.where(lane < SSD_HEAD_DIM, dtile, zero_b),
                                   jnp.where(lane >= SSD_HEAD_DIM, dtile, zero_b)], axis=0)
            y_tiles.append(_dot(jnp.concatenate(ms, axis=1), rhs)
                           + y_off[:, pr * LANES:(pr + 1) * LANES])
            if pr % 2 == 1:
                side()
        new = _dot_tn(bg, xdec_b[:, gs])
        state[:, gs] = st_g * ea_e[q - 1:q, gs] + new
    y = jnp.concatenate(y_tiles, axis=1) + dexp_ref[...] * xs
    y = y * _silu(z)
    ms = jnp.mean(y * y, axis=-1, keepdims=True)
    side(len(jobs))
    y_ref[...] = (y * lax.rsqrt(ms + EPS) * gn_ref[...]).astype(y_ref.dtype)

    @pl.when(last_chunk)
    def _():
        conv_ref[...] = x[q - (CONV_W - 1):q, :]
        ssm_ref[...] = state[...].T.reshape(ssm_ref.shape)


def _ssd_prompt_body(wins, x_ref, g_ref, w_ref, cw_ref, cb_ref, dtb_ref, alog_ref, dexp_ref, gn_ref,
                     e2_ref, y_ref, conv_ref, ssm_ref, za, xa, da, zb, xb, db, xpad, state):
    i = pl.program_id(0)
    j = pl.program_id(1)
    nc = pl.num_programs(1) - 1
    ns, q, d = x_ref.shape

    @pl.when(jnp.logical_and(i == 0, j == 0))
    def _():
        for r in (zb, xb, db):
            r[...] = jnp.zeros(r.shape, F32)

    @pl.when(j <= 1)
    def _():
        xpad[:, :, 0:8, :] = jnp.zeros(xpad.shape[:2] + (8, LANES), F32)
        state[...] = jnp.zeros(state.shape, F32)

    def step(wr, rd):
        xin = x_ref[...].reshape(ns * q, d)
        msx = jnp.mean(xin * xin, axis=-1, keepdims=True)
        hbox = []

        def proj_piece(dst, base, lo, hi):
            def run():
                if not hbox:
                    hbox.append((xin * lax.rsqrt(msx + EPS) * g_ref[...]).astype(BF16))
                dst[:, lo:hi] = _dot(hbox[0], w_ref[:, base + lo:base + hi])
            return run

        pieces = []
        for dst, (w_lo, w_hi) in ((wr[0], wins[0]), (wr[2], wins[2]), (wr[1], wins[1])):
            n = w_hi - w_lo
            pieces += [proj_piece(dst, w_lo, lo, min(lo + PROJ_PIECE, n)) for lo in range(0, n, PROJ_PIECE)]
        per = -(-len(pieces) // ns)
        for s in range(ns):
            rows = slice(s * q, (s + 1) * q)
            _ssd_chunk_math(rd[0][rows, :], rd[1][rows, :], rd[2][rows, :], j == nc, cw_ref, cb_ref, dtb_ref,
                            alog_ref, dexp_ref, gn_ref, e2_ref, y_ref.at[s], conv_ref.at[s], ssm_ref.at[s],
                            xpad.at[s], state.at[s], pieces[s * per:(s + 1) * per])

    @pl.when(j % 2 == 0)
    def _():
        step((za, xa, da), (zb, xb, db))

    @pl.when(j % 2 == 1)
    def _():
        step((zb, xb, db), (za, xa, da))


def _ssd_prompt(x, ln_g, w_in, cw, cb, dtb, alog, dexp, gn):
    b, l, d = x.shape
    conv_dim = cw.shape[1]
    d_ssd = conv_dim - 2 * SSD_GROUPS * SSD_STATE
    wins = _proj_windows(d_ssd, conv_dim)
    n_heads = d_ssd // SSD_HEAD_DIM
    q = SSD_CHUNK
    nc = l // q
    assert SSD_STATE == LANES and conv_dim == d_ssd + 2 * SSD_GROUPS * SSD_STATE
    nct = conv_dim // LANES
    cw4 = cw.reshape(CONV_W, nct, 1, LANES)
    cb3 = cb.reshape(nct, 1, LANES)
    e2 = (jnp.arange(2 * LANES)[:, None] % LANES == jnp.arange(d_ssd)[None, :] // SSD_HEAD_DIM).astype(BF16)
    ns = SSD_SEQS_PER_STEP if b % SSD_SEQS_PER_STEP == 0 else 1
    slot = lambda n: pltpu.VMEM((ns * q, n), F32)
    return pl.pallas_call(
        functools.partial(_ssd_prompt_body, wins),
        grid=(b // ns, nc + 1),
        in_specs=[pl.BlockSpec((ns, q, d), lambda i, j: (i, jnp.minimum(j, nc - 1), 0)),
                  _const_spec(ln_g.shape), _const_spec(w_in.shape),
                  _const_spec(cw4.shape), _const_spec(cb3.shape), _const_spec(dtb.shape), _const_spec(alog.shape),
                  _const_spec(dexp.shape), _const_spec(gn.shape), _const_spec(e2.shape)],
        out_specs=[pl.BlockSpec((ns, q, d_ssd), lambda i, j: (i, jnp.maximum(j - 1, 0), 0)),
                   pl.BlockSpec((ns, CONV_W - 1, conv_dim), lambda i, j: (i, 0, 0)),
                   pl.BlockSpec((ns, n_heads, SSD_HEAD_DIM, SSD_STATE), lambda i, j: (i, 0, 0, 0))],
        out_shape=[jax.ShapeDtypeStruct((b, l, d_ssd), BF16),
                   jax.ShapeDtypeStruct((b, CONV_W - 1, conv_dim), F32),
                   jax.ShapeDtypeStruct((b, n_heads, SSD_HEAD_DIM, SSD_STATE), F32)],
        scratch_shapes=[slot(d_ssd), slot(conv_dim), slot(LANES), slot(d_ssd), slot(conv_dim), slot(LANES),
                        pltpu.VMEM((ns, nct, q + 8, LANES), F32), pltpu.VMEM((ns, SSD_STATE, d_ssd), F32)],
        compiler_params=_params("arbitrary", "arbitrary"),
        name="ssd_prompt",
    )(x, ln_g, w_in, cw4, cb3, dtb, alog, dexp, gn, e2)


def _ssd_step_body(z_ref, xbc_ref, dt_ref, cs_ref, st_ref, cw_ref, cb_ref, dtb_ref, alog_ref, dexp_ref, gn_ref,
                   y_ref, conv_ref, ssm_ref):
    bt, d_ssd = z_ref.shape
    n_heads = d_ssd // SSD_HEAD_DIM
    hpg = n_heads // SSD_GROUPS
    x = xbc_ref[...]
    cw = cw_ref[...]
    cs = [cs_ref[:, k, :] for k in range(CONV_W - 1)]
    conv = cw[0:1] * cs[0] + cw[1:2] * cs[1] + cw[2:3] * cs[2] + cw[3:4] * x + cb_ref[...]
    conv_ref[:, 0, :] = cs[1]
    conv_ref[:, 1, :] = cs[2]
    conv_ref[:, 2, :] = x
    xact = _silu(conv)
    xs = xact[:, :d_ssd]
    bm = xact[:, d_ssd:d_ssd + SSD_GROUPS * SSD_STATE]
    cm = xact[:, d_ssd + SSD_GROUPS * SSD_STATE:]
    dt = _softplus(dt_ref[...] + dtb_ref[...])
    da = jnp.exp(dt * (-jnp.exp(alog_ref[...])))
    dtx = xs * _expand_heads(dt, n_heads)
    pad = jnp.zeros((LANES - bt, d_ssd), F32)
    to_cols = lambda v: jnp.concatenate([v, pad], axis=0).T
    dtx_t = to_cols(dtx)
    da_t = to_cols(_expand_heads(da, n_heads))
    lane = lax.broadcasted_iota(jnp.int32, (d_ssd, LANES), 1)
    gw = hpg * SSD_HEAD_DIM
    rows_of = lambda v, b: jnp.concatenate(
        [jnp.broadcast_to(v[b:b + 1, g * SSD_STATE:(g + 1) * SSD_STATE], (gw, SSD_STATE)) for g in range(SSD_GROUPS)],
        axis=0)
    y_t = jnp.zeros((d_ssd, LANES), F32)
    for b in range(bt):
        s0 = st_ref[b].reshape(d_ssd, SSD_STATE)
        s_new = s0 * da_t[:, b:b + 1] + dtx_t[:, b:b + 1] * rows_of(bm, b)
        ssm_ref[b] = s_new.reshape(n_heads, SSD_HEAD_DIM, SSD_STATE)
        ycol = jnp.sum(s_new * rows_of(cm, b), axis=1, keepdims=True)
        y_t = jnp.where(lane == b, ycol, y_t)
    y = y_t.T[:bt, :] + dexp_ref[...] * xs
    y = y * _silu(z_ref[...])
    y_ref[...] = _rms(y, gn_ref[...]).astype(y_ref.dtype)


def _ssd_step_rider(z, xbc, dt, conv_state, ssm_state, cw, cb, dtb, alog, dexp, gn, step_of, nsteps):
    _, bt, d_ssd = z.shape
    conv_dim = xbc.shape[-1]
    n_heads = d_ssd // SSD_HEAD_DIM
    assert z.shape[0] == nsteps and conv_state.shape[0] == nsteps * bt
    row = lambda n: pl.BlockSpec((None, bt, n), lambda *g: (step_of(*g), 0, 0))
    cs_spec = pl.BlockSpec((bt, CONV_W - 1, conv_dim), lambda *g: (step_of(*g), 0, 0))
    st_spec = pl.BlockSpec((bt, n_heads, SSD_HEAD_DIM, SSD_STATE), lambda *g: (step_of(*g), 0, 0, 0))
    cst = lambda a: pl.BlockSpec(a.shape, lambda *g: (0,) * a.ndim, pipeline_mode=pl.Buffered(1))
    ins = (z, xbc, dt, conv_state, ssm_state, cw, cb, dtb, alog, dexp, gn)
    specs = (row(d_ssd), row(conv_dim), row(LANES), cs_spec, st_spec, cst(cw), cst(cb), cst(dtb), cst(alog),
             cst(dexp), cst(gn))
    out_shapes = (jax.ShapeDtypeStruct((nsteps, bt, d_ssd), BF16),
                  jax.ShapeDtypeStruct(conv_state.shape, F32),
                  jax.ShapeDtypeStruct(ssm_state.shape, F32))
    return (_ssd_step_body, ins, specs, (row(d_ssd), cs_spec, st_spec), out_shapes)


def _split_bf16(x):
    hi = x.astype(BF16)
    lo = (x - hi.astype(F32)).astype(BF16)
    return hi, lo


def _s5_prep_body(q, ls_ref, lrp_ref, lip_ref, lr2_ref, li2_ref, br_ref, bi_ref, ccat_ref, ca_ref, cb_ref, rep_ref,
                  tt_ref, wt_ref, zt_ref, a1_ref, a2_ref, a1s_ref, a2s_ref):
    ng = ls_ref.shape[0]
    w = q * S5_CH
    step = jnp.exp(ls_ref[...])
    lr, li = lrp_ref[...], lip_ref[...]
    mag = jnp.exp(lr * step)
    ang = li * step
    lbr = mag * jnp.cos(ang)
    lbi = mag * jnp.sin(ang)
    den = lr * lr + li * li
    kr = ((lbr - 1.0) * lr + lbi * li) / den
    ki = (lbi * lr - (lbr - 1.0) * li) / den
    br, bi = br_ref[...], bi_ref[...]
    bbr = kr * br - ki * bi
    bbi = kr * bi + ki * br
    if q > 1:
        np_ = br.shape[1]
        rep = lambda v: jnp.dot(v.reshape(ng * np_, S5_CH), rep_ref[...], precision=lax.Precision.HIGHEST,
                                preferred_element_type=F32).reshape(ng, np_, w)
        bbr, bbi = rep(bbr), rep(bbi)
    if q == 1:
        ball = jnp.concatenate([bbr, bbi], axis=1)
    else:
        d = lax.broadcasted_iota(jnp.int32, (1, 1, w), 2) // S5_CH
        fr, fi = lbr, lbi
        pr = pi = None
        for b in range((q - 1).bit_length()):
            bit = ((d >> b) & 1) == 1
            sr, si = jnp.where(bit, fr, 1.0), jnp.where(bit, fi, 0.0)
            pr, pi = (sr, si) if pr is None else (pr * sr - pi * si, pr * si + pi * sr)
            fr, fi = fr * fr - fi * fi, 2.0 * (fr * fi)
        ball = jnp.concatenate([pr * bbr - pi * bbi, pr * bbi + pi * bbr], axis=1)
    wt_ref[...] = ball.astype(BF16)

    lane3 = lax.broadcasted_iota(jnp.int32, (1, 1, 2 * S5_STATE), 2)
    first = lane3 < S5_STATE
    csign = jnp.where(first, ccat_ref[...], -ccat_ref[...])
    ch, cl = _split_bf16(csign)
    bh, bl = _split_bf16(ball)
    bdot = lambda x, y: lax.dot_general(x, y, (((2,), (1,)), ((0,), (0,))), preferred_element_type=F32)
    kall = bdot(ch, bh) + bdot(ch, bl) + bdot(cl, bh)
    k2 = kall.reshape(ng * S5_CH, w)
    lane2 = lax.broadcasted_iota(jnp.int32, (ng * S5_CH, w), 1)
    for t in range(q):
        sh = (q - 1 - t) * S5_CH
        r = pltpu.roll(k2, sh, 1) if sh else k2
        r = jnp.where(lane2 >= sh, r, 0.0)
        tt_ref[:, t * S5_CH:(t + 1) * S5_CH, :] = r.reshape(ng, S5_CH, w).astype(BF16)

    lr2, li2 = lr2_ref[...], li2_ref[...]
    t1 = (lax.broadcasted_iota(jnp.int32, (1, q, 1), 1) + 1).astype(F32)
    zm = jnp.exp(t1 * (lr2 * step))
    za = t1 * (li2 * step)
    zr = zm * jnp.cos(za)
    zi = zm * jnp.sin(za)
    ca, cb = ca_ref[...], cb_ref[...]
    for t in range(q):
        prt = zr[:, t:t + 1, :]
        pit = zi[:, t:t + 1, :]
        zt = jnp.where(first, ca * prt - cb * pit, -(ca * pit) - cb * prt)
        zt_ref[:, t * S5_CH:(t + 1) * S5_CH, :] = zt.astype(BF16)
    qf = float(q)
    mq = jnp.exp(qf * (lr2 * step))
    aq = qf * (li2 * step)
    ar = mq * jnp.cos(aq)
    ai = mq * jnp.sin(aq)
    a1_ref[...] = ar
    a2_ref[...] = jnp.where(first, -ai, ai)
    a1s_ref[...] = zr[:, 0:1, :]
    a2s_ref[...] = jnp.where(first, -zi[:, 0:1, :], zi[:, 0:1, :])


def _s5_prep(q, lam_re, lam_im, log_step, b_re, b_im, c_re, c_im):
    ng, p = lam_re.shape
    w = q * S5_CH
    ls = log_step.reshape(ng, 1, 1)
    lrp = lam_re.reshape(ng, p, 1)
    lip = lam_im.reshape(ng, p, 1)
    lr2 = jnp.concatenate([lam_re, lam_re], axis=-1).reshape(ng, 1, 2 * p)
    li2 = jnp.concatenate([lam_im, lam_im], axis=-1).reshape(ng, 1, 2 * p)
    rep = (jnp.arange(S5_CH)[:, None] == jnp.arange(w)[None, :] % S5_CH).astype(F32)
    ccat = jnp.concatenate([c_re, c_im], axis=-1)
    ca = jnp.concatenate([c_re, c_re], axis=-1)
    cb = jnp.concatenate([c_im, c_im], axis=-1)
    gb = ng if q == 1 else 16
    blk = lambda a, b: pl.BlockSpec((gb, a, b), lambda i: (i, 0, 0))
    return pl.pallas_call(
        functools.partial(_s5_prep_body, q),
        grid=(ng // gb,),
        in_specs=[blk(1, 1), blk(p, 1), blk(p, 1), blk(1, 2 * p), blk(1, 2 * p), blk(p, S5_CH), blk(p, S5_CH),
                  blk(S5_CH, 2 * p), blk(S5_CH, 2 * p), blk(S5_CH, 2 * p), _const_spec(rep.shape)],
        out_specs=[blk(w, w), blk(2 * p, w), blk(w, 2 * p)] + [blk(1, 2 * p)] * 4,
        out_shape=[jax.ShapeDtypeStruct((ng, w, w), BF16), jax.ShapeDtypeStruct((ng, 2 * p, w), BF16),
                   jax.ShapeDtypeStruct((ng, w, 2 * p), BF16)] + [jax.ShapeDtypeStruct((ng, 1, 2 * p), F32)] * 4,
        compiler_params=_params("arbitrary"),
        name=f"s5_prep_q{q}",
    )(ls, lrp, lip, lr2, li2, b_re, b_im, ccat, ca, cb, rep)


def _s5_single_token_mats(q, tt, wt, zt):
    c = S5_CH
    return tt[:, :c, (q - 1) * c:], wt[:, :, :c], zt[:, :c, :]


def _s5_mix_body(q, nb, nkb, npi, npt, perm, x_ref, g_ref, wu_ref, tt_ref, wt_ref, zt_ref, a1_ref, a2_ref, s0_ref,
                 dcol_ref, wglut_ref, bglu_ref, wo_ref, p_ref, sfin_ref, ut, yt, carry, wut):
    ng = tt_ref.shape[0]
    nk = nkb // nb
    qs = q // npt
    ch = qs * nkb
    tile = pl.program_id(0)
    ph = pl.program_id(1)

    @pl.when(jnp.logical_and(tile == 0, ph == 0))
    def _():
        if perm:
            carry[...] = s0_ref[...]
        else:
            lane_s = lax.broadcasted_iota(jnp.int32, (nb, 2 * S5_STATE), 1)
            for pr in range(ng // 2):
                cols = slice(pr * 2 * S5_STATE, (pr + 1) * 2 * S5_STATE)
                re_t, im_t = s0_ref[0, :, cols], s0_ref[1, :, cols]
                carry[2 * pr] = jnp.where(lane_s < S5_STATE, re_t, pltpu.roll(im_t, S5_STATE, 1))
                carry[2 * pr + 1] = jnp.where(lane_s < S5_STATE, pltpu.roll(re_t, S5_STATE, 1), im_t)
        cw_ = 256
        for c in range(0, wu_ref.shape[1], cw_):
            wut[c:c + cw_, :] = wu_ref[:, c:c + cw_].T

    def rows_of(r):
        if not perm:
            return x_ref[r]
        nseq, nblk, ndt, nr, _ = x_ref.shape
        flat = x_ref.reshape(nseq * nblk * ndt * nr, LANES)

        return jnp.concatenate(
            [jnp.concatenate([flat[pl.ds((k * ndt + dt) * nr + r, nseq, stride=nblk * ndt * nr), :]
                              for dt in range(ndt)], axis=1) for k in range(nblk)], axis=0)

    def project(j):
        step = 2 if qs % 2 == 0 else 1
        r0 = (j * qs) % (q // npi)
        for r in range(0, qs, step):
            xin = jnp.concatenate([rows_of(r0 + r + s) for s in range(step)], axis=0)
            u = _dot_nt(wut[...], _rms(xin, g_ref[...]).astype(BF16))
            sl = slice((j * qs + r) * nkb, (j * qs + r + step) * nkb)
            ut[:, sl] = u.astype(BF16)
            yt[:, sl] = dcol_ref[...] * u

    def finish(j):
        gt = _gelu_tanh(yt[:, j * ch:(j + 1) * ch])
        gate = _dot(wglut_ref[...], gt.astype(BF16)) + bglu_ref[...]
        y5 = (gt * _sigmoid(gate)).astype(BF16)
        p_ref[...] = _dot_tn(y5, wo_ref[...]).reshape(p_ref.shape)

    spp = npt // npi
    for hh in range(npi):
        @pl.when(ph == hh)
        def _():
            for j in range(hh * spp, (hh + 1) * spp):
                project(j)

    @pl.when(ph == npi - 1)
    def _():
        gu = S5_GROUP_BATCH
        bdot = lambda a, b: lax.dot_general(a, b, (((2,), (1,)), ((0,), (0,))), preferred_element_type=F32)
        bdot_nt = lambda a, b: lax.dot_general(a, b, (((2,), (2,)), ((0,), (0,))), preferred_element_type=F32)

        def groups(i, _):
            g0 = pl.multiple_of(i * gu, gu)
            r0 = pl.multiple_of(i * (gu * S5_CH), gu * S5_CH)
            gsl = pl.ds(g0, gu)
            rows = ut[pl.ds(r0, gu * S5_CH), :].reshape(gu, S5_CH, q * nkb)
            ugt = jnp.concatenate([rows[:, :, (q - 1 - j) * nkb:(q - j) * nkb] for j in range(q)], axis=1)
            y = bdot(tt_ref[gsl], ugt)
            vt = bdot(wt_ref[gsl], ugt)
            v = jnp.swapaxes(vt, 1, 2)
            v_sw = jnp.swapaxes(jnp.concatenate([vt[:, S5_STATE:], vt[:, :S5_STATE]], axis=1), 1, 2)
            a1 = a1_ref[gsl]
            a2 = a2_ref[gsl]
            s = carry[gsl]
            s_sw = pltpu.roll(s.reshape(gu * nb, 2 * S5_STATE), S5_STATE, 1).reshape(s.shape)
            prev = []
            for k in range(nk):
                prev.append(s)
                s, s_sw = (a1 * s + a2 * s_sw + v[:, k * nb:(k + 1) * nb, :],
                           a1 * s_sw - a2 * s + v_sw[:, k * nb:(k + 1) * nb, :])
            carry[gsl] = s
            sprev = jnp.concatenate(prev, axis=1).astype(BF16)
            y = y + bdot_nt(zt_ref[gsl], sprev)
            for t in range(q):
                yt[pl.ds(r0, gu * S5_CH), t * nkb:(t + 1) * nkb] += (
                    y[:, t * S5_CH:(t + 1) * S5_CH, :].reshape(gu * S5_CH, nkb))
            return 0

        lax.fori_loop(0, ng // gu, groups, 0)
        if perm:
            sfin_ref[...] = carry[...]
        else:
            lane_s = lax.broadcasted_iota(jnp.int32, (nb, 2 * S5_STATE), 1)
            for pr in range(ng // 2):
                cols = slice(pr * 2 * S5_STATE, (pr + 1) * 2 * S5_STATE)
                c0, c1 = carry[2 * pr], carry[2 * pr + 1]
                sfin_ref[0, :, cols] = jnp.where(lane_s < S5_STATE, c0, pltpu.roll(c1, S5_STATE, 1))
                sfin_ref[1, :, cols] = jnp.where(lane_s < S5_STATE, pltpu.roll(c0, S5_STATE, 1), c1)

    for j in range(npt):
        @pl.when(ph == npi + j)
        def _():
            finish(j)


def _s5_mix(q, nb, x, ln_g, wu, tt, wt, zt, a1, a2, s0, dcol, wglut, bglu, wo):
    nseq, l, d = x.shape
    nblk = l // q
    nlt = nblk * nseq
    nkb = min(LANES, nlt)
    ntile = nlt // nkb
    tok = q * nkb
    npt = max(1, tok // TOKEN_TILE)
    dm = wo.shape[1]
    perm = q > 1
    if perm:
        half = 8
        npi = q // half
        assert nb == nseq and d % LANES == 0 and q % half == 0
        xv = x.reshape(nseq, nblk, npi, half, d // LANES, LANES).transpose(0, 1, 2, 4, 3, 5)
        assert npt % npi == 0
        x_spec = pl.BlockSpec((nseq, nkb // nseq, None, d // LANES, half, LANES),
                              lambda i, j: (0, i, jnp.minimum(j, npi - 1), 0, 0, 0))
    else:
        npi = 1
        xv = x.reshape(1, nseq, d)
        x_spec = pl.BlockSpec((1, nkb, d), lambda i, j: (0, i, 0))
    body = functools.partial(_s5_mix_body, q, nb, nkb, npi, npt, perm)
    p_spec = pl.BlockSpec((q // npt, nkb, dm), lambda i, j: (jnp.maximum(j - npi, 0), i, 0))
    return pl.pallas_call(
        body,
        grid=(ntile, npi + npt),
        in_specs=[x_spec, _const_spec(ln_g.shape), _const_spec(wu.shape), _const_spec(tt.shape),
                  _const_spec(wt.shape), _const_spec(zt.shape), _const_spec(a1.shape), _const_spec(a2.shape),
                  _const_spec(s0.shape), _const_spec(dcol.shape), _const_spec(wglut.shape),
                  _const_spec(bglu.shape), _const_spec(wo.shape)],
        out_specs=[p_spec, pl.BlockSpec(s0.shape, lambda i, j: (0, 0, 0))],
        out_shape=[jax.ShapeDtypeStruct((q, nlt, dm), F32), jax.ShapeDtypeStruct(s0.shape, F32)],
        scratch_shapes=[pltpu.VMEM((d, tok), BF16), pltpu.VMEM((d, tok), F32),
                        pltpu.VMEM((tt.shape[0], nb, 2 * S5_STATE), F32),
                        pltpu.VMEM((wu.shape[1], wu.shape[0]), BF16)],
        compiler_params=_params("arbitrary", "arbitrary"),
        name=f"s5_mix_q{q}",
    )(xv, ln_g, wu, tt, wt, zt, a1, a2, s0, dcol, wglut, bglu, wo)


def _kv_body(m_ref, wk_ref, wv_ref, k_ref, v_ref, kb_ref, vb_ref):
    tm, rows, _ = k_ref.shape
    nt = rows // X_HEADS
    m = m_ref[...].astype(BF16)
    for w_ref, o_ref, ob_ref in ((wk_ref, k_ref, kb_ref), (wv_ref, v_ref, vb_ref)):
        r = _dot(m, w_ref[...])
        ob_ref[...] = r.astype(BF16)
        flat = o_ref.reshape(tm * rows, LANES)
        for h in range(X_HEADS):
            for dt in range(nt):
                c = (h * nt + dt) * LANES
                flat[pl.ds(dt * X_HEADS + h, tm, stride=rows), :] = r[:, c:c + LANES]


def _kv_proj(mem2d, wk, wv):
    t, d = mem2d.shape
    tm = min(TOKEN_TILE, t)
    rows = d // LANES
    row = pl.BlockSpec((tm, d), lambda i: (i, 0))
    tile = pl.BlockSpec((tm, rows, LANES), lambda i: (i, 0, 0))
    return pl.pallas_call(
        _kv_body, grid=(t // tm,),
        in_specs=[row, _const_spec(wk.shape), _const_spec(wv.shape)],
        out_specs=[tile, tile, row, row],
        out_shape=[jax.ShapeDtypeStruct((t, rows, LANES), F32)] * 2 + [jax.ShapeDtypeStruct((t, d), BF16)] * 2,
        compiler_params=_params("arbitrary"),
        name="kv_proj",
    )(mem2d, wk, wv)


def _tile_unview_kv(kv8, b, m):
    nt = kv8.shape[1] // X_HEADS
    return kv8.reshape(b, m, nt, X_HEADS, LANES).transpose(0, 1, 3, 2, 4).reshape(b, m, X_HEADS, nt * LANES)


def _attn_prompt_body(x_ref, ys_ref, p_ref, wo1_ref, g2_ref, wq_ref, k_ref, v_ref, wxo_ref, o_ref):
    d = x_ref.shape[-1]
    hd = d // X_HEADS
    x1 = x_ref[...] + _dot(ys_ref[...], wo1_ref[...]) + p_ref[...]
    hq = _rms(x1, g2_ref[...]).astype(BF16)
    qv = _dot(hq, wq_ref[...]).astype(BF16)
    kb = k_ref[...]
    vb = v_ref[...]
    outs = []
    for h in range(X_HEADS):
        sl = slice(h * hd, (h + 1) * hd)
        s = _dot_nt(qv[:, sl], kb[:, sl]) * (hd ** -0.5)
        e = jnp.exp(s - jnp.max(s, axis=-1, keepdims=True))
        p = (e / jnp.sum(e, axis=-1, keepdims=True)).astype(BF16)
        outs.append(_dot(p, vb[:, sl]))
    o = jnp.concatenate(outs, axis=1).astype(BF16)
    o_ref[...] = x1 + _dot(o, wxo_ref[...])


def _attn_prompt(x, ys, pm, wo1, g2, wq, mk, mv, wxo, make_rider):
    b, l, d = x.shape
    nm = mk.shape[1]
    tm = min(TOKEN_TILE, l)
    nl = l // tm
    row = pl.BlockSpec((None, tm, d), lambda i, j: (i, j, 0))
    kv = pl.BlockSpec((None, nm, d), lambda i, j: (i, 0, 0))
    ins = [x, ys, pm, wo1, g2, wq, mk, mv, wxo]
    specs = [row, row, row, _const_spec(wo1.shape), _const_spec(g2.shape), _const_spec(wq.shape), kv, kv,
             _const_spec(wxo.shape)]
    rider = make_rider(lambda i, j: i * nl + j, b * nl)
    body = functools.partial(_two_bodies, _attn_prompt_body, len(ins), 1, rider[0], len(rider[1]))
    return pl.pallas_call(
        body, grid=(b, nl),
        in_specs=specs + list(rider[2]),
        out_specs=[row] + list(rider[3]),
        out_shape=[jax.ShapeDtypeStruct((b, l, d), F32)] + list(rider[4]),
        compiler_params=_params("arbitrary", "arbitrary"),
        name="attn_prompt",
    )(*ins, *rider[1])


def _merge_q_body(x_ref, ys_ref, p_ref, wo1_ref, g2_ref, wq_ref, x1_ref, q_ref):
    x1 = x_ref[...] + _dot(ys_ref[...], wo1_ref[...]) + p_ref[...]
    x1_ref[...] = x1
    q_ref[...] = _dot(_rms(x1, g2_ref[...]).astype(BF16), wq_ref[...])


def _merge_q(x2d, ys, pm, wo1, g2, wq):
    t, d = x2d.shape
    return pl.pallas_call(
        _merge_q_body,
        out_shape=[jax.ShapeDtypeStruct((t, d), F32)] * 2,
        compiler_params=pltpu.CompilerParams(vmem_limit_bytes=VMEM_LIMIT),
        name="merge_q",
    )(x2d, ys, pm, wo1, g2, wq)


def _tile_view(kv):
    b, m, nh, hd = kv.shape
    nt = hd // LANES
    return kv.reshape(b, m, nh, nt, LANES).transpose(0, 1, 3, 2, 4).reshape(b, m, nt * nh, LANES)


def _tile_unview(o):
    b, rows, _ = o.shape
    nt = rows // X_HEADS
    return o.reshape(b, nt, X_HEADS, LANES).transpose(0, 2, 1, 3).reshape(b, rows * LANES)


def _attn_step_body(q_ref, k_ref, v_ref, o_ref):
    bt, nm, rows, _ = k_ref.shape
    hd = rows * LANES // X_HEADS
    for b in range(bt):
        r = jnp.sum(k_ref[b] * q_ref[b], axis=-1, keepdims=True)
        s = (r + pltpu.roll(r, X_HEADS, 1)) * (hd ** -0.5)
        e = jnp.exp(s - jnp.max(s, axis=0, keepdims=True))
        p = e / jnp.sum(e, axis=0, keepdims=True)
        o_ref[b] = jnp.sum(p * v_ref[b], axis=0)


def _attn_step_rider(q8, k8, v8, nsteps):
    nbt, rows, _ = q8.shape
    nm = k8.shape[1]
    assert rows == 2 * X_HEADS, "score assembly assumes two 128-lane tiles per head"
    assert nbt % nsteps == 0
    bt = nbt // nsteps
    qs = pl.BlockSpec((bt, rows, LANES), lambda i: (i, 0, 0))
    kv = pl.BlockSpec((bt, nm, rows, LANES), lambda i: (i, 0, 0, 0))
    return (_attn_step_body, (q8, k8, v8), (qs, kv, kv), (qs,),
            (jax.ShapeDtypeStruct((nbt, rows, LANES), F32),))


def _ffn_body(has_o, final, *refs):
    if has_o:
        x_ref, o_in_ref, wxo_ref, g3_ref, wg_ref, wu_ref, wd_ref, gf_ref, y_ref = refs
        x2 = x_ref[...] + _dot(o_in_ref[...].astype(BF16), wxo_ref[...])
    else:
        x_ref, g3_ref, wg_ref, wu_ref, wd_ref, gf_ref, y_ref = refs
        x2 = x_ref[...]
    hf = _rms(x2, g3_ref[...]).astype(BF16)
    ff = wg_ref.shape[1]
    x3 = x2
    for lo in range(0, ff, FF_CHUNK):
        hi = min(lo + FF_CHUNK, ff)
        act = (_silu(_dot(hf, wg_ref[:, lo:hi])) * _dot(hf, wu_ref[:, lo:hi])).astype(BF16)
        x3 = x3 + _dot(act, wd_ref[lo:hi, :])
    y_ref[...] = _rms(x3, gf_ref[...]) if final else x3


def _ffn(x2d, g3, wg, wu, wd, gf, final, o_in=None, wxo=None, rider=None):
    t, d = x2d.shape
    tm = min(TOKEN_TILE, t)
    row = pl.BlockSpec((tm, d), lambda i: (i, 0))
    has_o = o_in is not None
    ins = [x2d] + ([o_in, wxo] if has_o else []) + [g3, wg, wu, wd, gf]
    specs = [row] + ([row, _const_spec(wxo.shape)] if has_o else []) + [
        _const_spec(g3.shape), _const_spec(wg.shape), _const_spec(wu.shape), _const_spec(wd.shape),
        _const_spec(gf.shape)]
    body = functools.partial(_ffn_body, has_o, final)
    out_specs, out_shape = [row], [jax.ShapeDtypeStruct((t, d), F32)]
    if rider is not None:
        body = functools.partial(_two_bodies, body, len(ins), 1, rider[0], len(rider[1]))
        ins, specs = ins + list(rider[1]), specs + list(rider[2])
        out_specs, out_shape = out_specs + list(rider[3]), out_shape + list(rider[4])
    res = pl.pallas_call(
        body, grid=(t // tm,),
        in_specs=specs, out_specs=out_specs, out_shape=out_shape,
        compiler_params=_params("arbitrary"),
        name="ffn_o" if has_o else "ffn",
    )(*ins)
    return res[0] if rider is None else res


def _two_bodies(body_a, n_in_a, n_out_a, body_b, n_in_b, *refs):
    outs = refs[n_in_a + n_in_b:]
    body_a(*refs[:n_in_a], *outs[:n_out_a])
    body_b(*refs[n_in_a:n_in_a + n_in_b], *outs[n_out_a:])


def _layer_weights(i, ln1_g, w_in, conv_w, conv_b, dt_bias, a_log, d_ssd, gn_g, lam_re, lam_im, log_step,
                   b_re, b_im, c_re, c_im, d_s5, w_glu, b_glu, w_out, ln2_g, w_xq, w_xk, w_xv, w_xo, ln3_g,
                   w_gate, w_up, w_down):
    d = w_in.shape[1]
    n_heads = dt_bias.shape[1]
    dssd = n_heads * SSD_HEAD_DIM
    conv_dim = conv_w.shape[2]
    o1, o2, o3 = dssd, dssd + conv_dim, dssd + conv_dim + n_heads
    win = w_in[i]
    w = {}
    w["ln1"] = ln1_g[i].reshape(1, d)
    w["win"] = win.astype(BF16)
    w["dims"] = (dssd, conv_dim)
    w["wu_s5"] = w["win"][:, o3:]
    w["cw"] = conv_w[i]
    w["cb"] = conv_b[i].reshape(1, conv_dim)
    w["dtb"] = jnp.pad(dt_bias[i], (0, LANES - n_heads)).reshape(1, LANES)
    w["alog"] = jnp.pad(a_log[i], (0, LANES - n_heads)).reshape(1, LANES)
    w["dexp"] = jnp.repeat(d_ssd[i], SSD_HEAD_DIM).reshape(1, dssd)
    w["gn"] = gn_g[i].reshape(1, dssd)
    w["s5"] = (lam_re[i], lam_im[i], log_step[i], b_re[i], b_im[i], c_re[i], c_im[i])
    ds5 = d_s5.shape[1]
    w["dcol"] = d_s5[i].reshape(ds5, 1)
    w["wglut"] = w_glu[i].T.astype(BF16)
    w["bglu"] = b_glu[i].reshape(ds5, 1)
    w["wo1"] = w_out[i, :dssd].astype(BF16)
    w["wo2"] = w_out[i, dssd:].astype(BF16)
    w["ln2"] = ln2_g[i].reshape(1, d)
    w["wq"] = w_xq[i].astype(BF16)
    w["wk"] = w_xk[i].astype(BF16)
    w["wv"] = w_xv[i].astype(BF16)
    w["wxo"] = w_xo[i].astype(BF16)
    w["ln3"] = ln3_g[i].reshape(1, d)
    w["wg"] = w_gate[i].astype(BF16)
    w["wu"] = w_up[i].astype(BF16)
    w["wdn"] = w_down[i].astype(BF16)
    return w


def _rows_to_state(s):
    p = s.shape[-1] // 2
    st = s.transpose(1, 0, 2)
    return st[..., :p], st[..., p:]


def kernel(x_prompt, x_sample, mem_prompt, state_conv, state_ssm, state_s5_re, state_s5_im, cache_mem_k, cache_mem_v, ln1_g, w_in, conv_w, conv_b, dt_bias, a_log, d_ssd, gn_g, lam_re, lam_im, log_step, b_re, b_im, c_re, c_im, d_s5, w_glu, b_glu, w_out, ln2_g, w_xq, w_xk, w_xv, w_xo, ln3_g, w_gate, w_up, w_down, final_g):
    bp, seq, d = x_prompt.shape
    bs = x_sample.shape[0]
    depth = w_in.shape[0]
    nm = mem_prompt.shape[1]
    ng, ns = lam_re.shape[1], lam_re.shape[2]
    gf = final_g.reshape(1, d)
    hp = x_prompt
    hs = x_sample.reshape(bs, d)
    outs = {k: [] for k in ("conv_p", "ssm_p", "re_p", "im_p", "mk_p", "mv_p", "conv_s", "ssm_s", "re_s", "im_s")}
    yp = ys_out = None
    for i in range(depth):
        w = _layer_weights(i, ln1_g, w_in, conv_w, conv_b, dt_bias, a_log, d_ssd, gn_g, lam_re, lam_im, log_step,
                           b_re, b_im, c_re, c_im, d_s5, w_glu, b_glu, w_out, ln2_g, w_xq, w_xk, w_xv, w_xo, ln3_g,
                           w_gate, w_up, w_down)
        last = i == depth - 1
        mk8, mv8, mk, mv = _kv_proj(mem_prompt.reshape(bp * nm, d), w["wk"], w["wv"])
        mk = mk.reshape(bp, nm, d)
        mv = mv.reshape(bp, nm, d)
        y_ssd, conv_p, ssm_p = _ssd_prompt(hp, w["ln1"], w["win"], w["cw"], w["cb"], w["dtb"], w["alog"], w["dexp"],
                                           w["gn"])
        q = S5_Q
        tt, wt, zt, a1, a2, a11, a21 = _s5_prep(q, *w["s5"])
        s0 = jnp.zeros((ng, bp, 2 * ns), F32)
        pperm, sfin = _s5_mix(q, bp, hp, w["ln1"], w["wu_s5"],tt, wt, zt, a1, a2, s0, w["dcol"], w["wglut"],
                              w["bglu"], w["wo2"])
        pm = pperm.reshape(q, seq // q, bp, d).transpose(2, 1, 0, 3).reshape(bp, seq, d)
        re_p, im_p = _rows_to_state(sfin)
        attn_steps = bp * (seq // min(TOKEN_TILE, seq))
        zs, xbcs, dts = _inproj(hs, w["ln1"], w["win"], *w["dims"], bs // attn_steps)
        ssd_rider = functools.partial(_ssd_step_rider, zs, xbcs, dts, state_conv[i], state_ssm[i],
                                      w["cw"], w["cb"], w["dtb"], w["alog"], w["dexp"], w["gn"])
        x2, ys_ssd, conv_s, ssm_s = _attn_prompt(hp, y_ssd, pm, w["wo1"], w["ln2"], w["wq"], mk, mv, w["wxo"],
                                                 ssd_rider)
        ys_ssd = ys_ssd.reshape(bs, -1)
        tt1, wt1, zt1 = _s5_single_token_mats(q, tt, wt, zt)
        s0s = jnp.stack([state_s5_re[i].reshape(bs, ng * ns), state_s5_im[i].reshape(bs, ng * ns)])
        ps, sfin_s = _s5_mix(1, bs, hs.reshape(bs, 1, d), w["ln1"], w["wu_s5"],tt1, wt1, zt1, a11, a21, s0s,
                             w["dcol"], w["wglut"], w["bglu"], w["wo2"])
        re_s, im_s = sfin_s[0].reshape(bs, ng, ns), sfin_s[1].reshape(bs, ng, ns)
        x1s, qs = _merge_q(hs, ys_ssd, ps.reshape(bs, d), w["wo1"], w["ln2"], w["wq"])
        attn_rider = _attn_step_rider(_tile_view(qs.reshape(bs, 1, X_HEADS, d // X_HEADS))[:, 0],
                                      _tile_view(cache_mem_k[i]), _tile_view(cache_mem_v[i]),
                                      (bp * seq) // min(TOKEN_TILE, bp * seq))
        hp, o_s = _ffn(x2.reshape(bp * seq, d), w["ln3"], w["wg"], w["wu"], w["wdn"], gf, last, rider=attn_rider)
        hp = hp.reshape(bp, seq, d)
        hs = _ffn(x1s, w["ln3"], w["wg"], w["wu"], w["wdn"], gf, last, o_in=_tile_unview(o_s), wxo=w["wxo"])
        for k, v in (("conv_p", conv_p), ("ssm_p", ssm_p), ("re_p", re_p), ("im_p", im_p),
                     ("mk_p", _tile_unview_kv(mk8, bp, nm)), ("mv_p", _tile_unview_kv(mv8, bp, nm)),
                     ("conv_s", conv_s), ("ssm_s", ssm_s), ("re_s", re_s), ("im_s", im_s)):
            outs[k].append(v)
    st = lambda k: jnp.stack(outs[k])
    return (hp, hs.reshape(bs, 1, d), st("conv_p"), st("ssm_p"), st("re_p"), st("im_p"), st("mk_p"), st("mv_p"),
            st("conv_s"), st("ssm_s"), st("re_s"), st("im_s"))
```

```python
import functools
import math

import jax
import jax.numpy as jnp
from jax import lax
from jax.experimental import pallas as pl
from jax.experimental.pallas import tpu as pltpu

F32 = jnp.float32
BF16 = jnp.bfloat16
EPS = 1e-6

LANES = 128
VMEM_LIMIT = 56 * 1024 * 1024

SSD_HEAD_DIM = 64
SSD_STATE = 128
SSD_GROUPS = 2
SSD_CHUNK = 128
CONV_W = 4
S5_CH = 16
S5_STATE = 64
S5_Q = 16
S5_GROUP_BATCH = 16
X_HEADS = 4
TOKEN_TILE = 512
PROJ_PIECE = 512
SSD_SEQS_PER_STEP = 2
FF_CHUNK = 1024


def _const_spec(shape):
    nd = len(shape)
    return pl.BlockSpec(shape, lambda *_: (0,) * nd, pipeline_mode=pl.Buffered(1))


def _params(*sem):
    return pltpu.CompilerParams(dimension_semantics=sem, vmem_limit_bytes=VMEM_LIMIT)


def _rms(x, g):
    return x * lax.rsqrt(jnp.mean(x * x, axis=-1, keepdims=True) + EPS) * g


def _sigmoid(x):
    return 1.0 / (1.0 + jnp.exp(-x))


def _silu(x):
    return x * _sigmoid(x)


def _gelu_tanh(x):
    return 0.5 * x * (1.0 + jnp.tanh(math.sqrt(2.0 / math.pi) * (x + 0.044715 * (x * x * x))))


def _softplus(x):
    return jnp.maximum(x, 0.0) + jnp.log1p(jnp.exp(-jnp.abs(x)))


def _dot(a, b):
    return jnp.dot(a, b, preferred_element_type=F32)


def _dot_nt(a, b):
    return lax.dot_general(a, b, (((1,), (1,)), ((), ())), preferred_element_type=F32)


def _dot_tn(a, b):
    return lax.dot_general(a, b, (((0,), (0,)), ((), ())), preferred_element_type=F32)


def _proj_windows(d_ssd, conv_dim):
    assert d_ssd % LANES == 0 and conv_dim % LANES == 0
    return (0, d_ssd), (d_ssd, d_ssd + conv_dim), (d_ssd + conv_dim, d_ssd + conv_dim + LANES)


def _inproj_body(wins, x_ref, g_ref, w_ref, z_ref, xbc_ref, dt_ref):
    h = _rms(x_ref[...], g_ref[...]).astype(BF16)
    for (lo, hi), o_ref in zip(wins, (z_ref, xbc_ref, dt_ref)):
        o_ref[...] = _dot(h, w_ref[:, lo:hi]).reshape(o_ref.shape)


def _inproj(x2d, ln_g, w_in, d_ssd, conv_dim, group):
    t, d = x2d.shape
    wins = _proj_windows(d_ssd, conv_dim)
    return pl.pallas_call(
        functools.partial(_inproj_body, wins),
        out_shape=[jax.ShapeDtypeStruct((t // group, group, hi - lo), F32) for lo, hi in wins],
        compiler_params=pltpu.CompilerParams(vmem_limit_bytes=VMEM_LIMIT),
        name="inproj",
    )(x2d, ln_g, w_in)


def _expand_heads(v, n_heads):
    rows = v.shape[0]
    lane = lax.broadcasted_iota(jnp.int32, (rows, LANES), 1)
    pieces = [jnp.where(lane < SSD_HEAD_DIM, v[:, 2 * j:2 * j + 1], v[:, 2 * j + 1:2 * j + 2])
              for j in range(n_heads // 2)]
    return jnp.concatenate(pieces, axis=1)


def _ssd_chunk_math(z, x, dt_raw, last_chunk, cw_ref, cb_ref, dtb_ref, alog_ref, dexp_ref, gn_ref, e2_ref,
                    y_ref, conv_ref, ssm_ref, xpad, state, side_jobs=()):
    jobs = list(side_jobs)

    def side(n=1):
        for _ in range(n):
            if jobs:
                jobs.pop(0)()

    q = SSD_CHUNK
    d_ssd = z.shape[-1]
    n_heads = d_ssd // SSD_HEAD_DIM
    hpg = n_heads // SSD_GROUPS
    gw = hpg * SSD_HEAD_DIM
    nct = xpad.shape[0]
    xt = d_ssd // LANES

    for t in range(nct):
        xpad[t, 8:8 + q, :] = x[:, t * LANES:(t + 1) * LANES]
    cw = cw_ref[...]
    conv = (cw[3] * xpad[:, 8:8 + q, :] + cw[2] * xpad[:, 7:7 + q, :] + cw[1] * xpad[:, 6:6 + q, :]
            + cw[0] * xpad[:, 5:5 + q, :] + cb_ref[...])
    xpad[:, 0:8, :] = xpad[:, q:q + 8, :]
    xact = _silu(conv)
    xs = jnp.concatenate([xact[t] for t in range(xt)], axis=1)
    bm = xact[xt:xt + SSD_GROUPS].astype(BF16)
    cm = xact[xt + SSD_GROUPS:].astype(BF16)
    side()

    dt = _softplus(dt_raw + dtb_ref[...])
    a = -jnp.exp(alog_ref[...])
    row = lax.broadcasted_iota(jnp.int32, (q, q), 0)
    col = lax.broadcasted_iota(jnp.int32, (q, q), 1)
    causal = row >= col
    tri = jnp.where(causal, 1.0, 0.0).astype(F32)
    acum = jnp.dot(tri, dt * a, precision=lax.Precision.HIGHEST, preferred_element_type=F32)
    acum_t = acum.T
    last = acum[q - 1:q, :]

    def expand(v):
        hi = v.astype(BF16)
        lo = (v - hi.astype(F32)).astype(BF16)
        return _dot(jnp.concatenate([hi, lo], axis=1), e2_ref[...])

    dt_e = expand(dt)
    ea_e = expand(jnp.exp(acum))
    dend_e = expand(jnp.exp(last - acum))

    dtx = xs * dt_e
    dtx_b = dtx.astype(BF16)
    xdec_b = (dtx * dend_e).astype(BF16)
    side()

    lane = lax.broadcasted_iota(jnp.int32, (q, LANES), 1)
    zero_b = jnp.zeros((q, LANES), BF16)
    y_tiles = []
    for g in range(SSD_GROUPS):
        bg = bm[g]
        cg = cm[g]
        cb = _dot_nt(cg, bg)
        gs = slice(g * gw, (g + 1) * gw)
        st_g = state[:, gs]
        y_off = _dot(cg, st_g.astype(BF16)) * ea_e[:, gs]
        for pr in range(hpg // 2):
            tile = g * (hpg // 2) + pr
            ms = []
            for h in (2 * tile, 2 * tile + 1):
                seg = acum[:, h:h + 1] - acum_t[h:h + 1, :]
                lmat = jnp.exp(jnp.where(causal, seg, -jnp.inf))
                ms.append((cb * lmat).astype(BF16))
            dtile = dtx_b[:, tile * LANES:(tile + 1) * LANES]
            rhs = jnp.concatenate([jnp.where(lane < SSD_HEAD_DIM, dtile, zero_b),
                                   jnp.where(lane >= SSD_HEAD_DIM, dtile, zero_b)], axis=0)
            y_tiles.append(_dot(jnp.concatenate(ms, axis=1), rhs)
                           + y_off[:, pr * LANES:(pr + 1) * LANES])
            if pr % 2 == 1:
                side()
        new = _dot_tn(bg, xdec_b[:, gs])
        state[:, gs] = st_g * ea_e[q - 1:q, gs] + new
    y = jnp.concatenate(y_tiles, axis=1) + dexp_ref[...] * xs
    y = y * _silu(z)
    ms = jnp.mean(y * y, axis=-1, keepdims=True)
    side(len(jobs))
    y_ref[...] = (y * lax.rsqrt(ms + EPS) * gn_ref[...]).astype(y_ref.dtype)

    @pl.when(last_chunk)
    def _():
        conv_ref[...] = x[q - (CONV_W - 1):q, :]
        ssm_ref[...] = state[...].T.reshape(ssm_ref.shape)


def _ssd_prompt_body(wins, x_ref, g_ref, w_ref, cw_ref, cb_ref, dtb_ref, alog_ref, dexp_ref, gn_ref,
                     e2_ref, y_ref, conv_ref, ssm_ref, za, xa, da, zb, xb, db, xpad, state):
    i = pl.program_id(0)
    j = pl.program_id(1)
    nc = pl.num_programs(1) - 1
    ns, q, d = x_ref.shape

    @pl.when(jnp.logical_and(i == 0, j == 0))
    def _():
        for r in (zb, xb, db):
            r[...] = jnp.zeros(r.shape, F32)

    @pl.when(j <= 1)
    def _():
        xpad[:, :, 0:8, :] = jnp.zeros(xpad.shape[:2] + (8, LANES), F32)
        state[...] = jnp.zeros(state.shape, F32)

    def step(wr, rd):
        xin = x_ref[...].reshape(ns * q, d)
        msx = jnp.mean(xin * xin, axis=-1, keepdims=True)
        hbox = []

        def proj_piece(dst, base, lo, hi):
            def run():
                if not hbox:
                    hbox.append((xin * lax.rsqrt(msx + EPS) * g_ref[...]).astype(BF16))
                dst[:, lo:hi] = _dot(hbox[0], w_ref[:, base + lo:base + hi])
            return run

        pieces = []
        for dst, (w_lo, w_hi) in ((wr[0], wins[0]), (wr[2], wins[2]), (wr[1], wins[1])):
            n = w_hi - w_lo
            pieces += [proj_piece(dst, w_lo, lo, min(lo + PROJ_PIECE, n)) for lo in range(0, n, PROJ_PIECE)]
        per = -(-len(pieces) // ns)
        for s in range(ns):
            rows = slice(s * q, (s + 1) * q)
            _ssd_chunk_math(rd[0][rows, :], rd[1][rows, :], rd[2][rows, :], j == nc, cw_ref, cb_ref, dtb_ref,
                            alog_ref, dexp_ref, gn_ref, e2_ref, y_ref.at[s], conv_ref.at[s], ssm_ref.at[s],
                            xpad.at[s], state.at[s], pieces[s * per:(s + 1) * per])

    @pl.when(j % 2 == 0)
    def _():
        step((za, xa, da), (zb, xb, db))

    @pl.when(j % 2 == 1)
    def _():
        step((zb, xb, db), (za, xa, da))


def _ssd_prompt(x, ln_g, w_in, cw, cb, dtb, alog, dexp, gn):
    b, l, d = x.shape
    conv_dim = cw.shape[1]
    d_ssd = conv_dim - 2 * SSD_GROUPS * SSD_STATE
    wins = _proj_windows(d_ssd, conv_dim)
    n_heads = d_ssd // SSD_HEAD_DIM
    q = SSD_CHUNK
    nc = l // q
    assert SSD_STATE == LANES and conv_dim == d_ssd + 2 * SSD_GROUPS * SSD_STATE
    nct = conv_dim // LANES
    cw4 = cw.reshape(CONV_W, nct, 1, LANES)
    cb3 = cb.reshape(nct, 1, LANES)
    e2 = (jnp.arange(2 * LANES)[:, None] % LANES == jnp.arange(d_ssd)[None, :] // SSD_HEAD_DIM).astype(BF16)
    ns = SSD_SEQS_PER_STEP if b % SSD_SEQS_PER_STEP == 0 else 1
    slot = lambda n: pltpu.VMEM((ns * q, n), F32)
    return pl.pallas_call(
        functools.partial(_ssd_prompt_body, wins),
        grid=(b // ns, nc + 1),
        in_specs=[pl.BlockSpec((ns, q, d), lambda i, j: (i, jnp.minimum(j, nc - 1), 0)),
                  _const_spec(ln_g.shape), _const_spec(w_in.shape),
                  _const_spec(cw4.shape), _const_spec(cb3.shape), _const_spec(dtb.shape), _const_spec(alog.shape),
                  _const_spec(dexp.shape), _const_spec(gn.shape), _const_spec(e2.shape)],
        out_specs=[pl.BlockSpec((ns, q, d_ssd), lambda i, j: (i, jnp.maximum(j - 1, 0), 0)),
                   pl.BlockSpec((ns, CONV_W - 1, conv_dim), lambda i, j: (i, 0, 0)),
                   pl.BlockSpec((ns, n_heads, SSD_HEAD_DIM, SSD_STATE), lambda i, j: (i, 0, 0, 0))],
        out_shape=[jax.ShapeDtypeStruct((b, l, d_ssd), BF16),
                   jax.ShapeDtypeStruct((b, CONV_W - 1, conv_dim), F32),
                   jax.ShapeDtypeStruct((b, n_heads, SSD_HEAD_DIM, SSD_STATE), F32)],
        scratch_shapes=[slot(d_ssd), slot(conv_dim), slot(LANES), slot(d_ssd), slot(conv_dim), slot(LANES),
                        pltpu.VMEM((ns, nct, q + 8, LANES), F32), pltpu.VMEM((ns, SSD_STATE, d_ssd), F32)],
        compiler_params=_params("arbitrary", "arbitrary"),
        name="ssd_prompt",
    )(x, ln_g, w_in, cw4, cb3, dtb, alog, dexp, gn, e2)


def _ssd_step_body(z_ref, xbc_ref, dt_ref, cs_ref, st_ref, cw_ref, cb_ref, dtb_ref, alog_ref, dexp_ref, gn_ref,
                   y_ref, conv_ref, ssm_ref):
    bt, d_ssd = z_ref.shape
    n_heads = d_ssd // SSD_HEAD_DIM
    hpg = n_heads // SSD_GROUPS
    x = xbc_ref[...]
    cw = cw_ref[...]
    cs = [cs_ref[:, k, :] for k in range(CONV_W - 1)]
    conv = cw[0:1] * cs[0] + cw[1:2] * cs[1] + cw[2:3] * cs[2] + cw[3:4] * x + cb_ref[...]
    conv_ref[:, 0, :] = cs[1]
    conv_ref[:, 1, :] = cs[2]
    conv_ref[:, 2, :] = x
    xact = _silu(conv)
    xs = xact[:, :d_ssd]
    bm = xact[:, d_ssd:d_ssd + SSD_GROUPS * SSD_STATE]
    cm = xact[:, d_ssd + SSD_GROUPS * SSD_STATE:]
    dt = _softplus(dt_ref[...] + dtb_ref[...])
    da = jnp.exp(dt * (-jnp.exp(alog_ref[...])))
    dtx = xs * _expand_heads(dt, n_heads)
    pad = jnp.zeros((LANES - bt, d_ssd), F32)
    to_cols = lambda v: jnp.concatenate([v, pad], axis=0).T
    dtx_t = to_cols(dtx)
    da_t = to_cols(_expand_heads(da, n_heads))
    lane = lax.broadcasted_iota(jnp.int32, (d_ssd, LANES), 1)
    gw = hpg * SSD_HEAD_DIM
    rows_of = lambda v, b: jnp.concatenate(
        [jnp.broadcast_to(v[b:b + 1, g * SSD_STATE:(g + 1) * SSD_STATE], (gw, SSD_STATE)) for g in range(SSD_GROUPS)],
        axis=0)
    y_t = jnp.zeros((d_ssd, LANES), F32)
    for b in range(bt):
        s0 = st_ref[b].reshape(d_ssd, SSD_STATE)
        s_new = s0 * da_t[:, b:b + 1] + dtx_t[:, b:b + 1] * rows_of(bm, b)
        ssm_ref[b] = s_new.reshape(n_heads, SSD_HEAD_DIM, SSD_STATE)
        ycol = jnp.sum(s_new * rows_of(cm, b), axis=1, keepdims=True)
        y_t = jnp.where(lane == b, ycol, y_t)
    y = y_t.T[:bt, :] + dexp_ref[...] * xs
    y = y * _silu(z_ref[...])
    y_ref[...] = _rms(y, gn_ref[...]).astype(y_ref.dtype)


def _ssd_step_rider(z, xbc, dt, conv_state, ssm_state, cw, cb, dtb, alog, dexp, gn, step_of, nsteps):
    _, bt, d_ssd = z.shape
    conv_dim = xbc.shape[-1]
    n_heads = d_ssd // SSD_HEAD_DIM
    assert z.shape[0] == nsteps and conv_state.shape[0] == nsteps * bt
    row = lambda n: pl.BlockSpec((None, bt, n), lambda *g: (step_of(*g), 0, 0))
    cs_spec = pl.BlockSpec((bt, CONV_W - 1, conv_dim), lambda *g: (step_of(*g), 0, 0))
    st_spec = pl.BlockSpec((bt, n_heads, SSD_HEAD_DIM, SSD_STATE), lambda *g: (step_of(*g), 0, 0, 0))
    cst = lambda a: pl.BlockSpec(a.shape, lambda *g: (0,) * a.ndim, pipeline_mode=pl.Buffered(1))
    ins = (z, xbc, dt, conv_state, ssm_state, cw, cb, dtb, alog, dexp, gn)
    specs = (row(d_ssd), row(conv_dim), row(LANES), cs_spec, st_spec, cst(cw), cst(cb), cst(dtb), cst(alog),
             cst(dexp), cst(gn))
    out_shapes = (jax.ShapeDtypeStruct((nsteps, bt, d_ssd), BF16),
                  jax.ShapeDtypeStruct(conv_state.shape, F32),
                  jax.ShapeDtypeStruct(ssm_state.shape, F32))
    return (_ssd_step_body, ins, specs, (row(d_ssd), cs_spec, st_spec), out_shapes)


def _split_bf16(x):
    hi = x.astype(BF16)
    lo = (x - hi.astype(F32)).astype(BF16)
    return hi, lo


def _s5_prep_body(q, ls_ref, lrp_ref, lip_ref, lr2_ref, li2_ref, br_ref, bi_ref, ccat_ref, ca_ref, cb_ref, rep_ref,
                  tt_ref, wt_ref, zt_ref, a1_ref, a2_ref, a1s_ref, a2s_ref):
    ng = ls_ref.shape[0]
    w = q * S5_CH
    step = jnp.exp(ls_ref[...])
    lr, li = lrp_ref[...], lip_ref[...]
    mag = jnp.exp(lr * step)
    ang = li * step
    lbr = mag * jnp.cos(ang)
    lbi = mag * jnp.sin(ang)
    den = lr * lr + li * li
    kr = ((lbr - 1.0) * lr + lbi * li) / den
    ki = (lbi * lr - (lbr - 1.0) * li) / den
    br, bi = br_ref[...], bi_ref[...]
    bbr = kr * br - ki * bi
    bbi = kr * bi + ki * br
    if q > 1:
        np_ = br.shape[1]
        rep = lambda v: jnp.dot(v.reshape(ng * np_, S5_CH), rep_ref[...], precision=lax.Precision.HIGHEST,
                                preferred_element_type=F32).reshape(ng, np_, w)
        bbr, bbi = rep(bbr), rep(bbi)
    if q == 1:
        ball = jnp.concatenate([bbr, bbi], axis=1)
    else:
        d = lax.broadcasted_iota(jnp.int32, (1, 1, w), 2) // S5_CH
        fr, fi = lbr, lbi
        pr = pi = None
        for b in range((q - 1).bit_length()):
            bit = ((d >> b) & 1) == 1
            sr, si = jnp.where(bit, fr, 1.0), jnp.where(bit, fi, 0.0)
            pr, pi = (sr, si) if pr is None else (pr * sr - pi * si, pr * si + pi * sr)
            fr, fi = fr * fr - fi * fi, 2.0 * (fr * fi)
        ball = jnp.concatenate([pr * bbr - pi * bbi, pr * bbi + pi * bbr], axis=1)
    wt_ref[...] = ball.astype(BF16)

    lane3 = lax.broadcasted_iota(jnp.int32, (1, 1, 2 * S5_STATE), 2)
    first = lane3 < S5_STATE
    csign = jnp.where(first, ccat_ref[...], -ccat_ref[...])
    ch, cl = _split_bf16(csign)
    bh, bl = _split_bf16(ball)
    bdot = lambda x, y: lax.dot_general(x, y, (((2,), (1,)), ((0,), (0,))), preferred_element_type=F32)
    kall = bdot(ch, bh) + bdot(ch, bl) + bdot(cl, bh)
    k2 = kall.reshape(ng * S5_CH, w)
    lane2 = lax.broadcasted_iota(jnp.int32, (ng * S5_CH, w), 1)
    for t in range(q):
        sh = (q - 1 - t) * S5_CH
        r = pltpu.roll(k2, sh, 1) if sh else k2
        r = jnp.where(lane2 >= sh, r, 0.0)
        tt_ref[:, t * S5_CH:(t + 1) * S5_CH, :] = r.reshape(ng, S5_CH, w).astype(BF16)

    lr2, li2 = lr2_ref[...], li2_ref[...]
    t1 = (lax.broadcasted_iota(jnp.int32, (1, q, 1), 1) + 1).astype(F32)
    zm = jnp.exp(t1 * (lr2 * step))
    za = t1 * (li2 * step)
    zr = zm * jnp.cos(za)
    zi = zm * jnp.sin(za)
    ca, cb = ca_ref[...], cb_ref[...]
    for t in range(q):
        prt = zr[:, t:t + 1, :]
        pit = zi[:, t:t + 1, :]
        zt = jnp.where(first, ca * prt - cb * pit, -(ca * pit) - cb * prt)
        zt_ref[:, t * S5_CH:(t + 1) * S5_CH, :] = zt.astype(BF16)
    qf = float(q)
    mq = jnp.exp(qf * (lr2 * step))
    aq = qf * (li2 * step)
    ar = mq * jnp.cos(aq)
    ai = mq * jnp.sin(aq)
    a1_ref[...] = ar
    a2_ref[...] = jnp.where(first, -ai, ai)
    a1s_ref[...] = zr[:, 0:1, :]
    a2s_ref[...] = jnp.where(first, -zi[:, 0:1, :], zi[:, 0:1, :])


def _s5_prep(q, lam_re, lam_im, log_step, b_re, b_im, c_re, c_im):
    ng, p = lam_re.shape
    w = q * S5_CH
    ls = log_step.reshape(ng, 1, 1)
    lrp = lam_re.reshape(ng, p, 1)
    lip = lam_im.reshape(ng, p, 1)
    lr2 = jnp.concatenate([lam_re, lam_re], axis=-1).reshape(ng, 1, 2 * p)
    li2 = jnp.concatenate([lam_im, lam_im], axis=-1).reshape(ng, 1, 2 * p)
    rep = (jnp.arange(S5_CH)[:, None] == jnp.arange(w)[None, :] % S5_CH).astype(F32)
    ccat = jnp.concatenate([c_re, c_im], axis=-1)
    ca = jnp.concatenate([c_re, c_re], axis=-1)
    cb = jnp.concatenate([c_im, c_im], axis=-1)
    gb = ng if q == 1 else 16
    blk = lambda a, b: pl.BlockSpec((gb, a, b), lambda i: (i, 0, 0))
    return pl.pallas_call(
        functools.partial(_s5_prep_body, q),
        grid=(ng // gb,),
        in_specs=[blk(1, 1), blk(p, 1), blk(p, 1), blk(1, 2 * p), blk(1, 2 * p), blk(p, S5_CH), blk(p, S5_CH),
                  blk(S5_CH, 2 * p), blk(S5_CH, 2 * p), blk(S5_CH, 2 * p), _const_spec(rep.shape)],
        out_specs=[blk(w, w), blk(2 * p, w), blk(w, 2 * p)] + [blk(1, 2 * p)] * 4,
        out_shape=[jax.ShapeDtypeStruct((ng, w, w), BF16), jax.ShapeDtypeStruct((ng, 2 * p, w), BF16),
                   jax.ShapeDtypeStruct((ng, w, 2 * p), BF16)] + [jax.ShapeDtypeStruct((ng, 1, 2 * p), F32)] * 4,
        compiler_params=_params("arbitrary"),
        name=f"s5_prep_q{q}",
    )(ls, lrp, lip, lr2, li2, b_re, b_im, ccat, ca, cb, rep)


def _s5_single_token_mats(q, tt, wt, zt):
    c = S5_CH
    return tt[:, :c, (q - 1) * c:], wt[:, :, :c], zt[:, :c, :]


def _s5_mix_body(q, nb, nkb, npi, npt, perm, x_ref, g_ref, wu_ref, tt_ref, wt_ref, zt_ref, a1_ref, a2_ref, s0_ref,
                 dcol_ref, wglut_ref, bglu_ref, wo_ref, p_ref, sfin_ref, ut, yt, carry, wut):
    ng = tt_ref.shape[0]
    nk = nkb // nb
    qs = q // npt
    ch = qs * nkb
    tile = pl.program_id(0)
    ph = pl.program_id(1)

    @pl.when(jnp.logical_and(tile == 0, ph == 0))
    def _():
        if perm:
            carry[...] = s0_ref[...]
        else:
            lane_s = lax.broadcasted_iota(jnp.int32, (nb, 2 * S5_STATE), 1)
            for pr in range(ng // 2):
                cols = slice(pr * 2 * S5_STATE, (pr + 1) * 2 * S5_STATE)
                re_t, im_t = s0_ref[0, :, cols], s0_ref[1, :, cols]
                carry[2 * pr] = jnp.where(lane_s < S5_STATE, re_t, pltpu.roll(im_t, S5_STATE, 1))
                carry[2 * pr + 1] = jnp.where(lane_s < S5_STATE, pltpu.roll(re_t, S5_STATE, 1), im_t)
        cw_ = 256
        for c in range(0, wu_ref.shape[1], cw_):
            wut[c:c + cw_, :] = wu_ref[:, c:c + cw_].T

    def rows_of(r):
        if not perm:
            return x_ref[r]
        nseq, nblk, ndt, nr, _ = x_ref.shape
        flat = x_ref.reshape(nseq * nblk * ndt * nr, LANES)

        return jnp.concatenate(
            [jnp.concatenate([flat[pl.ds((k * ndt + dt) * nr + r, nseq, stride=nblk * ndt * nr), :]
                              for dt in range(ndt)], axis=1) for k in range(nblk)], axis=0)

    def project(j):
        step = 2 if qs % 2 == 0 else 1
        r0 = (j * qs) % (q // npi)
        for r in range(0, qs, step):
            xin = jnp.concatenate([rows_of(r0 + r + s) for s in range(step)], axis=0)
            u = _dot_nt(wut[...], _rms(xin, g_ref[...]).astype(BF16))
            sl = slice((j * qs + r) * nkb, (j * qs + r + step) * nkb)
            ut[:, sl] = u.astype(BF16)
            yt[:, sl] = dcol_ref[...] * u

    def finish(j):
        gt = _gelu_tanh(yt[:, j * ch:(j + 1) * ch])
        gate = _dot(wglut_ref[...], gt.astype(BF16)) + bglu_ref[...]
        y5 = (gt * _sigmoid(gate)).astype(BF16)
        p_ref[...] = _dot_tn(y5, wo_ref[...]).reshape(p_ref.shape)

    spp = npt // npi
    for hh in range(npi):
        @pl.when(ph == hh)
        def _():
            for j in range(hh * spp, (hh + 1) * spp):
                project(j)

    @pl.when(ph == npi - 1)
    def _():
        gu = S5_GROUP_BATCH
        bdot = lambda a, b: lax.dot_general(a, b, (((2,), (1,)), ((0,), (0,))), preferred_element_type=F32)
        bdot_nt = lambda a, b: lax.dot_general(a, b, (((2,), (2,)), ((0,), (0,))), preferred_element_type=F32)

        def groups(i, _):
            g0 = pl.multiple_of(i * gu, gu)
            r0 = pl.multiple_of(i * (gu * S5_CH), gu * S5_CH)
            gsl = pl.ds(g0, gu)
            rows = ut[pl.ds(r0, gu * S5_CH), :].reshape(gu, S5_CH, q * nkb)
            ugt = jnp.concatenate([rows[:, :, (q - 1 - j) * nkb:(q - j) * nkb] for j in range(q)], axis=1)
            y = bdot(tt_ref[gsl], ugt)
            vt = bdot(wt_ref[gsl], ugt)
            v = jnp.swapaxes(vt, 1, 2)
            v_sw = jnp.swapaxes(jnp.concatenate([vt[:, S5_STATE:], vt[:, :S5_STATE]], axis=1), 1, 2)
            a1 = a1_ref[gsl]
            a2 = a2_ref[gsl]
            s = carry[gsl]
            s_sw = pltpu.roll(s.reshape(gu * nb, 2 * S5_STATE), S5_STATE, 1).reshape(s.shape)
            prev = []
            for k in range(nk):
                prev.append(s)
                s, s_sw = (a1 * s + a2 * s_sw + v[:, k * nb:(k + 1) * nb, :],
                           a1 * s_sw - a2 * s + v_sw[:, k * nb:(k + 1) * nb, :])
            carry[gsl] = s
            sprev = jnp.concatenate(prev, axis=1).astype(BF16)
            y = y + bdot_nt(zt_ref[gsl], sprev)
            for t in range(q):
                yt[pl.ds(r0, gu * S5_CH), t * nkb:(t + 1) * nkb] += (
                    y[:, t * S5_CH:(t + 1) * S5_CH, :].reshape(gu * S5_CH, nkb))
            return 0

        lax.fori_loop(0, ng // gu, groups, 0)
        if perm:
            sfin_ref[...] = carry[...]
        else:
            lane_s = lax.broadcasted_iota(jnp.int32, (nb, 2 * S5_STATE), 1)
            for pr in range(ng // 2):
                cols = slice(pr * 2 * S5_STATE, (pr + 1) * 2 * S5_STATE)
                c0, c1 = carry[2 * pr], carry[2 * pr + 1]
                sfin_ref[0, :, cols] = jnp.where(lane_s < S5_STATE, c0, pltpu.roll(c1, S5_STATE, 1))
                sfin_ref[1, :, cols] = jnp.where(lane_s < S5_STATE, pltpu.roll(c0, S5_STATE, 1), c1)

    for j in range(npt):
        @pl.when(ph == npi + j)
        def _():
            finish(j)


def _s5_mix(q, nb, x, ln_g, wu, tt, wt, zt, a1, a2, s0, dcol, wglut, bglu, wo):
    nseq, l, d = x.shape
    nblk = l // q
    nlt = nblk * nseq
    nkb = min(LANES, nlt)
    ntile = nlt // nkb
    tok = q * nkb
    npt = max(1, tok // TOKEN_TILE)
    dm = wo.shape[1]
    perm = q > 1
    if perm:
        half = 8
        npi = q // half
        assert nb == nseq and d % LANES == 0 and q % half == 0
        xv = x.reshape(nseq, nblk, npi, half, d // LANES, LANES).transpose(0, 1, 2, 4, 3, 5)
        assert npt % npi == 0
        x_spec = pl.BlockSpec((nseq, nkb // nseq, None, d // LANES, half, LANES),
                              lambda i, j: (0, i, jnp.minimum(j, npi - 1), 0, 0, 0))
    else:
        npi = 1
        xv = x.reshape(1, nseq, d)
        x_spec = pl.BlockSpec((1, nkb, d), lambda i, j: (0, i, 0))
    body = functools.partial(_s5_mix_body, q, nb, nkb, npi, npt, perm)
    p_spec = pl.BlockSpec((q // npt, nkb, dm), lambda i, j: (jnp.maximum(j - npi, 0), i, 0))
    return pl.pallas_call(
        body,
        grid=(ntile, npi + npt),
        in_specs=[x_spec, _const_spec(ln_g.shape), _const_spec(wu.shape), _const_spec(tt.shape),
                  _const_spec(wt.shape), _const_spec(zt.shape), _const_spec(a1.shape), _const_spec(a2.shape),
                  _const_spec(s0.shape), _const_spec(dcol.shape), _const_spec(wglut.shape),
                  _const_spec(bglu.shape), _const_spec(wo.shape)],
        out_specs=[p_spec, pl.BlockSpec(s0.shape, lambda i, j: (0, 0, 0))],
        out_shape=[jax.ShapeDtypeStruct((q, nlt, dm), F32), jax.ShapeDtypeStruct(s0.shape, F32)],
        scratch_shapes=[pltpu.VMEM((d, tok), BF16), pltpu.VMEM((d, tok), F32),
                        pltpu.VMEM((tt.shape[0], nb, 2 * S5_STATE), F32),
                        pltpu.VMEM((wu.shape[1], wu.shape[0]), BF16)],
        compiler_params=_params("arbitrary", "arbitrary"),
        name=f"s5_mix_q{q}",
    )(xv, ln_g, wu, tt, wt, zt, a1, a2, s0, dcol, wglut, bglu, wo)


def _kv_body(m_ref, wk_ref, wv_ref, k_ref, v_ref, kb_ref, vb_ref):
    tm, rows, _ = k_ref.shape
    nt = rows // X_HEADS
    m = m_ref[...].astype(BF16)
    for w_ref, o_ref, ob_ref in ((wk_ref, k_ref, kb_ref), (wv_ref, v_ref, vb_ref)):
        r = _dot(m, w_ref[...])
        ob_ref[...] = r.astype(BF16)
        flat = o_ref.reshape(tm * rows, LANES)
        for h in range(X_HEADS):
            for dt in range(nt):
                c = (h * nt + dt) * LANES
                flat[pl.ds(dt * X_HEADS + h, tm, stride=rows), :] = r[:, c:c + LANES]


def _kv_proj(mem2d, wk, wv):
    t, d = mem2d.shape
    tm = min(TOKEN_TILE, t)
    rows = d // LANES
    row = pl.BlockSpec((tm, d), lambda i: (i, 0))
    tile = pl.BlockSpec((tm, rows, LANES), lambda i: (i, 0, 0))
    return pl.pallas_call(
        _kv_body, grid=(t // tm,),
        in_specs=[row, _const_spec(wk.shape), _const_spec(wv.shape)],
        out_specs=[tile, tile, row, row],
        out_shape=[jax.ShapeDtypeStruct((t, rows, LANES), F32)] * 2 + [jax.ShapeDtypeStruct((t, d), BF16)] * 2,
        compiler_params=_params("arbitrary"),
        name="kv_proj",
    )(mem2d, wk, wv)


def _tile_unview_kv(kv8, b, m):
    nt = kv8.shape[1] // X_HEADS
    return kv8.reshape(b, m, nt, X_HEADS, LANES).transpose(0, 1, 3, 2, 4).reshape(b, m, X_HEADS, nt * LANES)


def _attn_prompt_body(x_ref, ys_ref, p_ref, wo1_ref, g2_ref, wq_ref, k_ref, v_ref, wxo_ref, o_ref):
    d = x_ref.shape[-1]
    hd = d // X_HEADS
    x1 = x_ref[...] + _dot(ys_ref[...], wo1_ref[...]) + p_ref[...]
    hq = _rms(x1, g2_ref[...]).astype(BF16)
    qv = _dot(hq, wq_ref[...]).astype(BF16)
    kb = k_ref[...]
    vb = v_ref[...]
    outs = []
    for h in range(X_HEADS):
        sl = slice(h * hd, (h + 1) * hd)
        s = _dot_nt(qv[:, sl], kb[:, sl]) * (hd ** -0.5)
        e = jnp.exp(s - jnp.max(s, axis=-1, keepdims=True))
        p = (e / jnp.sum(e, axis=-1, keepdims=True)).astype(BF16)
        outs.append(_dot(p, vb[:, sl]))
    o = jnp.concatenate(outs, axis=1).astype(BF16)
    o_ref[...] = x1 + _dot(o, wxo_ref[...])


def _attn_prompt(x, ys, pm, wo1, g2, wq, mk, mv, wxo, make_rider):
    b, l, d = x.shape
    nm = mk.shape[1]
    tm = min(TOKEN_TILE, l)
    nl = l // tm
    row = pl.BlockSpec((None, tm, d), lambda i, j: (i, j, 0))
    kv = pl.BlockSpec((None, nm, d), lambda i, j: (i, 0, 0))
    ins = [x, ys, pm, wo1, g2, wq, mk, mv, wxo]
    specs = [row, row, row, _const_spec(wo1.shape), _const_spec(g2.shape), _const_spec(wq.shape), kv, kv,
             _const_spec(wxo.shape)]
    rider = make_rider(lambda i, j: i * nl + j, b * nl)
    body = functools.partial(_two_bodies, _attn_prompt_body, len(ins), 1, rider[0], len(rider[1]))
    return pl.pallas_call(
        body, grid=(b, nl),
        in_specs=specs + list(rider[2]),
        out_specs=[row] + list(rider[3]),
        out_shape=[jax.ShapeDtypeStruct((b, l, d), F32)] + list(rider[4]),
        compiler_params=_params("arbitrary", "arbitrary"),
        name="attn_prompt",
    )(*ins, *rider[1])


def _merge_q_body(x_ref, ys_ref, p_ref, wo1_ref, g2_ref, wq_ref, x1_ref, q_ref):
    x1 = x_ref[...] + _dot(ys_ref[...], wo1_ref[...]) + p_ref[...]
    x1_ref[...] = x1
    q_ref[...] = _dot(_rms(x1, g2_ref[...]).astype(BF16), wq_ref[...])


def _merge_q(x2d, ys, pm, wo1, g2, wq):
    t, d = x2d.shape
    return pl.pallas_call(
        _merge_q_body,
        out_shape=[jax.ShapeDtypeStruct((t, d), F32)] * 2,
        compiler_params=pltpu.CompilerParams(vmem_limit_bytes=VMEM_LIMIT),
        name="merge_q",
    )(x2d, ys, pm, wo1, g2, wq)


def _tile_view(kv):
    b, m, nh, hd = kv.shape
    nt = hd // LANES
    return kv.reshape(b, m, nh, nt, LANES).transpose(0, 1, 3, 2, 4).reshape(b, m, nt * nh, LANES)


def _tile_unview(o):
    b, rows, _ = o.shape
    nt = rows // X_HEADS
    return o.reshape(b, nt, X_HEADS, LANES).transpose(0, 2, 1, 3).reshape(b, rows * LANES)


def _attn_step_body(q_ref, k_ref, v_ref, o_ref):
    bt, nm, rows, _ = k_ref.shape
    hd = rows * LANES // X_HEADS
    for b in range(bt):
        r = jnp.sum(k_ref[b] * q_ref[b], axis=-1, keepdims=True)
        s = (r + pltpu.roll(r, X_HEADS, 1)) * (hd ** -0.5)
        e = jnp.exp(s - jnp.max(s, axis=0, keepdims=True))
        p = e / jnp.sum(e, axis=0, keepdims=True)
        o_ref[b] = jnp.sum(p * v_ref[b], axis=0)


def _attn_step_rider(q8, k8, v8, nsteps):
    nbt, rows, _ = q8.shape
    nm = k8.shape[1]
    assert rows == 2 * X_HEADS, "score assembly assumes two 128-lane tiles per head"
    assert nbt % nsteps == 0
    bt = nbt // nsteps
    qs = pl.BlockSpec((bt, rows, LANES), lambda i: (i, 0, 0))
    kv = pl.BlockSpec((bt, nm, rows, LANES), lambda i: (i, 0, 0, 0))
    return (_attn_step_body, (q8, k8, v8), (qs, kv, kv), (qs,),
            (jax.ShapeDtypeStruct((nbt, rows, LANES), F32),))


def _ffn_body(has_o, final, *refs):
    if has_o:
        x_ref, o_in_ref, wxo_ref, g3_ref, wg_ref, wu_ref, wd_ref, gf_ref, y_ref = refs
        x2 = x_ref[...] + _dot(o_in_ref[...].astype(BF16), wxo_ref[...])
    else:
        x_ref, g3_ref, wg_ref, wu_ref, wd_ref, gf_ref, y_ref = refs
        x2 = x_ref[...]
    hf = _rms(x2, g3_ref[...]).astype(BF16)
    ff = wg_ref.shape[1]
    x3 = x2
    for lo in range(0, ff, FF_CHUNK):
        hi = min(lo + FF_CHUNK, ff)
        act = (_silu(_dot(hf, wg_ref[:, lo:hi])) * _dot(hf, wu_ref[:, lo:hi])).astype(BF16)
        x3 = x3 + _dot(act, wd_ref[lo:hi, :])
    y_ref[...] = _rms(x3, gf_ref[...]) if final else x3


def _ffn(x2d, g3, wg, wu, wd, gf, final, o_in=None, wxo=None, rider=None):
    t, d = x2d.shape
    tm = min(TOKEN_TILE, t)
    row = pl.BlockSpec((tm, d), lambda i: (i, 0))
    has_o = o_in is not None
    ins = [x2d] + ([o_in, wxo] if has_o else []) + [g3, wg, wu, wd, gf]
    specs = [row] + ([row, _const_spec(wxo.shape)] if has_o else []) + [
        _const_spec(g3.shape), _const_spec(wg.shape), _const_spec(wu.shape), _const_spec(wd.shape),
        _const_spec(gf.shape)]
    body = functools.partial(_ffn_body, has_o, final)
    out_specs, out_shape = [row], [jax.ShapeDtypeStruct((t, d), F32)]
    if rider is not None:
        body = functools.partial(_two_bodies, body, len(ins), 1, rider[0], len(rider[1]))
        ins, specs = ins + list(rider[1]), specs + list(rider[2])
        out_specs, out_shape = out_specs + list(rider[3]), out_shape + list(rider[4])
    res = pl.pallas_call(
        body, grid=(t // tm,),
        in_specs=specs, out_specs=out_specs, out_shape=out_shape,
        compiler_params=_params("arbitrary"),
        name="ffn_o" if has_o else "ffn",
    )(*ins)
    return res[0] if rider is None else res


def _two_bodies(body_a, n_in_a, n_out_a, body_b, n_in_b, *refs):
    outs = refs[n_in_a + n_in_b:]
    body_a(*refs[:n_in_a], *outs[:n_out_a])
    body_b(*refs[n_in_a:n_in_a + n_in_b], *outs[n_out_a:])


def _merge_riders(ra, rb):
    body = functools.partial(_two_bodies, ra[0], len(ra[1]), len(ra[3]), rb[0], len(rb[1]))
    return (body,) + tuple(tuple(ra[k]) + tuple(rb[k]) for k in range(1, 5))


def _cast_body(*refs):
    n = len(refs) // 2
    for src, dst in zip(refs[:n], refs[n:]):
        dst[...] = src[...].astype(dst.dtype)


def _cast_rider(ws, step_of, nsteps):
    bf16_rows = 16
    specs, shapes = [], []
    for w in ws:
        r, c = w.shape
        k = next(k for k in (1, 2, 4, 8) if r % (nsteps // k) == 0 and (r // (nsteps // k)) % bf16_rows == 0)
        specs.append(pl.BlockSpec((r // (nsteps // k), c), lambda *g, k=k: (step_of(*g) // k, 0)))
        shapes.append(jax.ShapeDtypeStruct((r, c), BF16))
    return (_cast_body, tuple(ws), tuple(specs), tuple(specs), tuple(shapes))


def _layer_weights(i, ln1_g, w_in, conv_w, conv_b, dt_bias, a_log, d_ssd, gn_g, lam_re, lam_im, log_step,
                   b_re, b_im, c_re, c_im, d_s5, w_glu, b_glu, w_out, ln2_g, w_xq, w_xk, w_xv, w_xo, ln3_g,
                   w_gate, w_up, w_down):
    d = w_in.shape[1]
    n_heads = dt_bias.shape[1]
    dssd = n_heads * SSD_HEAD_DIM
    conv_dim = conv_w.shape[2]
    o1, o2, o3 = dssd, dssd + conv_dim, dssd + conv_dim + n_heads
    win = w_in[i]
    w = {}
    w["ln1"] = ln1_g[i].reshape(1, d)
    w["win"] = win.astype(BF16)
    w["dims"] = (dssd, conv_dim)
    w["wu_s5"] = w["win"][:, o3:]
    w["cw"] = conv_w[i]
    w["cb"] = conv_b[i].reshape(1, conv_dim)
    w["dtb"] = jnp.pad(dt_bias[i], (0, LANES - n_heads)).reshape(1, LANES)
    w["alog"] = jnp.pad(a_log[i], (0, LANES - n_heads)).reshape(1, LANES)
    w["dexp"] = jnp.repeat(d_ssd[i], SSD_HEAD_DIM).reshape(1, dssd)
    w["gn"] = gn_g[i].reshape(1, dssd)
    w["s5"] = (lam_re[i], lam_im[i], log_step[i], b_re[i], b_im[i], c_re[i], c_im[i])
    ds5 = d_s5.shape[1]
    w["dcol"] = d_s5[i].reshape(ds5, 1)
    w["wglut"] = w_glu[i].T.astype(BF16)
    w["bglu"] = b_glu[i].reshape(ds5, 1)
    w["wo1"] = w_out[i, :dssd].astype(BF16)
    w["wo2"] = w_out[i, dssd:].astype(BF16)
    w["ln2"] = ln2_g[i].reshape(1, d)
    w["wq"] = w_xq[i].astype(BF16)
    w["wk"] = w_xk[i].astype(BF16)
    w["wv"] = w_xv[i].astype(BF16)
    w["wxo"] = w_xo[i].astype(BF16)
    w["ln3"] = ln3_g[i].reshape(1, d)
    w["ffn_f32"] = (w_gate[i], w_up[i], w_down[i])
    return w


def _rows_to_state(s):
    p = s.shape[-1] // 2
    st = s.transpose(1, 0, 2)
    return st[..., :p], st[..., p:]


def kernel(x_prompt, x_sample, mem_prompt, state_conv, state_ssm, state_s5_re, state_s5_im, cache_mem_k, cache_mem_v, ln1_g, w_in, conv_w, conv_b, dt_bias, a_log, d_ssd, gn_g, lam_re, lam_im, log_step, b_re, b_im, c_re, c_im, d_s5, w_glu, b_glu, w_out, ln2_g, w_xq, w_xk, w_xv, w_xo, ln3_g, w_gate, w_up, w_down, final_g):
    bp, seq, d = x_prompt.shape
    bs = x_sample.shape[0]
    depth = w_in.shape[0]
    nm = mem_prompt.shape[1]
    ng, ns = lam_re.shape[1], lam_re.shape[2]
    gf = final_g.reshape(1, d)
    hp = x_prompt
    hs = x_sample.reshape(bs, d)
    outs = {k: [] for k in ("conv_p", "ssm_p", "re_p", "im_p", "mk_p", "mv_p", "conv_s", "ssm_s", "re_s", "im_s")}
    yp = ys_out = None
    for i in range(depth):
        w = _layer_weights(i, ln1_g, w_in, conv_w, conv_b, dt_bias, a_log, d_ssd, gn_g, lam_re, lam_im, log_step,
                           b_re, b_im, c_re, c_im, d_s5, w_glu, b_glu, w_out, ln2_g, w_xq, w_xk, w_xv, w_xo, ln3_g,
                           w_gate, w_up, w_down)
        last = i == depth - 1
        mk8, mv8, mk, mv = _kv_proj(mem_prompt.reshape(bp * nm, d), w["wk"], w["wv"])
        mk = mk.reshape(bp, nm, d)
        mv = mv.reshape(bp, nm, d)
        y_ssd, conv_p, ssm_p = _ssd_prompt(hp, w["ln1"], w["win"], w["cw"], w["cb"], w["dtb"], w["alog"], w["dexp"],
                                           w["gn"])
        q = S5_Q
        tt, wt, zt, a1, a2, a11, a21 = _s5_prep(q, *w["s5"])
        s0 = jnp.zeros((ng, bp, 2 * ns), F32)
        pperm, sfin = _s5_mix(q, bp, hp, w["ln1"], w["wu_s5"],tt, wt, zt, a1, a2, s0, w["dcol"], w["wglut"],
                              w["bglu"], w["wo2"])
        pm = pperm.reshape(q, seq // q, bp, d).transpose(2, 1, 0, 3).reshape(bp, seq, d)
        re_p, im_p = _rows_to_state(sfin)
        attn_steps = bp * (seq // min(TOKEN_TILE, seq))
        zs, xbcs, dts = _inproj(hs, w["ln1"], w["win"], *w["dims"], bs // attn_steps)
        riders = lambda step_of, n: _merge_riders(
            _ssd_step_rider(zs, xbcs, dts, state_conv[i], state_ssm[i], w["cw"], w["cb"], w["dtb"], w["alog"],
                            w["dexp"], w["gn"], step_of, n),
            _cast_rider(w["ffn_f32"], step_of, n))
        x2, ys_ssd, conv_s, ssm_s, w["wg"], w["wu"], w["wdn"] = _attn_prompt(
            hp, y_ssd, pm, w["wo1"], w["ln2"], w["wq"], mk, mv, w["wxo"], riders)
        ys_ssd = ys_ssd.reshape(bs, -1)
        tt1, wt1, zt1 = _s5_single_token_mats(q, tt, wt, zt)
        s0s = jnp.stack([state_s5_re[i].reshape(bs, ng * ns), state_s5_im[i].reshape(bs, ng * ns)])
        ps, sfin_s = _s5_mix(1, bs, hs.reshape(bs, 1, d), w["ln1"], w["wu_s5"],tt1, wt1, zt1, a11, a21, s0s,
                             w["dcol"], w["wglut"], w["bglu"], w["wo2"])
        re_s, im_s = sfin_s[0].reshape(bs, ng, ns), sfin_s[1].reshape(bs, ng, ns)
        x1s, qs = _merge_q(hs, ys_ssd, ps.reshape(bs, d), w["wo1"], w["ln2"], w["wq"])
        attn_rider = _attn_step_rider(_tile_view(qs.reshape(bs, 1, X_HEADS, d // X_HEADS))[:, 0],
                                      _tile_view(cache_mem_k[i]), _tile_view(cache_mem_v[i]),
                                      (bp * seq) // min(TOKEN_TILE, bp * seq))
        hp, o_s = _ffn(x2.reshape(bp * seq, d), w["ln3"], w["wg"], w["wu"], w["wdn"], gf, last, rider=attn_rider)
        hp = hp.reshape(bp, seq, d)
        hs = _ffn(x1s, w["ln3"], w["wg"], w["wu"], w["wdn"], gf, last, o_in=_tile_unview(o_s), wxo=w["wxo"])
        for k, v in (("conv_p", conv_p), ("ssm_p", ssm_p), ("re_p", re_p), ("im_p", im_p),
                     ("mk_p", _tile_unview_kv(mk8, bp, nm)), ("mv_p", _tile_unview_kv(mv8, bp, nm)),
                     ("conv_s", conv_s), ("ssm_s", ssm_s), ("re_s", re_s), ("im_s", im_s)):
            outs[k].append(v)
    st = lambda k: jnp.stack(outs[k])
    return (hp, hs.reshape(bs, 1, d), st("conv_p"), st("ssm_p"), st("re_p"), st("im_p"), st("mk_p"), st("mv_p"),
            st("conv_s"), st("ssm_s"), st("re_s"), st("im_s"))
```

```python
import functools
import math

import jax
import jax.numpy as jnp
from jax import lax
from jax.experimental import pallas as pl
from jax.experimental.pallas import tpu as pltpu

F32 = jnp.float32
BF16 = jnp.bfloat16
EPS = 1e-6

LANES = 128
VMEM_LIMIT = 56 * 1024 * 1024

SSD_HEAD_DIM = 64
SSD_STATE = 128
SSD_GROUPS = 2
SSD_CHUNK = 128
CONV_W = 4
S5_CH = 16
S5_STATE = 64
S5_Q = 16
S5_GROUP_BATCH = 16
S5_SLICE_TOKENS = 512
X_HEADS = 4
TOKEN_TILE = 512
PROJ_PIECE = 256
SSD_SEQS_PER_STEP = 2
FF_CHUNK = 1024


def _const_spec(shape):
    nd = len(shape)
    return pl.BlockSpec(shape, lambda *_: (0,) * nd, pipeline_mode=pl.Buffered(1))


def _params(*sem):
    return pltpu.CompilerParams(dimension_semantics=sem, vmem_limit_bytes=VMEM_LIMIT)


def _rms(x, g):
    return x * lax.rsqrt(jnp.mean(x * x, axis=-1, keepdims=True) + EPS) * g


def _sigmoid(x):
    return 1.0 / (1.0 + jnp.exp(-x))


def _silu(x):
    return x * _sigmoid(x)


def _gelu_tanh(x):
    return 0.5 * x * (1.0 + jnp.tanh(math.sqrt(2.0 / math.pi) * (x + 0.044715 * (x * x * x))))


def _softplus(x):
    return jnp.maximum(x, 0.0) + jnp.log1p(jnp.exp(-jnp.abs(x)))


def _dot(a, b):
    return jnp.dot(a, b, preferred_element_type=F32)


def _dot_nt(a, b):
    return lax.dot_general(a, b, (((1,), (1,)), ((), ())), preferred_element_type=F32)


def _dot_tn(a, b):
    return lax.dot_general(a, b, (((0,), (0,)), ((), ())), preferred_element_type=F32)


def _proj_windows(d_ssd, conv_dim):
    assert d_ssd % LANES == 0 and conv_dim % LANES == 0
    return (0, d_ssd), (d_ssd, d_ssd + conv_dim), (d_ssd + conv_dim, d_ssd + conv_dim + LANES)


def _inproj_body(wins, x_ref, g_ref, w_ref, z_ref, xbc_ref, dt_ref):
    h = _rms(x_ref[...], g_ref[...]).astype(BF16)
    for (lo, hi), o_ref in zip(wins, (z_ref, xbc_ref, dt_ref)):
        o_ref[...] = _dot(h, w_ref[:, lo:hi]).reshape(o_ref.shape)


def _inproj(x2d, ln_g, w_in, d_ssd, conv_dim, group):
    t, d = x2d.shape
    wins = _proj_windows(d_ssd, conv_dim)
    return pl.pallas_call(
        functools.partial(_inproj_body, wins),
        out_shape=[jax.ShapeDtypeStruct((t // group, group, hi - lo), F32) for lo, hi in wins],
        compiler_params=pltpu.CompilerParams(vmem_limit_bytes=VMEM_LIMIT),
        name="inproj",
    )(x2d, ln_g, w_in)


def _expand_heads(v, n_heads):
    rows = v.shape[0]
    lane = lax.broadcasted_iota(jnp.int32, (rows, LANES), 1)
    pieces = [jnp.where(lane < SSD_HEAD_DIM, v[:, 2 * j:2 * j + 1], v[:, 2 * j + 1:2 * j + 2])
              for j in range(n_heads // 2)]
    return jnp.concatenate(pieces, axis=1)


def _ssd_chunk_math(z, x, dt_raw, last_chunk, cw_ref, cb_ref, dtb_ref, alog_ref, dexp_ref, gn_ref, e2_ref,
                    y_ref, conv_ref, ssm_ref, xpad, state, side_jobs=()):
    jobs = list(side_jobs)

    def side(n=1):
        for _ in range(n):
            if jobs:
                jobs.pop(0)()

    q = SSD_CHUNK
    d_ssd = z.shape[-1]
    n_heads = d_ssd // SSD_HEAD_DIM
    hpg = n_heads // SSD_GROUPS
    gw = hpg * SSD_HEAD_DIM
    nct = xpad.shape[0]
    xt = d_ssd // LANES

    for t in range(nct):
        xpad[t, 8:8 + q, :] = x[:, t * LANES:(t + 1) * LANES]
    cw = cw_ref[...]
    conv = (cw[3] * xpad[:, 8:8 + q, :] + cw[2] * xpad[:, 7:7 + q, :] + cw[1] * xpad[:, 6:6 + q, :]
            + cw[0] * xpad[:, 5:5 + q, :] + cb_ref[...])
    xpad[:, 0:8, :] = xpad[:, q:q + 8, :]
    xact = _silu(conv)
    xs = jnp.concatenate([xact[t] for t in range(xt)], axis=1)
    bm = xact[xt:xt + SSD_GROUPS].astype(BF16)
    cm = xact[xt + SSD_GROUPS:].astype(BF16)
    side()

    dt = _softplus(dt_raw + dtb_ref[...])
    a = -jnp.exp(alog_ref[...])
    row = lax.broadcasted_iota(jnp.int32, (q, q), 0)
    col = lax.broadcasted_iota(jnp.int32, (q, q), 1)
    causal = row >= col
    tri = jnp.where(causal, 1.0, 0.0).astype(F32)
    acum = jnp.dot(tri, dt * a, precision=lax.Precision.HIGHEST, preferred_element_type=F32)
    acum_t = acum.T
    last = acum[q - 1:q, :]

    def expand(v):
        hi = v.astype(BF16)
        lo = (v - hi.astype(F32)).astype(BF16)
        return _dot(jnp.concatenate([hi, lo], axis=1), e2_ref[...])

    dt_e = expand(dt)
    ea_e = expand(jnp.exp(acum))
    dend_e = expand(jnp.exp(last - acum))

    dtx = xs * dt_e
    dtx_b = dtx.astype(BF16)
    xdec_b = (dtx * dend_e).astype(BF16)
    side()

    lane = lax.broadcasted_iota(jnp.int32, (q, LANES), 1)
    zero_b = jnp.zeros((q, LANES), BF16)
    y_tiles = []
    for g in range(SSD_GROUPS):
        bg = bm[g]
        cg = cm[g]
        cb = _dot_nt(cg, bg)
        gs = slice(g * gw, (g + 1) * gw)
        st_g = state[:, gs]
        y_off = _dot(cg, st_g.astype(BF16)) * ea_e[:, gs]
        for pr in range(hpg // 2):
            tile = g * (hpg // 2) + pr
            ms = []
            for h in (2 * tile, 2 * tile + 1):
                seg = acum[:, h:h + 1] - acum_t[h:h + 1, :]
                lmat = jnp.exp(jnp.where(causal, seg, -jnp.inf))
                ms.append((cb * lmat).astype(BF16))
            dtile = dtx_b[:, tile * LANES:(tile + 1) * LANES]
            rhs = jnp.concatenate([jnp.where(lane < SSD_HEAD_DIM, dtile, zero_b),
                                   jnp.where(lane >= SSD_HEAD_DIM, dtile, zero_b)], axis=0)
            y_tiles.append(_dot(jnp.concatenate(ms, axis=1), rhs)
                           + y_off[:, pr * LANES:(pr + 1) * LANES])
            if pr % 2 == 1:
                side()
        new = _dot_tn(bg, xdec_b[:, gs])
        state[:, gs] = st_g * ea_e[q - 1:q, gs] + new
    y = jnp.concatenate(y_tiles, axis=1) + dexp_ref[...] * xs
    y = y * _silu(z)
    ms = jnp.mean(y * y, axis=-1, keepdims=True)
    side(len(jobs))
    y_ref[...] = (y * lax.rsqrt(ms + EPS) * gn_ref[...]).astype(y_ref.dtype)

    @pl.when(last_chunk)
    def _():
        conv_ref[...] = x[q - (CONV_W - 1):q, :]
        ssm_ref[...] = state[...].T.reshape(ssm_ref.shape)


def _ssd_prompt_body(wins, x_ref, g_ref, w_ref, cw_ref, cb_ref, dtb_ref, alog_ref, dexp_ref, gn_ref,
                     e2_ref, y_ref, conv_ref, ssm_ref, za, xa, da, zb, xb, db, xpad, state):
    i = pl.program_id(0)
    j = pl.program_id(1)
    nc = pl.num_programs(1) - 1
    ns, q, d = x_ref.shape

    @pl.when(jnp.logical_and(i == 0, j == 0))
    def _():
        for r in (zb, xb, db):
            r[...] = jnp.zeros(r.shape, F32)

    @pl.when(j <= 1)
    def _():
        xpad[:, :, 0:8, :] = jnp.zeros(xpad.shape[:2] + (8, LANES), F32)
        state[...] = jnp.zeros(state.shape, F32)

    def step(wr, rd):
        xin = x_ref[...].reshape(ns * q, d)
        msx = jnp.mean(xin * xin, axis=-1, keepdims=True)
        hbox = []

        def proj_piece(dst, base, lo, hi):
            def run():
                if not hbox:
                    hbox.append((xin * lax.rsqrt(msx + EPS) * g_ref[...]).astype(BF16))
                dst[:, lo:hi] = _dot(hbox[0], w_ref[:, base + lo:base + hi])
            return run

        pieces = []
        for dst, (w_lo, w_hi) in ((wr[0], wins[0]), (wr[2], wins[2]), (wr[1], wins[1])):
            n = w_hi - w_lo
            pieces += [proj_piece(dst, w_lo, lo, min(lo + PROJ_PIECE, n)) for lo in range(0, n, PROJ_PIECE)]
        per = -(-len(pieces) // ns)
        for s in range(ns):
            rows = slice(s * q, (s + 1) * q)
            _ssd_chunk_math(rd[0][rows, :], rd[1][rows, :], rd[2][rows, :], j == nc, cw_ref, cb_ref, dtb_ref,
                            alog_ref, dexp_ref, gn_ref, e2_ref, y_ref.at[s], conv_ref.at[s], ssm_ref.at[s],
                            xpad.at[s], state.at[s], pieces[s * per:(s + 1) * per])

    @pl.when(j % 2 == 0)
    def _():
        step((za, xa, da), (zb, xb, db))

    @pl.when(j % 2 == 1)
    def _():
        step((zb, xb, db), (za, xa, da))


def _ssd_prompt(x, ln_g, w_in, cw, cb, dtb, alog, dexp, gn):
    b, l, d = x.shape
    conv_dim = cw.shape[1]
    d_ssd = conv_dim - 2 * SSD_GROUPS * SSD_STATE
    wins = _proj_windows(d_ssd, conv_dim)
    n_heads = d_ssd // SSD_HEAD_DIM
    q = SSD_CHUNK
    nc = l // q
    assert SSD_STATE == LANES and conv_dim == d_ssd + 2 * SSD_GROUPS * SSD_STATE
    nct = conv_dim // LANES
    cw4 = cw.reshape(CONV_W, nct, 1, LANES)
    cb3 = cb.reshape(nct, 1, LANES)
    e2 = (jnp.arange(2 * LANES)[:, None] % LANES == jnp.arange(d_ssd)[None, :] // SSD_HEAD_DIM).astype(BF16)
    ns = SSD_SEQS_PER_STEP if b % SSD_SEQS_PER_STEP == 0 else 1
    slot = lambda n: pltpu.VMEM((ns * q, n), F32)
    return pl.pallas_call(
        functools.partial(_ssd_prompt_body, wins),
        grid=(b // ns, nc + 1),
        in_specs=[pl.BlockSpec((ns, q, d), lambda i, j: (i, jnp.minimum(j, nc - 1), 0)),
                  _const_spec(ln_g.shape), _const_spec(w_in.shape),
                  _const_spec(cw4.shape), _const_spec(cb3.shape), _const_spec(dtb.shape), _const_spec(alog.shape),
                  _const_spec(dexp.shape), _const_spec(gn.shape), _const_spec(e2.shape)],
        out_specs=[pl.BlockSpec((ns, q, d_ssd), lambda i, j: (i, jnp.maximum(j - 1, 0), 0)),
                   pl.BlockSpec((ns, CONV_W - 1, conv_dim), lambda i, j: (i, 0, 0)),
                   pl.BlockSpec((ns, n_heads, SSD_HEAD_DIM, SSD_STATE), lambda i, j: (i, 0, 0, 0))],
        out_shape=[jax.ShapeDtypeStruct((b, l, d_ssd), BF16),
                   jax.ShapeDtypeStruct((b, CONV_W - 1, conv_dim), F32),
                   jax.ShapeDtypeStruct((b, n_heads, SSD_HEAD_DIM, SSD_STATE), F32)],
        scratch_shapes=[slot(d_ssd), slot(conv_dim), slot(LANES), slot(d_ssd), slot(conv_dim), slot(LANES),
                        pltpu.VMEM((ns, nct, q + 8, LANES), F32), pltpu.VMEM((ns, SSD_STATE, d_ssd), F32)],
        compiler_params=_params("arbitrary", "arbitrary"),
        name="ssd_prompt",
    )(x, ln_g, w_in, cw4, cb3, dtb, alog, dexp, gn, e2)


def _ssd_step_body(z_ref, xbc_ref, dt_ref, cs_ref, st_ref, cw_ref, cb_ref, dtb_ref, alog_ref, dexp_ref, gn_ref,
                   y_ref, conv_ref, ssm_ref):
    bt, d_ssd = z_ref.shape
    n_heads = d_ssd // SSD_HEAD_DIM
    hpg = n_heads // SSD_GROUPS
    x = xbc_ref[...]
    cw = cw_ref[...]
    cs = [cs_ref[:, k, :] for k in range(CONV_W - 1)]
    conv = cw[0:1] * cs[0] + cw[1:2] * cs[1] + cw[2:3] * cs[2] + cw[3:4] * x + cb_ref[...]
    conv_ref[:, 0, :] = cs[1]
    conv_ref[:, 1, :] = cs[2]
    conv_ref[:, 2, :] = x
    xact = _silu(conv)
    xs = xact[:, :d_ssd]
    bm = xact[:, d_ssd:d_ssd + SSD_GROUPS * SSD_STATE]
    cm = xact[:, d_ssd + SSD_GROUPS * SSD_STATE:]
    dt = _softplus(dt_ref[...] + dtb_ref[...])
    da = jnp.exp(dt * (-jnp.exp(alog_ref[...])))
    dtx = xs * _expand_heads(dt, n_heads)
    pad = jnp.zeros((LANES - bt, d_ssd), F32)
    to_cols = lambda v: jnp.concatenate([v, pad], axis=0).T
    dtx_t = to_cols(dtx)
    da_t = to_cols(_expand_heads(da, n_heads))
    lane = lax.broadcasted_iota(jnp.int32, (d_ssd, LANES), 1)
    gw = hpg * SSD_HEAD_DIM
    rows_of = lambda v, b: jnp.concatenate(
        [jnp.broadcast_to(v[b:b + 1, g * SSD_STATE:(g + 1) * SSD_STATE], (gw, SSD_STATE)) for g in range(SSD_GROUPS)],
        axis=0)
    y_t = jnp.zeros((d_ssd, LANES), F32)
    for b in range(bt):
        s0 = st_ref[b].reshape(d_ssd, SSD_STATE)
        s_new = s0 * da_t[:, b:b + 1] + dtx_t[:, b:b + 1] * rows_of(bm, b)
        ssm_ref[b] = s_new.reshape(n_heads, SSD_HEAD_DIM, SSD_STATE)
        ycol = jnp.sum(s_new * rows_of(cm, b), axis=1, keepdims=True)
        y_t = jnp.where(lane == b, ycol, y_t)
    y = y_t.T[:bt, :] + dexp_ref[...] * xs
    y = y * _silu(z_ref[...])
    y_ref[...] = _rms(y, gn_ref[...]).astype(y_ref.dtype)


def _ssd_step_rider(z, xbc, dt, conv_state, ssm_state, cw, cb, dtb, alog, dexp, gn, step_of, nsteps):
    _, bt, d_ssd = z.shape
    conv_dim = xbc.shape[-1]
    n_heads = d_ssd // SSD_HEAD_DIM
    assert z.shape[0] == nsteps and conv_state.shape[0] == nsteps * bt
    row = lambda n: pl.BlockSpec((None, bt, n), lambda *g: (step_of(*g), 0, 0))
    cs_spec = pl.BlockSpec((bt, CONV_W - 1, conv_dim), lambda *g: (step_of(*g), 0, 0))
    st_spec = pl.BlockSpec((bt, n_heads, SSD_HEAD_DIM, SSD_STATE), lambda *g: (step_of(*g), 0, 0, 0))
    cst = lambda a: pl.BlockSpec(a.shape, lambda *g: (0,) * a.ndim, pipeline_mode=pl.Buffered(1))
    ins = (z, xbc, dt, conv_state, ssm_state, cw, cb, dtb, alog, dexp, gn)
    specs = (row(d_ssd), row(conv_dim), row(LANES), cs_spec, st_spec, cst(cw), cst(cb), cst(dtb), cst(alog),
             cst(dexp), cst(gn))
    out_shapes = (jax.ShapeDtypeStruct((nsteps, bt, d_ssd), BF16),
                  jax.ShapeDtypeStruct(conv_state.shape, F32),
                  jax.ShapeDtypeStruct(ssm_state.shape, F32))
    return (_ssd_step_body, ins, specs, (row(d_ssd), cs_spec, st_spec), out_shapes)


def _split_bf16(x):
    hi = x.astype(BF16)
    lo = (x - hi.astype(F32)).astype(BF16)
    return hi, lo


def _s5_prep_body(q, ls_ref, lrp_ref, lip_ref, lr2_ref, li2_ref, br_ref, bi_ref, ccat_ref, ca_ref, cb_ref, rep_ref,
                  tt_ref, wt_ref, zt_ref, a1_ref, a2_ref, a1s_ref, a2s_ref):
    ng = ls_ref.shape[0]
    w = q * S5_CH
    step = jnp.exp(ls_ref[...])
    lr, li = lrp_ref[...], lip_ref[...]
    mag = jnp.exp(lr * step)
    ang = li * step
    lbr = mag * jnp.cos(ang)
    lbi = mag * jnp.sin(ang)
    den = lr * lr + li * li
    kr = ((lbr - 1.0) * lr + lbi * li) / den
    ki = (lbi * lr - (lbr - 1.0) * li) / den
    br, bi = br_ref[...], bi_ref[...]
    bbr = kr * br - ki * bi
    bbi = kr * bi + ki * br
    if q > 1:
        np_ = br.shape[1]
        rep = lambda v: jnp.dot(v.reshape(ng * np_, S5_CH), rep_ref[...], precision=lax.Precision.HIGHEST,
                                preferred_element_type=F32).reshape(ng, np_, w)
        bbr, bbi = rep(bbr), rep(bbi)
    if q == 1:
        ball = jnp.concatenate([bbr, bbi], axis=1)
    else:
        d = lax.broadcasted_iota(jnp.int32, (1, 1, w), 2) // S5_CH
        fr, fi = lbr, lbi
        pr = pi = None
        for b in range((q - 1).bit_length()):
            bit = ((d >> b) & 1) == 1
            sr, si = jnp.where(bit, fr, 1.0), jnp.where(bit, fi, 0.0)
            pr, pi = (sr, si) if pr is None else (pr * sr - pi * si, pr * si + pi * sr)
            fr, fi = fr * fr - fi * fi, 2.0 * (fr * fi)
        ball = jnp.concatenate([pr * bbr - pi * bbi, pr * bbi + pi * bbr], axis=1)
    wt_ref[...] = ball.astype(BF16)

    lane3 = lax.broadcasted_iota(jnp.int32, (1, 1, 2 * S5_STATE), 2)
    first = lane3 < S5_STATE
    csign = jnp.where(first, ccat_ref[...], -ccat_ref[...])
    ch, cl = _split_bf16(csign)
    bh, bl = _split_bf16(ball)
    bdot = lambda x, y: lax.dot_general(x, y, (((2,), (1,)), ((0,), (0,))), preferred_element_type=F32)
    kall = bdot(ch, bh) + bdot(ch, bl) + bdot(cl, bh)
    k2 = kall.reshape(ng * S5_CH, w)
    lane2 = lax.broadcasted_iota(jnp.int32, (ng * S5_CH, w), 1)
    for t in range(q):
        sh = (q - 1 - t) * S5_CH
        r = pltpu.roll(k2, sh, 1) if sh else k2
        r = jnp.where(lane2 >= sh, r, 0.0)
        tt_ref[:, t * S5_CH:(t + 1) * S5_CH, :] = r.reshape(ng, S5_CH, w).astype(BF16)

    lr2, li2 = lr2_ref[...], li2_ref[...]
    t1 = (lax.broadcasted_iota(jnp.int32, (1, q, 1), 1) + 1).astype(F32)
    zm = jnp.exp(t1 * (lr2 * step))
    za = t1 * (li2 * step)
    zr = zm * jnp.cos(za)
    zi = zm * jnp.sin(za)
    ca, cb = ca_ref[...], cb_ref[...]
    for t in range(q):
        prt = zr[:, t:t + 1, :]
        pit = zi[:, t:t + 1, :]
        zt = jnp.where(first, ca * prt - cb * pit, -(ca * pit) - cb * prt)
        zt_ref[:, t * S5_CH:(t + 1) * S5_CH, :] = zt.astype(BF16)
    qf = float(q)
    mq = jnp.exp(qf * (lr2 * step))
    aq = qf * (li2 * step)
    ar = mq * jnp.cos(aq)
    ai = mq * jnp.sin(aq)
    a1_ref[...] = ar
    a2_ref[...] = jnp.where(first, -ai, ai)
    a1s_ref[...] = zr[:, 0:1, :]
    a2s_ref[...] = jnp.where(first, -zi[:, 0:1, :], zi[:, 0:1, :])


def _s5_prep(q, lam_re, lam_im, log_step, b_re, b_im, c_re, c_im, make_rider):
    ng, p = lam_re.shape
    w = q * S5_CH
    ls = log_step.reshape(ng, 1, 1)
    lrp = lam_re.reshape(ng, p, 1)
    lip = lam_im.reshape(ng, p, 1)
    lr2 = jnp.concatenate([lam_re, lam_re], axis=-1).reshape(ng, 1, 2 * p)
    li2 = jnp.concatenate([lam_im, lam_im], axis=-1).reshape(ng, 1, 2 * p)
    rep = (jnp.arange(S5_CH)[:, None] == jnp.arange(w)[None, :] % S5_CH).astype(F32)
    ccat = jnp.concatenate([c_re, c_im], axis=-1)
    ca = jnp.concatenate([c_re, c_re], axis=-1)
    cb = jnp.concatenate([c_im, c_im], axis=-1)
    gb = ng if q == 1 else 16
    blk = lambda a, b: pl.BlockSpec((gb, a, b), lambda i: (i, 0, 0))
    ins = (ls, lrp, lip, lr2, li2, b_re, b_im, ccat, ca, cb, rep)
    in_specs = [blk(1, 1), blk(p, 1), blk(p, 1), blk(1, 2 * p), blk(1, 2 * p), blk(p, S5_CH), blk(p, S5_CH),
                blk(S5_CH, 2 * p), blk(S5_CH, 2 * p), blk(S5_CH, 2 * p), _const_spec(rep.shape)]
    out_specs = [blk(w, w), blk(2 * p, w), blk(w, 2 * p)] + [blk(1, 2 * p)] * 4
    out_shape = [jax.ShapeDtypeStruct((ng, w, w), BF16), jax.ShapeDtypeStruct((ng, 2 * p, w), BF16),
                 jax.ShapeDtypeStruct((ng, w, 2 * p), BF16)] + [jax.ShapeDtypeStruct((ng, 1, 2 * p), F32)] * 4
    body = functools.partial(_s5_prep_body, q)
    rider = make_rider(ng // gb)
    body = functools.partial(_two_bodies, body, len(ins), len(out_specs), rider[0], len(rider[1]))
    res = pl.pallas_call(
        body,
        grid=(ng // gb,),
        in_specs=in_specs + list(rider[2]),
        out_specs=out_specs + list(rider[3]),
        out_shape=out_shape + list(rider[4]),
        compiler_params=_params("arbitrary"),
        name=f"s5_prep_q{q}",
    )(*ins, *rider[1])
    return res[:len(out_specs)], res[len(out_specs):]


def _s5_single_token_mats(q, tt, wt, zt):
    c = S5_CH
    return tt[:, :c, (q - 1) * c:], wt[:, :, :c], zt[:, :c, :]


def _s5_mix_body(q, nb, nkb, npi, npt, perm, x_ref, g_ref, wu_ref, tt_ref, wt_ref, zt_ref, a1_ref, a2_ref, s0_ref,
                 dcol_ref, wglut_ref, bglu_ref, wo_ref, p_ref, sfin_ref, ut, yt, carry, wut):
    ng = tt_ref.shape[0]
    nk = nkb // nb
    qs = q // npt
    ch = qs * nkb
    tile = pl.program_id(0)
    ph = pl.program_id(1)

    @pl.when(jnp.logical_and(tile == 0, ph == 0))
    def _():
        if perm:
            carry[...] = s0_ref[...]
        else:
            lane_s = lax.broadcasted_iota(jnp.int32, (nb, 2 * S5_STATE), 1)
            for pr in range(ng // 2):
                cols = slice(pr * 2 * S5_STATE, (pr + 1) * 2 * S5_STATE)
                re_t, im_t = s0_ref[0, :, cols], s0_ref[1, :, cols]
                carry[2 * pr] = jnp.where(lane_s < S5_STATE, re_t, pltpu.roll(im_t, S5_STATE, 1))
                carry[2 * pr + 1] = jnp.where(lane_s < S5_STATE, pltpu.roll(re_t, S5_STATE, 1), im_t)
        cw_ = 256
        for c in range(0, wu_ref.shape[1], cw_):
            wut[c:c + cw_, :] = wu_ref[:, c:c + cw_].T

    def rows_of(r):
        if not perm:
            return x_ref[r]
        nseq, nblk, ndt, nr, _ = x_ref.shape
        flat = x_ref.reshape(nseq * nblk * ndt * nr, LANES)

        return jnp.concatenate(
            [jnp.concatenate([flat[pl.ds((k * ndt + dt) * nr + r, nseq, stride=nblk * ndt * nr), :]
                              for dt in range(ndt)], axis=1) for k in range(nblk)], axis=0)

    def project(j):
        step = 2 if qs % 2 == 0 else 1
        r0 = (j * qs) % (q // npi)
        for r in range(0, qs, step):
            xin = jnp.concatenate([rows_of(r0 + r + s) for s in range(step)], axis=0)
            u = _dot_nt(wut[...], _rms(xin, g_ref[...]).astype(BF16))
            sl = slice((j * qs + r) * nkb, (j * qs + r + step) * nkb)
            ut[:, sl] = u.astype(BF16)
            yt[:, sl] = dcol_ref[...] * u

    def finish(j):
        gt = _gelu_tanh(yt[:, j * ch:(j + 1) * ch])
        gate = _dot(wglut_ref[...], gt.astype(BF16)) + bglu_ref[...]
        y5 = (gt * _sigmoid(gate)).astype(BF16)
        p_ref[...] = _dot_tn(y5, wo_ref[...]).reshape(p_ref.shape)

    spp = npt // npi
    for hh in range(npi):
        @pl.when(ph == hh)
        def _():
            for j in range(hh * spp, (hh + 1) * spp):
                project(j)

    @pl.when(ph == npi - 1)
    def _():
        gu = S5_GROUP_BATCH
        bdot = lambda a, b: lax.dot_general(a, b, (((2,), (1,)), ((0,), (0,))), preferred_element_type=F32)
        bdot_nt = lambda a, b: lax.dot_general(a, b, (((2,), (2,)), ((0,), (0,))), preferred_element_type=F32)

        def groups(i, _):
            g0 = pl.multiple_of(i * gu, gu)
            r0 = pl.multiple_of(i * (gu * S5_CH), gu * S5_CH)
            gsl = pl.ds(g0, gu)
            rows = ut[pl.ds(r0, gu * S5_CH), :].reshape(gu, S5_CH, q * nkb)
            ugt = jnp.concatenate([rows[:, :, (q - 1 - j) * nkb:(q - j) * nkb] for j in range(q)], axis=1)
            y = bdot(tt_ref[gsl], ugt)
            vt = bdot(wt_ref[gsl], ugt)
            v = jnp.swapaxes(vt, 1, 2)
            v_sw = jnp.swapaxes(jnp.concatenate([vt[:, S5_STATE:], vt[:, :S5_STATE]], axis=1), 1, 2)
            a1 = a1_ref[gsl]
            a2 = a2_ref[gsl]
            s = carry[gsl]
            s_sw = pltpu.roll(s.reshape(gu * nb, 2 * S5_STATE), S5_STATE, 1).reshape(s.shape)
            prev = []
            for k in range(nk):
                prev.append(s)
                s, s_sw = (a1 * s + a2 * s_sw + v[:, k * nb:(k + 1) * nb, :],
                           a1 * s_sw - a2 * s + v_sw[:, k * nb:(k + 1) * nb, :])
            carry[gsl] = s
            sprev = jnp.concatenate(prev, axis=1).astype(BF16)
            y = y + bdot_nt(zt_ref[gsl], sprev)
            for t in range(q):
                yt[pl.ds(r0, gu * S5_CH), t * nkb:(t + 1) * nkb] += (
                    y[:, t * S5_CH:(t + 1) * S5_CH, :].reshape(gu * S5_CH, nkb))
            return 0

        lax.fori_loop(0, ng // gu, groups, 0)
        if perm:
            sfin_ref[...] = carry[...]
        else:
            lane_s = lax.broadcasted_iota(jnp.int32, (nb, 2 * S5_STATE), 1)
            for pr in range(ng // 2):
                cols = slice(pr * 2 * S5_STATE, (pr + 1) * 2 * S5_STATE)
                c0, c1 = carry[2 * pr], carry[2 * pr + 1]
                sfin_ref[0, :, cols] = jnp.where(lane_s < S5_STATE, c0, pltpu.roll(c1, S5_STATE, 1))
                sfin_ref[1, :, cols] = jnp.where(lane_s < S5_STATE, pltpu.roll(c0, S5_STATE, 1), c1)

    for j in range(npt):
        @pl.when(ph == npi + j)
        def _():
            finish(j)


def _s5_mix(q, nb, x, ln_g, wu, tt, wt, zt, a1, a2, s0, dcol, wglut, bglu, wo):
    nseq, l, d = x.shape
    nblk = l // q
    nlt = nblk * nseq
    nkb = min(LANES, nlt)
    ntile = nlt // nkb
    tok = q * nkb
    npt = max(1, tok // S5_SLICE_TOKENS)
    dm = wo.shape[1]
    perm = q > 1
    if perm:
        half = 8
        npi = q // half
        assert nb == nseq and d % LANES == 0 and q % half == 0
        xv = x.reshape(nseq, nblk, npi, half, d // LANES, LANES).transpose(0, 1, 2, 4, 3, 5)
        assert npt % npi == 0
        x_spec = pl.BlockSpec((nseq, nkb // nseq, None, d // LANES, half, LANES),
                              lambda i, j: (0, i, jnp.minimum(j, npi - 1), 0, 0, 0))
    else:
        npi = 1
        xv = x.reshape(1, nseq, d)
        x_spec = pl.BlockSpec((1, nkb, d), lambda i, j: (0, i, 0))
    body = functools.partial(_s5_mix_body, q, nb, nkb, npi, npt, perm)
    p_spec = pl.BlockSpec((q // npt, nkb, dm), lambda i, j: (jnp.maximum(j - npi, 0), i, 0))
    return pl.pallas_call(
        body,
        grid=(ntile, npi + npt),
        in_specs=[x_spec, _const_spec(ln_g.shape), _const_spec(wu.shape), _const_spec(tt.shape),
                  _const_spec(wt.shape), _const_spec(zt.shape), _const_spec(a1.shape), _const_spec(a2.shape),
                  _const_spec(s0.shape), _const_spec(dcol.shape), _const_spec(wglut.shape),
                  _const_spec(bglu.shape), _const_spec(wo.shape)],
        out_specs=[p_spec, pl.BlockSpec(s0.shape, lambda i, j: (0, 0, 0))],
        out_shape=[jax.ShapeDtypeStruct((q, nlt, dm), F32), jax.ShapeDtypeStruct(s0.shape, F32)],
        scratch_shapes=[pltpu.VMEM((d, tok), BF16), pltpu.VMEM((d, tok), F32),
                        pltpu.VMEM((tt.shape[0], nb, 2 * S5_STATE), F32),
                        pltpu.VMEM((wu.shape[1], wu.shape[0]), BF16)],
        compiler_params=_params("arbitrary", "arbitrary"),
        name=f"s5_mix_q{q}",
    )(xv, ln_g, wu, tt, wt, zt, a1, a2, s0, dcol, wglut, bglu, wo)


def _kv_body(m_ref, wk_ref, wv_ref, k_ref, v_ref, kb_ref, vb_ref):
    tm, rows, _ = k_ref.shape
    nt = rows // X_HEADS
    m = m_ref[...].astype(BF16)
    for w_ref, o_ref, ob_ref in ((wk_ref, k_ref, kb_ref), (wv_ref, v_ref, vb_ref)):
        r = _dot(m, w_ref[...])
        ob_ref[...] = r.astype(BF16)
        flat = o_ref.reshape(tm * rows, LANES)
        for h in range(X_HEADS):
            for dt in range(nt):
                c = (h * nt + dt) * LANES
                flat[pl.ds(dt * X_HEADS + h, tm, stride=rows), :] = r[:, c:c + LANES]


def _kv_proj_rider(mem2d, wk, wv, nsteps):
    t, d = mem2d.shape
    assert t % nsteps == 0
    tm = t // nsteps
    rows = d // LANES
    row = pl.BlockSpec((tm, d), lambda i: (i, 0))
    tile = pl.BlockSpec((tm, rows, LANES), lambda i: (i, 0, 0))
    return (_kv_body, (mem2d, wk, wv), (row, _const_spec(wk.shape), _const_spec(wv.shape)),
            (tile, tile, row, row),
            (jax.ShapeDtypeStruct((t, rows, LANES), F32),) * 2 + (jax.ShapeDtypeStruct((t, d), BF16),) * 2)


def _tile_unview_kv(kv8, b, m):
    nt = kv8.shape[1] // X_HEADS
    return kv8.reshape(b, m, nt, X_HEADS, LANES).transpose(0, 1, 3, 2, 4).reshape(b, m, X_HEADS, nt * LANES)


def _attn_prompt_body(x_ref, ys_ref, p_ref, wo1_ref, g2_ref, wq_ref, k_ref, v_ref, wxo_ref, o_ref):
    d = x_ref.shape[-1]
    hd = d // X_HEADS
    x1 = x_ref[...] + _dot(ys_ref[...], wo1_ref[...]) + p_ref[...]
    hq = _rms(x1, g2_ref[...]).astype(BF16)
    qv = _dot(hq, wq_ref[...]).astype(BF16)
    kb = k_ref[...]
    vb = v_ref[...]
    outs = []
    for h in range(X_HEADS):
        sl = slice(h * hd, (h + 1) * hd)
        s = _dot_nt(qv[:, sl], kb[:, sl]) * (hd ** -0.5)
        e = jnp.exp(s - jnp.max(s, axis=-1, keepdims=True))
        p = (e / jnp.sum(e, axis=-1, keepdims=True)).astype(BF16)
        outs.append(_dot(p, vb[:, sl]))
    o = jnp.concatenate(outs, axis=1).astype(BF16)
    o_ref[...] = x1 + _dot(o, wxo_ref[...])


def _attn_prompt(x, ys, pm, wo1, g2, wq, mk, mv, wxo, make_rider):
    b, l, d = x.shape
    nm = mk.shape[1]
    tm = min(TOKEN_TILE, l)
    nl = l // tm
    row = pl.BlockSpec((None, tm, d), lambda i, j: (i, j, 0))
    kv = pl.BlockSpec((None, nm, d), lambda i, j: (i, 0, 0))
    ins = [x, ys, pm, wo1, g2, wq, mk, mv, wxo]
    specs = [row, row, row, _const_spec(wo1.shape), _const_spec(g2.shape), _const_spec(wq.shape), kv, kv,
             _const_spec(wxo.shape)]
    rider = make_rider(lambda i, j: i * nl + j, b * nl)
    body = functools.partial(_two_bodies, _attn_prompt_body, len(ins), 1, rider[0], len(rider[1]))
    return pl.pallas_call(
        body, grid=(b, nl),
        in_specs=specs + list(rider[2]),
        out_specs=[row] + list(rider[3]),
        out_shape=[jax.ShapeDtypeStruct((b, l, d), F32)] + list(rider[4]),
        compiler_params=_params("arbitrary", "arbitrary"),
        name="attn_prompt",
    )(*ins, *rider[1])


def _merge_q_body(x_ref, ys_ref, p_ref, wo1_ref, g2_ref, wq_ref, x1_ref, q_ref):
    x1 = x_ref[...] + _dot(ys_ref[...], wo1_ref[...]) + p_ref[...]
    x1_ref[...] = x1
    q_ref[...] = _dot(_rms(x1, g2_ref[...]).astype(BF16), wq_ref[...])


def _merge_q(x2d, ys, pm, wo1, g2, wq):
    t, d = x2d.shape
    return pl.pallas_call(
        _merge_q_body,
        out_shape=[jax.ShapeDtypeStruct((t, d), F32)] * 2,
        compiler_params=pltpu.CompilerParams(vmem_limit_bytes=VMEM_LIMIT),
        name="merge_q",
    )(x2d, ys, pm, wo1, g2, wq)


def _tile_view(kv):
    b, m, nh, hd = kv.shape
    nt = hd // LANES
    return kv.reshape(b, m, nh, nt, LANES).transpose(0, 1, 3, 2, 4).reshape(b, m, nt * nh, LANES)


def _tile_unview(o):
    b, rows, _ = o.shape
    nt = rows // X_HEADS
    return o.reshape(b, nt, X_HEADS, LANES).transpose(0, 2, 1, 3).reshape(b, rows * LANES)


def _attn_step_body(q_ref, k_ref, v_ref, o_ref):
    bt, nm, rows, _ = k_ref.shape
    hd = rows * LANES // X_HEADS
    for b in range(bt):
        r = jnp.sum(k_ref[b] * q_ref[b], axis=-1, keepdims=True)
        s = (r + pltpu.roll(r, X_HEADS, 1)) * (hd ** -0.5)
        e = jnp.exp(s - jnp.max(s, axis=0, keepdims=True))
        p = e / jnp.sum(e, axis=0, keepdims=True)
        o_ref[b] = jnp.sum(p * v_ref[b], axis=0)


def _attn_step_rider(q8, k8, v8, nsteps):
    nbt, rows, _ = q8.shape
    nm = k8.shape[1]
    assert rows == 2 * X_HEADS, "score assembly assumes two 128-lane tiles per head"
    assert nbt % nsteps == 0
    bt = nbt // nsteps
    qs = pl.BlockSpec((bt, rows, LANES), lambda i: (i, 0, 0))
    kv = pl.BlockSpec((bt, nm, rows, LANES), lambda i: (i, 0, 0, 0))
    return (_attn_step_body, (q8, k8, v8), (qs, kv, kv), (qs,),
            (jax.ShapeDtypeStruct((nbt, rows, LANES), F32),))


def _ffn_body(has_o, final, *refs):
    if has_o:
        x_ref, o_in_ref, wxo_ref, g3_ref, wg_ref, wu_ref, wd_ref, gf_ref, y_ref = refs
        x2 = x_ref[...] + _dot(o_in_ref[...].astype(BF16), wxo_ref[...])
    else:
        x_ref, g3_ref, wg_ref, wu_ref, wd_ref, gf_ref, y_ref = refs
        x2 = x_ref[...]
    hf = _rms(x2, g3_ref[...]).astype(BF16)
    ff = wg_ref.shape[1]
    x3 = x2
    for lo in range(0, ff, FF_CHUNK):
        hi = min(lo + FF_CHUNK, ff)
        act = (_silu(_dot(hf, wg_ref[:, lo:hi])) * _dot(hf, wu_ref[:, lo:hi])).astype(BF16)
        x3 = x3 + _dot(act, wd_ref[lo:hi, :])
    y_ref[...] = _rms(x3, gf_ref[...]) if final else x3


def _ffn(x2d, g3, wg, wu, wd, gf, final, o_in=None, wxo=None, rider=None):
    t, d = x2d.shape
    tm = min(TOKEN_TILE, t)
    row = pl.BlockSpec((tm, d), lambda i: (i, 0))
    has_o = o_in is not None
    ins = [x2d] + ([o_in, wxo] if has_o else []) + [g3, wg, wu, wd, gf]
    specs = [row] + ([row, _const_spec(wxo.shape)] if has_o else []) + [
        _const_spec(g3.shape), _const_spec(wg.shape), _const_spec(wu.shape), _const_spec(wd.shape),
        _const_spec(gf.shape)]
    body = functools.partial(_ffn_body, has_o, final)
    out_specs, out_shape = [row], [jax.ShapeDtypeStruct((t, d), F32)]
    if rider is not None:
        body = functools.partial(_two_bodies, body, len(ins), 1, rider[0], len(rider[1]))
        ins, specs = ins + list(rider[1]), specs + list(rider[2])
        out_specs, out_shape = out_specs + list(rider[3]), out_shape + list(rider[4])
    res = pl.pallas_call(
        body, grid=(t // tm,),
        in_specs=specs, out_specs=out_specs, out_shape=out_shape,
        compiler_params=_params("arbitrary"),
        name="ffn_o" if has_o else "ffn",
    )(*ins)
    return res[0] if rider is None else res


def _two_bodies(body_a, n_in_a, n_out_a, body_b, n_in_b, *refs):
    outs = refs[n_in_a + n_in_b:]
    body_a(*refs[:n_in_a], *outs[:n_out_a])
    body_b(*refs[n_in_a:n_in_a + n_in_b], *outs[n_out_a:])


def _merge_riders(ra, rb):
    body = functools.partial(_two_bodies, ra[0], len(ra[1]), len(ra[3]), rb[0], len(rb[1]))
    return (body,) + tuple(tuple(ra[k]) + tuple(rb[k]) for k in range(1, 5))


def _cast_body(*refs):
    n = len(refs) // 2
    for src, dst in zip(refs[:n], refs[n:]):
        dst[...] = src[...].astype(dst.dtype)


def _cast_rider(ws, step_of, nsteps):
    bf16_rows = 16
    specs, shapes = [], []
    for w in ws:
        r, c = w.shape
        k = next(k for k in (1, 2, 4, 8) if r % (nsteps // k) == 0 and (r // (nsteps // k)) % bf16_rows == 0)
        specs.append(pl.BlockSpec((r // (nsteps // k), c), lambda *g, k=k: (step_of(*g) // k, 0)))
        shapes.append(jax.ShapeDtypeStruct((r, c), BF16))
    return (_cast_body, tuple(ws), tuple(specs), tuple(specs), tuple(shapes))


def _layer_weights(i, ln1_g, w_in, conv_w, conv_b, dt_bias, a_log, d_ssd, gn_g, lam_re, lam_im, log_step,
                   b_re, b_im, c_re, c_im, d_s5, w_glu, b_glu, w_out, ln2_g, w_xq, w_xk, w_xv, w_xo, ln3_g,
                   w_gate, w_up, w_down):
    d = w_in.shape[1]
    n_heads = dt_bias.shape[1]
    dssd = n_heads * SSD_HEAD_DIM
    conv_dim = conv_w.shape[2]
    o1, o2, o3 = dssd, dssd + conv_dim, dssd + conv_dim + n_heads
    win = w_in[i]
    w = {}
    w["ln1"] = ln1_g[i].reshape(1, d)
    w["win"] = win.astype(BF16)
    w["dims"] = (dssd, conv_dim)
    w["wu_s5"] = w["win"][:, o3:]
    w["cw"] = conv_w[i]
    w["cb"] = conv_b[i].reshape(1, conv_dim)
    w["dtb"] = jnp.pad(dt_bias[i], (0, LANES - n_heads)).reshape(1, LANES)
    w["alog"] = jnp.pad(a_log[i], (0, LANES - n_heads)).reshape(1, LANES)
    w["dexp"] = jnp.repeat(d_ssd[i], SSD_HEAD_DIM).reshape(1, dssd)
    w["gn"] = gn_g[i].reshape(1, dssd)
    w["s5"] = (lam_re[i], lam_im[i], log_step[i], b_re[i], b_im[i], c_re[i], c_im[i])
    ds5 = d_s5.shape[1]
    w["dcol"] = d_s5[i].reshape(ds5, 1)
    w["wglut"] = w_glu[i].T.astype(BF16)
    w["bglu"] = b_glu[i].reshape(ds5, 1)
    w["wo1"] = w_out[i, :dssd].astype(BF16)
    w["wo2"] = w_out[i, dssd:].astype(BF16)
    w["ln2"] = ln2_g[i].reshape(1, d)
    w["wq"] = w_xq[i].astype(BF16)
    w["wk"] = w_xk[i].astype(BF16)
    w["wv"] = w_xv[i].astype(BF16)
    w["wxo"] = w_xo[i].astype(BF16)
    w["ln3"] = ln3_g[i].reshape(1, d)
    w["ffn_f32"] = (w_gate[i], w_up[i], w_down[i])
    return w


def _rows_to_state(s):
    p = s.shape[-1] // 2
    st = s.transpose(1, 0, 2)
    return st[..., :p], st[..., p:]


def kernel(x_prompt, x_sample, mem_prompt, state_conv, state_ssm, state_s5_re, state_s5_im, cache_mem_k, cache_mem_v, ln1_g, w_in, conv_w, conv_b, dt_bias, a_log, d_ssd, gn_g, lam_re, lam_im, log_step, b_re, b_im, c_re, c_im, d_s5, w_glu, b_glu, w_out, ln2_g, w_xq, w_xk, w_xv, w_xo, ln3_g, w_gate, w_up, w_down, final_g):
    bp, seq, d = x_prompt.shape
    bs = x_sample.shape[0]
    depth = w_in.shape[0]
    nm = mem_prompt.shape[1]
    ng, ns = lam_re.shape[1], lam_re.shape[2]
    gf = final_g.reshape(1, d)
    hp = x_prompt
    hs = x_sample.reshape(bs, d)
    outs = {k: [] for k in ("conv_p", "ssm_p", "re_p", "im_p", "mk_p", "mv_p", "conv_s", "ssm_s", "re_s", "im_s")}
    yp = ys_out = None
    for i in range(depth):
        w = _layer_weights(i, ln1_g, w_in, conv_w, conv_b, dt_bias, a_log, d_ssd, gn_g, lam_re, lam_im, log_step,
                           b_re, b_im, c_re, c_im, d_s5, w_glu, b_glu, w_out, ln2_g, w_xq, w_xk, w_xv, w_xo, ln3_g,
                           w_gate, w_up, w_down)
        last = i == depth - 1
        y_ssd, conv_p, ssm_p = _ssd_prompt(hp, w["ln1"], w["win"], w["cw"], w["cb"], w["dtb"], w["alog"], w["dexp"],
                                           w["gn"])
        q = S5_Q
        kv_rider = functools.partial(_kv_proj_rider, mem_prompt.reshape(bp * nm, d), w["wk"], w["wv"])
        (tt, wt, zt, a1, a2, a11, a21), (mk8, mv8, mk, mv) = _s5_prep(q, *w["s5"], kv_rider)
        mk = mk.reshape(bp, nm, d)
        mv = mv.reshape(bp, nm, d)
        s0 = jnp.zeros((ng, bp, 2 * ns), F32)
        pperm, sfin = _s5_mix(q, bp, hp, w["ln1"], w["wu_s5"],tt, wt, zt, a1, a2, s0, w["dcol"], w["wglut"],
                              w["bglu"], w["wo2"])
        pm = pperm.reshape(q, seq // q, bp, d).transpose(2, 1, 0, 3).reshape(bp, seq, d)
        re_p, im_p = _rows_to_state(sfin)
        attn_steps = bp * (seq // min(TOKEN_TILE, seq))
        zs, xbcs, dts = _inproj(hs, w["ln1"], w["win"], *w["dims"], bs // attn_steps)
        riders = lambda step_of, n: _merge_riders(
            _ssd_step_rider(zs, xbcs, dts, state_conv[i], state_ssm[i], w["cw"], w["cb"], w["dtb"], w["alog"],
                            w["dexp"], w["gn"], step_of, n),
            _cast_rider(w["ffn_f32"], step_of, n))
        x2, ys_ssd, conv_s, ssm_s, w["wg"], w["wu"], w["wdn"] = _attn_prompt(
            hp, y_ssd, pm, w["wo1"], w["ln2"], w["wq"], mk, mv, w["wxo"], riders)
        ys_ssd = ys_ssd.reshape(bs, -1)
        tt1, wt1, zt1 = _s5_single_token_mats(q, tt, wt, zt)
        s0s = jnp.stack([state_s5_re[i].reshape(bs, ng * ns), state_s5_im[i].reshape(bs, ng * ns)])
        ps, sfin_s = _s5_mix(1, bs, hs.reshape(bs, 1, d), w["ln1"], w["wu_s5"],tt1, wt1, zt1, a11, a21, s0s,
                             w["dcol"], w["wglut"], w["bglu"], w["wo2"])
        re_s, im_s = sfin_s[0].reshape(bs, ng, ns), sfin_s[1].reshape(bs, ng, ns)
        x1s, qs = _merge_q(hs, ys_ssd, ps.reshape(bs, d), w["wo1"], w["ln2"], w["wq"])
        attn_rider = _attn_step_rider(_tile_view(qs.reshape(bs, 1, X_HEADS, d // X_HEADS))[:, 0],
                                      _tile_view(cache_mem_k[i]), _tile_view(cache_mem_v[i]),
                                      (bp * seq) // min(TOKEN_TILE, bp * seq))
        hp, o_s = _ffn(x2.reshape(bp * seq, d), w["ln3"], w["wg"], w["wu"], w["wdn"], gf, last, rider=attn_rider)
        hp = hp.reshape(bp, seq, d)
        hs = _ffn(x1s, w["ln3"], w["wg"], w["wu"], w["wdn"], gf, last, o_in=_tile_unview(o_s), wxo=w["wxo"])
        for k, v in (("conv_p", conv_p), ("ssm_p", ssm_p), ("re_p", re_p), ("im_p", im_p),
                     ("mk_p", _tile_unview_kv(mk8, bp, nm)), ("mv_p", _tile_unview_kv(mv8, bp, nm)),
                     ("conv_s", conv_s), ("ssm_s", ssm_s), ("re_s", re_s), ("im_s", im_s)):
            outs[k].append(v)
    st = lambda k: jnp.stack(outs[k])
    return (hp, hs.reshape(bs, 1, d), st("conv_p"), st("ssm_p"), st("re_p"), st("im_p"), st("mk_p"), st("mv_p"),
            st("conv_s"), st("ssm_s"), st("re_s"), st("im_s"))
```

```python
import functools
import math

import jax
import jax.numpy as jnp
from jax import lax
from jax.experimental import pallas as pl
from jax.experimental.pallas import tpu as pltpu

F32 = jnp.float32
BF16 = jnp.bfloat16
EPS = 1e-6

LANES = 128
VMEM_LIMIT = 56 * 1024 * 1024

SSD_HEAD_DIM = 64
SSD_STATE = 128
SSD_GROUPS = 2
SSD_CHUNK = 128
CONV_W = 4
S5_CH = 16
S5_STATE = 64
S5_Q = 16
S5_GROUP_BATCH = 16
S5_SLICE_TOKENS = 512
X_HEADS = 4
TOKEN_TILE = 512
PROJ_PIECE = 256
SSD_SEQS_PER_STEP = 2
FF_CHUNK = 1024


def _const_spec(shape):
    nd = len(shape)
    return pl.BlockSpec(shape, lambda *_: (0,) * nd, pipeline_mode=pl.Buffered(1))


def _params(*sem):
    return pltpu.CompilerParams(dimension_semantics=sem, vmem_limit_bytes=VMEM_LIMIT)


def _rms(x, g):
    return x * lax.rsqrt(jnp.mean(x * x, axis=-1, keepdims=True) + EPS) * g


def _sigmoid(x):
    return 1.0 / (1.0 + jnp.exp(-x))


def _silu(x):
    return x * _sigmoid(x)


def _gelu_tanh(x):
    return 0.5 * x * (1.0 + jnp.tanh(math.sqrt(2.0 / math.pi) * (x + 0.044715 * (x * x * x))))


def _softplus(x):
    return jnp.maximum(x, 0.0) + jnp.log1p(jnp.exp(-jnp.abs(x)))


def _dot(a, b):
    return jnp.dot(a, b, preferred_element_type=F32)


def _dot_nt(a, b):
    return lax.dot_general(a, b, (((1,), (1,)), ((), ())), preferred_element_type=F32)


def _dot_tn(a, b):
    return lax.dot_general(a, b, (((0,), (0,)), ((), ())), preferred_element_type=F32)


def _proj_windows(d_ssd, conv_dim):
    assert d_ssd % LANES == 0 and conv_dim % LANES == 0
    return (0, d_ssd), (d_ssd, d_ssd + conv_dim), (d_ssd + conv_dim, d_ssd + conv_dim + LANES)


def _expand_heads(v, n_heads):
    rows = v.shape[0]
    lane = lax.broadcasted_iota(jnp.int32, (rows, LANES), 1)
    pieces = [jnp.where(lane < SSD_HEAD_DIM, v[:, 2 * j:2 * j + 1], v[:, 2 * j + 1:2 * j + 2])
              for j in range(n_heads // 2)]
    return jnp.concatenate(pieces, axis=1)


def _ssd_chunk_math(z, x, dt_raw, last_chunk, cw_ref, cb_ref, dtb_ref, alog_ref, dexp_ref, gn_ref, e2_ref,
                    y_ref, conv_ref, ssm_ref, xpad, state, side_jobs=()):
    jobs = list(side_jobs)

    def side(n=1):
        for _ in range(n):
            if jobs:
                jobs.pop(0)()

    q = SSD_CHUNK
    d_ssd = z.shape[-1]
    n_heads = d_ssd // SSD_HEAD_DIM
    hpg = n_heads // SSD_GROUPS
    gw = hpg * SSD_HEAD_DIM
    nct = xpad.shape[0]
    xt = d_ssd // LANES

    for t in range(nct):
        xpad[t, 8:8 + q, :] = x[:, t * LANES:(t + 1) * LANES]
    cw = cw_ref[...]
    conv = (cw[3] * xpad[:, 8:8 + q, :] + cw[2] * xpad[:, 7:7 + q, :] + cw[1] * xpad[:, 6:6 + q, :]
            + cw[0] * xpad[:, 5:5 + q, :] + cb_ref[...])
    xpad[:, 0:8, :] = xpad[:, q:q + 8, :]
    xact = _silu(conv)
    xs = jnp.concatenate([xact[t] for t in range(xt)], axis=1)
    bm = xact[xt:xt + SSD_GROUPS].astype(BF16)
    cm = xact[xt + SSD_GROUPS:].astype(BF16)
    side()

    dt = _softplus(dt_raw + dtb_ref[...])
    a = -jnp.exp(alog_ref[...])
    row = lax.broadcasted_iota(jnp.int32, (q, q), 0)
    col = lax.broadcasted_iota(jnp.int32, (q, q), 1)
    causal = row >= col
    tri = jnp.where(causal, 1.0, 0.0).astype(F32)
    acum = jnp.dot(tri, dt * a, precision=lax.Precision.HIGHEST, preferred_element_type=F32)
    acum_t = acum.T
    last = acum[q - 1:q, :]

    def expand(v):
        hi = v.astype(BF16)
        lo = (v - hi.astype(F32)).astype(BF16)
        return _dot(jnp.concatenate([hi, lo], axis=1), e2_ref[...])

    dt_e = expand(dt)
    ea_e = expand(jnp.exp(acum))
    dend_e = expand(jnp.exp(last - acum))

    dtx = xs * dt_e
    dtx_b = dtx.astype(BF16)
    xdec_b = (dtx * dend_e).astype(BF16)
    side()

    lane = lax.broadcasted_iota(jnp.int32, (q, LANES), 1)
    zero_b = jnp.zeros((q, LANES), BF16)
    y_tiles = []
    for g in range(SSD_GROUPS):
        bg = bm[g]
        cg = cm[g]
        cb = _dot_nt(cg, bg)
        gs = slice(g * gw, (g + 1) * gw)
        st_g = state[:, gs]
        y_off = _dot(cg, st_g.astype(BF16)) * ea_e[:, gs]
        for pr in range(hpg // 2):
            tile = g * (hpg // 2) + pr
            ms = []
            for h in (2 * tile, 2 * tile + 1):
                seg = acum[:, h:h + 1] - acum_t[h:h + 1, :]
                lmat = jnp.exp(jnp.where(causal, seg, -jnp.inf))
                ms.append((cb * lmat).astype(BF16))
            dtile = dtx_b[:, tile * LANES:(tile + 1) * LANES]
            rhs = jnp.concatenate([jnp.where(lane < SSD_HEAD_DIM, dtile, zero_b),
                                   jnp.where(lane >= SSD_HEAD_DIM, dtile, zero_b)], axis=0)
            y_tiles.append(_dot(jnp.concatenate(ms, axis=1), rhs)
                           + y_off[:, pr * LANES:(pr + 1) * LANES])
            if pr % 2 == 1:
                side()
        new = _dot_tn(bg, xdec_b[:, gs])
        state[:, gs] = st_g * ea_e[q - 1:q, gs] + new
    y = jnp.concatenate(y_tiles, axis=1) + dexp_ref[...] * xs
    y = y * _silu(z)
    ms = jnp.mean(y * y, axis=-1, keepdims=True)
    side(len(jobs))
    y_ref[...] = (y * lax.rsqrt(ms + EPS) * gn_ref[...]).astype(y_ref.dtype)

    @pl.when(last_chunk)
    def _():
        conv_ref[...] = x[q - (CONV_W - 1):q, :]
        ssm_ref[...] = state[...].T.reshape(ssm_ref.shape)


def _ssd_prompt_body(wins, x_ref, g_ref, w_ref, cw_ref, cb_ref, dtb_ref, alog_ref, dexp_ref, gn_ref,
                     e2_ref, xs_ref, y_ref, conv_ref, ssm_ref, zs_ref, xbcs_ref, dts_ref,
                     za, xa, da, zb, xb, db, xpad, state):
    i = pl.program_id(0)
    j = pl.program_id(1)
    nc = pl.num_programs(1) - 1
    ns, q, d = x_ref.shape

    @pl.when(jnp.logical_and(i == 0, j == 0))
    def _():
        for r in (zb, xb, db):
            r[...] = jnp.zeros(r.shape, F32)
        hs = _rms(xs_ref[...], g_ref[...]).astype(BF16)
        for (lo, hi), o_ref in zip(wins, (zs_ref, xbcs_ref, dts_ref)):
            o_ref[...] = _dot(hs, w_ref[:, lo:hi]).reshape(o_ref.shape)

    @pl.when(j <= 1)
    def _():
        xpad[:, :, 0:8, :] = jnp.zeros(xpad.shape[:2] + (8, LANES), F32)
        state[...] = jnp.zeros(state.shape, F32)

    def step(wr, rd):
        xin = x_ref[...].reshape(ns * q, d)
        msx = jnp.mean(xin * xin, axis=-1, keepdims=True)
        hbox = []

        def proj_piece(dst, base, lo, hi):
            def run():
                if not hbox:
                    hbox.append((xin * lax.rsqrt(msx + EPS) * g_ref[...]).astype(BF16))
                dst[:, lo:hi] = _dot(hbox[0], w_ref[:, base + lo:base + hi])
            return run

        pieces = []
        for dst, (w_lo, w_hi) in ((wr[0], wins[0]), (wr[2], wins[2]), (wr[1], wins[1])):
            n = w_hi - w_lo
            pieces += [proj_piece(dst, w_lo, lo, min(lo + PROJ_PIECE, n)) for lo in range(0, n, PROJ_PIECE)]
        per = -(-len(pieces) // ns)
        for s in range(ns):
            rows = slice(s * q, (s + 1) * q)
            _ssd_chunk_math(rd[0][rows, :], rd[1][rows, :], rd[2][rows, :], j == nc, cw_ref, cb_ref, dtb_ref,
                            alog_ref, dexp_ref, gn_ref, e2_ref, y_ref.at[s], conv_ref.at[s], ssm_ref.at[s],
                            xpad.at[s], state.at[s], pieces[s * per:(s + 1) * per])

    @pl.when(j % 2 == 0)
    def _():
        step((za, xa, da), (zb, xb, db))

    @pl.when(j % 2 == 1)
    def _():
        step((zb, xb, db), (za, xa, da))


def _ssd_prompt(x, ln_g, w_in, cw, cb, dtb, alog, dexp, gn, xs, group):
    b, l, d = x.shape
    ts = xs.shape[0]
    conv_dim = cw.shape[1]
    d_ssd = conv_dim - 2 * SSD_GROUPS * SSD_STATE
    wins = _proj_windows(d_ssd, conv_dim)
    n_heads = d_ssd // SSD_HEAD_DIM
    q = SSD_CHUNK
    nc = l // q
    assert SSD_STATE == LANES and conv_dim == d_ssd + 2 * SSD_GROUPS * SSD_STATE
    nct = conv_dim // LANES
    cw4 = cw.reshape(CONV_W, nct, 1, LANES)
    cb3 = cb.reshape(nct, 1, LANES)
    e2 = (jnp.arange(2 * LANES)[:, None] % LANES == jnp.arange(d_ssd)[None, :] // SSD_HEAD_DIM).astype(BF16)
    ns = SSD_SEQS_PER_STEP if b % SSD_SEQS_PER_STEP == 0 else 1
    slot = lambda n: pltpu.VMEM((ns * q, n), F32)
    return pl.pallas_call(
        functools.partial(_ssd_prompt_body, wins),
        grid=(b // ns, nc + 1),
        in_specs=[pl.BlockSpec((ns, q, d), lambda i, j: (i, jnp.minimum(j, nc - 1), 0)),
                  _const_spec(ln_g.shape), _const_spec(w_in.shape),
                  _const_spec(cw4.shape), _const_spec(cb3.shape), _const_spec(dtb.shape), _const_spec(alog.shape),
                  _const_spec(dexp.shape), _const_spec(gn.shape), _const_spec(e2.shape), _const_spec(xs.shape)],
        out_specs=[pl.BlockSpec((ns, q, d_ssd), lambda i, j: (i, jnp.maximum(j - 1, 0), 0)),
                   pl.BlockSpec((ns, CONV_W - 1, conv_dim), lambda i, j: (i, 0, 0)),
                   pl.BlockSpec((ns, n_heads, SSD_HEAD_DIM, SSD_STATE), lambda i, j: (i, 0, 0, 0))]
        + [pl.BlockSpec((ts // group, group, hi - lo), lambda i, j: (0, 0, 0)) for lo, hi in wins],
        out_shape=[jax.ShapeDtypeStruct((b, l, d_ssd), BF16),
                   jax.ShapeDtypeStruct((b, CONV_W - 1, conv_dim), F32),
                   jax.ShapeDtypeStruct((b, n_heads, SSD_HEAD_DIM, SSD_STATE), F32)]
        + [jax.ShapeDtypeStruct((ts // group, group, hi - lo), F32) for lo, hi in wins],
        scratch_shapes=[slot(d_ssd), slot(conv_dim), slot(LANES), slot(d_ssd), slot(conv_dim), slot(LANES),
                        pltpu.VMEM((ns, nct, q + 8, LANES), F32), pltpu.VMEM((ns, SSD_STATE, d_ssd), F32)],
        compiler_params=_params("arbitrary", "arbitrary"),
        name="ssd_prompt",
    )(x, ln_g, w_in, cw4, cb3, dtb, alog, dexp, gn, e2, xs)


def _ssd_step_body(z_ref, xbc_ref, dt_ref, cs_ref, st_ref, cw_ref, cb_ref, dtb_ref, alog_ref, dexp_ref, gn_ref,
                   y_ref, conv_ref, ssm_ref):
    bt, d_ssd = z_ref.shape
    n_heads = d_ssd // SSD_HEAD_DIM
    hpg = n_heads // SSD_GROUPS
    x = xbc_ref[...]
    cw = cw_ref[...]
    cs = [cs_ref[:, k, :] for k in range(CONV_W - 1)]
    conv = cw[0:1] * cs[0] + cw[1:2] * cs[1] + cw[2:3] * cs[2] + cw[3:4] * x + cb_ref[...]
    conv_ref[:, 0, :] = cs[1]
    conv_ref[:, 1, :] = cs[2]
    conv_ref[:, 2, :] = x
    xact = _silu(conv)
    xs = xact[:, :d_ssd]
    bm = xact[:, d_ssd:d_ssd + SSD_GROUPS * SSD_STATE]
    cm = xact[:, d_ssd + SSD_GROUPS * SSD_STATE:]
    dt = _softplus(dt_ref[...] + dtb_ref[...])
    da = jnp.exp(dt * (-jnp.exp(alog_ref[...])))
    dtx = xs * _expand_heads(dt, n_heads)
    pad = jnp.zeros((LANES - bt, d_ssd), F32)
    to_cols = lambda v: jnp.concatenate([v, pad], axis=0).T
    dtx_t = to_cols(dtx)
    da_t = to_cols(_expand_heads(da, n_heads))
    lane = lax.broadcasted_iota(jnp.int32, (d_ssd, LANES), 1)
    gw = hpg * SSD_HEAD_DIM
    rows_of = lambda v, b: jnp.concatenate(
        [jnp.broadcast_to(v[b:b + 1, g * SSD_STATE:(g + 1) * SSD_STATE], (gw, SSD_STATE)) for g in range(SSD_GROUPS)],
        axis=0)
    y_t = jnp.zeros((d_ssd, LANES), F32)
    for b in range(bt):
        s0 = st_ref[b].reshape(d_ssd, SSD_STATE)
        s_new = s0 * da_t[:, b:b + 1] + dtx_t[:, b:b + 1] * rows_of(bm, b)
        ssm_ref[b] = s_new.reshape(n_heads, SSD_HEAD_DIM, SSD_STATE)
        ycol = jnp.sum(s_new * rows_of(cm, b), axis=1, keepdims=True)
        y_t = jnp.where(lane == b, ycol, y_t)
    y = y_t.T[:bt, :] + dexp_ref[...] * xs
    y = y * _silu(z_ref[...])
    y_ref[...] = _rms(y, gn_ref[...]).astype(y_ref.dtype)


def _ssd_step_rider(z, xbc, dt, conv_state, ssm_state, cw, cb, dtb, alog, dexp, gn, step_of, nsteps):
    _, bt, d_ssd = z.shape
    conv_dim = xbc.shape[-1]
    n_heads = d_ssd // SSD_HEAD_DIM
    assert z.shape[0] == nsteps and conv_state.shape[0] == nsteps * bt
    row = lambda n: pl.BlockSpec((None, bt, n), lambda *g: (step_of(*g), 0, 0))
    cs_spec = pl.BlockSpec((bt, CONV_W - 1, conv_dim), lambda *g: (step_of(*g), 0, 0))
    st_spec = pl.BlockSpec((bt, n_heads, SSD_HEAD_DIM, SSD_STATE), lambda *g: (step_of(*g), 0, 0, 0))
    cst = lambda a: pl.BlockSpec(a.shape, lambda *g: (0,) * a.ndim, pipeline_mode=pl.Buffered(1))
    ins = (z, xbc, dt, conv_state, ssm_state, cw, cb, dtb, alog, dexp, gn)
    specs = (row(d_ssd), row(conv_dim), row(LANES), cs_spec, st_spec, cst(cw), cst(cb), cst(dtb), cst(alog),
             cst(dexp), cst(gn))
    out_shapes = (jax.ShapeDtypeStruct((nsteps, bt, d_ssd), BF16),
                  jax.ShapeDtypeStruct(conv_state.shape, F32),
                  jax.ShapeDtypeStruct(ssm_state.shape, F32))
    return (_ssd_step_body, ins, specs, (row(d_ssd), cs_spec, st_spec), out_shapes)


def _split_bf16(x):
    hi = x.astype(BF16)
    lo = (x - hi.astype(F32)).astype(BF16)
    return hi, lo


def _s5_prep_body(q, ls_ref, lrp_ref, lip_ref, lr2_ref, li2_ref, br_ref, bi_ref, ccat_ref, ca_ref, cb_ref, rep_ref,
                  tt_ref, wt_ref, zt_ref, a1_ref, a2_ref, a1s_ref, a2s_ref):
    ng = ls_ref.shape[0]
    w = q * S5_CH
    step = jnp.exp(ls_ref[...])
    lr, li = lrp_ref[...], lip_ref[...]
    mag = jnp.exp(lr * step)
    ang = li * step
    lbr = mag * jnp.cos(ang)
    lbi = mag * jnp.sin(ang)
    den = lr * lr + li * li
    kr = ((lbr - 1.0) * lr + lbi * li) / den
    ki = (lbi * lr - (lbr - 1.0) * li) / den
    br, bi = br_ref[...], bi_ref[...]
    bbr = kr * br - ki * bi
    bbi = kr * bi + ki * br
    if q > 1:
        np_ = br.shape[1]
        rep = lambda v: jnp.dot(v.reshape(ng * np_, S5_CH), rep_ref[...], precision=lax.Precision.HIGHEST,
                                preferred_element_type=F32).reshape(ng, np_, w)
        bbr, bbi = rep(bbr), rep(bbi)
    if q == 1:
        ball = jnp.concatenate([bbr, bbi], axis=1)
    else:
        d = lax.broadcasted_iota(jnp.int32, (1, 1, w), 2) // S5_CH
        fr, fi = lbr, lbi
        pr = pi = None
        for b in range((q - 1).bit_length()):
            bit = ((d >> b) & 1) == 1
            sr, si = jnp.where(bit, fr, 1.0), jnp.where(bit, fi, 0.0)
            pr, pi = (sr, si) if pr is None else (pr * sr - pi * si, pr * si + pi * sr)
            fr, fi = fr * fr - fi * fi, 2.0 * (fr * fi)
        ball = jnp.concatenate([pr * bbr - pi * bbi, pr * bbi + pi * bbr], axis=1)
    wt_ref[...] = ball.astype(BF16)

    lane3 = lax.broadcasted_iota(jnp.int32, (1, 1, 2 * S5_STATE), 2)
    first = lane3 < S5_STATE
    csign = jnp.where(first, ccat_ref[...], -ccat_ref[...])
    ch, cl = _split_bf16(csign)
    bh, bl = _split_bf16(ball)
    bdot = lambda x, y: lax.dot_general(x, y, (((2,), (1,)), ((0,), (0,))), preferred_element_type=F32)
    kall = bdot(ch, bh) + bdot(ch, bl) + bdot(cl, bh)
    k2 = kall.reshape(ng * S5_CH, w)
    lane2 = lax.broadcasted_iota(jnp.int32, (ng * S5_CH, w), 1)
    for t in range(q):
        sh = (q - 1 - t) * S5_CH
        r = pltpu.roll(k2, sh, 1) if sh else k2
        r = jnp.where(lane2 >= sh, r, 0.0)
        tt_ref[:, t * S5_CH:(t + 1) * S5_CH, :] = r.reshape(ng, S5_CH, w).astype(BF16)

    lr2, li2 = lr2_ref[...], li2_ref[...]
    t1 = (lax.broadcasted_iota(jnp.int32, (1, q, 1), 1) + 1).astype(F32)
    zm = jnp.exp(t1 * (lr2 * step))
    za = t1 * (li2 * step)
    zr = zm * jnp.cos(za)
    zi = zm * jnp.sin(za)
    ca, cb = ca_ref[...], cb_ref[...]
    for t in range(q):
        prt = zr[:, t:t + 1, :]
        pit = zi[:, t:t + 1, :]
        zt = jnp.where(first, ca * prt - cb * pit, -(ca * pit) - cb * prt)
        zt_ref[:, t * S5_CH:(t + 1) * S5_CH, :] = zt.astype(BF16)
    qf = float(q)
    mq = jnp.exp(qf * (lr2 * step))
    aq = qf * (li2 * step)
    ar = mq * jnp.cos(aq)
    ai = mq * jnp.sin(aq)
    a1_ref[...] = ar
    a2_ref[...] = jnp.where(first, -ai, ai)
    a1s_ref[...] = zr[:, 0:1, :]
    a2s_ref[...] = jnp.where(first, -zi[:, 0:1, :], zi[:, 0:1, :])


def _s5_prep(q, lam_re, lam_im, log_step, b_re, b_im, c_re, c_im, make_rider):
    ng, p = lam_re.shape
    w = q * S5_CH
    ls = log_step.reshape(ng, 1, 1)
    lrp = lam_re.reshape(ng, p, 1)
    lip = lam_im.reshape(ng, p, 1)
    lr2 = jnp.concatenate([lam_re, lam_re], axis=-1).reshape(ng, 1, 2 * p)
    li2 = jnp.concatenate([lam_im, lam_im], axis=-1).reshape(ng, 1, 2 * p)
    rep = (jnp.arange(S5_CH)[:, None] == jnp.arange(w)[None, :] % S5_CH).astype(F32)
    ccat = jnp.concatenate([c_re, c_im], axis=-1)
    ca = jnp.concatenate([c_re, c_re], axis=-1)
    cb = jnp.concatenate([c_im, c_im], axis=-1)
    gb = ng if q == 1 else 16
    blk = lambda a, b: pl.BlockSpec((gb, a, b), lambda i: (i, 0, 0))
    ins = (ls, lrp, lip, lr2, li2, b_re, b_im, ccat, ca, cb, rep)
    in_specs = [blk(1, 1), blk(p, 1), blk(p, 1), blk(1, 2 * p), blk(1, 2 * p), blk(p, S5_CH), blk(p, S5_CH),
                blk(S5_CH, 2 * p), blk(S5_CH, 2 * p), blk(S5_CH, 2 * p), _const_spec(rep.shape)]
    out_specs = [blk(w, w), blk(2 * p, w), blk(w, 2 * p)] + [blk(1, 2 * p)] * 4
    out_shape = [jax.ShapeDtypeStruct((ng, w, w), BF16), jax.ShapeDtypeStruct((ng, 2 * p, w), BF16),
                 jax.ShapeDtypeStruct((ng, w, 2 * p), BF16)] + [jax.ShapeDtypeStruct((ng, 1, 2 * p), F32)] * 4
    body = functools.partial(_s5_prep_body, q)
    rider = make_rider(ng // gb)
    body = functools.partial(_two_bodies, body, len(ins), len(out_specs), rider[0], len(rider[1]))
    res = pl.pallas_call(
        body,
        grid=(ng // gb,),
        in_specs=in_specs + list(rider[2]),
        out_specs=out_specs + list(rider[3]),
        out_shape=out_shape + list(rider[4]),
        compiler_params=_params("arbitrary"),
        name=f"s5_prep_q{q}",
    )(*ins, *rider[1])
    return res[:len(out_specs)], res[len(out_specs):]


def _s5_single_token_mats(q, tt, wt, zt):
    c = S5_CH
    return tt[:, :c, (q - 1) * c:], wt[:, :, :c], zt[:, :c, :]


def _s5_mix_body(q, nb, nkb, npi, npt, perm, x_ref, g_ref, wu_ref, tt_ref, wt_ref, zt_ref, a1_ref, a2_ref, s0_ref,
                 dcol_ref, wglut_ref, bglu_ref, wo_ref, p_ref, sfin_ref, ut, yt, carry, wut):
    ng = tt_ref.shape[0]
    nk = nkb // nb
    qs = q // npt
    ch = qs * nkb
    tile = pl.program_id(0)
    ph = pl.program_id(1)

    @pl.when(jnp.logical_and(tile == 0, ph == 0))
    def _():
        if perm:
            carry[...] = s0_ref[...]
        else:
            lane_s = lax.broadcasted_iota(jnp.int32, (nb, 2 * S5_STATE), 1)
            for pr in range(ng // 2):
                cols = slice(pr * 2 * S5_STATE, (pr + 1) * 2 * S5_STATE)
                re_t, im_t = s0_ref[0, :, cols], s0_ref[1, :, cols]
                carry[2 * pr] = jnp.where(lane_s < S5_STATE, re_t, pltpu.roll(im_t, S5_STATE, 1))
                carry[2 * pr + 1] = jnp.where(lane_s < S5_STATE, pltpu.roll(re_t, S5_STATE, 1), im_t)
        cw_ = 256
        for c in range(0, wu_ref.shape[1], cw_):
            wut[c:c + cw_, :] = wu_ref[:, c:c + cw_].T

    def rows_of(r):
        if not perm:
            return x_ref[r]
        nseq, nblk, ndt, nr, _ = x_ref.shape
        flat = x_ref.reshape(nseq * nblk * ndt * nr, LANES)

        return jnp.concatenate(
            [jnp.concatenate([flat[pl.ds((k * ndt + dt) * nr + r, nseq, stride=nblk * ndt * nr), :]
                              for dt in range(ndt)], axis=1) for k in range(nblk)], axis=0)

    def project(j):
        step = 2 if qs % 2 == 0 else 1
        r0 = (j * qs) % (q // npi)
        for r in range(0, qs, step):
            xin = jnp.concatenate([rows_of(r0 + r + s) for s in range(step)], axis=0)
            u = _dot_nt(wut[...], _rms(xin, g_ref[...]).astype(BF16))
            sl = slice((j * qs + r) * nkb, (j * qs + r + step) * nkb)
            ut[:, sl] = u.astype(BF16)
            yt[:, sl] = dcol_ref[...] * u

    def finish(j):
        gt = _gelu_tanh(yt[:, j * ch:(j + 1) * ch])
        gate = _dot(wglut_ref[...], gt.astype(BF16)) + bglu_ref[...]
        y5 = (gt * _sigmoid(gate)).astype(BF16)
        p_ref[...] = _dot_tn(y5, wo_ref[...]).reshape(p_ref.shape)

    spp = npt // npi
    for hh in range(npi):
        @pl.when(ph == hh)
        def _():
            for j in range(hh * spp, (hh + 1) * spp):
                project(j)

    @pl.when(ph == npi - 1)
    def _():
        gu = S5_GROUP_BATCH
        bdot = lambda a, b: lax.dot_general(a, b, (((2,), (1,)), ((0,), (0,))), preferred_element_type=F32)
        bdot_nt = lambda a, b: lax.dot_general(a, b, (((2,), (2,)), ((0,), (0,))), preferred_element_type=F32)

        def groups(i, _):
            g0 = pl.multiple_of(i * gu, gu)
            r0 = pl.multiple_of(i * (gu * S5_CH), gu * S5_CH)
            gsl = pl.ds(g0, gu)
            rows = ut[pl.ds(r0, gu * S5_CH), :].reshape(gu, S5_CH, q * nkb)
            ugt = jnp.concatenate([rows[:, :, (q - 1 - j) * nkb:(q - j) * nkb] for j in range(q)], axis=1)
            y = bdot(tt_ref[gsl], ugt)
            vt = bdot(wt_ref[gsl], ugt)
            v = jnp.swapaxes(vt, 1, 2)
            v_sw = jnp.swapaxes(jnp.concatenate([vt[:, S5_STATE:], vt[:, :S5_STATE]], axis=1), 1, 2)
            a1 = a1_ref[gsl]
            a2 = a2_ref[gsl]
            s = carry[gsl]
            s_sw = pltpu.roll(s.reshape(gu * nb, 2 * S5_STATE), S5_STATE, 1).reshape(s.shape)
            prev = []
            for k in range(nk):
                prev.append(s)
                s, s_sw = (a1 * s + a2 * s_sw + v[:, k * nb:(k + 1) * nb, :],
                           a1 * s_sw - a2 * s + v_sw[:, k * nb:(k + 1) * nb, :])
            carry[gsl] = s
            sprev = jnp.concatenate(prev, axis=1).astype(BF16)
            y = y + bdot_nt(zt_ref[gsl], sprev)
            for t in range(q):
                yt[pl.ds(r0, gu * S5_CH), t * nkb:(t + 1) * nkb] += (
                    y[:, t * S5_CH:(t + 1) * S5_CH, :].reshape(gu * S5_CH, nkb))
            return 0

        lax.fori_loop(0, ng // gu, groups, 0)
        if perm:
            sfin_ref[...] = carry[...]
        else:
            lane_s = lax.broadcasted_iota(jnp.int32, (nb, 2 * S5_STATE), 1)
            for pr in range(ng // 2):
                cols = slice(pr * 2 * S5_STATE, (pr + 1) * 2 * S5_STATE)
                c0, c1 = carry[2 * pr], carry[2 * pr + 1]
                sfin_ref[0, :, cols] = jnp.where(lane_s < S5_STATE, c0, pltpu.roll(c1, S5_STATE, 1))
                sfin_ref[1, :, cols] = jnp.where(lane_s < S5_STATE, pltpu.roll(c0, S5_STATE, 1), c1)

    for j in range(npt):
        @pl.when(ph == npi + j)
        def _():
            finish(j)


def _s5_mix(q, nb, x, ln_g, wu, tt, wt, zt, a1, a2, s0, dcol, wglut, bglu, wo):
    nseq, l, d = x.shape
    nblk = l // q
    nlt = nblk * nseq
    nkb = min(LANES, nlt)
    ntile = nlt // nkb
    tok = q * nkb
    npt = max(1, tok // S5_SLICE_TOKENS)
    dm = wo.shape[1]
    perm = q > 1
    if perm:
        half = 8
        npi = q // half
        assert nb == nseq and d % LANES == 0 and q % half == 0
        xv = x.reshape(nseq, nblk, npi, half, d // LANES, LANES).transpose(0, 1, 2, 4, 3, 5)
        assert npt % npi == 0
        x_spec = pl.BlockSpec((nseq, nkb // nseq, None, d // LANES, half, LANES),
                              lambda i, j: (0, i, jnp.minimum(j, npi - 1), 0, 0, 0))
    else:
        npi = 1
        xv = x.reshape(1, nseq, d)
        x_spec = pl.BlockSpec((1, nkb, d), lambda i, j: (0, i, 0))
    body = functools.partial(_s5_mix_body, q, nb, nkb, npi, npt, perm)
    p_spec = pl.BlockSpec((q // npt, nkb, dm), lambda i, j: (jnp.maximum(j - npi, 0), i, 0))
    return pl.pallas_call(
        body,
        grid=(ntile, npi + npt),
        in_specs=[x_spec, _const_spec(ln_g.shape), _const_spec(wu.shape), _const_spec(tt.shape),
                  _const_spec(wt.shape), _const_spec(zt.shape), _const_spec(a1.shape), _const_spec(a2.shape),
                  _const_spec(s0.shape), _const_spec(dcol.shape), _const_spec(wglut.shape),
                  _const_spec(bglu.shape), _const_spec(wo.shape)],
        out_specs=[p_spec, pl.BlockSpec(s0.shape, lambda i, j: (0, 0, 0))],
        out_shape=[jax.ShapeDtypeStruct((q, nlt, dm), F32), jax.ShapeDtypeStruct(s0.shape, F32)],
        scratch_shapes=[pltpu.VMEM((d, tok), BF16), pltpu.VMEM((d, tok), F32),
                        pltpu.VMEM((tt.shape[0], nb, 2 * S5_STATE), F32),
                        pltpu.VMEM((wu.shape[1], wu.shape[0]), BF16)],
        compiler_params=_params("arbitrary", "arbitrary"),
        name=f"s5_mix_q{q}",
    )(xv, ln_g, wu, tt, wt, zt, a1, a2, s0, dcol, wglut, bglu, wo)


def _kv_body(m_ref, wk_ref, wv_ref, k_ref, v_ref, kb_ref, vb_ref):
    tm, rows, _ = k_ref.shape
    nt = rows // X_HEADS
    m = m_ref[...].astype(BF16)
    for w_ref, o_ref, ob_ref in ((wk_ref, k_ref, kb_ref), (wv_ref, v_ref, vb_ref)):
        r = _dot(m, w_ref[...])
        ob_ref[...] = r.astype(BF16)
        flat = o_ref.reshape(tm * rows, LANES)
        for h in range(X_HEADS):
            for dt in range(nt):
                c = (h * nt + dt) * LANES
                flat[pl.ds(dt * X_HEADS + h, tm, stride=rows), :] = r[:, c:c + LANES]


def _kv_proj_rider(mem2d, wk, wv, nsteps):
    t, d = mem2d.shape
    assert t % nsteps == 0
    tm = t // nsteps
    rows = d // LANES
    row = pl.BlockSpec((tm, d), lambda i: (i, 0))
    tile = pl.BlockSpec((tm, rows, LANES), lambda i: (i, 0, 0))
    return (_kv_body, (mem2d, wk, wv), (row, _const_spec(wk.shape), _const_spec(wv.shape)),
            (tile, tile, row, row),
            (jax.ShapeDtypeStruct((t, rows, LANES), F32),) * 2 + (jax.ShapeDtypeStruct((t, d), BF16),) * 2)


def _tile_unview_kv(kv8, b, m):
    nt = kv8.shape[1] // X_HEADS
    return kv8.reshape(b, m, nt, X_HEADS, LANES).transpose(0, 1, 3, 2, 4).reshape(b, m, X_HEADS, nt * LANES)


def _attn_prompt_body(x_ref, ys_ref, p_ref, wo1_ref, g2_ref, wq_ref, k_ref, v_ref, wxo_ref, o_ref):
    d = x_ref.shape[-1]
    hd = d // X_HEADS
    x1 = x_ref[...] + _dot(ys_ref[...], wo1_ref[...]) + p_ref[...]
    hq = _rms(x1, g2_ref[...]).astype(BF16)
    qv = _dot(hq, wq_ref[...]).astype(BF16)
    kb = k_ref[...]
    vb = v_ref[...]
    outs = []
    for h in range(X_HEADS):
        sl = slice(h * hd, (h + 1) * hd)
        s = _dot_nt(qv[:, sl], kb[:, sl]) * (hd ** -0.5)
        e = jnp.exp(s - jnp.max(s, axis=-1, keepdims=True))
        p = (e / jnp.sum(e, axis=-1, keepdims=True)).astype(BF16)
        outs.append(_dot(p, vb[:, sl]))
    o = jnp.concatenate(outs, axis=1).astype(BF16)
    o_ref[...] = x1 + _dot(o, wxo_ref[...])


def _attn_prompt(x, ys, pm, wo1, g2, wq, mk, mv, wxo, make_rider):
    b, l, d = x.shape
    nm = mk.shape[1]
    tm = min(TOKEN_TILE, l)
    nl = l // tm
    row = pl.BlockSpec((None, tm, d), lambda i, j: (i, j, 0))
    kv = pl.BlockSpec((None, nm, d), lambda i, j: (i, 0, 0))
    ins = [x, ys, pm, wo1, g2, wq, mk, mv, wxo]
    specs = [row, row, row, _const_spec(wo1.shape), _const_spec(g2.shape), _const_spec(wq.shape), kv, kv,
             _const_spec(wxo.shape)]
    rider = make_rider(lambda i, j: i * nl + j, b * nl)
    body = functools.partial(_two_bodies, _attn_prompt_body, len(ins), 1, rider[0], len(rider[1]))
    return pl.pallas_call(
        body, grid=(b, nl),
        in_specs=specs + list(rider[2]),
        out_specs=[row] + list(rider[3]),
        out_shape=[jax.ShapeDtypeStruct((b, l, d), F32)] + list(rider[4]),
        compiler_params=_params("arbitrary", "arbitrary"),
        name="attn_prompt",
    )(*ins, *rider[1])


def _merge_q_body(x_ref, ys_ref, p_ref, wo1_ref, g2_ref, wq_ref, x1_ref, q_ref):
    x1 = x_ref[...] + _dot(ys_ref[...], wo1_ref[...]) + p_ref[...]
    x1_ref[...] = x1
    q_ref[...] = _dot(_rms(x1, g2_ref[...]).astype(BF16), wq_ref[...])


def _merge_q(x2d, ys, pm, wo1, g2, wq):
    t, d = x2d.shape
    return pl.pallas_call(
        _merge_q_body,
        out_shape=[jax.ShapeDtypeStruct((t, d), F32)] * 2,
        compiler_params=pltpu.CompilerParams(vmem_limit_bytes=VMEM_LIMIT),
        name="merge_q",
    )(x2d, ys, pm, wo1, g2, wq)


def _tile_view(kv):
    b, m, nh, hd = kv.shape
    nt = hd // LANES
    return kv.reshape(b, m, nh, nt, LANES).transpose(0, 1, 3, 2, 4).reshape(b, m, nt * nh, LANES)


def _tile_unview(o):
    b, rows, _ = o.shape
    nt = rows // X_HEADS
    return o.reshape(b, nt, X_HEADS, LANES).transpose(0, 2, 1, 3).reshape(b, rows * LANES)


def _attn_step_body(q_ref, k_ref, v_ref, o_ref):
    bt, nm, rows, _ = k_ref.shape
    hd = rows * LANES // X_HEADS
    for b in range(bt):
        r = jnp.sum(k_ref[b] * q_ref[b], axis=-1, keepdims=True)
        s = (r + pltpu.roll(r, X_HEADS, 1)) * (hd ** -0.5)
        e = jnp.exp(s - jnp.max(s, axis=0, keepdims=True))
        p = e / jnp.sum(e, axis=0, keepdims=True)
        o_ref[b] = jnp.sum(p * v_ref[b], axis=0)


def _attn_step_rider(q8, k8, v8, nsteps):
    nbt, rows, _ = q8.shape
    nm = k8.shape[1]
    assert rows == 2 * X_HEADS, "score assembly assumes two 128-lane tiles per head"
    assert nbt % nsteps == 0
    bt = nbt // nsteps
    qs = pl.BlockSpec((bt, rows, LANES), lambda i: (i, 0, 0))
    kv = pl.BlockSpec((bt, nm, rows, LANES), lambda i: (i, 0, 0, 0))
    return (_attn_step_body, (q8, k8, v8), (qs, kv, kv), (qs,),
            (jax.ShapeDtypeStruct((nbt, rows, LANES), F32),))


def _ffn_body(has_o, final, *refs):
    if has_o:
        x_ref, o_in_ref, wxo_ref, g3_ref, wg_ref, wu_ref, wd_ref, gf_ref, y_ref = refs
        x2 = x_ref[...] + _dot(o_in_ref[...].astype(BF16), wxo_ref[...])
    else:
        x_ref, g3_ref, wg_ref, wu_ref, wd_ref, gf_ref, y_ref = refs
        x2 = x_ref[...]
    hf = _rms(x2, g3_ref[...]).astype(BF16)
    ff = wg_ref.shape[1]
    x3 = x2
    for lo in range(0, ff, FF_CHUNK):
        hi = min(lo + FF_CHUNK, ff)
        act = (_silu(_dot(hf, wg_ref[:, lo:hi])) * _dot(hf, wu_ref[:, lo:hi])).astype(BF16)
        x3 = x3 + _dot(act, wd_ref[lo:hi, :])
    y_ref[...] = _rms(x3, gf_ref[...]) if final else x3


def _ffn(x2d, g3, wg, wu, wd, gf, final, o_in=None, wxo=None, rider=None):
    t, d = x2d.shape
    tm = min(TOKEN_TILE, t)
    row = pl.BlockSpec((tm, d), lambda i: (i, 0))
    has_o = o_in is not None
    ins = [x2d] + ([o_in, wxo] if has_o else []) + [g3, wg, wu, wd, gf]
    specs = [row] + ([row, _const_spec(wxo.shape)] if has_o else []) + [
        _const_spec(g3.shape), _const_spec(wg.shape), _const_spec(wu.shape), _const_spec(wd.shape),
        _const_spec(gf.shape)]
    body = functools.partial(_ffn_body, has_o, final)
    out_specs, out_shape = [row], [jax.ShapeDtypeStruct((t, d), F32)]
    if rider is not None:
        body = functools.partial(_two_bodies, body, len(ins), 1, rider[0], len(rider[1]))
        ins, specs = ins + list(rider[1]), specs + list(rider[2])
        out_specs, out_shape = out_specs + list(rider[3]), out_shape + list(rider[4])
    res = pl.pallas_call(
        body, grid=(t // tm,),
        in_specs=specs, out_specs=out_specs, out_shape=out_shape,
        compiler_params=_params("arbitrary"),
        name="ffn_o" if has_o else "ffn",
    )(*ins)
    return res[0] if rider is None else res


def _two_bodies(body_a, n_in_a, n_out_a, body_b, n_in_b, *refs):
    outs = refs[n_in_a + n_in_b:]
    body_a(*refs[:n_in_a], *outs[:n_out_a])
    body_b(*refs[n_in_a:n_in_a + n_in_b], *outs[n_out_a:])


def _merge_riders(ra, rb):
    body = functools.partial(_two_bodies, ra[0], len(ra[1]), len(ra[3]), rb[0], len(rb[1]))
    return (body,) + tuple(tuple(ra[k]) + tuple(rb[k]) for k in range(1, 5))


def _cast_body(*refs):
    n = len(refs) // 2
    for src, dst in zip(refs[:n], refs[n:]):
        dst[...] = src[...].astype(dst.dtype)


def _cast_rider(ws, step_of, nsteps):
    bf16_rows = 16
    specs, shapes = [], []
    for w in ws:
        r, c = w.shape
        k = next(k for k in (1, 2, 4, 8) if r % (nsteps // k) == 0 and (r // (nsteps // k)) % bf16_rows == 0)
        specs.append(pl.BlockSpec((r // (nsteps // k), c), lambda *g, k=k: (step_of(*g) // k, 0)))
        shapes.append(jax.ShapeDtypeStruct((r, c), BF16))
    return (_cast_body, tuple(ws), tuple(specs), tuple(specs), tuple(shapes))


def _layer_weights(i, ln1_g, w_in, conv_w, conv_b, dt_bias, a_log, d_ssd, gn_g, lam_re, lam_im, log_step,
                   b_re, b_im, c_re, c_im, d_s5, w_glu, b_glu, w_out, ln2_g, w_xq, w_xk, w_xv, w_xo, ln3_g,
                   w_gate, w_up, w_down):
    d = w_in.shape[1]
    n_heads = dt_bias.shape[1]
    dssd = n_heads * SSD_HEAD_DIM
    conv_dim = conv_w.shape[2]
    o1, o2, o3 = dssd, dssd + conv_dim, dssd + conv_dim + n_heads
    win = w_in[i]
    w = {}
    w["ln1"] = ln1_g[i].reshape(1, d)
    w["win"] = win.astype(BF16)
    w["wu_s5"] = w["win"][:, o3:]
    w["cw"] = conv_w[i]
    w["cb"] = conv_b[i].reshape(1, conv_dim)
    w["dtb"] = jnp.pad(dt_bias[i], (0, LANES - n_heads)).reshape(1, LANES)
    w["alog"] = jnp.pad(a_log[i], (0, LANES - n_heads)).reshape(1, LANES)
    w["dexp"] = jnp.repeat(d_ssd[i], SSD_HEAD_DIM).reshape(1, dssd)
    w["gn"] = gn_g[i].reshape(1, dssd)
    w["s5"] = (lam_re[i], lam_im[i], log_step[i], b_re[i], b_im[i], c_re[i], c_im[i])
    ds5 = d_s5.shape[1]
    w["dcol"] = d_s5[i].reshape(ds5, 1)
    w["wglut"] = w_glu[i].T.astype(BF16)
    w["bglu"] = b_glu[i].reshape(ds5, 1)
    w["wo1"] = w_out[i, :dssd].astype(BF16)
    w["wo2"] = w_out[i, dssd:].astype(BF16)
    w["ln2"] = ln2_g[i].reshape(1, d)
    w["wq"] = w_xq[i].astype(BF16)
    w["wk"] = w_xk[i].astype(BF16)
    w["wv"] = w_xv[i].astype(BF16)
    w["wxo"] = w_xo[i].astype(BF16)
    w["ln3"] = ln3_g[i].reshape(1, d)
    w["ffn_f32"] = (w_gate[i], w_up[i], w_down[i])
    return w


def _rows_to_state(s):
    p = s.shape[-1] // 2
    st = s.transpose(1, 0, 2)
    return st[..., :p], st[..., p:]


def kernel(x_prompt, x_sample, mem_prompt, state_conv, state_ssm, state_s5_re, state_s5_im, cache_mem_k, cache_mem_v, ln1_g, w_in, conv_w, conv_b, dt_bias, a_log, d_ssd, gn_g, lam_re, lam_im, log_step, b_re, b_im, c_re, c_im, d_s5, w_glu, b_glu, w_out, ln2_g, w_xq, w_xk, w_xv, w_xo, ln3_g, w_gate, w_up, w_down, final_g):
    bp, seq, d = x_prompt.shape
    bs = x_sample.shape[0]
    depth = w_in.shape[0]
    nm = mem_prompt.shape[1]
    ng, ns = lam_re.shape[1], lam_re.shape[2]
    gf = final_g.reshape(1, d)
    hp = x_prompt
    hs = x_sample.reshape(bs, d)
    outs = {k: [] for k in ("conv_p", "ssm_p", "re_p", "im_p", "mk_p", "mv_p", "conv_s", "ssm_s", "re_s", "im_s")}
    yp = ys_out = None
    for i in range(depth):
        w = _layer_weights(i, ln1_g, w_in, conv_w, conv_b, dt_bias, a_log, d_ssd, gn_g, lam_re, lam_im, log_step,
                           b_re, b_im, c_re, c_im, d_s5, w_glu, b_glu, w_out, ln2_g, w_xq, w_xk, w_xv, w_xo, ln3_g,
                           w_gate, w_up, w_down)
        last = i == depth - 1
        attn_steps = bp * (seq // min(TOKEN_TILE, seq))
        y_ssd, conv_p, ssm_p, zs, xbcs, dts = _ssd_prompt(hp, w["ln1"], w["win"], w["cw"], w["cb"], w["dtb"],
                                                          w["alog"], w["dexp"], w["gn"], hs, bs // attn_steps)
        q = S5_Q
        kv_rider = functools.partial(_kv_proj_rider, mem_prompt.reshape(bp * nm, d), w["wk"], w["wv"])
        (tt, wt, zt, a1, a2, a11, a21), (mk8, mv8, mk, mv) = _s5_prep(q, *w["s5"], kv_rider)
        mk = mk.reshape(bp, nm, d)
        mv = mv.reshape(bp, nm, d)
        s0 = jnp.zeros((ng, bp, 2 * ns), F32)
        pperm, sfin = _s5_mix(q, bp, hp, w["ln1"], w["wu_s5"],tt, wt, zt, a1, a2, s0, w["dcol"], w["wglut"],
                              w["bglu"], w["wo2"])
        pm = pperm.reshape(q, seq // q, bp, d).transpose(2, 1, 0, 3).reshape(bp, seq, d)
        re_p, im_p = _rows_to_state(sfin)
        riders = lambda step_of, n: _merge_riders(
            _ssd_step_rider(zs, xbcs, dts, state_conv[i], state_ssm[i], w["cw"], w["cb"], w["dtb"], w["alog"],
                            w["dexp"], w["gn"], step_of, n),
            _cast_rider(w["ffn_f32"], step_of, n))
        x2, ys_ssd, conv_s, ssm_s, w["wg"], w["wu"], w["wdn"] = _attn_prompt(
            hp, y_ssd, pm, w["wo1"], w["ln2"], w["wq"], mk, mv, w["wxo"], riders)
        ys_ssd = ys_ssd.reshape(bs, -1)
        tt1, wt1, zt1 = _s5_single_token_mats(q, tt, wt, zt)
        s0s = jnp.stack([state_s5_re[i].reshape(bs, ng * ns), state_s5_im[i].reshape(bs, ng * ns)])
        ps, sfin_s = _s5_mix(1, bs, hs.reshape(bs, 1, d), w["ln1"], w["wu_s5"],tt1, wt1, zt1, a11, a21, s0s,
                             w["dcol"], w["wglut"], w["bglu"], w["wo2"])
        re_s, im_s = sfin_s[0].reshape(bs, ng, ns), sfin_s[1].reshape(bs, ng, ns)
        x1s, qs = _merge_q(hs, ys_ssd, ps.reshape(bs, d), w["wo1"], w["ln2"], w["wq"])
        attn_rider = _attn_step_rider(_tile_view(qs.reshape(bs, 1, X_HEADS, d // X_HEADS))[:, 0],
                                      _tile_view(cache_mem_k[i]), _tile_view(cache_mem_v[i]),
                                      (bp * seq) // min(TOKEN_TILE, bp * seq))
        hp, o_s = _ffn(x2.reshape(bp * seq, d), w["ln3"], w["wg"], w["wu"], w["wdn"], gf, last, rider=attn_rider)
        hp = hp.reshape(bp, seq, d)
        hs = _ffn(x1s, w["ln3"], w["wg"], w["wu"], w["wdn"], gf, last, o_in=_tile_unview(o_s), wxo=w["wxo"])
        for k, v in (("conv_p", conv_p), ("ssm_p", ssm_p), ("re_p", re_p), ("im_p", im_p),
                     ("mk_p", _tile_unview_kv(mk8, bp, nm)), ("mv_p", _tile_unview_kv(mv8, bp, nm)),
                     ("conv_s", conv_s), ("ssm_s", ssm_s), ("re_s", re_s), ("im_s", im_s)):
            outs[k].append(v)
    st = lambda k: jnp.stack(outs[k])
    return (hp, hs.reshape(bs, 1, d), st("conv_p"), st("ssm_p"), st("re_p"), st("im_p"), st("mk_p"), st("mv_p"),
            st("conv_s"), st("ssm_s"), st("re_s"), st("im_s"))
```

```python
import functools
import math

import jax
import jax.numpy as jnp
from jax import lax
from jax.experimental import pallas as pl
from jax.experimental.pallas import tpu as pltpu

F32 = jnp.float32
BF16 = jnp.bfloat16
EPS = 1e-6

LANES = 128
VMEM_LIMIT = 60 * 1024 * 1024

SSD_HEAD_DIM = 64
SSD_STATE = 128
SSD_GROUPS = 2
SSD_CHUNK = 128
CONV_W = 4
S5_CH = 16
S5_STATE = 64
S5_Q = 16
S5_GROUP_BATCH = 32
S5_SLICE_TOKENS = 512
X_HEADS = 4
TOKEN_TILE = 512
PROJ_PIECE = 256
SSD_SEQS_PER_STEP = 4
FF_CHUNK = 1024


def _const_spec(shape):
    nd = len(shape)
    return pl.BlockSpec(shape, lambda *_: (0,) * nd, pipeline_mode=pl.Buffered(1))


def _params(*sem):
    return pltpu.CompilerParams(dimension_semantics=sem, vmem_limit_bytes=VMEM_LIMIT)


def _rms(x, g):
    return x * lax.rsqrt(jnp.mean(x * x, axis=-1, keepdims=True) + EPS) * g


def _sigmoid(x):
    return 1.0 / (1.0 + jnp.exp(-x))


def _silu(x):
    return x * _sigmoid(x)


def _gelu_tanh(x):
    return 0.5 * x * (1.0 + jnp.tanh(math.sqrt(2.0 / math.pi) * (x + 0.044715 * (x * x * x))))


def _softplus(x):
    return jnp.maximum(x, 0.0) + jnp.log1p(jnp.exp(-jnp.abs(x)))


def _dot(a, b):
    return jnp.dot(a, b, preferred_element_type=F32)


def _dot_nt(a, b):
    return lax.dot_general(a, b, (((1,), (1,)), ((), ())), preferred_element_type=F32)


def _dot_tn(a, b):
    return lax.dot_general(a, b, (((0,), (0,)), ((), ())), preferred_element_type=F32)


def _proj_windows(d_ssd, conv_dim):
    assert d_ssd % LANES == 0 and conv_dim % LANES == 0
    return (0, d_ssd), (d_ssd, d_ssd + conv_dim), (d_ssd + conv_dim, d_ssd + conv_dim + LANES)


def _expand_heads(v, n_heads):
    rows = v.shape[0]
    lane = lax.broadcasted_iota(jnp.int32, (rows, LANES), 1)
    pieces = [jnp.where(lane < SSD_HEAD_DIM, v[:, 2 * j:2 * j + 1], v[:, 2 * j + 1:2 * j + 2])
              for j in range(n_heads // 2)]
    return jnp.concatenate(pieces, axis=1)


def _ssd_chunk_math(z, x, dt_raw, last_chunk, cw_ref, cb_ref, dtb_ref, alog_ref, dexp_ref, gn_ref, e2_ref,
                    y_ref, conv_ref, ssm_ref, xpad, state, side_jobs=()):
    jobs = list(side_jobs)

    def side(n=1):
        for _ in range(n):
            if jobs:
                jobs.pop(0)()

    q = SSD_CHUNK
    d_ssd = z.shape[-1]
    n_heads = d_ssd // SSD_HEAD_DIM
    hpg = n_heads // SSD_GROUPS
    gw = hpg * SSD_HEAD_DIM
    nct = xpad.shape[0]
    xt = d_ssd // LANES

    for t in range(nct):
        xpad[t, 8:8 + q, :] = x[:, t * LANES:(t + 1) * LANES]
    cw = cw_ref[...]
    conv = (cw[3] * xpad[:, 8:8 + q, :] + cw[2] * xpad[:, 7:7 + q, :] + cw[1] * xpad[:, 6:6 + q, :]
            + cw[0] * xpad[:, 5:5 + q, :] + cb_ref[...])
    xpad[:, 0:8, :] = xpad[:, q:q + 8, :]
    xact = _silu(conv)
    xs = jnp.concatenate([xact[t] for t in range(xt)], axis=1)
    bm = xact[xt:xt + SSD_GROUPS].astype(BF16)
    cm = xact[xt + SSD_GROUPS:].astype(BF16)
    side()

    dt = _softplus(dt_raw + dtb_ref[...])
    a = -jnp.exp(alog_ref[...])
    row = lax.broadcasted_iota(jnp.int32, (q, q), 0)
    col = lax.broadcasted_iota(jnp.int32, (q, q), 1)
    causal = row >= col
    tri = jnp.where(causal, 1.0, 0.0).astype(F32)
    acum = jnp.dot(tri, dt * a, precision=lax.Precision.HIGHEST, preferred_element_type=F32)
    acum_t = acum.T
    last = acum[q - 1:q, :]

    def expand(v):
        hi = v.astype(BF16)
        lo = (v - hi.astype(F32)).astype(BF16)
        return _dot(jnp.concatenate([hi, lo], axis=1), e2_ref[...])

    dt_e = expand(dt)
    ea_e = expand(jnp.exp(acum))
    dend_e = expand(jnp.exp(last - acum))

    dtx = xs * dt_e
    dtx_b = dtx.astype(BF16)
    xdec_b = (dtx * dend_e).astype(BF16)
    side()

    lane = lax.broadcasted_iota(jnp.int32, (q, LANES), 1)
    zero_b = jnp.zeros((q, LANES), BF16)
    y_tiles = []
    for g in range(SSD_GROUPS):
        bg = bm[g]
        cg = cm[g]
        cb = _dot_nt(cg, bg)
        gs = slice(g * gw, (g + 1) * gw)
        st_g = state[:, gs]
        y_off = _dot(cg, st_g.astype(BF16)) * ea_e[:, gs]
        for pr in range(hpg // 2):
            tile = g * (hpg // 2) + pr
            ms = []
            for h in (2 * tile, 2 * tile + 1):
                seg = acum[:, h:h + 1] - acum_t[h:h + 1, :]
                lmat = jnp.exp(jnp.where(causal, seg, -jnp.inf))
                ms.append((cb * lmat).astype(BF16))
            dtile = dtx_b[:, tile * LANES:(tile + 1) * LANES]
            rhs = jnp.concatenate([jnp.where(lane < SSD_HEAD_DIM, dtile, zero_b),
                                   jnp.where(lane >= SSD_HEAD_DIM, dtile, zero_b)], axis=0)
            y_tiles.append(_dot(jnp.concatenate(ms, axis=1), rhs)
                           + y_off[:, pr * LANES:(pr + 1) * LANES])
            if pr % 2 == 1:
                side()
        new = _dot_tn(bg, xdec_b[:, gs])
        state[:, gs] = st_g * ea_e[q - 1:q, gs] + new
    y = jnp.concatenate(y_tiles, axis=1) + dexp_ref[...] * xs
    y = y * _silu(z)
    ms = jnp.mean(y * y, axis=-1, keepdims=True)
    side(len(jobs))
    y_ref[...] = (y * lax.rsqrt(ms + EPS) * gn_ref[...]).astype(y_ref.dtype)

    @pl.when(last_chunk)
    def _():
        conv_ref[...] = x[q - (CONV_W - 1):q, :]
        ssm_ref[...] = state[...].T.reshape(ssm_ref.shape)


def _ssd_prompt_body(wins, x_ref, g_ref, w_ref, cw_ref, cb_ref, dtb_ref, alog_ref, dexp_ref, gn_ref,
                     e2_ref, xs_ref, y_ref, conv_ref, ssm_ref, zs_ref, xbcs_ref, dts_ref,
                     za, xa, da, zb, xb, db, xpad, state):
    i = pl.program_id(0)
    j = pl.program_id(1)
    nc = pl.num_programs(1) - 1
    ns, q, d = x_ref.shape

    @pl.when(jnp.logical_and(i == 0, j == 0))
    def _():
        for r in (zb, xb, db):
            r[...] = jnp.zeros(r.shape, F32)
        hs = _rms(xs_ref[...], g_ref[...]).astype(BF16)
        for (lo, hi), o_ref in zip(wins, (zs_ref, xbcs_ref, dts_ref)):
            o_ref[...] = _dot(hs, w_ref[:, lo:hi]).reshape(o_ref.shape)

    @pl.when(j <= 1)
    def _():
        xpad[:, :, 0:8, :] = jnp.zeros(xpad.shape[:2] + (8, LANES), F32)
        state[...] = jnp.zeros(state.shape, F32)

    def step(wr, rd):
        xin = x_ref[...].reshape(ns * q, d)
        msx = jnp.mean(xin * xin, axis=-1, keepdims=True)
        hbox = []

        def proj_piece(dst, base, lo, hi):
            def run():
                if not hbox:
                    hbox.append((xin * lax.rsqrt(msx + EPS) * g_ref[...]).astype(BF16))
                dst[:, lo:hi] = _dot(hbox[0], w_ref[:, base + lo:base + hi])
            return run

        pieces = []
        for dst, (w_lo, w_hi) in ((wr[0], wins[0]), (wr[2], wins[2]), (wr[1], wins[1])):
            n = w_hi - w_lo
            pieces += [proj_piece(dst, w_lo, lo, min(lo + PROJ_PIECE, n)) for lo in range(0, n, PROJ_PIECE)]
        per = -(-len(pieces) // ns)
        for s in range(ns):
            rows = slice(s * q, (s + 1) * q)
            _ssd_chunk_math(rd[0][rows, :], rd[1][rows, :], rd[2][rows, :], j == nc, cw_ref, cb_ref, dtb_ref,
                            alog_ref, dexp_ref, gn_ref, e2_ref, y_ref.at[s], conv_ref.at[s], ssm_ref.at[s],
                            xpad.at[s], state.at[s], pieces[s * per:(s + 1) * per])

    @pl.when(j % 2 == 0)
    def _():
        step((za, xa, da), (zb, xb, db))

    @pl.when(j % 2 == 1)
    def _():
        step((zb, xb, db), (za, xa, da))


def _ssd_prompt(x, ln_g, w_in, cw, cb, dtb, alog, dexp, gn, xs, group):
    b, l, d = x.shape
    ts = xs.shape[0]
    conv_dim = cw.shape[1]
    d_ssd = conv_dim - 2 * SSD_GROUPS * SSD_STATE
    wins = _proj_windows(d_ssd, conv_dim)
    n_heads = d_ssd // SSD_HEAD_DIM
    q = SSD_CHUNK
    nc = l // q
    assert SSD_STATE == LANES and conv_dim == d_ssd + 2 * SSD_GROUPS * SSD_STATE
    nct = conv_dim // LANES
    cw4 = cw.reshape(CONV_W, nct, 1, LANES)
    cb3 = cb.reshape(nct, 1, LANES)
    e2 = (jnp.arange(2 * LANES)[:, None] % LANES == jnp.arange(d_ssd)[None, :] // SSD_HEAD_DIM).astype(BF16)
    ns = SSD_SEQS_PER_STEP if b % SSD_SEQS_PER_STEP == 0 else 1
    slot = lambda n: pltpu.VMEM((ns * q, n), F32)
    return pl.pallas_call(
        functools.partial(_ssd_prompt_body, wins),
        grid=(b // ns, nc + 1),
        in_specs=[pl.BlockSpec((ns, q, d), lambda i, j: (i, jnp.minimum(j, nc - 1), 0)),
                  _const_spec(ln_g.shape), _const_spec(w_in.shape),
                  _const_spec(cw4.shape), _const_spec(cb3.shape), _const_spec(dtb.shape), _const_spec(alog.shape),
                  _const_spec(dexp.shape), _const_spec(gn.shape), _const_spec(e2.shape), _const_spec(xs.shape)],
        out_specs=[pl.BlockSpec((ns, q, d_ssd), lambda i, j: (i, jnp.maximum(j - 1, 0), 0)),
                   pl.BlockSpec((ns, CONV_W - 1, conv_dim), lambda i, j: (i, 0, 0)),
                   pl.BlockSpec((ns, n_heads, SSD_HEAD_DIM, SSD_STATE), lambda i, j: (i, 0, 0, 0))]
        + [pl.BlockSpec((ts // group, group, hi - lo), lambda i, j: (0, 0, 0)) for lo, hi in wins],
        out_shape=[jax.ShapeDtypeStruct((b, l, d_ssd), BF16),
                   jax.ShapeDtypeStruct((b, CONV_W - 1, conv_dim), F32),
                   jax.ShapeDtypeStruct((b, n_heads, SSD_HEAD_DIM, SSD_STATE), F32)]
        + [jax.ShapeDtypeStruct((ts // group, group, hi - lo), F32) for lo, hi in wins],
        scratch_shapes=[slot(d_ssd), slot(conv_dim), slot(LANES), slot(d_ssd), slot(conv_dim), slot(LANES),
                        pltpu.VMEM((ns, nct, q + 8, LANES), F32), pltpu.VMEM((ns, SSD_STATE, d_ssd), F32)],
        compiler_params=_params("arbitrary", "arbitrary"),
        name="ssd_prompt",
    )(x, ln_g, w_in, cw4, cb3, dtb, alog, dexp, gn, e2, xs)


def _ssd_step_body(z_ref, xbc_ref, dt_ref, cs_ref, st_ref, cw_ref, cb_ref, dtb_ref, alog_ref, dexp_ref, gn_ref,
                   y_ref, conv_ref, ssm_ref):
    bt, d_ssd = z_ref.shape
    n_heads = d_ssd // SSD_HEAD_DIM
    hpg = n_heads // SSD_GROUPS
    x = xbc_ref[...]
    cw = cw_ref[...]
    cs = [cs_ref[:, k, :] for k in range(CONV_W - 1)]
    conv = cw[0:1] * cs[0] + cw[1:2] * cs[1] + cw[2:3] * cs[2] + cw[3:4] * x + cb_ref[...]
    conv_ref[:, 0, :] = cs[1]
    conv_ref[:, 1, :] = cs[2]
    conv_ref[:, 2, :] = x
    xact = _silu(conv)
    xs = xact[:, :d_ssd]
    bm = xact[:, d_ssd:d_ssd + SSD_GROUPS * SSD_STATE]
    cm = xact[:, d_ssd + SSD_GROUPS * SSD_STATE:]
    dt = _softplus(dt_ref[...] + dtb_ref[...])
    da = jnp.exp(dt * (-jnp.exp(alog_ref[...])))
    dtx = xs * _expand_heads(dt, n_heads)
    pad = jnp.zeros((LANES - bt, d_ssd), F32)
    to_cols = lambda v: jnp.concatenate([v, pad], axis=0).T
    dtx_t = to_cols(dtx)
    da_t = to_cols(_expand_heads(da, n_heads))
    lane = lax.broadcasted_iota(jnp.int32, (d_ssd, LANES), 1)
    gw = hpg * SSD_HEAD_DIM
    rows_of = lambda v, b: jnp.concatenate(
        [jnp.broadcast_to(v[b:b + 1, g * SSD_STATE:(g + 1) * SSD_STATE], (gw, SSD_STATE)) for g in range(SSD_GROUPS)],
        axis=0)
    y_t = jnp.zeros((d_ssd, LANES), F32)
    for b in range(bt):
        s0 = st_ref[b].reshape(d_ssd, SSD_STATE)
        s_new = s0 * da_t[:, b:b + 1] + dtx_t[:, b:b + 1] * rows_of(bm, b)
        ssm_ref[b] = s_new.reshape(n_heads, SSD_HEAD_DIM, SSD_STATE)
        ycol = jnp.sum(s_new * rows_of(cm, b), axis=1, keepdims=True)
        y_t = jnp.where(lane == b, ycol, y_t)
    y = y_t.T[:bt, :] + dexp_ref[...] * xs
    y = y * _silu(z_ref[...])
    y_ref[...] = _rms(y, gn_ref[...]).astype(y_ref.dtype)


def _ssd_step_rider(z, xbc, dt, conv_state, ssm_state, cw, cb, dtb, alog, dexp, gn, step_of, nsteps):
    _, bt, d_ssd = z.shape
    conv_dim = xbc.shape[-1]
    n_heads = d_ssd // SSD_HEAD_DIM
    assert z.shape[0] == nsteps and conv_state.shape[0] == nsteps * bt
    row = lambda n: pl.BlockSpec((None, bt, n), lambda *g: (step_of(*g), 0, 0))
    cs_spec = pl.BlockSpec((bt, CONV_W - 1, conv_dim), lambda *g: (step_of(*g), 0, 0))
    st_spec = pl.BlockSpec((bt, n_heads, SSD_HEAD_DIM, SSD_STATE), lambda *g: (step_of(*g), 0, 0, 0))
    cst = lambda a: pl.BlockSpec(a.shape, lambda *g: (0,) * a.ndim, pipeline_mode=pl.Buffered(1))
    ins = (z, xbc, dt, conv_state, ssm_state, cw, cb, dtb, alog, dexp, gn)
    specs = (row(d_ssd), row(conv_dim), row(LANES), cs_spec, st_spec, cst(cw), cst(cb), cst(dtb), cst(alog),
             cst(dexp), cst(gn))
    out_shapes = (jax.ShapeDtypeStruct((nsteps, bt, d_ssd), BF16),
                  jax.ShapeDtypeStruct(conv_state.shape, F32),
                  jax.ShapeDtypeStruct(ssm_state.shape, F32))
    return (_ssd_step_body, ins, specs, (row(d_ssd), cs_spec, st_spec), out_shapes)


def _split_bf16(x):
    hi = x.astype(BF16)
    lo = (x - hi.astype(F32)).astype(BF16)
    return hi, lo


def _s5_prep_body(q, ls_ref, lrp_ref, lip_ref, lr2_ref, li2_ref, br_ref, bi_ref, ccat_ref, ca_ref, cb_ref, rep_ref,
                  tt_ref, wt_ref, zt_ref, a1_ref, a2_ref, a1s_ref, a2s_ref):
    ng = ls_ref.shape[0]
    w = q * S5_CH
    step = jnp.exp(ls_ref[...])
    lr, li = lrp_ref[...], lip_ref[...]
    mag = jnp.exp(lr * step)
    ang = li * step
    lbr = mag * jnp.cos(ang)
    lbi = mag * jnp.sin(ang)
    den = lr * lr + li * li
    kr = ((lbr - 1.0) * lr + lbi * li) / den
    ki = (lbi * lr - (lbr - 1.0) * li) / den
    br, bi = br_ref[...], bi_ref[...]
    bbr = kr * br - ki * bi
    bbi = kr * bi + ki * br
    if q > 1:
        np_ = br.shape[1]
        rep = lambda v: jnp.dot(v.reshape(ng * np_, S5_CH), rep_ref[...], precision=lax.Precision.HIGHEST,
                                preferred_element_type=F32).reshape(ng, np_, w)
        bbr, bbi = rep(bbr), rep(bbi)
    if q == 1:
        ball = jnp.concatenate([bbr, bbi], axis=1)
    else:
        d = lax.broadcasted_iota(jnp.int32, (1, 1, w), 2) // S5_CH
        fr, fi = lbr, lbi
        pr = pi = None
        for b in range((q - 1).bit_length()):
            bit = ((d >> b) & 1) == 1
            sr, si = jnp.where(bit, fr, 1.0), jnp.where(bit, fi, 0.0)
            pr, pi = (sr, si) if pr is None else (pr * sr - pi * si, pr * si + pi * sr)
            fr, fi = fr * fr - fi * fi, 2.0 * (fr * fi)
        ball = jnp.concatenate([pr * bbr - pi * bbi, pr * bbi + pi * bbr], axis=1)
    wt_ref[...] = ball.astype(BF16)

    lane3 = lax.broadcasted_iota(jnp.int32, (1, 1, 2 * S5_STATE), 2)
    first = lane3 < S5_STATE
    csign = jnp.where(first, ccat_ref[...], -ccat_ref[...])
    ch, cl = _split_bf16(csign)
    bh, bl = _split_bf16(ball)
    bdot = lambda x, y: lax.dot_general(x, y, (((2,), (1,)), ((0,), (0,))), preferred_element_type=F32)
    kall = bdot(ch, bh) + bdot(ch, bl) + bdot(cl, bh)
    k2 = kall.reshape(ng * S5_CH, w)
    lane2 = lax.broadcasted_iota(jnp.int32, (ng * S5_CH, w), 1)
    for t in range(q):
        sh = (q - 1 - t) * S5_CH
        r = pltpu.roll(k2, sh, 1) if sh else k2
        r = jnp.where(lane2 >= sh, r, 0.0)
        tt_ref[:, t * S5_CH:(t + 1) * S5_CH, :] = r.reshape(ng, S5_CH, w).astype(BF16)

    lr2, li2 = lr2_ref[...], li2_ref[...]
    t1 = (lax.broadcasted_iota(jnp.int32, (1, q, 1), 1) + 1).astype(F32)
    zm = jnp.exp(t1 * (lr2 * step))
    za = t1 * (li2 * step)
    zr = zm * jnp.cos(za)
    zi = zm * jnp.sin(za)
    ca, cb = ca_ref[...], cb_ref[...]
    for t in range(q):
        prt = zr[:, t:t + 1, :]
        pit = zi[:, t:t + 1, :]
        zt = jnp.where(first, ca * prt - cb * pit, -(ca * pit) - cb * prt)
        zt_ref[:, t * S5_CH:(t + 1) * S5_CH, :] = zt.astype(BF16)
    qf = float(q)
    mq = jnp.exp(qf * (lr2 * step))
    aq = qf * (li2 * step)
    ar = mq * jnp.cos(aq)
    ai = mq * jnp.sin(aq)
    a1_ref[...] = ar
    a2_ref[...] = jnp.where(first, -ai, ai)
    a1s_ref[...] = zr[:, 0:1, :]
    a2s_ref[...] = jnp.where(first, -zi[:, 0:1, :], zi[:, 0:1, :])


def _s5_prep(q, lam_re, lam_im, log_step, b_re, b_im, c_re, c_im, make_rider):
    ng, p = lam_re.shape
    w = q * S5_CH
    ls = log_step.reshape(ng, 1, 1)
    lrp = lam_re.reshape(ng, p, 1)
    lip = lam_im.reshape(ng, p, 1)
    lr2 = jnp.concatenate([lam_re, lam_re], axis=-1).reshape(ng, 1, 2 * p)
    li2 = jnp.concatenate([lam_im, lam_im], axis=-1).reshape(ng, 1, 2 * p)
    rep = (jnp.arange(S5_CH)[:, None] == jnp.arange(w)[None, :] % S5_CH).astype(F32)
    ccat = jnp.concatenate([c_re, c_im], axis=-1)
    ca = jnp.concatenate([c_re, c_re], axis=-1)
    cb = jnp.concatenate([c_im, c_im], axis=-1)
    gb = ng if q == 1 else 16
    blk = lambda a, b: pl.BlockSpec((gb, a, b), lambda i: (i, 0, 0))
    ins = (ls, lrp, lip, lr2, li2, b_re, b_im, ccat, ca, cb, rep)
    in_specs = [blk(1, 1), blk(p, 1), blk(p, 1), blk(1, 2 * p), blk(1, 2 * p), blk(p, S5_CH), blk(p, S5_CH),
                blk(S5_CH, 2 * p), blk(S5_CH, 2 * p), blk(S5_CH, 2 * p), _const_spec(rep.shape)]
    out_specs = [blk(w, w), blk(2 * p, w), blk(w, 2 * p)] + [blk(1, 2 * p)] * 4
    out_shape = [jax.ShapeDtypeStruct((ng, w, w), BF16), jax.ShapeDtypeStruct((ng, 2 * p, w), BF16),
                 jax.ShapeDtypeStruct((ng, w, 2 * p), BF16)] + [jax.ShapeDtypeStruct((ng, 1, 2 * p), F32)] * 4
    body = functools.partial(_s5_prep_body, q)
    rider = make_rider(ng // gb)
    body = functools.partial(_two_bodies, body, len(ins), len(out_specs), rider[0], len(rider[1]))
    res = pl.pallas_call(
        body,
        grid=(ng // gb,),
        in_specs=in_specs + list(rider[2]),
        out_specs=out_specs + list(rider[3]),
        out_shape=out_shape + list(rider[4]),
        compiler_params=_params("arbitrary"),
        name=f"s5_prep_q{q}",
    )(*ins, *rider[1])
    return res[:len(out_specs)], res[len(out_specs):]


def _s5_single_token_mats(q, tt, wt, zt):
    c = S5_CH
    return tt[:, :c, (q - 1) * c:], wt[:, :, :c], zt[:, :c, :]


def _s5_mix_body(q, nb, nkb, npi, npt, perm, x_ref, g_ref, wu_ref, tt_ref, wt_ref, zt_ref, a1_ref, a2_ref, s0_ref,
                 dcol_ref, wglut_ref, bglu_ref, wo_ref, p_ref, sfin_ref, ut, yt, carry, wut):
    ng = tt_ref.shape[0]
    nk = nkb // nb
    qs = q // npt
    ch = qs * nkb
    tile = pl.program_id(0)
    ph = pl.program_id(1)

    @pl.when(jnp.logical_and(tile == 0, ph == 0))
    def _():
        if perm:
            carry[...] = s0_ref[...]
        else:
            lane_s = lax.broadcasted_iota(jnp.int32, (nb, 2 * S5_STATE), 1)
            for pr in range(ng // 2):
                cols = slice(pr * 2 * S5_STATE, (pr + 1) * 2 * S5_STATE)
                re_t, im_t = s0_ref[0, :, cols], s0_ref[1, :, cols]
                carry[2 * pr] = jnp.where(lane_s < S5_STATE, re_t, pltpu.roll(im_t, S5_STATE, 1))
                carry[2 * pr + 1] = jnp.where(lane_s < S5_STATE, pltpu.roll(re_t, S5_STATE, 1), im_t)
        cw_ = 256
        for c in range(0, wu_ref.shape[1], cw_):
            wut[c:c + cw_, :] = wu_ref[:, c:c + cw_].T

    def rows_of(r):
        if not perm:
            return x_ref[r]
        nseq, nblk, ndt, nr, _ = x_ref.shape
        flat = x_ref.reshape(nseq * nblk * ndt * nr, LANES)

        return jnp.concatenate(
            [jnp.concatenate([flat[pl.ds((k * ndt + dt) * nr + r, nseq, stride=nblk * ndt * nr), :]
                              for dt in range(ndt)], axis=1) for k in range(nblk)], axis=0)

    def project(j):
        step = 2 if qs % 2 == 0 else 1
        r0 = (j * qs) % (q // npi)
        for r in range(0, qs, step):
            xin = jnp.concatenate([rows_of(r0 + r + s) for s in range(step)], axis=0)
            u = _dot_nt(wut[...], _rms(xin, g_ref[...]).astype(BF16))
            sl = slice((j * qs + r) * nkb, (j * qs + r + step) * nkb)
            ut[:, sl] = u.astype(BF16)
            yt[:, sl] = dcol_ref[...] * u

    def finish(j):
        gt = _gelu_tanh(yt[:, j * ch:(j + 1) * ch])
        gate = _dot(wglut_ref[...], gt.astype(BF16)) + bglu_ref[...]
        y5 = (gt * _sigmoid(gate)).astype(BF16)
        p_ref[...] = _dot_tn(y5, wo_ref[...]).reshape(p_ref.shape)

    spp = npt // npi
    for hh in range(npi):
        @pl.when(ph == hh)
        def _():
            for j in range(hh * spp, (hh + 1) * spp):
                project(j)

    @pl.when(ph == npi - 1)
    def _():
        gu = S5_GROUP_BATCH
        bdot = lambda a, b: lax.dot_general(a, b, (((2,), (1,)), ((0,), (0,))), preferred_element_type=F32)
        bdot_nt = lambda a, b: lax.dot_general(a, b, (((2,), (2,)), ((0,), (0,))), preferred_element_type=F32)

        def groups(i, _):
            g0 = pl.multiple_of(i * gu, gu)
            r0 = pl.multiple_of(i * (gu * S5_CH), gu * S5_CH)
            gsl = pl.ds(g0, gu)
            rows = ut[pl.ds(r0, gu * S5_CH), :].reshape(gu, S5_CH, q * nkb)
            ugt = jnp.concatenate([rows[:, :, (q - 1 - j) * nkb:(q - j) * nkb] for j in range(q)], axis=1)
            y = bdot(tt_ref[gsl], ugt)
            vt = bdot(wt_ref[gsl], ugt)
            v = jnp.swapaxes(vt, 1, 2)
            v_sw = jnp.swapaxes(jnp.concatenate([vt[:, S5_STATE:], vt[:, :S5_STATE]], axis=1), 1, 2)
            a1 = a1_ref[gsl]
            a2 = a2_ref[gsl]
            s = carry[gsl]
            s_sw = pltpu.roll(s.reshape(gu * nb, 2 * S5_STATE), S5_STATE, 1).reshape(s.shape)
            prev = []
            for k in range(nk):
                prev.append(s)
                s, s_sw = (a1 * s + a2 * s_sw + v[:, k * nb:(k + 1) * nb, :],
                           a1 * s_sw - a2 * s + v_sw[:, k * nb:(k + 1) * nb, :])
            carry[gsl] = s
            sprev = jnp.concatenate(prev, axis=1).astype(BF16)
            y = y + bdot_nt(zt_ref[gsl], sprev)
            for t in range(q):
                yt[pl.ds(r0, gu * S5_CH), t * nkb:(t + 1) * nkb] += (
                    y[:, t * S5_CH:(t + 1) * S5_CH, :].reshape(gu * S5_CH, nkb))
            return 0

        lax.fori_loop(0, ng // gu, groups, 0)
        if perm:
            sfin_ref[...] = carry[...]
        else:
            lane_s = lax.broadcasted_iota(jnp.int32, (nb, 2 * S5_STATE), 1)
            for pr in range(ng // 2):
                cols = slice(pr * 2 * S5_STATE, (pr + 1) * 2 * S5_STATE)
                c0, c1 = carry[2 * pr], carry[2 * pr + 1]
                sfin_ref[0, :, cols] = jnp.where(lane_s < S5_STATE, c0, pltpu.roll(c1, S5_STATE, 1))
                sfin_ref[1, :, cols] = jnp.where(lane_s < S5_STATE, pltpu.roll(c0, S5_STATE, 1), c1)

    for j in range(npt):
        @pl.when(ph == npi + j)
        def _():
            finish(j)


def _s5_mix(q, nb, x, ln_g, wu, tt, wt, zt, a1, a2, s0, dcol, wglut, bglu, wo):
    nseq, l, d = x.shape
    nblk = l // q
    nlt = nblk * nseq
    nkb = min(LANES, nlt)
    ntile = nlt // nkb
    tok = q * nkb
    npt = max(1, tok // S5_SLICE_TOKENS)
    dm = wo.shape[1]
    perm = q > 1
    if perm:
        half = 8
        npi = q // half
        assert nb == nseq and d % LANES == 0 and q % half == 0
        xv = x.reshape(nseq, nblk, npi, half, d // LANES, LANES).transpose(0, 1, 2, 4, 3, 5)
        assert npt % npi == 0
        x_spec = pl.BlockSpec((nseq, nkb // nseq, None, d // LANES, half, LANES),
                              lambda i, j: (0, i, jnp.minimum(j, npi - 1), 0, 0, 0))
    else:
        npi = 1
        xv = x.reshape(1, nseq, d)
        x_spec = pl.BlockSpec((1, nkb, d), lambda i, j: (0, i, 0))
    body = functools.partial(_s5_mix_body, q, nb, nkb, npi, npt, perm)
    p_spec = pl.BlockSpec((q // npt, nkb, dm), lambda i, j: (jnp.maximum(j - npi, 0), i, 0))
    return pl.pallas_call(
        body,
        grid=(ntile, npi + npt),
        in_specs=[x_spec, _const_spec(ln_g.shape), _const_spec(wu.shape), _const_spec(tt.shape),
                  _const_spec(wt.shape), _const_spec(zt.shape), _const_spec(a1.shape), _const_spec(a2.shape),
                  _const_spec(s0.shape), _const_spec(dcol.shape), _const_spec(wglut.shape),
                  _const_spec(bglu.shape), _const_spec(wo.shape)],
        out_specs=[p_spec, pl.BlockSpec(s0.shape, lambda i, j: (0, 0, 0))],
        out_shape=[jax.ShapeDtypeStruct((q, nlt, dm), F32), jax.ShapeDtypeStruct(s0.shape, F32)],
        scratch_shapes=[pltpu.VMEM((d, tok), BF16), pltpu.VMEM((d, tok), F32),
                        pltpu.VMEM((tt.shape[0], nb, 2 * S5_STATE), F32),
                        pltpu.VMEM((wu.shape[1], wu.shape[0]), BF16)],
        compiler_params=_params("arbitrary", "arbitrary"),
        name=f"s5_mix_q{q}",
    )(xv, ln_g, wu, tt, wt, zt, a1, a2, s0, dcol, wglut, bglu, wo)


def _kv_body(m_ref, wk_ref, wv_ref, k_ref, v_ref, kb_ref, vb_ref):
    tm, rows, _ = k_ref.shape
    nt = rows // X_HEADS
    m = m_ref[...].astype(BF16)
    for w_ref, o_ref, ob_ref in ((wk_ref, k_ref, kb_ref), (wv_ref, v_ref, vb_ref)):
        r = _dot(m, w_ref[...])
        ob_ref[...] = r.astype(BF16)
        flat = o_ref.reshape(tm * rows, LANES)
        for h in range(X_HEADS):
            for dt in range(nt):
                c = (h * nt + dt) * LANES
                flat[pl.ds(dt * X_HEADS + h, tm, stride=rows), :] = r[:, c:c + LANES]


def _kv_proj_rider(mem2d, wk, wv, nsteps):
    t, d = mem2d.shape
    assert t % nsteps == 0
    tm = t // nsteps
    rows = d // LANES
    row = pl.BlockSpec((tm, d), lambda i: (i, 0))
    tile = pl.BlockSpec((tm, rows, LANES), lambda i: (i, 0, 0))
    return (_kv_body, (mem2d, wk, wv), (row, _const_spec(wk.shape), _const_spec(wv.shape)),
            (tile, tile, row, row),
            (jax.ShapeDtypeStruct((t, rows, LANES), F32),) * 2 + (jax.ShapeDtypeStruct((t, d), BF16),) * 2)


def _tile_unview_kv(kv8, b, m):
    nt = kv8.shape[1] // X_HEADS
    return kv8.reshape(b, m, nt, X_HEADS, LANES).transpose(0, 1, 3, 2, 4).reshape(b, m, X_HEADS, nt * LANES)


def _attn_prompt_body(x_ref, ys_ref, p_ref, wo1_ref, g2_ref, wq_ref, k_ref, v_ref, wxo_ref, o_ref):
    d = x_ref.shape[-1]
    hd = d // X_HEADS
    x1 = x_ref[...] + _dot(ys_ref[...], wo1_ref[...]) + p_ref[...]
    hq = _rms(x1, g2_ref[...]).astype(BF16)
    qv = _dot(hq, wq_ref[...]).astype(BF16)
    kb = k_ref[...]
    vb = v_ref[...]
    outs = []
    for h in range(X_HEADS):
        sl = slice(h * hd, (h + 1) * hd)
        s = _dot_nt(qv[:, sl], kb[:, sl]) * (hd ** -0.5)
        e = jnp.exp(s - jnp.max(s, axis=-1, keepdims=True))
        p = (e / jnp.sum(e, axis=-1, keepdims=True)).astype(BF16)
        outs.append(_dot(p, vb[:, sl]))
    o = jnp.concatenate(outs, axis=1).astype(BF16)
    o_ref[...] = x1 + _dot(o, wxo_ref[...])


def _attn_prompt(x, ys, pm, wo1, g2, wq, mk, mv, wxo, make_rider):
    b, l, d = x.shape
    nm = mk.shape[1]
    tm = min(TOKEN_TILE, l)
    nl = l // tm
    row = pl.BlockSpec((None, tm, d), lambda i, j: (i, j, 0))
    kv = pl.BlockSpec((None, nm, d), lambda i, j: (i, 0, 0))
    ins = [x, ys, pm, wo1, g2, wq, mk, mv, wxo]
    specs = [row, row, row, _const_spec(wo1.shape), _const_spec(g2.shape), _const_spec(wq.shape), kv, kv,
             _const_spec(wxo.shape)]
    rider = make_rider(lambda i, j: i * nl + j, b * nl)
    body = functools.partial(_two_bodies, _attn_prompt_body, len(ins), 1, rider[0], len(rider[1]))
    return pl.pallas_call(
        body, grid=(b, nl),
        in_specs=specs + list(rider[2]),
        out_specs=[row] + list(rider[3]),
        out_shape=[jax.ShapeDtypeStruct((b, l, d), F32)] + list(rider[4]),
        compiler_params=_params("arbitrary", "arbitrary"),
        name="attn_prompt",
    )(*ins, *rider[1])


def _merge_q_body(x_ref, ys_ref, p_ref, wo1_ref, g2_ref, wq_ref, x1_ref, q_ref):
    x1 = x_ref[...] + _dot(ys_ref[...], wo1_ref[...]) + p_ref[...]
    x1_ref[...] = x1
    q_ref[...] = _dot(_rms(x1, g2_ref[...]).astype(BF16), wq_ref[...])


def _merge_q(x2d, ys, pm, wo1, g2, wq):
    t, d = x2d.shape
    return pl.pallas_call(
        _merge_q_body,
        out_shape=[jax.ShapeDtypeStruct((t, d), F32)] * 2,
        compiler_params=pltpu.CompilerParams(vmem_limit_bytes=VMEM_LIMIT),
        name="merge_q",
    )(x2d, ys, pm, wo1, g2, wq)


def _tile_view(kv):
    b, m, nh, hd = kv.shape
    nt = hd // LANES
    return kv.reshape(b, m, nh, nt, LANES).transpose(0, 1, 3, 2, 4).reshape(b, m, nt * nh, LANES)


def _tile_unview(o):
    b, rows, _ = o.shape
    nt = rows // X_HEADS
    return o.reshape(b, nt, X_HEADS, LANES).transpose(0, 2, 1, 3).reshape(b, rows * LANES)


def _attn_step_body(q_ref, k_ref, v_ref, o_ref):
    bt, nm, rows, _ = k_ref.shape
    hd = rows * LANES // X_HEADS
    for b in range(bt):
        r = jnp.sum(k_ref[b] * q_ref[b], axis=-1, keepdims=True)
        s = (r + pltpu.roll(r, X_HEADS, 1)) * (hd ** -0.5)
        e = jnp.exp(s - jnp.max(s, axis=0, keepdims=True))
        p = e / jnp.sum(e, axis=0, keepdims=True)
        o_ref[b] = jnp.sum(p * v_ref[b], axis=0)


def _attn_step_rider(q8, k8, v8, nsteps):
    nbt, rows, _ = q8.shape
    nm = k8.shape[1]
    assert rows == 2 * X_HEADS, "score assembly assumes two 128-lane tiles per head"
    assert nbt % nsteps == 0
    bt = nbt // nsteps
    qs = pl.BlockSpec((bt, rows, LANES), lambda i: (i, 0, 0))
    kv = pl.BlockSpec((bt, nm, rows, LANES), lambda i: (i, 0, 0, 0))
    return (_attn_step_body, (q8, k8, v8), (qs, kv, kv), (qs,),
            (jax.ShapeDtypeStruct((nbt, rows, LANES), F32),))


def _ffn_body(has_o, final, *refs):
    if has_o:
        x_ref, o_in_ref, wxo_ref, g3_ref, wg_ref, wu_ref, wd_ref, gf_ref, y_ref = refs
        x2 = x_ref[...] + _dot(o_in_ref[...].astype(BF16), wxo_ref[...])
    else:
        x_ref, g3_ref, wg_ref, wu_ref, wd_ref, gf_ref, y_ref = refs
        x2 = x_ref[...]
    hf = _rms(x2, g3_ref[...]).astype(BF16)
    ff = wg_ref.shape[1]
    x3 = x2
    for lo in range(0, ff, FF_CHUNK):
        hi = min(lo + FF_CHUNK, ff)
        act = (_silu(_dot(hf, wg_ref[:, lo:hi])) * _dot(hf, wu_ref[:, lo:hi])).astype(BF16)
        x3 = x3 + _dot(act, wd_ref[lo:hi, :])
    y_ref[...] = _rms(x3, gf_ref[...]) if final else x3


def _ffn(x2d, g3, wg, wu, wd, gf, final, o_in=None, wxo=None, rider=None):
    t, d = x2d.shape
    tm = min(TOKEN_TILE, t)
    row = pl.BlockSpec((tm, d), lambda i: (i, 0))
    has_o = o_in is not None
    ins = [x2d] + ([o_in, wxo] if has_o else []) + [g3, wg, wu, wd, gf]
    specs = [row] + ([row, _const_spec(wxo.shape)] if has_o else []) + [
        _const_spec(g3.shape), _const_spec(wg.shape), _const_spec(wu.shape), _const_spec(wd.shape),
        _const_spec(gf.shape)]
    body = functools.partial(_ffn_body, has_o, final)
    out_specs, out_shape = [row], [jax.ShapeDtypeStruct((t, d), F32)]
    if rider is not None:
        body = functools.partial(_two_bodies, body, len(ins), 1, rider[0], len(rider[1]))
        ins, specs = ins + list(rider[1]), specs + list(rider[2])
        out_specs, out_shape = out_specs + list(rider[3]), out_shape + list(rider[4])
    res = pl.pallas_call(
        body, grid=(t // tm,),
        in_specs=specs, out_specs=out_specs, out_shape=out_shape,
        compiler_params=_params("arbitrary"),
        name="ffn_o" if has_o else "ffn",
    )(*ins)
    return res[0] if rider is None else res


def _two_bodies(body_a, n_in_a, n_out_a, body_b, n_in_b, *refs):
    outs = refs[n_in_a + n_in_b:]
    body_a(*refs[:n_in_a], *outs[:n_out_a])
    body_b(*refs[n_in_a:n_in_a + n_in_b], *outs[n_out_a:])


def _merge_riders(ra, rb):
    body = functools.partial(_two_bodies, ra[0], len(ra[1]), len(ra[3]), rb[0], len(rb[1]))
    return (body,) + tuple(tuple(ra[k]) + tuple(rb[k]) for k in range(1, 5))


def _cast_body(*refs):
    n = len(refs) // 2
    for src, dst in zip(refs[:n], refs[n:]):
        dst[...] = src[...].astype(dst.dtype)


def _cast_rider(ws, step_of, nsteps):
    bf16_rows = 16
    specs, shapes = [], []
    for w in ws:
        r, c = w.shape
        k = next(k for k in (1, 2, 4, 8) if r % (nsteps // k) == 0 and (r // (nsteps // k)) % bf16_rows == 0)
        specs.append(pl.BlockSpec((r // (nsteps // k), c), lambda *g, k=k: (step_of(*g) // k, 0)))
        shapes.append(jax.ShapeDtypeStruct((r, c), BF16))
    return (_cast_body, tuple(ws), tuple(specs), tuple(specs), tuple(shapes))


def _layer_weights(i, ln1_g, w_in, conv_w, conv_b, dt_bias, a_log, d_ssd, gn_g, lam_re, lam_im, log_step,
                   b_re, b_im, c_re, c_im, d_s5, w_glu, b_glu, w_out, ln2_g, w_xq, w_xk, w_xv, w_xo, ln3_g,
                   w_gate, w_up, w_down):
    d = w_in.shape[1]
    n_heads = dt_bias.shape[1]
    dssd = n_heads * SSD_HEAD_DIM
    conv_dim = conv_w.shape[2]
    o1, o2, o3 = dssd, dssd + conv_dim, dssd + conv_dim + n_heads
    win = w_in[i]
    w = {}
    w["ln1"] = ln1_g[i].reshape(1, d)
    w["win"] = win.astype(BF16)
    w["wu_s5"] = w["win"][:, o3:]
    w["cw"] = conv_w[i]
    w["cb"] = conv_b[i].reshape(1, conv_dim)
    w["dtb"] = jnp.pad(dt_bias[i], (0, LANES - n_heads)).reshape(1, LANES)
    w["alog"] = jnp.pad(a_log[i], (0, LANES - n_heads)).reshape(1, LANES)
    w["dexp"] = jnp.repeat(d_ssd[i], SSD_HEAD_DIM).reshape(1, dssd)
    w["gn"] = gn_g[i].reshape(1, dssd)
    w["s5"] = (lam_re[i], lam_im[i], log_step[i], b_re[i], b_im[i], c_re[i], c_im[i])
    ds5 = d_s5.shape[1]
    w["dcol"] = d_s5[i].reshape(ds5, 1)
    w["wglut"] = w_glu[i].T.astype(BF16)
    w["bglu"] = b_glu[i].reshape(ds5, 1)
    w["wo1"] = w_out[i, :dssd].astype(BF16)
    w["wo2"] = w_out[i, dssd:].astype(BF16)
    w["ln2"] = ln2_g[i].reshape(1, d)
    w["wq"] = w_xq[i].astype(BF16)
    w["wk"] = w_xk[i].astype(BF16)
    w["wv"] = w_xv[i].astype(BF16)
    w["wxo"] = w_xo[i].astype(BF16)
    w["ln3"] = ln3_g[i].reshape(1, d)
    w["ffn_f32"] = (w_gate[i], w_up[i], w_down[i])
    return w


def _rows_to_state(s):
    p = s.shape[-1] // 2
    st = s.transpose(1, 0, 2)
    return st[..., :p], st[..., p:]


def kernel(x_prompt, x_sample, mem_prompt, state_conv, state_ssm, state_s5_re, state_s5_im, cache_mem_k, cache_mem_v, ln1_g, w_in, conv_w, conv_b, dt_bias, a_log, d_ssd, gn_g, lam_re, lam_im, log_step, b_re, b_im, c_re, c_im, d_s5, w_glu, b_glu, w_out, ln2_g, w_xq, w_xk, w_xv, w_xo, ln3_g, w_gate, w_up, w_down, final_g):
    bp, seq, d = x_prompt.shape
    bs = x_sample.shape[0]
    depth = w_in.shape[0]
    nm = mem_prompt.shape[1]
    ng, ns = lam_re.shape[1], lam_re.shape[2]
    gf = final_g.reshape(1, d)
    hp = x_prompt
    hs = x_sample.reshape(bs, d)
    outs = {k: [] for k in ("conv_p", "ssm_p", "re_p", "im_p", "mk_p", "mv_p", "conv_s", "ssm_s", "re_s", "im_s")}
    yp = ys_out = None
    for i in range(depth):
        w = _layer_weights(i, ln1_g, w_in, conv_w, conv_b, dt_bias, a_log, d_ssd, gn_g, lam_re, lam_im, log_step,
                           b_re, b_im, c_re, c_im, d_s5, w_glu, b_glu, w_out, ln2_g, w_xq, w_xk, w_xv, w_xo, ln3_g,
                           w_gate, w_up, w_down)
        last = i == depth - 1
        attn_steps = bp * (seq // min(TOKEN_TILE, seq))
        y_ssd, conv_p, ssm_p, zs, xbcs, dts = _ssd_prompt(hp, w["ln1"], w["win"], w["cw"], w["cb"], w["dtb"],
                                                          w["alog"], w["dexp"], w["gn"], hs, bs // attn_steps)
        q = S5_Q
        kv_rider = functools.partial(_kv_proj_rider, mem_prompt.reshape(bp * nm, d), w["wk"], w["wv"])
        (tt, wt, zt, a1, a2, a11, a21), (mk8, mv8, mk, mv) = _s5_prep(q, *w["s5"], kv_rider)
        mk = mk.reshape(bp, nm, d)
        mv = mv.reshape(bp, nm, d)
        s0 = jnp.zeros((ng, bp, 2 * ns), F32)
        pperm, sfin = _s5_mix(q, bp, hp, w["ln1"], w["wu_s5"],tt, wt, zt, a1, a2, s0, w["dcol"], w["wglut"],
                              w["bglu"], w["wo2"])
        pm = pperm.reshape(q, seq // q, bp, d).transpose(2, 1, 0, 3).reshape(bp, seq, d)
        re_p, im_p = _rows_to_state(sfin)
        riders = lambda step_of, n: _merge_riders(
            _ssd_step_rider(zs, xbcs, dts, state_conv[i], state_ssm[i], w["cw"], w["cb"], w["dtb"], w["alog"],
                            w["dexp"], w["gn"], step_of, n),
            _cast_rider(w["ffn_f32"], step_of, n))
        x2, ys_ssd, conv_s, ssm_s, w["wg"], w["wu"], w["wdn"] = _attn_prompt(
            hp, y_ssd, pm, w["wo1"], w["ln2"], w["wq"], mk, mv, w["wxo"], riders)
        ys_ssd = ys_ssd.reshape(bs, -1)
        tt1, wt1, zt1 = _s5_single_token_mats(q, tt, wt, zt)
        s0s = jnp.stack([state_s5_re[i].reshape(bs, ng * ns), state_s5_im[i].reshape(bs, ng * ns)])
        ps, sfin_s = _s5_mix(1, bs, hs.reshape(bs, 1, d), w["ln1"], w["wu_s5"],tt1, wt1, zt1, a11, a21, s0s,
                             w["dcol"], w["wglut"], w["bglu"], w["wo2"])
        re_s, im_s = sfin_s[0].reshape(bs, ng, ns), sfin_s[1].reshape(bs, ng, ns)
        x1s, qs = _merge_q(hs, ys_ssd, ps.reshape(bs, d), w["wo1"], w["ln2"], w["wq"])
        attn_rider = _attn_step_rider(_tile_view(qs.reshape(bs, 1, X_HEADS, d // X_HEADS))[:, 0],
                                      _tile_view(cache_mem_k[i]), _tile_view(cache_mem_v[i]),
                                      (bp * seq) // min(TOKEN_TILE, bp * seq))
        hp, o_s = _ffn(x2.reshape(bp * seq, d), w["ln3"], w["wg"], w["wu"], w["wdn"], gf, last, rider=attn_rider)
        hp = hp.reshape(bp, seq, d)
        hs = _ffn(x1s, w["ln3"], w["wg"], w["wu"], w["wdn"], gf, last, o_in=_tile_unview(o_s), wxo=w["wxo"])
        for k, v in (("conv_p", conv_p), ("ssm_p", ssm_p), ("re_p", re_p), ("im_p", im_p),
                     ("mk_p", _tile_unview_kv(mk8, bp, nm)), ("mv_p", _tile_unview_kv(mv8, bp, nm)),
                     ("conv_s", conv_s), ("ssm_s", ssm_s), ("re_s", re_s), ("im_s", im_s)):
            outs[k].append(v)
    st = lambda k: jnp.stack(outs[k])
    return (hp, hs.reshape(bs, 1, d), st("conv_p"), st("ssm_p"), st("re_p"), st("im_p"), st("mk_p"), st("mv_p"),
            st("conv_s"), st("ssm_s"), st("re_s"), st("im_s"))
```

```python
import functools
import math

import jax
import jax.numpy as jnp
from jax import lax
from jax.experimental import pallas as pl
from jax.experimental.pallas import tpu as pltpu

F32 = jnp.float32
BF16 = jnp.bfloat16
EPS = 1e-6

LANES = 128
VMEM_LIMIT = 60 * 1024 * 1024

SSD_HEAD_DIM = 64
SSD_STATE = 128
SSD_GROUPS = 2
SSD_CHUNK = 128
CONV_W = 4
S5_CH = 16
S5_STATE = 64
S5_Q = 16
S5_GROUP_BATCH = 32
S5_SLICE_TOKENS = 512
X_HEADS = 4
TOKEN_TILE = 512
PROJ_PIECE = 256
SSD_SEQS_PER_STEP = 4
FF_CHUNK = 1024


def _const_spec(shape):
    nd = len(shape)
    return pl.BlockSpec(shape, lambda *_: (0,) * nd, pipeline_mode=pl.Buffered(1))


def _params(*sem):
    return pltpu.CompilerParams(dimension_semantics=sem, vmem_limit_bytes=VMEM_LIMIT)


def _rms(x, g):
    return x * lax.rsqrt(jnp.mean(x * x, axis=-1, keepdims=True) + EPS) * g


def _sigmoid(x):
    return 1.0 / (1.0 + jnp.exp(-x))


def _silu(x):
    return x * _sigmoid(x)


def _gelu_tanh(x):
    return 0.5 * x * (1.0 + jnp.tanh(math.sqrt(2.0 / math.pi) * (x + 0.044715 * (x * x * x))))


def _softplus(x):
    return jnp.maximum(x, 0.0) + jnp.log1p(jnp.exp(-jnp.abs(x)))


def _dot(a, b):
    return jnp.dot(a, b, preferred_element_type=F32)


def _dot_nt(a, b):
    return lax.dot_general(a, b, (((1,), (1,)), ((), ())), preferred_element_type=F32)


def _dot_tn(a, b):
    return lax.dot_general(a, b, (((0,), (0,)), ((), ())), preferred_element_type=F32)


def _proj_windows(d_ssd, conv_dim):
    assert d_ssd % LANES == 0 and conv_dim % LANES == 0
    return (0, d_ssd), (d_ssd, d_ssd + conv_dim), (d_ssd + conv_dim, d_ssd + conv_dim + LANES)


def _expand_heads(v, n_heads):
    rows = v.shape[0]
    lane = lax.broadcasted_iota(jnp.int32, (rows, LANES), 1)
    pieces = [jnp.where(lane < SSD_HEAD_DIM, v[:, 2 * j:2 * j + 1], v[:, 2 * j + 1:2 * j + 2])
              for j in range(n_heads // 2)]
    return jnp.concatenate(pieces, axis=1)


def _ssd_chunk_math(z, x, dt_raw, last_chunk, cw_ref, cb_ref, dtb_ref, alog_ref, dexp_ref, gn_ref, e2_ref,
                    y_ref, conv_ref, ssm_ref, xpad, state, side_jobs=()):
    jobs = list(side_jobs)

    def side(n=1):
        for _ in range(n):
            if jobs:
                jobs.pop(0)()

    q = SSD_CHUNK
    d_ssd = z.shape[-1]
    n_heads = d_ssd // SSD_HEAD_DIM
    hpg = n_heads // SSD_GROUPS
    gw = hpg * SSD_HEAD_DIM
    nct = xpad.shape[0]
    xt = d_ssd // LANES

    for t in range(nct):
        xpad[t, 8:8 + q, :] = x[:, t * LANES:(t + 1) * LANES]
    cw = cw_ref[...]
    conv = (cw[3] * xpad[:, 8:8 + q, :] + cw[2] * xpad[:, 7:7 + q, :] + cw[1] * xpad[:, 6:6 + q, :]
            + cw[0] * xpad[:, 5:5 + q, :] + cb_ref[...])
    xpad[:, 0:8, :] = xpad[:, q:q + 8, :]
    xact = _silu(conv)
    xs = jnp.concatenate([xact[t] for t in range(xt)], axis=1)
    bm = xact[xt:xt + SSD_GROUPS].astype(BF16)
    cm = xact[xt + SSD_GROUPS:].astype(BF16)
    side()

    dt = _softplus(dt_raw + dtb_ref[...])
    a = -jnp.exp(alog_ref[...])
    row = lax.broadcasted_iota(jnp.int32, (q, q), 0)
    col = lax.broadcasted_iota(jnp.int32, (q, q), 1)
    causal = row >= col
    tri = jnp.where(causal, 1.0, 0.0).astype(F32)
    acum = jnp.dot(tri, dt * a, precision=lax.Precision.HIGHEST, preferred_element_type=F32)
    acum_t = acum.T
    last = acum[q - 1:q, :]

    def expand(v):
        hi = v.astype(BF16)
        lo = (v - hi.astype(F32)).astype(BF16)
        return _dot(jnp.concatenate([hi, lo], axis=1), e2_ref[...])

    dt_e = expand(dt)
    ea_e = expand(jnp.exp(acum))
    dend_e = expand(jnp.exp(last - acum))

    dtx = xs * dt_e
    dtx_b = dtx.astype(BF16)
    xdec_b = (dtx * dend_e).astype(BF16)
    side()

    lane = lax.broadcasted_iota(jnp.int32, (q, LANES), 1)
    zero_b = jnp.zeros((q, LANES), BF16)
    y_tiles = []
    for g in range(SSD_GROUPS):
        bg = bm[g]
        cg = cm[g]
        cb = _dot_nt(cg, bg)
        gs = slice(g * gw, (g + 1) * gw)
        st_g = state[:, gs]
        y_off = _dot(cg, st_g.astype(BF16)) * ea_e[:, gs]
        for pr in range(hpg // 2):
            tile = g * (hpg // 2) + pr
            ms = []
            for h in (2 * tile, 2 * tile + 1):
                seg = acum[:, h:h + 1] - acum_t[h:h + 1, :]
                lmat = jnp.exp(jnp.where(causal, seg, -jnp.inf))
                ms.append((cb * lmat).astype(BF16))
            dtile = dtx_b[:, tile * LANES:(tile + 1) * LANES]
            rhs = jnp.concatenate([jnp.where(lane < SSD_HEAD_DIM, dtile, zero_b),
                                   jnp.where(lane >= SSD_HEAD_DIM, dtile, zero_b)], axis=0)
            y_tiles.append(_dot(jnp.concatenate(ms, axis=1), rhs)
                           + y_off[:, pr * LANES:(pr + 1) * LANES])
            if pr % 2 == 1:
                side()
        new = _dot_tn(bg, xdec_b[:, gs])
        state[:, gs] = st_g * ea_e[q - 1:q, gs] + new
    y = jnp.concatenate(y_tiles, axis=1) + dexp_ref[...] * xs
    y = y * _silu(z)
    ms = jnp.mean(y * y, axis=-1, keepdims=True)
    side(len(jobs))
    y_ref[...] = (y * lax.rsqrt(ms + EPS) * gn_ref[...]).astype(y_ref.dtype)

    @pl.when(last_chunk)
    def _():
        conv_ref[...] = x[q - (CONV_W - 1):q, :]
        ssm_ref[...] = state[...].T.reshape(ssm_ref.shape)


def _ssd_prompt_body(wins, x_ref, g_ref, w_ref, cw_ref, cb_ref, dtb_ref, alog_ref, dexp_ref, gn_ref,
                     e2_ref, xs_ref, y_ref, conv_ref, ssm_ref, zs_ref, xbcs_ref, dts_ref,
                     za, xa, da, zb, xb, db, xpad, state, wbf):
    i = pl.program_id(0)
    j = pl.program_id(1)
    nc = pl.num_programs(1) - 1
    ns, q, d = x_ref.shape

    @pl.when(jnp.logical_and(i == 0, j == 0))
    def _():
        for r in (zb, xb, db):
            r[...] = jnp.zeros(r.shape, F32)
        rows = 128
        for r0 in range(0, w_ref.shape[0], rows):
            wbf[r0:r0 + rows, :] = w_ref[r0:r0 + rows, :].astype(BF16)
        hs = _rms(xs_ref[...], g_ref[...]).astype(BF16)
        for (lo, hi), o_ref in zip(wins, (zs_ref, xbcs_ref, dts_ref)):
            o_ref[...] = _dot(hs, wbf[:, lo:hi]).reshape(o_ref.shape)

    @pl.when(j <= 1)
    def _():
        xpad[:, :, 0:8, :] = jnp.zeros(xpad.shape[:2] + (8, LANES), F32)
        state[...] = jnp.zeros(state.shape, F32)

    def step(wr, rd):
        xin = x_ref[...].reshape(ns * q, d)
        msx = jnp.mean(xin * xin, axis=-1, keepdims=True)
        hbox = []

        def proj_piece(dst, base, lo, hi):
            def run():
                if not hbox:
                    hbox.append((xin * lax.rsqrt(msx + EPS) * g_ref[...]).astype(BF16))
                dst[:, lo:hi] = _dot(hbox[0], wbf[:, base + lo:base + hi])
            return run

        pieces = []
        for dst, (w_lo, w_hi) in ((wr[0], wins[0]), (wr[2], wins[2]), (wr[1], wins[1])):
            n = w_hi - w_lo
            pieces += [proj_piece(dst, w_lo, lo, min(lo + PROJ_PIECE, n)) for lo in range(0, n, PROJ_PIECE)]
        per = -(-len(pieces) // ns)
        for s in range(ns):
            rows = slice(s * q, (s + 1) * q)
            _ssd_chunk_math(rd[0][rows, :], rd[1][rows, :], rd[2][rows, :], j == nc, cw_ref, cb_ref, dtb_ref,
                            alog_ref, dexp_ref, gn_ref, e2_ref, y_ref.at[s], conv_ref.at[s], ssm_ref.at[s],
                            xpad.at[s], state.at[s], pieces[s * per:(s + 1) * per])

    @pl.when(j % 2 == 0)
    def _():
        step((za, xa, da), (zb, xb, db))

    @pl.when(j % 2 == 1)
    def _():
        step((zb, xb, db), (za, xa, da))


def _ssd_prompt(x, ln_g, w_in, cw, cb, dtb, alog, dexp, gn, xs, group):
    b, l, d = x.shape
    ts = xs.shape[0]
    conv_dim = cw.shape[1]
    d_ssd = conv_dim - 2 * SSD_GROUPS * SSD_STATE
    wins = _proj_windows(d_ssd, conv_dim)
    n_heads = d_ssd // SSD_HEAD_DIM
    q = SSD_CHUNK
    nc = l // q
    assert SSD_STATE == LANES and conv_dim == d_ssd + 2 * SSD_GROUPS * SSD_STATE
    nct = conv_dim // LANES
    cw4 = cw.reshape(CONV_W, nct, 1, LANES)
    cb3 = cb.reshape(nct, 1, LANES)
    e2 = (jnp.arange(2 * LANES)[:, None] % LANES == jnp.arange(d_ssd)[None, :] // SSD_HEAD_DIM).astype(BF16)
    ns = SSD_SEQS_PER_STEP if b % SSD_SEQS_PER_STEP == 0 else 1
    slot = lambda n: pltpu.VMEM((ns * q, n), F32)
    return pl.pallas_call(
        functools.partial(_ssd_prompt_body, wins),
        grid=(b // ns, nc + 1),
        in_specs=[pl.BlockSpec((ns, q, d), lambda i, j: (i, jnp.minimum(j, nc - 1), 0)),
                  _const_spec(ln_g.shape), _const_spec(w_in.shape),
                  _const_spec(cw4.shape), _const_spec(cb3.shape), _const_spec(dtb.shape), _const_spec(alog.shape),
                  _const_spec(dexp.shape), _const_spec(gn.shape), _const_spec(e2.shape), _const_spec(xs.shape)],
        out_specs=[pl.BlockSpec((ns, q, d_ssd), lambda i, j: (i, jnp.maximum(j - 1, 0), 0)),
                   pl.BlockSpec((ns, CONV_W - 1, conv_dim), lambda i, j: (i, 0, 0)),
                   pl.BlockSpec((ns, n_heads, SSD_HEAD_DIM, SSD_STATE), lambda i, j: (i, 0, 0, 0))]
        + [pl.BlockSpec((ts // group, group, hi - lo), lambda i, j: (0, 0, 0)) for lo, hi in wins],
        out_shape=[jax.ShapeDtypeStruct((b, l, d_ssd), BF16),
                   jax.ShapeDtypeStruct((b, CONV_W - 1, conv_dim), F32),
                   jax.ShapeDtypeStruct((b, n_heads, SSD_HEAD_DIM, SSD_STATE), F32)]
        + [jax.ShapeDtypeStruct((ts // group, group, hi - lo), F32) for lo, hi in wins],
        scratch_shapes=[slot(d_ssd), slot(conv_dim), slot(LANES), slot(d_ssd), slot(conv_dim), slot(LANES),
                        pltpu.VMEM((ns, nct, q + 8, LANES), F32), pltpu.VMEM((ns, SSD_STATE, d_ssd), F32),
                        pltpu.VMEM(w_in.shape, BF16)],
        compiler_params=_params("arbitrary", "arbitrary"),
        name="ssd_prompt",
    )(x, ln_g, w_in, cw4, cb3, dtb, alog, dexp, gn, e2, xs)


def _ssd_step_body(z_ref, xbc_ref, dt_ref, cs_ref, st_ref, cw_ref, cb_ref, dtb_ref, alog_ref, dexp_ref, gn_ref,
                   y_ref, conv_ref, ssm_ref):
    bt, d_ssd = z_ref.shape
    n_heads = d_ssd // SSD_HEAD_DIM
    hpg = n_heads // SSD_GROUPS
    x = xbc_ref[...]
    cw = cw_ref[...]
    cs = [cs_ref[:, k, :] for k in range(CONV_W - 1)]
    conv = cw[0:1] * cs[0] + cw[1:2] * cs[1] + cw[2:3] * cs[2] + cw[3:4] * x + cb_ref[...]
    conv_ref[:, 0, :] = cs[1]
    conv_ref[:, 1, :] = cs[2]
    conv_ref[:, 2, :] = x
    xact = _silu(conv)
    xs = xact[:, :d_ssd]
    bm = xact[:, d_ssd:d_ssd + SSD_GROUPS * SSD_STATE]
    cm = xact[:, d_ssd + SSD_GROUPS * SSD_STATE:]
    dt = _softplus(dt_ref[...] + dtb_ref[...])
    da = jnp.exp(dt * (-jnp.exp(alog_ref[...])))
    dtx = xs * _expand_heads(dt, n_heads)
    pad = jnp.zeros((LANES - bt, d_ssd), F32)
    to_cols = lambda v: jnp.concatenate([v, pad], axis=0).T
    dtx_t = to_cols(dtx)
    da_t = to_cols(_expand_heads(da, n_heads))
    lane = lax.broadcasted_iota(jnp.int32, (d_ssd, LANES), 1)
    gw = hpg * SSD_HEAD_DIM
    rows_of = lambda v, b: jnp.concatenate(
        [jnp.broadcast_to(v[b:b + 1, g * SSD_STATE:(g + 1) * SSD_STATE], (gw, SSD_STATE)) for g in range(SSD_GROUPS)],
        axis=0)
    y_t = jnp.zeros((d_ssd, LANES), F32)
    for b in range(bt):
        s0 = st_ref[b].reshape(d_ssd, SSD_STATE)
        s_new = s0 * da_t[:, b:b + 1] + dtx_t[:, b:b + 1] * rows_of(bm, b)
        ssm_ref[b] = s_new.reshape(n_heads, SSD_HEAD_DIM, SSD_STATE)
        ycol = jnp.sum(s_new * rows_of(cm, b), axis=1, keepdims=True)
        y_t = jnp.where(lane == b, ycol, y_t)
    y = y_t.T[:bt, :] + dexp_ref[...] * xs
    y = y * _silu(z_ref[...])
    y_ref[...] = _rms(y, gn_ref[...]).astype(y_ref.dtype)


def _ssd_step_rider(z, xbc, dt, conv_state, ssm_state, cw, cb, dtb, alog, dexp, gn, step_of, nsteps):
    _, bt, d_ssd = z.shape
    conv_dim = xbc.shape[-1]
    n_heads = d_ssd // SSD_HEAD_DIM
    assert z.shape[0] == nsteps and conv_state.shape[0] == nsteps * bt
    row = lambda n: pl.BlockSpec((None, bt, n), lambda *g: (step_of(*g), 0, 0))
    cs_spec = pl.BlockSpec((bt, CONV_W - 1, conv_dim), lambda *g: (step_of(*g), 0, 0))
    st_spec = pl.BlockSpec((bt, n_heads, SSD_HEAD_DIM, SSD_STATE), lambda *g: (step_of(*g), 0, 0, 0))
    cst = lambda a: pl.BlockSpec(a.shape, lambda *g: (0,) * a.ndim, pipeline_mode=pl.Buffered(1))
    ins = (z, xbc, dt, conv_state, ssm_state, cw, cb, dtb, alog, dexp, gn)
    specs = (row(d_ssd), row(conv_dim), row(LANES), cs_spec, st_spec, cst(cw), cst(cb), cst(dtb), cst(alog),
             cst(dexp), cst(gn))
    out_shapes = (jax.ShapeDtypeStruct((nsteps, bt, d_ssd), BF16),
                  jax.ShapeDtypeStruct(conv_state.shape, F32),
                  jax.ShapeDtypeStruct(ssm_state.shape, F32))
    return (_ssd_step_body, ins, specs, (row(d_ssd), cs_spec, st_spec), out_shapes)


def _split_bf16(x):
    hi = x.astype(BF16)
    lo = (x - hi.astype(F32)).astype(BF16)
    return hi, lo


def _s5_prep_body(q, ls_ref, lrp_ref, lip_ref, lr2_ref, li2_ref, br_ref, bi_ref, ccat_ref, ca_ref, cb_ref, rep_ref,
                  tt_ref, wt_ref, zt_ref, a1_ref, a2_ref, a1s_ref, a2s_ref):
    ng = ls_ref.shape[0]
    w = q * S5_CH
    step = jnp.exp(ls_ref[...])
    lr, li = lrp_ref[...], lip_ref[...]
    mag = jnp.exp(lr * step)
    ang = li * step
    lbr = mag * jnp.cos(ang)
    lbi = mag * jnp.sin(ang)
    den = lr * lr + li * li
    kr = ((lbr - 1.0) * lr + lbi * li) / den
    ki = (lbi * lr - (lbr - 1.0) * li) / den
    br, bi = br_ref[...], bi_ref[...]
    bbr = kr * br - ki * bi
    bbi = kr * bi + ki * br
    if q > 1:
        np_ = br.shape[1]
        rep = lambda v: jnp.dot(v.reshape(ng * np_, S5_CH), rep_ref[...], precision=lax.Precision.HIGHEST,
                                preferred_element_type=F32).reshape(ng, np_, w)
        bbr, bbi = rep(bbr), rep(bbi)
    if q == 1:
        ball = jnp.concatenate([bbr, bbi], axis=1)
    else:
        d = lax.broadcasted_iota(jnp.int32, (1, 1, w), 2) // S5_CH
        fr, fi = lbr, lbi
        pr = pi = None
        for b in range((q - 1).bit_length()):
            bit = ((d >> b) & 1) == 1
            sr, si = jnp.where(bit, fr, 1.0), jnp.where(bit, fi, 0.0)
            pr, pi = (sr, si) if pr is None else (pr * sr - pi * si, pr * si + pi * sr)
            fr, fi = fr * fr - fi * fi, 2.0 * (fr * fi)
        ball = jnp.concatenate([pr * bbr - pi * bbi, pr * bbi + pi * bbr], axis=1)
    wt_ref[...] = ball.astype(BF16)

    lane3 = lax.broadcasted_iota(jnp.int32, (1, 1, 2 * S5_STATE), 2)
    first = lane3 < S5_STATE
    csign = jnp.where(first, ccat_ref[...], -ccat_ref[...])
    ch, cl = _split_bf16(csign)
    bh, bl = _split_bf16(ball)
    bdot = lambda x, y: lax.dot_general(x, y, (((2,), (1,)), ((0,), (0,))), preferred_element_type=F32)
    kall = bdot(ch, bh) + bdot(ch, bl) + bdot(cl, bh)
    k2 = kall.reshape(ng * S5_CH, w)
    lane2 = lax.broadcasted_iota(jnp.int32, (ng * S5_CH, w), 1)
    for t in range(q):
        sh = (q - 1 - t) * S5_CH
        r = pltpu.roll(k2, sh, 1) if sh else k2
        r = jnp.where(lane2 >= sh, r, 0.0)
        tt_ref[:, t * S5_CH:(t + 1) * S5_CH, :] = r.reshape(ng, S5_CH, w).astype(BF16)

    lr2, li2 = lr2_ref[...], li2_ref[...]
    t1 = (lax.broadcasted_iota(jnp.int32, (1, q, 1), 1) + 1).astype(F32)
    zm = jnp.exp(t1 * (lr2 * step))
    za = t1 * (li2 * step)
    zr = zm * jnp.cos(za)
    zi = zm * jnp.sin(za)
    ca, cb = ca_ref[...], cb_ref[...]
    for t in range(q):
        prt = zr[:, t:t + 1, :]
        pit = zi[:, t:t + 1, :]
        zt = jnp.where(first, ca * prt - cb * pit, -(ca * pit) - cb * prt)
        zt_ref[:, t * S5_CH:(t + 1) * S5_CH, :] = zt.astype(BF16)
    qf = float(q)
    mq = jnp.exp(qf * (lr2 * step))
    aq = qf * (li2 * step)
    ar = mq * jnp.cos(aq)
    ai = mq * jnp.sin(aq)
    a1_ref[...] = ar
    a2_ref[...] = jnp.where(first, -ai, ai)
    a1s_ref[...] = zr[:, 0:1, :]
    a2s_ref[...] = jnp.where(first, -zi[:, 0:1, :], zi[:, 0:1, :])


def _s5_prep(q, lam_re, lam_im, log_step, b_re, b_im, c_re, c_im, make_rider):
    ng, p = lam_re.shape
    w = q * S5_CH
    ls = log_step.reshape(ng, 1, 1)
    lrp = lam_re.reshape(ng, p, 1)
    lip = lam_im.reshape(ng, p, 1)
    lr2 = jnp.concatenate([lam_re, lam_re], axis=-1).reshape(ng, 1, 2 * p)
    li2 = jnp.concatenate([lam_im, lam_im], axis=-1).reshape(ng, 1, 2 * p)
    rep = (jnp.arange(S5_CH)[:, None] == jnp.arange(w)[None, :] % S5_CH).astype(F32)
    ccat = jnp.concatenate([c_re, c_im], axis=-1)
    ca = jnp.concatenate([c_re, c_re], axis=-1)
    cb = jnp.concatenate([c_im, c_im], axis=-1)
    gb = ng if q == 1 else 16
    blk = lambda a, b: pl.BlockSpec((gb, a, b), lambda i: (i, 0, 0))
    ins = (ls, lrp, lip, lr2, li2, b_re, b_im, ccat, ca, cb, rep)
    in_specs = [blk(1, 1), blk(p, 1), blk(p, 1), blk(1, 2 * p), blk(1, 2 * p), blk(p, S5_CH), blk(p, S5_CH),
                blk(S5_CH, 2 * p), blk(S5_CH, 2 * p), blk(S5_CH, 2 * p), _const_spec(rep.shape)]
    out_specs = [blk(w, w), blk(2 * p, w), blk(w, 2 * p)] + [blk(1, 2 * p)] * 4
    out_shape = [jax.ShapeDtypeStruct((ng, w, w), BF16), jax.ShapeDtypeStruct((ng, 2 * p, w), BF16),
                 jax.ShapeDtypeStruct((ng, w, 2 * p), BF16)] + [jax.ShapeDtypeStruct((ng, 1, 2 * p), F32)] * 4
    body = functools.partial(_s5_prep_body, q)
    rider = make_rider(ng // gb)
    body = functools.partial(_two_bodies, body, len(ins), len(out_specs), rider[0], len(rider[1]))
    res = pl.pallas_call(
        body,
        grid=(ng // gb,),
        in_specs=in_specs + list(rider[2]),
        out_specs=out_specs + list(rider[3]),
        out_shape=out_shape + list(rider[4]),
        compiler_params=_params("arbitrary"),
        name=f"s5_prep_q{q}",
    )(*ins, *rider[1])
    return res[:len(out_specs)], res[len(out_specs):]


def _s5_single_token_mats(q, tt, wt, zt):
    c = S5_CH
    return tt[:, :c, (q - 1) * c:], wt[:, :, :c], zt[:, :c, :]


def _s5_mix_body(q, nb, nkb, npi, npt, perm, x_ref, g_ref, wu_ref, tt_ref, wt_ref, zt_ref, a1_ref, a2_ref, s0_ref,
                 dcol_ref, wglut_ref, bglu_ref, wo_ref, p_ref, sfin_ref, ut, yt, carry, wut):
    ng = tt_ref.shape[0]
    nk = nkb // nb
    qs = q // npt
    ch = qs * nkb
    tile = pl.program_id(0)
    ph = pl.program_id(1)

    @pl.when(jnp.logical_and(tile == 0, ph == 0))
    def _():
        if perm:
            carry[...] = s0_ref[...]
        else:
            lane_s = lax.broadcasted_iota(jnp.int32, (nb, 2 * S5_STATE), 1)
            for pr in range(ng // 2):
                cols = slice(pr * 2 * S5_STATE, (pr + 1) * 2 * S5_STATE)
                re_t, im_t = s0_ref[0, :, cols], s0_ref[1, :, cols]
                carry[2 * pr] = jnp.where(lane_s < S5_STATE, re_t, pltpu.roll(im_t, S5_STATE, 1))
                carry[2 * pr + 1] = jnp.where(lane_s < S5_STATE, pltpu.roll(re_t, S5_STATE, 1), im_t)
        cw_ = 256
        for c in range(0, wu_ref.shape[1], cw_):
            wut[c:c + cw_, :] = wu_ref[:, c:c + cw_].T

    def rows_of(r):
        if not perm:
            return x_ref[r]
        nseq, nblk, ndt, nr, _ = x_ref.shape
        flat = x_ref.reshape(nseq * nblk * ndt * nr, LANES)

        return jnp.concatenate(
            [jnp.concatenate([flat[pl.ds((k * ndt + dt) * nr + r, nseq, stride=nblk * ndt * nr), :]
                              for dt in range(ndt)], axis=1) for k in range(nblk)], axis=0)

    def project(j):
        step = 2 if qs % 2 == 0 else 1
        r0 = (j * qs) % (q // npi)
        for r in range(0, qs, step):
            xin = jnp.concatenate([rows_of(r0 + r + s) for s in range(step)], axis=0)
            u = _dot_nt(wut[...], _rms(xin, g_ref[...]).astype(BF16))
            sl = slice((j * qs + r) * nkb, (j * qs + r + step) * nkb)
            ut[:, sl] = u.astype(BF16)
            yt[:, sl] = dcol_ref[...] * u

    def finish(j):
        gt = _gelu_tanh(yt[:, j * ch:(j + 1) * ch])
        gate = _dot(wglut_ref[...], gt.astype(BF16)) + bglu_ref[...]
        y5 = (gt * _sigmoid(gate)).astype(BF16)
        p_ref[...] = _dot_tn(y5, wo_ref[...]).reshape(p_ref.shape)

    spp = npt // npi
    for hh in range(npi):
        @pl.when(ph == hh)
        def _():
            for j in range(hh * spp, (hh + 1) * spp):
                project(j)

    @pl.when(ph == npi - 1)
    def _():
        gu = S5_GROUP_BATCH
        bdot = lambda a, b: lax.dot_general(a, b, (((2,), (1,)), ((0,), (0,))), preferred_element_type=F32)
        bdot_nt = lambda a, b: lax.dot_general(a, b, (((2,), (2,)), ((0,), (0,))), preferred_element_type=F32)

        def groups(i, _):
            g0 = pl.multiple_of(i * gu, gu)
            r0 = pl.multiple_of(i * (gu * S5_CH), gu * S5_CH)
            gsl = pl.ds(g0, gu)
            rows = ut[pl.ds(r0, gu * S5_CH), :].reshape(gu, S5_CH, q * nkb)
            ugt = jnp.concatenate([rows[:, :, (q - 1 - j) * nkb:(q - j) * nkb] for j in range(q)], axis=1)
            y = bdot(tt_ref[gsl], ugt)
            vt = bdot(wt_ref[gsl], ugt)
            v = jnp.swapaxes(vt, 1, 2)
            v_sw = jnp.swapaxes(jnp.concatenate([vt[:, S5_STATE:], vt[:, :S5_STATE]], axis=1), 1, 2)
            a1 = a1_ref[gsl]
            a2 = a2_ref[gsl]
            s = carry[gsl]
            s_sw = pltpu.roll(s.reshape(gu * nb, 2 * S5_STATE), S5_STATE, 1).reshape(s.shape)
            prev = []
            for k in range(nk):
                prev.append(s)
                s, s_sw = (a1 * s + a2 * s_sw + v[:, k * nb:(k + 1) * nb, :],
                           a1 * s_sw - a2 * s + v_sw[:, k * nb:(k + 1) * nb, :])
            carry[gsl] = s
            sprev = jnp.concatenate(prev, axis=1).astype(BF16)
            y = y + bdot_nt(zt_ref[gsl], sprev)
            for t in range(q):
                yt[pl.ds(r0, gu * S5_CH), t * nkb:(t + 1) * nkb] += (
                    y[:, t * S5_CH:(t + 1) * S5_CH, :].reshape(gu * S5_CH, nkb))
            return 0

        lax.fori_loop(0, ng // gu, groups, 0)
        if perm:
            sfin_ref[...] = carry[...]
        else:
            lane_s = lax.broadcasted_iota(jnp.int32, (nb, 2 * S5_STATE), 1)
            for pr in range(ng // 2):
                cols = slice(pr * 2 * S5_STATE, (pr + 1) * 2 * S5_STATE)
                c0, c1 = carry[2 * pr], carry[2 * pr + 1]
                sfin_ref[0, :, cols] = jnp.where(lane_s < S5_STATE, c0, pltpu.roll(c1, S5_STATE, 1))
                sfin_ref[1, :, cols] = jnp.where(lane_s < S5_STATE, pltpu.roll(c0, S5_STATE, 1), c1)

    for j in range(npt):
        @pl.when(ph == npi + j)
        def _():
            finish(j)


def _s5_mix(q, nb, x, ln_g, wu, tt, wt, zt, a1, a2, s0, dcol, wglut, bglu, wo):
    nseq, l, d = x.shape
    nblk = l // q
    nlt = nblk * nseq
    nkb = min(LANES, nlt)
    ntile = nlt // nkb
    tok = q * nkb
    npt = max(1, tok // S5_SLICE_TOKENS)
    dm = wo.shape[1]
    perm = q > 1
    if perm:
        half = 8
        npi = q // half
        assert nb == nseq and d % LANES == 0 and q % half == 0
        xv = x.reshape(nseq, nblk, npi, half, d // LANES, LANES).transpose(0, 1, 2, 4, 3, 5)
        assert npt % npi == 0
        x_spec = pl.BlockSpec((nseq, nkb // nseq, None, d // LANES, half, LANES),
                              lambda i, j: (0, i, jnp.minimum(j, npi - 1), 0, 0, 0))
    else:
        npi = 1
        xv = x.reshape(1, nseq, d)
        x_spec = pl.BlockSpec((1, nkb, d), lambda i, j: (0, i, 0))
    body = functools.partial(_s5_mix_body, q, nb, nkb, npi, npt, perm)
    p_spec = pl.BlockSpec((q // npt, nkb, dm), lambda i, j: (jnp.maximum(j - npi, 0), i, 0))
    return pl.pallas_call(
        body,
        grid=(ntile, npi + npt),
        in_specs=[x_spec, _const_spec(ln_g.shape), _const_spec(wu.shape), _const_spec(tt.shape),
                  _const_spec(wt.shape), _const_spec(zt.shape), _const_spec(a1.shape), _const_spec(a2.shape),
                  _const_spec(s0.shape), _const_spec(dcol.shape), _const_spec(wglut.shape),
                  _const_spec(bglu.shape), _const_spec(wo.shape)],
        out_specs=[p_spec, pl.BlockSpec(s0.shape, lambda i, j: (0, 0, 0))],
        out_shape=[jax.ShapeDtypeStruct((q, nlt, dm), F32), jax.ShapeDtypeStruct(s0.shape, F32)],
        scratch_shapes=[pltpu.VMEM((d, tok), BF16), pltpu.VMEM((d, tok), F32),
                        pltpu.VMEM((tt.shape[0], nb, 2 * S5_STATE), F32),
                        pltpu.VMEM((wu.shape[1], wu.shape[0]), BF16)],
        compiler_params=_params("arbitrary", "arbitrary"),
        name=f"s5_mix_q{q}",
    )(xv, ln_g, wu, tt, wt, zt, a1, a2, s0, dcol, wglut, bglu, wo)


def _kv_body(m_ref, wk_ref, wv_ref, k_ref, v_ref, kb_ref, vb_ref):
    tm, rows, _ = k_ref.shape
    nt = rows // X_HEADS
    m = m_ref[...].astype(BF16)
    for w_ref, o_ref, ob_ref in ((wk_ref, k_ref, kb_ref), (wv_ref, v_ref, vb_ref)):
        r = _dot(m, w_ref[...])
        ob_ref[...] = r.astype(BF16)
        flat = o_ref.reshape(tm * rows, LANES)
        for h in range(X_HEADS):
            for dt in range(nt):
                c = (h * nt + dt) * LANES
                flat[pl.ds(dt * X_HEADS + h, tm, stride=rows), :] = r[:, c:c + LANES]


def _kv_proj_rider(mem2d, wk, wv, nsteps):
    t, d = mem2d.shape
    assert t % nsteps == 0
    tm = t // nsteps
    rows = d // LANES
    row = pl.BlockSpec((tm, d), lambda i: (i, 0))
    tile = pl.BlockSpec((tm, rows, LANES), lambda i: (i, 0, 0))
    return (_kv_body, (mem2d, wk, wv), (row, _const_spec(wk.shape), _const_spec(wv.shape)),
            (tile, tile, row, row),
            (jax.ShapeDtypeStruct((t, rows, LANES), F32),) * 2 + (jax.ShapeDtypeStruct((t, d), BF16),) * 2)


def _tile_unview_kv(kv8, b, m):
    nt = kv8.shape[1] // X_HEADS
    return kv8.reshape(b, m, nt, X_HEADS, LANES).transpose(0, 1, 3, 2, 4).reshape(b, m, X_HEADS, nt * LANES)


def _attn_prompt_body(x_ref, ys_ref, p_ref, wo1_ref, g2_ref, wq_ref, k_ref, v_ref, wxo_ref, o_ref):
    d = x_ref.shape[-1]
    hd = d // X_HEADS
    x1 = x_ref[...] + _dot(ys_ref[...], wo1_ref[...]) + p_ref[...]
    hq = _rms(x1, g2_ref[...]).astype(BF16)
    qv = _dot(hq, wq_ref[...]).astype(BF16)
    kb = k_ref[...]
    vb = v_ref[...]
    outs = []
    for h in range(X_HEADS):
        sl = slice(h * hd, (h + 1) * hd)
        s = _dot_nt(qv[:, sl], kb[:, sl]) * (hd ** -0.5)
        e = jnp.exp(s - jnp.max(s, axis=-1, keepdims=True))
        p = (e / jnp.sum(e, axis=-1, keepdims=True)).astype(BF16)
        outs.append(_dot(p, vb[:, sl]))
    o = jnp.concatenate(outs, axis=1).astype(BF16)
    o_ref[...] = x1 + _dot(o, wxo_ref[...])


def _attn_prompt(x, ys, pm, wo1, g2, wq, mk, mv, wxo, make_rider):
    b, l, d = x.shape
    nm = mk.shape[1]
    tm = min(TOKEN_TILE, l)
    nl = l // tm
    row = pl.BlockSpec((None, tm, d), lambda i, j: (i, j, 0))
    kv = pl.BlockSpec((None, nm, d), lambda i, j: (i, 0, 0))
    ins = [x, ys, pm, wo1, g2, wq, mk, mv, wxo]
    specs = [row, row, row, _const_spec(wo1.shape), _const_spec(g2.shape), _const_spec(wq.shape), kv, kv,
             _const_spec(wxo.shape)]
    rider = make_rider(lambda i, j: i * nl + j, b * nl)
    body = functools.partial(_two_bodies, _attn_prompt_body, len(ins), 1, rider[0], len(rider[1]))
    return pl.pallas_call(
        body, grid=(b, nl),
        in_specs=specs + list(rider[2]),
        out_specs=[row] + list(rider[3]),
        out_shape=[jax.ShapeDtypeStruct((b, l, d), F32)] + list(rider[4]),
        compiler_params=_params("arbitrary", "arbitrary"),
        name="attn_prompt",
    )(*ins, *rider[1])


def _merge_q_body(x_ref, ys_ref, p_ref, wo1_ref, g2_ref, wq_ref, x1_ref, q_ref):
    x1 = x_ref[...] + _dot(ys_ref[...], wo1_ref[...]) + p_ref[...]
    x1_ref[...] = x1
    q_ref[...] = _dot(_rms(x1, g2_ref[...]).astype(BF16), wq_ref[...])


def _merge_q(x2d, ys, pm, wo1, g2, wq):
    t, d = x2d.shape
    return pl.pallas_call(
        _merge_q_body,
        out_shape=[jax.ShapeDtypeStruct((t, d), F32)] * 2,
        compiler_params=pltpu.CompilerParams(vmem_limit_bytes=VMEM_LIMIT),
        name="merge_q",
    )(x2d, ys, pm, wo1, g2, wq)


def _tile_view(kv):
    b, m, nh, hd = kv.shape
    nt = hd // LANES
    return kv.reshape(b, m, nh, nt, LANES).transpose(0, 1, 3, 2, 4).reshape(b, m, nt * nh, LANES)


def _tile_unview(o):
    b, rows, _ = o.shape
    nt = rows // X_HEADS
    return o.reshape(b, nt, X_HEADS, LANES).transpose(0, 2, 1, 3).reshape(b, rows * LANES)


def _attn_step_body(q_ref, k_ref, v_ref, o_ref):
    bt, nm, rows, _ = k_ref.shape
    hd = rows * LANES // X_HEADS
    for b in range(bt):
        r = jnp.sum(k_ref[b] * q_ref[b], axis=-1, keepdims=True)
        s = (r + pltpu.roll(r, X_HEADS, 1)) * (hd ** -0.5)
        e = jnp.exp(s - jnp.max(s, axis=0, keepdims=True))
        p = e / jnp.sum(e, axis=0, keepdims=True)
        o_ref[b] = jnp.sum(p * v_ref[b], axis=0)


def _attn_step_rider(q8, k8, v8, nsteps):
    nbt, rows, _ = q8.shape
    nm = k8.shape[1]
    assert rows == 2 * X_HEADS, "score assembly assumes two 128-lane tiles per head"
    assert nbt % nsteps == 0
    bt = nbt // nsteps
    qs = pl.BlockSpec((bt, rows, LANES), lambda i: (i, 0, 0))
    kv = pl.BlockSpec((bt, nm, rows, LANES), lambda i: (i, 0, 0, 0))
    return (_attn_step_body, (q8, k8, v8), (qs, kv, kv), (qs,),
            (jax.ShapeDtypeStruct((nbt, rows, LANES), F32),))


def _ffn_body(has_o, final, *refs):
    if has_o:
        x_ref, o_in_ref, wxo_ref, g3_ref, wg_ref, wu_ref, wd_ref, gf_ref, y_ref = refs
        x2 = x_ref[...] + _dot(o_in_ref[...].astype(BF16), wxo_ref[...])
    else:
        x_ref, g3_ref, wg_ref, wu_ref, wd_ref, gf_ref, y_ref = refs
        x2 = x_ref[...]
    hf = _rms(x2, g3_ref[...]).astype(BF16)
    ff = wg_ref.shape[1]
    x3 = x2
    for lo in range(0, ff, FF_CHUNK):
        hi = min(lo + FF_CHUNK, ff)
        act = (_silu(_dot(hf, wg_ref[:, lo:hi])) * _dot(hf, wu_ref[:, lo:hi])).astype(BF16)
        x3 = x3 + _dot(act, wd_ref[lo:hi, :])
    y_ref[...] = _rms(x3, gf_ref[...]) if final else x3


def _ffn(x2d, g3, wg, wu, wd, gf, final, o_in=None, wxo=None, rider=None):
    t, d = x2d.shape
    tm = min(TOKEN_TILE, t)
    row = pl.BlockSpec((tm, d), lambda i: (i, 0))
    has_o = o_in is not None
    ins = [x2d] + ([o_in, wxo] if has_o else []) + [g3, wg, wu, wd, gf]
    specs = [row] + ([row, _const_spec(wxo.shape)] if has_o else []) + [
        _const_spec(g3.shape), _const_spec(wg.shape), _const_spec(wu.shape), _const_spec(wd.shape),
        _const_spec(gf.shape)]
    body = functools.partial(_ffn_body, has_o, final)
    out_specs, out_shape = [row], [jax.ShapeDtypeStruct((t, d), F32)]
    if rider is not None:
        body = functools.partial(_two_bodies, body, len(ins), 1, rider[0], len(rider[1]))
        ins, specs = ins + list(rider[1]), specs + list(rider[2])
        out_specs, out_shape = out_specs + list(rider[3]), out_shape + list(rider[4])
    res = pl.pallas_call(
        body, grid=(t // tm,),
        in_specs=specs, out_specs=out_specs, out_shape=out_shape,
        compiler_params=_params("arbitrary"),
        name="ffn_o" if has_o else "ffn",
    )(*ins)
    return res[0] if rider is None else res


def _two_bodies(body_a, n_in_a, n_out_a, body_b, n_in_b, *refs):
    outs = refs[n_in_a + n_in_b:]
    body_a(*refs[:n_in_a], *outs[:n_out_a])
    body_b(*refs[n_in_a:n_in_a + n_in_b], *outs[n_out_a:])


def _merge_riders(ra, rb):
    body = functools.partial(_two_bodies, ra[0], len(ra[1]), len(ra[3]), rb[0], len(rb[1]))
    return (body,) + tuple(tuple(ra[k]) + tuple(rb[k]) for k in range(1, 5))


def _cast_body(*refs):
    n = len(refs) // 2
    for src, dst in zip(refs[:n], refs[n:]):
        dst[...] = src[...].astype(dst.dtype)


def _cast_rider(ws, step_of, nsteps):
    bf16_rows = 16
    ins, in_specs, out_specs, shapes = [], [], [], []
    for w in ws:
        w, lo, hi = w if isinstance(w, tuple) else (w, 0, w.shape[0])
        r, c = hi - lo, w.shape[1]
        k = next(k for k in (1, 2, 4, 8) if r % (nsteps // k) == 0 and (r // (nsteps // k)) % bf16_rows == 0)
        rp = r // (nsteps // k)
        assert lo % rp == 0
        ins.append(w)
        in_specs.append(pl.BlockSpec((rp, c), lambda *g, k=k, o=lo // rp: (o + step_of(*g) // k, 0)))
        out_specs.append(pl.BlockSpec((rp, c), lambda *g, k=k: (step_of(*g) // k, 0)))
        shapes.append(jax.ShapeDtypeStruct((r, c), BF16))
    return (_cast_body, tuple(ins), tuple(in_specs), tuple(out_specs), tuple(shapes))


def _layer_weights(i, ln1_g, w_in, conv_w, conv_b, dt_bias, a_log, d_ssd, gn_g, lam_re, lam_im, log_step,
                   b_re, b_im, c_re, c_im, d_s5, w_glu, b_glu, w_out, ln2_g, w_xq, w_xk, w_xv, w_xo, ln3_g,
                   w_gate, w_up, w_down):
    d = w_in.shape[1]
    n_heads = dt_bias.shape[1]
    dssd = n_heads * SSD_HEAD_DIM
    conv_dim = conv_w.shape[2]
    o1, o2, o3 = dssd, dssd + conv_dim, dssd + conv_dim + n_heads
    win = w_in[i]
    w = {}
    w["ln1"] = ln1_g[i].reshape(1, d)
    w["win"] = win
    w["wu_s5"] = win[:, o3:].astype(BF16)
    w["cw"] = conv_w[i]
    w["cb"] = conv_b[i].reshape(1, conv_dim)
    w["dtb"] = jnp.pad(dt_bias[i], (0, LANES - n_heads)).reshape(1, LANES)
    w["alog"] = jnp.pad(a_log[i], (0, LANES - n_heads)).reshape(1, LANES)
    w["dexp"] = jnp.repeat(d_ssd[i], SSD_HEAD_DIM).reshape(1, dssd)
    w["gn"] = gn_g[i].reshape(1, dssd)
    w["s5"] = (lam_re[i], lam_im[i], log_step[i], b_re[i], b_im[i], c_re[i], c_im[i])
    ds5 = d_s5.shape[1]
    w["dcol"] = d_s5[i].reshape(ds5, 1)
    w["wglut"] = w_glu[i].T.astype(BF16)
    w["bglu"] = b_glu[i].reshape(ds5, 1)
    w["cast_f32"] = ((w_out[i], 0, dssd), (w_out[i], dssd, w_out.shape[1]), w_xq[i], w_xo[i])
    w["ln2"] = ln2_g[i].reshape(1, d)
    w["wk"] = w_xk[i].astype(BF16)
    w["wv"] = w_xv[i].astype(BF16)
    w["ln3"] = ln3_g[i].reshape(1, d)
    w["ffn_f32"] = (w_gate[i], w_up[i], w_down[i])
    return w


def _rows_to_state(s):
    p = s.shape[-1] // 2
    st = s.transpose(1, 0, 2)
    return st[..., :p], st[..., p:]


def kernel(x_prompt, x_sample, mem_prompt, state_conv, state_ssm, state_s5_re, state_s5_im, cache_mem_k, cache_mem_v, ln1_g, w_in, conv_w, conv_b, dt_bias, a_log, d_ssd, gn_g, lam_re, lam_im, log_step, b_re, b_im, c_re, c_im, d_s5, w_glu, b_glu, w_out, ln2_g, w_xq, w_xk, w_xv, w_xo, ln3_g, w_gate, w_up, w_down, final_g):
    bp, seq, d = x_prompt.shape
    bs = x_sample.shape[0]
    depth = w_in.shape[0]
    nm = mem_prompt.shape[1]
    ng, ns = lam_re.shape[1], lam_re.shape[2]
    gf = final_g.reshape(1, d)
    hp = x_prompt
    hs = x_sample.reshape(bs, d)
    outs = {k: [] for k in ("conv_p", "ssm_p", "re_p", "im_p", "mk_p", "mv_p", "conv_s", "ssm_s", "re_s", "im_s")}
    yp = ys_out = None
    for i in range(depth):
        w = _layer_weights(i, ln1_g, w_in, conv_w, conv_b, dt_bias, a_log, d_ssd, gn_g, lam_re, lam_im, log_step,
                           b_re, b_im, c_re, c_im, d_s5, w_glu, b_glu, w_out, ln2_g, w_xq, w_xk, w_xv, w_xo, ln3_g,
                           w_gate, w_up, w_down)
        last = i == depth - 1
        attn_steps = bp * (seq // min(TOKEN_TILE, seq))
        y_ssd, conv_p, ssm_p, zs, xbcs, dts = _ssd_prompt(hp, w["ln1"], w["win"], w["cw"], w["cb"], w["dtb"],
                                                          w["alog"], w["dexp"], w["gn"], hs, bs // attn_steps)
        q = S5_Q
        prep_riders = lambda n: _merge_riders(
            _kv_proj_rider(mem_prompt.reshape(bp * nm, d), w["wk"], w["wv"], n),
            _cast_rider(w["cast_f32"], lambda i: i, n))
        (tt, wt, zt, a1, a2, a11, a21), (mk8, mv8, mk, mv, w["wo1"], w["wo2"], w["wq"], w["wxo"]) = _s5_prep(
            q, *w["s5"], prep_riders)
        mk = mk.reshape(bp, nm, d)
        mv = mv.reshape(bp, nm, d)
        s0 = jnp.zeros((ng, bp, 2 * ns), F32)
        pperm, sfin = _s5_mix(q, bp, hp, w["ln1"], w["wu_s5"],tt, wt, zt, a1, a2, s0, w["dcol"], w["wglut"],
                              w["bglu"], w["wo2"])
        pm = pperm.reshape(q, seq // q, bp, d).transpose(2, 1, 0, 3).reshape(bp, seq, d)
        re_p, im_p = _rows_to_state(sfin)
        riders = lambda step_of, n: _merge_riders(
            _ssd_step_rider(zs, xbcs, dts, state_conv[i], state_ssm[i], w["cw"], w["cb"], w["dtb"], w["alog"],
                            w["dexp"], w["gn"], step_of, n),
            _cast_rider(w["ffn_f32"], step_of, n))
        x2, ys_ssd, conv_s, ssm_s, w["wg"], w["wu"], w["wdn"] = _attn_prompt(
            hp, y_ssd, pm, w["wo1"], w["ln2"], w["wq"], mk, mv, w["wxo"], riders)
        ys_ssd = ys_ssd.reshape(bs, -1)
        tt1, wt1, zt1 = _s5_single_token_mats(q, tt, wt, zt)
        s0s = jnp.stack([state_s5_re[i].reshape(bs, ng * ns), state_s5_im[i].reshape(bs, ng * ns)])
        ps, sfin_s = _s5_mix(1, bs, hs.reshape(bs, 1, d), w["ln1"], w["wu_s5"],tt1, wt1, zt1, a11, a21, s0s,
                             w["dcol"], w["wglut"], w["bglu"], w["wo2"])
        re_s, im_s = sfin_s[0].reshape(bs, ng, ns), sfin_s[1].reshape(bs, ng, ns)
        x1s, qs = _merge_q(hs, ys_ssd, ps.reshape(bs, d), w["wo1"], w["ln2"], w["wq"])
        attn_rider = _attn_step_rider(_tile_view(qs.reshape(bs, 1, X_HEADS, d // X_HEADS))[:, 0],
                                      _tile_view(cache_mem_k[i]), _tile_view(cache_mem_v[i]),
                                      (bp * seq) // min(TOKEN_TILE, bp * seq))
        hp, o_s = _ffn(x2.reshape(bp * seq, d), w["ln3"], w["wg"], w["wu"], w["wdn"], gf, last, rider=attn_rider)
        hp = hp.reshape(bp, seq, d)
        hs = _ffn(x1s, w["ln3"], w["wg"], w["wu"], w["wdn"], gf, last, o_in=_tile_unview(o_s), wxo=w["wxo"])
        for k, v in (("conv_p", conv_p), ("ssm_p", ssm_p), ("re_p", re_p), ("im_p", im_p),
                     ("mk_p", _tile_unview_kv(mk8, bp, nm)), ("mv_p", _tile_unview_kv(mv8, bp, nm)),
                     ("conv_s", conv_s), ("ssm_s", ssm_s), ("re_s", re_s), ("im_s", im_s)):
            outs[k].append(v)
    st = lambda k: jnp.stack(outs[k])
    return (hp, hs.reshape(bs, 1, d), st("conv_p"), st("ssm_p"), st("re_p"), st("im_p"), st("mk_p"), st("mv_p"),
            st("conv_s"), st("ssm_s"), st("re_s"), st("im_s"))
```

```python
import functools
import math

import jax
import jax.numpy as jnp
from jax import lax
from jax.experimental import pallas as pl
from jax.experimental.pallas import tpu as pltpu

F32 = jnp.float32
BF16 = jnp.bfloat16
EPS = 1e-6

LANES = 128
VMEM_LIMIT = 60 * 1024 * 1024

SSD_HEAD_DIM = 64
SSD_STATE = 128
SSD_GROUPS = 2
SSD_CHUNK = 128
CONV_W = 4
S5_CH = 16
S5_STATE = 64
S5_Q = 16
S5_GROUP_BATCH = 32
S5_SLICE_TOKENS = 512
X_HEADS = 4
TOKEN_TILE = 512
PROJ_PIECE = 256
SSD_SEQS_PER_STEP = 4
FF_CHUNK = 1024


def _const_spec(shape):
    nd = len(shape)
    return pl.BlockSpec(shape, lambda *_: (0,) * nd, pipeline_mode=pl.Buffered(1))


def _params(*sem):
    return pltpu.CompilerParams(dimension_semantics=sem, vmem_limit_bytes=VMEM_LIMIT)


def _rms(x, g):
    return x * lax.rsqrt(jnp.mean(x * x, axis=-1, keepdims=True) + EPS) * g


def _sigmoid(x):
    return 1.0 / (1.0 + jnp.exp(-x))


def _silu(x):
    return x * _sigmoid(x)


def _gelu_tanh(x):
    return 0.5 * x * (1.0 + jnp.tanh(math.sqrt(2.0 / math.pi) * (x + 0.044715 * (x * x * x))))


def _softplus(x):
    return jnp.maximum(x, 0.0) + jnp.log1p(jnp.exp(-jnp.abs(x)))


def _dot(a, b):
    return jnp.dot(a, b, preferred_element_type=F32)


def _dot_nt(a, b):
    return lax.dot_general(a, b, (((1,), (1,)), ((), ())), preferred_element_type=F32)


def _dot_tn(a, b):
    return lax.dot_general(a, b, (((0,), (0,)), ((), ())), preferred_element_type=F32)


def _proj_windows(d_ssd, conv_dim):
    assert d_ssd % LANES == 0 and conv_dim % LANES == 0
    return (0, d_ssd), (d_ssd, d_ssd + conv_dim), (d_ssd + conv_dim, d_ssd + conv_dim + LANES)


def _expand_heads(v, n_heads):
    rows = v.shape[0]
    lane = lax.broadcasted_iota(jnp.int32, (rows, LANES), 1)
    pieces = [jnp.where(lane < SSD_HEAD_DIM, v[:, 2 * j:2 * j + 1], v[:, 2 * j + 1:2 * j + 2])
              for j in range(n_heads // 2)]
    return jnp.concatenate(pieces, axis=1)


def _ssd_chunk_math(z, x, dt_raw, last_chunk, cw_ref, cb_ref, dtb_ref, alog_ref, dexp_ref, gn_ref, e2_ref,
                    y_ref, conv_ref, ssm_ref, xpad, state, side_jobs=()):
    jobs = list(side_jobs)

    def side(n=1):
        for _ in range(n):
            if jobs:
                jobs.pop(0)()

    q = SSD_CHUNK
    d_ssd = z.shape[-1]
    n_heads = d_ssd // SSD_HEAD_DIM
    hpg = n_heads // SSD_GROUPS
    gw = hpg * SSD_HEAD_DIM
    nct = xpad.shape[0]
    xt = d_ssd // LANES

    for t in range(nct):
        xpad[t, 8:8 + q, :] = x[:, t * LANES:(t + 1) * LANES]
    cw = cw_ref[...]
    conv = (cw[3] * xpad[:, 8:8 + q, :] + cw[2] * xpad[:, 7:7 + q, :] + cw[1] * xpad[:, 6:6 + q, :]
            + cw[0] * xpad[:, 5:5 + q, :] + cb_ref[...])
    xpad[:, 0:8, :] = xpad[:, q:q + 8, :]
    xact = _silu(conv)
    xs = jnp.concatenate([xact[t] for t in range(xt)], axis=1)
    bm = xact[xt:xt + SSD_GROUPS].astype(BF16)
    cm = xact[xt + SSD_GROUPS:].astype(BF16)
    side()

    dt = _softplus(dt_raw + dtb_ref[...])
    a = -jnp.exp(alog_ref[...])
    row = lax.broadcasted_iota(jnp.int32, (q, q), 0)
    col = lax.broadcasted_iota(jnp.int32, (q, q), 1)
    causal = row >= col
    tri = jnp.where(causal, 1.0, 0.0).astype(F32)
    acum = jnp.dot(tri, dt * a, precision=lax.Precision.HIGHEST, preferred_element_type=F32)
    acum_t = acum.T
    last = acum[q - 1:q, :]

    def expand(v):
        hi = v.astype(BF16)
        lo = (v - hi.astype(F32)).astype(BF16)
        return _dot(jnp.concatenate([hi, lo], axis=1), e2_ref[...])

    dt_e = expand(dt)
    ea_e = expand(jnp.exp(acum))
    dend_e = expand(jnp.exp(last - acum))

    dtx = xs * dt_e
    dtx_b = dtx.astype(BF16)
    xdec_b = (dtx * dend_e).astype(BF16)
    side()

    lane = lax.broadcasted_iota(jnp.int32, (q, LANES), 1)
    zero_b = jnp.zeros((q, LANES), BF16)
    y_tiles = []
    for g in range(SSD_GROUPS):
        bg = bm[g]
        cg = cm[g]
        cb = _dot_nt(cg, bg)
        gs = slice(g * gw, (g + 1) * gw)
        st_g = state[:, gs]
        y_off = _dot(cg, st_g.astype(BF16)) * ea_e[:, gs]
        for pr in range(hpg // 2):
            tile = g * (hpg // 2) + pr
            ms = []
            for h in (2 * tile, 2 * tile + 1):
                seg = acum[:, h:h + 1] - acum_t[h:h + 1, :]
                lmat = jnp.exp(jnp.where(causal, seg, -jnp.inf))
                ms.append((cb * lmat).astype(BF16))
            dtile = dtx_b[:, tile * LANES:(tile + 1) * LANES]
            rhs = jnp.concatenate([jnp.where(lane < SSD_HEAD_DIM, dtile, zero_b),
                                   jnp.where(lane >= SSD_HEAD_DIM, dtile, zero_b)], axis=0)
            y_tiles.append(_dot(jnp.concatenate(ms, axis=1), rhs)
                           + y_off[:, pr * LANES:(pr + 1) * LANES])
            if pr % 2 == 1:
                side()
        new = _dot_tn(bg, xdec_b[:, gs])
        state[:, gs] = st_g * ea_e[q - 1:q, gs] + new
    y = jnp.concatenate(y_tiles, axis=1) + dexp_ref[...] * xs
    y = y * _silu(z)
    ms = jnp.mean(y * y, axis=-1, keepdims=True)
    side(len(jobs))
    y_ref[...] = (y * lax.rsqrt(ms + EPS) * gn_ref[...]).astype(y_ref.dtype)

    @pl.when(last_chunk)
    def _():
        conv_ref[...] = x[q - (CONV_W - 1):q, :]
        ssm_ref[...] = state[...].T.reshape(ssm_ref.shape)


def _ssd_prompt_body(wins, x_ref, g_ref, w_ref, cw_ref, cb_ref, dtb_ref, alog_ref, dexp_ref, gn_ref,
                     e2_ref, xs_ref, y_ref, conv_ref, ssm_ref, zs_ref, xbcs_ref, dts_ref,
                     za, xa, da, zb, xb, db, xpad, state):
    i = pl.program_id(0)
    j = pl.program_id(1)
    nc = pl.num_programs(1) - 1
    ns, q, d = x_ref.shape

    @pl.when(jnp.logical_and(i == 0, j == 0))
    def _():
        for r in (zb, xb, db):
            r[...] = jnp.zeros(r.shape, F32)
        hs = _rms(xs_ref[...], g_ref[...]).astype(BF16)
        for (lo, hi), o_ref in zip(wins, (zs_ref, xbcs_ref, dts_ref)):
            o_ref[...] = _dot(hs, w_ref[:, lo:hi]).reshape(o_ref.shape)

    @pl.when(j <= 1)
    def _():
        xpad[:, :, 0:8, :] = jnp.zeros(xpad.shape[:2] + (8, LANES), F32)
        state[...] = jnp.zeros(state.shape, F32)

    def step(wr, rd):
        xin = x_ref[...].reshape(ns * q, d)
        msx = jnp.mean(xin * xin, axis=-1, keepdims=True)
        hbox = []

        def proj_piece(dst, base, lo, hi):
            def run():
                if not hbox:
                    hbox.append((xin * lax.rsqrt(msx + EPS) * g_ref[...]).astype(BF16))
                dst[:, lo:hi] = _dot(hbox[0], w_ref[:, base + lo:base + hi])
            return run

        pieces = []
        for dst, (w_lo, w_hi) in ((wr[0], wins[0]), (wr[2], wins[2]), (wr[1], wins[1])):
            n = w_hi - w_lo
            pieces += [proj_piece(dst, w_lo, lo, min(lo + PROJ_PIECE, n)) for lo in range(0, n, PROJ_PIECE)]
        per = -(-len(pieces) // ns)
        for s in range(ns):
            rows = slice(s * q, (s + 1) * q)
            _ssd_chunk_math(rd[0][rows, :], rd[1][rows, :], rd[2][rows, :], j == nc, cw_ref, cb_ref, dtb_ref,
                            alog_ref, dexp_ref, gn_ref, e2_ref, y_ref.at[s], conv_ref.at[s], ssm_ref.at[s],
                            xpad.at[s], state.at[s], pieces[s * per:(s + 1) * per])

    @pl.when(j % 2 == 0)
    def _():
        step((za, xa, da), (zb, xb, db))

    @pl.when(j % 2 == 1)
    def _():
        step((zb, xb, db), (za, xa, da))


def _ssd_prompt(x, ln_g, w_in, cw, cb, dtb, alog, dexp, gn, xs, group):
    b, l, d = x.shape
    ts = xs.shape[0]
    conv_dim = cw.shape[1]
    d_ssd = conv_dim - 2 * SSD_GROUPS * SSD_STATE
    wins = _proj_windows(d_ssd, conv_dim)
    n_heads = d_ssd // SSD_HEAD_DIM
    q = SSD_CHUNK
    nc = l // q
    assert SSD_STATE == LANES and conv_dim == d_ssd + 2 * SSD_GROUPS * SSD_STATE
    nct = conv_dim // LANES
    cw4 = cw.reshape(CONV_W, nct, 1, LANES)
    cb3 = cb.reshape(nct, 1, LANES)
    e2 = (jnp.arange(2 * LANES)[:, None] % LANES == jnp.arange(d_ssd)[None, :] // SSD_HEAD_DIM).astype(BF16)
    ns = SSD_SEQS_PER_STEP if b % SSD_SEQS_PER_STEP == 0 else 1
    slot = lambda n: pltpu.VMEM((ns * q, n), F32)
    return pl.pallas_call(
        functools.partial(_ssd_prompt_body, wins),
        grid=(b // ns, nc + 1),
        in_specs=[pl.BlockSpec((ns, q, d), lambda i, j: (i, jnp.minimum(j, nc - 1), 0)),
                  _const_spec(ln_g.shape), _const_spec(w_in.shape),
                  _const_spec(cw4.shape), _const_spec(cb3.shape), _const_spec(dtb.shape), _const_spec(alog.shape),
                  _const_spec(dexp.shape), _const_spec(gn.shape), _const_spec(e2.shape), _const_spec(xs.shape)],
        out_specs=[pl.BlockSpec((ns, q, d_ssd), lambda i, j: (i, jnp.maximum(j - 1, 0), 0)),
                   pl.BlockSpec((ns, CONV_W - 1, conv_dim), lambda i, j: (i, 0, 0)),
                   pl.BlockSpec((ns, n_heads, SSD_HEAD_DIM, SSD_STATE), lambda i, j: (i, 0, 0, 0))]
        + [pl.BlockSpec((ts // group, group, hi - lo), lambda i, j: (0, 0, 0)) for lo, hi in wins],
        out_shape=[jax.ShapeDtypeStruct((b, l, d_ssd), BF16),
                   jax.ShapeDtypeStruct((b, CONV_W - 1, conv_dim), F32),
                   jax.ShapeDtypeStruct((b, n_heads, SSD_HEAD_DIM, SSD_STATE), F32)]
        + [jax.ShapeDtypeStruct((ts // group, group, hi - lo), F32) for lo, hi in wins],
        scratch_shapes=[slot(d_ssd), slot(conv_dim), slot(LANES), slot(d_ssd), slot(conv_dim), slot(LANES),
                        pltpu.VMEM((ns, nct, q + 8, LANES), F32), pltpu.VMEM((ns, SSD_STATE, d_ssd), F32)],
        compiler_params=_params("arbitrary", "arbitrary"),
        name="ssd_prompt",
    )(x, ln_g, w_in, cw4, cb3, dtb, alog, dexp, gn, e2, xs)


def _ssd_step_body(z_ref, xbc_ref, dt_ref, cs_ref, st_ref, cw_ref, cb_ref, dtb_ref, alog_ref, dexp_ref, gn_ref,
                   y_ref, conv_ref, ssm_ref):
    bt, d_ssd = z_ref.shape
    n_heads = d_ssd // SSD_HEAD_DIM
    hpg = n_heads // SSD_GROUPS
    x = xbc_ref[...]
    cw = cw_ref[...]
    cs = [cs_ref[:, k, :] for k in range(CONV_W - 1)]
    conv = cw[0:1] * cs[0] + cw[1:2] * cs[1] + cw[2:3] * cs[2] + cw[3:4] * x + cb_ref[...]
    conv_ref[:, 0, :] = cs[1]
    conv_ref[:, 1, :] = cs[2]
    conv_ref[:, 2, :] = x
    xact = _silu(conv)
    xs = xact[:, :d_ssd]
    bm = xact[:, d_ssd:d_ssd + SSD_GROUPS * SSD_STATE]
    cm = xact[:, d_ssd + SSD_GROUPS * SSD_STATE:]
    dt = _softplus(dt_ref[...] + dtb_ref[...])
    da = jnp.exp(dt * (-jnp.exp(alog_ref[...])))
    dtx = xs * _expand_heads(dt, n_heads)
    pad = jnp.zeros((LANES - bt, d_ssd), F32)
    to_cols = lambda v: jnp.concatenate([v, pad], axis=0).T
    dtx_t = to_cols(dtx)
    da_t = to_cols(_expand_heads(da, n_heads))
    lane = lax.broadcasted_iota(jnp.int32, (d_ssd, LANES), 1)
    gw = hpg * SSD_HEAD_DIM
    rows_of = lambda v, b: jnp.concatenate(
        [jnp.broadcast_to(v[b:b + 1, g * SSD_STATE:(g + 1) * SSD_STATE], (gw, SSD_STATE)) for g in range(SSD_GROUPS)],
        axis=0)
    y_t = jnp.zeros((d_ssd, LANES), F32)
    for b in range(bt):
        s0 = st_ref[b].reshape(d_ssd, SSD_STATE)
        s_new = s0 * da_t[:, b:b + 1] + dtx_t[:, b:b + 1] * rows_of(bm, b)
        ssm_ref[b] = s_new.reshape(n_heads, SSD_HEAD_DIM, SSD_STATE)
        ycol = jnp.sum(s_new * rows_of(cm, b), axis=1, keepdims=True)
        y_t = jnp.where(lane == b, ycol, y_t)
    y = y_t.T[:bt, :] + dexp_ref[...] * xs
    y = y * _silu(z_ref[...])
    y_ref[...] = _rms(y, gn_ref[...]).astype(y_ref.dtype)


def _ssd_step_rider(z, xbc, dt, conv_state, ssm_state, cw, cb, dtb, alog, dexp, gn, step_of, nsteps):
    _, bt, d_ssd = z.shape
    conv_dim = xbc.shape[-1]
    n_heads = d_ssd // SSD_HEAD_DIM
    assert z.shape[0] == nsteps and conv_state.shape[0] == nsteps * bt
    row = lambda n: pl.BlockSpec((None, bt, n), lambda *g: (step_of(*g), 0, 0))
    cs_spec = pl.BlockSpec((bt, CONV_W - 1, conv_dim), lambda *g: (step_of(*g), 0, 0))
    st_spec = pl.BlockSpec((bt, n_heads, SSD_HEAD_DIM, SSD_STATE), lambda *g: (step_of(*g), 0, 0, 0))
    cst = lambda a: pl.BlockSpec(a.shape, lambda *g: (0,) * a.ndim, pipeline_mode=pl.Buffered(1))
    ins = (z, xbc, dt, conv_state, ssm_state, cw, cb, dtb, alog, dexp, gn)
    specs = (row(d_ssd), row(conv_dim), row(LANES), cs_spec, st_spec, cst(cw), cst(cb), cst(dtb), cst(alog),
             cst(dexp), cst(gn))
    out_shapes = (jax.ShapeDtypeStruct((nsteps, bt, d_ssd), BF16),
                  jax.ShapeDtypeStruct(conv_state.shape, F32),
                  jax.ShapeDtypeStruct(ssm_state.shape, F32))
    return (_ssd_step_body, ins, specs, (row(d_ssd), cs_spec, st_spec), out_shapes)


def _split_bf16(x):
    hi = x.astype(BF16)
    lo = (x - hi.astype(F32)).astype(BF16)
    return hi, lo


def _s5_prep_body(q, ls_ref, lrp_ref, lip_ref, lr2_ref, li2_ref, br_ref, bi_ref, ccat_ref, ca_ref, cb_ref, rep_ref,
                  tt_ref, wt_ref, zt_ref, a1_ref, a2_ref, a1s_ref, a2s_ref):
    ng = ls_ref.shape[0]
    w = q * S5_CH
    step = jnp.exp(ls_ref[...])
    lr, li = lrp_ref[...], lip_ref[...]
    mag = jnp.exp(lr * step)
    ang = li * step
    lbr = mag * jnp.cos(ang)
    lbi = mag * jnp.sin(ang)
    den = lr * lr + li * li
    kr = ((lbr - 1.0) * lr + lbi * li) / den
    ki = (lbi * lr - (lbr - 1.0) * li) / den
    br, bi = br_ref[...], bi_ref[...]
    bbr = kr * br - ki * bi
    bbi = kr * bi + ki * br
    if q > 1:
        np_ = br.shape[1]
        rep = lambda v: jnp.dot(v.reshape(ng * np_, S5_CH), rep_ref[...], precision=lax.Precision.HIGHEST,
                                preferred_element_type=F32).reshape(ng, np_, w)
        bbr, bbi = rep(bbr), rep(bbi)
    if q == 1:
        ball = jnp.concatenate([bbr, bbi], axis=1)
    else:
        d = lax.broadcasted_iota(jnp.int32, (1, 1, w), 2) // S5_CH
        fr, fi = lbr, lbi
        pr = pi = None
        for b in range((q - 1).bit_length()):
            bit = ((d >> b) & 1) == 1
            sr, si = jnp.where(bit, fr, 1.0), jnp.where(bit, fi, 0.0)
            pr, pi = (sr, si) if pr is None else (pr * sr - pi * si, pr * si + pi * sr)
            fr, fi = fr * fr - fi * fi, 2.0 * (fr * fi)
        ball = jnp.concatenate([pr * bbr - pi * bbi, pr * bbi + pi * bbr], axis=1)
    wt_ref[...] = ball.astype(BF16)

    lane3 = lax.broadcasted_iota(jnp.int32, (1, 1, 2 * S5_STATE), 2)
    first = lane3 < S5_STATE
    csign = jnp.where(first, ccat_ref[...], -ccat_ref[...])
    ch, cl = _split_bf16(csign)
    bh, bl = _split_bf16(ball)
    bdot = lambda x, y: lax.dot_general(x, y, (((2,), (1,)), ((0,), (0,))), preferred_element_type=F32)
    kall = bdot(ch, bh) + bdot(ch, bl) + bdot(cl, bh)
    k2 = kall.reshape(ng * S5_CH, w)
    lane2 = lax.broadcasted_iota(jnp.int32, (ng * S5_CH, w), 1)
    for t in range(q):
        sh = (q - 1 - t) * S5_CH
        r = pltpu.roll(k2, sh, 1) if sh else k2
        r = jnp.where(lane2 >= sh, r, 0.0)
        tt_ref[:, t * S5_CH:(t + 1) * S5_CH, :] = r.reshape(ng, S5_CH, w).astype(BF16)

    lr2, li2 = lr2_ref[...], li2_ref[...]
    t1 = (lax.broadcasted_iota(jnp.int32, (1, q, 1), 1) + 1).astype(F32)
    zm = jnp.exp(t1 * (lr2 * step))
    za = t1 * (li2 * step)
    zr = zm * jnp.cos(za)
    zi = zm * jnp.sin(za)
    ca, cb = ca_ref[...], cb_ref[...]
    for t in range(q):
        prt = zr[:, t:t + 1, :]
        pit = zi[:, t:t + 1, :]
        zt = jnp.where(first, ca * prt - cb * pit, -(ca * pit) - cb * prt)
        zt_ref[:, t * S5_CH:(t + 1) * S5_CH, :] = zt.astype(BF16)
    qf = float(q)
    mq = jnp.exp(qf * (lr2 * step))
    aq = qf * (li2 * step)
    ar = mq * jnp.cos(aq)
    ai = mq * jnp.sin(aq)
    a1_ref[...] = ar
    a2_ref[...] = jnp.where(first, -ai, ai)
    a1s_ref[...] = zr[:, 0:1, :]
    a2s_ref[...] = jnp.where(first, -zi[:, 0:1, :], zi[:, 0:1, :])


def _s5_prep(q, lam_re, lam_im, log_step, b_re, b_im, c_re, c_im, make_rider):
    ng, p = lam_re.shape
    w = q * S5_CH
    ls = log_step.reshape(ng, 1, 1)
    lrp = lam_re.reshape(ng, p, 1)
    lip = lam_im.reshape(ng, p, 1)
    lr2 = jnp.concatenate([lam_re, lam_re], axis=-1).reshape(ng, 1, 2 * p)
    li2 = jnp.concatenate([lam_im, lam_im], axis=-1).reshape(ng, 1, 2 * p)
    rep = (jnp.arange(S5_CH)[:, None] == jnp.arange(w)[None, :] % S5_CH).astype(F32)
    ccat = jnp.concatenate([c_re, c_im], axis=-1)
    ca = jnp.concatenate([c_re, c_re], axis=-1)
    cb = jnp.concatenate([c_im, c_im], axis=-1)
    gb = ng if q == 1 else 16
    blk = lambda a, b: pl.BlockSpec((gb, a, b), lambda i: (i, 0, 0))
    ins = (ls, lrp, lip, lr2, li2, b_re, b_im, ccat, ca, cb, rep)
    in_specs = [blk(1, 1), blk(p, 1), blk(p, 1), blk(1, 2 * p), blk(1, 2 * p), blk(p, S5_CH), blk(p, S5_CH),
                blk(S5_CH, 2 * p), blk(S5_CH, 2 * p), blk(S5_CH, 2 * p), _const_spec(rep.shape)]
    out_specs = [blk(w, w), blk(2 * p, w), blk(w, 2 * p)] + [blk(1, 2 * p)] * 4
    out_shape = [jax.ShapeDtypeStruct((ng, w, w), BF16), jax.ShapeDtypeStruct((ng, 2 * p, w), BF16),
                 jax.ShapeDtypeStruct((ng, w, 2 * p), BF16)] + [jax.ShapeDtypeStruct((ng, 1, 2 * p), F32)] * 4
    body = functools.partial(_s5_prep_body, q)
    rider = make_rider(ng // gb)
    body = functools.partial(_two_bodies, body, len(ins), len(out_specs), rider[0], len(rider[1]))
    res = pl.pallas_call(
        body,
        grid=(ng // gb,),
        in_specs=in_specs + list(rider[2]),
        out_specs=out_specs + list(rider[3]),
        out_shape=out_shape + list(rider[4]),
        compiler_params=_params("arbitrary"),
        name=f"s5_prep_q{q}",
    )(*ins, *rider[1])
    return res[:len(out_specs)], res[len(out_specs):]


def _s5_single_token_mats(q, tt, wt, zt):
    c = S5_CH
    return tt[:, :c, (q - 1) * c:], wt[:, :, :c], zt[:, :c, :]


def _s5_mix_body(q, nb, nkb, npi, npt, perm, x_ref, g_ref, wu_ref, tt_ref, wt_ref, zt_ref, a1_ref, a2_ref, s0_ref,
                 dcol_ref, wglut_ref, bglu_ref, wo_ref, p_ref, sfin_ref, ut, yt, carry, wut):
    ng = tt_ref.shape[0]
    nk = nkb // nb
    qs = q // npt
    ch = qs * nkb
    tile = pl.program_id(0)
    ph = pl.program_id(1)

    @pl.when(jnp.logical_and(tile == 0, ph == 0))
    def _():
        if perm:
            carry[...] = s0_ref[...]
        else:
            lane_s = lax.broadcasted_iota(jnp.int32, (nb, 2 * S5_STATE), 1)
            for pr in range(ng // 2):
                cols = slice(pr * 2 * S5_STATE, (pr + 1) * 2 * S5_STATE)
                re_t, im_t = s0_ref[0, :, cols], s0_ref[1, :, cols]
                carry[2 * pr] = jnp.where(lane_s < S5_STATE, re_t, pltpu.roll(im_t, S5_STATE, 1))
                carry[2 * pr + 1] = jnp.where(lane_s < S5_STATE, pltpu.roll(re_t, S5_STATE, 1), im_t)
        cw_ = 256
        for c in range(0, wu_ref.shape[1], cw_):
            wut[c:c + cw_, :] = wu_ref[:, c:c + cw_].T

    def rows_of(r):
        if not perm:
            return x_ref[r]
        nseq, nblk, ndt, nr, _ = x_ref.shape
        flat = x_ref.reshape(nseq * nblk * ndt * nr, LANES)

        return jnp.concatenate(
            [jnp.concatenate([flat[pl.ds((k * ndt + dt) * nr + r, nseq, stride=nblk * ndt * nr), :]
                              for dt in range(ndt)], axis=1) for k in range(nblk)], axis=0)

    def project(j):
        step = 2 if qs % 2 == 0 else 1
        r0 = (j * qs) % (q // npi)
        for r in range(0, qs, step):
            xin = jnp.concatenate([rows_of(r0 + r + s) for s in range(step)], axis=0)
            u = _dot_nt(wut[...], _rms(xin, g_ref[...]).astype(BF16))
            sl = slice((j * qs + r) * nkb, (j * qs + r + step) * nkb)
            ut[:, sl] = u.astype(BF16)
            yt[:, sl] = dcol_ref[...] * u

    def finish(j):
        gt = _gelu_tanh(yt[:, j * ch:(j + 1) * ch])
        gate = _dot(wglut_ref[...], gt.astype(BF16)) + bglu_ref[...]
        y5 = (gt * _sigmoid(gate)).astype(BF16)
        p_ref[...] = _dot_tn(y5, wo_ref[...]).reshape(p_ref.shape)

    spp = npt // npi
    for hh in range(npi):
        @pl.when(ph == hh)
        def _():
            for j in range(hh * spp, (hh + 1) * spp):
                project(j)

    @pl.when(ph == npi - 1)
    def _():
        gu = S5_GROUP_BATCH
        bdot = lambda a, b: lax.dot_general(a, b, (((2,), (1,)), ((0,), (0,))), preferred_element_type=F32)
        bdot_nt = lambda a, b: lax.dot_general(a, b, (((2,), (2,)), ((0,), (0,))), preferred_element_type=F32)

        def groups(i, _):
            g0 = pl.multiple_of(i * gu, gu)
            r0 = pl.multiple_of(i * (gu * S5_CH), gu * S5_CH)
            gsl = pl.ds(g0, gu)
            rows = ut[pl.ds(r0, gu * S5_CH), :].reshape(gu, S5_CH, q * nkb)
            ugt = jnp.concatenate([rows[:, :, (q - 1 - j) * nkb:(q - j) * nkb] for j in range(q)], axis=1)
            y = bdot(tt_ref[gsl], ugt)
            vt = bdot(wt_ref[gsl], ugt)
            v = jnp.swapaxes(vt, 1, 2)
            v_sw = jnp.swapaxes(jnp.concatenate([vt[:, S5_STATE:], vt[:, :S5_STATE]], axis=1), 1, 2)
            a1 = a1_ref[gsl]
            a2 = a2_ref[gsl]
            s = carry[gsl]
            s_sw = pltpu.roll(s.reshape(gu * nb, 2 * S5_STATE), S5_STATE, 1).reshape(s.shape)
            prev = []
            for k in range(nk):
                prev.append(s)
                s, s_sw = (a1 * s + a2 * s_sw + v[:, k * nb:(k + 1) * nb, :],
                           a1 * s_sw - a2 * s + v_sw[:, k * nb:(k + 1) * nb, :])
            carry[gsl] = s
            sprev = jnp.concatenate(prev, axis=1).astype(BF16)
            y = y + bdot_nt(zt_ref[gsl], sprev)
            for t in range(q):
                yt[pl.ds(r0, gu * S5_CH), t * nkb:(t + 1) * nkb] += (
                    y[:, t * S5_CH:(t + 1) * S5_CH, :].reshape(gu * S5_CH, nkb))
            return 0

        lax.fori_loop(0, ng // gu, groups, 0)
        if perm:
            sfin_ref[...] = carry[...]
        else:
            lane_s = lax.broadcasted_iota(jnp.int32, (nb, 2 * S5_STATE), 1)
            for pr in range(ng // 2):
                cols = slice(pr * 2 * S5_STATE, (pr + 1) * 2 * S5_STATE)
                c0, c1 = carry[2 * pr], carry[2 * pr + 1]
                sfin_ref[0, :, cols] = jnp.where(lane_s < S5_STATE, c0, pltpu.roll(c1, S5_STATE, 1))
                sfin_ref[1, :, cols] = jnp.where(lane_s < S5_STATE, pltpu.roll(c0, S5_STATE, 1), c1)

    for j in range(npt):
        @pl.when(ph == npi + j)
        def _():
            finish(j)


def _s5_mix(q, nb, x, ln_g, wu, tt, wt, zt, a1, a2, s0, dcol, wglut, bglu, wo):
    nseq, l, d = x.shape
    nblk = l // q
    nlt = nblk * nseq
    nkb = min(LANES, nlt)
    ntile = nlt // nkb
    tok = q * nkb
    npt = max(1, tok // S5_SLICE_TOKENS)
    dm = wo.shape[1]
    perm = q > 1
    if perm:
        half = 8
        npi = q // half
        assert nb == nseq and d % LANES == 0 and q % half == 0
        xv = x.reshape(nseq, nblk, npi, half, d // LANES, LANES).transpose(0, 1, 2, 4, 3, 5)
        assert npt % npi == 0
        x_spec = pl.BlockSpec((nseq, nkb // nseq, None, d // LANES, half, LANES),
                              lambda i, j: (0, i, jnp.minimum(j, npi - 1), 0, 0, 0))
    else:
        npi = 1
        xv = x.reshape(1, nseq, d)
        x_spec = pl.BlockSpec((1, nkb, d), lambda i, j: (0, i, 0))
    body = functools.partial(_s5_mix_body, q, nb, nkb, npi, npt, perm)
    p_spec = pl.BlockSpec((q // npt, nkb, dm), lambda i, j: (jnp.maximum(j - npi, 0), i, 0))
    return pl.pallas_call(
        body,
        grid=(ntile, npi + npt),
        in_specs=[x_spec, _const_spec(ln_g.shape), _const_spec(wu.shape), _const_spec(tt.shape),
                  _const_spec(wt.shape), _const_spec(zt.shape), _const_spec(a1.shape), _const_spec(a2.shape),
                  _const_spec(s0.shape), _const_spec(dcol.shape), _const_spec(wglut.shape),
                  _const_spec(bglu.shape), _const_spec(wo.shape)],
        out_specs=[p_spec, pl.BlockSpec(s0.shape, lambda i, j: (0, 0, 0))],
        out_shape=[jax.ShapeDtypeStruct((q, nlt, dm), F32), jax.ShapeDtypeStruct(s0.shape, F32)],
        scratch_shapes=[pltpu.VMEM((d, tok), BF16), pltpu.VMEM((d, tok), F32),
                        pltpu.VMEM((tt.shape[0], nb, 2 * S5_STATE), F32),
                        pltpu.VMEM((wu.shape[1], wu.shape[0]), BF16)],
        compiler_params=_params("arbitrary", "arbitrary"),
        name=f"s5_mix_q{q}",
    )(xv, ln_g, wu, tt, wt, zt, a1, a2, s0, dcol, wglut, bglu, wo)


def _kv_body(m_ref, wk_ref, wv_ref, k_ref, v_ref, kb_ref, vb_ref):
    tm, rows, _ = k_ref.shape
    nt = rows // X_HEADS
    m = m_ref[...].astype(BF16)
    for w_ref, o_ref, ob_ref in ((wk_ref, k_ref, kb_ref), (wv_ref, v_ref, vb_ref)):
        r = _dot(m, w_ref[...])
        ob_ref[...] = r.astype(BF16)
        flat = o_ref.reshape(tm * rows, LANES)
        for h in range(X_HEADS):
            for dt in range(nt):
                c = (h * nt + dt) * LANES
                flat[pl.ds(dt * X_HEADS + h, tm, stride=rows), :] = r[:, c:c + LANES]


def _kv_proj_rider(mem2d, wk, wv, nsteps):
    t, d = mem2d.shape
    assert t % nsteps == 0
    tm = t // nsteps
    rows = d // LANES
    row = pl.BlockSpec((tm, d), lambda i: (i, 0))
    tile = pl.BlockSpec((tm, rows, LANES), lambda i: (i, 0, 0))
    return (_kv_body, (mem2d, wk, wv), (row, _const_spec(wk.shape), _const_spec(wv.shape)),
            (tile, tile, row, row),
            (jax.ShapeDtypeStruct((t, rows, LANES), F32),) * 2 + (jax.ShapeDtypeStruct((t, d), BF16),) * 2)


def _tile_unview_kv(kv8, b, m):
    nt = kv8.shape[1] // X_HEADS
    return kv8.reshape(b, m, nt, X_HEADS, LANES).transpose(0, 1, 3, 2, 4).reshape(b, m, X_HEADS, nt * LANES)


def _attn_prompt_body(x_ref, ys_ref, p_ref, wo1_ref, g2_ref, wq_ref, k_ref, v_ref, wxo_ref, o_ref):
    d = x_ref.shape[-1]
    hd = d // X_HEADS
    x1 = x_ref[...] + _dot(ys_ref[...], wo1_ref[...]) + p_ref[...]
    hq = _rms(x1, g2_ref[...]).astype(BF16)
    qv = _dot(hq, wq_ref[...]).astype(BF16)
    kb = k_ref[...]
    vb = v_ref[...]
    outs = []
    for h in range(X_HEADS):
        sl = slice(h * hd, (h + 1) * hd)
        s = _dot_nt(qv[:, sl], kb[:, sl]) * (hd ** -0.5)
        e = jnp.exp(s - jnp.max(s, axis=-1, keepdims=True))
        p = (e / jnp.sum(e, axis=-1, keepdims=True)).astype(BF16)
        outs.append(_dot(p, vb[:, sl]))
    o = jnp.concatenate(outs, axis=1).astype(BF16)
    o_ref[...] = x1 + _dot(o, wxo_ref[...])


def _attn_prompt(x, ys, pm, wo1, g2, wq, mk, mv, wxo, make_rider):
    b, l, d = x.shape
    nm = mk.shape[1]
    tm = min(TOKEN_TILE, l)
    nl = l // tm
    row = pl.BlockSpec((None, tm, d), lambda i, j: (i, j, 0))
    kv = pl.BlockSpec((None, nm, d), lambda i, j: (i, 0, 0))
    ins = [x, ys, pm, wo1, g2, wq, mk, mv, wxo]
    specs = [row, row, row, _const_spec(wo1.shape), _const_spec(g2.shape), _const_spec(wq.shape), kv, kv,
             _const_spec(wxo.shape)]
    rider = make_rider(lambda i, j: i * nl + j, b * nl)
    body = functools.partial(_two_bodies, _attn_prompt_body, len(ins), 1, rider[0], len(rider[1]))
    return pl.pallas_call(
        body, grid=(b, nl),
        in_specs=specs + list(rider[2]),
        out_specs=[row] + list(rider[3]),
        out_shape=[jax.ShapeDtypeStruct((b, l, d), F32)] + list(rider[4]),
        compiler_params=_params("arbitrary", "arbitrary"),
        name="attn_prompt",
    )(*ins, *rider[1])


def _merge_q_body(x_ref, ys_ref, p_ref, wo1_ref, g2_ref, wq_ref, x1_ref, q_ref):
    x1 = x_ref[...] + _dot(ys_ref[...], wo1_ref[...]) + p_ref[...]
    x1_ref[...] = x1
    q_ref[...] = _dot(_rms(x1, g2_ref[...]).astype(BF16), wq_ref[...])


def _merge_q(x2d, ys, pm, wo1, g2, wq):
    t, d = x2d.shape
    return pl.pallas_call(
        _merge_q_body,
        out_shape=[jax.ShapeDtypeStruct((t, d), F32)] * 2,
        compiler_params=pltpu.CompilerParams(vmem_limit_bytes=VMEM_LIMIT),
        name="merge_q",
    )(x2d, ys, pm, wo1, g2, wq)


def _tile_view(kv):
    b, m, nh, hd = kv.shape
    nt = hd // LANES
    return kv.reshape(b, m, nh, nt, LANES).transpose(0, 1, 3, 2, 4).reshape(b, m, nt * nh, LANES)


def _tile_unview(o):
    b, rows, _ = o.shape
    nt = rows // X_HEADS
    return o.reshape(b, nt, X_HEADS, LANES).transpose(0, 2, 1, 3).reshape(b, rows * LANES)


def _attn_step_body(q_ref, k_ref, v_ref, o_ref):
    bt, nm, rows, _ = k_ref.shape
    hd = rows * LANES // X_HEADS
    for b in range(bt):
        r = jnp.sum(k_ref[b] * q_ref[b], axis=-1, keepdims=True)
        s = (r + pltpu.roll(r, X_HEADS, 1)) * (hd ** -0.5)
        e = jnp.exp(s - jnp.max(s, axis=0, keepdims=True))
        p = e / jnp.sum(e, axis=0, keepdims=True)
        o_ref[b] = jnp.sum(p * v_ref[b], axis=0)


def _attn_step_rider(q8, k8, v8, nsteps):
    nbt, rows, _ = q8.shape
    nm = k8.shape[1]
    assert rows == 2 * X_HEADS, "score assembly assumes two 128-lane tiles per head"
    assert nbt % nsteps == 0
    bt = nbt // nsteps
    qs = pl.BlockSpec((bt, rows, LANES), lambda i: (i, 0, 0))
    kv = pl.BlockSpec((bt, nm, rows, LANES), lambda i: (i, 0, 0, 0))
    return (_attn_step_body, (q8, k8, v8), (qs, kv, kv), (qs,),
            (jax.ShapeDtypeStruct((nbt, rows, LANES), F32),))


def _ffn_body(has_o, final, *refs):
    if has_o:
        x_ref, o_in_ref, wxo_ref, g3_ref, wg_ref, wu_ref, wd_ref, gf_ref, y_ref = refs
        x2 = x_ref[...] + _dot(o_in_ref[...].astype(BF16), wxo_ref[...])
    else:
        x_ref, g3_ref, wg_ref, wu_ref, wd_ref, gf_ref, y_ref = refs
        x2 = x_ref[...]
    hf = _rms(x2, g3_ref[...]).astype(BF16)
    ff = wg_ref.shape[1]
    x3 = x2
    for lo in range(0, ff, FF_CHUNK):
        hi = min(lo + FF_CHUNK, ff)
        act = (_silu(_dot(hf, wg_ref[:, lo:hi])) * _dot(hf, wu_ref[:, lo:hi])).astype(BF16)
        x3 = x3 + _dot(act, wd_ref[lo:hi, :])
    y_ref[...] = _rms(x3, gf_ref[...]) if final else x3


def _ffn(x2d, g3, wg, wu, wd, gf, final, o_in=None, wxo=None, rider=None):
    t, d = x2d.shape
    tm = min(TOKEN_TILE, t)
    row = pl.BlockSpec((tm, d), lambda i: (i, 0))
    has_o = o_in is not None
    ins = [x2d] + ([o_in, wxo] if has_o else []) + [g3, wg, wu, wd, gf]
    specs = [row] + ([row, _const_spec(wxo.shape)] if has_o else []) + [
        _const_spec(g3.shape), _const_spec(wg.shape), _const_spec(wu.shape), _const_spec(wd.shape),
        _const_spec(gf.shape)]
    body = functools.partial(_ffn_body, has_o, final)
    out_specs, out_shape = [row], [jax.ShapeDtypeStruct((t, d), F32)]
    if rider is not None:
        body = functools.partial(_two_bodies, body, len(ins), 1, rider[0], len(rider[1]))
        ins, specs = ins + list(rider[1]), specs + list(rider[2])
        out_specs, out_shape = out_specs + list(rider[3]), out_shape + list(rider[4])
    res = pl.pallas_call(
        body, grid=(t // tm,),
        in_specs=specs, out_specs=out_specs, out_shape=out_shape,
        compiler_params=_params("arbitrary"),
        name="ffn_o" if has_o else "ffn",
    )(*ins)
    return res[0] if rider is None else res


def _two_bodies(body_a, n_in_a, n_out_a, body_b, n_in_b, *refs):
    outs = refs[n_in_a + n_in_b:]
    body_a(*refs[:n_in_a], *outs[:n_out_a])
    body_b(*refs[n_in_a:n_in_a + n_in_b], *outs[n_out_a:])


def _merge_riders(ra, rb):
    body = functools.partial(_two_bodies, ra[0], len(ra[1]), len(ra[3]), rb[0], len(rb[1]))
    return (body,) + tuple(tuple(ra[k]) + tuple(rb[k]) for k in range(1, 5))


def _cast_body(*refs):
    n = len(refs) // 2
    for src, dst in zip(refs[:n], refs[n:]):
        dst[...] = src[...].astype(dst.dtype)


def _cast_rider(ws, step_of, nsteps):
    bf16_rows = 16
    ins, in_specs, out_specs, shapes = [], [], [], []
    for w in ws:
        w, lo, hi = w if isinstance(w, tuple) else (w, 0, w.shape[0])
        r, c = hi - lo, w.shape[1]
        k = next(k for k in (1, 2, 4, 8) if r % (nsteps // k) == 0 and (r // (nsteps // k)) % bf16_rows == 0)
        rp = r // (nsteps // k)
        assert lo % rp == 0
        ins.append(w)
        in_specs.append(pl.BlockSpec((rp, c), lambda *g, k=k, o=lo // rp: (o + step_of(*g) // k, 0)))
        out_specs.append(pl.BlockSpec((rp, c), lambda *g, k=k: (step_of(*g) // k, 0)))
        shapes.append(jax.ShapeDtypeStruct((r, c), BF16))
    return (_cast_body, tuple(ins), tuple(in_specs), tuple(out_specs), tuple(shapes))


def _layer_weights(i, ln1_g, w_in, conv_w, conv_b, dt_bias, a_log, d_ssd, gn_g, lam_re, lam_im, log_step,
                   b_re, b_im, c_re, c_im, d_s5, w_glu, b_glu, w_out, ln2_g, w_xq, w_xk, w_xv, w_xo, ln3_g,
                   w_gate, w_up, w_down):
    d = w_in.shape[1]
    n_heads = dt_bias.shape[1]
    dssd = n_heads * SSD_HEAD_DIM
    conv_dim = conv_w.shape[2]
    o1, o2, o3 = dssd, dssd + conv_dim, dssd + conv_dim + n_heads
    win = w_in[i]
    w = {}
    w["ln1"] = ln1_g[i].reshape(1, d)
    w["win"] = win.astype(BF16)
    w["wu_s5"] = win[:, o3:].astype(BF16)
    w["cw"] = conv_w[i]
    w["cb"] = conv_b[i].reshape(1, conv_dim)
    w["dtb"] = jnp.pad(dt_bias[i], (0, LANES - n_heads)).reshape(1, LANES)
    w["alog"] = jnp.pad(a_log[i], (0, LANES - n_heads)).reshape(1, LANES)
    w["dexp"] = jnp.repeat(d_ssd[i], SSD_HEAD_DIM).reshape(1, dssd)
    w["gn"] = gn_g[i].reshape(1, dssd)
    w["s5"] = (lam_re[i], lam_im[i], log_step[i], b_re[i], b_im[i], c_re[i], c_im[i])
    ds5 = d_s5.shape[1]
    w["dcol"] = d_s5[i].reshape(ds5, 1)
    w["wglut"] = w_glu[i].T.astype(BF16)
    w["bglu"] = b_glu[i].reshape(ds5, 1)
    w["cast_f32"] = ((w_out[i], 0, dssd), (w_out[i], dssd, w_out.shape[1]), w_xq[i], w_xo[i])
    w["ln2"] = ln2_g[i].reshape(1, d)
    w["wk"] = w_xk[i].astype(BF16)
    w["wv"] = w_xv[i].astype(BF16)
    w["ln3"] = ln3_g[i].reshape(1, d)
    w["ffn_f32"] = (w_gate[i], w_up[i], w_down[i])
    return w


def _rows_to_state(s):
    p = s.shape[-1] // 2
    st = s.transpose(1, 0, 2)
    return st[..., :p], st[..., p:]


def kernel(x_prompt, x_sample, mem_prompt, state_conv, state_ssm, state_s5_re, state_s5_im, cache_mem_k, cache_mem_v, ln1_g, w_in, conv_w, conv_b, dt_bias, a_log, d_ssd, gn_g, lam_re, lam_im, log_step, b_re, b_im, c_re, c_im, d_s5, w_glu, b_glu, w_out, ln2_g, w_xq, w_xk, w_xv, w_xo, ln3_g, w_gate, w_up, w_down, final_g):
    bp, seq, d = x_prompt.shape
    bs = x_sample.shape[0]
    depth = w_in.shape[0]
    nm = mem_prompt.shape[1]
    ng, ns = lam_re.shape[1], lam_re.shape[2]
    gf = final_g.reshape(1, d)
    hp = x_prompt
    hs = x_sample.reshape(bs, d)
    outs = {k: [] for k in ("conv_p", "ssm_p", "re_p", "im_p", "mk_p", "mv_p", "conv_s", "ssm_s", "re_s", "im_s")}
    yp = ys_out = None
    for i in range(depth):
        w = _layer_weights(i, ln1_g, w_in, conv_w, conv_b, dt_bias, a_log, d_ssd, gn_g, lam_re, lam_im, log_step,
                           b_re, b_im, c_re, c_im, d_s5, w_glu, b_glu, w_out, ln2_g, w_xq, w_xk, w_xv, w_xo, ln3_g,
                           w_gate, w_up, w_down)
        last = i == depth - 1
        attn_steps = bp * (seq // min(TOKEN_TILE, seq))
        y_ssd, conv_p, ssm_p, zs, xbcs, dts = _ssd_prompt(hp, w["ln1"], w["win"], w["cw"], w["cb"], w["dtb"],
                                                          w["alog"], w["dexp"], w["gn"], hs, bs // attn_steps)
        q = S5_Q
        prep_riders = lambda n: _merge_riders(
            _kv_proj_rider(mem_prompt.reshape(bp * nm, d), w["wk"], w["wv"], n),
            _cast_rider(w["cast_f32"], lambda i: i, n))
        (tt, wt, zt, a1, a2, a11, a21), (mk8, mv8, mk, mv, w["wo1"], w["wo2"], w["wq"], w["wxo"]) = _s5_prep(
            q, *w["s5"], prep_riders)
        mk = mk.reshape(bp, nm, d)
        mv = mv.reshape(bp, nm, d)
        s0 = jnp.zeros((ng, bp, 2 * ns), F32)
        pperm, sfin = _s5_mix(q, bp, hp, w["ln1"], w["wu_s5"],tt, wt, zt, a1, a2, s0, w["dcol"], w["wglut"],
                              w["bglu"], w["wo2"])
        pm = pperm.reshape(q, seq // q, bp, d).transpose(2, 1, 0, 3).reshape(bp, seq, d)
        re_p, im_p = _rows_to_state(sfin)
        riders = lambda step_of, n: _merge_riders(
            _ssd_step_rider(zs, xbcs, dts, state_conv[i], state_ssm[i], w["cw"], w["cb"], w["dtb"], w["alog"],
                            w["dexp"], w["gn"], step_of, n),
            _cast_rider(w["ffn_f32"], step_of, n))
        x2, ys_ssd, conv_s, ssm_s, w["wg"], w["wu"], w["wdn"] = _attn_prompt(
            hp, y_ssd, pm, w["wo1"], w["ln2"], w["wq"], mk, mv, w["wxo"], riders)
        ys_ssd = ys_ssd.reshape(bs, -1)
        tt1, wt1, zt1 = _s5_single_token_mats(q, tt, wt, zt)
        s0s = jnp.stack([state_s5_re[i].reshape(bs, ng * ns), state_s5_im[i].reshape(bs, ng * ns)])
        ps, sfin_s = _s5_mix(1, bs, hs.reshape(bs, 1, d), w["ln1"], w["wu_s5"],tt1, wt1, zt1, a11, a21, s0s,
                             w["dcol"], w["wglut"], w["bglu"], w["wo2"])
        re_s, im_s = sfin_s[0].reshape(bs, ng, ns), sfin_s[1].reshape(bs, ng, ns)
        x1s, qs = _merge_q(hs, ys_ssd, ps.reshape(bs, d), w["wo1"], w["ln2"], w["wq"])
        attn_rider = _attn_step_rider(_tile_view(qs.reshape(bs, 1, X_HEADS, d // X_HEADS))[:, 0],
                                      _tile_view(cache_mem_k[i]), _tile_view(cache_mem_v[i]),
                                      (bp * seq) // min(TOKEN_TILE, bp * seq))
        hp, o_s = _ffn(x2.reshape(bp * seq, d), w["ln3"], w["wg"], w["wu"], w["wdn"], gf, last, rider=attn_rider)
        hp = hp.reshape(bp, seq, d)
        hs = _ffn(x1s, w["ln3"], w["wg"], w["wu"], w["wdn"], gf, last, o_in=_tile_unview(o_s), wxo=w["wxo"])
        for k, v in (("conv_p", conv_p), ("ssm_p", ssm_p), ("re_p", re_p), ("im_p", im_p),
                     ("mk_p", _tile_unview_kv(mk8, bp, nm)), ("mv_p", _tile_unview_kv(mv8, bp, nm)),
                     ("conv_s", conv_s), ("ssm_s", ssm_s), ("re_s", re_s), ("im_s", im_s)):
            outs[k].append(v)
    st = lambda k: jnp.stack(outs[k])
    return (hp, hs.reshape(bs, 1, d), st("conv_p"), st("ssm_p"), st("re_p"), st("im_p"), st("mk_p"), st("mv_p"),
            st("conv_s"), st("ssm_s"), st("re_s"), st("im_s"))
```

```python
import functools
import math

import jax
import jax.numpy as jnp
from jax import lax
from jax.experimental import pallas as pl
from jax.experimental.pallas import tpu as pltpu

F32 = jnp.float32
BF16 = jnp.bfloat16
EPS = 1e-6

LANES = 128
VMEM_LIMIT = 60 * 1024 * 1024

SSD_HEAD_DIM = 64
SSD_STATE = 128
SSD_GROUPS = 2
SSD_CHUNK = 128
CONV_W = 4
S5_CH = 16
S5_STATE = 64
S5_Q = 16
S5_GROUP_BATCH = 32
S5_SLICE_TOKENS = 512
X_HEADS = 4
TOKEN_TILE = 512
PROJ_PIECE = 256
SSD_SEQS_PER_STEP = 4
FF_CHUNK = 1024


def _const_spec(shape):
    nd = len(shape)
    return pl.BlockSpec(shape, lambda *_: (0,) * nd, pipeline_mode=pl.Buffered(1))


def _params(*sem):
    return pltpu.CompilerParams(dimension_semantics=sem, vmem_limit_bytes=VMEM_LIMIT)


def _rms(x, g):
    return x * lax.rsqrt(jnp.mean(x * x, axis=-1, keepdims=True) + EPS) * g


def _sigmoid(x):
    return 0.5 * jnp.tanh(0.5 * x) + 0.5


def _silu(x):
    return x * _sigmoid(x)


def _gelu_tanh(x):
    return 0.5 * x * (1.0 + jnp.tanh(math.sqrt(2.0 / math.pi) * (x + 0.044715 * (x * x * x))))


def _softplus(x):
    return jnp.maximum(x, 0.0) + jnp.log1p(jnp.exp(-jnp.abs(x)))


def _dot(a, b):
    return jnp.dot(a, b, preferred_element_type=F32)


def _dot_nt(a, b):
    return lax.dot_general(a, b, (((1,), (1,)), ((), ())), preferred_element_type=F32)


def _dot_tn(a, b):
    return lax.dot_general(a, b, (((0,), (0,)), ((), ())), preferred_element_type=F32)


def _proj_windows(d_ssd, conv_dim):
    assert d_ssd % LANES == 0 and conv_dim % LANES == 0
    return (0, d_ssd), (d_ssd, d_ssd + conv_dim), (d_ssd + conv_dim, d_ssd + conv_dim + LANES)


def _expand_heads(v, n_heads):
    rows = v.shape[0]
    lane = lax.broadcasted_iota(jnp.int32, (rows, LANES), 1)
    pieces = [jnp.where(lane < SSD_HEAD_DIM, v[:, 2 * j:2 * j + 1], v[:, 2 * j + 1:2 * j + 2])
              for j in range(n_heads // 2)]
    return jnp.concatenate(pieces, axis=1)


def _ssd_chunk_math(z, x, dt_raw, last_chunk, cw_ref, cb_ref, dtb_ref, alog_ref, dexp_ref, gn_ref, e2_ref,
                    y_ref, conv_ref, ssm_ref, xpad, state, side_jobs=()):
    jobs = list(side_jobs)

    def side(n=1):
        for _ in range(n):
            if jobs:
                jobs.pop(0)()

    q = SSD_CHUNK
    d_ssd = z.shape[-1]
    n_heads = d_ssd // SSD_HEAD_DIM
    hpg = n_heads // SSD_GROUPS
    gw = hpg * SSD_HEAD_DIM
    nct = xpad.shape[0]
    xt = d_ssd // LANES

    for t in range(nct):
        xpad[t, 8:8 + q, :] = x[:, t * LANES:(t + 1) * LANES]
    cw = cw_ref[...]
    conv = (cw[3] * xpad[:, 8:8 + q, :] + cw[2] * xpad[:, 7:7 + q, :] + cw[1] * xpad[:, 6:6 + q, :]
            + cw[0] * xpad[:, 5:5 + q, :] + cb_ref[...])
    xpad[:, 0:8, :] = xpad[:, q:q + 8, :]
    xact = _silu(conv)
    xs = jnp.concatenate([xact[t] for t in range(xt)], axis=1)
    bm = xact[xt:xt + SSD_GROUPS].astype(BF16)
    cm = xact[xt + SSD_GROUPS:].astype(BF16)
    side()

    dt = _softplus(dt_raw + dtb_ref[...])
    a = -jnp.exp(alog_ref[...])
    row = lax.broadcasted_iota(jnp.int32, (q, q), 0)
    col = lax.broadcasted_iota(jnp.int32, (q, q), 1)
    causal = row >= col
    tri = jnp.where(causal, 1.0, 0.0).astype(F32)
    acum = jnp.dot(tri, dt * a, precision=lax.Precision.HIGHEST, preferred_element_type=F32)
    acum_t = acum.T
    last = acum[q - 1:q, :]

    def expand(v):
        hi = v.astype(BF16)
        lo = (v - hi.astype(F32)).astype(BF16)
        return _dot(jnp.concatenate([hi, lo], axis=1), e2_ref[...])

    dt_e = expand(dt)
    ea_e = expand(jnp.exp(acum))
    dend_e = expand(jnp.exp(last - acum))

    dtx = xs * dt_e
    dtx_b = dtx.astype(BF16)
    xdec_b = (dtx * dend_e).astype(BF16)
    side()

    lane = lax.broadcasted_iota(jnp.int32, (q, LANES), 1)
    zero_b = jnp.zeros((q, LANES), BF16)
    y_tiles = []
    for g in range(SSD_GROUPS):
        bg = bm[g]
        cg = cm[g]
        cb = _dot_nt(cg, bg)
        gs = slice(g * gw, (g + 1) * gw)
        st_g = state[:, gs]
        y_off = _dot(cg, st_g.astype(BF16)) * ea_e[:, gs]
        for pr in range(hpg // 2):
            tile = g * (hpg // 2) + pr
            ms = []
            for h in (2 * tile, 2 * tile + 1):
                seg = acum[:, h:h + 1] - acum_t[h:h + 1, :]
                lmat = jnp.exp(jnp.where(causal, seg, -jnp.inf))
                ms.append((cb * lmat).astype(BF16))
            dtile = dtx_b[:, tile * LANES:(tile + 1) * LANES]
            rhs = jnp.concatenate([jnp.where(lane < SSD_HEAD_DIM, dtile, zero_b),
                                   jnp.where(lane >= SSD_HEAD_DIM, dtile, zero_b)], axis=0)
            y_tiles.append(_dot(jnp.concatenate(ms, axis=1), rhs)
                           + y_off[:, pr * LANES:(pr + 1) * LANES])
            if pr % 2 == 1:
                side()
        new = _dot_tn(bg, xdec_b[:, gs])
        state[:, gs] = st_g * ea_e[q - 1:q, gs] + new
    y = jnp.concatenate(y_tiles, axis=1) + dexp_ref[...] * xs
    y = y * _silu(z)
    ms = jnp.mean(y * y, axis=-1, keepdims=True)
    side(len(jobs))
    y_ref[...] = (y * lax.rsqrt(ms + EPS) * gn_ref[...]).astype(y_ref.dtype)

    @pl.when(last_chunk)
    def _():
        conv_ref[...] = x[q - (CONV_W - 1):q, :]
        ssm_ref[...] = state[...].T.reshape(ssm_ref.shape)


def _ssd_prompt_body(wins, x_ref, g_ref, w_ref, cw_ref, cb_ref, dtb_ref, alog_ref, dexp_ref, gn_ref,
                     e2_ref, xs_ref, y_ref, conv_ref, ssm_ref, zs_ref, xbcs_ref, dts_ref,
                     za, xa, da, zb, xb, db, xpad, state):
    i = pl.program_id(0)
    j = pl.program_id(1)
    nc = pl.num_programs(1) - 1
    ns, q, d = x_ref.shape

    @pl.when(jnp.logical_and(i == 0, j == 0))
    def _():
        for r in (zb, xb, db):
            r[...] = jnp.zeros(r.shape, F32)
        hs = _rms(xs_ref[...], g_ref[...]).astype(BF16)
        for (lo, hi), o_ref in zip(wins, (zs_ref, xbcs_ref, dts_ref)):
            o_ref[...] = _dot(hs, w_ref[:, lo:hi]).reshape(o_ref.shape)

    @pl.when(j <= 1)
    def _():
        xpad[:, :, 0:8, :] = jnp.zeros(xpad.shape[:2] + (8, LANES), F32)
        state[...] = jnp.zeros(state.shape, F32)

    def step(wr, rd):
        xin = x_ref[...].reshape(ns * q, d)
        msx = jnp.mean(xin * xin, axis=-1, keepdims=True)
        hbox = []

        def proj_piece(dst, base, lo, hi):
            def run():
                if not hbox:
                    hbox.append((xin * lax.rsqrt(msx + EPS) * g_ref[...]).astype(BF16))
                dst[:, lo:hi] = _dot(hbox[0], w_ref[:, base + lo:base + hi])
            return run

        pieces = []
        for dst, (w_lo, w_hi) in ((wr[0], wins[0]), (wr[2], wins[2]), (wr[1], wins[1])):
            n = w_hi - w_lo
            pieces += [proj_piece(dst, w_lo, lo, min(lo + PROJ_PIECE, n)) for lo in range(0, n, PROJ_PIECE)]
        per = -(-len(pieces) // ns)
        for s in range(ns):
            rows = slice(s * q, (s + 1) * q)
            _ssd_chunk_math(rd[0][rows, :], rd[1][rows, :], rd[2][rows, :], j == nc, cw_ref, cb_ref, dtb_ref,
                            alog_ref, dexp_ref, gn_ref, e2_ref, y_ref.at[s], conv_ref.at[s], ssm_ref.at[s],
                            xpad.at[s], state.at[s], pieces[s * per:(s + 1) * per])

    @pl.when(j % 2 == 0)
    def _():
        step((za, xa, da), (zb, xb, db))

    @pl.when(j % 2 == 1)
    def _():
        step((zb, xb, db), (za, xa, da))


def _ssd_prompt(x, ln_g, w_in, cw, cb, dtb, alog, dexp, gn, xs, group):
    b, l, d = x.shape
    ts = xs.shape[0]
    conv_dim = cw.shape[1]
    d_ssd = conv_dim - 2 * SSD_GROUPS * SSD_STATE
    wins = _proj_windows(d_ssd, conv_dim)
    n_heads = d_ssd // SSD_HEAD_DIM
    q = SSD_CHUNK
    nc = l // q
    assert SSD_STATE == LANES and conv_dim == d_ssd + 2 * SSD_GROUPS * SSD_STATE
    nct = conv_dim // LANES
    cw4 = cw.reshape(CONV_W, nct, 1, LANES)
    cb3 = cb.reshape(nct, 1, LANES)
    e2 = (jnp.arange(2 * LANES)[:, None] % LANES == jnp.arange(d_ssd)[None, :] // SSD_HEAD_DIM).astype(BF16)
    ns = SSD_SEQS_PER_STEP if b % SSD_SEQS_PER_STEP == 0 else 1
    slot = lambda n: pltpu.VMEM((ns * q, n), F32)
    return pl.pallas_call(
        functools.partial(_ssd_prompt_body, wins),
        grid=(b // ns, nc + 1),
        in_specs=[pl.BlockSpec((ns, q, d), lambda i, j: (i, jnp.minimum(j, nc - 1), 0)),
                  _const_spec(ln_g.shape), _const_spec(w_in.shape),
                  _const_spec(cw4.shape), _const_spec(cb3.shape), _const_spec(dtb.shape), _const_spec(alog.shape),
                  _const_spec(dexp.shape), _const_spec(gn.shape), _const_spec(e2.shape), _const_spec(xs.shape)],
        out_specs=[pl.BlockSpec((ns, q, d_ssd), lambda i, j: (i, jnp.maximum(j - 1, 0), 0)),
                   pl.BlockSpec((ns, CONV_W - 1, conv_dim), lambda i, j: (i, 0, 0)),
                   pl.BlockSpec((ns, n_heads, SSD_HEAD_DIM, SSD_STATE), lambda i, j: (i, 0, 0, 0))]
        + [pl.BlockSpec((ts // group, group, hi - lo), lambda i, j: (0, 0, 0)) for lo, hi in wins],
        out_shape=[jax.ShapeDtypeStruct((b, l, d_ssd), BF16),
                   jax.ShapeDtypeStruct((b, CONV_W - 1, conv_dim), F32),
                   jax.ShapeDtypeStruct((b, n_heads, SSD_HEAD_DIM, SSD_STATE), F32)]
        + [jax.ShapeDtypeStruct((ts // group, group, hi - lo), F32) for lo, hi in wins],
        scratch_shapes=[slot(d_ssd), slot(conv_dim), slot(LANES), slot(d_ssd), slot(conv_dim), slot(LANES),
                        pltpu.VMEM((ns, nct, q + 8, LANES), F32), pltpu.VMEM((ns, SSD_STATE, d_ssd), F32)],
        compiler_params=_params("arbitrary", "arbitrary"),
        name="ssd_prompt",
    )(x, ln_g, w_in, cw4, cb3, dtb, alog, dexp, gn, e2, xs)


def _ssd_step_body(z_ref, xbc_ref, dt_ref, cs_ref, st_ref, cw_ref, cb_ref, dtb_ref, alog_ref, dexp_ref, gn_ref,
                   y_ref, conv_ref, ssm_ref):
    bt, d_ssd = z_ref.shape
    n_heads = d_ssd // SSD_HEAD_DIM
    hpg = n_heads // SSD_GROUPS
    x = xbc_ref[...]
    cw = cw_ref[...]
    cs = [cs_ref[:, k, :] for k in range(CONV_W - 1)]
    conv = cw[0:1] * cs[0] + cw[1:2] * cs[1] + cw[2:3] * cs[2] + cw[3:4] * x + cb_ref[...]
    conv_ref[:, 0, :] = cs[1]
    conv_ref[:, 1, :] = cs[2]
    conv_ref[:, 2, :] = x
    xact = _silu(conv)
    xs = xact[:, :d_ssd]
    bm = xact[:, d_ssd:d_ssd + SSD_GROUPS * SSD_STATE]
    cm = xact[:, d_ssd + SSD_GROUPS * SSD_STATE:]
    dt = _softplus(dt_ref[...] + dtb_ref[...])
    da = jnp.exp(dt * (-jnp.exp(alog_ref[...])))
    dtx = xs * _expand_heads(dt, n_heads)
    pad = jnp.zeros((LANES - bt, d_ssd), F32)
    to_cols = lambda v: jnp.concatenate([v, pad], axis=0).T
    dtx_t = to_cols(dtx)
    da_t = to_cols(_expand_heads(da, n_heads))
    lane = lax.broadcasted_iota(jnp.int32, (d_ssd, LANES), 1)
    gw = hpg * SSD_HEAD_DIM
    rows_of = lambda v, b: jnp.concatenate(
        [jnp.broadcast_to(v[b:b + 1, g * SSD_STATE:(g + 1) * SSD_STATE], (gw, SSD_STATE)) for g in range(SSD_GROUPS)],
        axis=0)
    y_t = jnp.zeros((d_ssd, LANES), F32)
    for b in range(bt):
        s0 = st_ref[b].reshape(d_ssd, SSD_STATE)
        s_new = s0 * da_t[:, b:b + 1] + dtx_t[:, b:b + 1] * rows_of(bm, b)
        ssm_ref[b] = s_new.reshape(n_heads, SSD_HEAD_DIM, SSD_STATE)
        ycol = jnp.sum(s_new * rows_of(cm, b), axis=1, keepdims=True)
        y_t = jnp.where(lane == b, ycol, y_t)
    y = y_t.T[:bt, :] + dexp_ref[...] * xs
    y = y * _silu(z_ref[...])
    y_ref[...] = _rms(y, gn_ref[...]).astype(y_ref.dtype)


def _ssd_step_rider(z, xbc, dt, conv_state, ssm_state, cw, cb, dtb, alog, dexp, gn, step_of, nsteps):
    _, bt, d_ssd = z.shape
    conv_dim = xbc.shape[-1]
    n_heads = d_ssd // SSD_HEAD_DIM
    assert z.shape[0] == nsteps and conv_state.shape[0] == nsteps * bt
    row = lambda n: pl.BlockSpec((None, bt, n), lambda *g: (step_of(*g), 0, 0))
    cs_spec = pl.BlockSpec((bt, CONV_W - 1, conv_dim), lambda *g: (step_of(*g), 0, 0))
    st_spec = pl.BlockSpec((bt, n_heads, SSD_HEAD_DIM, SSD_STATE), lambda *g: (step_of(*g), 0, 0, 0))
    cst = lambda a: pl.BlockSpec(a.shape, lambda *g: (0,) * a.ndim, pipeline_mode=pl.Buffered(1))
    ins = (z, xbc, dt, conv_state, ssm_state, cw, cb, dtb, alog, dexp, gn)
    specs = (row(d_ssd), row(conv_dim), row(LANES), cs_spec, st_spec, cst(cw), cst(cb), cst(dtb), cst(alog),
             cst(dexp), cst(gn))
    out_shapes = (jax.ShapeDtypeStruct((nsteps, bt, d_ssd), BF16),
                  jax.ShapeDtypeStruct(conv_state.shape, F32),
                  jax.ShapeDtypeStruct(ssm_state.shape, F32))
    return (_ssd_step_body, ins, specs, (row(d_ssd), cs_spec, st_spec), out_shapes)


def _split_bf16(x):
    hi = x.astype(BF16)
    lo = (x - hi.astype(F32)).astype(BF16)
    return hi, lo


def _s5_prep_body(q, ls_ref, lrp_ref, lip_ref, lr2_ref, li2_ref, br_ref, bi_ref, ccat_ref, ca_ref, cb_ref, rep_ref,
                  tt_ref, wt_ref, zt_ref, a1_ref, a2_ref, a1s_ref, a2s_ref):
    ng = ls_ref.shape[0]
    w = q * S5_CH
    step = jnp.exp(ls_ref[...])
    lr, li = lrp_ref[...], lip_ref[...]
    mag = jnp.exp(lr * step)
    ang = li * step
    lbr = mag * jnp.cos(ang)
    lbi = mag * jnp.sin(ang)
    den = lr * lr + li * li
    kr = ((lbr - 1.0) * lr + lbi * li) / den
    ki = (lbi * lr - (lbr - 1.0) * li) / den
    br, bi = br_ref[...], bi_ref[...]
    bbr = kr * br - ki * bi
    bbi = kr * bi + ki * br
    if q > 1:
        np_ = br.shape[1]
        rep = lambda v: jnp.dot(v.reshape(ng * np_, S5_CH), rep_ref[...], precision=lax.Precision.HIGHEST,
                                preferred_element_type=F32).reshape(ng, np_, w)
        bbr, bbi = rep(bbr), rep(bbi)
    if q == 1:
        ball = jnp.concatenate([bbr, bbi], axis=1)
    else:
        d = lax.broadcasted_iota(jnp.int32, (1, 1, w), 2) // S5_CH
        fr, fi = lbr, lbi
        pr = pi = None
        for b in range((q - 1).bit_length()):
            bit = ((d >> b) & 1) == 1
            sr, si = jnp.where(bit, fr, 1.0), jnp.where(bit, fi, 0.0)
            pr, pi = (sr, si) if pr is None else (pr * sr - pi * si, pr * si + pi * sr)
            fr, fi = fr * fr - fi * fi, 2.0 * (fr * fi)
        ball = jnp.concatenate([pr * bbr - pi * bbi, pr * bbi + pi * bbr], axis=1)
    wt_ref[...] = ball.astype(BF16)

    lane3 = lax.broadcasted_iota(jnp.int32, (1, 1, 2 * S5_STATE), 2)
    first = lane3 < S5_STATE
    csign = jnp.where(first, ccat_ref[...], -ccat_ref[...])
    ch, cl = _split_bf16(csign)
    bh, bl = _split_bf16(ball)
    bdot = lambda x, y: lax.dot_general(x, y, (((2,), (1,)), ((0,), (0,))), preferred_element_type=F32)
    kall = bdot(ch, bh) + bdot(ch, bl) + bdot(cl, bh)
    k2 = kall.reshape(ng * S5_CH, w)
    lane2 = lax.broadcasted_iota(jnp.int32, (ng * S5_CH, w), 1)
    for t in range(q):
        sh = (q - 1 - t) * S5_CH
        r = pltpu.roll(k2, sh, 1) if sh else k2
        r = jnp.where(lane2 >= sh, r, 0.0)
        tt_ref[:, t * S5_CH:(t + 1) * S5_CH, :] = r.reshape(ng, S5_CH, w).astype(BF16)

    lr2, li2 = lr2_ref[...], li2_ref[...]
    t1 = (lax.broadcasted_iota(jnp.int32, (1, q, 1), 1) + 1).astype(F32)
    zm = jnp.exp(t1 * (lr2 * step))
    za = t1 * (li2 * step)
    zr = zm * jnp.cos(za)
    zi = zm * jnp.sin(za)
    ca, cb = ca_ref[...], cb_ref[...]
    for t in range(q):
        prt = zr[:, t:t + 1, :]
        pit = zi[:, t:t + 1, :]
        zt = jnp.where(first, ca * prt - cb * pit, -(ca * pit) - cb * prt)
        zt_ref[:, t * S5_CH:(t + 1) * S5_CH, :] = zt.astype(BF16)
    qf = float(q)
    mq = jnp.exp(qf * (lr2 * step))
    aq = qf * (li2 * step)
    ar = mq * jnp.cos(aq)
    ai = mq * jnp.sin(aq)
    a1_ref[...] = ar
    a2_ref[...] = jnp.where(first, -ai, ai)
    a1s_ref[...] = zr[:, 0:1, :]
    a2s_ref[...] = jnp.where(first, -zi[:, 0:1, :], zi[:, 0:1, :])


def _s5_prep(q, lam_re, lam_im, log_step, b_re, b_im, c_re, c_im, make_rider):
    ng, p = lam_re.shape
    w = q * S5_CH
    ls = log_step.reshape(ng, 1, 1)
    lrp = lam_re.reshape(ng, p, 1)
    lip = lam_im.reshape(ng, p, 1)
    lr2 = jnp.concatenate([lam_re, lam_re], axis=-1).reshape(ng, 1, 2 * p)
    li2 = jnp.concatenate([lam_im, lam_im], axis=-1).reshape(ng, 1, 2 * p)
    rep = (jnp.arange(S5_CH)[:, None] == jnp.arange(w)[None, :] % S5_CH).astype(F32)
    ccat = jnp.concatenate([c_re, c_im], axis=-1)
    ca = jnp.concatenate([c_re, c_re], axis=-1)
    cb = jnp.concatenate([c_im, c_im], axis=-1)
    gb = ng if q == 1 else 16
    blk = lambda a, b: pl.BlockSpec((gb, a, b), lambda i: (i, 0, 0))
    ins = (ls, lrp, lip, lr2, li2, b_re, b_im, ccat, ca, cb, rep)
    in_specs = [blk(1, 1), blk(p, 1), blk(p, 1), blk(1, 2 * p), blk(1, 2 * p), blk(p, S5_CH), blk(p, S5_CH),
                blk(S5_CH, 2 * p), blk(S5_CH, 2 * p), blk(S5_CH, 2 * p), _const_spec(rep.shape)]
    out_specs = [blk(w, w), blk(2 * p, w), blk(w, 2 * p)] + [blk(1, 2 * p)] * 4
    out_shape = [jax.ShapeDtypeStruct((ng, w, w), BF16), jax.ShapeDtypeStruct((ng, 2 * p, w), BF16),
                 jax.ShapeDtypeStruct((ng, w, 2 * p), BF16)] + [jax.ShapeDtypeStruct((ng, 1, 2 * p), F32)] * 4
    body = functools.partial(_s5_prep_body, q)
    rider = make_rider(ng // gb)
    body = functools.partial(_two_bodies, body, len(ins), len(out_specs), rider[0], len(rider[1]))
    res = pl.pallas_call(
        body,
        grid=(ng // gb,),
        in_specs=in_specs + list(rider[2]),
        out_specs=out_specs + list(rider[3]),
        out_shape=out_shape + list(rider[4]),
        compiler_params=_params("arbitrary"),
        name=f"s5_prep_q{q}",
    )(*ins, *rider[1])
    return res[:len(out_specs)], res[len(out_specs):]


def _s5_single_token_mats(q, tt, wt, zt):
    c = S5_CH
    return tt[:, :c, (q - 1) * c:], wt[:, :, :c], zt[:, :c, :]


def _s5_mix_body(q, nb, nkb, npi, npt, perm, x_ref, g_ref, wu_ref, tt_ref, wt_ref, zt_ref, a1_ref, a2_ref, s0_ref,
                 dcol_ref, wglut_ref, bglu_ref, wo_ref, p_ref, sfin_ref, ut, yt, carry, wut):
    ng = tt_ref.shape[0]
    nk = nkb // nb
    qs = q // npt
    ch = qs * nkb
    tile = pl.program_id(0)
    ph = pl.program_id(1)

    @pl.when(jnp.logical_and(tile == 0, ph == 0))
    def _():
        if perm:
            carry[...] = s0_ref[...]
        else:
            lane_s = lax.broadcasted_iota(jnp.int32, (nb, 2 * S5_STATE), 1)
            for pr in range(ng // 2):
                cols = slice(pr * 2 * S5_STATE, (pr + 1) * 2 * S5_STATE)
                re_t, im_t = s0_ref[0, :, cols], s0_ref[1, :, cols]
                carry[2 * pr] = jnp.where(lane_s < S5_STATE, re_t, pltpu.roll(im_t, S5_STATE, 1))
                carry[2 * pr + 1] = jnp.where(lane_s < S5_STATE, pltpu.roll(re_t, S5_STATE, 1), im_t)
        cw_ = 256
        for c in range(0, wu_ref.shape[1], cw_):
            wut[c:c + cw_, :] = wu_ref[:, c:c + cw_].T

    def rows_of(r):
        if not perm:
            return x_ref[r]
        nseq, nblk, ndt, nr, _ = x_ref.shape
        flat = x_ref.reshape(nseq * nblk * ndt * nr, LANES)

        return jnp.concatenate(
            [jnp.concatenate([flat[pl.ds((k * ndt + dt) * nr + r, nseq, stride=nblk * ndt * nr), :]
                              for dt in range(ndt)], axis=1) for k in range(nblk)], axis=0)

    def project(j):
        step = 2 if qs % 2 == 0 else 1
        r0 = (j * qs) % (q // npi)
        for r in range(0, qs, step):
            xin = jnp.concatenate([rows_of(r0 + r + s) for s in range(step)], axis=0)
            u = _dot_nt(wut[...], _rms(xin, g_ref[...]).astype(BF16))
            sl = slice((j * qs + r) * nkb, (j * qs + r + step) * nkb)
            ut[:, sl] = u.astype(BF16)
            yt[:, sl] = dcol_ref[...] * u

    def finish(j):
        gt = _gelu_tanh(yt[:, j * ch:(j + 1) * ch])
        gate = _dot(wglut_ref[...], gt.astype(BF16)) + bglu_ref[...]
        y5 = (gt * _sigmoid(gate)).astype(BF16)
        p_ref[...] = _dot_tn(y5, wo_ref[...]).reshape(p_ref.shape)

    spp = npt // npi
    for hh in range(npi):
        @pl.when(ph == hh)
        def _():
            for j in range(hh * spp, (hh + 1) * spp):
                project(j)

    @pl.when(ph == npi - 1)
    def _():
        gu = S5_GROUP_BATCH
        bdot = lambda a, b: lax.dot_general(a, b, (((2,), (1,)), ((0,), (0,))), preferred_element_type=F32)
        bdot_nt = lambda a, b: lax.dot_general(a, b, (((2,), (2,)), ((0,), (0,))), preferred_element_type=F32)

        def groups(i, _):
            g0 = pl.multiple_of(i * gu, gu)
            r0 = pl.multiple_of(i * (gu * S5_CH), gu * S5_CH)
            gsl = pl.ds(g0, gu)
            rows = ut[pl.ds(r0, gu * S5_CH), :].reshape(gu, S5_CH, q * nkb)
            ugt = jnp.concatenate([rows[:, :, (q - 1 - j) * nkb:(q - j) * nkb] for j in range(q)], axis=1)
            y = bdot(tt_ref[gsl], ugt)
            vt = bdot(wt_ref[gsl], ugt)
            v = jnp.swapaxes(vt, 1, 2)
            v_sw = jnp.swapaxes(jnp.concatenate([vt[:, S5_STATE:], vt[:, :S5_STATE]], axis=1), 1, 2)
            a1 = a1_ref[gsl]
            a2 = a2_ref[gsl]
            s = carry[gsl]
            s_sw = pltpu.roll(s.reshape(gu * nb, 2 * S5_STATE), S5_STATE, 1).reshape(s.shape)
            prev = []
            for k in range(nk):
                prev.append(s)
                s, s_sw = (a1 * s + a2 * s_sw + v[:, k * nb:(k + 1) * nb, :],
                           a1 * s_sw - a2 * s + v_sw[:, k * nb:(k + 1) * nb, :])
            carry[gsl] = s
            sprev = jnp.concatenate(prev, axis=1).astype(BF16)
            y = y + bdot_nt(zt_ref[gsl], sprev)
            for t in range(q):
                yt[pl.ds(r0, gu * S5_CH), t * nkb:(t + 1) * nkb] += (
                    y[:, t * S5_CH:(t + 1) * S5_CH, :].reshape(gu * S5_CH, nkb))
            return 0

        lax.fori_loop(0, ng // gu, groups, 0)
        if perm:
            sfin_ref[...] = carry[...]
        else:
            lane_s = lax.broadcasted_iota(jnp.int32, (nb, 2 * S5_STATE), 1)
            for pr in range(ng // 2):
                cols = slice(pr * 2 * S5_STATE, (pr + 1) * 2 * S5_STATE)
                c0, c1 = carry[2 * pr], carry[2 * pr + 1]
                sfin_ref[0, :, cols] = jnp.where(lane_s < S5_STATE, c0, pltpu.roll(c1, S5_STATE, 1))
                sfin_ref[1, :, cols] = jnp.where(lane_s < S5_STATE, pltpu.roll(c0, S5_STATE, 1), c1)

    for j in range(npt):
        @pl.when(ph == npi + j)
        def _():
            finish(j)


def _s5_mix(q, nb, x, ln_g, wu, tt, wt, zt, a1, a2, s0, dcol, wglut, bglu, wo):
    nseq, l, d = x.shape
    nblk = l // q
    nlt = nblk * nseq
    nkb = min(LANES, nlt)
    ntile = nlt // nkb
    tok = q * nkb
    npt = max(1, tok // S5_SLICE_TOKENS)
    dm = wo.shape[1]
    perm = q > 1
    if perm:
        half = 8
        npi = q // half
        assert nb == nseq and d % LANES == 0 and q % half == 0
        xv = x.reshape(nseq, nblk, npi, half, d // LANES, LANES).transpose(0, 1, 2, 4, 3, 5)
        assert npt % npi == 0
        x_spec = pl.BlockSpec((nseq, nkb // nseq, None, d // LANES, half, LANES),
                              lambda i, j: (0, i, jnp.minimum(j, npi - 1), 0, 0, 0))
    else:
        npi = 1
        xv = x.reshape(1, nseq, d)
        x_spec = pl.BlockSpec((1, nkb, d), lambda i, j: (0, i, 0))
    body = functools.partial(_s5_mix_body, q, nb, nkb, npi, npt, perm)
    p_spec = pl.BlockSpec((q // npt, nkb, dm), lambda i, j: (jnp.maximum(j - npi, 0), i, 0))
    return pl.pallas_call(
        body,
        grid=(ntile, npi + npt),
        in_specs=[x_spec, _const_spec(ln_g.shape), _const_spec(wu.shape), _const_spec(tt.shape),
                  _const_spec(wt.shape), _const_spec(zt.shape), _const_spec(a1.shape), _const_spec(a2.shape),
                  _const_spec(s0.shape), _const_spec(dcol.shape), _const_spec(wglut.shape),
                  _const_spec(bglu.shape), _const_spec(wo.shape)],
        out_specs=[p_spec, pl.BlockSpec(s0.shape, lambda i, j: (0, 0, 0))],
        out_shape=[jax.ShapeDtypeStruct((q, nlt, dm), F32), jax.ShapeDtypeStruct(s0.shape, F32)],
        scratch_shapes=[pltpu.VMEM((d, tok), BF16), pltpu.VMEM((d, tok), F32),
                        pltpu.VMEM((tt.shape[0], nb, 2 * S5_STATE), F32),
                        pltpu.VMEM((wu.shape[1], wu.shape[0]), BF16)],
        compiler_params=_params("arbitrary", "arbitrary"),
        name=f"s5_mix_q{q}",
    )(xv, ln_g, wu, tt, wt, zt, a1, a2, s0, dcol, wglut, bglu, wo)


def _kv_body(m_ref, wk_ref, wv_ref, k_ref, v_ref, kb_ref, vb_ref):
    tm, rows, _ = k_ref.shape
    nt = rows // X_HEADS
    m = m_ref[...].astype(BF16)
    for w_ref, o_ref, ob_ref in ((wk_ref, k_ref, kb_ref), (wv_ref, v_ref, vb_ref)):
        r = _dot(m, w_ref[...])
        ob_ref[...] = r.astype(BF16)
        flat = o_ref.reshape(tm * rows, LANES)
        for h in range(X_HEADS):
            for dt in range(nt):
                c = (h * nt + dt) * LANES
                flat[pl.ds(dt * X_HEADS + h, tm, stride=rows), :] = r[:, c:c + LANES]


def _kv_proj_rider(mem2d, wk, wv, nsteps):
    t, d = mem2d.shape
    assert t % nsteps == 0
    tm = t // nsteps
    rows = d // LANES
    row = pl.BlockSpec((tm, d), lambda i: (i, 0))
    tile = pl.BlockSpec((tm, rows, LANES), lambda i: (i, 0, 0))
    return (_kv_body, (mem2d, wk, wv), (row, _const_spec(wk.shape), _const_spec(wv.shape)),
            (tile, tile, row, row),
            (jax.ShapeDtypeStruct((t, rows, LANES), F32),) * 2 + (jax.ShapeDtypeStruct((t, d), BF16),) * 2)


def _tile_unview_kv(kv8, b, m):
    nt = kv8.shape[1] // X_HEADS
    return kv8.reshape(b, m, nt, X_HEADS, LANES).transpose(0, 1, 3, 2, 4).reshape(b, m, X_HEADS, nt * LANES)


def _attn_prompt_body(x_ref, ys_ref, p_ref, wo1_ref, g2_ref, wq_ref, k_ref, v_ref, wxo_ref, o_ref):
    d = x_ref.shape[-1]
    hd = d // X_HEADS
    x1 = x_ref[...] + _dot(ys_ref[...], wo1_ref[...]) + p_ref[...]
    hq = _rms(x1, g2_ref[...]).astype(BF16)
    qv = _dot(hq, wq_ref[...]).astype(BF16)
    kb = k_ref[...]
    vb = v_ref[...]
    outs = []
    for h in range(X_HEADS):
        sl = slice(h * hd, (h + 1) * hd)
        s = _dot_nt(qv[:, sl], kb[:, sl]) * (hd ** -0.5)
        e = jnp.exp(s - jnp.max(s, axis=-1, keepdims=True))
        p = (e / jnp.sum(e, axis=-1, keepdims=True)).astype(BF16)
        outs.append(_dot(p, vb[:, sl]))
    o = jnp.concatenate(outs, axis=1).astype(BF16)
    o_ref[...] = x1 + _dot(o, wxo_ref[...])


def _attn_prompt(x, ys, pm, wo1, g2, wq, mk, mv, wxo, make_rider):
    b, l, d = x.shape
    nm = mk.shape[1]
    tm = min(TOKEN_TILE, l)
    nl = l // tm
    row = pl.BlockSpec((None, tm, d), lambda i, j: (i, j, 0))
    kv = pl.BlockSpec((None, nm, d), lambda i, j: (i, 0, 0))
    ins = [x, ys, pm, wo1, g2, wq, mk, mv, wxo]
    specs = [row, row, row, _const_spec(wo1.shape), _const_spec(g2.shape), _const_spec(wq.shape), kv, kv,
             _const_spec(wxo.shape)]
    rider = make_rider(lambda i, j: i * nl + j, b * nl)
    body = functools.partial(_two_bodies, _attn_prompt_body, len(ins), 1, rider[0], len(rider[1]))
    return pl.pallas_call(
        body, grid=(b, nl),
        in_specs=specs + list(rider[2]),
        out_specs=[row] + list(rider[3]),
        out_shape=[jax.ShapeDtypeStruct((b, l, d), F32)] + list(rider[4]),
        compiler_params=_params("arbitrary", "arbitrary"),
        name="attn_prompt",
    )(*ins, *rider[1])


def _merge_q_body(x_ref, ys_ref, p_ref, wo1_ref, g2_ref, wq_ref, x1_ref, q_ref):
    x1 = x_ref[...] + _dot(ys_ref[...], wo1_ref[...]) + p_ref[...]
    x1_ref[...] = x1
    q_ref[...] = _dot(_rms(x1, g2_ref[...]).astype(BF16), wq_ref[...])


def _merge_q(x2d, ys, pm, wo1, g2, wq):
    t, d = x2d.shape
    return pl.pallas_call(
        _merge_q_body,
        out_shape=[jax.ShapeDtypeStruct((t, d), F32)] * 2,
        compiler_params=pltpu.CompilerParams(vmem_limit_bytes=VMEM_LIMIT),
        name="merge_q",
    )(x2d, ys, pm, wo1, g2, wq)


def _tile_view(kv):
    b, m, nh, hd = kv.shape
    nt = hd // LANES
    return kv.reshape(b, m, nh, nt, LANES).transpose(0, 1, 3, 2, 4).reshape(b, m, nt * nh, LANES)


def _tile_unview(o):
    b, rows, _ = o.shape
    nt = rows // X_HEADS
    return o.reshape(b, nt, X_HEADS, LANES).transpose(0, 2, 1, 3).reshape(b, rows * LANES)


def _attn_step_body(q_ref, k_ref, v_ref, o_ref):
    bt, nm, rows, _ = k_ref.shape
    hd = rows * LANES // X_HEADS
    for b in range(bt):
        r = jnp.sum(k_ref[b] * q_ref[b], axis=-1, keepdims=True)
        s = (r + pltpu.roll(r, X_HEADS, 1)) * (hd ** -0.5)
        e = jnp.exp(s - jnp.max(s, axis=0, keepdims=True))
        p = e / jnp.sum(e, axis=0, keepdims=True)
        o_ref[b] = jnp.sum(p * v_ref[b], axis=0)


def _attn_step_rider(q8, k8, v8, nsteps):
    nbt, rows, _ = q8.shape
    nm = k8.shape[1]
    assert rows == 2 * X_HEADS, "score assembly assumes two 128-lane tiles per head"
    assert nbt % nsteps == 0
    bt = nbt // nsteps
    qs = pl.BlockSpec((bt, rows, LANES), lambda i: (i, 0, 0))
    kv = pl.BlockSpec((bt, nm, rows, LANES), lambda i: (i, 0, 0, 0))
    return (_attn_step_body, (q8, k8, v8), (qs, kv, kv), (qs,),
            (jax.ShapeDtypeStruct((nbt, rows, LANES), F32),))


def _ffn_body(has_o, final, *refs):
    if has_o:
        x_ref, o_in_ref, wxo_ref, g3_ref, wg_ref, wu_ref, wd_ref, gf_ref, y_ref = refs
        x2 = x_ref[...] + _dot(o_in_ref[...].astype(BF16), wxo_ref[...])
    else:
        x_ref, g3_ref, wg_ref, wu_ref, wd_ref, gf_ref, y_ref = refs
        x2 = x_ref[...]
    hf = _rms(x2, g3_ref[...]).astype(BF16)
    ff = wg_ref.shape[1]
    x3 = x2
    for lo in range(0, ff, FF_CHUNK):
        hi = min(lo + FF_CHUNK, ff)
        act = (_silu(_dot(hf, wg_ref[:, lo:hi])) * _dot(hf, wu_ref[:, lo:hi])).astype(BF16)
        x3 = x3 + _dot(act, wd_ref[lo:hi, :])
    y_ref[...] = _rms(x3, gf_ref[...]) if final else x3


def _ffn(x2d, g3, wg, wu, wd, gf, final, o_in=None, wxo=None, rider=None):
    t, d = x2d.shape
    tm = min(TOKEN_TILE, t)
    row = pl.BlockSpec((tm, d), lambda i: (i, 0))
    has_o = o_in is not None
    ins = [x2d] + ([o_in, wxo] if has_o else []) + [g3, wg, wu, wd, gf]
    specs = [row] + ([row, _const_spec(wxo.shape)] if has_o else []) + [
        _const_spec(g3.shape), _const_spec(wg.shape), _const_spec(wu.shape), _const_spec(wd.shape),
        _const_spec(gf.shape)]
    body = functools.partial(_ffn_body, has_o, final)
    out_specs, out_shape = [row], [jax.ShapeDtypeStruct((t, d), F32)]
    if rider is not None:
        body = functools.partial(_two_bodies, body, len(ins), 1, rider[0], len(rider[1]))
        ins, specs = ins + list(rider[1]), specs + list(rider[2])
        out_specs, out_shape = out_specs + list(rider[3]), out_shape + list(rider[4])
    res = pl.pallas_call(
        body, grid=(t // tm,),
        in_specs=specs, out_specs=out_specs, out_shape=out_shape,
        compiler_params=_params("arbitrary"),
        name="ffn_o" if has_o else "ffn",
    )(*ins)
    return res[0] if rider is None else res


def _two_bodies(body_a, n_in_a, n_out_a, body_b, n_in_b, *refs):
    outs = refs[n_in_a + n_in_b:]
    body_a(*refs[:n_in_a], *outs[:n_out_a])
    body_b(*refs[n_in_a:n_in_a + n_in_b], *outs[n_out_a:])


def _merge_riders(ra, rb):
    body = functools.partial(_two_bodies, ra[0], len(ra[1]), len(ra[3]), rb[0], len(rb[1]))
    return (body,) + tuple(tuple(ra[k]) + tuple(rb[k]) for k in range(1, 5))


def _cast_body(*refs):
    n = len(refs) // 2
    for src, dst in zip(refs[:n], refs[n:]):
        dst[...] = src[...].astype(dst.dtype)


def _cast_rider(ws, step_of, nsteps):
    bf16_rows = 16
    ins, in_specs, out_specs, shapes = [], [], [], []
    for w in ws:
        w, lo, hi = w if isinstance(w, tuple) else (w, 0, w.shape[0])
        r, c = hi - lo, w.shape[1]
        k = next(k for k in (1, 2, 4, 8) if r % (nsteps // k) == 0 and (r // (nsteps // k)) % bf16_rows == 0)
        rp = r // (nsteps // k)
        assert lo % rp == 0
        ins.append(w)
        in_specs.append(pl.BlockSpec((rp, c), lambda *g, k=k, o=lo // rp: (o + step_of(*g) // k, 0)))
        out_specs.append(pl.BlockSpec((rp, c), lambda *g, k=k: (step_of(*g) // k, 0)))
        shapes.append(jax.ShapeDtypeStruct((r, c), BF16))
    return (_cast_body, tuple(ins), tuple(in_specs), tuple(out_specs), tuple(shapes))


def _layer_weights(i, ln1_g, w_in, conv_w, conv_b, dt_bias, a_log, d_ssd, gn_g, lam_re, lam_im, log_step,
                   b_re, b_im, c_re, c_im, d_s5, w_glu, b_glu, w_out, ln2_g, w_xq, w_xk, w_xv, w_xo, ln3_g,
                   w_gate, w_up, w_down):
    d = w_in.shape[1]
    n_heads = dt_bias.shape[1]
    dssd = n_heads * SSD_HEAD_DIM
    conv_dim = conv_w.shape[2]
    o1, o2, o3 = dssd, dssd + conv_dim, dssd + conv_dim + n_heads
    win = w_in[i]
    w = {}
    w["ln1"] = ln1_g[i].reshape(1, d)
    w["win"] = win.astype(BF16)
    w["wu_s5"] = win[:, o3:].astype(BF16)
    w["cw"] = conv_w[i]
    w["cb"] = conv_b[i].reshape(1, conv_dim)
    w["dtb"] = jnp.pad(dt_bias[i], (0, LANES - n_heads)).reshape(1, LANES)
    w["alog"] = jnp.pad(a_log[i], (0, LANES - n_heads)).reshape(1, LANES)
    w["dexp"] = jnp.repeat(d_ssd[i], SSD_HEAD_DIM).reshape(1, dssd)
    w["gn"] = gn_g[i].reshape(1, dssd)
    w["s5"] = (lam_re[i], lam_im[i], log_step[i], b_re[i], b_im[i], c_re[i], c_im[i])
    ds5 = d_s5.shape[1]
    w["dcol"] = d_s5[i].reshape(ds5, 1)
    w["wglut"] = w_glu[i].T.astype(BF16)
    w["bglu"] = b_glu[i].reshape(ds5, 1)
    w["cast_f32"] = ((w_out[i], 0, dssd), (w_out[i], dssd, w_out.shape[1]), w_xq[i], w_xo[i])
    w["ln2"] = ln2_g[i].reshape(1, d)
    w["wk"] = w_xk[i].astype(BF16)
    w["wv"] = w_xv[i].astype(BF16)
    w["ln3"] = ln3_g[i].reshape(1, d)
    w["ffn_f32"] = (w_gate[i], w_up[i], w_down[i])
    return w


def _rows_to_state(s):
    p = s.shape[-1] // 2
    st = s.transpose(1, 0, 2)
    return st[..., :p], st[..., p:]


def kernel(x_prompt, x_sample, mem_prompt, state_conv, state_ssm, state_s5_re, state_s5_im, cache_mem_k, cache_mem_v, ln1_g, w_in, conv_w, conv_b, dt_bias, a_log, d_ssd, gn_g, lam_re, lam_im, log_step, b_re, b_im, c_re, c_im, d_s5, w_glu, b_glu, w_out, ln2_g, w_xq, w_xk, w_xv, w_xo, ln3_g, w_gate, w_up, w_down, final_g):
    bp, seq, d = x_prompt.shape
    bs = x_sample.shape[0]
    depth = w_in.shape[0]
    nm = mem_prompt.shape[1]
    ng, ns = lam_re.shape[1], lam_re.shape[2]
    gf = final_g.reshape(1, d)
    hp = x_prompt
    hs = x_sample.reshape(bs, d)
    outs = {k: [] for k in ("conv_p", "ssm_p", "re_p", "im_p", "mk_p", "mv_p", "conv_s", "ssm_s", "re_s", "im_s")}
    yp = ys_out = None
    for i in range(depth):
        w = _layer_weights(i, ln1_g, w_in, conv_w, conv_b, dt_bias, a_log, d_ssd, gn_g, lam_re, lam_im, log_step,
                           b_re, b_im, c_re, c_im, d_s5, w_glu, b_glu, w_out, ln2_g, w_xq, w_xk, w_xv, w_xo, ln3_g,
                           w_gate, w_up, w_down)
        last = i == depth - 1
        attn_steps = bp * (seq // min(TOKEN_TILE, seq))
        y_ssd, conv_p, ssm_p, zs, xbcs, dts = _ssd_prompt(hp, w["ln1"], w["win"], w["cw"], w["cb"], w["dtb"],
                                                          w["alog"], w["dexp"], w["gn"], hs, bs // attn_steps)
        q = S5_Q
        prep_riders = lambda n: _merge_riders(
            _kv_proj_rider(mem_prompt.reshape(bp * nm, d), w["wk"], w["wv"], n),
            _cast_rider(w["cast_f32"], lambda i: i, n))
        (tt, wt, zt, a1, a2, a11, a21), (mk8, mv8, mk, mv, w["wo1"], w["wo2"], w["wq"], w["wxo"]) = _s5_prep(
            q, *w["s5"], prep_riders)
        mk = mk.reshape(bp, nm, d)
        mv = mv.reshape(bp, nm, d)
        s0 = jnp.zeros((ng, bp, 2 * ns), F32)
        pperm, sfin = _s5_mix(q, bp, hp, w["ln1"], w["wu_s5"],tt, wt, zt, a1, a2, s0, w["dcol"], w["wglut"],
                              w["bglu"], w["wo2"])
        pm = pperm.reshape(q, seq // q, bp, d).transpose(2, 1, 0, 3).reshape(bp, seq, d)
        re_p, im_p = _rows_to_state(sfin)
        riders = lambda step_of, n: _merge_riders(
            _ssd_step_rider(zs, xbcs, dts, state_conv[i], state_ssm[i], w["cw"], w["cb"], w["dtb"], w["alog"],
                            w["dexp"], w["gn"], step_of, n),
            _cast_rider(w["ffn_f32"], step_of, n))
        x2, ys_ssd, conv_s, ssm_s, w["wg"], w["wu"], w["wdn"] = _attn_prompt(
            hp, y_ssd, pm, w["wo1"], w["ln2"], w["wq"], mk, mv, w["wxo"], riders)
        ys_ssd = ys_ssd.reshape(bs, -1)
        tt1, wt1, zt1 = _s5_single_token_mats(q, tt, wt, zt)
        s0s = jnp.stack([state_s5_re[i].reshape(bs, ng * ns), state_s5_im[i].reshape(bs, ng * ns)])
        ps, sfin_s = _s5_mix(1, bs, hs.reshape(bs, 1, d), w["ln1"], w["wu_s5"],tt1, wt1, zt1, a11, a21, s0s,
                             w["dcol"], w["wglut"], w["bglu"], w["wo2"])
        re_s, im_s = sfin_s[0].reshape(bs, ng, ns), sfin_s[1].reshape(bs, ng, ns)
        x1s, qs = _merge_q(hs, ys_ssd, ps.reshape(bs, d), w["wo1"], w["ln2"], w["wq"])
        attn_rider = _attn_step_rider(_tile_view(qs.reshape(bs, 1, X_HEADS, d // X_HEADS))[:, 0],
                                      _tile_view(cache_mem_k[i]), _tile_view(cache_mem_v[i]),
                                      (bp * seq) // min(TOKEN_TILE, bp * seq))
        hp, o_s = _ffn(x2.reshape(bp * seq, d), w["ln3"], w["wg"], w["wu"], w["wdn"], gf, last, rider=attn_rider)
        hp = hp.reshape(bp, seq, d)
        hs = _ffn(x1s, w["ln3"], w["wg"], w["wu"], w["wdn"], gf, last, o_in=_tile_unview(o_s), wxo=w["wxo"])
        for k, v in (("conv_p", conv_p), ("ssm_p", ssm_p), ("re_p", re_p), ("im_p", im_p),
                     ("mk_p", _tile_unview_kv(mk8, bp, nm)), ("mv_p", _tile_unview_kv(mv8, bp, nm)),
                     ("conv_s", conv_s), ("ssm_s", ssm_s), ("re_s", re_s), ("im_s", im_s)):
            outs[k].append(v)
    st = lambda k: jnp.stack(outs[k])
    return (hp, hs.reshape(bs, 1, d), st("conv_p"), st("ssm_p"), st("re_p"), st("im_p"), st("mk_p"), st("mv_p"),
            st("conv_s"), st("ssm_s"), st("re_s"), st("im_s"))
```

```python
import functools
import math

import jax
import jax.numpy as jnp
from jax import lax
from jax.experimental import pallas as pl
from jax.experimental.pallas import tpu as pltpu

F32 = jnp.float32
BF16 = jnp.bfloat16
EPS = 1e-6

LANES = 128
VMEM_LIMIT = 60 * 1024 * 1024

SSD_HEAD_DIM = 64
SSD_STATE = 128
SSD_GROUPS = 2
SSD_CHUNK = 128
CONV_W = 4
S5_CH = 16
S5_STATE = 64
S5_Q = 16
S5_GROUP_BATCH = 32
S5_SLICE_TOKENS = 512
X_HEADS = 4
TOKEN_TILE = 512
PROJ_PIECE = 256
SSD_SEQS_PER_STEP = 4
FF_CHUNK = 1024


def _const_spec(shape):
    nd = len(shape)
    return pl.BlockSpec(shape, lambda *_: (0,) * nd, pipeline_mode=pl.Buffered(1))


def _params(*sem):
    return pltpu.CompilerParams(dimension_semantics=sem, vmem_limit_bytes=VMEM_LIMIT)


def _rms(x, g):
    return x * lax.rsqrt(jnp.mean(x * x, axis=-1, keepdims=True) + EPS) * g


def _sigmoid(x):
    return 0.5 * jnp.tanh(0.5 * x) + 0.5


def _silu(x):
    h = 0.5 * x
    return h + h * jnp.tanh(h)


def _gelu_tanh(x):
    c = math.sqrt(2.0 / math.pi)
    h = 0.5 * x
    return h + h * jnp.tanh(x * (c + (0.044715 * c) * (x * x)))


def _softplus(x):
    return jnp.maximum(x, 0.0) + jnp.log1p(jnp.exp(-jnp.abs(x)))


def _dot(a, b):
    return jnp.dot(a, b, preferred_element_type=F32)


def _dot_nt(a, b):
    return lax.dot_general(a, b, (((1,), (1,)), ((), ())), preferred_element_type=F32)


def _dot_tn(a, b):
    return lax.dot_general(a, b, (((0,), (0,)), ((), ())), preferred_element_type=F32)


def _proj_windows(d_ssd, conv_dim):
    assert d_ssd % LANES == 0 and conv_dim % LANES == 0
    return (0, d_ssd), (d_ssd, d_ssd + conv_dim), (d_ssd + conv_dim, d_ssd + conv_dim + LANES)


def _expand_heads(v, n_heads):
    rows = v.shape[0]
    lane = lax.broadcasted_iota(jnp.int32, (rows, LANES), 1)
    pieces = [jnp.where(lane < SSD_HEAD_DIM, v[:, 2 * j:2 * j + 1], v[:, 2 * j + 1:2 * j + 2])
              for j in range(n_heads // 2)]
    return jnp.concatenate(pieces, axis=1)


def _ssd_chunk_math(z, x, dt_raw, last_chunk, cw_ref, cb_ref, dtb_ref, alog_ref, dexp_ref, gn_ref, e2_ref,
                    y_ref, conv_ref, ssm_ref, xpad, state, side_jobs=()):
    jobs = list(side_jobs)

    def side(n=1):
        for _ in range(n):
            if jobs:
                jobs.pop(0)()

    q = SSD_CHUNK
    d_ssd = z.shape[-1]
    n_heads = d_ssd // SSD_HEAD_DIM
    hpg = n_heads // SSD_GROUPS
    gw = hpg * SSD_HEAD_DIM
    nct = xpad.shape[0]
    xt = d_ssd // LANES

    for t in range(nct):
        xpad[t, 8:8 + q, :] = x[:, t * LANES:(t + 1) * LANES]
    cw = cw_ref[...]
    conv = (cw[3] * xpad[:, 8:8 + q, :] + cw[2] * xpad[:, 7:7 + q, :] + cw[1] * xpad[:, 6:6 + q, :]
            + cw[0] * xpad[:, 5:5 + q, :] + cb_ref[...])
    xpad[:, 0:8, :] = xpad[:, q:q + 8, :]
    xact = _silu(conv)
    xs = jnp.concatenate([xact[t] for t in range(xt)], axis=1)
    bm = xact[xt:xt + SSD_GROUPS].astype(BF16)
    cm = xact[xt + SSD_GROUPS:].astype(BF16)
    side()

    dt = _softplus(dt_raw + dtb_ref[...])
    a = -jnp.exp(alog_ref[...])
    row = lax.broadcasted_iota(jnp.int32, (q, q), 0)
    col = lax.broadcasted_iota(jnp.int32, (q, q), 1)
    causal = row >= col
    tri = jnp.where(causal, 1.0, 0.0).astype(F32)
    acum = jnp.dot(tri, dt * a, precision=lax.Precision.HIGHEST, preferred_element_type=F32)
    acum_t = acum.T
    last = acum[q - 1:q, :]

    def expand(v):
        hi = v.astype(BF16)
        lo = (v - hi.astype(F32)).astype(BF16)
        return _dot(jnp.concatenate([hi, lo], axis=1), e2_ref[...])

    dt_e = expand(dt)
    ea_e = expand(jnp.exp(acum))
    dend_e = expand(jnp.exp(last - acum))

    dtx = xs * dt_e
    dtx_b = dtx.astype(BF16)
    xdec_b = (dtx * dend_e).astype(BF16)
    side()

    lane = lax.broadcasted_iota(jnp.int32, (q, LANES), 1)
    zero_b = jnp.zeros((q, LANES), BF16)
    y_tiles = []
    for g in range(SSD_GROUPS):
        bg = bm[g]
        cg = cm[g]
        cb = _dot_nt(cg, bg)
        gs = slice(g * gw, (g + 1) * gw)
        st_g = state[:, gs]
        y_off = _dot(cg, st_g.astype(BF16)) * ea_e[:, gs]
        for pr in range(hpg // 2):
            tile = g * (hpg // 2) + pr
            ms = []
            for h in (2 * tile, 2 * tile + 1):
                seg = acum[:, h:h + 1] - acum_t[h:h + 1, :]
                lmat = jnp.exp(jnp.where(causal, seg, -jnp.inf))
                ms.append((cb * lmat).astype(BF16))
            dtile = dtx_b[:, tile * LANES:(tile + 1) * LANES]
            rhs = jnp.concatenate([jnp.where(lane < SSD_HEAD_DIM, dtile, zero_b),
                                   jnp.where(lane >= SSD_HEAD_DIM, dtile, zero_b)], axis=0)
            y_tiles.append(_dot(jnp.concatenate(ms, axis=1), rhs)
                           + y_off[:, pr * LANES:(pr + 1) * LANES])
            if pr % 2 == 1:
                side()
        new = _dot_tn(bg, xdec_b[:, gs])
        state[:, gs] = st_g * ea_e[q - 1:q, gs] + new
    y = jnp.concatenate(y_tiles, axis=1) + dexp_ref[...] * xs
    y = y * _silu(z)
    ms = jnp.mean(y * y, axis=-1, keepdims=True)
    side(len(jobs))
    y_ref[...] = (y * lax.rsqrt(ms + EPS) * gn_ref[...]).astype(y_ref.dtype)

    @pl.when(last_chunk)
    def _():
        conv_ref[...] = x[q - (CONV_W - 1):q, :]
        ssm_ref[...] = state[...].T.reshape(ssm_ref.shape)


def _ssd_prompt_body(wins, x_ref, g_ref, w_ref, cw_ref, cb_ref, dtb_ref, alog_ref, dexp_ref, gn_ref,
                     e2_ref, xs_ref, y_ref, conv_ref, ssm_ref, zs_ref, xbcs_ref, dts_ref,
                     za, xa, da, zb, xb, db, xpad, state):
    i = pl.program_id(0)
    j = pl.program_id(1)
    nc = pl.num_programs(1) - 1
    ns, q, d = x_ref.shape

    @pl.when(jnp.logical_and(i == 0, j == 0))
    def _():
        for r in (zb, xb, db):
            r[...] = jnp.zeros(r.shape, F32)
        hs = _rms(xs_ref[...], g_ref[...]).astype(BF16)
        for (lo, hi), o_ref in zip(wins, (zs_ref, xbcs_ref, dts_ref)):
            o_ref[...] = _dot(hs, w_ref[:, lo:hi]).reshape(o_ref.shape)

    @pl.when(j <= 1)
    def _():
        xpad[:, :, 0:8, :] = jnp.zeros(xpad.shape[:2] + (8, LANES), F32)
        state[...] = jnp.zeros(state.shape, F32)

    def step(wr, rd):
        xin = x_ref[...].reshape(ns * q, d)
        msx = jnp.mean(xin * xin, axis=-1, keepdims=True)
        hbox = []

        def proj_piece(dst, base, lo, hi):
            def run():
                if not hbox:
                    hbox.append((xin * lax.rsqrt(msx + EPS) * g_ref[...]).astype(BF16))
                dst[:, lo:hi] = _dot(hbox[0], w_ref[:, base + lo:base + hi])
            return run

        pieces = []
        for dst, (w_lo, w_hi) in ((wr[0], wins[0]), (wr[2], wins[2]), (wr[1], wins[1])):
            n = w_hi - w_lo
            pieces += [proj_piece(dst, w_lo, lo, min(lo + PROJ_PIECE, n)) for lo in range(0, n, PROJ_PIECE)]
        per = -(-len(pieces) // ns)
        for s in range(ns):
            rows = slice(s * q, (s + 1) * q)
            _ssd_chunk_math(rd[0][rows, :], rd[1][rows, :], rd[2][rows, :], j == nc, cw_ref, cb_ref, dtb_ref,
                            alog_ref, dexp_ref, gn_ref, e2_ref, y_ref.at[s], conv_ref.at[s], ssm_ref.at[s],
                            xpad.at[s], state.at[s], pieces[s * per:(s + 1) * per])

    @pl.when(j % 2 == 0)
    def _():
        step((za, xa, da), (zb, xb, db))

    @pl.when(j % 2 == 1)
    def _():
        step((zb, xb, db), (za, xa, da))


def _ssd_prompt(x, ln_g, w_in, cw, cb, dtb, alog, dexp, gn, xs, group):
    b, l, d = x.shape
    ts = xs.shape[0]
    conv_dim = cw.shape[1]
    d_ssd = conv_dim - 2 * SSD_GROUPS * SSD_STATE
    wins = _proj_windows(d_ssd, conv_dim)
    n_heads = d_ssd // SSD_HEAD_DIM
    q = SSD_CHUNK
    nc = l // q
    assert SSD_STATE == LANES and conv_dim == d_ssd + 2 * SSD_GROUPS * SSD_STATE
    nct = conv_dim // LANES
    cw4 = cw.reshape(CONV_W, nct, 1, LANES)
    cb3 = cb.reshape(nct, 1, LANES)
    e2 = (jnp.arange(2 * LANES)[:, None] % LANES == jnp.arange(d_ssd)[None, :] // SSD_HEAD_DIM).astype(BF16)
    ns = SSD_SEQS_PER_STEP if b % SSD_SEQS_PER_STEP == 0 else 1
    slot = lambda n: pltpu.VMEM((ns * q, n), F32)
    return pl.pallas_call(
        functools.partial(_ssd_prompt_body, wins),
        grid=(b // ns, nc + 1),
        in_specs=[pl.BlockSpec((ns, q, d), lambda i, j: (i, jnp.minimum(j, nc - 1), 0)),
                  _const_spec(ln_g.shape), _const_spec(w_in.shape),
                  _const_spec(cw4.shape), _const_spec(cb3.shape), _const_spec(dtb.shape), _const_spec(alog.shape),
                  _const_spec(dexp.shape), _const_spec(gn.shape), _const_spec(e2.shape), _const_spec(xs.shape)],
        out_specs=[pl.BlockSpec((ns, q, d_ssd), lambda i, j: (i, jnp.maximum(j - 1, 0), 0)),
                   pl.BlockSpec((ns, CONV_W - 1, conv_dim), lambda i, j: (i, 0, 0)),
                   pl.BlockSpec((ns, n_heads, SSD_HEAD_DIM, SSD_STATE), lambda i, j: (i, 0, 0, 0))]
        + [pl.BlockSpec((ts // group, group, hi - lo), lambda i, j: (0, 0, 0)) for lo, hi in wins],
        out_shape=[jax.ShapeDtypeStruct((b, l, d_ssd), BF16),
                   jax.ShapeDtypeStruct((b, CONV_W - 1, conv_dim), F32),
                   jax.ShapeDtypeStruct((b, n_heads, SSD_HEAD_DIM, SSD_STATE), F32)]
        + [jax.ShapeDtypeStruct((ts // group, group, hi - lo), F32) for lo, hi in wins],
        scratch_shapes=[slot(d_ssd), slot(conv_dim), slot(LANES), slot(d_ssd), slot(conv_dim), slot(LANES),
                        pltpu.VMEM((ns, nct, q + 8, LANES), F32), pltpu.VMEM((ns, SSD_STATE, d_ssd), F32)],
        compiler_params=_params("arbitrary", "arbitrary"),
        name="ssd_prompt",
    )(x, ln_g, w_in, cw4, cb3, dtb, alog, dexp, gn, e2, xs)


def _ssd_step_body(z_ref, xbc_ref, dt_ref, cs_ref, st_ref, cw_ref, cb_ref, dtb_ref, alog_ref, dexp_ref, gn_ref,
                   y_ref, conv_ref, ssm_ref):
    bt, d_ssd = z_ref.shape
    n_heads = d_ssd // SSD_HEAD_DIM
    hpg = n_heads // SSD_GROUPS
    x = xbc_ref[...]
    cw = cw_ref[...]
    cs = [cs_ref[:, k, :] for k in range(CONV_W - 1)]
    conv = cw[0:1] * cs[0] + cw[1:2] * cs[1] + cw[2:3] * cs[2] + cw[3:4] * x + cb_ref[...]
    conv_ref[:, 0, :] = cs[1]
    conv_ref[:, 1, :] = cs[2]
    conv_ref[:, 2, :] = x
    xact = _silu(conv)
    xs = xact[:, :d_ssd]
    bm = xact[:, d_ssd:d_ssd + SSD_GROUPS * SSD_STATE]
    cm = xact[:, d_ssd + SSD_GROUPS * SSD_STATE:]
    dt = _softplus(dt_ref[...] + dtb_ref[...])
    da = jnp.exp(dt * (-jnp.exp(alog_ref[...])))
    dtx = xs * _expand_heads(dt, n_heads)
    pad = jnp.zeros((LANES - bt, d_ssd), F32)
    to_cols = lambda v: jnp.concatenate([v, pad], axis=0).T
    dtx_t = to_cols(dtx)
    da_t = to_cols(_expand_heads(da, n_heads))
    lane = lax.broadcasted_iota(jnp.int32, (d_ssd, LANES), 1)
    gw = hpg * SSD_HEAD_DIM
    rows_of = lambda v, b: jnp.concatenate(
        [jnp.broadcast_to(v[b:b + 1, g * SSD_STATE:(g + 1) * SSD_STATE], (gw, SSD_STATE)) for g in range(SSD_GROUPS)],
        axis=0)
    y_t = jnp.zeros((d_ssd, LANES), F32)
    for b in range(bt):
        s0 = st_ref[b].reshape(d_ssd, SSD_STATE)
        s_new = s0 * da_t[:, b:b + 1] + dtx_t[:, b:b + 1] * rows_of(bm, b)
        ssm_ref[b] = s_new.reshape(n_heads, SSD_HEAD_DIM, SSD_STATE)
        ycol = jnp.sum(s_new * rows_of(cm, b), axis=1, keepdims=True)
        y_t = jnp.where(lane == b, ycol, y_t)
    y = y_t.T[:bt, :] + dexp_ref[...] * xs
    y = y * _silu(z_ref[...])
    y_ref[...] = _rms(y, gn_ref[...]).astype(y_ref.dtype)


def _ssd_step_rider(z, xbc, dt, conv_state, ssm_state, cw, cb, dtb, alog, dexp, gn, step_of, nsteps):
    _, bt, d_ssd = z.shape
    conv_dim = xbc.shape[-1]
    n_heads = d_ssd // SSD_HEAD_DIM
    assert z.shape[0] == nsteps and conv_state.shape[0] == nsteps * bt
    row = lambda n: pl.BlockSpec((None, bt, n), lambda *g: (step_of(*g), 0, 0))
    cs_spec = pl.BlockSpec((bt, CONV_W - 1, conv_dim), lambda *g: (step_of(*g), 0, 0))
    st_spec = pl.BlockSpec((bt, n_heads, SSD_HEAD_DIM, SSD_STATE), lambda *g: (step_of(*g), 0, 0, 0))
    cst = lambda a: pl.BlockSpec(a.shape, lambda *g: (0,) * a.ndim, pipeline_mode=pl.Buffered(1))
    ins = (z, xbc, dt, conv_state, ssm_state, cw, cb, dtb, alog, dexp, gn)
    specs = (row(d_ssd), row(conv_dim), row(LANES), cs_spec, st_spec, cst(cw), cst(cb), cst(dtb), cst(alog),
             cst(dexp), cst(gn))
    out_shapes = (jax.ShapeDtypeStruct((nsteps, bt, d_ssd), BF16),
                  jax.ShapeDtypeStruct(conv_state.shape, F32),
                  jax.ShapeDtypeStruct(ssm_state.shape, F32))
    return (_ssd_step_body, ins, specs, (row(d_ssd), cs_spec, st_spec), out_shapes)


def _split_bf16(x):
    hi = x.astype(BF16)
    lo = (x - hi.astype(F32)).astype(BF16)
    return hi, lo


def _s5_prep_body(q, ls_ref, lrp_ref, lip_ref, lr2_ref, li2_ref, br_ref, bi_ref, ccat_ref, ca_ref, cb_ref, rep_ref,
                  tt_ref, wt_ref, zt_ref, a1_ref, a2_ref, a1s_ref, a2s_ref):
    ng = ls_ref.shape[0]
    w = q * S5_CH
    step = jnp.exp(ls_ref[...])
    lr, li = lrp_ref[...], lip_ref[...]
    mag = jnp.exp(lr * step)
    ang = li * step
    lbr = mag * jnp.cos(ang)
    lbi = mag * jnp.sin(ang)
    den = lr * lr + li * li
    kr = ((lbr - 1.0) * lr + lbi * li) / den
    ki = (lbi * lr - (lbr - 1.0) * li) / den
    br, bi = br_ref[...], bi_ref[...]
    bbr = kr * br - ki * bi
    bbi = kr * bi + ki * br
    if q > 1:
        np_ = br.shape[1]
        rep = lambda v: jnp.dot(v.reshape(ng * np_, S5_CH), rep_ref[...], precision=lax.Precision.HIGHEST,
                                preferred_element_type=F32).reshape(ng, np_, w)
        bbr, bbi = rep(bbr), rep(bbi)
    if q == 1:
        ball = jnp.concatenate([bbr, bbi], axis=1)
    else:
        d = lax.broadcasted_iota(jnp.int32, (1, 1, w), 2) // S5_CH
        fr, fi = lbr, lbi
        pr = pi = None
        for b in range((q - 1).bit_length()):
            bit = ((d >> b) & 1) == 1
            sr, si = jnp.where(bit, fr, 1.0), jnp.where(bit, fi, 0.0)
            pr, pi = (sr, si) if pr is None else (pr * sr - pi * si, pr * si + pi * sr)
            fr, fi = fr * fr - fi * fi, 2.0 * (fr * fi)
        ball = jnp.concatenate([pr * bbr - pi * bbi, pr * bbi + pi * bbr], axis=1)
    wt_ref[...] = ball.astype(BF16)

    lane3 = lax.broadcasted_iota(jnp.int32, (1, 1, 2 * S5_STATE), 2)
    first = lane3 < S5_STATE
    csign = jnp.where(first, ccat_ref[...], -ccat_ref[...])
    ch, cl = _split_bf16(csign)
    bh, bl = _split_bf16(ball)
    bdot = lambda x, y: lax.dot_general(x, y, (((2,), (1,)), ((0,), (0,))), preferred_element_type=F32)
    kall = bdot(ch, bh) + bdot(ch, bl) + bdot(cl, bh)
    k2 = kall.reshape(ng * S5_CH, w)
    lane2 = lax.broadcasted_iota(jnp.int32, (ng * S5_CH, w), 1)
    for t in range(q):
        sh = (q - 1 - t) * S5_CH
        r = pltpu.roll(k2, sh, 1) if sh else k2
        r = jnp.where(lane2 >= sh, r, 0.0)
        tt_ref[:, t * S5_CH:(t + 1) * S5_CH, :] = r.reshape(ng, S5_CH, w).astype(BF16)

    lr2, li2 = lr2_ref[...], li2_ref[...]
    t1 = (lax.broadcasted_iota(jnp.int32, (1, q, 1), 1) + 1).astype(F32)
    zm = jnp.exp(t1 * (lr2 * step))
    za = t1 * (li2 * step)
    zr = zm * jnp.cos(za)
    zi = zm * jnp.sin(za)
    ca, cb = ca_ref[...], cb_ref[...]
    for t in range(q):
        prt = zr[:, t:t + 1, :]
        pit = zi[:, t:t + 1, :]
        zt = jnp.where(first, ca * prt - cb * pit, -(ca * pit) - cb * prt)
        zt_ref[:, t * S5_CH:(t + 1) * S5_CH, :] = zt.astype(BF16)
    qf = float(q)
    mq = jnp.exp(qf * (lr2 * step))
    aq = qf * (li2 * step)
    ar = mq * jnp.cos(aq)
    ai = mq * jnp.sin(aq)
    a1_ref[...] = ar
    a2_ref[...] = jnp.where(first, -ai, ai)
    a1s_ref[...] = zr[:, 0:1, :]
    a2s_ref[...] = jnp.where(first, -zi[:, 0:1, :], zi[:, 0:1, :])


def _s5_prep(q, lam_re, lam_im, log_step, b_re, b_im, c_re, c_im, make_rider):
    ng, p = lam_re.shape
    w = q * S5_CH
    ls = log_step.reshape(ng, 1, 1)
    lrp = lam_re.reshape(ng, p, 1)
    lip = lam_im.reshape(ng, p, 1)
    lr2 = jnp.concatenate([lam_re, lam_re], axis=-1).reshape(ng, 1, 2 * p)
    li2 = jnp.concatenate([lam_im, lam_im], axis=-1).reshape(ng, 1, 2 * p)
    rep = (jnp.arange(S5_CH)[:, None] == jnp.arange(w)[None, :] % S5_CH).astype(F32)
    ccat = jnp.concatenate([c_re, c_im], axis=-1)
    ca = jnp.concatenate([c_re, c_re], axis=-1)
    cb = jnp.concatenate([c_im, c_im], axis=-1)
    gb = ng if q == 1 else 16
    blk = lambda a, b: pl.BlockSpec((gb, a, b), lambda i: (i, 0, 0))
    ins = (ls, lrp, lip, lr2, li2, b_re, b_im, ccat, ca, cb, rep)
    in_specs = [blk(1, 1), blk(p, 1), blk(p, 1), blk(1, 2 * p), blk(1, 2 * p), blk(p, S5_CH), blk(p, S5_CH),
                blk(S5_CH, 2 * p), blk(S5_CH, 2 * p), blk(S5_CH, 2 * p), _const_spec(rep.shape)]
    out_specs = [blk(w, w), blk(2 * p, w), blk(w, 2 * p)] + [blk(1, 2 * p)] * 4
    out_shape = [jax.ShapeDtypeStruct((ng, w, w), BF16), jax.ShapeDtypeStruct((ng, 2 * p, w), BF16),
                 jax.ShapeDtypeStruct((ng, w, 2 * p), BF16)] + [jax.ShapeDtypeStruct((ng, 1, 2 * p), F32)] * 4
    body = functools.partial(_s5_prep_body, q)
    rider = make_rider(ng // gb)
    body = functools.partial(_two_bodies, body, len(ins), len(out_specs), rider[0], len(rider[1]))
    res = pl.pallas_call(
        body,
        grid=(ng // gb,),
        in_specs=in_specs + list(rider[2]),
        out_specs=out_specs + list(rider[3]),
        out_shape=out_shape + list(rider[4]),
        compiler_params=_params("arbitrary"),
        name=f"s5_prep_q{q}",
    )(*ins, *rider[1])
    return res[:len(out_specs)], res[len(out_specs):]


def _s5_single_token_mats(q, tt, wt, zt):
    c = S5_CH
    return tt[:, :c, (q - 1) * c:], wt[:, :, :c], zt[:, :c, :]


def _s5_mix_body(q, nb, nkb, npi, npt, perm, x_ref, g_ref, wu_ref, tt_ref, wt_ref, zt_ref, a1_ref, a2_ref, s0_ref,
                 dcol_ref, wglut_ref, bglu_ref, wo_ref, p_ref, sfin_ref, ut, yt, carry, wut):
    ng = tt_ref.shape[0]
    nk = nkb // nb
    qs = q // npt
    ch = qs * nkb
    tile = pl.program_id(0)
    ph = pl.program_id(1)

    @pl.when(jnp.logical_and(tile == 0, ph == 0))
    def _():
        if perm:
            carry[...] = s0_ref[...]
        else:
            lane_s = lax.broadcasted_iota(jnp.int32, (nb, 2 * S5_STATE), 1)
            for pr in range(ng // 2):
                cols = slice(pr * 2 * S5_STATE, (pr + 1) * 2 * S5_STATE)
                re_t, im_t = s0_ref[0, :, cols], s0_ref[1, :, cols]
                carry[2 * pr] = jnp.where(lane_s < S5_STATE, re_t, pltpu.roll(im_t, S5_STATE, 1))
                carry[2 * pr + 1] = jnp.where(lane_s < S5_STATE, pltpu.roll(re_t, S5_STATE, 1), im_t)
        cw_ = 256
        for c in range(0, wu_ref.shape[1], cw_):
            wut[c:c + cw_, :] = wu_ref[:, c:c + cw_].T

    def rows_of(r):
        if not perm:
            return x_ref[r]
        nseq, nblk, ndt, nr, _ = x_ref.shape
        flat = x_ref.reshape(nseq * nblk * ndt * nr, LANES)

        return jnp.concatenate(
            [jnp.concatenate([flat[pl.ds((k * ndt + dt) * nr + r, nseq, stride=nblk * ndt * nr), :]
                              for dt in range(ndt)], axis=1) for k in range(nblk)], axis=0)

    def project(j):
        step = 2 if qs % 2 == 0 else 1
        r0 = (j * qs) % (q // npi)
        for r in range(0, qs, step):
            xin = jnp.concatenate([rows_of(r0 + r + s) for s in range(step)], axis=0)
            u = _dot_nt(wut[...], _rms(xin, g_ref[...]).astype(BF16))
            sl = slice((j * qs + r) * nkb, (j * qs + r + step) * nkb)
            ut[:, sl] = u.astype(BF16)
            yt[:, sl] = dcol_ref[...] * u

    def finish(j):
        gt = _gelu_tanh(yt[:, j * ch:(j + 1) * ch])
        gate = _dot(wglut_ref[...], gt.astype(BF16)) + bglu_ref[...]
        y5 = (gt * _sigmoid(gate)).astype(BF16)
        p_ref[...] = _dot_tn(y5, wo_ref[...]).reshape(p_ref.shape)

    spp = npt // npi
    for hh in range(npi):
        @pl.when(ph == hh)
        def _():
            for j in range(hh * spp, (hh + 1) * spp):
                project(j)

    @pl.when(ph == npi - 1)
    def _():
        gu = S5_GROUP_BATCH
        bdot = lambda a, b: lax.dot_general(a, b, (((2,), (1,)), ((0,), (0,))), preferred_element_type=F32)
        bdot_nt = lambda a, b: lax.dot_general(a, b, (((2,), (2,)), ((0,), (0,))), preferred_element_type=F32)

        def groups(i, _):
            g0 = pl.multiple_of(i * gu, gu)
            r0 = pl.multiple_of(i * (gu * S5_CH), gu * S5_CH)
            gsl = pl.ds(g0, gu)
            rows = ut[pl.ds(r0, gu * S5_CH), :].reshape(gu, S5_CH, q * nkb)
            ugt = jnp.concatenate([rows[:, :, (q - 1 - j) * nkb:(q - j) * nkb] for j in range(q)], axis=1)
            y = bdot(tt_ref[gsl], ugt)
            vt = bdot(wt_ref[gsl], ugt)
            v = jnp.swapaxes(vt, 1, 2)
            v_sw = jnp.swapaxes(jnp.concatenate([vt[:, S5_STATE:], vt[:, :S5_STATE]], axis=1), 1, 2)
            a1 = a1_ref[gsl]
            a2 = a2_ref[gsl]
            s = carry[gsl]
            s_sw = pltpu.roll(s.reshape(gu * nb, 2 * S5_STATE), S5_STATE, 1).reshape(s.shape)
            prev = []
            for k in range(nk):
                prev.append(s)
                s, s_sw = (a1 * s + a2 * s_sw + v[:, k * nb:(k + 1) * nb, :],
                           a1 * s_sw - a2 * s + v_sw[:, k * nb:(k + 1) * nb, :])
            carry[gsl] = s
            sprev = jnp.concatenate(prev, axis=1).astype(BF16)
            y = y + bdot_nt(zt_ref[gsl], sprev)
            for t in range(q):
                yt[pl.ds(r0, gu * S5_CH), t * nkb:(t + 1) * nkb] += (
                    y[:, t * S5_CH:(t + 1) * S5_CH, :].reshape(gu * S5_CH, nkb))
            return 0

        lax.fori_loop(0, ng // gu, groups, 0)
        if perm:
            sfin_ref[...] = carry[...]
        else:
            lane_s = lax.broadcasted_iota(jnp.int32, (nb, 2 * S5_STATE), 1)
            for pr in range(ng // 2):
                cols = slice(pr * 2 * S5_STATE, (pr + 1) * 2 * S5_STATE)
                c0, c1 = carry[2 * pr], carry[2 * pr + 1]
                sfin_ref[0, :, cols] = jnp.where(lane_s < S5_STATE, c0, pltpu.roll(c1, S5_STATE, 1))
                sfin_ref[1, :, cols] = jnp.where(lane_s < S5_STATE, pltpu.roll(c0, S5_STATE, 1), c1)

    for j in range(npt):
        @pl.when(ph == npi + j)
        def _():
            finish(j)


def _s5_mix(q, nb, x, ln_g, wu, tt, wt, zt, a1, a2, s0, dcol, wglut, bglu, wo):
    nseq, l, d = x.shape
    nblk = l // q
    nlt = nblk * nseq
    nkb = min(LANES, nlt)
    ntile = nlt // nkb
    tok = q * nkb
    npt = max(1, tok // S5_SLICE_TOKENS)
    dm = wo.shape[1]
    perm = q > 1
    if perm:
        half = 8
        npi = q // half
        assert nb == nseq and d % LANES == 0 and q % half == 0
        xv = x.reshape(nseq, nblk, npi, half, d // LANES, LANES).transpose(0, 1, 2, 4, 3, 5)
        assert npt % npi == 0
        x_spec = pl.BlockSpec((nseq, nkb // nseq, None, d // LANES, half, LANES),
                              lambda i, j: (0, i, jnp.minimum(j, npi - 1), 0, 0, 0))
    else:
        npi = 1
        xv = x.reshape(1, nseq, d)
        x_spec = pl.BlockSpec((1, nkb, d), lambda i, j: (0, i, 0))
    body = functools.partial(_s5_mix_body, q, nb, nkb, npi, npt, perm)
    p_spec = pl.BlockSpec((q // npt, nkb, dm), lambda i, j: (jnp.maximum(j - npi, 0), i, 0))
    return pl.pallas_call(
        body,
        grid=(ntile, npi + npt),
        in_specs=[x_spec, _const_spec(ln_g.shape), _const_spec(wu.shape), _const_spec(tt.shape),
                  _const_spec(wt.shape), _const_spec(zt.shape), _const_spec(a1.shape), _const_spec(a2.shape),
                  _const_spec(s0.shape), _const_spec(dcol.shape), _const_spec(wglut.shape),
                  _const_spec(bglu.shape), _const_spec(wo.shape)],
        out_specs=[p_spec, pl.BlockSpec(s0.shape, lambda i, j: (0, 0, 0))],
        out_shape=[jax.ShapeDtypeStruct((q, nlt, dm), F32), jax.ShapeDtypeStruct(s0.shape, F32)],
        scratch_shapes=[pltpu.VMEM((d, tok), BF16), pltpu.VMEM((d, tok), F32),
                        pltpu.VMEM((tt.shape[0], nb, 2 * S5_STATE), F32),
                        pltpu.VMEM((wu.shape[1], wu.shape[0]), BF16)],
        compiler_params=_params("arbitrary", "arbitrary"),
        name=f"s5_mix_q{q}",
    )(xv, ln_g, wu, tt, wt, zt, a1, a2, s0, dcol, wglut, bglu, wo)


def _kv_body(m_ref, wk_ref, wv_ref, k_ref, v_ref, kb_ref, vb_ref):
    tm, rows, _ = k_ref.shape
    nt = rows // X_HEADS
    m = m_ref[...].astype(BF16)
    for w_ref, o_ref, ob_ref in ((wk_ref, k_ref, kb_ref), (wv_ref, v_ref, vb_ref)):
        r = _dot(m, w_ref[...])
        ob_ref[...] = r.astype(BF16)
        flat = o_ref.reshape(tm * rows, LANES)
        for h in range(X_HEADS):
            for dt in range(nt):
                c = (h * nt + dt) * LANES
                flat[pl.ds(dt * X_HEADS + h, tm, stride=rows), :] = r[:, c:c + LANES]


def _kv_proj_rider(mem2d, wk, wv, nsteps):
    t, d = mem2d.shape
    assert t % nsteps == 0
    tm = t // nsteps
    rows = d // LANES
    row = pl.BlockSpec((tm, d), lambda i: (i, 0))
    tile = pl.BlockSpec((tm, rows, LANES), lambda i: (i, 0, 0))
    return (_kv_body, (mem2d, wk, wv), (row, _const_spec(wk.shape), _const_spec(wv.shape)),
            (tile, tile, row, row),
            (jax.ShapeDtypeStruct((t, rows, LANES), F32),) * 2 + (jax.ShapeDtypeStruct((t, d), BF16),) * 2)


def _tile_unview_kv(kv8, b, m):
    nt = kv8.shape[1] // X_HEADS
    return kv8.reshape(b, m, nt, X_HEADS, LANES).transpose(0, 1, 3, 2, 4).reshape(b, m, X_HEADS, nt * LANES)


def _attn_prompt_body(x_ref, ys_ref, p_ref, wo1_ref, g2_ref, wq_ref, k_ref, v_ref, wxo_ref, o_ref):
    d = x_ref.shape[-1]
    hd = d // X_HEADS
    x1 = x_ref[...] + _dot(ys_ref[...], wo1_ref[...]) + p_ref[...]
    hq = _rms(x1, g2_ref[...]).astype(BF16)
    qv = _dot(hq, wq_ref[...]).astype(BF16)
    kb = k_ref[...]
    vb = v_ref[...]
    outs = []
    for h in range(X_HEADS):
        sl = slice(h * hd, (h + 1) * hd)
        s = _dot_nt(qv[:, sl], kb[:, sl]) * (hd ** -0.5)
        e = jnp.exp(s - jnp.max(s, axis=-1, keepdims=True))
        p = (e / jnp.sum(e, axis=-1, keepdims=True)).astype(BF16)
        outs.append(_dot(p, vb[:, sl]))
    o = jnp.concatenate(outs, axis=1).astype(BF16)
    o_ref[...] = x1 + _dot(o, wxo_ref[...])


def _attn_prompt(x, ys, pm, wo1, g2, wq, mk, mv, wxo, make_rider):
    b, l, d = x.shape
    nm = mk.shape[1]
    tm = min(TOKEN_TILE, l)
    nl = l // tm
    row = pl.BlockSpec((None, tm, d), lambda i, j: (i, j, 0))
    kv = pl.BlockSpec((None, nm, d), lambda i, j: (i, 0, 0))
    ins = [x, ys, pm, wo1, g2, wq, mk, mv, wxo]
    specs = [row, row, row, _const_spec(wo1.shape), _const_spec(g2.shape), _const_spec(wq.shape), kv, kv,
             _const_spec(wxo.shape)]
    rider = make_rider(lambda i, j: i * nl + j, b * nl)
    body = functools.partial(_two_bodies, _attn_prompt_body, len(ins), 1, rider[0], len(rider[1]))
    return pl.pallas_call(
        body, grid=(b, nl),
        in_specs=specs + list(rider[2]),
        out_specs=[row] + list(rider[3]),
        out_shape=[jax.ShapeDtypeStruct((b, l, d), F32)] + list(rider[4]),
        compiler_params=_params("arbitrary", "arbitrary"),
        name="attn_prompt",
    )(*ins, *rider[1])


def _merge_q_body(x_ref, ys_ref, p_ref, wo1_ref, g2_ref, wq_ref, x1_ref, q_ref):
    x1 = x_ref[...] + _dot(ys_ref[...], wo1_ref[...]) + p_ref[...]
    x1_ref[...] = x1
    q_ref[...] = _dot(_rms(x1, g2_ref[...]).astype(BF16), wq_ref[...])


def _merge_q(x2d, ys, pm, wo1, g2, wq):
    t, d = x2d.shape
    return pl.pallas_call(
        _merge_q_body,
        out_shape=[jax.ShapeDtypeStruct((t, d), F32)] * 2,
        compiler_params=pltpu.CompilerParams(vmem_limit_bytes=VMEM_LIMIT),
        name="merge_q",
    )(x2d, ys, pm, wo1, g2, wq)


def _tile_view(kv):
    b, m, nh, hd = kv.shape
    nt = hd // LANES
    return kv.reshape(b, m, nh, nt, LANES).transpose(0, 1, 3, 2, 4).reshape(b, m, nt * nh, LANES)


def _tile_unview(o):
    b, rows, _ = o.shape
    nt = rows // X_HEADS
    return o.reshape(b, nt, X_HEADS, LANES).transpose(0, 2, 1, 3).reshape(b, rows * LANES)


def _attn_step_body(q_ref, k_ref, v_ref, o_ref):
    bt, nm, rows, _ = k_ref.shape
    hd = rows * LANES // X_HEADS
    for b in range(bt):
        r = jnp.sum(k_ref[b] * q_ref[b], axis=-1, keepdims=True)
        s = (r + pltpu.roll(r, X_HEADS, 1)) * (hd ** -0.5)
        e = jnp.exp(s - jnp.max(s, axis=0, keepdims=True))
        p = e / jnp.sum(e, axis=0, keepdims=True)
        o_ref[b] = jnp.sum(p * v_ref[b], axis=0)


def _attn_step_rider(q8, k8, v8, nsteps):
    nbt, rows, _ = q8.shape
    nm = k8.shape[1]
    assert rows == 2 * X_HEADS, "score assembly assumes two 128-lane tiles per head"
    assert nbt % nsteps == 0
    bt = nbt // nsteps
    qs = pl.BlockSpec((bt, rows, LANES), lambda i: (i, 0, 0))
    kv = pl.BlockSpec((bt, nm, rows, LANES), lambda i: (i, 0, 0, 0))
    return (_attn_step_body, (q8, k8, v8), (qs, kv, kv), (qs,),
            (jax.ShapeDtypeStruct((nbt, rows, LANES), F32),))


def _ffn_body(has_o, final, *refs):
    if has_o:
        x_ref, o_in_ref, wxo_ref, g3_ref, wg_ref, wu_ref, wd_ref, gf_ref, y_ref = refs
        x2 = x_ref[...] + _dot(o_in_ref[...].astype(BF16), wxo_ref[...])
    else:
        x_ref, g3_ref, wg_ref, wu_ref, wd_ref, gf_ref, y_ref = refs
        x2 = x_ref[...]
    hf = _rms(x2, g3_ref[...]).astype(BF16)
    ff = wg_ref.shape[1]
    x3 = x2
    for lo in range(0, ff, FF_CHUNK):
        hi = min(lo + FF_CHUNK, ff)
        act = (_silu(_dot(hf, wg_ref[:, lo:hi])) * _dot(hf, wu_ref[:, lo:hi])).astype(BF16)
        x3 = x3 + _dot(act, wd_ref[lo:hi, :])
    y_ref[...] = _rms(x3, gf_ref[...]) if final else x3


def _ffn(x2d, g3, wg, wu, wd, gf, final, o_in=None, wxo=None, rider=None):
    t, d = x2d.shape
    tm = min(TOKEN_TILE, t)
    row = pl.BlockSpec((tm, d), lambda i: (i, 0))
    has_o = o_in is not None
    ins = [x2d] + ([o_in, wxo] if has_o else []) + [g3, wg, wu, wd, gf]
    specs = [row] + ([row, _const_spec(wxo.shape)] if has_o else []) + [
        _const_spec(g3.shape), _const_spec(wg.shape), _const_spec(wu.shape), _const_spec(wd.shape),
        _const_spec(gf.shape)]
    body = functools.partial(_ffn_body, has_o, final)
    out_specs, out_shape = [row], [jax.ShapeDtypeStruct((t, d), F32)]
    if rider is not None:
        body = functools.partial(_two_bodies, body, len(ins), 1, rider[0], len(rider[1]))
        ins, specs = ins + list(rider[1]), specs + list(rider[2])
        out_specs, out_shape = out_specs + list(rider[3]), out_shape + list(rider[4])
    res = pl.pallas_call(
        body, grid=(t // tm,),
        in_specs=specs, out_specs=out_specs, out_shape=out_shape,
        compiler_params=_params("arbitrary"),
        name="ffn_o" if has_o else "ffn",
    )(*ins)
    return res[0] if rider is None else res


def _two_bodies(body_a, n_in_a, n_out_a, body_b, n_in_b, *refs):
    outs = refs[n_in_a + n_in_b:]
    body_a(*refs[:n_in_a], *outs[:n_out_a])
    body_b(*refs[n_in_a:n_in_a + n_in_b], *outs[n_out_a:])


def _merge_riders(ra, rb):
    body = functools.partial(_two_bodies, ra[0], len(ra[1]), len(ra[3]), rb[0], len(rb[1]))
    return (body,) + tuple(tuple(ra[k]) + tuple(rb[k]) for k in range(1, 5))


def _cast_body(*refs):
    n = len(refs) // 2
    for src, dst in zip(refs[:n], refs[n:]):
        dst[...] = src[...].astype(dst.dtype)


def _cast_rider(ws, step_of, nsteps):
    bf16_rows = 16
    ins, in_specs, out_specs, shapes = [], [], [], []
    for w in ws:
        w, lo, hi = w if isinstance(w, tuple) else (w, 0, w.shape[0])
        r, c = hi - lo, w.shape[1]
        k = next(k for k in (1, 2, 4, 8) if r % (nsteps // k) == 0 and (r // (nsteps // k)) % bf16_rows == 0)
        rp = r // (nsteps // k)
        assert lo % rp == 0
        ins.append(w)
        in_specs.append(pl.BlockSpec((rp, c), lambda *g, k=k, o=lo // rp: (o + step_of(*g) // k, 0)))
        out_specs.append(pl.BlockSpec((rp, c), lambda *g, k=k: (step_of(*g) // k, 0)))
        shapes.append(jax.ShapeDtypeStruct((r, c), BF16))
    return (_cast_body, tuple(ins), tuple(in_specs), tuple(out_specs), tuple(shapes))


def _layer_weights(i, ln1_g, w_in, conv_w, conv_b, dt_bias, a_log, d_ssd, gn_g, lam_re, lam_im, log_step,
                   b_re, b_im, c_re, c_im, d_s5, w_glu, b_glu, w_out, ln2_g, w_xq, w_xk, w_xv, w_xo, ln3_g,
                   w_gate, w_up, w_down):
    d = w_in.shape[1]
    n_heads = dt_bias.shape[1]
    dssd = n_heads * SSD_HEAD_DIM
    conv_dim = conv_w.shape[2]
    o1, o2, o3 = dssd, dssd + conv_dim, dssd + conv_dim + n_heads
    win = w_in[i]
    w = {}
    w["ln1"] = ln1_g[i].reshape(1, d)
    w["win"] = win.astype(BF16)
    w["wu_s5"] = win[:, o3:].astype(BF16)
    w["cw"] = conv_w[i]
    w["cb"] = conv_b[i].reshape(1, conv_dim)
    w["dtb"] = jnp.pad(dt_bias[i], (0, LANES - n_heads)).reshape(1, LANES)
    w["alog"] = jnp.pad(a_log[i], (0, LANES - n_heads)).reshape(1, LANES)
    w["dexp"] = jnp.repeat(d_ssd[i], SSD_HEAD_DIM).reshape(1, dssd)
    w["gn"] = gn_g[i].reshape(1, dssd)
    w["s5"] = (lam_re[i], lam_im[i], log_step[i], b_re[i], b_im[i], c_re[i], c_im[i])
    ds5 = d_s5.shape[1]
    w["dcol"] = d_s5[i].reshape(ds5, 1)
    w["wglut"] = w_glu[i].T.astype(BF16)
    w["bglu"] = b_glu[i].reshape(ds5, 1)
    w["cast_f32"] = ((w_out[i], 0, dssd), (w_out[i], dssd, w_out.shape[1]), w_xq[i], w_xo[i])
    w["ln2"] = ln2_g[i].reshape(1, d)
    w["wk"] = w_xk[i].astype(BF16)
    w["wv"] = w_xv[i].astype(BF16)
    w["ln3"] = ln3_g[i].reshape(1, d)
    w["ffn_f32"] = (w_gate[i], w_up[i], w_down[i])
    return w


def _rows_to_state(s):
    p = s.shape[-1] // 2
    st = s.transpose(1, 0, 2)
    return st[..., :p], st[..., p:]


def kernel(x_prompt, x_sample, mem_prompt, state_conv, state_ssm, state_s5_re, state_s5_im, cache_mem_k, cache_mem_v, ln1_g, w_in, conv_w, conv_b, dt_bias, a_log, d_ssd, gn_g, lam_re, lam_im, log_step, b_re, b_im, c_re, c_im, d_s5, w_glu, b_glu, w_out, ln2_g, w_xq, w_xk, w_xv, w_xo, ln3_g, w_gate, w_up, w_down, final_g):
    bp, seq, d = x_prompt.shape
    bs = x_sample.shape[0]
    depth = w_in.shape[0]
    nm = mem_prompt.shape[1]
    ng, ns = lam_re.shape[1], lam_re.shape[2]
    gf = final_g.reshape(1, d)
    hp = x_prompt
    hs = x_sample.reshape(bs, d)
    outs = {k: [] for k in ("conv_p", "ssm_p", "re_p", "im_p", "mk_p", "mv_p", "conv_s", "ssm_s", "re_s", "im_s")}
    yp = ys_out = None
    for i in range(depth):
        w = _layer_weights(i, ln1_g, w_in, conv_w, conv_b, dt_bias, a_log, d_ssd, gn_g, lam_re, lam_im, log_step,
                           b_re, b_im, c_re, c_im, d_s5, w_glu, b_glu, w_out, ln2_g, w_xq, w_xk, w_xv, w_xo, ln3_g,
                           w_gate, w_up, w_down)
        last = i == depth - 1
        attn_steps = bp * (seq // min(TOKEN_TILE, seq))
        y_ssd, conv_p, ssm_p, zs, xbcs, dts = _ssd_prompt(hp, w["ln1"], w["win"], w["cw"], w["cb"], w["dtb"],
                                                          w["alog"], w["dexp"], w["gn"], hs, bs // attn_steps)
        q = S5_Q
        prep_riders = lambda n: _merge_riders(
            _kv_proj_rider(mem_prompt.reshape(bp * nm, d), w["wk"], w["wv"], n),
            _cast_rider(w["cast_f32"], lambda i: i, n))
        (tt, wt, zt, a1, a2, a11, a21), (mk8, mv8, mk, mv, w["wo1"], w["wo2"], w["wq"], w["wxo"]) = _s5_prep(
            q, *w["s5"], prep_riders)
        mk = mk.reshape(bp, nm, d)
        mv = mv.reshape(bp, nm, d)
        s0 = jnp.zeros((ng, bp, 2 * ns), F32)
        pperm, sfin = _s5_mix(q, bp, hp, w["ln1"], w["wu_s5"],tt, wt, zt, a1, a2, s0, w["dcol"], w["wglut"],
                              w["bglu"], w["wo2"])
        pm = pperm.reshape(q, seq // q, bp, d).transpose(2, 1, 0, 3).reshape(bp, seq, d)
        re_p, im_p = _rows_to_state(sfin)
        riders = lambda step_of, n: _merge_riders(
            _ssd_step_rider(zs, xbcs, dts, state_conv[i], state_ssm[i], w["cw"], w["cb"], w["dtb"], w["alog"],
                            w["dexp"], w["gn"], step_of, n),
            _cast_rider(w["ffn_f32"], step_of, n))
        x2, ys_ssd, conv_s, ssm_s, w["wg"], w["wu"], w["wdn"] = _attn_prompt(
            hp, y_ssd, pm, w["wo1"], w["ln2"], w["wq"], mk, mv, w["wxo"], riders)
        ys_ssd = ys_ssd.reshape(bs, -1)
        tt1, wt1, zt1 = _s5_single_token_mats(q, tt, wt, zt)
        s0s = jnp.stack([state_s5_re[i].reshape(bs, ng * ns), state_s5_im[i].reshape(bs, ng * ns)])
        ps, sfin_s = _s5_mix(1, bs, hs.reshape(bs, 1, d), w["ln1"], w["wu_s5"],tt1, wt1, zt1, a11, a21, s0s,
                             w["dcol"], w["wglut"], w["bglu"], w["wo2"])
        re_s, im_s = sfin_s[0].reshape(bs, ng, ns), sfin_s[1].reshape(bs, ng, ns)
        x1s, qs = _merge_q(hs, ys_ssd, ps.reshape(bs, d), w["wo1"], w["ln2"], w["wq"])
        attn_rider = _attn_step_rider(_tile_view(qs.reshape(bs, 1, X_HEADS, d // X_HEADS))[:, 0],
                                      _tile_view(cache_mem_k[i]), _tile_view(cache_mem_v[i]),
                                      (bp * seq) // min(TOKEN_TILE, bp * seq))
        hp, o_s = _ffn(x2.reshape(bp * seq, d), w["ln3"], w["wg"], w["wu"], w["wdn"], gf, last, rider=attn_rider)
        hp = hp.reshape(bp, seq, d)
        hs = _ffn(x1s, w["ln3"], w["wg"], w["wu"], w["wdn"], gf, last, o_in=_tile_unview(o_s), wxo=w["wxo"])
        for k, v in (("conv_p", conv_p), ("ssm_p", ssm_p), ("re_p", re_p), ("im_p", im_p),
                     ("mk_p", _tile_unview_kv(mk8, bp, nm)), ("mv_p", _tile_unview_kv(mv8, bp, nm)),
                     ("conv_s", conv_s), ("ssm_s", ssm_s), ("re_s", re_s), ("im_s", im_s)):
            outs[k].append(v)
    st = lambda k: jnp.stack(outs[k])
    return (hp, hs.reshape(bs, 1, d), st("conv_p"), st("ssm_p"), st("re_p"), st("im_p"), st("mk_p"), st("mv_p"),
            st("conv_s"), st("ssm_s"), st("re_s"), st("im_s"))
```

```python
import functools
import math

import jax
import jax.numpy as jnp
from jax import lax
from jax.experimental import pallas as pl
from jax.experimental.pallas import tpu as pltpu

F32 = jnp.float32
BF16 = jnp.bfloat16
EPS = 1e-6

LANES = 128
VMEM_LIMIT = 60 * 1024 * 1024

SSD_HEAD_DIM = 64
SSD_STATE = 128
SSD_GROUPS = 2
SSD_CHUNK = 128
CONV_W = 4
S5_CH = 16
S5_STATE = 64
S5_Q = 16
S5_GROUP_BATCH = 32
S5_SLICE_TOKENS = 512
X_HEADS = 4
TOKEN_TILE = 512
PROJ_PIECE = 256
SSD_SEQS_PER_STEP = 4
FF_CHUNK = 1024


def _const_spec(shape):
    nd = len(shape)
    return pl.BlockSpec(shape, lambda *_: (0,) * nd, pipeline_mode=pl.Buffered(1))


def _params(*sem):
    return pltpu.CompilerParams(dimension_semantics=sem, vmem_limit_bytes=VMEM_LIMIT)


def _rms(x, g):
    return x * lax.rsqrt(jnp.mean(x * x, axis=-1, keepdims=True) + EPS) * g


def _sigmoid(x):
    return 0.5 * jnp.tanh(0.5 * x) + 0.5


def _silu(x):
    h = 0.5 * x
    return h + h * jnp.tanh(h)


def _gelu_tanh(x):
    c = math.sqrt(2.0 / math.pi)
    h = 0.5 * x
    return h + h * jnp.tanh(x * (c + (0.044715 * c) * (x * x)))


def _softplus(x):
    return jnp.maximum(x, 0.0) + jnp.log1p(jnp.exp(-jnp.abs(x)))


def _dot(a, b):
    return jnp.dot(a, b, preferred_element_type=F32)


def _dot_nt(a, b):
    return lax.dot_general(a, b, (((1,), (1,)), ((), ())), preferred_element_type=F32)


def _dot_tn(a, b):
    return lax.dot_general(a, b, (((0,), (0,)), ((), ())), preferred_element_type=F32)


def _proj_windows(d_ssd, conv_dim):
    assert d_ssd % LANES == 0 and conv_dim % LANES == 0
    return (0, d_ssd), (d_ssd, d_ssd + conv_dim), (d_ssd + conv_dim, d_ssd + conv_dim + LANES)


def _expand_heads(v, n_heads):
    rows = v.shape[0]
    lane = lax.broadcasted_iota(jnp.int32, (rows, LANES), 1)
    pieces = [jnp.where(lane < SSD_HEAD_DIM, v[:, 2 * j:2 * j + 1], v[:, 2 * j + 1:2 * j + 2])
              for j in range(n_heads // 2)]
    return jnp.concatenate(pieces, axis=1)


def _ssd_chunk_math(z, x, dt_raw, last_chunk, cw_ref, cb_ref, dtb_ref, alog_ref, dexp_ref, gn_ref, e2_ref,
                    y_ref, conv_ref, ssm_ref, xpad, state, side_jobs=()):
    jobs = list(side_jobs)

    def side(n=1):
        for _ in range(n):
            if jobs:
                jobs.pop(0)()

    q = SSD_CHUNK
    d_ssd = z.shape[-1]
    n_heads = d_ssd // SSD_HEAD_DIM
    hpg = n_heads // SSD_GROUPS
    gw = hpg * SSD_HEAD_DIM
    nct = xpad.shape[0]
    xt = d_ssd // LANES

    for t in range(nct):
        xpad[t, 8:8 + q, :] = x[:, t * LANES:(t + 1) * LANES]
    cw = cw_ref[...]
    conv = (cw[3] * xpad[:, 8:8 + q, :] + cw[2] * xpad[:, 7:7 + q, :] + cw[1] * xpad[:, 6:6 + q, :]
            + cw[0] * xpad[:, 5:5 + q, :] + cb_ref[...])
    xpad[:, 0:8, :] = xpad[:, q:q + 8, :]
    xact = _silu(conv)
    xs = jnp.concatenate([xact[t] for t in range(xt)], axis=1)
    bm = xact[xt:xt + SSD_GROUPS].astype(BF16)
    cm = xact[xt + SSD_GROUPS:].astype(BF16)
    side()

    dt = _softplus(dt_raw + dtb_ref[...])
    a = -jnp.exp(alog_ref[...])
    row = lax.broadcasted_iota(jnp.int32, (q, q), 0)
    col = lax.broadcasted_iota(jnp.int32, (q, q), 1)
    causal = row >= col
    tri = jnp.where(causal, 1.0, 0.0).astype(F32)
    acum = jnp.dot(tri, dt * a, precision=lax.Precision.HIGHEST, preferred_element_type=F32)
    acum_t = acum.T
    last = acum[q - 1:q, :]

    def expand(v):
        hi = v.astype(BF16)
        lo = (v - hi.astype(F32)).astype(BF16)
        return _dot(jnp.concatenate([hi, lo], axis=1), e2_ref[...])

    dt_e = expand(dt)
    ea_e = expand(jnp.exp(acum))
    dend_e = expand(jnp.exp(last - acum))

    dtx = xs * dt_e
    dtx_b = dtx.astype(BF16)
    xdec_b = (dtx * dend_e).astype(BF16)
    side()

    lane = lax.broadcasted_iota(jnp.int32, (q, LANES), 1)
    zero_b = jnp.zeros((q, LANES), BF16)
    y_tiles = []
    for g in range(SSD_GROUPS):
        bg = bm[g]
        cg = cm[g]
        cb = _dot_nt(cg, bg)
        gs = slice(g * gw, (g + 1) * gw)
        st_g = state[:, gs]
        y_off = _dot(cg, st_g.astype(BF16)) * ea_e[:, gs]
        for pr in range(hpg // 2):
            tile = g * (hpg // 2) + pr
            ms = []
            for h in (2 * tile, 2 * tile + 1):
                seg = acum[:, h:h + 1] - acum_t[h:h + 1, :]
                lmat = jnp.exp(jnp.where(causal, seg, -jnp.inf))
                ms.append((cb * lmat).astype(BF16))
            dtile = dtx_b[:, tile * LANES:(tile + 1) * LANES]
            rhs = jnp.concatenate([jnp.where(lane < SSD_HEAD_DIM, dtile, zero_b),
                                   jnp.where(lane >= SSD_HEAD_DIM, dtile, zero_b)], axis=0)
            y_tiles.append(_dot(jnp.concatenate(ms, axis=1), rhs)
                           + y_off[:, pr * LANES:(pr + 1) * LANES])
            if pr % 2 == 1:
                side()
        new = _dot_tn(bg, xdec_b[:, gs])
        state[:, gs] = st_g * ea_e[q - 1:q, gs] + new
    y = jnp.concatenate(y_tiles, axis=1) + dexp_ref[...] * xs
    y = y * _silu(z)
    ms = jnp.mean(y * y, axis=-1, keepdims=True)
    side(len(jobs))
    y_ref[...] = (y * lax.rsqrt(ms + EPS) * gn_ref[...]).astype(y_ref.dtype)

    @pl.when(last_chunk)
    def _():
        conv_ref[...] = x[q - (CONV_W - 1):q, :]
        ssm_ref[...] = state[...].T.reshape(ssm_ref.shape)


def _ssd_prompt_body(wins, x_ref, g_ref, w_ref, cw_ref, cb_ref, dtb_ref, alog_ref, dexp_ref, gn_ref,
                     e2_ref, xs_ref, y_ref, conv_ref, ssm_ref, zs_ref, xbcs_ref, dts_ref,
                     za, xa, da, zb, xb, db, xpad, state):
    i = pl.program_id(0)
    j = pl.program_id(1)
    nc = pl.num_programs(1) - 1
    ns, q, d = x_ref.shape

    @pl.when(jnp.logical_and(i == 0, j == 0))
    def _():
        for r in (zb, xb, db):
            r[...] = jnp.zeros(r.shape, F32)
        hs = _rms(xs_ref[...], g_ref[...]).astype(BF16)
        for (lo, hi), o_ref in zip(wins, (zs_ref, xbcs_ref, dts_ref)):
            o_ref[...] = _dot(hs, w_ref[:, lo:hi]).reshape(o_ref.shape)

    @pl.when(j <= 1)
    def _():
        xpad[:, :, 0:8, :] = jnp.zeros(xpad.shape[:2] + (8, LANES), F32)
        state[...] = jnp.zeros(state.shape, F32)

    def step(wr, rd):
        xin = x_ref[...].reshape(ns * q, d)
        msx = jnp.mean(xin * xin, axis=-1, keepdims=True)
        hbox = []

        def proj_piece(dst, base, lo, hi):
            def run():
                if not hbox:
                    hbox.append((xin * lax.rsqrt(msx + EPS) * g_ref[...]).astype(BF16))
                dst[:, lo:hi] = _dot(hbox[0], w_ref[:, base + lo:base + hi])
            return run

        pieces = []
        for dst, (w_lo, w_hi) in ((wr[0], wins[0]), (wr[2], wins[2]), (wr[1], wins[1])):
            n = w_hi - w_lo
            pieces += [proj_piece(dst, w_lo, lo, min(lo + PROJ_PIECE, n)) for lo in range(0, n, PROJ_PIECE)]
        per = -(-len(pieces) // ns)
        for s in range(ns):
            rows = slice(s * q, (s + 1) * q)
            _ssd_chunk_math(rd[0][rows, :], rd[1][rows, :], rd[2][rows, :], j == nc, cw_ref, cb_ref, dtb_ref,
                            alog_ref, dexp_ref, gn_ref, e2_ref, y_ref.at[s], conv_ref.at[s], ssm_ref.at[s],
                            xpad.at[s], state.at[s], pieces[s * per:(s + 1) * per])

    @pl.when(j % 2 == 0)
    def _():
        step((za, xa, da), (zb, xb, db))

    @pl.when(j % 2 == 1)
    def _():
        step((zb, xb, db), (za, xa, da))


def _ssd_prompt(x, ln_g, w_in, cw, cb, dtb, alog, dexp, gn, xs, group):
    b, l, d = x.shape
    ts = xs.shape[0]
    conv_dim = cw.shape[1]
    d_ssd = conv_dim - 2 * SSD_GROUPS * SSD_STATE
    wins = _proj_windows(d_ssd, conv_dim)
    n_heads = d_ssd // SSD_HEAD_DIM
    q = SSD_CHUNK
    nc = l // q
    assert SSD_STATE == LANES and conv_dim == d_ssd + 2 * SSD_GROUPS * SSD_STATE
    nct = conv_dim // LANES
    cw4 = cw.reshape(CONV_W, nct, 1, LANES)
    cb3 = cb.reshape(nct, 1, LANES)
    e2 = (jnp.arange(2 * LANES)[:, None] % LANES == jnp.arange(d_ssd)[None, :] // SSD_HEAD_DIM).astype(BF16)
    ns = SSD_SEQS_PER_STEP if b % SSD_SEQS_PER_STEP == 0 else 1
    slot = lambda n: pltpu.VMEM((ns * q, n), F32)
    return pl.pallas_call(
        functools.partial(_ssd_prompt_body, wins),
        grid=(b // ns, nc + 1),
        in_specs=[pl.BlockSpec((ns, q, d), lambda i, j: (i, jnp.minimum(j, nc - 1), 0)),
                  _const_spec(ln_g.shape), _const_spec(w_in.shape),
                  _const_spec(cw4.shape), _const_spec(cb3.shape), _const_spec(dtb.shape), _const_spec(alog.shape),
                  _const_spec(dexp.shape), _const_spec(gn.shape), _const_spec(e2.shape), _const_spec(xs.shape)],
        out_specs=[pl.BlockSpec((ns, q, d_ssd), lambda i, j: (i, jnp.maximum(j - 1, 0), 0)),
                   pl.BlockSpec((ns, CONV_W - 1, conv_dim), lambda i, j: (i, 0, 0)),
                   pl.BlockSpec((ns, n_heads, SSD_HEAD_DIM, SSD_STATE), lambda i, j: (i, 0, 0, 0))]
        + [pl.BlockSpec((ts // group, group, hi - lo), lambda i, j: (0, 0, 0)) for lo, hi in wins],
        out_shape=[jax.ShapeDtypeStruct((b, l, d_ssd), BF16),
                   jax.ShapeDtypeStruct((b, CONV_W - 1, conv_dim), F32),
                   jax.ShapeDtypeStruct((b, n_heads, SSD_HEAD_DIM, SSD_STATE), F32)]
        + [jax.ShapeDtypeStruct((ts // group, group, hi - lo), F32) for lo, hi in wins],
        scratch_shapes=[slot(d_ssd), slot(conv_dim), slot(LANES), slot(d_ssd), slot(conv_dim), slot(LANES),
                        pltpu.VMEM((ns, nct, q + 8, LANES), F32), pltpu.VMEM((ns, SSD_STATE, d_ssd), F32)],
        compiler_params=_params("arbitrary", "arbitrary"),
        name="ssd_prompt",
    )(x, ln_g, w_in, cw4, cb3, dtb, alog, dexp, gn, e2, xs)


def _ssd_step_body(z_ref, xbc_ref, dt_ref, cs_ref, st_ref, cw_ref, cb_ref, dtb_ref, alog_ref, dexp_ref, gn_ref,
                   y_ref, conv_ref, ssm_ref):
    bt, d_ssd = z_ref.shape
    n_heads = d_ssd // SSD_HEAD_DIM
    hpg = n_heads // SSD_GROUPS
    x = xbc_ref[...]
    cw = cw_ref[...]
    cs = [cs_ref[:, k, :] for k in range(CONV_W - 1)]
    conv = cw[0:1] * cs[0] + cw[1:2] * cs[1] + cw[2:3] * cs[2] + cw[3:4] * x + cb_ref[...]
    conv_ref[:, 0, :] = cs[1]
    conv_ref[:, 1, :] = cs[2]
    conv_ref[:, 2, :] = x
    xact = _silu(conv)
    xs = xact[:, :d_ssd]
    bm = xact[:, d_ssd:d_ssd + SSD_GROUPS * SSD_STATE]
    cm = xact[:, d_ssd + SSD_GROUPS * SSD_STATE:]
    dt = _softplus(dt_ref[...] + dtb_ref[...])
    da = jnp.exp(dt * (-jnp.exp(alog_ref[...])))
    dtx = xs * _expand_heads(dt, n_heads)
    pad = jnp.zeros((LANES - bt, d_ssd), F32)
    to_cols = lambda v: jnp.concatenate([v, pad], axis=0).T
    dtx_t = to_cols(dtx)
    da_t = to_cols(_expand_heads(da, n_heads))
    lane = lax.broadcasted_iota(jnp.int32, (d_ssd, LANES), 1)
    gw = hpg * SSD_HEAD_DIM
    rows_of = lambda v, b: jnp.concatenate(
        [jnp.broadcast_to(v[b:b + 1, g * SSD_STATE:(g + 1) * SSD_STATE], (gw, SSD_STATE)) for g in range(SSD_GROUPS)],
        axis=0)
    y_t = jnp.zeros((d_ssd, LANES), F32)
    for b in range(bt):
        s0 = st_ref[b].reshape(d_ssd, SSD_STATE)
        s_new = s0 * da_t[:, b:b + 1] + dtx_t[:, b:b + 1] * rows_of(bm, b)
        ssm_ref[b] = s_new.reshape(n_heads, SSD_HEAD_DIM, SSD_STATE)
        ycol = jnp.sum(s_new * rows_of(cm, b), axis=1, keepdims=True)
        y_t = jnp.where(lane == b, ycol, y_t)
    y = y_t.T[:bt, :] + dexp_ref[...] * xs
    y = y * _silu(z_ref[...])
    y_ref[...] = _rms(y, gn_ref[...]).astype(y_ref.dtype)


def _ssd_step_rider(z, xbc, dt, conv_state, ssm_state, cw, cb, dtb, alog, dexp, gn, step_of, nsteps):
    _, bt, d_ssd = z.shape
    conv_dim = xbc.shape[-1]
    n_heads = d_ssd // SSD_HEAD_DIM
    assert z.shape[0] == nsteps and conv_state.shape[0] == nsteps * bt
    row = lambda n: pl.BlockSpec((None, bt, n), lambda *g: (step_of(*g), 0, 0))
    cs_spec = pl.BlockSpec((bt, CONV_W - 1, conv_dim), lambda *g: (step_of(*g), 0, 0))
    st_spec = pl.BlockSpec((bt, n_heads, SSD_HEAD_DIM, SSD_STATE), lambda *g: (step_of(*g), 0, 0, 0))
    cst = lambda a: pl.BlockSpec(a.shape, lambda *g: (0,) * a.ndim, pipeline_mode=pl.Buffered(1))
    ins = (z, xbc, dt, conv_state, ssm_state, cw, cb, dtb, alog, dexp, gn)
    specs = (row(d_ssd), row(conv_dim), row(LANES), cs_spec, st_spec, cst(cw), cst(cb), cst(dtb), cst(alog),
             cst(dexp), cst(gn))
    out_shapes = (jax.ShapeDtypeStruct((nsteps, bt, d_ssd), BF16),
                  jax.ShapeDtypeStruct(conv_state.shape, F32),
                  jax.ShapeDtypeStruct(ssm_state.shape, F32))
    return (_ssd_step_body, ins, specs, (row(d_ssd), cs_spec, st_spec), out_shapes)


def _split_bf16(x):
    hi = x.astype(BF16)
    lo = (x - hi.astype(F32)).astype(BF16)
    return hi, lo


def _s5_prep_body(q, ls_ref, lrp_ref, lip_ref, lr2_ref, li2_ref, br_ref, bi_ref, ccat_ref, ca_ref, cb_ref, rep_ref,
                  tt_ref, wt_ref, zt_ref, a1_ref, a2_ref, a1s_ref, a2s_ref):
    ng = ls_ref.shape[0]
    w = q * S5_CH
    step = jnp.exp(ls_ref[...])
    lr, li = lrp_ref[...], lip_ref[...]
    mag = jnp.exp(lr * step)
    ang = li * step
    lbr = mag * jnp.cos(ang)
    lbi = mag * jnp.sin(ang)
    den = lr * lr + li * li
    kr = ((lbr - 1.0) * lr + lbi * li) / den
    ki = (lbi * lr - (lbr - 1.0) * li) / den
    br, bi = br_ref[...], bi_ref[...]
    bbr = kr * br - ki * bi
    bbi = kr * bi + ki * br
    if q > 1:
        np_ = br.shape[1]
        rep = lambda v: jnp.dot(v.reshape(ng * np_, S5_CH), rep_ref[...], precision=lax.Precision.HIGHEST,
                                preferred_element_type=F32).reshape(ng, np_, w)
        bbr, bbi = rep(bbr), rep(bbi)
    if q == 1:
        ball = jnp.concatenate([bbr, bbi], axis=1)
    else:
        d = lax.broadcasted_iota(jnp.int32, (1, 1, w), 2) // S5_CH
        fr, fi = lbr, lbi
        pr = pi = None
        for b in range((q - 1).bit_length()):
            bit = ((d >> b) & 1) == 1
            sr, si = jnp.where(bit, fr, 1.0), jnp.where(bit, fi, 0.0)
            pr, pi = (sr, si) if pr is None else (pr * sr - pi * si, pr * si + pi * sr)
            fr, fi = fr * fr - fi * fi, 2.0 * (fr * fi)
        ball = jnp.concatenate([pr * bbr - pi * bbi, pr * bbi + pi * bbr], axis=1)
    wt_ref[...] = ball.astype(BF16)

    lane3 = lax.broadcasted_iota(jnp.int32, (1, 1, 2 * S5_STATE), 2)
    first = lane3 < S5_STATE
    csign = jnp.where(first, ccat_ref[...], -ccat_ref[...])
    ch, cl = _split_bf16(csign)
    bh, bl = _split_bf16(ball)
    bdot = lambda x, y: lax.dot_general(x, y, (((2,), (1,)), ((0,), (0,))), preferred_element_type=F32)
    kall = bdot(ch, bh) + bdot(ch, bl) + bdot(cl, bh)
    k2 = kall.reshape(ng * S5_CH, w)
    lane2 = lax.broadcasted_iota(jnp.int32, (ng * S5_CH, w), 1)
    for t in range(q):
        sh = (q - 1 - t) * S5_CH
        r = pltpu.roll(k2, sh, 1) if sh else k2
        r = jnp.where(lane2 >= sh, r, 0.0)
        tt_ref[:, t * S5_CH:(t + 1) * S5_CH, :] = r.reshape(ng, S5_CH, w).astype(BF16)

    lr2, li2 = lr2_ref[...], li2_ref[...]
    t1 = (lax.broadcasted_iota(jnp.int32, (1, q, 1), 1) + 1).astype(F32)
    zm = jnp.exp(t1 * (lr2 * step))
    za = t1 * (li2 * step)
    zr = zm * jnp.cos(za)
    zi = zm * jnp.sin(za)
    ca, cb = ca_ref[...], cb_ref[...]
    for t in range(q):
        prt = zr[:, t:t + 1, :]
        pit = zi[:, t:t + 1, :]
        zt = jnp.where(first, ca * prt - cb * pit, -(ca * pit) - cb * prt)
        zt_ref[:, t * S5_CH:(t + 1) * S5_CH, :] = zt.astype(BF16)
    qf = float(q)
    mq = jnp.exp(qf * (lr2 * step))
    aq = qf * (li2 * step)
    ar = mq * jnp.cos(aq)
    ai = mq * jnp.sin(aq)
    a1_ref[...] = ar
    a2_ref[...] = jnp.where(first, -ai, ai)
    a1s_ref[...] = zr[:, 0:1, :]
    a2s_ref[...] = jnp.where(first, -zi[:, 0:1, :], zi[:, 0:1, :])


def _s5_prep(q, lam_re, lam_im, log_step, b_re, b_im, c_re, c_im, make_rider):
    ng, p = lam_re.shape
    w = q * S5_CH
    ls = log_step.reshape(ng, 1, 1)
    lrp = lam_re.reshape(ng, p, 1)
    lip = lam_im.reshape(ng, p, 1)
    lr2 = jnp.concatenate([lam_re, lam_re], axis=-1).reshape(ng, 1, 2 * p)
    li2 = jnp.concatenate([lam_im, lam_im], axis=-1).reshape(ng, 1, 2 * p)
    rep = (jnp.arange(S5_CH)[:, None] == jnp.arange(w)[None, :] % S5_CH).astype(F32)
    ccat = jnp.concatenate([c_re, c_im], axis=-1)
    ca = jnp.concatenate([c_re, c_re], axis=-1)
    cb = jnp.concatenate([c_im, c_im], axis=-1)
    gb = ng if q == 1 else 16
    blk = lambda a, b: pl.BlockSpec((gb, a, b), lambda i: (i, 0, 0))
    ins = (ls, lrp, lip, lr2, li2, b_re, b_im, ccat, ca, cb, rep)
    in_specs = [blk(1, 1), blk(p, 1), blk(p, 1), blk(1, 2 * p), blk(1, 2 * p), blk(p, S5_CH), blk(p, S5_CH),
                blk(S5_CH, 2 * p), blk(S5_CH, 2 * p), blk(S5_CH, 2 * p), _const_spec(rep.shape)]
    out_specs = [blk(w, w), blk(2 * p, w), blk(w, 2 * p)] + [blk(1, 2 * p)] * 4
    out_shape = [jax.ShapeDtypeStruct((ng, w, w), BF16), jax.ShapeDtypeStruct((ng, 2 * p, w), BF16),
                 jax.ShapeDtypeStruct((ng, w, 2 * p), BF16)] + [jax.ShapeDtypeStruct((ng, 1, 2 * p), F32)] * 4
    body = functools.partial(_s5_prep_body, q)
    rider = make_rider(ng // gb)
    body = functools.partial(_two_bodies, body, len(ins), len(out_specs), rider[0], len(rider[1]))
    res = pl.pallas_call(
        body,
        grid=(ng // gb,),
        in_specs=in_specs + list(rider[2]),
        out_specs=out_specs + list(rider[3]),
        out_shape=out_shape + list(rider[4]),
        compiler_params=_params("arbitrary"),
        name=f"s5_prep_q{q}",
    )(*ins, *rider[1])
    return res[:len(out_specs)], res[len(out_specs):]


def _s5_single_token_mats(q, tt, wt, zt):
    c = S5_CH
    return tt[:, :c, (q - 1) * c:], wt[:, :, :c], zt[:, :c, :]


def _s5_mix_body(q, nb, nkb, npi, npt, perm, x_ref, g_ref, wu_ref, tt_ref, wt_ref, zt_ref, a1_ref, a2_ref, s0_ref,
                 dcol_ref, wglut_ref, bglu_ref, wo_ref, p_ref, sfin_ref, ut, yt, carry, wut):
    ng = tt_ref.shape[0]
    nk = nkb // nb
    qs = q // npt
    ch = qs * nkb
    tile = pl.program_id(0)
    ph = pl.program_id(1)

    @pl.when(jnp.logical_and(tile == 0, ph == 0))
    def _():
        if perm:
            carry[...] = s0_ref[...]
        else:
            lane_s = lax.broadcasted_iota(jnp.int32, (nb, 2 * S5_STATE), 1)
            for pr in range(ng // 2):
                cols = slice(pr * 2 * S5_STATE, (pr + 1) * 2 * S5_STATE)
                re_t, im_t = s0_ref[0, :, cols], s0_ref[1, :, cols]
                carry[2 * pr] = jnp.where(lane_s < S5_STATE, re_t, pltpu.roll(im_t, S5_STATE, 1))
                carry[2 * pr + 1] = jnp.where(lane_s < S5_STATE, pltpu.roll(re_t, S5_STATE, 1), im_t)
        cw_ = 256
        for c in range(0, wu_ref.shape[1], cw_):
            wut[c:c + cw_, :] = wu_ref[:, c:c + cw_].T

    def rows_of(r):
        if not perm:
            return x_ref[r]
        nseq, nblk, ndt, nr, _ = x_ref.shape
        flat = x_ref.reshape(nseq * nblk * ndt * nr, LANES)

        return jnp.concatenate(
            [jnp.concatenate([flat[pl.ds((k * ndt + dt) * nr + r, nseq, stride=nblk * ndt * nr), :]
                              for dt in range(ndt)], axis=1) for k in range(nblk)], axis=0)

    def project(j):
        step = 2 if qs % 2 == 0 else 1
        r0 = (j * qs) % (q // npi)
        for r in range(0, qs, step):
            xin = jnp.concatenate([rows_of(r0 + r + s) for s in range(step)], axis=0)
            u = _dot_nt(wut[...], _rms(xin, g_ref[...]).astype(BF16))
            sl = slice((j * qs + r) * nkb, (j * qs + r + step) * nkb)
            ut[:, sl] = u.astype(BF16)
            yt[:, sl] = dcol_ref[...] * u

    def finish(j):
        gt = _gelu_tanh(yt[:, j * ch:(j + 1) * ch])
        gate = _dot(wglut_ref[...], gt.astype(BF16)) + bglu_ref[...]
        y5 = (gt * _sigmoid(gate)).astype(BF16)
        p_ref[...] = _dot_tn(y5, wo_ref[...]).reshape(p_ref.shape)

    spp = npt // npi
    for hh in range(npi):
        @pl.when(ph == hh)
        def _():
            for j in range(hh * spp, (hh + 1) * spp):
                project(j)

    @pl.when(ph == npi - 1)
    def _():
        gu = S5_GROUP_BATCH
        bdot = lambda a, b: lax.dot_general(a, b, (((2,), (1,)), ((0,), (0,))), preferred_element_type=F32)
        bdot_nt = lambda a, b: lax.dot_general(a, b, (((2,), (2,)), ((0,), (0,))), preferred_element_type=F32)

        def groups(i, _):
            g0 = pl.multiple_of(i * gu, gu)
            r0 = pl.multiple_of(i * (gu * S5_CH), gu * S5_CH)
            gsl = pl.ds(g0, gu)
            rows = ut[pl.ds(r0, gu * S5_CH), :].reshape(gu, S5_CH, q * nkb)
            ugt = jnp.concatenate([rows[:, :, (q - 1 - j) * nkb:(q - j) * nkb] for j in range(q)], axis=1)
            y = bdot(tt_ref[gsl], ugt)
            vt = bdot(wt_ref[gsl], ugt)
            v = jnp.swapaxes(vt, 1, 2)
            v_sw = jnp.swapaxes(jnp.concatenate([vt[:, S5_STATE:], vt[:, :S5_STATE]], axis=1), 1, 2)
            a1 = a1_ref[gsl]
            a2 = a2_ref[gsl]
            s = carry[gsl]
            s_sw = pltpu.roll(s.reshape(gu * nb, 2 * S5_STATE), S5_STATE, 1).reshape(s.shape)
            prev = []
            for k in range(nk):
                prev.append(s)
                s, s_sw = (a1 * s + a2 * s_sw + v[:, k * nb:(k + 1) * nb, :],
                           a1 * s_sw - a2 * s + v_sw[:, k * nb:(k + 1) * nb, :])
            carry[gsl] = s
            sprev = jnp.concatenate(prev, axis=1).astype(BF16)
            y = y + bdot_nt(zt_ref[gsl], sprev)
            for t in range(q):
                yt[pl.ds(r0, gu * S5_CH), t * nkb:(t + 1) * nkb] += (
                    y[:, t * S5_CH:(t + 1) * S5_CH, :].reshape(gu * S5_CH, nkb))
            return 0

        lax.fori_loop(0, ng // gu, groups, 0)
        if perm:
            sfin_ref[...] = carry[...]
        else:
            lane_s = lax.broadcasted_iota(jnp.int32, (nb, 2 * S5_STATE), 1)
            for pr in range(ng // 2):
                cols = slice(pr * 2 * S5_STATE, (pr + 1) * 2 * S5_STATE)
                c0, c1 = carry[2 * pr], carry[2 * pr + 1]
                sfin_ref[0, :, cols] = jnp.where(lane_s < S5_STATE, c0, pltpu.roll(c1, S5_STATE, 1))
                sfin_ref[1, :, cols] = jnp.where(lane_s < S5_STATE, pltpu.roll(c0, S5_STATE, 1), c1)

    for j in range(npt):
        @pl.when(ph == npi + j)
        def _():
            finish(j)


def _s5_mix(q, nb, x, ln_g, wu, tt, wt, zt, a1, a2, s0, dcol, wglut, bglu, wo):
    nseq, l, d = x.shape
    nblk = l // q
    nlt = nblk * nseq
    nkb = min(LANES, nlt)
    ntile = nlt // nkb
    tok = q * nkb
    npt = max(1, tok // S5_SLICE_TOKENS)
    dm = wo.shape[1]
    perm = q > 1
    if perm:
        half = 8
        npi = q // half
        assert nb == nseq and d % LANES == 0 and q % half == 0
        xv = x.reshape(nseq, nblk, npi, half, d // LANES, LANES).transpose(0, 1, 2, 4, 3, 5)
        assert npt % npi == 0
        x_spec = pl.BlockSpec((nseq, nkb // nseq, None, d // LANES, half, LANES),
                              lambda i, j: (0, i, jnp.minimum(j, npi - 1), 0, 0, 0))
    else:
        npi = 1
        xv = x.reshape(1, nseq, d)
        x_spec = pl.BlockSpec((1, nkb, d), lambda i, j: (0, i, 0))
    body = functools.partial(_s5_mix_body, q, nb, nkb, npi, npt, perm)
    p_spec = pl.BlockSpec((q // npt, nkb, dm), lambda i, j: (jnp.maximum(j - npi, 0), i, 0))
    return pl.pallas_call(
        body,
        grid=(ntile, npi + npt),
        in_specs=[x_spec, _const_spec(ln_g.shape), _const_spec(wu.shape), _const_spec(tt.shape),
                  _const_spec(wt.shape), _const_spec(zt.shape), _const_spec(a1.shape), _const_spec(a2.shape),
                  _const_spec(s0.shape), _const_spec(dcol.shape), _const_spec(wglut.shape),
                  _const_spec(bglu.shape), _const_spec(wo.shape)],
        out_specs=[p_spec, pl.BlockSpec(s0.shape, lambda i, j: (0, 0, 0))],
        out_shape=[jax.ShapeDtypeStruct((q, nlt, dm), F32), jax.ShapeDtypeStruct(s0.shape, F32)],
        scratch_shapes=[pltpu.VMEM((d, tok), BF16), pltpu.VMEM((d, tok), F32),
                        pltpu.VMEM((tt.shape[0], nb, 2 * S5_STATE), F32),
                        pltpu.VMEM((wu.shape[1], wu.shape[0]), BF16)],
        compiler_params=_params("arbitrary", "arbitrary"),
        name=f"s5_mix_q{q}",
    )(xv, ln_g, wu, tt, wt, zt, a1, a2, s0, dcol, wglut, bglu, wo)


def _kv_body(m_ref, wk_ref, wv_ref, k_ref, v_ref, kb_ref, vb_ref):
    tm, rows, _ = k_ref.shape
    nt = rows // X_HEADS
    m = m_ref[...].astype(BF16)
    for w_ref, o_ref, ob_ref in ((wk_ref, k_ref, kb_ref), (wv_ref, v_ref, vb_ref)):
        r = _dot(m, w_ref[...])
        ob_ref[...] = r.astype(BF16)
        flat = o_ref.reshape(tm * rows, LANES)
        for h in range(X_HEADS):
            for dt in range(nt):
                c = (h * nt + dt) * LANES
                flat[pl.ds(dt * X_HEADS + h, tm, stride=rows), :] = r[:, c:c + LANES]


def _kv_proj_rider(mem2d, wk, wv, nsteps):
    t, d = mem2d.shape
    assert t % nsteps == 0
    tm = t // nsteps
    rows = d // LANES
    row = pl.BlockSpec((tm, d), lambda i: (i, 0))
    tile = pl.BlockSpec((tm, rows, LANES), lambda i: (i, 0, 0))
    return (_kv_body, (mem2d, wk, wv), (row, _const_spec(wk.shape), _const_spec(wv.shape)),
            (tile, tile, row, row),
            (jax.ShapeDtypeStruct((t, rows, LANES), F32),) * 2 + (jax.ShapeDtypeStruct((t, d), BF16),) * 2)


def _tile_unview_kv(kv8, b, m):
    nt = kv8.shape[1] // X_HEADS
    return kv8.reshape(b, m, nt, X_HEADS, LANES).transpose(0, 1, 3, 2, 4).reshape(b, m, X_HEADS, nt * LANES)


def _attn_prompt_body(x_ref, ys_ref, p_ref, wo1_ref, g2_ref, wq_ref, k_ref, v_ref, wxo_ref, o_ref):
    d = x_ref.shape[-1]
    hd = d // X_HEADS
    x1 = x_ref[...] + _dot(ys_ref[...], wo1_ref[...]) + p_ref[...]
    hq = _rms(x1, g2_ref[...]).astype(BF16)
    qv = _dot(hq, wq_ref[...]).astype(BF16)
    kb = k_ref[...]
    vb = v_ref[...]
    outs = []
    for h in range(X_HEADS):
        sl = slice(h * hd, (h + 1) * hd)
        s = _dot_nt(qv[:, sl], kb[:, sl]) * (hd ** -0.5)
        e = jnp.exp(s - jnp.max(s, axis=-1, keepdims=True))
        p = (e / jnp.sum(e, axis=-1, keepdims=True)).astype(BF16)
        outs.append(_dot(p, vb[:, sl]))
    o = jnp.concatenate(outs, axis=1).astype(BF16)
    o_ref[...] = x1 + _dot(o, wxo_ref[...])


def _attn_prompt(x, ys, pm, wo1, g2, wq, mk, mv, wxo, make_rider):
    b, l, d = x.shape
    nm = mk.shape[1]
    tm = min(TOKEN_TILE, l)
    nl = l // tm
    row = pl.BlockSpec((None, tm, d), lambda i, j: (i, j, 0))
    kv = pl.BlockSpec((None, nm, d), lambda i, j: (i, 0, 0))
    ins = [x, ys, pm, wo1, g2, wq, mk, mv, wxo]
    specs = [row, row, row, _const_spec(wo1.shape), _const_spec(g2.shape), _const_spec(wq.shape), kv, kv,
             _const_spec(wxo.shape)]
    rider = make_rider(lambda i, j: i * nl + j, b * nl)
    body = functools.partial(_two_bodies, _attn_prompt_body, len(ins), 1, rider[0], len(rider[1]))
    return pl.pallas_call(
        body, grid=(b, nl),
        in_specs=specs + list(rider[2]),
        out_specs=[row] + list(rider[3]),
        out_shape=[jax.ShapeDtypeStruct((b, l, d), F32)] + list(rider[4]),
        compiler_params=_params("arbitrary", "arbitrary"),
        name="attn_prompt",
    )(*ins, *rider[1])


def _merge_q_body(x_ref, ys_ref, p_ref, wo1_ref, g2_ref, wq_ref, x1_ref, q_ref):
    x1 = x_ref[...] + _dot(ys_ref[...], wo1_ref[...]) + p_ref[...]
    x1_ref[...] = x1
    q_ref[...] = _dot(_rms(x1, g2_ref[...]).astype(BF16), wq_ref[...])


def _merge_q(x2d, ys, pm, wo1, g2, wq):
    t, d = x2d.shape
    return pl.pallas_call(
        _merge_q_body,
        out_shape=[jax.ShapeDtypeStruct((t, d), F32)] * 2,
        compiler_params=pltpu.CompilerParams(vmem_limit_bytes=VMEM_LIMIT),
        name="merge_q",
    )(x2d, ys, pm, wo1, g2, wq)


def _tile_view(kv):
    b, m, nh, hd = kv.shape
    nt = hd // LANES
    return kv.reshape(b, m, nh, nt, LANES).transpose(0, 1, 3, 2, 4).reshape(b, m, nt * nh, LANES)


def _tile_unview(o):
    b, rows, _ = o.shape
    nt = rows // X_HEADS
    return o.reshape(b, nt, X_HEADS, LANES).transpose(0, 2, 1, 3).reshape(b, rows * LANES)


def _attn_step_body(q_ref, k_ref, v_ref, o_ref):
    bt, nm, rows, _ = k_ref.shape
    hd = rows * LANES // X_HEADS
    for b in range(bt):
        q = q_ref[b] * (hd ** -0.5)
        r = jnp.sum(k_ref[b] * q, axis=-1, keepdims=True)
        s = r + pltpu.roll(r, X_HEADS, 1)
        e = jnp.exp(s - jnp.max(s, axis=0, keepdims=True))
        p = e / jnp.sum(e, axis=0, keepdims=True)
        o_ref[b] = jnp.sum(p * v_ref[b], axis=0)


def _attn_step_rider(q8, k8, v8, nsteps):
    nbt, rows, _ = q8.shape
    nm = k8.shape[1]
    assert rows == 2 * X_HEADS, "score assembly assumes two 128-lane tiles per head"
    assert nbt % nsteps == 0
    bt = nbt // nsteps
    qs = pl.BlockSpec((bt, rows, LANES), lambda i: (i, 0, 0))
    kv = pl.BlockSpec((bt, nm, rows, LANES), lambda i: (i, 0, 0, 0))
    return (_attn_step_body, (q8, k8, v8), (qs, kv, kv), (qs,),
            (jax.ShapeDtypeStruct((nbt, rows, LANES), F32),))


def _ffn_body(has_o, final, *refs):
    if has_o:
        x_ref, o_in_ref, wxo_ref, g3_ref, wg_ref, wu_ref, wd_ref, gf_ref, y_ref = refs
        x2 = x_ref[...] + _dot(o_in_ref[...].astype(BF16), wxo_ref[...])
    else:
        x_ref, g3_ref, wg_ref, wu_ref, wd_ref, gf_ref, y_ref = refs
        x2 = x_ref[...]
    hf = _rms(x2, g3_ref[...]).astype(BF16)
    ff = wg_ref.shape[1]
    x3 = x2
    for lo in range(0, ff, FF_CHUNK):
        hi = min(lo + FF_CHUNK, ff)
        act = (_silu(_dot(hf, wg_ref[:, lo:hi])) * _dot(hf, wu_ref[:, lo:hi])).astype(BF16)
        x3 = x3 + _dot(act, wd_ref[lo:hi, :])
    y_ref[...] = _rms(x3, gf_ref[...]) if final else x3


def _ffn(x2d, g3, wg, wu, wd, gf, final, o_in=None, wxo=None, rider=None):
    t, d = x2d.shape
    tm = min(TOKEN_TILE, t)
    row = pl.BlockSpec((tm, d), lambda i: (i, 0))
    has_o = o_in is not None
    ins = [x2d] + ([o_in, wxo] if has_o else []) + [g3, wg, wu, wd, gf]
    specs = [row] + ([row, _const_spec(wxo.shape)] if has_o else []) + [
        _const_spec(g3.shape), _const_spec(wg.shape), _const_spec(wu.shape), _const_spec(wd.shape),
        _const_spec(gf.shape)]
    body = functools.partial(_ffn_body, has_o, final)
    out_specs, out_shape = [row], [jax.ShapeDtypeStruct((t, d), F32)]
    if rider is not None:
        body = functools.partial(_two_bodies, body, len(ins), 1, rider[0], len(rider[1]))
        ins, specs = ins + list(rider[1]), specs + list(rider[2])
        out_specs, out_shape = out_specs + list(rider[3]), out_shape + list(rider[4])
    res = pl.pallas_call(
        body, grid=(t // tm,),
        in_specs=specs, out_specs=out_specs, out_shape=out_shape,
        compiler_params=_params("arbitrary"),
        name="ffn_o" if has_o else "ffn",
    )(*ins)
    return res[0] if rider is None else res


def _two_bodies(body_a, n_in_a, n_out_a, body_b, n_in_b, *refs):
    outs = refs[n_in_a + n_in_b:]
    body_a(*refs[:n_in_a], *outs[:n_out_a])
    body_b(*refs[n_in_a:n_in_a + n_in_b], *outs[n_out_a:])


def _merge_riders(ra, rb):
    body = functools.partial(_two_bodies, ra[0], len(ra[1]), len(ra[3]), rb[0], len(rb[1]))
    return (body,) + tuple(tuple(ra[k]) + tuple(rb[k]) for k in range(1, 5))


def _cast_body(*refs):
    n = len(refs) // 2
    for src, dst in zip(refs[:n], refs[n:]):
        dst[...] = src[...].astype(dst.dtype)


def _cast_rider(ws, step_of, nsteps):
    bf16_rows = 16
    ins, in_specs, out_specs, shapes = [], [], [], []
    for w in ws:
        w, lo, hi = w if isinstance(w, tuple) else (w, 0, w.shape[0])
        r, c = hi - lo, w.shape[1]
        k = next(k for k in (1, 2, 4, 8) if r % (nsteps // k) == 0 and (r // (nsteps // k)) % bf16_rows == 0)
        rp = r // (nsteps // k)
        assert lo % rp == 0
        ins.append(w)
        in_specs.append(pl.BlockSpec((rp, c), lambda *g, k=k, o=lo // rp: (o + step_of(*g) // k, 0)))
        out_specs.append(pl.BlockSpec((rp, c), lambda *g, k=k: (step_of(*g) // k, 0)))
        shapes.append(jax.ShapeDtypeStruct((r, c), BF16))
    return (_cast_body, tuple(ins), tuple(in_specs), tuple(out_specs), tuple(shapes))


def _layer_weights(i, ln1_g, w_in, conv_w, conv_b, dt_bias, a_log, d_ssd, gn_g, lam_re, lam_im, log_step,
                   b_re, b_im, c_re, c_im, d_s5, w_glu, b_glu, w_out, ln2_g, w_xq, w_xk, w_xv, w_xo, ln3_g,
                   w_gate, w_up, w_down):
    d = w_in.shape[1]
    n_heads = dt_bias.shape[1]
    dssd = n_heads * SSD_HEAD_DIM
    conv_dim = conv_w.shape[2]
    o1, o2, o3 = dssd, dssd + conv_dim, dssd + conv_dim + n_heads
    win = w_in[i]
    w = {}
    w["ln1"] = ln1_g[i].reshape(1, d)
    w["win"] = win.astype(BF16)
    w["wu_s5"] = win[:, o3:].astype(BF16)
    w["cw"] = conv_w[i]
    w["cb"] = conv_b[i].reshape(1, conv_dim)
    w["dtb"] = jnp.pad(dt_bias[i], (0, LANES - n_heads)).reshape(1, LANES)
    w["alog"] = jnp.pad(a_log[i], (0, LANES - n_heads)).reshape(1, LANES)
    w["dexp"] = jnp.repeat(d_ssd[i], SSD_HEAD_DIM).reshape(1, dssd)
    w["gn"] = gn_g[i].reshape(1, dssd)
    w["s5"] = (lam_re[i], lam_im[i], log_step[i], b_re[i], b_im[i], c_re[i], c_im[i])
    ds5 = d_s5.shape[1]
    w["dcol"] = d_s5[i].reshape(ds5, 1)
    w["wglut"] = w_glu[i].T.astype(BF16)
    w["bglu"] = b_glu[i].reshape(ds5, 1)
    w["cast_f32"] = ((w_out[i], 0, dssd), (w_out[i], dssd, w_out.shape[1]), w_xq[i], w_xo[i])
    w["ln2"] = ln2_g[i].reshape(1, d)
    w["wk"] = w_xk[i].astype(BF16)
    w["wv"] = w_xv[i].astype(BF16)
    w["ln3"] = ln3_g[i].reshape(1, d)
    w["ffn_f32"] = (w_gate[i], w_up[i], w_down[i])
    return w


def _rows_to_state(s):
    p = s.shape[-1] // 2
    st = s.transpose(1, 0, 2)
    return st[..., :p], st[..., p:]


def kernel(x_prompt, x_sample, mem_prompt, state_conv, state_ssm, state_s5_re, state_s5_im, cache_mem_k, cache_mem_v, ln1_g, w_in, conv_w, conv_b, dt_bias, a_log, d_ssd, gn_g, lam_re, lam_im, log_step, b_re, b_im, c_re, c_im, d_s5, w_glu, b_glu, w_out, ln2_g, w_xq, w_xk, w_xv, w_xo, ln3_g, w_gate, w_up, w_down, final_g):
    bp, seq, d = x_prompt.shape
    bs = x_sample.shape[0]
    depth = w_in.shape[0]
    nm = mem_prompt.shape[1]
    ng, ns = lam_re.shape[1], lam_re.shape[2]
    gf = final_g.reshape(1, d)
    hp = x_prompt
    hs = x_sample.reshape(bs, d)
    outs = {k: [] for k in ("conv_p", "ssm_p", "re_p", "im_p", "mk_p", "mv_p", "conv_s", "ssm_s", "re_s", "im_s")}
    yp = ys_out = None
    for i in range(depth):
        w = _layer_weights(i, ln1_g, w_in, conv_w, conv_b, dt_bias, a_log, d_ssd, gn_g, lam_re, lam_im, log_step,
                           b_re, b_im, c_re, c_im, d_s5, w_glu, b_glu, w_out, ln2_g, w_xq, w_xk, w_xv, w_xo, ln3_g,
                           w_gate, w_up, w_down)
        last = i == depth - 1
        attn_steps = bp * (seq // min(TOKEN_TILE, seq))
        y_ssd, conv_p, ssm_p, zs, xbcs, dts = _ssd_prompt(hp, w["ln1"], w["win"], w["cw"], w["cb"], w["dtb"],
                                                          w["alog"], w["dexp"], w["gn"], hs, bs // attn_steps)
        q = S5_Q
        prep_riders = lambda n: _merge_riders(
            _kv_proj_rider(mem_prompt.reshape(bp * nm, d), w["wk"], w["wv"], n),
            _cast_rider(w["cast_f32"], lambda i: i, n))
        (tt, wt, zt, a1, a2, a11, a21), (mk8, mv8, mk, mv, w["wo1"], w["wo2"], w["wq"], w["wxo"]) = _s5_prep(
            q, *w["s5"], prep_riders)
        mk = mk.reshape(bp, nm, d)
        mv = mv.reshape(bp, nm, d)
        s0 = jnp.zeros((ng, bp, 2 * ns), F32)
        pperm, sfin = _s5_mix(q, bp, hp, w["ln1"], w["wu_s5"],tt, wt, zt, a1, a2, s0, w["dcol"], w["wglut"],
                              w["bglu"], w["wo2"])
        pm = pperm.reshape(q, seq // q, bp, d).transpose(2, 1, 0, 3).reshape(bp, seq, d)
        re_p, im_p = _rows_to_state(sfin)
        riders = lambda step_of, n: _merge_riders(
            _ssd_step_rider(zs, xbcs, dts, state_conv[i], state_ssm[i], w["cw"], w["cb"], w["dtb"], w["alog"],
                            w["dexp"], w["gn"], step_of, n),
            _cast_rider(w["ffn_f32"], step_of, n))
        x2, ys_ssd, conv_s, ssm_s, w["wg"], w["wu"], w["wdn"] = _attn_prompt(
            hp, y_ssd, pm, w["wo1"], w["ln2"], w["wq"], mk, mv, w["wxo"], riders)
        ys_ssd = ys_ssd.reshape(bs, -1)
        tt1, wt1, zt1 = _s5_single_token_mats(q, tt, wt, zt)
        s0s = jnp.stack([state_s5_re[i].reshape(bs, ng * ns), state_s5_im[i].reshape(bs, ng * ns)])
        ps, sfin_s = _s5_mix(1, bs, hs.reshape(bs, 1, d), w["ln1"], w["wu_s5"],tt1, wt1, zt1, a11, a21, s0s,
                             w["dcol"], w["wglut"], w["bglu"], w["wo2"])
        re_s, im_s = sfin_s[0].reshape(bs, ng, ns), sfin_s[1].reshape(bs, ng, ns)
        x1s, qs = _merge_q(hs, ys_ssd, ps.reshape(bs, d), w["wo1"], w["ln2"], w["wq"])
        attn_rider = _attn_step_rider(_tile_view(qs.reshape(bs, 1, X_HEADS, d // X_HEADS))[:, 0],
                                      _tile_view(cache_mem_k[i]), _tile_view(cache_mem_v[i]),
                                      (bp * seq) // min(TOKEN_TILE, bp * seq))
        hp, o_s = _ffn(x2.reshape(bp * seq, d), w["ln3"], w["wg"], w["wu"], w["wdn"], gf, last, rider=attn_rider)
        hp = hp.reshape(bp, seq, d)
        hs = _ffn(x1s, w["ln3"], w["wg"], w["wu"], w["wdn"], gf, last, o_in=_tile_unview(o_s), wxo=w["wxo"])
        for k, v in (("conv_p", conv_p), ("ssm_p", ssm_p), ("re_p", re_p), ("im_p", im_p),
                     ("mk_p", _tile_unview_kv(mk8, bp, nm)), ("mv_p", _tile_unview_kv(mv8, bp, nm)),
                     ("conv_s", conv_s), ("ssm_s", ssm_s), ("re_s", re_s), ("im_s", im_s)):
            outs[k].append(v)
    st = lambda k: jnp.stack(outs[k])
    return (hp, hs.reshape(bs, 1, d), st("conv_p"), st("ssm_p"), st("re_p"), st("im_p"), st("mk_p"), st("mv_p"),
            st("conv_s"), st("ssm_s"), st("re_s"), st("im_s"))
```
